```python
import jax, jax.numpy as jnp
from jax import lax
import numpy as np

D_MODEL = 1024
BATCH = 8
SEQ = 2048
DEPTH = 2
DEC_BATCH = 128
DEC_SEQ = 1
PAST_LEN = 16384
PAGE_SIZE = 128

TOK_WIDTH = 3 * D_MODEL // 4
MEM_WIDTH = D_MODEL // 4
MIX_WIDTH = TOK_WIDTH + MEM_WIDTH
MEM_HEADS = 4
MEM_HEAD_DIM = MEM_WIDTH // MEM_HEADS
N_MEM = 256
RWKV_HEAD_DIM = 64
RWKV_HEADS = TOK_WIDTH // RWKV_HEAD_DIM
RWKV_DECAY_RANK = 64
RWKV_A_RANK = 64
RWKV_GATE_RANK = 128
RWKV_GN_EPS = 64e-5
GLA_HEADS = 4
GLA_K_WIDTH = TOK_WIDTH // 2
GLA_DK = GLA_K_WIDTH // GLA_HEADS
GLA_DV = TOK_WIDTH // GLA_HEADS
GLA_GATE_RANK = 16
GLA_TAU = 16.0
GLA_CHUNK = 64
IN_COLS = 3 * TOK_WIDTH + MEM_WIDTH
N_GROUPS = 4
EXPERTS_PER_GROUP = 8
N_EXPERTS = N_GROUPS * EXPERTS_PER_GROUP
TOP_K_IN_GROUP = 2
EXPERT_FF = D_MODEL // 2
MOE_BLOCK = 128
N_RWKV_LAYERS = (DEPTH + 1) // 2
N_GLA_LAYERS = DEPTH // 2
NORM_EPS = 1e-6

kernel_name = 'hybrid_rwkv7_gla_memxattn_hmoe_step'


def rms_norm(x, g):
    xf = x.astype(jnp.float32)
    y = xf * lax.rsqrt(jnp.mean(xf * xf, -1, keepdims=True) + NORM_EPS)
    return (y * g.astype(jnp.float32)).astype(x.dtype)


def rwkv7_mix(h, shift_prev, S0, w_rkv, mu, w0, w1, w2, a0, a1, a2, g1, g2, k_k, k_a, r_k, ln_g, ln_b):
    B, T, _ = h.shape
    H, N = RWKV_HEADS, RWKV_HEAD_DIM
    f32 = jnp.float32
    h_prev = jnp.concatenate([shift_prev[:, None, :].astype(h.dtype), h[:, :-1]], axis=1)
    xx = h_prev - h
    xr, xw, xk, xv, xa, xg = [h + xx * mu[j] for j in range(6)]
    r = xr @ w_rkv[:, :TOK_WIDTH]
    k = xk @ w_rkv[:, TOK_WIDTH:2 * TOK_WIDTH]
    v = xv @ w_rkv[:, 2 * TOK_WIDTH:3 * TOK_WIDTH]
    w_log = -jax.nn.softplus(-(w0 + jnp.tanh(xw @ w1) @ w2).astype(f32)) - 0.5
    decay = jnp.exp(-jnp.exp(w_log))
    a = jax.nn.sigmoid((a0 + (xa @ a1) @ a2).astype(f32))
    g = jax.nn.sigmoid(xg @ g1) @ g2
    heads = lambda t: t.reshape(B, T, H, N)
    kk = heads((k * k_k).astype(f32))
    kk = kk / jnp.maximum(jnp.sqrt(jnp.sum(kk * kk, -1, keepdims=True)), 1e-12)
    k = k.astype(f32) * (1.0 + (a - 1.0) * k_a.astype(f32))
    r4, k4, v4, a4, d4 = heads(r.astype(f32)), heads(k), heads(v.astype(f32)), heads(a), heads(decay)

    def step(S, inp):
        r_t, k_t, v_t, kk_t, b_t, d_t = inp
        sa = jnp.einsum('bhij,bhj->bhi', S, -kk_t)
        S = S * d_t[:, :, None, :] + sa[..., None] * b_t[:, :, None, :] + v_t[..., None] * k_t[:, :, None, :]
        return S, jnp.einsum('bhij,bhj->bhi', S, r_t)

    seq_major = lambda t: jnp.moveaxis(t, 1, 0)
    S_fin, y = lax.scan(step, S0.astype(f32), tuple(seq_major(t) for t in (r4, k4, v4, kk, kk * a4, d4)))
    y = jnp.moveaxis(y, 0, 1)
    mean = jnp.mean(y, -1, keepdims=True)
    var = jnp.mean(jnp.square(y - mean), -1, keepdims=True)
    gn = ((y - mean) * lax.rsqrt(var + RWKV_GN_EPS)).reshape(B, T, TOK_WIDTH) * ln_g.astype(f32) + ln_b.astype(f32)
    bonus = (jnp.sum(r4 * k4 * r_k.astype(f32), -1, keepdims=True) * v4).reshape(B, T, TOK_WIDTH)
    out = ((gn + bonus) * g.astype(f32)).astype(h.dtype)
    return out, h[:, -1], S_fin.astype(S0.dtype)


def gla_mix(h, S0, w_qkvr, a1, a2, ab, norm_g):
    B, T, _ = h.shape
    H, DK, DV = GLA_HEADS, GLA_DK, GLA_DV
    f32 = jnp.float32
    proj = h @ w_qkvr
    q = proj[..., :GLA_K_WIDTH]
    k = proj[..., GLA_K_WIDTH:2 * GLA_K_WIDTH]
    v = proj[..., 2 * GLA_K_WIDTH:2 * GLA_K_WIDTH + TOK_WIDTH]
    r = proj[..., 2 * GLA_K_WIDTH + TOK_WIDTH:]
    log_a = jax.nn.log_sigmoid(((h @ a1) @ a2 + ab).astype(f32)) / GLA_TAU
    C = min(GLA_CHUNK, T)
    n_chunks = -(-T // C)
    Tp = n_chunks * C

    def chunks(t, d):
        t = t.astype(f32).reshape(B, T, H, d)
        t = jnp.pad(t, ((0, 0), (0, Tp - T), (0, 0), (0, 0)))
        return t.reshape(B, n_chunks, C, H, d).transpose(1, 0, 3, 2, 4)

    qc = chunks(q, DK) * (DK ** -0.5)
    kc, vc, lac = chunks(k, DK), chunks(v, DV), chunks(log_a, DK)
    causal = jnp.tril(jnp.ones((C, C), bool))

    def step(S, inp):
        q_c, k_c, v_c, la_c = inp
        b = jnp.cumsum(la_c, axis=2)
        inter = jnp.einsum('bhtd,bhde->bhte', q_c * jnp.exp(b), S)
        gap = jnp.where(causal[:, :, None], b[:, :, :, None, :] - b[:, :, None, :, :], -jnp.inf)
        att = jnp.einsum('bhtd,bhsd,bhtsd->bhts', q_c, k_c, jnp.exp(gap))
        intra = jnp.einsum('bhts,bhse->bhte', att, v_c)
        b_end = b[:, :, -1]
        S = jnp.exp(b_end)[..., None] * S + jnp.einsum('bhsd,bhse->bhde', k_c * jnp.exp(b_end[:, :, None] - b), v_c)
        return S, inter + intra

    S_fin, o = lax.scan(step, S0.astype(f32), (qc, kc, vc, lac))
    o = o.transpose(1, 0, 3, 2, 4).reshape(B, Tp, H, DV)[:, :T]
    o = o * lax.rsqrt(jnp.mean(o * o, -1, keepdims=True) + NORM_EPS) * norm_g.astype(f32)
    out = (o.reshape(B, T, TOK_WIDTH) * jax.nn.silu(r.astype(f32))).astype(h.dtype)
    return out, S_fin.astype(S0.dtype)


def memory_kv(mem, g, w_kv):
    B, M, _ = mem.shape
    kv = rms_norm(mem, g) @ w_kv
    k = kv[..., :MEM_WIDTH].reshape(B, M, MEM_HEADS, MEM_HEAD_DIM)
    v = kv[..., MEM_WIDTH:].reshape(B, M, MEM_HEADS, MEM_HEAD_DIM)
    return k, v


def mem_attention(q, mem_k, mem_v):
    B, T, _ = q.shape
    q = q.reshape(B, T, MEM_HEADS, MEM_HEAD_DIM)
    s = jnp.einsum('bthd,bmhd->bhtm', q, mem_k).astype(jnp.float32) * (MEM_HEAD_DIM ** -0.5)
    p = jax.nn.softmax(s, axis=-1).astype(q.dtype)
    return jnp.einsum('bhtm,bmhd->bthd', p, mem_v).reshape(B, T, MEM_WIDTH)


def hier_moe(h, wg, bg, we, be, w_up, w_down):
    B, T, D = h.shape
    N = B * T
    K = TOP_K_IN_GROUP
    hf = h.reshape(N, D)
    group_logits = (hf @ wg + bg).astype(jnp.float32)
    group = jnp.argmax(group_logits, -1).astype(jnp.int32)
    p_group = jnp.max(jax.nn.softmax(group_logits, -1), -1, keepdims=True)
    exp_logits = (hf @ we + be).astype(jnp.float32).reshape(N, N_GROUPS, EXPERTS_PER_GROUP)
    in_group = exp_logits[jnp.arange(N), group]
    top_val, top_idx = lax.top_k(in_group, K)
    gate = p_group * jax.nn.softmax(top_val, -1)
    expert = group[:, None] * EXPERTS_PER_GROUP + top_idx.astype(jnp.int32)
    A = N * K
    flat_e = expert.reshape(A)
    flat_tok = jnp.arange(A, dtype=jnp.int32) // K
    flat_gate = gate.reshape(A)
    order = jnp.argsort(flat_e)
    e_sorted = flat_e[order]
    counts = jnp.bincount(flat_e, length=N_EXPERTS).astype(jnp.int32)
    padded = (counts + MOE_BLOCK - 1) // MOE_BLOCK * MOE_BLOCK
    padded_end = jnp.cumsum(padded)
    start = jnp.cumsum(counts) - counts
    dest = (padded_end - padded)[e_sorted] + jnp.arange(A, dtype=jnp.int32) - start[e_sorted]
    n_blocks = -(-A // MOE_BLOCK) + N_EXPERTS
    row_tok = jnp.zeros((n_blocks * MOE_BLOCK,), jnp.int32).at[dest].set(flat_tok[order])
    block_expert = jnp.minimum(jnp.searchsorted(padded_end, jnp.arange(n_blocks, dtype=jnp.int32) * MOE_BLOCK, side='right'), N_EXPERTS - 1)

    def run_block(args):
        xb, e = args
        gu = xb @ w_up[e]
        return (jax.nn.silu(gu[:, :EXPERT_FF]) * gu[:, EXPERT_FF:]) @ w_down[e]

    rows = lax.map(run_block, (hf[row_tok].reshape(n_blocks, MOE_BLOCK, D), block_expert))
    contrib = rows.reshape(n_blocks * MOE_BLOCK, D)[dest] * flat_gate[order][:, None].astype(h.dtype)
    y = jnp.zeros((N, D), h.dtype).at[flat_tok[order]].add(contrib)
    return y.reshape(B, T, D)


def run_trunk(x, mem_k, mem_v, rw_S, rw_shift, gla_S, P):
    new_rw_S, new_rw_shift, new_gla_S = [], [], []
    for i in range(DEPTH):
        h = rms_norm(x, P['norm_mix_g'][i])
        w_in = P['w_in'][i]
        j = i // 2
        if i % 2 == 0:
            tok, shift_new, S_new = rwkv7_mix(
                h, rw_shift[j], rw_S[j], w_in[:, :3 * TOK_WIDTH], P['rw_mu'][j], P['rw_w0'][j], P['rw_w1'][j],
                P['rw_w2'][j], P['rw_a0'][j], P['rw_a1'][j], P['rw_a2'][j], P['rw_g1'][j], P['rw_g2'][j],
                P['rw_k_k'][j], P['rw_k_a'][j], P['rw_r_k'][j], P['rw_ln_g'][j], P['rw_ln_b'][j])
            new_rw_S.append(S_new)
            new_rw_shift.append(shift_new)
        else:
            tok, S_new = gla_mix(h, gla_S[j], w_in[:, :3 * TOK_WIDTH], P['gla_a1'][j], P['gla_a2'][j],
                                 P['gla_ab'][j], P['gla_norm_g'][j])
            new_gla_S.append(S_new)
        q_mem = h @ w_in[:, 3 * TOK_WIDTH:]
        mixed = jnp.concatenate([tok, mem_attention(q_mem, mem_k[i], mem_v[i])], axis=-1)
        x = x + mixed @ P['w_out'][i]
        x = x + hier_moe(rms_norm(x, P['norm_ffn_g'][i]), P['router_wg'][i], P['router_bg'][i],
                         P['router_we'][i], P['router_be'][i], P['exp_w_up'][i], P['exp_w_down'][i])
    y = rms_norm(x, P['norm_final_g'])
    return y, jnp.stack(new_rw_S), jnp.stack(new_rw_shift), jnp.stack(new_gla_S)


def setup_inputs(seed: int = 0) -> dict:
    key = jax.random.key(seed)
    ks = iter(jax.random.split(key, 48))
    f32 = jnp.float32

    def nrm(shape, scale=1.0):
        return jax.random.normal(next(ks), shape, f32) * scale

    def gain(shape):
        return 1.0 + nrm(shape, 0.05)

    NA, NB = N_RWKV_LAYERS, N_GLA_LAYERS
    inv = D_MODEL ** -0.5
    return {
        'x_prompt': nrm((BATCH, SEQ, D_MODEL)),
        'x_sample': nrm((DEC_BATCH, DEC_SEQ, D_MODEL)),
        'mem_prompt': nrm((BATCH, N_MEM, D_MODEL)),
        'state_rwkv_S': nrm((NA, DEC_BATCH, RWKV_HEADS, RWKV_HEAD_DIM, RWKV_HEAD_DIM), 0.3),
        'state_rwkv_shift': nrm((NA, DEC_BATCH, D_MODEL)),
        'state_gla_S': nrm((NB, DEC_BATCH, GLA_HEADS, GLA_DK, GLA_DV), 0.3),
        'cache_mem_k': nrm((DEPTH, DEC_BATCH, N_MEM, MEM_HEADS, MEM_HEAD_DIM)),
        'cache_mem_v': nrm((DEPTH, DEC_BATCH, N_MEM, MEM_HEADS, MEM_HEAD_DIM)),
        'norm_mix_g': gain((DEPTH, D_MODEL)),
        'norm_ffn_g': gain((DEPTH, D_MODEL)),
        'norm_mem_g': gain((DEPTH, D_MODEL)),
        'norm_final_g': gain((D_MODEL,)),
        'w_in': nrm((DEPTH, D_MODEL, IN_COLS), inv),
        'w_out': nrm((DEPTH, MIX_WIDTH, D_MODEL), 0.5 * MIX_WIDTH ** -0.5),
        'w_mem_kv': nrm((DEPTH, D_MODEL, 2 * MEM_WIDTH), inv),
        'rw_mu': jax.random.uniform(next(ks), (NA, 6, D_MODEL), f32),
        'rw_w0': jnp.linspace(-6.5, -1.5, TOK_WIDTH, dtype=f32)[None, :] + nrm((NA, TOK_WIDTH), 0.1),
        'rw_w1': nrm((NA, D_MODEL, RWKV_DECAY_RANK), inv),
        'rw_w2': nrm((NA, RWKV_DECAY_RANK, TOK_WIDTH), 0.5 * RWKV_DECAY_RANK ** -0.5),
        'rw_a0': nrm((NA, TOK_WIDTH), 0.1),
        'rw_a1': nrm((NA, D_MODEL, RWKV_A_RANK), inv),
        'rw_a2': nrm((NA, RWKV_A_RANK, TOK_WIDTH), 0.5 * RWKV_A_RANK ** -0.5),
        'rw_g1': nrm((NA, D_MODEL, RWKV_GATE_RANK), inv),
        'rw_g2': nrm((NA, RWKV_GATE_RANK, TOK_WIDTH), RWKV_GATE_RANK ** -0.5),
        'rw_k_k': 0.85 + nrm((NA, TOK_WIDTH), 0.05),
        'rw_k_a': gain((NA, TOK_WIDTH)),
        'rw_r_k': nrm((NA, RWKV_HEADS, RWKV_HEAD_DIM), 0.1),
        'rw_ln_g': gain((NA, TOK_WIDTH)),
        'rw_ln_b': nrm((NA, TOK_WIDTH), 0.02),
        'gla_a1': nrm((NB, D_MODEL, GLA_GATE_RANK), inv),
        'gla_a2': nrm((NB, GLA_GATE_RANK, GLA_K_WIDTH), GLA_GATE_RANK ** -0.5),
        'gla_ab': nrm((NB, GLA_K_WIDTH), 0.1),
        'gla_norm_g': gain((NB, GLA_DV)),
        'router_wg': nrm((DEPTH, D_MODEL, N_GROUPS), inv),
        'router_bg': nrm((DEPTH, N_GROUPS), 0.01),
        'router_we': nrm((DEPTH, D_MODEL, N_EXPERTS), inv),
        'router_be': nrm((DEPTH, N_EXPERTS), 0.01),
        'exp_w_up': nrm((DEPTH, N_EXPERTS, D_MODEL, 2 * EXPERT_FF), inv),
        'exp_w_down': nrm((DEPTH, N_EXPERTS, EXPERT_FF, D_MODEL), EXPERT_FF ** -0.5),
    }


def reference(x_prompt, x_sample, mem_prompt, state_rwkv_S, state_rwkv_shift, state_gla_S, cache_mem_k, cache_mem_v,
              norm_mix_g, norm_ffn_g, norm_mem_g, norm_final_g, w_in, w_out, w_mem_kv,
              rw_mu, rw_w0, rw_w1, rw_w2, rw_a0, rw_a1, rw_a2, rw_g1, rw_g2, rw_k_k, rw_k_a, rw_r_k, rw_ln_g, rw_ln_b,
              gla_a1, gla_a2, gla_ab, gla_norm_g, router_wg, router_bg, router_we, router_be, exp_w_up, exp_w_down):
    P = {
        'norm_mix_g': norm_mix_g, 'norm_ffn_g': norm_ffn_g, 'norm_final_g': norm_final_g,
        'w_in': w_in, 'w_out': w_out,
        'rw_mu': rw_mu, 'rw_w0': rw_w0, 'rw_w1': rw_w1, 'rw_w2': rw_w2, 'rw_a0': rw_a0, 'rw_a1': rw_a1,
        'rw_a2': rw_a2, 'rw_g1': rw_g1, 'rw_g2': rw_g2, 'rw_k_k': rw_k_k, 'rw_k_a': rw_k_a, 'rw_r_k': rw_r_k,
        'rw_ln_g': rw_ln_g, 'rw_ln_b': rw_ln_b,
        'gla_a1': gla_a1, 'gla_a2': gla_a2, 'gla_ab': gla_ab, 'gla_norm_g': gla_norm_g,
        'router_wg': router_wg, 'router_bg': router_bg, 'router_we': router_we, 'router_be': router_be,
        'exp_w_up': exp_w_up, 'exp_w_down': exp_w_down,
    }
    mem_kv = [memory_kv(mem_prompt, norm_mem_g[i], w_mem_kv[i]) for i in range(DEPTH)]
    prompt_mem_k = jnp.stack([kv[0] for kv in mem_kv])
    prompt_mem_v = jnp.stack([kv[1] for kv in mem_kv])
    B = x_prompt.shape[0]
    dt = x_prompt.dtype
    rw_S0 = jnp.zeros((N_RWKV_LAYERS, B, RWKV_HEADS, RWKV_HEAD_DIM, RWKV_HEAD_DIM), dt)
    rw_shift0 = jnp.zeros((N_RWKV_LAYERS, B, D_MODEL), dt)
    gla_S0 = jnp.zeros((N_GLA_LAYERS, B, GLA_HEADS, GLA_DK, GLA_DV), dt)
    y_prompt, prompt_rwkv_S, prompt_rwkv_shift, prompt_gla_S = run_trunk(
        x_prompt, prompt_mem_k, prompt_mem_v, rw_S0, rw_shift0, gla_S0, P)
    y_sample, sample_rwkv_S, sample_rwkv_shift, sample_gla_S = run_trunk(
        x_sample, cache_mem_k, cache_mem_v, state_rwkv_S, state_rwkv_shift, state_gla_S, P)
    return (y_prompt, y_sample, prompt_rwkv_S, prompt_rwkv_shift, prompt_gla_S, prompt_mem_k, prompt_mem_v,
            sample_rwkv_S, sample_rwkv_shift, sample_gla_S)
```

```python
import functools

import jax
import jax.numpy as jnp
from jax import lax
from jax.experimental import pallas as pl
from jax.experimental.pallas import tpu as pltpu

F32 = jnp.float32
BF16 = jnp.bfloat16
HIGHEST = lax.Precision.HIGHEST

D_MODEL = 1024
TOK_WIDTH = 768
MEM_WIDTH = 256
MEM_HEADS = 4
MEM_HEAD_DIM = 64
N_MEM = 256
RWKV_HEADS = 12
RWKV_N = 64
RWKV_GN_EPS = 64e-5
GLA_HEADS = 4
GLA_KW = 384
GLA_DK = 96
GLA_DV = 192
GLA_TAU = 16.0
GLA_CHUNK = 64
N_GROUPS = 4
EXPERTS_PER_GROUP = 8
N_EXPERTS = 32
EXPERT_FF = 512
NORM_EPS = 1e-6
ROUTER_LANES = 128

TOKEN_BLOCK = 384
MOE_BLOCK = 256
SCAN_TIME_BLOCK = 32
VMEM_LIMIT = 56 * 1024 * 1024


def _cparams(*sem):
    return pltpu.CompilerParams(dimension_semantics=sem, vmem_limit_bytes=VMEM_LIMIT)


def _dot(a, b):
    return jnp.dot(a, b, preferred_element_type=F32)


def _rms(x, g):
    return x * lax.rsqrt(jnp.mean(x * x, axis=-1, keepdims=True) + NORM_EPS) * g


def _sigmoid(x):
    return 1.0 / (1.0 + jnp.exp(-x))


def _row_spec(tm, n):
    return pl.BlockSpec((tm, n), lambda i: (i, 0))


def _full_spec(shape):
    nd = len(shape)
    return pl.BlockSpec(shape, lambda *_: (0,) * nd)


def _norm_kernel(x_ref, g_ref, o_ref):
    o_ref[...] = _rms(x_ref[...], g_ref[...])


def _norm(x, g, tm):
    m, d = x.shape
    return pl.pallas_call(
        _norm_kernel, grid=(m // tm,),
        in_specs=[_row_spec(tm, d), _full_spec((1, d))],
        out_specs=_row_spec(tm, d),
        out_shape=jax.ShapeDtypeStruct((m, d), F32),
        compiler_params=_cparams("parallel"), name="rms_norm")(x, g.reshape(1, d))


def _norm_matmul_kernel(x_ref, g_ref, w_ref, o_ref):
    o_ref[...] = _dot(_rms(x_ref[...], g_ref[...]).astype(BF16), w_ref[...])


def _norm_matmul(x, g, w, tm):
    m, d = x.shape
    n = w.shape[1]
    return pl.pallas_call(
        _norm_matmul_kernel, grid=(m // tm,),
        in_specs=[_row_spec(tm, d), _full_spec((1, d)), _full_spec((d, n))],
        out_specs=_row_spec(tm, n),
        out_shape=jax.ShapeDtypeStruct((m, n), F32),
        compiler_params=_cparams("parallel"), name="norm_matmul")(x, g.reshape(1, d), w.astype(BF16))


def _rwkv_proj_kernel(h_ref, hp_ref, mu_ref, wr_ref, wk_ref, wv_ref, wq_ref, w1_ref, w2_ref, w0_ref,
                      a1_ref, a2_ref, a0_ref, g1_ref, g2_ref,
                      r_out, k_out, v_out, d_out, a_out, g_out, q_out):
    h = h_ref[...]
    xx = hp_ref[...] - h

    def mix(j):
        return (h + xx * mu_ref[j:j + 1, :]).astype(BF16)

    r_out[...] = _dot(mix(0), wr_ref[...])
    wl = w0_ref[...] + _dot(jnp.tanh(_dot(mix(1), w1_ref[...])).astype(BF16), w2_ref[...])
    k_out[...] = _dot(mix(2), wk_ref[...])
    v_out[...] = _dot(mix(3), wv_ref[...])
    al = a0_ref[...] + _dot(_dot(mix(4), a1_ref[...]).astype(BF16), a2_ref[...])
    g_out[...] = _dot(_sigmoid(_dot(mix(5), g1_ref[...])).astype(BF16), g2_ref[...])
    q_out[...] = _dot(h.astype(BF16), wq_ref[...])
    z = -wl
    softplus = jnp.maximum(z, 0.0) + jnp.log(1.0 + jnp.exp(-jnp.abs(z)))
    d_out[...] = jnp.exp(-jnp.exp(-softplus - 0.5))
    a_out[...] = _sigmoid(al)


def _rwkv_proj(h, hp, w_in, mu, w0, w1, w2, a0, a1, a2, g1, g2):
    m = h.shape[0]
    tm = TOKEN_BLOCK
    tw = TOK_WIDTH
    bf = lambda t: t.astype(BF16)
    ws = [mu, bf(w_in[:, :tw]), bf(w_in[:, tw:2 * tw]), bf(w_in[:, 2 * tw:3 * tw]), bf(w_in[:, 3 * tw:]),
          bf(w1), bf(w2), w0.reshape(1, tw), bf(a1), bf(a2), a0.reshape(1, tw), bf(g1), bf(g2)]
    outs = [jax.ShapeDtypeStruct((m, tw), F32)] * 6 + [jax.ShapeDtypeStruct((m, MEM_WIDTH), F32)]
    return pl.pallas_call(
        _rwkv_proj_kernel, grid=(m // tm,),
        in_specs=[_row_spec(tm, D_MODEL), _row_spec(tm, D_MODEL)] + [_full_spec(w.shape) for w in ws],
        out_specs=[_row_spec(tm, tw)] * 6 + [_row_spec(tm, MEM_WIDTH)],
        out_shape=outs,
        compiler_params=_cparams("parallel"), name="rwkv_proj")(h, hp, *ws)


def _rwkv_scan_kernel(r_ref, k_ref, v_ref, d_ref, a_ref, kkp_ref, kap_ref, rkp_ref, lng_ref, lnb_ref, s0_ref,
                      y_ref, sfin_ref, s_scr, yrow_scr, *, tc):
    n = RWKV_N

    @pl.when(pl.program_id(1) == 0)
    def _():
        s_scr[...] = s0_ref[...]

    def step(t, carry):
        r_t = r_ref[t]
        k_t = k_ref[t]
        v_t = v_ref[t]
        d_t = d_ref[t]
        a_t = a_ref[t]
        kkr = k_t * kkp_ref[...]
        nrm = jnp.maximum(jnp.sqrt(jnp.sum(kkr * kkr, axis=0, keepdims=True)), 1e-12)
        kk = kkr * (1.0 / nrm)
        k2 = k_t * (1.0 + (a_t - 1.0) * kap_ref[...])
        nkk = -kk
        b_t = kk * a_t

        def ibody(i, c):
            s_i = s_scr[i]
            sa = jnp.sum(s_i * nkk, axis=0, keepdims=True)
            v_i = v_ref[t, pl.ds(i, 1), :]
            s_n = s_i * d_t + sa * b_t + v_i * k2
            s_scr[i] = s_n
            yrow_scr[pl.ds(i, 1), :] = jnp.sum(s_n * r_t, axis=0, keepdims=True)
            return c

        lax.fori_loop(0, n, ibody, 0, unroll=8)
        y = yrow_scr[...]
        yc = y - jnp.mean(y, axis=0, keepdims=True)
        var = jnp.mean(yc * yc, axis=0, keepdims=True)
        gn = yc * lax.rsqrt(var + RWKV_GN_EPS) * lng_ref[...] + lnb_ref[...]
        bonus = jnp.sum(r_t * k2 * rkp_ref[...], axis=0, keepdims=True) * v_t
        y_ref[t] = gn + bonus
        return carry

    lax.fori_loop(0, tc, step, 0)

    @pl.when(pl.program_id(1) == pl.num_programs(1) - 1)
    def _():
        sfin_ref[...] = s_scr[...]


def _rwkv_scan(r, k, v, d, a, params, s0, lane_block, tc):
    t, n, l = r.shape
    seq = pl.BlockSpec((tc, n, lane_block), lambda li, ti: (ti, 0, li))
    par = pl.BlockSpec((n, lane_block), lambda li, ti: (0, li))
    st = pl.BlockSpec((n, n, lane_block), lambda li, ti: (0, 0, li))
    return pl.pallas_call(
        functools.partial(_rwkv_scan_kernel, tc=tc), grid=(l // lane_block, t // tc),
        in_specs=[seq] * 5 + [par] * 5 + [st],
        out_specs=[seq, st],
        out_shape=[jax.ShapeDtypeStruct((t, n, l), F32), jax.ShapeDtypeStruct((n, n, l), F32)],
        scratch_shapes=[pltpu.VMEM((n, n, lane_block), F32), pltpu.VMEM((n, lane_block), F32)],
        compiler_params=_cparams("parallel", "arbitrary"), name="rwkv_scan")(r, k, v, d, a, *params, s0)


def _gla_proj_kernel(h_ref, w_ref, a1_ref, a2_ref, ab_ref, q_out, k_out, v_out, r_out, la_out, qm_out):
    hb = h_ref[...].astype(BF16)
    kw, tw = GLA_KW, TOK_WIDTH
    q_out[...] = _dot(hb, w_ref[:, :kw])
    k_out[...] = _dot(hb, w_ref[:, kw:2 * kw])
    v_out[...] = _dot(hb, w_ref[:, 2 * kw:2 * kw + tw])
    r_out[...] = _dot(hb, w_ref[:, 2 * kw + tw:3 * tw])
    qm_out[...] = _dot(hb, w_ref[:, 3 * tw:])
    x = _dot(_dot(hb, a1_ref[...]).astype(BF16), a2_ref[...]) + ab_ref[...]
    log_sigmoid = jnp.minimum(x, 0.0) - jnp.log(1.0 + jnp.exp(-jnp.abs(x)))
    la_out[...] = log_sigmoid / GLA_TAU


def _gla_proj(h, w_in, a1, a2, ab):
    m = h.shape[0]
    tm = TOKEN_BLOCK
    ws = [w_in.astype(BF16), a1.astype(BF16), a2.astype(BF16), ab.reshape(1, GLA_KW)]
    widths = [GLA_KW, GLA_KW, TOK_WIDTH, TOK_WIDTH, GLA_KW, MEM_WIDTH]
    return pl.pallas_call(
        _gla_proj_kernel, grid=(m // tm,),
        in_specs=[_row_spec(tm, D_MODEL)] + [_full_spec(w.shape) for w in ws],
        out_specs=[_row_spec(tm, w) for w in widths],
        out_shape=[jax.ShapeDtypeStruct((m, w), F32) for w in widths],
        compiler_params=_cparams("parallel"), name="gla_proj")(h, *ws)


def _gla_out_norm(o, g):
    return o * lax.rsqrt(jnp.mean(o * o, axis=-1, keepdims=True) + NORM_EPS) * g


def _gla_chunk_kernel(q_ref, k_ref, v_ref, la_ref, ng_ref, s0_ref, o_ref, sfin_ref, s_scr, b_scr):
    c, dk, dv = GLA_CHUNK, GLA_DK, GLA_DV

    @pl.when(pl.program_id(1) == 0)
    def _():
        s_scr[...] = s0_ref[0]

    la = la_ref[0]
    row = lax.broadcasted_iota(jnp.int32, (c, c), 0)
    col = lax.broadcasted_iota(jnp.int32, (c, c), 1)
    tril = (row >= col).astype(F32)
    b = jnp.dot(tril, la, precision=HIGHEST, preferred_element_type=F32)
    b_scr[...] = b
    b_end = b[c - 1:c, :]
    b_end_rows = lax.dot_general(la, jnp.ones((c, dv), F32), (((0,), (0,)), ((), ())),
                                 precision=HIGHEST, preferred_element_type=F32)
    s_old = s_scr[...]
    k_c = k_ref[0]
    v_c = v_ref[0]
    qs = q_ref[0] * (dk ** -0.5)
    inter = _dot((qs * jnp.exp(b)).astype(BF16), s_old.astype(BF16))
    rows = lax.broadcasted_iota(jnp.int32, (c, dk), 0)

    def sbody(s, acc):
        b_s = b_scr[pl.ds(s, 1), :]
        k_s = k_ref[0, pl.ds(s, 1), :]
        v_s = v_ref[0, pl.ds(s, 1), :]
        e = jnp.exp(jnp.where(rows >= s, b - b_s, -jnp.inf))
        att_col = jnp.sum(qs * k_s * e, axis=1, keepdims=True)
        return acc + att_col * v_s

    intra = lax.fori_loop(0, c, sbody, jnp.zeros((c, dv), F32))
    o_ref[0] = _gla_out_norm(inter + intra, ng_ref[...])
    kd = (k_c * jnp.exp(b_end - b)).astype(BF16)
    s_new = jnp.exp(b_end_rows) * s_old + lax.dot_general(
        kd, v_c.astype(BF16), (((0,), (0,)), ((), ())), preferred_element_type=F32)
    s_scr[...] = s_new

    @pl.when(pl.program_id(1) == pl.num_programs(1) - 1)
    def _():
        sfin_ref[0] = s_new


def _gla_chunk_scan(q, k, v, la, norm_g, s0):
    bh, t, dk = q.shape
    dv = v.shape[2]
    c = GLA_CHUNK
    kspec = pl.BlockSpec((1, c, dk), lambda i, j: (i, j, 0))
    vspec = pl.BlockSpec((1, c, dv), lambda i, j: (i, j, 0))
    sspec = pl.BlockSpec((1, dk, dv), lambda i, j: (i, 0, 0))
    return pl.pallas_call(
        _gla_chunk_kernel, grid=(bh, t // c),
        in_specs=[kspec, kspec, vspec, kspec, pl.BlockSpec((1, dv), lambda i, j: (0, 0)), sspec],
        out_specs=[vspec, sspec],
        out_shape=[jax.ShapeDtypeStruct((bh, t, dv), F32), jax.ShapeDtypeStruct((bh, dk, dv), F32)],
        scratch_shapes=[pltpu.VMEM((dk, dv), F32), pltpu.VMEM((c, dk), F32)],
        compiler_params=_cparams("parallel", "arbitrary"), name="gla_chunk")(q, k, v, la, norm_g.reshape(1, dv), s0)


GLA_STEP_GROUP = 8


def _gla_step_kernel(q_ref, k_ref, v_ref, la_ref, ng_ref, s0_ref, o_ref, s_ref):
    for g in range(GLA_STEP_GROUP):
        s_new = jnp.exp(la_ref[g]) * s0_ref[g] + k_ref[g] * v_ref[g]
        s_ref[g] = s_new
        o = jnp.sum((q_ref[g] * (GLA_DK ** -0.5)) * s_new, axis=0, keepdims=True)
        o_ref[g] = _gla_out_norm(o, ng_ref[...])


def _gla_step(q, k, v, la, norm_g, s0):
    bh, dk, _ = q.shape
    dv = v.shape[2]
    g = GLA_STEP_GROUP
    cspec = pl.BlockSpec((g, dk, 1), lambda i: (i, 0, 0))
    vspec = pl.BlockSpec((g, 1, dv), lambda i: (i, 0, 0))
    sspec = pl.BlockSpec((g, dk, dv), lambda i: (i, 0, 0))
    return pl.pallas_call(
        _gla_step_kernel, grid=(bh // g,),
        in_specs=[cspec, cspec, vspec, cspec, _full_spec((1, dv)), sspec],
        out_specs=[vspec, sspec],
        out_shape=[jax.ShapeDtypeStruct((bh, 1, dv), F32), jax.ShapeDtypeStruct((bh, dk, dv), F32)],
        compiler_params=_cparams("parallel"), name="gla_step")(q, k, v, la, norm_g.reshape(1, dv), s0)


def _mem_attn_kernel(q_ref, k_ref, v_ref, o_ref):
    q = q_ref[0]
    k = k_ref[0].astype(BF16)
    v = v_ref[0].astype(BF16)
    head_of_lane = lax.broadcasted_iota(jnp.int32, (1, MEM_WIDTH), 1) // MEM_HEAD_DIM
    out = jnp.zeros(q.shape, F32)
    for h in range(MEM_HEADS):
        mine = head_of_lane == h
        qh = jnp.where(mine, q, 0.0).astype(BF16)
        s = lax.dot_general(qh, k, (((1,), (1,)), ((), ())), preferred_element_type=F32) * (MEM_HEAD_DIM ** -0.5)
        p = jnp.exp(s - jnp.max(s, axis=-1, keepdims=True))
        p = p / jnp.sum(p, axis=-1, keepdims=True)
        out = out + jnp.where(mine, _dot(p.astype(BF16), v), 0.0)
    o_ref[0] = out


def _mem_attn(q, mem_k, mem_v, tq):
    b, t, w = q.shape
    qspec = pl.BlockSpec((1, tq, w), lambda i, j: (i, j, 0))
    mspec = pl.BlockSpec((1, N_MEM, w), lambda i, j: (i, 0, 0))
    return pl.pallas_call(
        _mem_attn_kernel, grid=(b, t // tq),
        in_specs=[qspec, mspec, mspec], out_specs=qspec,
        out_shape=jax.ShapeDtypeStruct((b, t, w), F32),
        compiler_params=_cparams("parallel", "parallel"), name="mem_attn")(q, mem_k, mem_v)


def _out_proj_kernel(tok_ref, gate_ref, att_ref, x_ref, wo_ref, g_ref, wr_ref, br_ref,
                     x1_out, h2_out, logit_out, *, silu_gate):
    gate = gate_ref[...]
    if silu_gate:
        gate = gate * _sigmoid(gate)
    mixed = (tok_ref[...] * gate).astype(BF16)
    x1 = (x_ref[...] + _dot(mixed, wo_ref[:TOK_WIDTH, :]) + _dot(att_ref[...].astype(BF16), wo_ref[TOK_WIDTH:, :]))
    x1_out[...] = x1
    h2 = _rms(x1, g_ref[...])
    h2_out[...] = h2.astype(BF16)
    logit_out[...] = jnp.dot(h2, wr_ref[...], precision=HIGHEST, preferred_element_type=F32) + br_ref[...]


def _out_proj(tok, gate, att, x, w_out, norm_g, w_router, b_router, silu_gate):
    m = x.shape[0]
    tm = TOKEN_BLOCK
    ws = [w_out.astype(BF16), norm_g.reshape(1, D_MODEL), w_router, b_router]
    return pl.pallas_call(
        functools.partial(_out_proj_kernel, silu_gate=silu_gate), grid=(m // tm,),
        in_specs=[_row_spec(tm, TOK_WIDTH), _row_spec(tm, TOK_WIDTH), _row_spec(tm, MEM_WIDTH), _row_spec(tm, D_MODEL)]
        + [_full_spec(w.shape) for w in ws],
        out_specs=[_row_spec(tm, D_MODEL), _row_spec(tm, D_MODEL), _row_spec(tm, ROUTER_LANES)],
        out_shape=[jax.ShapeDtypeStruct((m, D_MODEL), F32), jax.ShapeDtypeStruct((m, D_MODEL), BF16),
                   jax.ShapeDtypeStruct((m, ROUTER_LANES), F32)],
        compiler_params=_cparams("parallel"), name="out_proj")(tok, gate, att, x, *ws)


def _moe_kernel(be_ref, nv_ref, x_ref, gate_ref, wu_ref, wd_ref, o_ref):
    i = pl.program_id(0)

    @pl.when(i < nv_ref[0])
    def _():
        gu = _dot(x_ref[...], wu_ref[0])
        g = gu[:, :EXPERT_FF]
        act = (g * _sigmoid(g) * gu[:, EXPERT_FF:]).astype(BF16)
        o_ref[...] = _dot(act, wd_ref[0]) * gate_ref[...]

    @pl.when(i >= nv_ref[0])
    def _():
        o_ref[...] = jnp.zeros(o_ref.shape, F32)


def _moe_ffn(xs, row_gate, block_expert, n_valid, w_up, w_down):
    r = xs.shape[0]
    tm = MOE_BLOCK
    grid_spec = pltpu.PrefetchScalarGridSpec(
        num_scalar_prefetch=2, grid=(r // tm,),
        in_specs=[pl.BlockSpec((tm, D_MODEL), lambda i, be, nv: (i, 0)),
                  pl.BlockSpec((tm, 1), lambda i, be, nv: (i, 0)),
                  pl.BlockSpec((1, D_MODEL, 2 * EXPERT_FF), lambda i, be, nv: (be[i], 0, 0)),
                  pl.BlockSpec((1, EXPERT_FF, D_MODEL), lambda i, be, nv: (be[i], 0, 0))],
        out_specs=pl.BlockSpec((tm, D_MODEL), lambda i, be, nv: (i, 0)))
    return pl.pallas_call(
        _moe_kernel, grid_spec=grid_spec,
        out_shape=jax.ShapeDtypeStruct((r, D_MODEL), F32),
        compiler_params=_cparams("arbitrary"), name="moe_ffn")(block_expert, n_valid, xs, row_gate, w_up, w_down)


def _route(logits, tm):
    m = logits.shape[0]
    a = 2 * m
    gl = logits[:, :N_GROUPS]
    el = logits[:, N_GROUPS:N_GROUPS + N_EXPERTS].reshape(m, N_GROUPS, EXPERTS_PER_GROUP)
    group = jnp.argmax(gl, -1).astype(jnp.int32)
    p_group = jnp.max(jax.nn.softmax(gl, -1), -1, keepdims=True)
    in_group = jnp.take_along_axis(el, group[:, None, None], axis=1)[:, 0]
    top_val, top_idx = lax.top_k(in_group, 2)
    gate = p_group * jax.nn.softmax(top_val, -1)
    expert = group[:, None] * EXPERTS_PER_GROUP + top_idx.astype(jnp.int32)
    flat_e = expert.reshape(a)
    order = jnp.argsort(flat_e)
    e_sorted = flat_e[order]
    counts = jnp.bincount(flat_e, length=N_EXPERTS).astype(jnp.int32)
    padded = (counts + tm - 1) // tm * tm
    padded_end = jnp.cumsum(padded)
    start = jnp.cumsum(counts) - counts
    dest = (padded_end - padded)[e_sorted] + jnp.arange(a, dtype=jnp.int32) - start[e_sorted]
    n_blocks = -(-a // tm) + N_EXPERTS
    n_rows = n_blocks * tm
    row_tok = jnp.zeros((n_rows,), jnp.int32).at[dest].set((order // 2).astype(jnp.int32))
    row_gate = jnp.zeros((n_rows,), F32).at[dest].set(gate.reshape(a)[order])
    pos = jnp.zeros((a,), jnp.int32).at[order].set(dest).reshape(m, 2)
    block_expert = jnp.minimum(
        jnp.searchsorted(padded_end, jnp.arange(n_blocks, dtype=jnp.int32) * tm, side='right'),
        N_EXPERTS - 1).astype(jnp.int32)
    n_valid = (padded_end[-1:] // tm).astype(jnp.int32)
    return row_tok, row_gate.reshape(n_rows, 1), pos, block_expert, n_valid


def _combine_kernel(x_ref, r0_ref, r1_ref, g_ref, x_out, h_out):
    x2 = x_ref[...] + (r0_ref[...] + r1_ref[...])
    x_out[...] = x2
    h_out[...] = _rms(x2, g_ref[...])


def _combine(x1, r0, r1, g):
    m = x1.shape[0]
    tm = TOKEN_BLOCK
    spec = _row_spec(tm, D_MODEL)
    return pl.pallas_call(
        _combine_kernel, grid=(m // tm,),
        in_specs=[spec, spec, spec, _full_spec((1, D_MODEL))],
        out_specs=[spec, spec],
        out_shape=[jax.ShapeDtypeStruct((m, D_MODEL), F32)] * 2,
        compiler_params=_cparams("parallel"), name="moe_combine")(x1, r0, r1, g.reshape(1, D_MODEL))


def _to_scan_layout(t2d, b, t):
    return t2d.reshape(b, t, RWKV_HEADS, RWKV_N).transpose(1, 3, 0, 2).reshape(t, RWKV_N, b * RWKV_HEADS)


def _from_scan_layout(y, b, t):
    return y.reshape(t, RWKV_N, b, RWKV_HEADS).transpose(2, 0, 3, 1).reshape(b * t, TOK_WIDTH)


def _scan_param(p, b):
    return jnp.tile(p.reshape(RWKV_HEADS, RWKV_N).T, (1, b))


def _state_to_scan(s):
    b = s.shape[0]
    return s.transpose(2, 3, 0, 1).reshape(RWKV_N, RWKV_N, b * RWKV_HEADS)


def _state_from_scan(s, b):
    return s.reshape(RWKV_N, RWKV_N, b, RWKV_HEADS).transpose(2, 3, 0, 1)


def _heads_major(t2d, b, t, d):
    return t2d.reshape(b, t, GLA_HEADS, d).transpose(0, 2, 1, 3).reshape(b * GLA_HEADS, t, d)


def kernel(x_prompt, x_sample, mem_prompt, state_rwkv_S, state_rwkv_shift, state_gla_S, cache_mem_k, cache_mem_v, norm_mix_g, norm_ffn_g, norm_mem_g, norm_final_g, w_in, w_out, w_mem_kv, rw_mu, rw_w0, rw_w1, rw_w2, rw_a0, rw_a1, rw_a2, rw_g1, rw_g2, rw_k_k, rw_k_a, rw_r_k, rw_ln_g, rw_ln_b, gla_a1, gla_a2, gla_ab, gla_norm_g, router_wg, router_bg, router_we, router_be, exp_w_up, exp_w_down):
    bp, tp, _ = x_prompt.shape
    bs, ts, _ = x_sample.shape
    assert ts == 1 and tp % GLA_CHUNK == 0 and tp % SCAN_TIME_BLOCK == 0
    np_ = bp * tp
    ns = bs * ts
    m = np_ + ns
    assert m % TOKEN_BLOCK == 0
    depth = w_in.shape[0]

    mem2d = mem_prompt.reshape(bp * N_MEM, D_MODEL)
    mem_kv = [_norm_matmul(mem2d, norm_mem_g[i], w_mem_kv[i], 512) for i in range(depth)]
    pk = [kv[:, :MEM_WIDTH].reshape(bp, N_MEM, MEM_WIDTH) for kv in mem_kv]
    pv = [kv[:, MEM_WIDTH:].reshape(bp, N_MEM, MEM_WIDTH) for kv in mem_kv]
    prompt_mem_k = jnp.stack(pk).reshape(depth, bp, N_MEM, MEM_HEADS, MEM_HEAD_DIM)
    prompt_mem_v = jnp.stack(pv).reshape(depth, bp, N_MEM, MEM_HEADS, MEM_HEAD_DIM)
    sk = cache_mem_k.reshape(depth, bs, N_MEM, MEM_WIDTH)
    sv = cache_mem_v.reshape(depth, bs, N_MEM, MEM_WIDTH)

    x = jnp.concatenate([x_prompt.reshape(np_, D_MODEL), x_sample.reshape(ns, D_MODEL)], axis=0)
    h = _norm(x, norm_mix_g[0], TOKEN_BLOCK)

    w_router = jnp.zeros((depth, D_MODEL, ROUTER_LANES), F32)
    w_router = w_router.at[:, :, :N_GROUPS].set(router_wg).at[:, :, N_GROUPS:N_GROUPS + N_EXPERTS].set(router_we)
    b_router = jnp.zeros((depth, 1, ROUTER_LANES), F32)
    b_router = b_router.at[:, 0, :N_GROUPS].set(router_bg).at[:, 0, N_GROUPS:N_GROUPS + N_EXPERTS].set(router_be)
    w_up_bf = exp_w_up.astype(BF16)
    w_down_bf = exp_w_down.astype(BF16)

    p_rw_S, p_rw_shift, p_gla_S, s_rw_S, s_rw_shift, s_gla_S = [], [], [], [], [], []
    for i in range(depth):
        j = i // 2
        if i % 2 == 0:
            hp3 = h[:np_].reshape(bp, tp, D_MODEL)
            h_prev = jnp.concatenate([
                jnp.concatenate([jnp.zeros((bp, 1, D_MODEL), F32), hp3[:, :-1]], axis=1).reshape(np_, D_MODEL),
                state_rwkv_shift[j]], axis=0)
            r, k, v, d, a, gate, qm = _rwkv_proj(h, h_prev, w_in[i], rw_mu[j], rw_w0[j], rw_w1[j], rw_w2[j],
                                                 rw_a0[j], rw_a1[j], rw_a2[j], rw_g1[j], rw_g2[j])
            pvec = [rw_k_k[j], rw_k_a[j], rw_r_k[j], rw_ln_g[j], rw_ln_b[j]]
            yp, sp = _rwkv_scan(*[_to_scan_layout(t_[:np_], bp, tp) for t_ in (r, k, v, d, a)],
                                [_scan_param(p, bp) for p in pvec],
                                jnp.zeros((RWKV_N, RWKV_N, bp * RWKV_HEADS), F32), bp * RWKV_HEADS, SCAN_TIME_BLOCK)
            ys, ss = _rwkv_scan(*[_to_scan_layout(t_[np_:], bs, ts) for t_ in (r, k, v, d, a)],
                                [_scan_param(p, bs) for p in pvec],
                                _state_to_scan(state_rwkv_S[j]), 128, 1)
            tok = jnp.concatenate([_from_scan_layout(yp, bp, tp), _from_scan_layout(ys, bs, ts)], axis=0)
            p_rw_S.append(_state_from_scan(sp, bp))
            s_rw_S.append(_state_from_scan(ss, bs))
            p_rw_shift.append(hp3[:, -1])
            s_rw_shift.append(h[np_:])
            silu_gate = False
        else:
            q, k, v, gate, la, qm = _gla_proj(h, w_in[i], gla_a1[j], gla_a2[j], gla_ab[j])
            op, sp = _gla_chunk_scan(_heads_major(q[:np_], bp, tp, GLA_DK), _heads_major(k[:np_], bp, tp, GLA_DK),
                                     _heads_major(v[:np_], bp, tp, GLA_DV), _heads_major(la[:np_], bp, tp, GLA_DK),
                                     gla_norm_g[j], jnp.zeros((bp * GLA_HEADS, GLA_DK, GLA_DV), F32))
            col = lambda t_: t_[np_:].reshape(bs * GLA_HEADS, GLA_DK, 1)
            os_, ss = _gla_step(col(q), col(k), v[np_:].reshape(bs * GLA_HEADS, 1, GLA_DV), col(la), gla_norm_g[j],
                                state_gla_S[j].reshape(bs * GLA_HEADS, GLA_DK, GLA_DV))
            tok = jnp.concatenate([
                op.reshape(bp, GLA_HEADS, tp, GLA_DV).transpose(0, 2, 1, 3).reshape(np_, TOK_WIDTH),
                os_.reshape(ns, TOK_WIDTH)], axis=0)
            p_gla_S.append(sp.reshape(bp, GLA_HEADS, GLA_DK, GLA_DV))
            s_gla_S.append(ss.reshape(bs, GLA_HEADS, GLA_DK, GLA_DV))
            silu_gate = True

        att = jnp.concatenate([
            _mem_attn(qm[:np_].reshape(bp, tp, MEM_WIDTH), pk[i], pv[i], 512).reshape(np_, MEM_WIDTH),
            _mem_attn(qm[np_:].reshape(bs, ts, MEM_WIDTH), sk[i], sv[i], 1).reshape(ns, MEM_WIDTH)], axis=0)
        x1, h2, logits = _out_proj(tok, gate, att, x, w_out[i], norm_ffn_g[i], w_router[i], b_router[i], silu_gate)
        row_tok, row_gate, pos, block_expert, n_valid = _route(logits, MOE_BLOCK)
        rows = _moe_ffn(h2[row_tok], row_gate, block_expert, n_valid, w_up_bf[i], w_down_bf[i])
        g_next = norm_mix_g[i + 1] if i + 1 < depth else norm_final_g
        x, h = _combine(x1, rows[pos[:, 0]], rows[pos[:, 1]], g_next)

    y_prompt = h[:np_].reshape(bp, tp, D_MODEL)
    y_sample = h[np_:].reshape(bs, ts, D_MODEL)
    return (y_prompt, y_sample, jnp.stack(p_rw_S), jnp.stack(p_rw_shift), jnp.stack(p_gla_S),
            prompt_mem_k, prompt_mem_v, jnp.stack(s_rw_S), jnp.stack(s_rw_shift), jnp.stack(s_gla_S))
```

```python
import functools

import jax
import jax.numpy as jnp
from jax import lax
from jax.experimental import pallas as pl
from jax.experimental.pallas import tpu as pltpu

F32 = jnp.float32
BF16 = jnp.bfloat16
HIGHEST = lax.Precision.HIGHEST

D_MODEL = 1024
TOK_WIDTH = 768
MEM_WIDTH = 256
MEM_HEADS = 4
MEM_HEAD_DIM = 64
N_MEM = 256
RWKV_HEADS = 12
RWKV_N = 64
RWKV_GN_EPS = 64e-5
GLA_HEADS = 4
GLA_KW = 384
GLA_DK = 96
GLA_DV = 192
GLA_TAU = 16.0
GLA_CHUNK = 64
GLA_TILE = 8
N_GROUPS = 4
EXPERTS_PER_GROUP = 8
N_EXPERTS = 32
EXPERT_FF = 512
NORM_EPS = 1e-6
ROUTER_LANES = 128
LANES = 128

TOKEN_BLOCK = 384
MOE_BLOCK = 256
SCAN_TIME_BLOCK = 32
VMEM_LIMIT = 56 * 1024 * 1024


def _cparams(*sem):
    return pltpu.CompilerParams(dimension_semantics=sem, vmem_limit_bytes=VMEM_LIMIT)


def _dot(a, b):
    return jnp.dot(a, b, preferred_element_type=F32)


def _dot_nt(a, b):
    return lax.dot_general(a, b, (((1,), (1,)), ((), ())), preferred_element_type=F32)


def _dot_tn(a, b):
    return lax.dot_general(a, b, (((0,), (0,)), ((), ())), preferred_element_type=F32)


def _rms(x, g):
    return x * lax.rsqrt(jnp.mean(x * x, axis=-1, keepdims=True) + NORM_EPS) * g


def _sigmoid(x):
    return 1.0 / (1.0 + jnp.exp(-x))


def _row_spec(tm, n):
    return pl.BlockSpec((tm, n), lambda i: (i, 0))


def _head_row_spec(tm, n):
    return pl.BlockSpec((GLA_HEADS, tm, n), lambda i: (0, i, 0))


def _full_spec(shape):
    nd = len(shape)
    return pl.BlockSpec(shape, lambda *_: (0,) * nd)


def _norm_kernel(x_ref, g_ref, o_ref):
    o_ref[...] = _rms(x_ref[...], g_ref[...])


def _norm(x, g, tm):
    m, d = x.shape
    return pl.pallas_call(
        _norm_kernel, grid=(m // tm,),
        in_specs=[_row_spec(tm, d), _full_spec((1, d))],
        out_specs=_row_spec(tm, d),
        out_shape=jax.ShapeDtypeStruct((m, d), F32),
        compiler_params=_cparams("parallel"), name="rms_norm")(x, g.reshape(1, d))


def _norm_matmul_kernel(x_ref, g_ref, w_ref, o_ref):
    o_ref[...] = _dot(_rms(x_ref[...], g_ref[...]).astype(BF16), w_ref[...])


def _norm_matmul(x, g, w, tm):
    m, d = x.shape
    n = w.shape[1]
    return pl.pallas_call(
        _norm_matmul_kernel, grid=(m // tm,),
        in_specs=[_row_spec(tm, d), _full_spec((1, d)), _full_spec((d, n))],
        out_specs=_row_spec(tm, n),
        out_shape=jax.ShapeDtypeStruct((m, n), F32),
        compiler_params=_cparams("parallel"), name="norm_matmul")(x, g.reshape(1, d), w.astype(BF16))


def _rwkv_proj_kernel(h_ref, hp_ref, mu_ref, wr_ref, wk_ref, wv_ref, wq_ref, w1_ref, w2_ref, w0_ref,
                      a1_ref, a2_ref, a0_ref, g1_ref, g2_ref,
                      r_out, k_out, v_out, d_out, a_out, g_out, q_out):
    h = h_ref[...]
    xx = hp_ref[...] - h

    def mix(j):
        return (h + xx * mu_ref[j:j + 1, :]).astype(BF16)

    r_out[...] = _dot(mix(0), wr_ref[...])
    wl = w0_ref[...] + _dot(jnp.tanh(_dot(mix(1), w1_ref[...])).astype(BF16), w2_ref[...])
    k_out[...] = _dot(mix(2), wk_ref[...])
    v_out[...] = _dot(mix(3), wv_ref[...])
    al = a0_ref[...] + _dot(_dot(mix(4), a1_ref[...]).astype(BF16), a2_ref[...])
    g_out[...] = _dot(_sigmoid(_dot(mix(5), g1_ref[...])).astype(BF16), g2_ref[...])
    q_out[...] = _dot(h.astype(BF16), wq_ref[...])
    z = -wl
    softplus = jnp.maximum(z, 0.0) + jnp.log(1.0 + jnp.exp(-jnp.abs(z)))
    d_out[...] = jnp.exp(-jnp.exp(-softplus - 0.5))
    a_out[...] = _sigmoid(al)


def _rwkv_proj(h, hp, w_in, mu, w0, w1, w2, a0, a1, a2, g1, g2):
    m = h.shape[0]
    tm = TOKEN_BLOCK
    tw = TOK_WIDTH
    bf = lambda t: t.astype(BF16)
    ws = [mu, bf(w_in[:, :tw]), bf(w_in[:, tw:2 * tw]), bf(w_in[:, 2 * tw:3 * tw]), bf(w_in[:, 3 * tw:]),
          bf(w1), bf(w2), w0.reshape(1, tw), bf(a1), bf(a2), a0.reshape(1, tw), bf(g1), bf(g2)]
    outs = [jax.ShapeDtypeStruct((m, tw), F32)] * 6 + [jax.ShapeDtypeStruct((m, MEM_WIDTH), F32)]
    return pl.pallas_call(
        _rwkv_proj_kernel, grid=(m // tm,),
        in_specs=[_row_spec(tm, D_MODEL), _row_spec(tm, D_MODEL)] + [_full_spec(w.shape) for w in ws],
        out_specs=[_row_spec(tm, tw)] * 6 + [_row_spec(tm, MEM_WIDTH)],
        out_shape=outs,
        compiler_params=_cparams("parallel"), name="rwkv_proj")(h, hp, *ws)


def _rwkv_scan_kernel(r_ref, k_ref, v_ref, d_ref, a_ref, kkp_ref, kap_ref, rkp_ref, lng_ref, lnb_ref, s0_ref,
                      y_ref, sfin_ref, s_scr, yrow_scr, *, tc):
    n = RWKV_N

    @pl.when(pl.program_id(1) == 0)
    def _():
        s_scr[...] = s0_ref[...]

    def step(t, carry):
        r_t = r_ref[t]
        k_t = k_ref[t]
        v_t = v_ref[t]
        d_t = d_ref[t]
        a_t = a_ref[t]
        kkr = k_t * kkp_ref[...]
        nrm = jnp.maximum(jnp.sqrt(jnp.sum(kkr * kkr, axis=0, keepdims=True)), 1e-12)
        kk = kkr * (1.0 / nrm)
        k2 = k_t * (1.0 + (a_t - 1.0) * kap_ref[...])
        nkk = -kk
        b_t = kk * a_t

        def ibody(i, c):
            s_i = s_scr[i]
            sa = jnp.sum(s_i * nkk, axis=0, keepdims=True)
            v_i = v_ref[t, pl.ds(i, 1), :]
            s_n = s_i * d_t + sa * b_t + v_i * k2
            s_scr[i] = s_n
            yrow_scr[pl.ds(i, 1), :] = jnp.sum(s_n * r_t, axis=0, keepdims=True)
            return c

        lax.fori_loop(0, n, ibody, 0, unroll=8)
        y = yrow_scr[...]
        yc = y - jnp.mean(y, axis=0, keepdims=True)
        var = jnp.mean(yc * yc, axis=0, keepdims=True)
        gn = yc * lax.rsqrt(var + RWKV_GN_EPS) * lng_ref[...] + lnb_ref[...]
        bonus = jnp.sum(r_t * k2 * rkp_ref[...], axis=0, keepdims=True) * v_t
        y_ref[t] = gn + bonus
        return carry

    lax.fori_loop(0, tc, step, 0)

    @pl.when(pl.program_id(1) == pl.num_programs(1) - 1)
    def _():
        sfin_ref[...] = s_scr[...]


def _rwkv_scan(r, k, v, d, a, params, s0, lane_block, tc):
    t, n, l = r.shape
    seq = pl.BlockSpec((tc, n, lane_block), lambda li, ti: (ti, 0, li))
    par = pl.BlockSpec((n, lane_block), lambda li, ti: (0, li))
    st = pl.BlockSpec((n, n, lane_block), lambda li, ti: (0, 0, li))
    return pl.pallas_call(
        functools.partial(_rwkv_scan_kernel, tc=tc), grid=(l // lane_block, t // tc),
        in_specs=[seq] * 5 + [par] * 5 + [st],
        out_specs=[seq, st],
        out_shape=[jax.ShapeDtypeStruct((t, n, l), F32), jax.ShapeDtypeStruct((n, n, l), F32)],
        scratch_shapes=[pltpu.VMEM((n, n, lane_block), F32), pltpu.VMEM((n, lane_block), F32)],
        compiler_params=_cparams("parallel", "arbitrary"), name="rwkv_scan")(r, k, v, d, a, *params, s0)


def _gla_proj_kernel(h_ref, wq_ref, wk_ref, wv_ref, wr_ref, wm_ref, a1_ref, a2_ref, ab_ref,
                     q_out, k_out, v_out, r_out, la_out, qm_out):
    hb = h_ref[...].astype(BF16)
    low = _dot(hb, a1_ref[...]).astype(BF16)
    for hd in range(GLA_HEADS):
        q_out[hd] = _dot(hb, wq_ref[hd])
        k_out[hd] = _dot(hb, wk_ref[hd])
        v_out[hd] = _dot(hb, wv_ref[hd])
        r_out[hd] = _dot(hb, wr_ref[hd])
        x = _dot(low, a2_ref[hd]) + ab_ref[hd]
        log_sigmoid = jnp.minimum(x, 0.0) - jnp.log(1.0 + jnp.exp(-jnp.abs(x)))
        la_out[hd] = log_sigmoid / GLA_TAU
    qm_out[...] = _dot(hb, wm_ref[...])


def _gla_proj(h, w_in, a1, a2, ab):
    m = h.shape[0]
    tm = TOKEN_BLOCK
    kw, tw, nh = GLA_KW, TOK_WIDTH, GLA_HEADS

    def heads(w, d):
        return w.reshape(w.shape[0], nh, d).transpose(1, 0, 2)

    ws = [heads(w_in[:, :kw], GLA_DK).astype(BF16), heads(w_in[:, kw:2 * kw], GLA_DK).astype(BF16),
          heads(w_in[:, 2 * kw:2 * kw + tw], GLA_DV).astype(BF16),
          heads(w_in[:, 2 * kw + tw:3 * tw], GLA_DV).astype(BF16), w_in[:, 3 * tw:].astype(BF16),
          a1.astype(BF16), heads(a2, GLA_DK).astype(BF16), ab.reshape(nh, 1, GLA_DK)]
    widths = [GLA_DK, GLA_DK, GLA_DV, GLA_DV, GLA_DK]
    return pl.pallas_call(
        _gla_proj_kernel, grid=(m // tm,),
        in_specs=[_row_spec(tm, D_MODEL)] + [_full_spec(w.shape) for w in ws],
        out_specs=[_head_row_spec(tm, w) for w in widths] + [_row_spec(tm, MEM_WIDTH)],
        out_shape=[jax.ShapeDtypeStruct((nh, m, w), F32) for w in widths] + [jax.ShapeDtypeStruct((m, MEM_WIDTH), F32)],
        compiler_params=_cparams("parallel"), name="gla_proj")(h, *ws)


def _gla_out_norm(o, g):
    return o * lax.rsqrt(jnp.mean(o * o, axis=-1, keepdims=True) + NORM_EPS) * g


def _gla_chunk_kernel(q_ref, k_ref, v_ref, la_ref, ng_ref, o_ref, sfin_ref, st_scr):
    c, dk = GLA_CHUNK, GLA_DK

    @pl.when(pl.program_id(1) == 0)
    def _():
        st_scr[...] = jnp.zeros(st_scr.shape, F32)

    row = lax.broadcasted_iota(jnp.int32, (c, c), 0)
    col = lax.broadcasted_iota(jnp.int32, (c, c), 1)
    tril = (row >= col).astype(F32)
    rr = lax.broadcasted_iota(jnp.int32, (c, dk), 0)
    ones_sum = jnp.ones((dk, LANES), BF16)

    for hd in range(GLA_HEADS):
        la = la_ref[hd]
        k = k_ref[hd]
        vb = v_ref[hd].astype(BF16)
        q = q_ref[hd] * (dk ** -0.5)
        b = jnp.dot(tril, la, precision=HIGHEST, preferred_element_type=F32)
        b_end = b[c - 1:c, :]
        st = st_scr[hd]
        inter = _dot_nt((q * jnp.exp(b)).astype(BF16), st.astype(BF16))

        att = jnp.zeros((c, c), F32)
        blk = c // 2
        while blk >= GLA_TILE:
            two = 2 * blk
            b_ref_rows = jnp.concatenate(
                [jnp.broadcast_to(b[s0 + blk - 1:s0 + blk, :], (two, dk)) for s0 in range(0, c, two)], axis=0)
            upper = (rr & (two - 1)) >= blk
            q_l = jnp.where(upper, q * jnp.exp(jnp.minimum(b - b_ref_rows, 0.0)), 0.0).astype(BF16)
            k_l = jnp.where(upper, 0.0, k * jnp.exp(jnp.minimum(b_ref_rows - b, 0.0))).astype(BF16)
            att = att + jnp.where((row ^ col) < two, _dot_nt(q_l, k_l), 0.0)
            blk //= 2

        prods = [(q * k).astype(BF16)]
        for dlt in range(1, GLA_TILE):
            k_s = pltpu.roll(k, dlt, axis=0)
            b_s = pltpu.roll(b, dlt, axis=0)
            p = q * k_s * jnp.exp(jnp.minimum(b - b_s, 0.0))
            prods.append(jnp.where((rr & (GLA_TILE - 1)) >= dlt, p, 0.0).astype(BF16))
        sums = _dot(jnp.concatenate(prods, axis=0), ones_sum)
        for dlt in range(GLA_TILE):
            att = att + jnp.where(col == row - dlt, sums[dlt * c:(dlt + 1) * c, :c], 0.0)

        o = inter + _dot(att.astype(BF16), vb)
        o_ref[hd] = _gla_out_norm(o, ng_ref[...])
        kd = (k * jnp.exp(b_end - b)).astype(BF16)
        st_scr[hd] = jnp.exp(b_end) * st + _dot_tn(vb, kd)

    @pl.when(pl.program_id(1) == pl.num_programs(1) - 1)
    def _():
        sfin_ref[0] = st_scr[...]


def _gla_chunk_scan(q, k, v, la, norm_g, batch, t):
    nh, dk, dv, c = GLA_HEADS, GLA_DK, GLA_DV, GLA_CHUNK
    nc = t // c
    kspec = pl.BlockSpec((nh, c, dk), lambda i, j: (0, i * nc + j, 0))
    vspec = pl.BlockSpec((nh, c, dv), lambda i, j: (0, i * nc + j, 0))
    sspec = pl.BlockSpec((1, nh, dv, dk), lambda i, j: (i, 0, 0, 0))
    return pl.pallas_call(
        _gla_chunk_kernel, grid=(batch, nc),
        in_specs=[kspec, kspec, vspec, kspec, pl.BlockSpec((1, dv), lambda i, j: (0, 0))],
        out_specs=[vspec, sspec],
        out_shape=[jax.ShapeDtypeStruct((nh, batch * t, dv), F32), jax.ShapeDtypeStruct((batch, nh, dv, dk), F32)],
        scratch_shapes=[pltpu.VMEM((nh, dv, dk), F32)],
        compiler_params=_cparams("parallel", "arbitrary"), name="gla_chunk")(q, k, v, la, norm_g.reshape(1, dv))


GLA_STEP_GROUP = 8


def _gla_step_kernel(q_ref, k_ref, v_ref, la_ref, ng_ref, s0_ref, o_ref, s_ref):
    for g in range(GLA_STEP_GROUP):
        s_new = jnp.exp(la_ref[g]) * s0_ref[g] + k_ref[g] * v_ref[g]
        s_ref[g] = s_new
        o = jnp.sum((q_ref[g] * (GLA_DK ** -0.5)) * s_new, axis=0, keepdims=True)
        o_ref[g] = _gla_out_norm(o, ng_ref[...])


def _gla_step(q, k, v, la, norm_g, s0):
    bh, dk, _ = q.shape
    dv = v.shape[2]
    g = GLA_STEP_GROUP
    cspec = pl.BlockSpec((g, dk, 1), lambda i: (i, 0, 0))
    vspec = pl.BlockSpec((g, 1, dv), lambda i: (i, 0, 0))
    sspec = pl.BlockSpec((g, dk, dv), lambda i: (i, 0, 0))
    return pl.pallas_call(
        _gla_step_kernel, grid=(bh // g,),
        in_specs=[cspec, cspec, vspec, cspec, _full_spec((1, dv)), sspec],
        out_specs=[vspec, sspec],
        out_shape=[jax.ShapeDtypeStruct((bh, 1, dv), F32), jax.ShapeDtypeStruct((bh, dk, dv), F32)],
        compiler_params=_cparams("parallel"), name="gla_step")(q, k, v, la, norm_g.reshape(1, dv), s0)


def _mem_attn_kernel(q_ref, k_ref, v_ref, o_ref):
    q = q_ref[0]
    k = k_ref[0].astype(BF16)
    v = v_ref[0].astype(BF16)
    head_of_lane = lax.broadcasted_iota(jnp.int32, (1, MEM_WIDTH), 1) // MEM_HEAD_DIM
    out = jnp.zeros(q.shape, F32)
    for h in range(MEM_HEADS):
        mine = head_of_lane == h
        qh = jnp.where(mine, q, 0.0).astype(BF16)
        s = _dot_nt(qh, k) * (MEM_HEAD_DIM ** -0.5)
        p = jnp.exp(s - jnp.max(s, axis=-1, keepdims=True))
        p = p / jnp.sum(p, axis=-1, keepdims=True)
        out = out + jnp.where(mine, _dot(p.astype(BF16), v), 0.0)
    o_ref[0] = out


def _mem_attn(q, mem_k, mem_v, tq):
    b, t, w = q.shape
    qspec = pl.BlockSpec((1, tq, w), lambda i, j: (i, j, 0))
    mspec = pl.BlockSpec((1, N_MEM, w), lambda i, j: (i, 0, 0))
    return pl.pallas_call(
        _mem_attn_kernel, grid=(b, t // tq),
        in_specs=[qspec, mspec, mspec], out_specs=qspec,
        out_shape=jax.ShapeDtypeStruct((b, t, w), F32),
        compiler_params=_cparams("parallel", "parallel"), name="mem_attn")(q, mem_k, mem_v)


def _out_proj_kernel(tok_ref, gate_ref, att_ref, x_ref, wo_ref, g_ref, wr_ref, br_ref,
                     x1_out, h2_out, logit_out, *, head_major):
    x1 = x_ref[...] + _dot(att_ref[...].astype(BF16), wo_ref[TOK_WIDTH:, :])
    if head_major:
        for hd in range(GLA_HEADS):
            gate = gate_ref[hd]
            mixed = (tok_ref[hd] * (gate * _sigmoid(gate))).astype(BF16)
            x1 = x1 + _dot(mixed, wo_ref[hd * GLA_DV:(hd + 1) * GLA_DV, :])
    else:
        x1 = x1 + _dot((tok_ref[...] * gate_ref[...]).astype(BF16), wo_ref[:TOK_WIDTH, :])
    x1_out[...] = x1
    h2 = _rms(x1, g_ref[...])
    h2_out[...] = h2.astype(BF16)
    logit_out[...] = jnp.dot(h2, wr_ref[...], precision=HIGHEST, preferred_element_type=F32) + br_ref[...]


def _out_proj(tok, gate, att, x, w_out, norm_g, w_router, b_router, head_major):
    m = x.shape[0]
    tm = TOKEN_BLOCK
    ws = [w_out.astype(BF16), norm_g.reshape(1, D_MODEL), w_router, b_router]
    tok_spec = _head_row_spec(tm, GLA_DV) if head_major else _row_spec(tm, TOK_WIDTH)
    return pl.pallas_call(
        functools.partial(_out_proj_kernel, head_major=head_major), grid=(m // tm,),
        in_specs=[tok_spec, tok_spec, _row_spec(tm, MEM_WIDTH), _row_spec(tm, D_MODEL)]
        + [_full_spec(w.shape) for w in ws],
        out_specs=[_row_spec(tm, D_MODEL), _row_spec(tm, D_MODEL), _row_spec(tm, ROUTER_LANES)],
        out_shape=[jax.ShapeDtypeStruct((m, D_MODEL), F32), jax.ShapeDtypeStruct((m, D_MODEL), BF16),
                   jax.ShapeDtypeStruct((m, ROUTER_LANES), F32)],
        compiler_params=_cparams("parallel"), name="out_proj")(tok, gate, att, x, *ws)


def _moe_kernel(tile_ref, exp_ref, nitem_ref, lo_ref, hi_ref, x_ref, gate_ref, wu_ref, wd_ref, o_ref, wu_scr, wd_scr):
    w = pl.program_id(0)
    prev = jnp.maximum(w - 1, 0)
    e = exp_ref[w]
    valid = w < nitem_ref[0]

    @pl.when(jnp.logical_and(valid, jnp.logical_or(w == 0, e != exp_ref[prev])))
    def _():
        wu_scr[...] = wu_ref[0].astype(BF16)
        wd_scr[...] = wd_ref[0].astype(BF16)

    @pl.when(jnp.logical_or(w == 0, tile_ref[w] != tile_ref[prev]))
    def _():
        o_ref[...] = jnp.zeros(o_ref.shape, F32)

    @pl.when(valid)
    def _():
        gu = _dot(x_ref[...], wu_scr[...])
        g = gu[:, :EXPERT_FF]
        act = (g * _sigmoid(g) * gu[:, EXPERT_FF:]).astype(BF16)
        out = _dot(act, wd_scr[...]) * gate_ref[...]
        rows = tile_ref[w] * MOE_BLOCK + lax.broadcasted_iota(jnp.int32, (MOE_BLOCK, 1), 0)
        mine = jnp.logical_and(rows >= lo_ref[e], rows < hi_ref[e])
        o_ref[...] = o_ref[...] + jnp.where(mine, out, 0.0)


def _moe_ffn(xs, row_gate, item_tile, item_expert, n_items, lo, hi, w_up, w_down):
    tm = MOE_BLOCK
    n_work = item_tile.shape[0]
    a = xs.shape[0]
    row_map = lambda w, tile, ex, ni, lo_, hi_: (tile[w], 0)
    exp_map = lambda w, tile, ex, ni, lo_, hi_: (ex[w], 0, 0)
    grid_spec = pltpu.PrefetchScalarGridSpec(
        num_scalar_prefetch=5, grid=(n_work,),
        in_specs=[pl.BlockSpec((tm, D_MODEL), row_map),
                  pl.BlockSpec((tm, 1), row_map),
                  pl.BlockSpec((1, D_MODEL, 2 * EXPERT_FF), exp_map),
                  pl.BlockSpec((1, EXPERT_FF, D_MODEL), exp_map)],
        out_specs=pl.BlockSpec((tm, D_MODEL), row_map),
        scratch_shapes=[pltpu.VMEM((D_MODEL, 2 * EXPERT_FF), BF16), pltpu.VMEM((EXPERT_FF, D_MODEL), BF16)])
    return pl.pallas_call(
        _moe_kernel, grid_spec=grid_spec,
        out_shape=jax.ShapeDtypeStruct((a, D_MODEL), F32),
        compiler_params=_cparams("arbitrary"), name="moe_ffn")(
            item_tile, item_expert, n_items, lo, hi, xs, row_gate, w_up, w_down)


def _route(logits, tm):
    m = logits.shape[0]
    a = 2 * m
    gl = logits[:, :N_GROUPS]
    el = logits[:, N_GROUPS:N_GROUPS + N_EXPERTS].reshape(m, N_GROUPS, EXPERTS_PER_GROUP)
    group = jnp.argmax(gl, -1).astype(jnp.int32)
    p_group = jnp.max(jax.nn.softmax(gl, -1), -1, keepdims=True)
    in_group = jnp.take_along_axis(el, group[:, None, None], axis=1)[:, 0]
    top_val, top_idx = lax.top_k(in_group, 2)
    gate = p_group * jax.nn.softmax(top_val, -1)
    expert = group[:, None] * EXPERTS_PER_GROUP + top_idx.astype(jnp.int32)
    flat_e = expert.T.reshape(a)
    flat_gate = gate.T.reshape(a)
    ids = jnp.arange(a, dtype=jnp.int32)
    e_sorted, order, gate_sorted = lax.sort((flat_e, ids, flat_gate), num_keys=1, is_stable=True)
    _, inv = lax.sort((order, ids), num_keys=1)
    tok_sorted = jnp.where(order >= m, order - m, order)
    counts = jnp.sum((flat_e[:, None] == jnp.arange(N_EXPERTS, dtype=jnp.int32)[None, :]).astype(jnp.int32), axis=0)
    hi = jnp.cumsum(counts).astype(jnp.int32)
    lo = hi - counts
    n_tiles = a // tm
    first_tile = lo // tm
    tiles_of = jnp.where(counts > 0, (hi - 1) // tm - first_tile + 1, 0)
    item_end = jnp.cumsum(tiles_of).astype(jnp.int32)
    n_items = item_end[-1:]
    n_work = n_tiles + N_EXPERTS - 1
    w = jnp.arange(n_work, dtype=jnp.int32)
    item_expert = jnp.minimum(jnp.searchsorted(item_end, w, side='right'), N_EXPERTS - 1).astype(jnp.int32)
    item_tile = first_tile[item_expert] + w - (item_end - tiles_of)[item_expert]
    last = jnp.maximum(n_items[0] - 1, 0)
    item_expert = jnp.where(w < n_items[0], item_expert, item_expert[last])
    item_tile = jnp.where(w < n_items[0], item_tile, item_tile[last]).astype(jnp.int32)
    return tok_sorted, gate_sorted.reshape(a, 1), inv, item_tile, item_expert, n_items, lo, hi


def _combine_kernel(x_ref, r0_ref, r1_ref, g_ref, x_out, h_out):
    x2 = x_ref[...] + (r0_ref[...] + r1_ref[...])
    x_out[...] = x2
    h_out[...] = _rms(x2, g_ref[...])


def _combine(x1, gathered, g):
    m = x1.shape[0]
    tm = TOKEN_BLOCK
    nb = m // tm
    spec = _row_spec(tm, D_MODEL)
    return pl.pallas_call(
        _combine_kernel, grid=(nb,),
        in_specs=[spec, spec, pl.BlockSpec((tm, D_MODEL), lambda i: (i + nb, 0)), _full_spec((1, D_MODEL))],
        out_specs=[spec, spec],
        out_shape=[jax.ShapeDtypeStruct((m, D_MODEL), F32)] * 2,
        compiler_params=_cparams("parallel"), name="moe_combine")(x1, gathered, gathered, g.reshape(1, D_MODEL))


def _to_scan_layout(t2d, b, t):
    return t2d.reshape(b, t, RWKV_HEADS, RWKV_N).transpose(1, 3, 0, 2).reshape(t, RWKV_N, b * RWKV_HEADS)


def _from_scan_layout(y, b, t):
    return y.reshape(t, RWKV_N, b, RWKV_HEADS).transpose(2, 0, 3, 1).reshape(b * t, TOK_WIDTH)


def _scan_param(p, b):
    return jnp.tile(p.reshape(RWKV_HEADS, RWKV_N).T, (1, b))


def _state_to_scan(s):
    b = s.shape[0]
    return s.transpose(2, 3, 0, 1).reshape(RWKV_N, RWKV_N, b * RWKV_HEADS)


def _state_from_scan(s, b):
    return s.reshape(RWKV_N, RWKV_N, b, RWKV_HEADS).transpose(2, 3, 0, 1)


def kernel(x_prompt, x_sample, mem_prompt, state_rwkv_S, state_rwkv_shift, state_gla_S, cache_mem_k, cache_mem_v, norm_mix_g, norm_ffn_g, norm_mem_g, norm_final_g, w_in, w_out, w_mem_kv, rw_mu, rw_w0, rw_w1, rw_w2, rw_a0, rw_a1, rw_a2, rw_g1, rw_g2, rw_k_k, rw_k_a, rw_r_k, rw_ln_g, rw_ln_b, gla_a1, gla_a2, gla_ab, gla_norm_g, router_wg, router_bg, router_we, router_be, exp_w_up, exp_w_down):
    bp, tp, _ = x_prompt.shape
    bs, ts, _ = x_sample.shape
    assert ts == 1 and tp % GLA_CHUNK == 0 and tp % SCAN_TIME_BLOCK == 0
    np_ = bp * tp
    ns = bs * ts
    m = np_ + ns
    assert m % TOKEN_BLOCK == 0 and (2 * m) % MOE_BLOCK == 0
    depth = w_in.shape[0]
    nh = GLA_HEADS

    mem2d = mem_prompt.reshape(bp * N_MEM, D_MODEL)
    mem_kv = [_norm_matmul(mem2d, norm_mem_g[i], w_mem_kv[i], 512) for i in range(depth)]
    pk = [kv[:, :MEM_WIDTH].reshape(bp, N_MEM, MEM_WIDTH) for kv in mem_kv]
    pv = [kv[:, MEM_WIDTH:].reshape(bp, N_MEM, MEM_WIDTH) for kv in mem_kv]
    prompt_mem_k = jnp.stack(pk).reshape(depth, bp, N_MEM, MEM_HEADS, MEM_HEAD_DIM)
    prompt_mem_v = jnp.stack(pv).reshape(depth, bp, N_MEM, MEM_HEADS, MEM_HEAD_DIM)
    sk = cache_mem_k.reshape(depth, bs, N_MEM, MEM_WIDTH)
    sv = cache_mem_v.reshape(depth, bs, N_MEM, MEM_WIDTH)

    x = jnp.concatenate([x_prompt.reshape(np_, D_MODEL), x_sample.reshape(ns, D_MODEL)], axis=0)
    h = _norm(x, norm_mix_g[0], TOKEN_BLOCK)

    w_router = jnp.zeros((depth, D_MODEL, ROUTER_LANES), F32)
    w_router = w_router.at[:, :, :N_GROUPS].set(router_wg).at[:, :, N_GROUPS:N_GROUPS + N_EXPERTS].set(router_we)
    b_router = jnp.zeros((depth, 1, ROUTER_LANES), F32)
    b_router = b_router.at[:, 0, :N_GROUPS].set(router_bg).at[:, 0, N_GROUPS:N_GROUPS + N_EXPERTS].set(router_be)

    p_rw_S, p_rw_shift, p_gla_S, s_rw_S, s_rw_shift, s_gla_S = [], [], [], [], [], []
    for i in range(depth):
        j = i // 2
        if i % 2 == 0:
            hp3 = h[:np_].reshape(bp, tp, D_MODEL)
            h_prev = jnp.concatenate([
                jnp.concatenate([jnp.zeros((bp, 1, D_MODEL), F32), hp3[:, :-1]], axis=1).reshape(np_, D_MODEL),
                state_rwkv_shift[j]], axis=0)
            r, k, v, d, a, gate, qm = _rwkv_proj(h, h_prev, w_in[i], rw_mu[j], rw_w0[j], rw_w1[j], rw_w2[j],
                                                 rw_a0[j], rw_a1[j], rw_a2[j], rw_g1[j], rw_g2[j])
            pvec = [rw_k_k[j], rw_k_a[j], rw_r_k[j], rw_ln_g[j], rw_ln_b[j]]
            yp, sp = _rwkv_scan(*[_to_scan_layout(t_[:np_], bp, tp) for t_ in (r, k, v, d, a)],
                                [_scan_param(p, bp) for p in pvec],
                                jnp.zeros((RWKV_N, RWKV_N, bp * RWKV_HEADS), F32), bp * RWKV_HEADS, SCAN_TIME_BLOCK)
            ys, ss = _rwkv_scan(*[_to_scan_layout(t_[np_:], bs, ts) for t_ in (r, k, v, d, a)],
                                [_scan_param(p, bs) for p in pvec],
                                _state_to_scan(state_rwkv_S[j]), LANES, 1)
            tok = jnp.concatenate([_from_scan_layout(yp, bp, tp), _from_scan_layout(ys, bs, ts)], axis=0)
            p_rw_S.append(_state_from_scan(sp, bp))
            s_rw_S.append(_state_from_scan(ss, bs))
            p_rw_shift.append(hp3[:, -1])
            s_rw_shift.append(h[np_:])
            head_major = False
        else:
            q, k, v, gate, la, qm = _gla_proj(h, w_in[i], gla_a1[j], gla_a2[j], gla_ab[j])
            op, sp_t = _gla_chunk_scan(q, k, v, la, gla_norm_g[j], bp, tp)
            col = lambda t_: t_[:, np_:].transpose(1, 0, 2).reshape(bs * nh, GLA_DK, 1)
            os_, ss = _gla_step(col(q), col(k), v[:, np_:].transpose(1, 0, 2).reshape(bs * nh, 1, GLA_DV), col(la),
                                gla_norm_g[j], state_gla_S[j].reshape(bs * nh, GLA_DK, GLA_DV))
            tok = jnp.concatenate([op, os_.reshape(bs, nh, GLA_DV).transpose(1, 0, 2)], axis=1)
            p_gla_S.append(sp_t.transpose(0, 1, 3, 2))
            s_gla_S.append(ss.reshape(bs, nh, GLA_DK, GLA_DV))
            head_major = True

        att = jnp.concatenate([
            _mem_attn(qm[:np_].reshape(bp, tp, MEM_WIDTH), pk[i], pv[i], 512).reshape(np_, MEM_WIDTH),
            _mem_attn(qm[np_:].reshape(bs, ts, MEM_WIDTH), sk[i], sv[i], 1).reshape(ns, MEM_WIDTH)], axis=0)
        x1, h2, logits = _out_proj(tok, gate, att, x, w_out[i], norm_ffn_g[i], w_router[i], b_router[i], head_major)
        tok_sorted, gate_sorted, inv, item_tile, item_expert, n_items, lo, hi = _route(logits, MOE_BLOCK)
        rows = _moe_ffn(h2[tok_sorted], gate_sorted, item_tile, item_expert, n_items, lo, hi,
                        exp_w_up[i], exp_w_down[i])
        g_next = norm_mix_g[i + 1] if i + 1 < depth else norm_final_g
        x, h = _combine(x1, rows[inv], g_next)

    y_prompt = h[:np_].reshape(bp, tp, D_MODEL)
    y_sample = h[np_:].reshape(bs, ts, D_MODEL)
    return (y_prompt, y_sample, jnp.stack(p_rw_S), jnp.stack(p_rw_shift), jnp.stack(p_gla_S),
            prompt_mem_k, prompt_mem_v, jnp.stack(s_rw_S), jnp.stack(s_rw_shift), jnp.stack(s_gla_S))
```

```python
import functools

import numpy as np
import jax
import jax.numpy as jnp
from jax import lax
from jax.experimental import pallas as pl
from jax.experimental.pallas import tpu as pltpu

F32 = jnp.float32
BF16 = jnp.bfloat16
HIGHEST = lax.Precision.HIGHEST

D_MODEL = 1024
TOK_WIDTH = 768
MEM_WIDTH = 256
MEM_HEADS = 4
MEM_HEAD_DIM = 64
N_MEM = 256
RWKV_HEADS = 12
RWKV_N = 64
RWKV_GN_EPS = 64e-5
GLA_HEADS = 4
GLA_KW = 384
GLA_DK = 96
GLA_DV = 192
GLA_TAU = 16.0
GLA_CHUNK = 64
GLA_TILE = 8
N_GROUPS = 4
EXPERTS_PER_GROUP = 8
N_EXPERTS = 32
EXPERT_FF = 512
NORM_EPS = 1e-6
ROUTER_LANES = 128
LANES = 128
SUBLANES = 8

PROJ_BLOCK = 256
LIGHT_BLOCK = 512
SAMPLE_BLOCK = 128
MOE_BLOCK = 256
SCAN_TIME_BLOCK = 32
VMEM_LIMIT = 56 * 1024 * 1024


def _cparams(*sem):
    return pltpu.CompilerParams(dimension_semantics=sem, vmem_limit_bytes=VMEM_LIMIT)


def _dot(a, b):
    return jnp.dot(a, b, preferred_element_type=F32)


def _dot_nt(a, b):
    return lax.dot_general(a, b, (((1,), (1,)), ((), ())), preferred_element_type=F32)


def _dot_tn(a, b):
    return lax.dot_general(a, b, (((0,), (0,)), ((), ())), preferred_element_type=F32)


def _rms(x, g):
    return x * lax.rsqrt(jnp.mean(x * x, axis=-1, keepdims=True) + NORM_EPS) * g


def _sigmoid(x):
    return 1.0 / (1.0 + jnp.exp(-x))


def _row_spec(tm, n, offset=0):
    return pl.BlockSpec((tm, n), lambda i: (i + offset, 0))


def _head_row_spec(tm, n):
    return pl.BlockSpec((GLA_HEADS, tm, n), lambda i: (0, i, 0))


def _full_spec(shape):
    nd = len(shape)
    return pl.BlockSpec(shape, lambda *_: (0,) * nd)


def _norm_kernel(x_ref, g_ref, o_ref):
    o_ref[...] = _rms(x_ref[...], g_ref[...])


def _norm(x, g, tm):
    m, d = x.shape
    return pl.pallas_call(
        _norm_kernel, grid=(m // tm,),
        in_specs=[_row_spec(tm, d), _full_spec((1, d))],
        out_specs=_row_spec(tm, d),
        out_shape=jax.ShapeDtypeStruct((m, d), F32),
        compiler_params=_cparams("parallel"), name="rms_norm")(x, g.reshape(1, d))


def _norm_matmul_kernel(x_ref, g_ref, w_ref, o_ref):
    o_ref[...] = _dot(_rms(x_ref[...], g_ref[...]).astype(BF16), w_ref[...])


def _norm_matmul(x, g, w, tm):
    m, d = x.shape
    n = w.shape[1]
    return pl.pallas_call(
        _norm_matmul_kernel, grid=(m // tm,),
        in_specs=[_row_spec(tm, d), _full_spec((1, d)), _full_spec((d, n))],
        out_specs=_row_spec(tm, n),
        out_shape=jax.ShapeDtypeStruct((m, n), F32),
        compiler_params=_cparams("parallel"), name="norm_matmul")(x, g.reshape(1, d), w.astype(BF16))


def _rwkv_proj_kernel(h_ref, hp_ref, mu_ref, wr_ref, wk_ref, wv_ref, wq_ref, w1_ref, w2_ref, w0_ref,
                      a1_ref, a2_ref, a0_ref, g1_ref, g2_ref,
                      r_out, k_out, v_out, d_out, a_out, g_out, q_out, *, blocks_per_seq):
    h = h_ref[...]
    if blocks_per_seq:
        seq_start = (pl.program_id(0) % blocks_per_seq) == 0
        before = jnp.where(seq_start, 0.0, hp_ref[SUBLANES - 1:SUBLANES, :])
        row = lax.broadcasted_iota(jnp.int32, (h.shape[0], 1), 0)
        hp = jnp.where(row == 0, before, pltpu.roll(h, 1, axis=0))
    else:
        hp = hp_ref[...]
    xx = hp - h

    def mix(j):
        return (h + xx * mu_ref[j:j + 1, :]).astype(BF16)

    r_out[...] = _dot(mix(0), wr_ref[...])
    wl = w0_ref[...] + _dot(jnp.tanh(_dot(mix(1), w1_ref[...])).astype(BF16), w2_ref[...])
    k_out[...] = _dot(mix(2), wk_ref[...])
    v_out[...] = _dot(mix(3), wv_ref[...])
    al = a0_ref[...] + _dot(_dot(mix(4), a1_ref[...]).astype(BF16), a2_ref[...])
    g_out[...] = _dot(_sigmoid(_dot(mix(5), g1_ref[...])).astype(BF16), g2_ref[...])
    q_out[...] = _dot(h.astype(BF16), wq_ref[...])
    z = -wl
    softplus = jnp.maximum(z, 0.0) + jnp.log(1.0 + jnp.exp(-jnp.abs(z)))
    d_out[...] = jnp.exp(-jnp.exp(-softplus - 0.5))
    a_out[...] = _sigmoid(al)


def _rwkv_proj(h, h_prev, ws, tm, seq_len):
    m = h.shape[0]
    tw = TOK_WIDTH
    if h_prev is None:
        assert seq_len % tm == 0 and tm % SUBLANES == 0
        per8 = tm // SUBLANES
        hp_spec = pl.BlockSpec((SUBLANES, D_MODEL), lambda i: (jnp.maximum(i * per8 - 1, 0), 0))
        h_prev, blocks_per_seq = h, seq_len // tm
    else:
        hp_spec, blocks_per_seq = _row_spec(tm, D_MODEL), 0
    outs = [jax.ShapeDtypeStruct((m, tw), F32)] * 6 + [jax.ShapeDtypeStruct((m, MEM_WIDTH), F32)]
    return pl.pallas_call(
        functools.partial(_rwkv_proj_kernel, blocks_per_seq=blocks_per_seq), grid=(m // tm,),
        in_specs=[_row_spec(tm, D_MODEL), hp_spec] + [_full_spec(w.shape) for w in ws],
        out_specs=[_row_spec(tm, tw)] * 6 + [_row_spec(tm, MEM_WIDTH)],
        out_shape=outs,
        compiler_params=_cparams("parallel"), name="rwkv_proj")(h, h_prev, *ws)


def _pad_lanes(x):
    short = LANES - x.shape[-1]
    if short == 0:
        return x
    return jnp.concatenate([x, jnp.zeros(x.shape[:-1] + (short,), x.dtype)], axis=-1)


def _rwkv_scan_kernel(r_ref, k_ref, v_ref, d_ref, a_ref, kkp_ref, kap_ref, rkp_ref, lng_ref, lnb_ref, s0_ref,
                      y_ref, sfin_ref, s_scr, v_scr, yrow_scr, *, tc):
    n = RWKV_N
    nl = r_ref.shape[-1]

    @pl.when(pl.program_id(1) == 0)
    def _():
        s_scr[...] = _pad_lanes(s0_ref[...])

    kkp = _pad_lanes(kkp_ref[...])
    kap = _pad_lanes(kap_ref[...])
    rkp = _pad_lanes(rkp_ref[...])
    lng = _pad_lanes(lng_ref[...])
    lnb = _pad_lanes(lnb_ref[...])

    def step(t, carry):
        r_t = _pad_lanes(r_ref[t])
        k_t = _pad_lanes(k_ref[t])
        v_t = _pad_lanes(v_ref[t])
        d_t = _pad_lanes(d_ref[t])
        a_t = _pad_lanes(a_ref[t])
        v_scr[...] = v_t
        kkr = k_t * kkp
        nrm = jnp.maximum(jnp.sqrt(jnp.sum(kkr * kkr, axis=0, keepdims=True)), 1e-12)
        kk = kkr * (1.0 / nrm)
        k2 = k_t * (1.0 + (a_t - 1.0) * kap)
        nkk = -kk
        b_t = kk * a_t

        def ibody(i, c):
            s_i = s_scr[i]
            sa = jnp.sum(s_i * nkk, axis=0, keepdims=True)
            v_i = v_scr[pl.ds(i, 1), :]
            s_n = s_i * d_t + sa * b_t + v_i * k2
            s_scr[i] = s_n
            yrow_scr[pl.ds(i, 1), :] = jnp.sum(s_n * r_t, axis=0, keepdims=True)
            return c

        lax.fori_loop(0, n, ibody, 0, unroll=8)
        y = yrow_scr[...]
        yc = y - jnp.mean(y, axis=0, keepdims=True)
        var = jnp.mean(yc * yc, axis=0, keepdims=True)
        gn = yc * lax.rsqrt(var + RWKV_GN_EPS) * lng + lnb
        bonus = jnp.sum(r_t * k2 * rkp, axis=0, keepdims=True) * v_t
        y_ref[t] = (gn + bonus)[:, :nl]
        return carry

    lax.fori_loop(0, tc, step, 0)

    @pl.when(pl.program_id(1) == pl.num_programs(1) - 1)
    def _():
        sfin_ref[...] = s_scr[:, :, :nl]


def _rwkv_scan(r, k, v, d, a, params, s0, lane_block, tc):
    t, n, l = r.shape
    seq = pl.BlockSpec((tc, n, lane_block), lambda li, ti: (ti, 0, li))
    par = pl.BlockSpec((n, lane_block), lambda li, ti: (0, li))
    st = pl.BlockSpec((n, n, lane_block), lambda li, ti: (0, 0, li))
    return pl.pallas_call(
        functools.partial(_rwkv_scan_kernel, tc=tc), grid=(l // lane_block, t // tc),
        in_specs=[seq] * 5 + [par] * 5 + [st],
        out_specs=[seq, st],
        out_shape=[jax.ShapeDtypeStruct((t, n, l), F32), jax.ShapeDtypeStruct((n, n, l), F32)],
        scratch_shapes=[pltpu.VMEM((n, n, LANES), F32), pltpu.VMEM((n, LANES), F32), pltpu.VMEM((n, LANES), F32)],
        compiler_params=_cparams("parallel", "arbitrary"), name="rwkv_scan")(r, k, v, d, a, *params, s0)


def _gla_proj_kernel(h_ref, wq_ref, wk_ref, wv_ref, wr_ref, wm_ref, a1_ref, a2_ref, ab_ref,
                     q_out, k_out, v_out, r_out, la_out, qm_out):
    hb = h_ref[...].astype(BF16)
    low = _dot(hb, a1_ref[...]).astype(BF16)
    for hd in range(GLA_HEADS):
        q_out[hd] = _dot(hb, wq_ref[hd])
        k_out[hd] = _dot(hb, wk_ref[hd])
        v_out[hd] = _dot(hb, wv_ref[hd])
        r_out[hd] = _dot(hb, wr_ref[hd])
        x = _dot(low, a2_ref[hd]) + ab_ref[hd]
        log_sigmoid = jnp.minimum(x, 0.0) - jnp.log(1.0 + jnp.exp(-jnp.abs(x)))
        la_out[hd] = log_sigmoid / GLA_TAU
    qm_out[...] = _dot(hb, wm_ref[...])


def _gla_weights(w_in, a1, a2, ab):
    kw, tw, nh = GLA_KW, TOK_WIDTH, GLA_HEADS

    def heads(w, d):
        return w.reshape(w.shape[0], nh, d).transpose(1, 0, 2)

    return [heads(w_in[:, :kw], GLA_DK).astype(BF16), heads(w_in[:, kw:2 * kw], GLA_DK).astype(BF16),
            heads(w_in[:, 2 * kw:2 * kw + tw], GLA_DV).astype(BF16),
            heads(w_in[:, 2 * kw + tw:3 * tw], GLA_DV).astype(BF16), w_in[:, 3 * tw:].astype(BF16),
            a1.astype(BF16), heads(a2, GLA_DK).astype(BF16), ab.reshape(nh, 1, GLA_DK)]


def _gla_proj(h, ws, tm):
    m = h.shape[0]
    nh = GLA_HEADS
    widths = [GLA_DK, GLA_DK, GLA_DV, GLA_DV, GLA_DK]
    return pl.pallas_call(
        _gla_proj_kernel, grid=(m // tm,),
        in_specs=[_row_spec(tm, D_MODEL)] + [_full_spec(w.shape) for w in ws],
        out_specs=[_head_row_spec(tm, w) for w in widths] + [_row_spec(tm, MEM_WIDTH)],
        out_shape=[jax.ShapeDtypeStruct((nh, m, w), F32) for w in widths] + [jax.ShapeDtypeStruct((m, MEM_WIDTH), F32)],
        compiler_params=_cparams("parallel"), name="gla_proj")(h, *ws)


def _gla_out_norm(o, g):
    return o * lax.rsqrt(jnp.mean(o * o, axis=-1, keepdims=True) + NORM_EPS) * g


def _gla_chunk_kernel(q_ref, k_ref, v_ref, la_ref, ng_ref, o_ref, sfin_ref, st_scr):
    c, dk = GLA_CHUNK, GLA_DK

    @pl.when(pl.program_id(1) == 0)
    def _():
        st_scr[...] = jnp.zeros(st_scr.shape, F32)

    row = lax.broadcasted_iota(jnp.int32, (c, c), 0)
    col = lax.broadcasted_iota(jnp.int32, (c, c), 1)
    tril = (row >= col).astype(F32)
    rr = lax.broadcasted_iota(jnp.int32, (c, dk), 0)
    ones_sum = jnp.ones((dk, LANES), BF16)

    for hd in range(GLA_HEADS):
        la = la_ref[hd]
        k = k_ref[hd]
        vb = v_ref[hd].astype(BF16)
        q = q_ref[hd] * (dk ** -0.5)
        b = jnp.dot(tril, la, precision=HIGHEST, preferred_element_type=F32)
        b_end = b[c - 1:c, :]
        st = st_scr[hd]
        inter = _dot_nt((q * jnp.exp(b)).astype(BF16), st.astype(BF16))

        att = jnp.zeros((c, c), F32)
        blk = c // 2
        while blk >= GLA_TILE:
            two = 2 * blk
            b_ref_rows = jnp.concatenate(
                [jnp.broadcast_to(b[s0 + blk - 1:s0 + blk, :], (two, dk)) for s0 in range(0, c, two)], axis=0)
            upper = (rr & (two - 1)) >= blk
            q_l = jnp.where(upper, q * jnp.exp(jnp.minimum(b - b_ref_rows, 0.0)), 0.0).astype(BF16)
            k_l = jnp.where(upper, 0.0, k * jnp.exp(jnp.minimum(b_ref_rows - b, 0.0))).astype(BF16)
            att = att + jnp.where((row ^ col) < two, _dot_nt(q_l, k_l), 0.0)
            blk //= 2

        prods = [(q * k).astype(BF16)]
        for dlt in range(1, GLA_TILE):
            k_s = pltpu.roll(k, dlt, axis=0)
            b_s = pltpu.roll(b, dlt, axis=0)
            p = q * k_s * jnp.exp(jnp.minimum(b - b_s, 0.0))
            prods.append(jnp.where((rr & (GLA_TILE - 1)) >= dlt, p, 0.0).astype(BF16))
        sums = _dot(jnp.concatenate(prods, axis=0), ones_sum)
        for dlt in range(GLA_TILE):
            att = att + jnp.where(col == row - dlt, sums[dlt * c:(dlt + 1) * c, :c], 0.0)

        o = inter + _dot(att.astype(BF16), vb)
        o_ref[hd] = _gla_out_norm(o, ng_ref[...])
        kd = (k * jnp.exp(b_end - b)).astype(BF16)
        st_scr[hd] = jnp.exp(b_end) * st + _dot_tn(vb, kd)

    @pl.when(pl.program_id(1) == pl.num_programs(1) - 1)
    def _():
        sfin_ref[0] = st_scr[...]


def _gla_chunk_scan(q, k, v, la, norm_g, batch, t):
    nh, dk, dv, c = GLA_HEADS, GLA_DK, GLA_DV, GLA_CHUNK
    nc = t // c
    kspec = pl.BlockSpec((nh, c, dk), lambda i, j: (0, i * nc + j, 0))
    vspec = pl.BlockSpec((nh, c, dv), lambda i, j: (0, i * nc + j, 0))
    sspec = pl.BlockSpec((1, nh, dv, dk), lambda i, j: (i, 0, 0, 0))
    return pl.pallas_call(
        _gla_chunk_kernel, grid=(batch, nc),
        in_specs=[kspec, kspec, vspec, kspec, pl.BlockSpec((1, dv), lambda i, j: (0, 0))],
        out_specs=[vspec, sspec],
        out_shape=[jax.ShapeDtypeStruct((nh, batch * t, dv), F32), jax.ShapeDtypeStruct((batch, nh, dv, dk), F32)],
        scratch_shapes=[pltpu.VMEM((nh, dv, dk), F32)],
        compiler_params=_cparams("parallel", "arbitrary"), name="gla_chunk")(q, k, v, la, norm_g.reshape(1, dv))


GLA_STEP_GROUP = 8


def _gla_step_kernel(q_ref, k_ref, v_ref, la_ref, ng_ref, s0_ref, o_ref, s_ref):
    for g in range(GLA_STEP_GROUP):
        s_new = jnp.exp(la_ref[g]) * s0_ref[g] + k_ref[g] * v_ref[g]
        s_ref[g] = s_new
        o = jnp.sum((q_ref[g] * (GLA_DK ** -0.5)) * s_new, axis=0, keepdims=True)
        o_ref[g] = _gla_out_norm(o, ng_ref[...])


def _gla_step(q, k, v, la, norm_g, s0):
    bh, dk, _ = q.shape
    dv = v.shape[2]
    g = GLA_STEP_GROUP
    cspec = pl.BlockSpec((g, dk, 1), lambda i: (i, 0, 0))
    vspec = pl.BlockSpec((g, 1, dv), lambda i: (i, 0, 0))
    sspec = pl.BlockSpec((g, dk, dv), lambda i: (i, 0, 0))
    return pl.pallas_call(
        _gla_step_kernel, grid=(bh // g,),
        in_specs=[cspec, cspec, vspec, cspec, _full_spec((1, dv)), sspec],
        out_specs=[vspec, sspec],
        out_shape=[jax.ShapeDtypeStruct((bh, 1, dv), F32), jax.ShapeDtypeStruct((bh, dk, dv), F32)],
        compiler_params=_cparams("parallel"), name="gla_step")(q, k, v, la, norm_g.reshape(1, dv), s0)


def _mem_attn_kernel(q_ref, k_ref, v_ref, o_ref):
    q = q_ref[0]
    k = k_ref[0].astype(BF16)
    v = v_ref[0].astype(BF16)
    head_of_lane = lax.broadcasted_iota(jnp.int32, (1, MEM_WIDTH), 1) // MEM_HEAD_DIM
    out = jnp.zeros(q.shape, F32)
    for h in range(MEM_HEADS):
        mine = head_of_lane == h
        qh = jnp.where(mine, q, 0.0).astype(BF16)
        s = _dot_nt(qh, k) * (MEM_HEAD_DIM ** -0.5)
        p = jnp.exp(s - jnp.max(s, axis=-1, keepdims=True))
        p = p / jnp.sum(p, axis=-1, keepdims=True)
        out = out + jnp.where(mine, _dot(p.astype(BF16), v), 0.0)
    o_ref[0] = out


def _mem_attn(q, mem_k, mem_v, tq):
    b, t, w = q.shape
    qspec = pl.BlockSpec((1, tq, w), lambda i, j: (i, j, 0))
    mspec = pl.BlockSpec((1, N_MEM, w), lambda i, j: (i, 0, 0))
    return pl.pallas_call(
        _mem_attn_kernel, grid=(b, t // tq),
        in_specs=[qspec, mspec, mspec], out_specs=qspec,
        out_shape=jax.ShapeDtypeStruct((b, t, w), F32),
        compiler_params=_cparams("parallel", "parallel"), name="mem_attn")(q, mem_k, mem_v)


def _out_proj_kernel(*refs, head_major, aliased):
    tok_ref, gate_ref, att_ref, x_ref, wo_ref, g_ref, wr_ref, br_ref = refs[:8]
    x1_out, h2_out, logit_out = refs[8 + aliased:]
    x1 = x_ref[...] + _dot(att_ref[...].astype(BF16), wo_ref[TOK_WIDTH:, :])
    if head_major:
        for hd in range(GLA_HEADS):
            gate = gate_ref[hd]
            mixed = (tok_ref[hd] * (gate * _sigmoid(gate))).astype(BF16)
            x1 = x1 + _dot(mixed, wo_ref[hd * GLA_DV:(hd + 1) * GLA_DV, :])
    else:
        x1 = x1 + _dot((tok_ref[...] * gate_ref[...]).astype(BF16), wo_ref[:TOK_WIDTH, :])
    x1_out[...] = x1
    h2 = _rms(x1, g_ref[...])
    h2_out[...] = h2
    logit_out[...] = jnp.dot(h2, wr_ref[...], precision=HIGHEST, preferred_element_type=F32) + br_ref[...]


def _out_proj(tok, gate, att, x, ws, head_major, tm, h2_rows, h2_row_offset, h2_buffer=None):
    m = x.shape[0]
    aliased = h2_buffer is not None
    tok_spec = _head_row_spec(tm, GLA_DV) if head_major else _row_spec(tm, TOK_WIDTH)
    in_specs = ([tok_spec, tok_spec, _row_spec(tm, MEM_WIDTH), _row_spec(tm, D_MODEL)]
                + [_full_spec(w.shape) for w in ws])
    args = [tok, gate, att, x, *ws]
    if aliased:
        in_specs.append(pl.BlockSpec(memory_space=pl.ANY))
        args.append(h2_buffer)
    return pl.pallas_call(
        functools.partial(_out_proj_kernel, head_major=head_major, aliased=int(aliased)), grid=(m // tm,),
        in_specs=in_specs,
        out_specs=[_row_spec(tm, D_MODEL), _row_spec(tm, D_MODEL, h2_row_offset // tm), _row_spec(tm, ROUTER_LANES)],
        out_shape=[jax.ShapeDtypeStruct((m, D_MODEL), F32), jax.ShapeDtypeStruct((h2_rows, D_MODEL), F32),
                   jax.ShapeDtypeStruct((m, ROUTER_LANES), F32)],
        input_output_aliases={len(args) - 1: 1} if aliased else {},
        compiler_params=_cparams("parallel"), name="out_proj")(*args)


def _moe_kernel(tile_ref, exp_ref, nitem_ref, lo_ref, hi_ref, x_ref, gate_ref, wu_ref, wd_ref, o_ref, wu_scr, wd_scr):
    w = pl.program_id(0)
    prev = jnp.maximum(w - 1, 0)
    e = exp_ref[w]
    valid = w < nitem_ref[0]

    @pl.when(jnp.logical_and(valid, jnp.logical_or(w == 0, e != exp_ref[prev])))
    def _():
        wu_scr[...] = wu_ref[0, 0].astype(BF16)
        wd_scr[...] = wd_ref[0, 0].astype(BF16)

    @pl.when(jnp.logical_or(w == 0, tile_ref[w] != tile_ref[prev]))
    def _():
        o_ref[...] = jnp.zeros(o_ref.shape, F32)

    @pl.when(valid)
    def _():
        gu = _dot(x_ref[...].astype(BF16), wu_scr[...])
        g = gu[:, :EXPERT_FF]
        act = (g * _sigmoid(g) * gu[:, EXPERT_FF:]).astype(BF16)
        out = _dot(act, wd_scr[...]) * gate_ref[...]
        rows = tile_ref[w] * MOE_BLOCK + lax.broadcasted_iota(jnp.int32, (MOE_BLOCK, 1), 0)
        mine = jnp.logical_and(rows >= lo_ref[e], rows < hi_ref[e])
        o_ref[...] = o_ref[...] + jnp.where(mine, out, 0.0)


def _moe_ffn(xs, row_gate, item_tile, item_expert, n_items, lo, hi, w_up, w_down, layer):
    tm = MOE_BLOCK
    n_work = item_tile.shape[0]
    a = xs.shape[0]
    row_map = lambda w, tile, ex, ni, lo_, hi_: (tile[w], 0)
    exp_map = lambda w, tile, ex, ni, lo_, hi_: (layer, ex[w], 0, 0)
    grid_spec = pltpu.PrefetchScalarGridSpec(
        num_scalar_prefetch=5, grid=(n_work,),
        in_specs=[pl.BlockSpec((tm, D_MODEL), row_map),
                  pl.BlockSpec((tm, 1), row_map),
                  pl.BlockSpec((1, 1, D_MODEL, 2 * EXPERT_FF), exp_map),
                  pl.BlockSpec((1, 1, EXPERT_FF, D_MODEL), exp_map)],
        out_specs=pl.BlockSpec((tm, D_MODEL), row_map),
        scratch_shapes=[pltpu.VMEM((D_MODEL, 2 * EXPERT_FF), BF16), pltpu.VMEM((EXPERT_FF, D_MODEL), BF16)])
    return pl.pallas_call(
        _moe_kernel, grid_spec=grid_spec,
        out_shape=jax.ShapeDtypeStruct((a, D_MODEL), F32),
        compiler_params=_cparams("arbitrary"), name="moe_ffn")(
            item_tile, item_expert, n_items, lo, hi, xs, row_gate, w_up, w_down)


def _route(logits, n_prompt, tm):
    m = logits.shape[0]
    n_sample = m - n_prompt
    a = 2 * m
    gl = logits[:, :N_GROUPS]
    el = logits[:, N_GROUPS:N_GROUPS + N_EXPERTS].reshape(m, N_GROUPS, EXPERTS_PER_GROUP)
    group = jnp.argmax(gl, -1).astype(jnp.int32)
    p_group = jnp.max(jax.nn.softmax(gl, -1), -1, keepdims=True)
    in_group = jnp.take_along_axis(el, group[:, None, None], axis=1)[:, 0]
    top_val, top_idx = lax.top_k(in_group, 2)
    gate = p_group * jax.nn.softmax(top_val, -1)
    expert = group[:, None] * EXPERTS_PER_GROUP + top_idx.astype(jnp.int32)

    def by_id(t):
        return jnp.concatenate([t[:n_prompt, 0], t[:n_prompt, 1], t[n_prompt:, 0], t[n_prompt:, 1]])

    tok_of_id = jnp.asarray(np.concatenate([np.arange(n_prompt), np.arange(n_prompt),
                                            n_prompt + np.arange(n_sample), n_prompt + np.arange(n_sample)]), jnp.int32)
    flat_e = by_id(expert)
    ids = jnp.arange(a, dtype=jnp.int32)
    _, order, gate_sorted, tok_sorted = lax.sort((flat_e, ids, by_id(gate), tok_of_id), num_keys=1, is_stable=True)
    _, inv = lax.sort((order, ids), num_keys=1)
    experts = jnp.arange(N_EXPERTS, dtype=jnp.int32)
    counts = jnp.sum((flat_e[:, None] == experts[None, :]).astype(jnp.int32), axis=0)
    hi = jnp.cumsum(counts).astype(jnp.int32)
    lo = hi - counts
    n_tiles = a // tm
    first_tile = lo // tm
    tiles_of = jnp.where(counts > 0, (hi - 1) // tm - first_tile + 1, 0)
    item_end = jnp.cumsum(tiles_of).astype(jnp.int32)
    n_items = item_end[-1:]
    n_work = n_tiles + N_EXPERTS - 1
    w = jnp.minimum(jnp.arange(n_work, dtype=jnp.int32), n_items[0] - 1)
    item_expert = jnp.sum((item_end[None, :] <= w[:, None]).astype(jnp.int32), axis=1)
    onehot = (item_expert[:, None] == experts[None, :]).astype(jnp.int32)
    item_tile = jnp.sum(onehot * (first_tile - (item_end - tiles_of))[None, :], axis=1) + w
    return tok_sorted, gate_sorted.reshape(a, 1), inv, item_tile.astype(jnp.int32), item_expert, n_items, lo, hi


def _combine_kernel(x_ref, r0_ref, r1_ref, g_ref, x_out, h_out):
    x2 = x_ref[...] + (r0_ref[...] + r1_ref[...])
    x_out[...] = x2
    h_out[...] = _rms(x2, g_ref[...])


def _combine(x1, gathered, first_row, g, tm):
    m = x1.shape[0]
    spec = _row_spec(tm, D_MODEL)
    return pl.pallas_call(
        _combine_kernel, grid=(m // tm,),
        in_specs=[spec, _row_spec(tm, D_MODEL, first_row // tm), _row_spec(tm, D_MODEL, (first_row + m) // tm),
                  _full_spec((1, D_MODEL))],
        out_specs=[spec, spec],
        out_shape=[jax.ShapeDtypeStruct((m, D_MODEL), F32)] * 2,
        compiler_params=_cparams("parallel"), name="moe_combine")(x1, gathered, gathered, g.reshape(1, D_MODEL))


def _to_scan_layout(t2d, b, t):
    return t2d.reshape(b, t, RWKV_HEADS, RWKV_N).transpose(1, 3, 0, 2).reshape(t, RWKV_N, b * RWKV_HEADS)


def _from_scan_layout(y, b, t):
    return y.reshape(t, RWKV_N, b, RWKV_HEADS).transpose(2, 0, 3, 1).reshape(b * t, TOK_WIDTH)


def _scan_param(p, b):
    return jnp.tile(p.reshape(RWKV_HEADS, RWKV_N).T, (1, b))


def _state_to_scan(s):
    b = s.shape[0]
    return s.transpose(2, 3, 0, 1).reshape(RWKV_N, RWKV_N, b * RWKV_HEADS)


def _state_from_scan(s, b):
    return s.reshape(RWKV_N, RWKV_N, b, RWKV_HEADS).transpose(2, 3, 0, 1)


def kernel(x_prompt, x_sample, mem_prompt, state_rwkv_S, state_rwkv_shift, state_gla_S, cache_mem_k, cache_mem_v, norm_mix_g, norm_ffn_g, norm_mem_g, norm_final_g, w_in, w_out, w_mem_kv, rw_mu, rw_w0, rw_w1, rw_w2, rw_a0, rw_a1, rw_a2, rw_g1, rw_g2, rw_k_k, rw_k_a, rw_r_k, rw_ln_g, rw_ln_b, gla_a1, gla_a2, gla_ab, gla_norm_g, router_wg, router_bg, router_we, router_be, exp_w_up, exp_w_down):
    bp, tp, _ = x_prompt.shape
    bs, ts, _ = x_sample.shape
    assert ts == 1 and tp % GLA_CHUNK == 0 and tp % SCAN_TIME_BLOCK == 0
    np_ = bp * tp
    ns = bs * ts
    m = np_ + ns
    assert np_ % LIGHT_BLOCK == 0 and ns % SAMPLE_BLOCK == 0 and (2 * m) % MOE_BLOCK == 0
    depth = w_in.shape[0]
    nh = GLA_HEADS
    tw = TOK_WIDTH
    bf = lambda t_: t_.astype(BF16)

    mem2d = mem_prompt.reshape(bp * N_MEM, D_MODEL)
    mem_kv = [_norm_matmul(mem2d, norm_mem_g[i], w_mem_kv[i], 512) for i in range(depth)]
    pk = [kv[:, :MEM_WIDTH].reshape(bp, N_MEM, MEM_WIDTH) for kv in mem_kv]
    pv = [kv[:, MEM_WIDTH:].reshape(bp, N_MEM, MEM_WIDTH) for kv in mem_kv]
    prompt_mem_k = jnp.stack(pk).reshape(depth, bp, N_MEM, MEM_HEADS, MEM_HEAD_DIM)
    prompt_mem_v = jnp.stack(pv).reshape(depth, bp, N_MEM, MEM_HEADS, MEM_HEAD_DIM)
    sk = cache_mem_k.reshape(depth, bs, N_MEM, MEM_WIDTH)
    sv = cache_mem_v.reshape(depth, bs, N_MEM, MEM_WIDTH)

    x_p = x_prompt.reshape(np_, D_MODEL)
    x_s = x_sample.reshape(ns, D_MODEL)
    h_p = _norm(x_p, norm_mix_g[0], LIGHT_BLOCK)
    h_s = _norm(x_s, norm_mix_g[0], SAMPLE_BLOCK)

    w_router = jnp.zeros((depth, D_MODEL, ROUTER_LANES), F32)
    w_router = w_router.at[:, :, :N_GROUPS].set(router_wg).at[:, :, N_GROUPS:N_GROUPS + N_EXPERTS].set(router_we)
    b_router = jnp.zeros((depth, 1, ROUTER_LANES), F32)
    b_router = b_router.at[:, 0, :N_GROUPS].set(router_bg).at[:, 0, N_GROUPS:N_GROUPS + N_EXPERTS].set(router_be)

    p_rw_S, p_rw_shift, p_gla_S, s_rw_S, s_rw_shift, s_gla_S = [], [], [], [], [], []
    for i in range(depth):
        j = i // 2
        if i % 2 == 0:
            wi = w_in[i]
            ws = [rw_mu[j], bf(wi[:, :tw]), bf(wi[:, tw:2 * tw]), bf(wi[:, 2 * tw:3 * tw]), bf(wi[:, 3 * tw:]),
                  bf(rw_w1[j]), bf(rw_w2[j]), rw_w0[j].reshape(1, tw), bf(rw_a1[j]), bf(rw_a2[j]),
                  rw_a0[j].reshape(1, tw), bf(rw_g1[j]), bf(rw_g2[j])]
            *rkvda_p, gate_p, qm_p = _rwkv_proj(h_p, None, ws, PROJ_BLOCK, tp)
            *rkvda_s, gate_s, qm_s = _rwkv_proj(h_s, state_rwkv_shift[j], ws, SAMPLE_BLOCK, ts)
            pvec = [rw_k_k[j], rw_k_a[j], rw_r_k[j], rw_ln_g[j], rw_ln_b[j]]
            yp, sp = _rwkv_scan(*[_to_scan_layout(t_, bp, tp) for t_ in rkvda_p], [_scan_param(p, bp) for p in pvec],
                                jnp.zeros((RWKV_N, RWKV_N, bp * RWKV_HEADS), F32), bp * RWKV_HEADS, SCAN_TIME_BLOCK)
            ys, ss = _rwkv_scan(*[_to_scan_layout(t_, bs, ts) for t_ in rkvda_s], [_scan_param(p, bs) for p in pvec],
                                _state_to_scan(state_rwkv_S[j]), LANES, 1)
            tok_p, tok_s = _from_scan_layout(yp, bp, tp), _from_scan_layout(ys, bs, ts)
            p_rw_S.append(_state_from_scan(sp, bp))
            s_rw_S.append(_state_from_scan(ss, bs))
            p_rw_shift.append(h_p.reshape(bp, tp, D_MODEL)[:, -1])
            s_rw_shift.append(h_s)
            head_major = False
        else:
            ws = _gla_weights(w_in[i], gla_a1[j], gla_a2[j], gla_ab[j])
            q_p, k_p, v_p, gate_p, la_p, qm_p = _gla_proj(h_p, ws, PROJ_BLOCK)
            q_s, k_s, v_s, gate_s, la_s, qm_s = _gla_proj(h_s, ws, SAMPLE_BLOCK)
            tok_p, sp_t = _gla_chunk_scan(q_p, k_p, v_p, la_p, gla_norm_g[j], bp, tp)
            col = lambda t_: t_.transpose(1, 0, 2).reshape(bs * nh, GLA_DK, 1)
            os_, ss = _gla_step(col(q_s), col(k_s), v_s.transpose(1, 0, 2).reshape(bs * nh, 1, GLA_DV), col(la_s),
                                gla_norm_g[j], state_gla_S[j].reshape(bs * nh, GLA_DK, GLA_DV))
            tok_s = os_.reshape(bs, nh, GLA_DV).transpose(1, 0, 2)
            p_gla_S.append(sp_t.transpose(0, 1, 3, 2))
            s_gla_S.append(ss.reshape(bs, nh, GLA_DK, GLA_DV))
            head_major = True

        att_p = _mem_attn(qm_p.reshape(bp, tp, MEM_WIDTH), pk[i], pv[i], 512).reshape(np_, MEM_WIDTH)
        att_s = _mem_attn(qm_s.reshape(bs, ts, MEM_WIDTH), sk[i], sv[i], 1).reshape(ns, MEM_WIDTH)
        ws = [bf(w_out[i]), norm_ffn_g[i].reshape(1, D_MODEL), w_router[i], b_router[i]]
        x1_p, h2, logits_p = _out_proj(tok_p, gate_p, att_p, x_p, ws, head_major, PROJ_BLOCK, m, 0)
        x1_s, h2, logits_s = _out_proj(tok_s, gate_s, att_s, x_s, ws, head_major, SAMPLE_BLOCK, m, np_, h2_buffer=h2)
        tok_sorted, gate_sorted, inv, item_tile, item_expert, n_items, lo, hi = _route(
            jnp.concatenate([logits_p, logits_s], axis=0), np_, MOE_BLOCK)
        rows = _moe_ffn(h2[tok_sorted], gate_sorted, item_tile, item_expert, n_items, lo, hi,
                        exp_w_up, exp_w_down, i)
        gathered = rows[inv]
        g_next = norm_mix_g[i + 1] if i + 1 < depth else norm_final_g
        x_p, h_p = _combine(x1_p, gathered, 0, g_next, LIGHT_BLOCK)
        x_s, h_s = _combine(x1_s, gathered, 2 * np_, g_next, SAMPLE_BLOCK)

    y_prompt = h_p.reshape(bp, tp, D_MODEL)
    y_sample = h_s.reshape(bs, ts, D_MODEL)
    return (y_prompt, y_sample, jnp.stack(p_rw_S), jnp.stack(p_rw_shift), jnp.stack(p_gla_S),
            prompt_mem_k, prompt_mem_v, jnp.stack(s_rw_S), jnp.stack(s_rw_shift), jnp.stack(s_gla_S))
```

```python
import functools

import numpy as np
import jax
import jax.numpy as jnp
from jax import lax
from jax.experimental import pallas as pl
from jax.experimental.pallas import tpu as pltpu

F32 = jnp.float32
BF16 = jnp.bfloat16
HIGHEST = lax.Precision.HIGHEST

D_MODEL = 1024
TOK_WIDTH = 768
MEM_WIDTH = 256
MEM_HEADS = 4
MEM_HEAD_DIM = 64
N_MEM = 256
RWKV_HEADS = 12
RWKV_N = 64
RWKV_GN_EPS = 64e-5
GLA_HEADS = 4
GLA_KW = 384
GLA_DK = 96
GLA_DV = 192
GLA_TAU = 16.0
GLA_CHUNK = 64
GLA_TILE = 8
N_GROUPS = 4
EXPERTS_PER_GROUP = 8
N_EXPERTS = 32
EXPERT_FF = 512
NORM_EPS = 1e-6
ROUTER_LANES = 128
LANES = 128
SUBLANES = 8

PROJ_BLOCK = 512
LIGHT_BLOCK = 512
SAMPLE_BLOCK = 128
MOE_BLOCK = 256
SCAN_TIME_BLOCK = 32
VMEM_LIMIT = 56 * 1024 * 1024


def _cparams(*sem):
    return pltpu.CompilerParams(dimension_semantics=sem, vmem_limit_bytes=VMEM_LIMIT)


def _dot(a, b):
    return jnp.dot(a, b, preferred_element_type=F32)


def _dot_nt(a, b):
    return lax.dot_general(a, b, (((1,), (1,)), ((), ())), preferred_element_type=F32)


def _dot_tn(a, b):
    return lax.dot_general(a, b, (((0,), (0,)), ((), ())), preferred_element_type=F32)


def _rms(x, g):
    return x * lax.rsqrt(jnp.mean(x * x, axis=-1, keepdims=True) + NORM_EPS) * g


def _sigmoid(x):
    return 1.0 / (1.0 + jnp.exp(-x))


def _row_spec(tm, n, offset=0):
    return pl.BlockSpec((tm, n), lambda i: (i + offset, 0))


def _head_row_spec(tm, n):
    return pl.BlockSpec((GLA_HEADS, tm, n), lambda i: (0, i, 0))


def _full_spec(shape):
    nd = len(shape)
    return pl.BlockSpec(shape, lambda *_: (0,) * nd)


def _norm_kernel(x_ref, g_ref, o_ref):
    o_ref[...] = _rms(x_ref[...], g_ref[...])


def _norm(x, g, tm):
    m, d = x.shape
    return pl.pallas_call(
        _norm_kernel, grid=(m // tm,),
        in_specs=[_row_spec(tm, d), _full_spec((1, d))],
        out_specs=_row_spec(tm, d),
        out_shape=jax.ShapeDtypeStruct((m, d), F32),
        compiler_params=_cparams("parallel"), name="rms_norm")(x, g.reshape(1, d))


def _norm_matmul_kernel(x_ref, g_ref, w_ref, o_ref):
    o_ref[...] = _dot(_rms(x_ref[...], g_ref[...]).astype(BF16), w_ref[...])


def _norm_matmul(x, g, w, tm):
    m, d = x.shape
    n = w.shape[1]
    return pl.pallas_call(
        _norm_matmul_kernel, grid=(m // tm,),
        in_specs=[_row_spec(tm, d), _full_spec((1, d)), _full_spec((d, n))],
        out_specs=_row_spec(tm, n),
        out_shape=jax.ShapeDtypeStruct((m, n), F32),
        compiler_params=_cparams("parallel"), name="norm_matmul")(x, g.reshape(1, d), w.astype(BF16))


def _rwkv_proj_kernel(h_ref, hp_ref, mu_ref, wr_ref, wk_ref, wv_ref, wq_ref, w1_ref, w2_ref, w0_ref,
                      a1_ref, a2_ref, a0_ref, g1_ref, g2_ref,
                      r_out, k_out, v_out, d_out, a_out, g_out, q_out, *, blocks_per_seq):
    h = h_ref[...]
    if blocks_per_seq:
        seq_start = (pl.program_id(0) % blocks_per_seq) == 0
        before = jnp.where(seq_start, 0.0, hp_ref[SUBLANES - 1:SUBLANES, :])
        row = lax.broadcasted_iota(jnp.int32, (h.shape[0], 1), 0)
        hp = jnp.where(row == 0, before, pltpu.roll(h, 1, axis=0))
    else:
        hp = hp_ref[...]
    xx = hp - h

    def mix(j):
        return (h + xx * mu_ref[j:j + 1, :]).astype(BF16)

    r_out[...] = _dot(mix(0), wr_ref[...])
    wl = w0_ref[...] + _dot(jnp.tanh(_dot(mix(1), w1_ref[...])).astype(BF16), w2_ref[...])
    k_out[...] = _dot(mix(2), wk_ref[...])
    v_out[...] = _dot(mix(3), wv_ref[...])
    al = a0_ref[...] + _dot(_dot(mix(4), a1_ref[...]).astype(BF16), a2_ref[...])
    g_out[...] = _dot(_sigmoid(_dot(mix(5), g1_ref[...])).astype(BF16), g2_ref[...])
    q_out[...] = _dot(h.astype(BF16), wq_ref[...])
    z = -wl
    softplus = jnp.maximum(z, 0.0) + jnp.log(1.0 + jnp.exp(-jnp.abs(z)))
    d_out[...] = jnp.exp(-jnp.exp(-softplus - 0.5))
    a_out[...] = _sigmoid(al)


def _rwkv_proj(h, h_prev, ws, tm, seq_len):
    m = h.shape[0]
    tw = TOK_WIDTH
    if h_prev is None:
        assert seq_len % tm == 0 and tm % SUBLANES == 0
        per8 = tm // SUBLANES
        hp_spec = pl.BlockSpec((SUBLANES, D_MODEL), lambda i: (jnp.maximum(i * per8 - 1, 0), 0))
        h_prev, blocks_per_seq = h, seq_len // tm
    else:
        hp_spec, blocks_per_seq = _row_spec(tm, D_MODEL), 0
    outs = [jax.ShapeDtypeStruct((m, tw), F32)] * 6 + [jax.ShapeDtypeStruct((m, MEM_WIDTH), F32)]
    return pl.pallas_call(
        functools.partial(_rwkv_proj_kernel, blocks_per_seq=blocks_per_seq), grid=(m // tm,),
        in_specs=[_row_spec(tm, D_MODEL), hp_spec] + [_full_spec(w.shape) for w in ws],
        out_specs=[_row_spec(tm, tw)] * 6 + [_row_spec(tm, MEM_WIDTH)],
        out_shape=outs,
        compiler_params=_cparams("parallel"), name="rwkv_proj")(h, h_prev, *ws)


def _pad_lanes(x):
    short = LANES - x.shape[-1]
    if short == 0:
        return x
    return jnp.concatenate([x, jnp.zeros(x.shape[:-1] + (short,), x.dtype)], axis=-1)


def _rwkv_scan_kernel(r_ref, k_ref, v_ref, d_ref, a_ref, kkp_ref, kap_ref, rkp_ref, lng_ref, lnb_ref, s0_ref,
                      y_ref, sfin_ref, s_scr, v_scr, yrow_scr, *, tc):
    n = RWKV_N
    nl = r_ref.shape[-1]

    @pl.when(pl.program_id(1) == 0)
    def _():
        s_scr[...] = _pad_lanes(s0_ref[...])

    kkp = _pad_lanes(kkp_ref[...])
    kap = _pad_lanes(kap_ref[...])
    rkp = _pad_lanes(rkp_ref[...])
    lng = _pad_lanes(lng_ref[...])
    lnb = _pad_lanes(lnb_ref[...])

    def step(t, carry):
        r_t = _pad_lanes(r_ref[t])
        k_t = _pad_lanes(k_ref[t])
        v_t = _pad_lanes(v_ref[t])
        d_t = _pad_lanes(d_ref[t])
        a_t = _pad_lanes(a_ref[t])
        v_scr[...] = v_t
        kkr = k_t * kkp
        nrm = jnp.maximum(jnp.sqrt(jnp.sum(kkr * kkr, axis=0, keepdims=True)), 1e-12)
        kk = kkr * (1.0 / nrm)
        k2 = k_t * (1.0 + (a_t - 1.0) * kap)
        nkk = -kk
        b_t = kk * a_t

        def ibody(i, c):
            s_i = s_scr[i]
            sa = jnp.sum(s_i * nkk, axis=0, keepdims=True)
            v_i = v_scr[pl.ds(i, 1), :]
            s_n = s_i * d_t + sa * b_t + v_i * k2
            s_scr[i] = s_n
            yrow_scr[pl.ds(i, 1), :] = jnp.sum(s_n * r_t, axis=0, keepdims=True)
            return c

        lax.fori_loop(0, n, ibody, 0, unroll=8)
        y = yrow_scr[...]
        yc = y - jnp.mean(y, axis=0, keepdims=True)
        var = jnp.mean(yc * yc, axis=0, keepdims=True)
        gn = yc * lax.rsqrt(var + RWKV_GN_EPS) * lng + lnb
        bonus = jnp.sum(r_t * k2 * rkp, axis=0, keepdims=True) * v_t
        y_ref[t] = (gn + bonus)[:, :nl]
        return carry

    lax.fori_loop(0, tc, step, 0)

    @pl.when(pl.program_id(1) == pl.num_programs(1) - 1)
    def _():
        sfin_ref[...] = s_scr[:, :, :nl]


def _rwkv_scan(r, k, v, d, a, params, s0, lane_block, tc):
    t, n, l = r.shape
    seq = pl.BlockSpec((tc, n, lane_block), lambda li, ti: (ti, 0, li))
    par = pl.BlockSpec((n, lane_block), lambda li, ti: (0, li))
    st = pl.BlockSpec((n, n, lane_block), lambda li, ti: (0, 0, li))
    return pl.pallas_call(
        functools.partial(_rwkv_scan_kernel, tc=tc), grid=(l // lane_block, t // tc),
        in_specs=[seq] * 5 + [par] * 5 + [st],
        out_specs=[seq, st],
        out_shape=[jax.ShapeDtypeStruct((t, n, l), F32), jax.ShapeDtypeStruct((n, n, l), F32)],
        scratch_shapes=[pltpu.VMEM((n, n, LANES), F32), pltpu.VMEM((n, LANES), F32), pltpu.VMEM((n, LANES), F32)],
        compiler_params=_cparams("parallel", "arbitrary"), name="rwkv_scan")(r, k, v, d, a, *params, s0)


def _gla_proj_kernel(h_ref, wq_ref, wk_ref, wv_ref, wr_ref, wm_ref, a1_ref, a2_ref, ab_ref,
                     q_out, k_out, v_out, r_out, la_out, qm_out):
    hb = h_ref[...].astype(BF16)
    low = _dot(hb, a1_ref[...]).astype(BF16)
    for hd in range(GLA_HEADS):
        q_out[hd] = _dot(hb, wq_ref[hd])
        k_out[hd] = _dot(hb, wk_ref[hd])
        v_out[hd] = _dot(hb, wv_ref[hd])
        r_out[hd] = _dot(hb, wr_ref[hd])
        x = _dot(low, a2_ref[hd]) + ab_ref[hd]
        log_sigmoid = jnp.minimum(x, 0.0) - jnp.log(1.0 + jnp.exp(-jnp.abs(x)))
        la_out[hd] = log_sigmoid / GLA_TAU
    qm_out[...] = _dot(hb, wm_ref[...])


def _gla_weights(w_in, a1, a2, ab):
    kw, tw, nh = GLA_KW, TOK_WIDTH, GLA_HEADS

    def heads(w, d):
        return w.reshape(w.shape[0], nh, d).transpose(1, 0, 2)

    return [heads(w_in[:, :kw], GLA_DK).astype(BF16), heads(w_in[:, kw:2 * kw], GLA_DK).astype(BF16),
            heads(w_in[:, 2 * kw:2 * kw + tw], GLA_DV).astype(BF16),
            heads(w_in[:, 2 * kw + tw:3 * tw], GLA_DV).astype(BF16), w_in[:, 3 * tw:].astype(BF16),
            a1.astype(BF16), heads(a2, GLA_DK).astype(BF16), ab.reshape(nh, 1, GLA_DK)]


def _gla_proj(h, ws, tm):
    m = h.shape[0]
    nh = GLA_HEADS
    widths = [GLA_DK, GLA_DK, GLA_DV, GLA_DV, GLA_DK]
    return pl.pallas_call(
        _gla_proj_kernel, grid=(m // tm,),
        in_specs=[_row_spec(tm, D_MODEL)] + [_full_spec(w.shape) for w in ws],
        out_specs=[_head_row_spec(tm, w) for w in widths] + [_row_spec(tm, MEM_WIDTH)],
        out_shape=[jax.ShapeDtypeStruct((nh, m, w), F32) for w in widths] + [jax.ShapeDtypeStruct((m, MEM_WIDTH), F32)],
        compiler_params=_cparams("parallel"), name="gla_proj")(h, *ws)


def _gla_out_norm(o, g):
    return o * lax.rsqrt(jnp.mean(o * o, axis=-1, keepdims=True) + NORM_EPS) * g


def _gla_chunk_kernel(q_ref, k_ref, v_ref, la_ref, ng_ref, o_ref, sfin_ref, st_scr):
    c, dk = GLA_CHUNK, GLA_DK

    @pl.when(pl.program_id(1) == 0)
    def _():
        st_scr[...] = jnp.zeros(st_scr.shape, F32)

    row = lax.broadcasted_iota(jnp.int32, (c, c), 0)
    col = lax.broadcasted_iota(jnp.int32, (c, c), 1)
    tril = (row >= col).astype(F32)
    rr = lax.broadcasted_iota(jnp.int32, (c, dk), 0)
    ones_sum = jnp.ones((dk, LANES), BF16)

    for hd in range(GLA_HEADS):
        la = la_ref[hd]
        k = k_ref[hd]
        vb = v_ref[hd].astype(BF16)
        q = q_ref[hd] * (dk ** -0.5)
        b = jnp.dot(tril, la, precision=HIGHEST, preferred_element_type=F32)
        b_end = b[c - 1:c, :]
        st = st_scr[hd]
        inter = _dot_nt((q * jnp.exp(b)).astype(BF16), st.astype(BF16))

        att = jnp.zeros((c, c), F32)
        blk = c // 2
        while blk >= GLA_TILE:
            two = 2 * blk
            b_ref_rows = jnp.concatenate(
                [jnp.broadcast_to(b[s0 + blk - 1:s0 + blk, :], (two, dk)) for s0 in range(0, c, two)], axis=0)
            upper = (rr & (two - 1)) >= blk
            q_l = jnp.where(upper, q * jnp.exp(jnp.minimum(b - b_ref_rows, 0.0)), 0.0).astype(BF16)
            k_l = jnp.where(upper, 0.0, k * jnp.exp(jnp.minimum(b_ref_rows - b, 0.0))).astype(BF16)
            att = att + jnp.where((row ^ col) < two, _dot_nt(q_l, k_l), 0.0)
            blk //= 2

        prods = [(q * k).astype(BF16)]
        for dlt in range(1, GLA_TILE):
            k_s = pltpu.roll(k, dlt, axis=0)
            b_s = pltpu.roll(b, dlt, axis=0)
            p = q * k_s * jnp.exp(jnp.minimum(b - b_s, 0.0))
            prods.append(jnp.where((rr & (GLA_TILE - 1)) >= dlt, p, 0.0).astype(BF16))
        sums = _dot(jnp.concatenate(prods, axis=0), ones_sum)
        for dlt in range(GLA_TILE):
            att = att + jnp.where(col == row - dlt, sums[dlt * c:(dlt + 1) * c, :c], 0.0)

        o = inter + _dot(att.astype(BF16), vb)
        o_ref[hd] = _gla_out_norm(o, ng_ref[...])
        kd = (k * jnp.exp(b_end - b)).astype(BF16)
        st_scr[hd] = jnp.exp(b_end) * st + _dot_tn(vb, kd)

    @pl.when(pl.program_id(1) == pl.num_programs(1) - 1)
    def _():
        sfin_ref[0] = st_scr[...]


def _gla_chunk_scan(q, k, v, la, norm_g, batch, t):
    nh, dk, dv, c = GLA_HEADS, GLA_DK, GLA_DV, GLA_CHUNK
    nc = t // c
    kspec = pl.BlockSpec((nh, c, dk), lambda i, j: (0, i * nc + j, 0))
    vspec = pl.BlockSpec((nh, c, dv), lambda i, j: (0, i * nc + j, 0))
    sspec = pl.BlockSpec((1, nh, dv, dk), lambda i, j: (i, 0, 0, 0))
    return pl.pallas_call(
        _gla_chunk_kernel, grid=(batch, nc),
        in_specs=[kspec, kspec, vspec, kspec, pl.BlockSpec((1, dv), lambda i, j: (0, 0))],
        out_specs=[vspec, sspec],
        out_shape=[jax.ShapeDtypeStruct((nh, batch * t, dv), F32), jax.ShapeDtypeStruct((batch, nh, dv, dk), F32)],
        scratch_shapes=[pltpu.VMEM((nh, dv, dk), F32)],
        compiler_params=_cparams("parallel", "arbitrary"), name="gla_chunk")(q, k, v, la, norm_g.reshape(1, dv))


GLA_STEP_GROUP = 8


def _gla_step_kernel(q_ref, k_ref, v_ref, la_ref, ng_ref, s0_ref, o_ref, s_ref):
    for g in range(GLA_STEP_GROUP):
        s_new = jnp.exp(la_ref[g]) * s0_ref[g] + k_ref[g] * v_ref[g]
        s_ref[g] = s_new
        o = jnp.sum((q_ref[g] * (GLA_DK ** -0.5)) * s_new, axis=0, keepdims=True)
        o_ref[g] = _gla_out_norm(o, ng_ref[...])


def _gla_step(q, k, v, la, norm_g, s0):
    bh, dk, _ = q.shape
    dv = v.shape[2]
    g = GLA_STEP_GROUP
    cspec = pl.BlockSpec((g, dk, 1), lambda i: (i, 0, 0))
    vspec = pl.BlockSpec((g, 1, dv), lambda i: (i, 0, 0))
    sspec = pl.BlockSpec((g, dk, dv), lambda i: (i, 0, 0))
    return pl.pallas_call(
        _gla_step_kernel, grid=(bh // g,),
        in_specs=[cspec, cspec, vspec, cspec, _full_spec((1, dv)), sspec],
        out_specs=[vspec, sspec],
        out_shape=[jax.ShapeDtypeStruct((bh, 1, dv), F32), jax.ShapeDtypeStruct((bh, dk, dv), F32)],
        compiler_params=_cparams("parallel"), name="gla_step")(q, k, v, la, norm_g.reshape(1, dv), s0)


def _mem_attn_kernel(q_ref, k_ref, v_ref, o_ref):
    q = q_ref[0]
    k = k_ref[0].astype(BF16)
    v_ones = jnp.concatenate([v_ref[0].astype(BF16), jnp.ones((N_MEM, LANES), BF16)], axis=1)
    head_of_lane = lax.broadcasted_iota(jnp.int32, (1, MEM_WIDTH), 1) // MEM_HEAD_DIM
    out = jnp.zeros(q.shape, F32)
    for h in range(MEM_HEADS):
        mine = head_of_lane == h
        qh = jnp.where(mine, q, 0.0).astype(BF16)
        s = _dot_nt(qh, k) * (MEM_HEAD_DIM ** -0.5)
        e = jnp.exp(s - jnp.max(s, axis=-1, keepdims=True)).astype(BF16)
        ev = _dot(e, v_ones)
        inv = 1.0 / ev[:, MEM_WIDTH:]
        out = out + jnp.where(mine, ev[:, :MEM_WIDTH] * jnp.concatenate([inv, inv], axis=1), 0.0)
    o_ref[0] = out


def _mem_attn(q, mem_k, mem_v, tq):
    b, t, w = q.shape
    qspec = pl.BlockSpec((1, tq, w), lambda i, j: (i, j, 0))
    mspec = pl.BlockSpec((1, N_MEM, w), lambda i, j: (i, 0, 0))
    return pl.pallas_call(
        _mem_attn_kernel, grid=(b, t // tq),
        in_specs=[qspec, mspec, mspec], out_specs=qspec,
        out_shape=jax.ShapeDtypeStruct((b, t, w), F32),
        compiler_params=_cparams("parallel", "parallel"), name="mem_attn")(q, mem_k, mem_v)


def _out_proj_kernel(*refs, head_major, aliased):
    tok_ref, gate_ref, att_ref, x_ref, wo_ref, g_ref, wr_ref, br_ref = refs[:8]
    x1_out, h2_out, logit_out = refs[8 + aliased:]
    x1 = x_ref[...] + _dot(att_ref[...].astype(BF16), wo_ref[TOK_WIDTH:, :])
    if head_major:
        for hd in range(GLA_HEADS):
            gate = gate_ref[hd]
            mixed = (tok_ref[hd] * (gate * _sigmoid(gate))).astype(BF16)
            x1 = x1 + _dot(mixed, wo_ref[hd * GLA_DV:(hd + 1) * GLA_DV, :])
    else:
        x1 = x1 + _dot((tok_ref[...] * gate_ref[...]).astype(BF16), wo_ref[:TOK_WIDTH, :])
    x1_out[...] = x1
    h2 = _rms(x1, g_ref[...])
    h2_out[...] = h2
    logit_out[...] = jnp.dot(h2, wr_ref[...], precision=HIGHEST, preferred_element_type=F32) + br_ref[...]


def _out_proj(tok, gate, att, x, ws, head_major, tm, h2_rows, h2_row_offset, h2_buffer=None):
    m = x.shape[0]
    aliased = h2_buffer is not None
    tok_spec = _head_row_spec(tm, GLA_DV) if head_major else _row_spec(tm, TOK_WIDTH)
    in_specs = ([tok_spec, tok_spec, _row_spec(tm, MEM_WIDTH), _row_spec(tm, D_MODEL)]
                + [_full_spec(w.shape) for w in ws])
    args = [tok, gate, att, x, *ws]
    if aliased:
        in_specs.append(pl.BlockSpec(memory_space=pl.ANY))
        args.append(h2_buffer)
    return pl.pallas_call(
        functools.partial(_out_proj_kernel, head_major=head_major, aliased=int(aliased)), grid=(m // tm,),
        in_specs=in_specs,
        out_specs=[_row_spec(tm, D_MODEL), _row_spec(tm, D_MODEL, h2_row_offset // tm), _row_spec(tm, ROUTER_LANES)],
        out_shape=[jax.ShapeDtypeStruct((m, D_MODEL), F32), jax.ShapeDtypeStruct((h2_rows, D_MODEL), F32),
                   jax.ShapeDtypeStruct((m, ROUTER_LANES), F32)],
        input_output_aliases={len(args) - 1: 1} if aliased else {},
        compiler_params=_cparams("parallel"), name="out_proj")(*args)


def _moe_kernel(tile_ref, exp_ref, nitem_ref, lo_ref, hi_ref, x_ref, gate_ref, wu_ref, wd_ref, o_ref, wu_scr, wd_scr):
    w = pl.program_id(0)
    prev = jnp.maximum(w - 1, 0)
    e = exp_ref[w]
    valid = w < nitem_ref[0]

    @pl.when(jnp.logical_and(valid, jnp.logical_or(w == 0, e != exp_ref[prev])))
    def _():
        wu_scr[...] = wu_ref[0, 0].astype(BF16)
        wd_scr[...] = wd_ref[0, 0].astype(BF16)

    @pl.when(jnp.logical_or(w == 0, tile_ref[w] != tile_ref[prev]))
    def _():
        o_ref[...] = jnp.zeros(o_ref.shape, F32)

    @pl.when(valid)
    def _():
        gu = _dot(x_ref[...].astype(BF16), wu_scr[...])
        g = gu[:, :EXPERT_FF]
        act = (g * _sigmoid(g) * gu[:, EXPERT_FF:]).astype(BF16)
        out = _dot(act, wd_scr[...]) * gate_ref[...]
        rows = tile_ref[w] * MOE_BLOCK + lax.broadcasted_iota(jnp.int32, (MOE_BLOCK, 1), 0)
        mine = jnp.logical_and(rows >= lo_ref[e], rows < hi_ref[e])
        o_ref[...] = o_ref[...] + jnp.where(mine, out, 0.0)


def _moe_ffn(xs, row_gate, item_tile, item_expert, n_items, lo, hi, w_up, w_down, layer):
    tm = MOE_BLOCK
    n_work = item_tile.shape[0]
    a = xs.shape[0]
    row_map = lambda w, tile, ex, ni, lo_, hi_: (tile[w], 0)
    exp_map = lambda w, tile, ex, ni, lo_, hi_: (layer, ex[w], 0, 0)
    grid_spec = pltpu.PrefetchScalarGridSpec(
        num_scalar_prefetch=5, grid=(n_work,),
        in_specs=[pl.BlockSpec((tm, D_MODEL), row_map),
                  pl.BlockSpec((tm, 1), row_map),
                  pl.BlockSpec((1, 1, D_MODEL, 2 * EXPERT_FF), exp_map),
                  pl.BlockSpec((1, 1, EXPERT_FF, D_MODEL), exp_map)],
        out_specs=pl.BlockSpec((tm, D_MODEL), row_map),
        scratch_shapes=[pltpu.VMEM((D_MODEL, 2 * EXPERT_FF), BF16), pltpu.VMEM((EXPERT_FF, D_MODEL), BF16)])
    return pl.pallas_call(
        _moe_kernel, grid_spec=grid_spec,
        out_shape=jax.ShapeDtypeStruct((a, D_MODEL), F32),
        compiler_params=_cparams("arbitrary"), name="moe_ffn")(
            item_tile, item_expert, n_items, lo, hi, xs, row_gate, w_up, w_down)


def _route(logits, n_prompt, tm):
    m = logits.shape[0]
    n_sample = m - n_prompt
    a = 2 * m
    gl = logits[:, :N_GROUPS]
    el = logits[:, N_GROUPS:N_GROUPS + N_EXPERTS].reshape(m, N_GROUPS, EXPERTS_PER_GROUP)
    group = jnp.argmax(gl, -1).astype(jnp.int32)
    p_group = jnp.max(jax.nn.softmax(gl, -1), -1, keepdims=True)
    in_group = jnp.take_along_axis(el, group[:, None, None], axis=1)[:, 0]
    top_val, top_idx = lax.top_k(in_group, 2)
    gate = p_group * jax.nn.softmax(top_val, -1)
    expert = group[:, None] * EXPERTS_PER_GROUP + top_idx.astype(jnp.int32)

    def by_id(t):
        return jnp.concatenate([t[:n_prompt, 0], t[:n_prompt, 1], t[n_prompt:, 0], t[n_prompt:, 1]])

    tok_of_id = jnp.asarray(np.concatenate([np.arange(n_prompt), np.arange(n_prompt),
                                            n_prompt + np.arange(n_sample), n_prompt + np.arange(n_sample)]), jnp.int32)
    flat_e = by_id(expert)
    ids = jnp.arange(a, dtype=jnp.int32)
    _, order, gate_sorted, tok_sorted = lax.sort((flat_e, ids, by_id(gate), tok_of_id), num_keys=1, is_stable=True)
    _, inv = lax.sort((order, ids), num_keys=1)
    experts = jnp.arange(N_EXPERTS, dtype=jnp.int32)
    counts = jnp.sum((flat_e[:, None] == experts[None, :]).astype(jnp.int32), axis=0)
    hi = jnp.cumsum(counts).astype(jnp.int32)
    lo = hi - counts
    n_tiles = a // tm
    first_tile = lo // tm
    tiles_of = jnp.where(counts > 0, (hi - 1) // tm - first_tile + 1, 0)
    item_end = jnp.cumsum(tiles_of).astype(jnp.int32)
    n_items = item_end[-1:]
    n_work = n_tiles + N_EXPERTS - 1
    w = jnp.minimum(jnp.arange(n_work, dtype=jnp.int32), n_items[0] - 1)
    item_expert = jnp.sum((item_end[None, :] <= w[:, None]).astype(jnp.int32), axis=1)
    onehot = (item_expert[:, None] == experts[None, :]).astype(jnp.int32)
    item_tile = jnp.sum(onehot * (first_tile - (item_end - tiles_of))[None, :], axis=1) + w
    return tok_sorted, gate_sorted.reshape(a, 1), inv, item_tile.astype(jnp.int32), item_expert, n_items, lo, hi


def _combine_kernel(x_ref, r0_ref, r1_ref, g_ref, x_out, h_out):
    x2 = x_ref[...] + (r0_ref[...] + r1_ref[...])
    x_out[...] = x2
    h_out[...] = _rms(x2, g_ref[...])


def _combine(x1, gathered, first_row, g, tm):
    m = x1.shape[0]
    spec = _row_spec(tm, D_MODEL)
    return pl.pallas_call(
        _combine_kernel, grid=(m // tm,),
        in_specs=[spec, _row_spec(tm, D_MODEL, first_row // tm), _row_spec(tm, D_MODEL, (first_row + m) // tm),
                  _full_spec((1, D_MODEL))],
        out_specs=[spec, spec],
        out_shape=[jax.ShapeDtypeStruct((m, D_MODEL), F32)] * 2,
        compiler_params=_cparams("parallel"), name="moe_combine")(x1, gathered, gathered, g.reshape(1, D_MODEL))


def _to_scan_layout(t2d, b, t):
    return t2d.reshape(b, t, RWKV_HEADS, RWKV_N).transpose(1, 3, 0, 2).reshape(t, RWKV_N, b * RWKV_HEADS)


def _from_scan_layout(y, b, t):
    return y.reshape(t, RWKV_N, b, RWKV_HEADS).transpose(2, 0, 3, 1).reshape(b * t, TOK_WIDTH)


def _scan_param(p, b):
    return jnp.tile(p.reshape(RWKV_HEADS, RWKV_N).T, (1, b))


def _state_to_scan(s):
    b = s.shape[0]
    return s.transpose(2, 3, 0, 1).reshape(RWKV_N, RWKV_N, b * RWKV_HEADS)


def _state_from_scan(s, b):
    return s.reshape(RWKV_N, RWKV_N, b, RWKV_HEADS).transpose(2, 3, 0, 1)


def kernel(x_prompt, x_sample, mem_prompt, state_rwkv_S, state_rwkv_shift, state_gla_S, cache_mem_k, cache_mem_v, norm_mix_g, norm_ffn_g, norm_mem_g, norm_final_g, w_in, w_out, w_mem_kv, rw_mu, rw_w0, rw_w1, rw_w2, rw_a0, rw_a1, rw_a2, rw_g1, rw_g2, rw_k_k, rw_k_a, rw_r_k, rw_ln_g, rw_ln_b, gla_a1, gla_a2, gla_ab, gla_norm_g, router_wg, router_bg, router_we, router_be, exp_w_up, exp_w_down):
    bp, tp, _ = x_prompt.shape
    bs, ts, _ = x_sample.shape
    assert ts == 1 and tp % GLA_CHUNK == 0 and tp % SCAN_TIME_BLOCK == 0
    np_ = bp * tp
    ns = bs * ts
    m = np_ + ns
    assert np_ % LIGHT_BLOCK == 0 and ns % SAMPLE_BLOCK == 0 and (2 * m) % MOE_BLOCK == 0
    depth = w_in.shape[0]
    nh = GLA_HEADS
    tw = TOK_WIDTH
    bf = lambda t_: t_.astype(BF16)

    mem2d = mem_prompt.reshape(bp * N_MEM, D_MODEL)
    mem_kv = [_norm_matmul(mem2d, norm_mem_g[i], w_mem_kv[i], 512) for i in range(depth)]
    pk = [kv[:, :MEM_WIDTH].reshape(bp, N_MEM, MEM_WIDTH) for kv in mem_kv]
    pv = [kv[:, MEM_WIDTH:].reshape(bp, N_MEM, MEM_WIDTH) for kv in mem_kv]
    prompt_mem_k = jnp.stack(pk).reshape(depth, bp, N_MEM, MEM_HEADS, MEM_HEAD_DIM)
    prompt_mem_v = jnp.stack(pv).reshape(depth, bp, N_MEM, MEM_HEADS, MEM_HEAD_DIM)
    sk = cache_mem_k.reshape(depth, bs, N_MEM, MEM_WIDTH)
    sv = cache_mem_v.reshape(depth, bs, N_MEM, MEM_WIDTH)

    x_p = x_prompt.reshape(np_, D_MODEL)
    x_s = x_sample.reshape(ns, D_MODEL)
    h_p = _norm(x_p, norm_mix_g[0], LIGHT_BLOCK)
    h_s = _norm(x_s, norm_mix_g[0], SAMPLE_BLOCK)

    w_router = jnp.zeros((depth, D_MODEL, ROUTER_LANES), F32)
    w_router = w_router.at[:, :, :N_GROUPS].set(router_wg).at[:, :, N_GROUPS:N_GROUPS + N_EXPERTS].set(router_we)
    b_router = jnp.zeros((depth, 1, ROUTER_LANES), F32)
    b_router = b_router.at[:, 0, :N_GROUPS].set(router_bg).at[:, 0, N_GROUPS:N_GROUPS + N_EXPERTS].set(router_be)

    p_rw_S, p_rw_shift, p_gla_S, s_rw_S, s_rw_shift, s_gla_S = [], [], [], [], [], []
    for i in range(depth):
        j = i // 2
        if i % 2 == 0:
            wi = w_in[i]
            ws = [rw_mu[j], bf(wi[:, :tw]), bf(wi[:, tw:2 * tw]), bf(wi[:, 2 * tw:3 * tw]), bf(wi[:, 3 * tw:]),
                  bf(rw_w1[j]), bf(rw_w2[j]), rw_w0[j].reshape(1, tw), bf(rw_a1[j]), bf(rw_a2[j]),
                  rw_a0[j].reshape(1, tw), bf(rw_g1[j]), bf(rw_g2[j])]
            *rkvda_p, gate_p, qm_p = _rwkv_proj(h_p, None, ws, PROJ_BLOCK, tp)
            *rkvda_s, gate_s, qm_s = _rwkv_proj(h_s, state_rwkv_shift[j], ws, SAMPLE_BLOCK, ts)
            pvec = [rw_k_k[j], rw_k_a[j], rw_r_k[j], rw_ln_g[j], rw_ln_b[j]]
            yp, sp = _rwkv_scan(*[_to_scan_layout(t_, bp, tp) for t_ in rkvda_p], [_scan_param(p, bp) for p in pvec],
                                jnp.zeros((RWKV_N, RWKV_N, bp * RWKV_HEADS), F32), bp * RWKV_HEADS, SCAN_TIME_BLOCK)
            ys, ss = _rwkv_scan(*[_to_scan_layout(t_, bs, ts) for t_ in rkvda_s], [_scan_param(p, bs) for p in pvec],
                                _state_to_scan(state_rwkv_S[j]), LANES, 1)
            tok_p, tok_s = _from_scan_layout(yp, bp, tp), _from_scan_layout(ys, bs, ts)
            p_rw_S.append(_state_from_scan(sp, bp))
            s_rw_S.append(_state_from_scan(ss, bs))
            p_rw_shift.append(h_p.reshape(bp, tp, D_MODEL)[:, -1])
            s_rw_shift.append(h_s)
            head_major = False
        else:
            ws = _gla_weights(w_in[i], gla_a1[j], gla_a2[j], gla_ab[j])
            q_p, k_p, v_p, gate_p, la_p, qm_p = _gla_proj(h_p, ws, PROJ_BLOCK)
            q_s, k_s, v_s, gate_s, la_s, qm_s = _gla_proj(h_s, ws, SAMPLE_BLOCK)
            tok_p, sp_t = _gla_chunk_scan(q_p, k_p, v_p, la_p, gla_norm_g[j], bp, tp)
            col = lambda t_: t_.transpose(1, 0, 2).reshape(bs * nh, GLA_DK, 1)
            os_, ss = _gla_step(col(q_s), col(k_s), v_s.transpose(1, 0, 2).reshape(bs * nh, 1, GLA_DV), col(la_s),
                                gla_norm_g[j], state_gla_S[j].reshape(bs * nh, GLA_DK, GLA_DV))
            tok_s = os_.reshape(bs, nh, GLA_DV).transpose(1, 0, 2)
            p_gla_S.append(sp_t.transpose(0, 1, 3, 2))
            s_gla_S.append(ss.reshape(bs, nh, GLA_DK, GLA_DV))
            head_major = True

        att_p = _mem_attn(qm_p.reshape(bp, tp, MEM_WIDTH), pk[i], pv[i], 512).reshape(np_, MEM_WIDTH)
        att_s = _mem_attn(qm_s.reshape(bs, ts, MEM_WIDTH), sk[i], sv[i], 1).reshape(ns, MEM_WIDTH)
        ws = [bf(w_out[i]), norm_ffn_g[i].reshape(1, D_MODEL), w_router[i], b_router[i]]
        x1_p, h2, logits_p = _out_proj(tok_p, gate_p, att_p, x_p, ws, head_major, PROJ_BLOCK, m, 0)
        x1_s, h2, logits_s = _out_proj(tok_s, gate_s, att_s, x_s, ws, head_major, SAMPLE_BLOCK, m, np_, h2_buffer=h2)
        tok_sorted, gate_sorted, inv, item_tile, item_expert, n_items, lo, hi = _route(
            jnp.concatenate([logits_p, logits_s], axis=0), np_, MOE_BLOCK)
        rows = _moe_ffn(h2[tok_sorted], gate_sorted, item_tile, item_expert, n_items, lo, hi,
                        exp_w_up, exp_w_down, i)
        gathered = rows[inv]
        g_next = norm_mix_g[i + 1] if i + 1 < depth else norm_final_g
        x_p, h_p = _combine(x1_p, gathered, 0, g_next, LIGHT_BLOCK)
        x_s, h_s = _combine(x1_s, gathered, 2 * np_, g_next, SAMPLE_BLOCK)

    y_prompt = h_p.reshape(bp, tp, D_MODEL)
    y_sample = h_s.reshape(bs, ts, D_MODEL)
    return (y_prompt, y_sample, jnp.stack(p_rw_S), jnp.stack(p_rw_shift), jnp.stack(p_gla_S),
            prompt_mem_k, prompt_mem_v, jnp.stack(s_rw_S), jnp.stack(s_rw_shift), jnp.stack(s_gla_S))
```

```python
import functools

import numpy as np
import jax
import jax.numpy as jnp
from jax import lax
from jax.experimental import pallas as pl
from jax.experimental.pallas import tpu as pltpu

F32 = jnp.float32
BF16 = jnp.bfloat16
HIGHEST = lax.Precision.HIGHEST

D_MODEL = 1024
TOK_WIDTH = 768
MEM_WIDTH = 256
MEM_HEADS = 4
MEM_HEAD_DIM = 64
N_MEM = 256
RWKV_HEADS = 12
RWKV_N = 64
RWKV_PAIRS = RWKV_HEADS // 2
RWKV_GN_EPS = 64e-5
GLA_HEADS = 4
GLA_KW = 384
GLA_DK = 96
GLA_DV = 192
GLA_TAU = 16.0
GLA_CHUNK = 64
GLA_TILE = 8
N_GROUPS = 4
EXPERTS_PER_GROUP = 8
N_EXPERTS = 32
EXPERT_FF = 512
NORM_EPS = 1e-6
ROUTER_LANES = 128
LANES = 128
SUBLANES = 8

PROJ_BLOCK = 512
LIGHT_BLOCK = 512
SAMPLE_BLOCK = 128
MOE_BLOCK = 256
SCAN_TIME_BLOCK = 32
VMEM_LIMIT = 56 * 1024 * 1024


def _cparams(*sem):
    return pltpu.CompilerParams(dimension_semantics=sem, vmem_limit_bytes=VMEM_LIMIT)


def _dot(a, b):
    return jnp.dot(a, b, preferred_element_type=F32)


def _dot_nt(a, b):
    return lax.dot_general(a, b, (((1,), (1,)), ((), ())), preferred_element_type=F32)


def _dot_tn(a, b):
    return lax.dot_general(a, b, (((0,), (0,)), ((), ())), preferred_element_type=F32)


def _rms(x, g):
    return x * lax.rsqrt(jnp.mean(x * x, axis=-1, keepdims=True) + NORM_EPS) * g


def _sigmoid(x):
    return 1.0 / (1.0 + jnp.exp(-x))


def _row_spec(tm, n, offset=0):
    return pl.BlockSpec((tm, n), lambda i: (i + offset, 0))


def _head_row_spec(tm, n):
    return pl.BlockSpec((GLA_HEADS, tm, n), lambda i: (0, i, 0))


def _full_spec(shape):
    nd = len(shape)
    return pl.BlockSpec(shape, lambda *_: (0,) * nd)


def _norm_kernel(x_ref, g_ref, o_ref):
    o_ref[...] = _rms(x_ref[...], g_ref[...])


def _norm(x, g, tm):
    m, d = x.shape
    return pl.pallas_call(
        _norm_kernel, grid=(m // tm,),
        in_specs=[_row_spec(tm, d), _full_spec((1, d))],
        out_specs=_row_spec(tm, d),
        out_shape=jax.ShapeDtypeStruct((m, d), F32),
        compiler_params=_cparams("parallel"), name="rms_norm")(x, g.reshape(1, d))


def _norm_matmul_kernel(x_ref, g_ref, w_ref, o_ref):
    o_ref[...] = _dot(_rms(x_ref[...], g_ref[...]).astype(BF16), w_ref[...])


def _norm_matmul(x, g, w, tm):
    m, d = x.shape
    n = w.shape[1]
    return pl.pallas_call(
        _norm_matmul_kernel, grid=(m // tm,),
        in_specs=[_row_spec(tm, d), _full_spec((1, d)), _full_spec((d, n))],
        out_specs=_row_spec(tm, n),
        out_shape=jax.ShapeDtypeStruct((m, n), F32),
        compiler_params=_cparams("parallel"), name="norm_matmul")(x, g.reshape(1, d), w.astype(BF16))


def _rwkv_proj_kernel(h_ref, hp_ref, mu_ref, wr_ref, wk_ref, wv_ref, wq_ref, w1_ref, w2_ref, w0_ref,
                      a1_ref, a2_ref, a0_ref, g1_ref, g2_ref,
                      r_out, k_out, v_out, d_out, a_out, g_out, q_out, *, blocks_per_seq):
    def put(out, val):
        if blocks_per_seq:
            for p in range(RWKV_PAIRS):
                out[p] = val[:, p * LANES:(p + 1) * LANES]
        else:
            out[...] = val

    h = h_ref[...]
    if blocks_per_seq:
        seq_start = (pl.program_id(0) % blocks_per_seq) == 0
        before = jnp.where(seq_start, 0.0, hp_ref[SUBLANES - 1:SUBLANES, :])
        row = lax.broadcasted_iota(jnp.int32, (h.shape[0], 1), 0)
        hp = jnp.where(row == 0, before, pltpu.roll(h, 1, axis=0))
    else:
        hp = hp_ref[...]
    xx = hp - h

    def mix(j):
        return (h + xx * mu_ref[j:j + 1, :]).astype(BF16)

    put(r_out, _dot(mix(0), wr_ref[...]))
    wl = w0_ref[...] + _dot(jnp.tanh(_dot(mix(1), w1_ref[...])).astype(BF16), w2_ref[...])
    put(k_out, _dot(mix(2), wk_ref[...]))
    put(v_out, _dot(mix(3), wv_ref[...]))
    al = a0_ref[...] + _dot(_dot(mix(4), a1_ref[...]).astype(BF16), a2_ref[...])
    g_out[...] = _dot(_sigmoid(_dot(mix(5), g1_ref[...])).astype(BF16), g2_ref[...])
    q_out[...] = _dot(h.astype(BF16), wq_ref[...])
    z = -wl
    softplus = jnp.maximum(z, 0.0) + jnp.log(1.0 + jnp.exp(-jnp.abs(z)))
    put(d_out, jnp.exp(-jnp.exp(-softplus - 0.5)))
    put(a_out, _sigmoid(al))


def _rwkv_proj(h, h_prev, ws, tm, seq_len):
    m = h.shape[0]
    tw = TOK_WIDTH
    if h_prev is None:
        assert seq_len % tm == 0 and tm % SUBLANES == 0
        per8 = tm // SUBLANES
        hp_spec = pl.BlockSpec((SUBLANES, D_MODEL), lambda i: (jnp.maximum(i * per8 - 1, 0), 0))
        h_prev, blocks_per_seq = h, seq_len // tm
    else:
        hp_spec, blocks_per_seq = _row_spec(tm, D_MODEL), 0
    if blocks_per_seq:
        scan_spec = pl.BlockSpec((RWKV_PAIRS, tm, LANES), lambda i: (0, i, 0))
        scan_shape = jax.ShapeDtypeStruct((RWKV_PAIRS, m, LANES), F32)
    else:
        scan_spec, scan_shape = _row_spec(tm, tw), jax.ShapeDtypeStruct((m, tw), F32)
    return pl.pallas_call(
        functools.partial(_rwkv_proj_kernel, blocks_per_seq=blocks_per_seq), grid=(m // tm,),
        in_specs=[_row_spec(tm, D_MODEL), hp_spec] + [_full_spec(w.shape) for w in ws],
        out_specs=[scan_spec] * 5 + [_row_spec(tm, tw), _row_spec(tm, MEM_WIDTH)],
        out_shape=[scan_shape] * 5 + [jax.ShapeDtypeStruct((m, tw), F32), jax.ShapeDtypeStruct((m, MEM_WIDTH), F32)],
        compiler_params=_cparams("parallel"), name="rwkv_proj")(h, h_prev, *ws)


def _pad_lanes(x):
    short = LANES - x.shape[-1]
    if short == 0:
        return x
    return jnp.concatenate([x, jnp.zeros(x.shape[:-1] + (short,), x.dtype)], axis=-1)


def _rwkv_scan_kernel(r_ref, k_ref, v_ref, d_ref, a_ref, kkp_ref, kap_ref, rkp_ref, lng_ref, lnb_ref, s0_ref,
                      y_ref, sfin_ref, s_scr, v_scr, yrow_scr, *, tc):
    n = RWKV_N
    nl = r_ref.shape[-1]

    @pl.when(pl.program_id(1) == 0)
    def _():
        s_scr[...] = _pad_lanes(s0_ref[...])

    kkp = _pad_lanes(kkp_ref[...])
    kap = _pad_lanes(kap_ref[...])
    rkp = _pad_lanes(rkp_ref[...])
    lng = _pad_lanes(lng_ref[...])
    lnb = _pad_lanes(lnb_ref[...])

    def step(t, carry):
        r_t = _pad_lanes(r_ref[t])
        k_t = _pad_lanes(k_ref[t])
        v_t = _pad_lanes(v_ref[t])
        d_t = _pad_lanes(d_ref[t])
        a_t = _pad_lanes(a_ref[t])
        v_scr[...] = v_t
        kkr = k_t * kkp
        nrm = jnp.maximum(jnp.sqrt(jnp.sum(kkr * kkr, axis=0, keepdims=True)), 1e-12)
        kk = kkr * (1.0 / nrm)
        k2 = k_t * (1.0 + (a_t - 1.0) * kap)
        nkk = -kk
        b_t = kk * a_t

        def ibody(i, c):
            s_i = s_scr[i]
            sa = jnp.sum(s_i * nkk, axis=0, keepdims=True)
            v_i = v_scr[pl.ds(i, 1), :]
            s_n = s_i * d_t + sa * b_t + v_i * k2
            s_scr[i] = s_n
            yrow_scr[pl.ds(i, 1), :] = jnp.sum(s_n * r_t, axis=0, keepdims=True)
            return c

        lax.fori_loop(0, n, ibody, 0, unroll=8)
        y = yrow_scr[...]
        yc = y - jnp.mean(y, axis=0, keepdims=True)
        var = jnp.mean(yc * yc, axis=0, keepdims=True)
        gn = yc * lax.rsqrt(var + RWKV_GN_EPS) * lng + lnb
        bonus = jnp.sum(r_t * k2 * rkp, axis=0, keepdims=True) * v_t
        y_ref[t] = (gn + bonus)[:, :nl]
        return carry

    lax.fori_loop(0, tc, step, 0)

    @pl.when(pl.program_id(1) == pl.num_programs(1) - 1)
    def _():
        sfin_ref[...] = s_scr[:, :, :nl]


def _rwkv_scan(r, k, v, d, a, params, s0, lane_block, tc):
    t, n, l = r.shape
    seq = pl.BlockSpec((tc, n, lane_block), lambda li, ti: (ti, 0, li))
    par = pl.BlockSpec((n, lane_block), lambda li, ti: (0, li))
    st = pl.BlockSpec((n, n, lane_block), lambda li, ti: (0, 0, li))
    return pl.pallas_call(
        functools.partial(_rwkv_scan_kernel, tc=tc), grid=(l // lane_block, t // tc),
        in_specs=[seq] * 5 + [par] * 5 + [st],
        out_specs=[seq, st],
        out_shape=[jax.ShapeDtypeStruct((t, n, l), F32), jax.ShapeDtypeStruct((n, n, l), F32)],
        scratch_shapes=[pltpu.VMEM((n, n, LANES), F32), pltpu.VMEM((n, LANES), F32), pltpu.VMEM((n, LANES), F32)],
        compiler_params=_cparams("parallel", "arbitrary"), name="rwkv_scan")(r, k, v, d, a, *params, s0)


def _rwkv_seq_scan_kernel(r_ref, k_ref, v_ref, d_ref, a_ref, kkp_ref, kap_ref, rkp_ref, lng_ref, lnb_ref,
                          y_ref, sfin_ref, s_scr, xin_scr, yout_scr, yrow_scr, sa_scr, *, tc, nb):
    n = RWKV_N
    groups = n // SUBLANES
    rows_per_pair = nb * tc
    ins = [ref.reshape(RWKV_PAIRS * rows_per_pair, LANES) for ref in (r_ref, k_ref, v_ref, d_ref, a_ref)]
    y2 = y_ref.reshape(RWKV_PAIRS * rows_per_pair, LANES)
    zero_rows = jnp.zeros((LANES - RWKV_HEADS * nb, n), F32)
    zero_lanes = jnp.zeros((LANES, LANES - n), F32)

    def load_transposed(x2, t):
        pieces = []
        for p in range(RWKV_PAIRS):
            vp = x2[pl.ds(p * rows_per_pair + t, nb, stride=tc), :]
            pieces += [vp[:, :n], vp[:, n:]]
        m = jnp.concatenate(pieces + [zero_rows], axis=0)
        return jnp.concatenate([m, zero_lanes], axis=1).T[:n, :]

    def store_token_major(y, t):
        w = jnp.concatenate([y, jnp.zeros((LANES - n, LANES), F32)], axis=0).T
        for p in range(RWKV_PAIRS):
            even = w[(2 * p) * nb:(2 * p + 1) * nb, :n]
            odd = w[(2 * p + 1) * nb:(2 * p + 2) * nb, :n]
            y2[pl.ds(p * rows_per_pair + t, nb, stride=tc), :] = jnp.concatenate([even, odd], axis=1)

    @pl.when(pl.program_id(0) == 0)
    def _():
        s_scr[...] = jnp.zeros(s_scr.shape, F32)
        yout_scr[...] = jnp.zeros(yout_scr.shape, F32)

    for q in range(5):
        xin_scr[0, q] = load_transposed(ins[q], 0)

    kkp = kkp_ref[...]
    kap = kap_ref[...]
    rkp = rkp_ref[...]
    lng = lng_ref[...]
    lnb = lnb_ref[...]
    sub = lax.broadcasted_iota(jnp.int32, (SUBLANES, LANES), 0)
    low4 = sub < 4
    low2 = (sub & 3) < 2
    low1 = (sub & 1) == 0

    def fold(x, y, dist, low):
        if dist == 4:
            return jnp.where(low, x, y) + pltpu.roll(jnp.where(low, y, x), 4, axis=0)
        return (jnp.where(low, x, pltpu.roll(y, dist, axis=0))
                + jnp.where(low, pltpu.roll(x, SUBLANES - dist, axis=0), y))

    def sublane_sums(ps):
        z = [fold(ps[0], ps[4], 4, low4), fold(ps[2], ps[6], 4, low4),
             fold(ps[1], ps[5], 4, low4), fold(ps[3], ps[7], 4, low4)]
        return fold(fold(z[0], z[1], 2, low2), fold(z[2], z[3], 2, low2), 1, low1)

    def tile_sum(x):
        acc = x[0:SUBLANES]
        for u in range(1, groups):
            acc = acc + x[u * SUBLANES:(u + 1) * SUBLANES]
        return acc

    def step(t, carry):
        slot = lax.rem(t, 2)
        other = 1 - slot
        r_t = xin_scr[slot, 0]
        k_t = xin_scr[slot, 1]
        v_t = xin_scr[slot, 2]
        d_t = xin_scr[slot, 3]
        a_t = xin_scr[slot, 4]
        kkr = k_t * kkp
        nrm = jnp.maximum(jnp.sqrt(jnp.sum(kkr * kkr, axis=0, keepdims=True)), 1e-12)
        kk = kkr * (1.0 / nrm)
        k2 = k_t * (1.0 + (a_t - 1.0) * kap)
        nkk = -kk
        b_t = kk * a_t
        for g in range(groups):
            sa_scr[g * SUBLANES:(g + 1) * SUBLANES, :] = sublane_sums(
                [tile_sum(s_scr[g * SUBLANES + u] * nkk) for u in range(SUBLANES)])
        for g in range(groups):
            ps = []
            for u in range(SUBLANES):
                i = g * SUBLANES + u
                s_n = s_scr[i] * d_t + sa_scr[pl.ds(i, 1), :] * b_t + xin_scr[slot, 2, pl.ds(i, 1), :] * k2
                s_scr[i] = s_n
                ps.append(tile_sum(s_n * r_t))
            yrow_scr[g * SUBLANES:(g + 1) * SUBLANES, :] = sublane_sums(ps)
        store_token_major(yout_scr[other], jnp.maximum(t - 1, 0))
        t_next = jnp.minimum(t + 1, tc - 1)
        for q in range(5):
            xin_scr[other, q] = load_transposed(ins[q], t_next)
        y = yrow_scr[...]
        yc = y - jnp.mean(y, axis=0, keepdims=True)
        var = jnp.mean(yc * yc, axis=0, keepdims=True)
        gn = yc * lax.rsqrt(var + RWKV_GN_EPS) * lng + lnb
        bonus = jnp.sum(r_t * k2 * rkp, axis=0, keepdims=True) * v_t
        yout_scr[slot] = gn + bonus
        return carry

    lax.fori_loop(0, tc, step, 0)
    store_token_major(yout_scr[(tc - 1) % 2], tc - 1)

    @pl.when(pl.program_id(0) == pl.num_programs(0) - 1)
    def _():
        sfin_ref[...] = s_scr[...]


def _rwkv_seq_scan(r, k, v, d, a, params, nb, t, tc):
    n = RWKV_N
    assert tc % 2 == 0 and tc % SUBLANES == 0 and t % tc == 0 and RWKV_HEADS * nb <= LANES
    seq = pl.BlockSpec((RWKV_PAIRS, nb, tc, LANES), lambda ti: (0, 0, ti, 0))
    par = pl.BlockSpec((n, LANES), lambda ti: (0, 0))
    st = pl.BlockSpec((n, n, LANES), lambda ti: (0, 0, 0))
    args = [x.reshape(RWKV_PAIRS, nb, t, LANES) for x in (r, k, v, d, a)]
    y, s = pl.pallas_call(
        functools.partial(_rwkv_seq_scan_kernel, tc=tc, nb=nb), grid=(t // tc,),
        in_specs=[seq] * 5 + [par] * 5,
        out_specs=[seq, st],
        out_shape=[jax.ShapeDtypeStruct((RWKV_PAIRS, nb, t, LANES), F32), jax.ShapeDtypeStruct((n, n, LANES), F32)],
        scratch_shapes=[pltpu.VMEM((n, n, LANES), F32), pltpu.VMEM((2, 5, n, LANES), F32),
                        pltpu.VMEM((2, n, LANES), F32), pltpu.VMEM((n, LANES), F32), pltpu.VMEM((n, LANES), F32)],
        compiler_params=_cparams("arbitrary"), name="rwkv_seq_scan")(*args, *params)
    return y.reshape(RWKV_PAIRS, nb * t, LANES), s


def _gla_proj_kernel(h_ref, wq_ref, wk_ref, wv_ref, wr_ref, wm_ref, a1_ref, a2_ref, ab_ref,
                     q_out, k_out, v_out, r_out, la_out, qm_out):
    hb = h_ref[...].astype(BF16)
    low = _dot(hb, a1_ref[...]).astype(BF16)
    for hd in range(GLA_HEADS):
        q_out[hd] = _dot(hb, wq_ref[hd])
        k_out[hd] = _dot(hb, wk_ref[hd])
        v_out[hd] = _dot(hb, wv_ref[hd])
        r_out[hd] = _dot(hb, wr_ref[hd])
        x = _dot(low, a2_ref[hd]) + ab_ref[hd]
        log_sigmoid = jnp.minimum(x, 0.0) - jnp.log(1.0 + jnp.exp(-jnp.abs(x)))
        la_out[hd] = log_sigmoid / GLA_TAU
    qm_out[...] = _dot(hb, wm_ref[...])


def _gla_weights(w_in, a1, a2, ab):
    kw, tw, nh = GLA_KW, TOK_WIDTH, GLA_HEADS

    def heads(w, d):
        return w.reshape(w.shape[0], nh, d).transpose(1, 0, 2)

    return [heads(w_in[:, :kw], GLA_DK).astype(BF16), heads(w_in[:, kw:2 * kw], GLA_DK).astype(BF16),
            heads(w_in[:, 2 * kw:2 * kw + tw], GLA_DV).astype(BF16),
            heads(w_in[:, 2 * kw + tw:3 * tw], GLA_DV).astype(BF16), w_in[:, 3 * tw:].astype(BF16),
            a1.astype(BF16), heads(a2, GLA_DK).astype(BF16), ab.reshape(nh, 1, GLA_DK)]


def _gla_proj(h, ws, tm):
    m = h.shape[0]
    nh = GLA_HEADS
    widths = [GLA_DK, GLA_DK, GLA_DV, GLA_DV, GLA_DK]
    return pl.pallas_call(
        _gla_proj_kernel, grid=(m // tm,),
        in_specs=[_row_spec(tm, D_MODEL)] + [_full_spec(w.shape) for w in ws],
        out_specs=[_head_row_spec(tm, w) for w in widths] + [_row_spec(tm, MEM_WIDTH)],
        out_shape=[jax.ShapeDtypeStruct((nh, m, w), F32) for w in widths] + [jax.ShapeDtypeStruct((m, MEM_WIDTH), F32)],
        compiler_params=_cparams("parallel"), name="gla_proj")(h, *ws)


def _gla_out_norm(o, g):
    return o * lax.rsqrt(jnp.mean(o * o, axis=-1, keepdims=True) + NORM_EPS) * g


def _gla_chunk_kernel(q_ref, k_ref, v_ref, la_ref, ng_ref, o_ref, sfin_ref, st_scr):
    c, dk = GLA_CHUNK, GLA_DK

    @pl.when(pl.program_id(1) == 0)
    def _():
        st_scr[...] = jnp.zeros(st_scr.shape, F32)

    row = lax.broadcasted_iota(jnp.int32, (c, c), 0)
    col = lax.broadcasted_iota(jnp.int32, (c, c), 1)
    tril = (row >= col).astype(F32)
    rr = lax.broadcasted_iota(jnp.int32, (c, dk), 0)
    ones_sum = jnp.ones((dk, LANES), BF16)

    for hd in range(GLA_HEADS):
        la = la_ref[hd]
        k = k_ref[hd]
        vb = v_ref[hd].astype(BF16)
        q = q_ref[hd] * (dk ** -0.5)
        b = jnp.dot(tril, la, precision=HIGHEST, preferred_element_type=F32)
        b_end = b[c - 1:c, :]
        st = st_scr[hd]
        inter = _dot_nt((q * jnp.exp(b)).astype(BF16), st.astype(BF16))

        att = jnp.zeros((c, c), F32)
        blk = c // 2
        while blk >= GLA_TILE:
            two = 2 * blk
            b_ref_rows = jnp.concatenate(
                [jnp.broadcast_to(b[s0 + blk - 1:s0 + blk, :], (two, dk)) for s0 in range(0, c, two)], axis=0)
            upper = (rr & (two - 1)) >= blk
            q_l = jnp.where(upper, q * jnp.exp(jnp.minimum(b - b_ref_rows, 0.0)), 0.0).astype(BF16)
            k_l = jnp.where(upper, 0.0, k * jnp.exp(jnp.minimum(b_ref_rows - b, 0.0))).astype(BF16)
            att = att + jnp.where((row ^ col) < two, _dot_nt(q_l, k_l), 0.0)
            blk //= 2

        prods = [(q * k).astype(BF16)]
        for dlt in range(1, GLA_TILE):
            k_s = pltpu.roll(k, dlt, axis=0)
            b_s = pltpu.roll(b, dlt, axis=0)
            p = q * k_s * jnp.exp(jnp.minimum(b - b_s, 0.0))
            prods.append(jnp.where((rr & (GLA_TILE - 1)) >= dlt, p, 0.0).astype(BF16))
        sums = _dot(jnp.concatenate(prods, axis=0), ones_sum)
        for dlt in range(GLA_TILE):
            att = att + jnp.where(col == row - dlt, sums[dlt * c:(dlt + 1) * c, :c], 0.0)

        o = inter + _dot(att.astype(BF16), vb)
        o_ref[hd] = _gla_out_norm(o, ng_ref[...])
        kd = (k * jnp.exp(b_end - b)).astype(BF16)
        st_scr[hd] = jnp.exp(b_end) * st + _dot_tn(vb, kd)

    @pl.when(pl.program_id(1) == pl.num_programs(1) - 1)
    def _():
        sfin_ref[0] = st_scr[...]


def _gla_chunk_scan(q, k, v, la, norm_g, batch, t):
    nh, dk, dv, c = GLA_HEADS, GLA_DK, GLA_DV, GLA_CHUNK
    nc = t // c
    kspec = pl.BlockSpec((nh, c, dk), lambda i, j: (0, i * nc + j, 0))
    vspec = pl.BlockSpec((nh, c, dv), lambda i, j: (0, i * nc + j, 0))
    sspec = pl.BlockSpec((1, nh, dv, dk), lambda i, j: (i, 0, 0, 0))
    return pl.pallas_call(
        _gla_chunk_kernel, grid=(batch, nc),
        in_specs=[kspec, kspec, vspec, kspec, pl.BlockSpec((1, dv), lambda i, j: (0, 0))],
        out_specs=[vspec, sspec],
        out_shape=[jax.ShapeDtypeStruct((nh, batch * t, dv), F32), jax.ShapeDtypeStruct((batch, nh, dv, dk), F32)],
        scratch_shapes=[pltpu.VMEM((nh, dv, dk), F32)],
        compiler_params=_cparams("parallel", "arbitrary"), name="gla_chunk")(q, k, v, la, norm_g.reshape(1, dv))


GLA_STEP_GROUP = 8


def _gla_step_kernel(q_ref, k_ref, v_ref, la_ref, ng_ref, s0_ref, o_ref, s_ref):
    for g in range(GLA_STEP_GROUP):
        s_new = jnp.exp(la_ref[g]) * s0_ref[g] + k_ref[g] * v_ref[g]
        s_ref[g] = s_new
        o = jnp.sum((q_ref[g] * (GLA_DK ** -0.5)) * s_new, axis=0, keepdims=True)
        o_ref[g] = _gla_out_norm(o, ng_ref[...])


def _gla_step(q, k, v, la, norm_g, s0):
    bh, dk, _ = q.shape
    dv = v.shape[2]
    g = GLA_STEP_GROUP
    cspec = pl.BlockSpec((g, dk, 1), lambda i: (i, 0, 0))
    vspec = pl.BlockSpec((g, 1, dv), lambda i: (i, 0, 0))
    sspec = pl.BlockSpec((g, dk, dv), lambda i: (i, 0, 0))
    return pl.pallas_call(
        _gla_step_kernel, grid=(bh // g,),
        in_specs=[cspec, cspec, vspec, cspec, _full_spec((1, dv)), sspec],
        out_specs=[vspec, sspec],
        out_shape=[jax.ShapeDtypeStruct((bh, 1, dv), F32), jax.ShapeDtypeStruct((bh, dk, dv), F32)],
        compiler_params=_cparams("parallel"), name="gla_step")(q, k, v, la, norm_g.reshape(1, dv), s0)


def _mem_attn_kernel(q_ref, k_ref, v_ref, o_ref):
    q = q_ref[0]
    k = k_ref[0].astype(BF16)
    v_ones = jnp.concatenate([v_ref[0].astype(BF16), jnp.ones((N_MEM, LANES), BF16)], axis=1)
    head_of_lane = lax.broadcasted_iota(jnp.int32, (1, MEM_WIDTH), 1) // MEM_HEAD_DIM
    out = jnp.zeros(q.shape, F32)
    for h in range(MEM_HEADS):
        mine = head_of_lane == h
        qh = jnp.where(mine, q, 0.0).astype(BF16)
        s = _dot_nt(qh, k) * (MEM_HEAD_DIM ** -0.5)
        e = jnp.exp(s - jnp.max(s, axis=-1, keepdims=True)).astype(BF16)
        ev = _dot(e, v_ones)
        inv = 1.0 / ev[:, MEM_WIDTH:]
        out = out + jnp.where(mine, ev[:, :MEM_WIDTH] * jnp.concatenate([inv, inv], axis=1), 0.0)
    o_ref[0] = out


def _mem_attn(q, mem_k, mem_v, tq):
    b, t, w = q.shape
    qspec = pl.BlockSpec((1, tq, w), lambda i, j: (i, j, 0))
    mspec = pl.BlockSpec((1, N_MEM, w), lambda i, j: (i, 0, 0))
    return pl.pallas_call(
        _mem_attn_kernel, grid=(b, t // tq),
        in_specs=[qspec, mspec, mspec], out_specs=qspec,
        out_shape=jax.ShapeDtypeStruct((b, t, w), F32),
        compiler_params=_cparams("parallel", "parallel"), name="mem_attn")(q, mem_k, mem_v)


def _out_proj_kernel(*refs, layout, aliased):
    tok_ref, gate_ref, att_ref, x_ref, wo_ref, g_ref, wr_ref, br_ref = refs[:8]
    x1_out, h2_out, logit_out = refs[8 + aliased:]
    x1 = x_ref[...] + _dot(att_ref[...].astype(BF16), wo_ref[TOK_WIDTH:, :])
    if layout == "gla_heads":
        for hd in range(GLA_HEADS):
            gate = gate_ref[hd]
            mixed = (tok_ref[hd] * (gate * _sigmoid(gate))).astype(BF16)
            x1 = x1 + _dot(mixed, wo_ref[hd * GLA_DV:(hd + 1) * GLA_DV, :])
    else:
        if layout == "rwkv_pairs":
            tok = jnp.concatenate([tok_ref[p] for p in range(RWKV_PAIRS)], axis=1)
        else:
            tok = tok_ref[...]
        x1 = x1 + _dot((tok * gate_ref[...]).astype(BF16), wo_ref[:TOK_WIDTH, :])
    x1_out[...] = x1
    h2 = _rms(x1, g_ref[...])
    h2_out[...] = h2
    logit_out[...] = jnp.dot(h2, wr_ref[...], precision=HIGHEST, preferred_element_type=F32) + br_ref[...]


def _out_proj(tok, gate, att, x, ws, layout, tm, h2_rows, h2_row_offset, h2_buffer=None):
    m = x.shape[0]
    aliased = h2_buffer is not None
    rows_spec = _row_spec(tm, TOK_WIDTH)
    tok_spec, gate_spec = {
        "rows": (rows_spec, rows_spec),
        "rwkv_pairs": (pl.BlockSpec((RWKV_PAIRS, tm, LANES), lambda i: (0, i, 0)), rows_spec),
        "gla_heads": (_head_row_spec(tm, GLA_DV), _head_row_spec(tm, GLA_DV))}[layout]
    in_specs = ([tok_spec, gate_spec, _row_spec(tm, MEM_WIDTH), _row_spec(tm, D_MODEL)]
                + [_full_spec(w.shape) for w in ws])
    args = [tok, gate, att, x, *ws]
    if aliased:
        in_specs.append(pl.BlockSpec(memory_space=pl.ANY))
        args.append(h2_buffer)
    return pl.pallas_call(
        functools.partial(_out_proj_kernel, layout=layout, aliased=int(aliased)), grid=(m // tm,),
        in_specs=in_specs,
        out_specs=[_row_spec(tm, D_MODEL), _row_spec(tm, D_MODEL, h2_row_offset // tm), _row_spec(tm, ROUTER_LANES)],
        out_shape=[jax.ShapeDtypeStruct((m, D_MODEL), F32), jax.ShapeDtypeStruct((h2_rows, D_MODEL), F32),
                   jax.ShapeDtypeStruct((m, ROUTER_LANES), F32)],
        input_output_aliases={len(args) - 1: 1} if aliased else {},
        compiler_params=_cparams("parallel"), name="out_proj")(*args)


def _moe_kernel(tile_ref, exp_ref, nitem_ref, lo_ref, hi_ref, x_ref, gate_ref, wu_ref, wd_ref, o_ref, wu_scr, wd_scr):
    w = pl.program_id(0)
    prev = jnp.maximum(w - 1, 0)
    e = exp_ref[w]
    valid = w < nitem_ref[0]

    @pl.when(jnp.logical_and(valid, jnp.logical_or(w == 0, e != exp_ref[prev])))
    def _():
        wu_scr[...] = wu_ref[0, 0].astype(BF16)
        wd_scr[...] = wd_ref[0, 0].astype(BF16)

    @pl.when(jnp.logical_or(w == 0, tile_ref[w] != tile_ref[prev]))
    def _():
        o_ref[...] = jnp.zeros(o_ref.shape, F32)

    @pl.when(valid)
    def _():
        gu = _dot(x_ref[...].astype(BF16), wu_scr[...])
        g = gu[:, :EXPERT_FF]
        act = (g * _sigmoid(g) * gu[:, EXPERT_FF:]).astype(BF16)
        out = _dot(act, wd_scr[...]) * gate_ref[...]
        rows = tile_ref[w] * MOE_BLOCK + lax.broadcasted_iota(jnp.int32, (MOE_BLOCK, 1), 0)
        mine = jnp.logical_and(rows >= lo_ref[e], rows < hi_ref[e])
        o_ref[...] = o_ref[...] + jnp.where(mine, out, 0.0)


def _moe_ffn(xs, row_gate, item_tile, item_expert, n_items, lo, hi, w_up, w_down, layer):
    tm = MOE_BLOCK
    n_work = item_tile.shape[0]
    a = xs.shape[0]
    row_map = lambda w, tile, ex, ni, lo_, hi_: (tile[w], 0)
    exp_map = lambda w, tile, ex, ni, lo_, hi_: (layer, ex[w], 0, 0)
    grid_spec = pltpu.PrefetchScalarGridSpec(
        num_scalar_prefetch=5, grid=(n_work,),
        in_specs=[pl.BlockSpec((tm, D_MODEL), row_map),
                  pl.BlockSpec((tm, 1), row_map),
                  pl.BlockSpec((1, 1, D_MODEL, 2 * EXPERT_FF), exp_map),
                  pl.BlockSpec((1, 1, EXPERT_FF, D_MODEL), exp_map)],
        out_specs=pl.BlockSpec((tm, D_MODEL), row_map),
        scratch_shapes=[pltpu.VMEM((D_MODEL, 2 * EXPERT_FF), BF16), pltpu.VMEM((EXPERT_FF, D_MODEL), BF16)])
    return pl.pallas_call(
        _moe_kernel, grid_spec=grid_spec,
        out_shape=jax.ShapeDtypeStruct((a, D_MODEL), F32),
        compiler_params=_cparams("arbitrary"), name="moe_ffn")(
            item_tile, item_expert, n_items, lo, hi, xs, row_gate, w_up, w_down)


def _route(logits, n_prompt, tm):
    m = logits.shape[0]
    n_sample = m - n_prompt
    a = 2 * m
    gl = logits[:, :N_GROUPS]
    el = logits[:, N_GROUPS:N_GROUPS + N_EXPERTS].reshape(m, N_GROUPS, EXPERTS_PER_GROUP)
    group = jnp.argmax(gl, -1).astype(jnp.int32)
    p_group = jnp.max(jax.nn.softmax(gl, -1), -1, keepdims=True)
    in_group = jnp.take_along_axis(el, group[:, None, None], axis=1)[:, 0]
    top_val, top_idx = lax.top_k(in_group, 2)
    gate = p_group * jax.nn.softmax(top_val, -1)
    expert = group[:, None] * EXPERTS_PER_GROUP + top_idx.astype(jnp.int32)

    def by_id(t):
        return jnp.concatenate([t[:n_prompt, 0], t[:n_prompt, 1], t[n_prompt:, 0], t[n_prompt:, 1]])

    tok_of_id = jnp.asarray(np.concatenate([np.arange(n_prompt), np.arange(n_prompt),
                                            n_prompt + np.arange(n_sample), n_prompt + np.arange(n_sample)]), jnp.int32)
    flat_e = by_id(expert)
    ids = jnp.arange(a, dtype=jnp.int32)
    _, order, gate_sorted, tok_sorted = lax.sort((flat_e, ids, by_id(gate), tok_of_id), num_keys=1, is_stable=True)
    _, inv = lax.sort((order, ids), num_keys=1)
    experts = jnp.arange(N_EXPERTS, dtype=jnp.int32)
    counts = jnp.sum((flat_e[:, None] == experts[None, :]).astype(jnp.int32), axis=0)
    hi = jnp.cumsum(counts).astype(jnp.int32)
    lo = hi - counts
    n_tiles = a // tm
    first_tile = lo // tm
    tiles_of = jnp.where(counts > 0, (hi - 1) // tm - first_tile + 1, 0)
    item_end = jnp.cumsum(tiles_of).astype(jnp.int32)
    n_items = item_end[-1:]
    n_work = n_tiles + N_EXPERTS - 1
    w = jnp.minimum(jnp.arange(n_work, dtype=jnp.int32), n_items[0] - 1)
    item_expert = jnp.sum((item_end[None, :] <= w[:, None]).astype(jnp.int32), axis=1)
    onehot = (item_expert[:, None] == experts[None, :]).astype(jnp.int32)
    item_tile = jnp.sum(onehot * (first_tile - (item_end - tiles_of))[None, :], axis=1) + w
    return tok_sorted, gate_sorted.reshape(a, 1), inv, item_tile.astype(jnp.int32), item_expert, n_items, lo, hi


def _combine_kernel(x_ref, r0_ref, r1_ref, g_ref, x_out, h_out):
    x2 = x_ref[...] + (r0_ref[...] + r1_ref[...])
    x_out[...] = x2
    h_out[...] = _rms(x2, g_ref[...])


def _combine(x1, gathered, first_row, g, tm):
    m = x1.shape[0]
    spec = _row_spec(tm, D_MODEL)
    return pl.pallas_call(
        _combine_kernel, grid=(m // tm,),
        in_specs=[spec, _row_spec(tm, D_MODEL, first_row // tm), _row_spec(tm, D_MODEL, (first_row + m) // tm),
                  _full_spec((1, D_MODEL))],
        out_specs=[spec, spec],
        out_shape=[jax.ShapeDtypeStruct((m, D_MODEL), F32)] * 2,
        compiler_params=_cparams("parallel"), name="moe_combine")(x1, gathered, gathered, g.reshape(1, D_MODEL))


def _to_scan_layout(t2d, b, t):
    return t2d.reshape(b, t, RWKV_HEADS, RWKV_N).transpose(1, 3, 0, 2).reshape(t, RWKV_N, b * RWKV_HEADS)


def _from_scan_layout(y, b, t):
    return y.reshape(t, RWKV_N, b, RWKV_HEADS).transpose(2, 0, 3, 1).reshape(b * t, TOK_WIDTH)


def _scan_param(p, b):
    return jnp.tile(p.reshape(RWKV_HEADS, RWKV_N).T, (1, b))


def _seq_scan_param(p, b):
    lanes = jnp.repeat(p.reshape(RWKV_HEADS, RWKV_N).T, b, axis=1)
    return jnp.pad(lanes, ((0, 0), (0, LANES - RWKV_HEADS * b)))


def _state_to_scan(s):
    b = s.shape[0]
    return s.transpose(2, 3, 0, 1).reshape(RWKV_N, RWKV_N, b * RWKV_HEADS)


def _state_from_scan(s, b):
    return s.reshape(RWKV_N, RWKV_N, b, RWKV_HEADS).transpose(2, 3, 0, 1)


def kernel(x_prompt, x_sample, mem_prompt, state_rwkv_S, state_rwkv_shift, state_gla_S, cache_mem_k, cache_mem_v, norm_mix_g, norm_ffn_g, norm_mem_g, norm_final_g, w_in, w_out, w_mem_kv, rw_mu, rw_w0, rw_w1, rw_w2, rw_a0, rw_a1, rw_a2, rw_g1, rw_g2, rw_k_k, rw_k_a, rw_r_k, rw_ln_g, rw_ln_b, gla_a1, gla_a2, gla_ab, gla_norm_g, router_wg, router_bg, router_we, router_be, exp_w_up, exp_w_down):
    bp, tp, _ = x_prompt.shape
    bs, ts, _ = x_sample.shape
    assert ts == 1 and tp % GLA_CHUNK == 0 and tp % SCAN_TIME_BLOCK == 0
    np_ = bp * tp
    ns = bs * ts
    m = np_ + ns
    assert np_ % LIGHT_BLOCK == 0 and ns % SAMPLE_BLOCK == 0 and (2 * m) % MOE_BLOCK == 0
    depth = w_in.shape[0]
    nh = GLA_HEADS
    tw = TOK_WIDTH
    bf = lambda t_: t_.astype(BF16)

    mem2d = mem_prompt.reshape(bp * N_MEM, D_MODEL)
    mem_kv = [_norm_matmul(mem2d, norm_mem_g[i], w_mem_kv[i], 512) for i in range(depth)]
    pk = [kv[:, :MEM_WIDTH].reshape(bp, N_MEM, MEM_WIDTH) for kv in mem_kv]
    pv = [kv[:, MEM_WIDTH:].reshape(bp, N_MEM, MEM_WIDTH) for kv in mem_kv]
    prompt_mem_k = jnp.stack(pk).reshape(depth, bp, N_MEM, MEM_HEADS, MEM_HEAD_DIM)
    prompt_mem_v = jnp.stack(pv).reshape(depth, bp, N_MEM, MEM_HEADS, MEM_HEAD_DIM)
    sk = cache_mem_k.reshape(depth, bs, N_MEM, MEM_WIDTH)
    sv = cache_mem_v.reshape(depth, bs, N_MEM, MEM_WIDTH)

    x_p = x_prompt.reshape(np_, D_MODEL)
    x_s = x_sample.reshape(ns, D_MODEL)
    h_p = _norm(x_p, norm_mix_g[0], LIGHT_BLOCK)
    h_s = _norm(x_s, norm_mix_g[0], SAMPLE_BLOCK)

    w_router = jnp.zeros((depth, D_MODEL, ROUTER_LANES), F32)
    w_router = w_router.at[:, :, :N_GROUPS].set(router_wg).at[:, :, N_GROUPS:N_GROUPS + N_EXPERTS].set(router_we)
    b_router = jnp.zeros((depth, 1, ROUTER_LANES), F32)
    b_router = b_router.at[:, 0, :N_GROUPS].set(router_bg).at[:, 0, N_GROUPS:N_GROUPS + N_EXPERTS].set(router_be)

    p_rw_S, p_rw_shift, p_gla_S, s_rw_S, s_rw_shift, s_gla_S = [], [], [], [], [], []
    for i in range(depth):
        j = i // 2
        if i % 2 == 0:
            wi = w_in[i]
            ws = [rw_mu[j], bf(wi[:, :tw]), bf(wi[:, tw:2 * tw]), bf(wi[:, 2 * tw:3 * tw]), bf(wi[:, 3 * tw:]),
                  bf(rw_w1[j]), bf(rw_w2[j]), rw_w0[j].reshape(1, tw), bf(rw_a1[j]), bf(rw_a2[j]),
                  rw_a0[j].reshape(1, tw), bf(rw_g1[j]), bf(rw_g2[j])]
            *rkvda_p, gate_p, qm_p = _rwkv_proj(h_p, None, ws, PROJ_BLOCK, tp)
            *rkvda_s, gate_s, qm_s = _rwkv_proj(h_s, state_rwkv_shift[j], ws, SAMPLE_BLOCK, ts)
            pvec = [rw_k_k[j], rw_k_a[j], rw_r_k[j], rw_ln_g[j], rw_ln_b[j]]
            tok_p, sp = _rwkv_seq_scan(*rkvda_p, [_seq_scan_param(p, bp) for p in pvec], bp, tp, SCAN_TIME_BLOCK)
            ys, ss = _rwkv_scan(*[_to_scan_layout(t_, bs, ts) for t_ in rkvda_s], [_scan_param(p, bs) for p in pvec],
                                _state_to_scan(state_rwkv_S[j]), LANES, 1)
            tok_s = _from_scan_layout(ys, bs, ts)
            p_rw_S.append(sp[:, :, :RWKV_HEADS * bp].reshape(RWKV_N, RWKV_N, RWKV_HEADS, bp).transpose(3, 2, 0, 1))
            s_rw_S.append(_state_from_scan(ss, bs))
            p_rw_shift.append(h_p.reshape(bp, tp, D_MODEL)[:, -1])
            s_rw_shift.append(h_s)
            layout_p, layout_s = "rwkv_pairs", "rows"
        else:
            ws = _gla_weights(w_in[i], gla_a1[j], gla_a2[j], gla_ab[j])
            q_p, k_p, v_p, gate_p, la_p, qm_p = _gla_proj(h_p, ws, PROJ_BLOCK)
            q_s, k_s, v_s, gate_s, la_s, qm_s = _gla_proj(h_s, ws, SAMPLE_BLOCK)
            tok_p, sp_t = _gla_chunk_scan(q_p, k_p, v_p, la_p, gla_norm_g[j], bp, tp)
            col = lambda t_: t_.transpose(1, 0, 2).reshape(bs * nh, GLA_DK, 1)
            os_, ss = _gla_step(col(q_s), col(k_s), v_s.transpose(1, 0, 2).reshape(bs * nh, 1, GLA_DV), col(la_s),
                                gla_norm_g[j], state_gla_S[j].reshape(bs * nh, GLA_DK, GLA_DV))
            tok_s = os_.reshape(bs, nh, GLA_DV).transpose(1, 0, 2)
            p_gla_S.append(sp_t.transpose(0, 1, 3, 2))
            s_gla_S.append(ss.reshape(bs, nh, GLA_DK, GLA_DV))
            layout_p = layout_s = "gla_heads"

        att_p = _mem_attn(qm_p.reshape(bp, tp, MEM_WIDTH), pk[i], pv[i], 512).reshape(np_, MEM_WIDTH)
        att_s = _mem_attn(qm_s.reshape(bs, ts, MEM_WIDTH), sk[i], sv[i], 1).reshape(ns, MEM_WIDTH)
        ws = [bf(w_out[i]), norm_ffn_g[i].reshape(1, D_MODEL), w_router[i], b_router[i]]
        x1_p, h2, logits_p = _out_proj(tok_p, gate_p, att_p, x_p, ws, layout_p, PROJ_BLOCK, m, 0)
        x1_s, h2, logits_s = _out_proj(tok_s, gate_s, att_s, x_s, ws, layout_s, SAMPLE_BLOCK, m, np_, h2_buffer=h2)
        tok_sorted, gate_sorted, inv, item_tile, item_expert, n_items, lo, hi = _route(
            jnp.concatenate([logits_p, logits_s], axis=0), np_, MOE_BLOCK)
        rows = _moe_ffn(h2[tok_sorted], gate_sorted, item_tile, item_expert, n_items, lo, hi,
                        exp_w_up, exp_w_down, i)
        gathered = rows[inv]
        g_next = norm_mix_g[i + 1] if i + 1 < depth else norm_final_g
        x_p, h_p = _combine(x1_p, gathered, 0, g_next, LIGHT_BLOCK)
        x_s, h_s = _combine(x1_s, gathered, 2 * np_, g_next, SAMPLE_BLOCK)

    y_prompt = h_p.reshape(bp, tp, D_MODEL)
    y_sample = h_s.reshape(bs, ts, D_MODEL)
    return (y_prompt, y_sample, jnp.stack(p_rw_S), jnp.stack(p_rw_shift), jnp.stack(p_gla_S),
            prompt_mem_k, prompt_mem_v, jnp.stack(s_rw_S), jnp.stack(s_rw_shift), jnp.stack(s_gla_S))
```

```python
import functools

import numpy as np
import jax
import jax.numpy as jnp
from jax import lax
from jax.experimental import pallas as pl
from jax.experimental.pallas import tpu as pltpu

F32 = jnp.float32
BF16 = jnp.bfloat16
HIGHEST = lax.Precision.HIGHEST

D_MODEL = 1024
TOK_WIDTH = 768
MEM_WIDTH = 256
MEM_HEADS = 4
MEM_HEAD_DIM = 64
N_MEM = 256
RWKV_HEADS = 12
RWKV_N = 64
RWKV_PAIRS = RWKV_HEADS // 2
RWKV_GN_EPS = 64e-5
GLA_HEADS = 4
GLA_KW = 384
GLA_DK = 96
GLA_DV = 192
GLA_TAU = 16.0
GLA_CHUNK = 64
GLA_TILE = 8
N_GROUPS = 4
EXPERTS_PER_GROUP = 8
N_EXPERTS = 32
EXPERT_FF = 512
NORM_EPS = 1e-6
ROUTER_LANES = 128
LANES = 128
SUBLANES = 8

PROJ_BLOCK = 512
LIGHT_BLOCK = 512
SAMPLE_BLOCK = 128
MOE_BLOCK = 256
SCAN_TIME_BLOCK = 32
VMEM_LIMIT = 56 * 1024 * 1024


def _cparams(*sem):
    return pltpu.CompilerParams(dimension_semantics=sem, vmem_limit_bytes=VMEM_LIMIT)


def _dot(a, b):
    return jnp.dot(a, b, preferred_element_type=F32)


def _dot_nt(a, b):
    return lax.dot_general(a, b, (((1,), (1,)), ((), ())), preferred_element_type=F32)


def _dot_tn(a, b):
    return lax.dot_general(a, b, (((0,), (0,)), ((), ())), preferred_element_type=F32)


def _rms(x, g):
    return x * lax.rsqrt(jnp.mean(x * x, axis=-1, keepdims=True) + NORM_EPS) * g


def _sigmoid(x):
    return 1.0 / (1.0 + jnp.exp(-x))


def _row_spec(tm, n, offset=0):
    return pl.BlockSpec((tm, n), lambda i: (i + offset, 0))


def _head_row_spec(tm, n):
    return pl.BlockSpec((GLA_HEADS, tm, n), lambda i: (0, i, 0))


def _full_spec(shape):
    nd = len(shape)
    return pl.BlockSpec(shape, lambda *_: (0,) * nd)


def _norm_kernel(x_ref, g_ref, o_ref):
    o_ref[...] = _rms(x_ref[...], g_ref[...])


def _norm(x, g, tm):
    m, d = x.shape
    return pl.pallas_call(
        _norm_kernel, grid=(m // tm,),
        in_specs=[_row_spec(tm, d), _full_spec((1, d))],
        out_specs=_row_spec(tm, d),
        out_shape=jax.ShapeDtypeStruct((m, d), F32),
        compiler_params=_cparams("parallel"), name="rms_norm")(x, g.reshape(1, d))


def _norm_matmul_kernel(x_ref, g_ref, w_ref, o_ref):
    o_ref[...] = _dot(_rms(x_ref[...], g_ref[...]).astype(BF16), w_ref[...])


def _norm_matmul(x, g, w, tm):
    m, d = x.shape
    n = w.shape[1]
    return pl.pallas_call(
        _norm_matmul_kernel, grid=(m // tm,),
        in_specs=[_row_spec(tm, d), _full_spec((1, d)), _full_spec((d, n))],
        out_specs=_row_spec(tm, n),
        out_shape=jax.ShapeDtypeStruct((m, n), F32),
        compiler_params=_cparams("parallel"), name="norm_matmul")(x, g.reshape(1, d), w.astype(BF16))


def _rwkv_proj_kernel(h_ref, hp_ref, mu_ref, wr_ref, wk_ref, wv_ref, wq_ref, w1_ref, w2_ref, w0_ref,
                      a1_ref, a2_ref, a0_ref, g1_ref, g2_ref,
                      r_out, k_out, v_out, d_out, a_out, g_out, q_out, *, blocks_per_seq):
    def put(out, val):
        if blocks_per_seq:
            for p in range(RWKV_PAIRS):
                out[p] = val[:, p * LANES:(p + 1) * LANES]
        else:
            out[...] = val

    h = h_ref[...]
    if blocks_per_seq:
        seq_start = (pl.program_id(0) % blocks_per_seq) == 0
        before = jnp.where(seq_start, 0.0, hp_ref[SUBLANES - 1:SUBLANES, :])
        row = lax.broadcasted_iota(jnp.int32, (h.shape[0], 1), 0)
        hp = jnp.where(row == 0, before, pltpu.roll(h, 1, axis=0))
    else:
        hp = hp_ref[...]
    xx = hp - h

    def mix(j):
        return (h + xx * mu_ref[j:j + 1, :]).astype(BF16)

    put(r_out, _dot(mix(0), wr_ref[...]))
    wl = w0_ref[...] + _dot(jnp.tanh(_dot(mix(1), w1_ref[...])).astype(BF16), w2_ref[...])
    put(k_out, _dot(mix(2), wk_ref[...]))
    put(v_out, _dot(mix(3), wv_ref[...]))
    al = a0_ref[...] + _dot(_dot(mix(4), a1_ref[...]).astype(BF16), a2_ref[...])
    g_out[...] = _dot(_sigmoid(_dot(mix(5), g1_ref[...])).astype(BF16), g2_ref[...])
    q_out[...] = _dot(h.astype(BF16), wq_ref[...])
    z = -wl
    softplus = jnp.maximum(z, 0.0) + jnp.log(1.0 + jnp.exp(-jnp.abs(z)))
    put(d_out, jnp.exp(-jnp.exp(-softplus - 0.5)))
    put(a_out, _sigmoid(al))


def _rwkv_proj(h, h_prev, ws, tm, seq_len):
    m = h.shape[0]
    tw = TOK_WIDTH
    if h_prev is None:
        assert seq_len % tm == 0 and tm % SUBLANES == 0
        per8 = tm // SUBLANES
        hp_spec = pl.BlockSpec((SUBLANES, D_MODEL), lambda i: (jnp.maximum(i * per8 - 1, 0), 0))
        h_prev, blocks_per_seq = h, seq_len // tm
    else:
        hp_spec, blocks_per_seq = _row_spec(tm, D_MODEL), 0
    if blocks_per_seq:
        scan_spec = pl.BlockSpec((RWKV_PAIRS, tm, LANES), lambda i: (0, i, 0))
        scan_shape = jax.ShapeDtypeStruct((RWKV_PAIRS, m, LANES), F32)
    else:
        scan_spec, scan_shape = _row_spec(tm, tw), jax.ShapeDtypeStruct((m, tw), F32)
    return pl.pallas_call(
        functools.partial(_rwkv_proj_kernel, blocks_per_seq=blocks_per_seq), grid=(m // tm,),
        in_specs=[_row_spec(tm, D_MODEL), hp_spec] + [_full_spec(w.shape) for w in ws],
        out_specs=[scan_spec] * 5 + [_row_spec(tm, tw), _row_spec(tm, MEM_WIDTH)],
        out_shape=[scan_shape] * 5 + [jax.ShapeDtypeStruct((m, tw), F32), jax.ShapeDtypeStruct((m, MEM_WIDTH), F32)],
        compiler_params=_cparams("parallel"), name="rwkv_proj")(h, h_prev, *ws)


def _pad_lanes(x):
    short = LANES - x.shape[-1]
    if short == 0:
        return x
    return jnp.concatenate([x, jnp.zeros(x.shape[:-1] + (short,), x.dtype)], axis=-1)


def _rwkv_scan_kernel(r_ref, k_ref, v_ref, d_ref, a_ref, kkp_ref, kap_ref, rkp_ref, lng_ref, lnb_ref, s0_ref,
                      y_ref, sfin_ref, s_scr, v_scr, yrow_scr, *, tc):
    n = RWKV_N
    nl = r_ref.shape[-1]

    @pl.when(pl.program_id(1) == 0)
    def _():
        s_scr[...] = _pad_lanes(s0_ref[...])

    kkp = _pad_lanes(kkp_ref[...])
    kap = _pad_lanes(kap_ref[...])
    rkp = _pad_lanes(rkp_ref[...])
    lng = _pad_lanes(lng_ref[...])
    lnb = _pad_lanes(lnb_ref[...])

    def step(t, carry):
        r_t = _pad_lanes(r_ref[t])
        k_t = _pad_lanes(k_ref[t])
        v_t = _pad_lanes(v_ref[t])
        d_t = _pad_lanes(d_ref[t])
        a_t = _pad_lanes(a_ref[t])
        v_scr[...] = v_t
        kkr = k_t * kkp
        nrm = jnp.maximum(jnp.sqrt(jnp.sum(kkr * kkr, axis=0, keepdims=True)), 1e-12)
        kk = kkr * (1.0 / nrm)
        k2 = k_t * (1.0 + (a_t - 1.0) * kap)
        nkk = -kk
        b_t = kk * a_t

        def ibody(i, c):
            s_i = s_scr[i]
            sa = jnp.sum(s_i * nkk, axis=0, keepdims=True)
            v_i = v_scr[pl.ds(i, 1), :]
            s_n = s_i * d_t + sa * b_t + v_i * k2
            s_scr[i] = s_n
            yrow_scr[pl.ds(i, 1), :] = jnp.sum(s_n * r_t, axis=0, keepdims=True)
            return c

        lax.fori_loop(0, n, ibody, 0, unroll=8)
        y = yrow_scr[...]
        yc = y - jnp.mean(y, axis=0, keepdims=True)
        var = jnp.mean(yc * yc, axis=0, keepdims=True)
        gn = yc * lax.rsqrt(var + RWKV_GN_EPS) * lng + lnb
        bonus = jnp.sum(r_t * k2 * rkp, axis=0, keepdims=True) * v_t
        y_ref[t] = (gn + bonus)[:, :nl]
        return carry

    lax.fori_loop(0, tc, step, 0)

    @pl.when(pl.program_id(1) == pl.num_programs(1) - 1)
    def _():
        sfin_ref[...] = s_scr[:, :, :nl]


def _rwkv_scan(r, k, v, d, a, params, s0, lane_block, tc):
    t, n, l = r.shape
    seq = pl.BlockSpec((tc, n, lane_block), lambda li, ti: (ti, 0, li))
    par = pl.BlockSpec((n, lane_block), lambda li, ti: (0, li))
    st = pl.BlockSpec((n, n, lane_block), lambda li, ti: (0, 0, li))
    return pl.pallas_call(
        functools.partial(_rwkv_scan_kernel, tc=tc), grid=(l // lane_block, t // tc),
        in_specs=[seq] * 5 + [par] * 5 + [st],
        out_specs=[seq, st],
        out_shape=[jax.ShapeDtypeStruct((t, n, l), F32), jax.ShapeDtypeStruct((n, n, l), F32)],
        scratch_shapes=[pltpu.VMEM((n, n, LANES), F32), pltpu.VMEM((n, LANES), F32), pltpu.VMEM((n, LANES), F32)],
        compiler_params=_cparams("parallel", "arbitrary"), name="rwkv_scan")(r, k, v, d, a, *params, s0)


PREP_TILES = 7


def _rwkv_prep_kernel(r_ref, k_ref, v_ref, d_ref, a_ref, kkp_ref, kap_ref, rkp_ref, o_ref, *, tc, nb):
    n = RWKV_N
    rows_per_pair = nb * tc
    ins = [ref.reshape(RWKV_PAIRS * rows_per_pair, LANES) for ref in (r_ref, k_ref, v_ref, d_ref, a_ref)]
    zero_rows = jnp.zeros((LANES - RWKV_HEADS * nb, LANES), F32)
    first_half = lax.broadcasted_iota(jnp.int32, (nb, LANES), 1) < n
    kkp = kkp_ref[...]
    kap = kap_ref[...]
    rkp = rkp_ref[...]

    def load_transposed_pair(x2, t):
        pieces = []
        for p in range(RWKV_PAIRS):
            now = x2[pl.ds(p * rows_per_pair + t, nb, stride=tc), :]
            nxt = x2[pl.ds(p * rows_per_pair + t + 1, nb, stride=tc), :]
            pieces += [jnp.where(first_half, now, pltpu.roll(nxt, n, axis=1)),
                       jnp.where(first_half, pltpu.roll(now, n, axis=1), nxt)]
        both = jnp.concatenate(pieces + [zero_rows], axis=0).T
        return both[:n, :], both[n:, :]

    def emit(t, r_t, k_t, v_t, d_t, a_t):
        kkr = k_t * kkp
        nrm = jnp.maximum(jnp.sqrt(jnp.sum(kkr * kkr, axis=0, keepdims=True)), 1e-12)
        kk = kkr * (1.0 / nrm)
        k2 = k_t * (1.0 + (a_t - 1.0) * kap)
        o_ref[t, 0] = -kk
        o_ref[t, 1] = d_t
        o_ref[t, 2] = kk * a_t
        o_ref[t, 3] = k2
        o_ref[t, 4] = r_t
        o_ref[t, 5] = v_t
        o_ref[t, 6] = jnp.sum(r_t * k2 * rkp, axis=0, keepdims=True) * v_t

    def two_steps(u, carry):
        t = 2 * u
        tiles = [load_transposed_pair(x2, t) for x2 in ins]
        emit(t, *[tile[0] for tile in tiles])
        emit(t + 1, *[tile[1] for tile in tiles])
        return carry

    lax.fori_loop(0, tc // 2, two_steps, 0, unroll=2)


def _rwkv_state_scan_kernel(x_ref, lng_ref, lnb_ref, y_ref, sfin_ref, s_scr, sa_scr, yrow_scr, *, tc):
    n = RWKV_N
    groups = n // SUBLANES

    @pl.when(pl.program_id(0) == 0)
    def _():
        s_scr[...] = jnp.zeros(s_scr.shape, F32)

    lng = lng_ref[...]
    lnb = lnb_ref[...]
    sub = lax.broadcasted_iota(jnp.int32, (SUBLANES, LANES), 0)
    low4 = sub < 4
    low2 = (sub & 3) < 2
    low1 = (sub & 1) == 0

    def fold(x, y, dist, low):
        if dist == 4:
            return jnp.where(low, x, y) + pltpu.roll(jnp.where(low, y, x), 4, axis=0)
        return (jnp.where(low, x, pltpu.roll(y, dist, axis=0))
                + jnp.where(low, pltpu.roll(x, SUBLANES - dist, axis=0), y))

    def sublane_sums(ps):
        z = [fold(ps[0], ps[4], 4, low4), fold(ps[2], ps[6], 4, low4),
             fold(ps[1], ps[5], 4, low4), fold(ps[3], ps[7], 4, low4)]
        return fold(fold(z[0], z[1], 2, low2), fold(z[2], z[3], 2, low2), 1, low1)

    def tile_sum(x):
        acc = x[0:SUBLANES]
        for u in range(1, groups):
            acc = acc + x[u * SUBLANES:(u + 1) * SUBLANES]
        return acc

    def step(t, carry):
        nkk = x_ref[t, 0]
        d_t = x_ref[t, 1]
        b_t = x_ref[t, 2]
        k2 = x_ref[t, 3]
        r_t = x_ref[t, 4]
        for g in range(groups):
            sa_scr[g * SUBLANES:(g + 1) * SUBLANES, :] = sublane_sums(
                [tile_sum(s_scr[g * SUBLANES + u] * nkk) for u in range(SUBLANES)])
        for g in range(groups):
            ps = []
            for u in range(SUBLANES):
                i = g * SUBLANES + u
                s_n = s_scr[i] * d_t + sa_scr[pl.ds(i, 1), :] * b_t + x_ref[t, 5, pl.ds(i, 1), :] * k2
                s_scr[i] = s_n
                ps.append(tile_sum(s_n * r_t))
            yrow_scr[g * SUBLANES:(g + 1) * SUBLANES, :] = sublane_sums(ps)
        y = yrow_scr[...]
        yc = y - jnp.mean(y, axis=0, keepdims=True)
        var = jnp.mean(yc * yc, axis=0, keepdims=True)
        y_ref[t] = yc * lax.rsqrt(var + RWKV_GN_EPS) * lng + lnb + x_ref[t, 6]
        return carry

    lax.fori_loop(0, tc, step, 0)

    @pl.when(pl.program_id(0) == pl.num_programs(0) - 1)
    def _():
        sfin_ref[...] = s_scr[...]


def _rwkv_unprep_kernel(y_ref, o_ref, *, tc, nb):
    n = RWKV_N
    rows_per_pair = nb * tc
    o2 = o_ref.reshape(RWKV_PAIRS * rows_per_pair, LANES)
    first_half = lax.broadcasted_iota(jnp.int32, (nb, LANES), 1) < n

    def two_steps(u, carry):
        t = 2 * u
        w = jnp.concatenate([y_ref[t], y_ref[t + 1]], axis=0).T
        for p in range(RWKV_PAIRS):
            even = w[(2 * p) * nb:(2 * p + 1) * nb, :]
            odd = w[(2 * p + 1) * nb:(2 * p + 2) * nb, :]
            o2[pl.ds(p * rows_per_pair + t, nb, stride=tc), :] = jnp.where(
                first_half, even, pltpu.roll(odd, n, axis=1))
            o2[pl.ds(p * rows_per_pair + t + 1, nb, stride=tc), :] = jnp.where(
                first_half, pltpu.roll(even, n, axis=1), odd)
        return carry

    lax.fori_loop(0, tc // 2, two_steps, 0, unroll=2)


def _rwkv_prompt_mixer(r, k, v, d, a, kkp, kap, rkp, lng, lnb, nb, t, tc):
    n = RWKV_N
    assert tc % SUBLANES == 0 and tc % 4 == 0 and t % tc == 0 and RWKV_HEADS * nb <= LANES
    grid = (t // tc,)
    tok = pl.BlockSpec((RWKV_PAIRS, nb, tc, LANES), lambda ti: (0, 0, ti, 0))
    par = pl.BlockSpec((n, LANES), lambda ti: (0, 0))
    tiles = pl.BlockSpec((tc, PREP_TILES, n, LANES), lambda ti: (ti, 0, 0, 0))
    ytile = pl.BlockSpec((tc, n, LANES), lambda ti: (ti, 0, 0))
    prepared = pl.pallas_call(
        functools.partial(_rwkv_prep_kernel, tc=tc, nb=nb), grid=grid,
        in_specs=[tok] * 5 + [par] * 3, out_specs=tiles,
        out_shape=jax.ShapeDtypeStruct((t, PREP_TILES, n, LANES), F32),
        compiler_params=_cparams("parallel"), name="rwkv_prep")(
            *[x.reshape(RWKV_PAIRS, nb, t, LANES) for x in (r, k, v, d, a)], kkp, kap, rkp)
    y, s = pl.pallas_call(
        functools.partial(_rwkv_state_scan_kernel, tc=tc), grid=grid,
        in_specs=[tiles, par, par],
        out_specs=[ytile, pl.BlockSpec((n, n, LANES), lambda ti: (0, 0, 0))],
        out_shape=[jax.ShapeDtypeStruct((t, n, LANES), F32), jax.ShapeDtypeStruct((n, n, LANES), F32)],
        scratch_shapes=[pltpu.VMEM((n, n, LANES), F32), pltpu.VMEM((n, LANES), F32), pltpu.VMEM((n, LANES), F32)],
        compiler_params=_cparams("arbitrary"), name="rwkv_state_scan")(prepared, lng, lnb)
    tok_out = pl.pallas_call(
        functools.partial(_rwkv_unprep_kernel, tc=tc, nb=nb), grid=grid,
        in_specs=[ytile], out_specs=tok,
        out_shape=jax.ShapeDtypeStruct((RWKV_PAIRS, nb, t, LANES), F32),
        compiler_params=_cparams("parallel"), name="rwkv_unprep")(y)
    return tok_out.reshape(RWKV_PAIRS, nb * t, LANES), s


def _rwkv_seq_scan_kernel(r_ref, k_ref, v_ref, d_ref, a_ref, kkp_ref, kap_ref, rkp_ref, lng_ref, lnb_ref,
                          y_ref, sfin_ref, s_scr, xin_scr, yout_scr, yrow_scr, sa_scr, *, tc, nb):
    n = RWKV_N
    groups = n // SUBLANES
    rows_per_pair = nb * tc
    ins = [ref.reshape(RWKV_PAIRS * rows_per_pair, LANES) for ref in (r_ref, k_ref, v_ref, d_ref, a_ref)]
    y2 = y_ref.reshape(RWKV_PAIRS * rows_per_pair, LANES)
    zero_rows = jnp.zeros((LANES - RWKV_HEADS * nb, n), F32)
    zero_lanes = jnp.zeros((LANES, LANES - n), F32)

    def load_transposed(x2, t):
        pieces = []
        for p in range(RWKV_PAIRS):
            vp = x2[pl.ds(p * rows_per_pair + t, nb, stride=tc), :]
            pieces += [vp[:, :n], vp[:, n:]]
        m = jnp.concatenate(pieces + [zero_rows], axis=0)
        return jnp.concatenate([m, zero_lanes], axis=1).T[:n, :]

    def store_token_major(y, t):
        w = jnp.concatenate([y, jnp.zeros((LANES - n, LANES), F32)], axis=0).T
        for p in range(RWKV_PAIRS):
            even = w[(2 * p) * nb:(2 * p + 1) * nb, :n]
            odd = w[(2 * p + 1) * nb:(2 * p + 2) * nb, :n]
            y2[pl.ds(p * rows_per_pair + t, nb, stride=tc), :] = jnp.concatenate([even, odd], axis=1)

    @pl.when(pl.program_id(0) == 0)
    def _():
        s_scr[...] = jnp.zeros(s_scr.shape, F32)
        yout_scr[...] = jnp.zeros(yout_scr.shape, F32)

    for q in range(5):
        xin_scr[0, q] = load_transposed(ins[q], 0)

    kkp = kkp_ref[...]
    kap = kap_ref[...]
    rkp = rkp_ref[...]
    lng = lng_ref[...]
    lnb = lnb_ref[...]
    sub = lax.broadcasted_iota(jnp.int32, (SUBLANES, LANES), 0)
    low4 = sub < 4
    low2 = (sub & 3) < 2
    low1 = (sub & 1) == 0

    def fold(x, y, dist, low):
        if dist == 4:
            return jnp.where(low, x, y) + pltpu.roll(jnp.where(low, y, x), 4, axis=0)
        return (jnp.where(low, x, pltpu.roll(y, dist, axis=0))
                + jnp.where(low, pltpu.roll(x, SUBLANES - dist, axis=0), y))

    def sublane_sums(ps):
        z = [fold(ps[0], ps[4], 4, low4), fold(ps[2], ps[6], 4, low4),
             fold(ps[1], ps[5], 4, low4), fold(ps[3], ps[7], 4, low4)]
        return fold(fold(z[0], z[1], 2, low2), fold(z[2], z[3], 2, low2), 1, low1)

    def tile_sum(x):
        acc = x[0:SUBLANES]
        for u in range(1, groups):
            acc = acc + x[u * SUBLANES:(u + 1) * SUBLANES]
        return acc

    def step(t, carry):
        slot = lax.rem(t, 2)
        other = 1 - slot
        r_t = xin_scr[slot, 0]
        k_t = xin_scr[slot, 1]
        v_t = xin_scr[slot, 2]
        d_t = xin_scr[slot, 3]
        a_t = xin_scr[slot, 4]
        kkr = k_t * kkp
        nrm = jnp.maximum(jnp.sqrt(jnp.sum(kkr * kkr, axis=0, keepdims=True)), 1e-12)
        kk = kkr * (1.0 / nrm)
        k2 = k_t * (1.0 + (a_t - 1.0) * kap)
        nkk = -kk
        b_t = kk * a_t
        for g in range(groups):
            sa_scr[g * SUBLANES:(g + 1) * SUBLANES, :] = sublane_sums(
                [tile_sum(s_scr[g * SUBLANES + u] * nkk) for u in range(SUBLANES)])
        for g in range(groups):
            ps = []
            for u in range(SUBLANES):
                i = g * SUBLANES + u
                s_n = s_scr[i] * d_t + sa_scr[pl.ds(i, 1), :] * b_t + xin_scr[slot, 2, pl.ds(i, 1), :] * k2
                s_scr[i] = s_n
                ps.append(tile_sum(s_n * r_t))
            yrow_scr[g * SUBLANES:(g + 1) * SUBLANES, :] = sublane_sums(ps)
        store_token_major(yout_scr[other], jnp.maximum(t - 1, 0))
        t_next = jnp.minimum(t + 1, tc - 1)
        for q in range(5):
            xin_scr[other, q] = load_transposed(ins[q], t_next)
        y = yrow_scr[...]
        yc = y - jnp.mean(y, axis=0, keepdims=True)
        var = jnp.mean(yc * yc, axis=0, keepdims=True)
        gn = yc * lax.rsqrt(var + RWKV_GN_EPS) * lng + lnb
        bonus = jnp.sum(r_t * k2 * rkp, axis=0, keepdims=True) * v_t
        yout_scr[slot] = gn + bonus
        return carry

    lax.fori_loop(0, tc, step, 0)
    store_token_major(yout_scr[(tc - 1) % 2], tc - 1)

    @pl.when(pl.program_id(0) == pl.num_programs(0) - 1)
    def _():
        sfin_ref[...] = s_scr[...]


def _rwkv_seq_scan(r, k, v, d, a, params, nb, t, tc):
    n = RWKV_N
    assert tc % 2 == 0 and tc % SUBLANES == 0 and t % tc == 0 and RWKV_HEADS * nb <= LANES
    seq = pl.BlockSpec((RWKV_PAIRS, nb, tc, LANES), lambda ti: (0, 0, ti, 0))
    par = pl.BlockSpec((n, LANES), lambda ti: (0, 0))
    st = pl.BlockSpec((n, n, LANES), lambda ti: (0, 0, 0))
    args = [x.reshape(RWKV_PAIRS, nb, t, LANES) for x in (r, k, v, d, a)]
    y, s = pl.pallas_call(
        functools.partial(_rwkv_seq_scan_kernel, tc=tc, nb=nb), grid=(t // tc,),
        in_specs=[seq] * 5 + [par] * 5,
        out_specs=[seq, st],
        out_shape=[jax.ShapeDtypeStruct((RWKV_PAIRS, nb, t, LANES), F32), jax.ShapeDtypeStruct((n, n, LANES), F32)],
        scratch_shapes=[pltpu.VMEM((n, n, LANES), F32), pltpu.VMEM((2, 5, n, LANES), F32),
                        pltpu.VMEM((2, n, LANES), F32), pltpu.VMEM((n, LANES), F32), pltpu.VMEM((n, LANES), F32)],
        compiler_params=_cparams("arbitrary"), name="rwkv_seq_scan")(*args, *params)
    return y.reshape(RWKV_PAIRS, nb * t, LANES), s


def _gla_proj_kernel(h_ref, wq_ref, wk_ref, wv_ref, wr_ref, wm_ref, a1_ref, a2_ref, ab_ref,
                     q_out, k_out, v_out, r_out, la_out, qm_out):
    hb = h_ref[...].astype(BF16)
    low = _dot(hb, a1_ref[...]).astype(BF16)
    for hd in range(GLA_HEADS):
        q_out[hd] = _dot(hb, wq_ref[hd])
        k_out[hd] = _dot(hb, wk_ref[hd])
        v_out[hd] = _dot(hb, wv_ref[hd])
        r_out[hd] = _dot(hb, wr_ref[hd])
        x = _dot(low, a2_ref[hd]) + ab_ref[hd]
        log_sigmoid = jnp.minimum(x, 0.0) - jnp.log(1.0 + jnp.exp(-jnp.abs(x)))
        la_out[hd] = log_sigmoid / GLA_TAU
    qm_out[...] = _dot(hb, wm_ref[...])


def _gla_weights(w_in, a1, a2, ab):
    kw, tw, nh = GLA_KW, TOK_WIDTH, GLA_HEADS

    def heads(w, d):
        return w.reshape(w.shape[0], nh, d).transpose(1, 0, 2)

    return [heads(w_in[:, :kw], GLA_DK).astype(BF16), heads(w_in[:, kw:2 * kw], GLA_DK).astype(BF16),
            heads(w_in[:, 2 * kw:2 * kw + tw], GLA_DV).astype(BF16),
            heads(w_in[:, 2 * kw + tw:3 * tw], GLA_DV).astype(BF16), w_in[:, 3 * tw:].astype(BF16),
            a1.astype(BF16), heads(a2, GLA_DK).astype(BF16), ab.reshape(nh, 1, GLA_DK)]


def _gla_proj(h, ws, tm):
    m = h.shape[0]
    nh = GLA_HEADS
    widths = [GLA_DK, GLA_DK, GLA_DV, GLA_DV, GLA_DK]
    return pl.pallas_call(
        _gla_proj_kernel, grid=(m // tm,),
        in_specs=[_row_spec(tm, D_MODEL)] + [_full_spec(w.shape) for w in ws],
        out_specs=[_head_row_spec(tm, w) for w in widths] + [_row_spec(tm, MEM_WIDTH)],
        out_shape=[jax.ShapeDtypeStruct((nh, m, w), F32) for w in widths] + [jax.ShapeDtypeStruct((m, MEM_WIDTH), F32)],
        compiler_params=_cparams("parallel"), name="gla_proj")(h, *ws)


def _gla_out_norm(o, g):
    return o * lax.rsqrt(jnp.mean(o * o, axis=-1, keepdims=True) + NORM_EPS) * g


def _gla_chunk_kernel(q_ref, k_ref, v_ref, la_ref, ng_ref, o_ref, sfin_ref, st_scr):
    c, dk = GLA_CHUNK, GLA_DK

    @pl.when(pl.program_id(1) == 0)
    def _():
        st_scr[...] = jnp.zeros(st_scr.shape, F32)

    row = lax.broadcasted_iota(jnp.int32, (c, c), 0)
    col = lax.broadcasted_iota(jnp.int32, (c, c), 1)
    tril = (row >= col).astype(F32)
    rr = lax.broadcasted_iota(jnp.int32, (c, dk), 0)
    ones_sum = jnp.ones((dk, LANES), BF16)

    for hd in range(GLA_HEADS):
        la = la_ref[hd]
        k = k_ref[hd]
        vb = v_ref[hd].astype(BF16)
        q = q_ref[hd] * (dk ** -0.5)
        b = jnp.dot(tril, la, precision=HIGHEST, preferred_element_type=F32)
        b_end = b[c - 1:c, :]
        st = st_scr[hd]
        inter = _dot_nt((q * jnp.exp(b)).astype(BF16), st.astype(BF16))

        att = jnp.zeros((c, c), F32)
        blk = c // 2
        while blk >= GLA_TILE:
            two = 2 * blk
            b_ref_rows = jnp.concatenate(
                [jnp.broadcast_to(b[s0 + blk - 1:s0 + blk, :], (two, dk)) for s0 in range(0, c, two)], axis=0)
            upper = (rr & (two - 1)) >= blk
            q_l = jnp.where(upper, q * jnp.exp(jnp.minimum(b - b_ref_rows, 0.0)), 0.0).astype(BF16)
            k_l = jnp.where(upper, 0.0, k * jnp.exp(jnp.minimum(b_ref_rows - b, 0.0))).astype(BF16)
            att = att + jnp.where((row ^ col) < two, _dot_nt(q_l, k_l), 0.0)
            blk //= 2

        prods = [(q * k).astype(BF16)]
        for dlt in range(1, GLA_TILE):
            k_s = pltpu.roll(k, dlt, axis=0)
            b_s = pltpu.roll(b, dlt, axis=0)
            p = q * k_s * jnp.exp(jnp.minimum(b - b_s, 0.0))
            prods.append(jnp.where((rr & (GLA_TILE - 1)) >= dlt, p, 0.0).astype(BF16))
        sums = _dot(jnp.concatenate(prods, axis=0), ones_sum)
        for dlt in range(GLA_TILE):
            att = att + jnp.where(col == row - dlt, sums[dlt * c:(dlt + 1) * c, :c], 0.0)

        o = inter + _dot(att.astype(BF16), vb)
        o_ref[hd] = _gla_out_norm(o, ng_ref[...])
        kd = (k * jnp.exp(b_end - b)).astype(BF16)
        st_scr[hd] = jnp.exp(b_end) * st + _dot_tn(vb, kd)

    @pl.when(pl.program_id(1) == pl.num_programs(1) - 1)
    def _():
        sfin_ref[0] = st_scr[...]


def _gla_chunk_scan(q, k, v, la, norm_g, batch, t):
    nh, dk, dv, c = GLA_HEADS, GLA_DK, GLA_DV, GLA_CHUNK
    nc = t // c
    kspec = pl.BlockSpec((nh, c, dk), lambda i, j: (0, i * nc + j, 0))
    vspec = pl.BlockSpec((nh, c, dv), lambda i, j: (0, i * nc + j, 0))
    sspec = pl.BlockSpec((1, nh, dv, dk), lambda i, j: (i, 0, 0, 0))
    return pl.pallas_call(
        _gla_chunk_kernel, grid=(batch, nc),
        in_specs=[kspec, kspec, vspec, kspec, pl.BlockSpec((1, dv), lambda i, j: (0, 0))],
        out_specs=[vspec, sspec],
        out_shape=[jax.ShapeDtypeStruct((nh, batch * t, dv), F32), jax.ShapeDtypeStruct((batch, nh, dv, dk), F32)],
        scratch_shapes=[pltpu.VMEM((nh, dv, dk), F32)],
        compiler_params=_cparams("parallel", "arbitrary"), name="gla_chunk")(q, k, v, la, norm_g.reshape(1, dv))


GLA_STEP_GROUP = 8


def _gla_step_kernel(q_ref, k_ref, v_ref, la_ref, ng_ref, s0_ref, o_ref, s_ref):
    for g in range(GLA_STEP_GROUP):
        s_new = jnp.exp(la_ref[g]) * s0_ref[g] + k_ref[g] * v_ref[g]
        s_ref[g] = s_new
        o = jnp.sum((q_ref[g] * (GLA_DK ** -0.5)) * s_new, axis=0, keepdims=True)
        o_ref[g] = _gla_out_norm(o, ng_ref[...])


def _gla_step(q, k, v, la, norm_g, s0):
    bh, dk, _ = q.shape
    dv = v.shape[2]
    g = GLA_STEP_GROUP
    cspec = pl.BlockSpec((g, dk, 1), lambda i: (i, 0, 0))
    vspec = pl.BlockSpec((g, 1, dv), lambda i: (i, 0, 0))
    sspec = pl.BlockSpec((g, dk, dv), lambda i: (i, 0, 0))
    return pl.pallas_call(
        _gla_step_kernel, grid=(bh // g,),
        in_specs=[cspec, cspec, vspec, cspec, _full_spec((1, dv)), sspec],
        out_specs=[vspec, sspec],
        out_shape=[jax.ShapeDtypeStruct((bh, 1, dv), F32), jax.ShapeDtypeStruct((bh, dk, dv), F32)],
        compiler_params=_cparams("parallel"), name="gla_step")(q, k, v, la, norm_g.reshape(1, dv), s0)


def _mem_attn_kernel(q_ref, k_ref, v_ref, o_ref):
    q = q_ref[0]
    k = k_ref[0].astype(BF16)
    v_ones = jnp.concatenate([v_ref[0].astype(BF16), jnp.ones((N_MEM, LANES), BF16)], axis=1)
    head_of_lane = lax.broadcasted_iota(jnp.int32, (1, MEM_WIDTH), 1) // MEM_HEAD_DIM
    out = jnp.zeros(q.shape, F32)
    for h in range(MEM_HEADS):
        mine = head_of_lane == h
        qh = jnp.where(mine, q, 0.0).astype(BF16)
        s = _dot_nt(qh, k) * (MEM_HEAD_DIM ** -0.5)
        e = jnp.exp(s - jnp.max(s, axis=-1, keepdims=True)).astype(BF16)
        ev = _dot(e, v_ones)
        inv = 1.0 / ev[:, MEM_WIDTH:]
        out = out + jnp.where(mine, ev[:, :MEM_WIDTH] * jnp.concatenate([inv, inv], axis=1), 0.0)
    o_ref[0] = out


def _mem_attn(q, mem_k, mem_v, layer, tq):
    b, t, w = q.shape
    qspec = pl.BlockSpec((1, tq, w), lambda i, j: (i, j, 0))
    mspec = pl.BlockSpec((None, 1, N_MEM, w), lambda i, j: (layer, i, 0, 0))
    return pl.pallas_call(
        _mem_attn_kernel, grid=(b, t // tq),
        in_specs=[qspec, mspec, mspec], out_specs=qspec,
        out_shape=jax.ShapeDtypeStruct((b, t, w), F32),
        compiler_params=_cparams("parallel", "parallel"), name="mem_attn")(q, mem_k, mem_v)


def _out_proj_kernel(*refs, layout, aliased):
    tok_ref, gate_ref, att_ref, x_ref, wo_ref, g_ref, wr_ref, br_ref = refs[:8]
    x1_out, h2_out, logit_out = refs[8 + aliased:]
    x1 = x_ref[...] + _dot(att_ref[...].astype(BF16), wo_ref[TOK_WIDTH:, :])
    if layout == "gla_heads":
        for hd in range(GLA_HEADS):
            gate = gate_ref[hd]
            mixed = (tok_ref[hd] * (gate * _sigmoid(gate))).astype(BF16)
            x1 = x1 + _dot(mixed, wo_ref[hd * GLA_DV:(hd + 1) * GLA_DV, :])
    else:
        if layout == "rwkv_pairs":
            tok = jnp.concatenate([tok_ref[p] for p in range(RWKV_PAIRS)], axis=1)
        else:
            tok = tok_ref[...]
        x1 = x1 + _dot((tok * gate_ref[...]).astype(BF16), wo_ref[:TOK_WIDTH, :])
    x1_out[...] = x1
    h2 = _rms(x1, g_ref[...])
    h2_out[...] = h2
    logit_out[...] = jnp.dot(h2, wr_ref[...], precision=HIGHEST, preferred_element_type=F32) + br_ref[...]


def _out_proj(tok, gate, att, x, ws, layout, tm, h2_rows, h2_row_offset, h2_buffer=None):
    m = x.shape[0]
    aliased = h2_buffer is not None
    rows_spec = _row_spec(tm, TOK_WIDTH)
    tok_spec, gate_spec = {
        "rows": (rows_spec, rows_spec),
        "rwkv_pairs": (pl.BlockSpec((RWKV_PAIRS, tm, LANES), lambda i: (0, i, 0)), rows_spec),
        "gla_heads": (_head_row_spec(tm, GLA_DV), _head_row_spec(tm, GLA_DV))}[layout]
    in_specs = ([tok_spec, gate_spec, _row_spec(tm, MEM_WIDTH), _row_spec(tm, D_MODEL)]
                + [_full_spec(w.shape) for w in ws])
    args = [tok, gate, att, x, *ws]
    if aliased:
        in_specs.append(pl.BlockSpec(memory_space=pl.ANY))
        args.append(h2_buffer)
    return pl.pallas_call(
        functools.partial(_out_proj_kernel, layout=layout, aliased=int(aliased)), grid=(m // tm,),
        in_specs=in_specs,
        out_specs=[_row_spec(tm, D_MODEL), _row_spec(tm, D_MODEL, h2_row_offset // tm), _row_spec(tm, ROUTER_LANES)],
        out_shape=[jax.ShapeDtypeStruct((m, D_MODEL), F32), jax.ShapeDtypeStruct((h2_rows, D_MODEL), F32),
                   jax.ShapeDtypeStruct((m, ROUTER_LANES), F32)],
        input_output_aliases={len(args) - 1: 1} if aliased else {},
        compiler_params=_cparams("parallel"), name="out_proj")(*args)


def _moe_kernel(tile_ref, exp_ref, nitem_ref, lo_ref, hi_ref, x_ref, gate_ref, wu_ref, wd_ref, o_ref, wu_scr, wd_scr):
    w = pl.program_id(0)
    prev = jnp.maximum(w - 1, 0)
    e = exp_ref[w]
    valid = w < nitem_ref[0]

    @pl.when(jnp.logical_and(valid, jnp.logical_or(w == 0, e != exp_ref[prev])))
    def _():
        wu_scr[...] = wu_ref[0, 0].astype(BF16)
        wd_scr[...] = wd_ref[0, 0].astype(BF16)

    @pl.when(jnp.logical_or(w == 0, tile_ref[w] != tile_ref[prev]))
    def _():
        o_ref[...] = jnp.zeros(o_ref.shape, F32)

    @pl.when(valid)
    def _():
        gu = _dot(x_ref[...].astype(BF16), wu_scr[...])
        g = gu[:, :EXPERT_FF]
        act = (g * _sigmoid(g) * gu[:, EXPERT_FF:]).astype(BF16)
        out = _dot(act, wd_scr[...]) * gate_ref[...]
        rows = tile_ref[w] * MOE_BLOCK + lax.broadcasted_iota(jnp.int32, (MOE_BLOCK, 1), 0)
        mine = jnp.logical_and(rows >= lo_ref[e], rows < hi_ref[e])
        o_ref[...] = o_ref[...] + jnp.where(mine, out, 0.0)


def _moe_ffn(xs, row_gate, item_tile, item_expert, n_items, lo, hi, w_up, w_down, layer):
    tm = MOE_BLOCK
    n_work = item_tile.shape[0]
    a = xs.shape[0]
    row_map = lambda w, tile, ex, ni, lo_, hi_: (tile[w], 0)
    exp_map = lambda w, tile, ex, ni, lo_, hi_: (layer, ex[w], 0, 0)
    grid_spec = pltpu.PrefetchScalarGridSpec(
        num_scalar_prefetch=5, grid=(n_work,),
        in_specs=[pl.BlockSpec((tm, D_MODEL), row_map),
                  pl.BlockSpec((tm, 1), row_map),
                  pl.BlockSpec((1, 1, D_MODEL, 2 * EXPERT_FF), exp_map),
                  pl.BlockSpec((1, 1, EXPERT_FF, D_MODEL), exp_map)],
        out_specs=pl.BlockSpec((tm, D_MODEL), row_map),
        scratch_shapes=[pltpu.VMEM((D_MODEL, 2 * EXPERT_FF), BF16), pltpu.VMEM((EXPERT_FF, D_MODEL), BF16)])
    return pl.pallas_call(
        _moe_kernel, grid_spec=grid_spec,
        out_shape=jax.ShapeDtypeStruct((a, D_MODEL), F32),
        compiler_params=_cparams("arbitrary"), name="moe_ffn")(
            item_tile, item_expert, n_items, lo, hi, xs, row_gate, w_up, w_down)


def _route(logits, n_prompt, tm):
    m = logits.shape[0]
    n_sample = m - n_prompt
    a = 2 * m
    gl = logits[:, :N_GROUPS]
    el = logits[:, N_GROUPS:N_GROUPS + N_EXPERTS].reshape(m, N_GROUPS, EXPERTS_PER_GROUP)
    group = jnp.argmax(gl, -1).astype(jnp.int32)
    p_group = jnp.max(jax.nn.softmax(gl, -1), -1, keepdims=True)
    in_group = jnp.take_along_axis(el, group[:, None, None], axis=1)[:, 0]
    top_val, top_idx = lax.top_k(in_group, 2)
    gate = p_group * jax.nn.softmax(top_val, -1)
    expert = group[:, None] * EXPERTS_PER_GROUP + top_idx.astype(jnp.int32)

    def by_id(t):
        return jnp.concatenate([t[:n_prompt, 0], t[:n_prompt, 1], t[n_prompt:, 0], t[n_prompt:, 1]])

    tok_of_id = jnp.asarray(np.concatenate([np.arange(n_prompt), np.arange(n_prompt),
                                            n_prompt + np.arange(n_sample), n_prompt + np.arange(n_sample)]), jnp.int32)
    flat_e = by_id(expert)
    ids = jnp.arange(a, dtype=jnp.int32)
    _, order, gate_sorted, tok_sorted = lax.sort((flat_e, ids, by_id(gate), tok_of_id), num_keys=1, is_stable=True)
    _, inv = lax.sort((order, ids), num_keys=1)
    experts = jnp.arange(N_EXPERTS, dtype=jnp.int32)
    counts = jnp.sum((flat_e[:, None] == experts[None, :]).astype(jnp.int32), axis=0)
    hi = jnp.cumsum(counts).astype(jnp.int32)
    lo = hi - counts
    n_tiles = a // tm
    first_tile = lo // tm
    tiles_of = jnp.where(counts > 0, (hi - 1) // tm - first_tile + 1, 0)
    item_end = jnp.cumsum(tiles_of).astype(jnp.int32)
    n_items = item_end[-1:]
    n_work = n_tiles + N_EXPERTS - 1
    w = jnp.minimum(jnp.arange(n_work, dtype=jnp.int32), n_items[0] - 1)
    item_expert = jnp.sum((item_end[None, :] <= w[:, None]).astype(jnp.int32), axis=1)
    onehot = (item_expert[:, None] == experts[None, :]).astype(jnp.int32)
    item_tile = jnp.sum(onehot * (first_tile - (item_end - tiles_of))[None, :], axis=1) + w
    return tok_sorted, gate_sorted.reshape(a, 1), inv, item_tile.astype(jnp.int32), item_expert, n_items, lo, hi


def _combine_kernel(x_ref, r0_ref, r1_ref, g_ref, x_out, h_out):
    x2 = x_ref[...] + (r0_ref[...] + r1_ref[...])
    x_out[...] = x2
    h_out[...] = _rms(x2, g_ref[...])


def _combine(x1, gathered, first_row, g, tm):
    m = x1.shape[0]
    spec = _row_spec(tm, D_MODEL)
    return pl.pallas_call(
        _combine_kernel, grid=(m // tm,),
        in_specs=[spec, _row_spec(tm, D_MODEL, first_row // tm), _row_spec(tm, D_MODEL, (first_row + m) // tm),
                  _full_spec((1, D_MODEL))],
        out_specs=[spec, spec],
        out_shape=[jax.ShapeDtypeStruct((m, D_MODEL), F32)] * 2,
        compiler_params=_cparams("parallel"), name="moe_combine")(x1, gathered, gathered, g.reshape(1, D_MODEL))


def _to_scan_layout(t2d, b, t):
    return t2d.reshape(b, t, RWKV_HEADS, RWKV_N).transpose(1, 3, 0, 2).reshape(t, RWKV_N, b * RWKV_HEADS)


def _from_scan_layout(y, b, t):
    return y.reshape(t, RWKV_N, b, RWKV_HEADS).transpose(2, 0, 3, 1).reshape(b * t, TOK_WIDTH)


def _scan_param(p, b):
    return jnp.tile(p.reshape(RWKV_HEADS, RWKV_N).T, (1, b))


def _seq_scan_param(p, b):
    lanes = jnp.repeat(p.reshape(RWKV_HEADS, RWKV_N).T, b, axis=1)
    return jnp.pad(lanes, ((0, 0), (0, LANES - RWKV_HEADS * b)))


def _state_to_scan(s):
    b = s.shape[0]
    return s.transpose(2, 3, 0, 1).reshape(RWKV_N, RWKV_N, b * RWKV_HEADS)


def _state_from_scan(s, b):
    return s.reshape(RWKV_N, RWKV_N, b, RWKV_HEADS).transpose(2, 3, 0, 1)


def kernel(x_prompt, x_sample, mem_prompt, state_rwkv_S, state_rwkv_shift, state_gla_S, cache_mem_k, cache_mem_v, norm_mix_g, norm_ffn_g, norm_mem_g, norm_final_g, w_in, w_out, w_mem_kv, rw_mu, rw_w0, rw_w1, rw_w2, rw_a0, rw_a1, rw_a2, rw_g1, rw_g2, rw_k_k, rw_k_a, rw_r_k, rw_ln_g, rw_ln_b, gla_a1, gla_a2, gla_ab, gla_norm_g, router_wg, router_bg, router_we, router_be, exp_w_up, exp_w_down):
    bp, tp, _ = x_prompt.shape
    bs, ts, _ = x_sample.shape
    assert ts == 1 and tp % GLA_CHUNK == 0 and tp % SCAN_TIME_BLOCK == 0
    np_ = bp * tp
    ns = bs * ts
    m = np_ + ns
    assert np_ % LIGHT_BLOCK == 0 and ns % SAMPLE_BLOCK == 0 and (2 * m) % MOE_BLOCK == 0
    depth = w_in.shape[0]
    nh = GLA_HEADS
    tw = TOK_WIDTH
    bf = lambda t_: t_.astype(BF16)

    mem2d = mem_prompt.reshape(bp * N_MEM, D_MODEL)
    mem_kv = [_norm_matmul(mem2d, norm_mem_g[i], w_mem_kv[i], 512) for i in range(depth)]
    pk = jnp.stack([kv[:, :MEM_WIDTH].reshape(bp, N_MEM, MEM_WIDTH) for kv in mem_kv])
    pv = jnp.stack([kv[:, MEM_WIDTH:].reshape(bp, N_MEM, MEM_WIDTH) for kv in mem_kv])
    prompt_mem_k = pk.reshape(depth, bp, N_MEM, MEM_HEADS, MEM_HEAD_DIM)
    prompt_mem_v = pv.reshape(depth, bp, N_MEM, MEM_HEADS, MEM_HEAD_DIM)
    sk = cache_mem_k.reshape(depth, bs, N_MEM, MEM_WIDTH)
    sv = cache_mem_v.reshape(depth, bs, N_MEM, MEM_WIDTH)

    x_p = x_prompt.reshape(np_, D_MODEL)
    x_s = x_sample.reshape(ns, D_MODEL)
    h_p = _norm(x_p, norm_mix_g[0], LIGHT_BLOCK)
    h_s = _norm(x_s, norm_mix_g[0], SAMPLE_BLOCK)

    w_router = jnp.zeros((depth, D_MODEL, ROUTER_LANES), F32)
    w_router = w_router.at[:, :, :N_GROUPS].set(router_wg).at[:, :, N_GROUPS:N_GROUPS + N_EXPERTS].set(router_we)
    b_router = jnp.zeros((depth, 1, ROUTER_LANES), F32)
    b_router = b_router.at[:, 0, :N_GROUPS].set(router_bg).at[:, 0, N_GROUPS:N_GROUPS + N_EXPERTS].set(router_be)

    p_rw_S, p_rw_shift, p_gla_S, s_rw_S, s_rw_shift, s_gla_S = [], [], [], [], [], []
    for i in range(depth):
        j = i // 2
        if i % 2 == 0:
            wi = w_in[i]
            ws = [rw_mu[j], bf(wi[:, :tw]), bf(wi[:, tw:2 * tw]), bf(wi[:, 2 * tw:3 * tw]), bf(wi[:, 3 * tw:]),
                  bf(rw_w1[j]), bf(rw_w2[j]), rw_w0[j].reshape(1, tw), bf(rw_a1[j]), bf(rw_a2[j]),
                  rw_a0[j].reshape(1, tw), bf(rw_g1[j]), bf(rw_g2[j])]
            *rkvda_p, gate_p, qm_p = _rwkv_proj(h_p, None, ws, PROJ_BLOCK, tp)
            *rkvda_s, gate_s, qm_s = _rwkv_proj(h_s, state_rwkv_shift[j], ws, SAMPLE_BLOCK, ts)
            pvec = [rw_k_k[j], rw_k_a[j], rw_r_k[j], rw_ln_g[j], rw_ln_b[j]]
            tok_p, sp = _rwkv_prompt_mixer(*rkvda_p, *[_seq_scan_param(p, bp) for p in pvec], bp, tp, SCAN_TIME_BLOCK)
            ys, ss = _rwkv_scan(*[_to_scan_layout(t_, bs, ts) for t_ in rkvda_s], [_scan_param(p, bs) for p in pvec],
                                _state_to_scan(state_rwkv_S[j]), LANES, 1)
            tok_s = _from_scan_layout(ys, bs, ts)
            p_rw_S.append(sp[:, :, :RWKV_HEADS * bp].reshape(RWKV_N, RWKV_N, RWKV_HEADS, bp).transpose(3, 2, 0, 1))
            s_rw_S.append(_state_from_scan(ss, bs))
            p_rw_shift.append(h_p.reshape(bp, tp, D_MODEL)[:, -1])
            s_rw_shift.append(h_s)
            layout_p, layout_s = "rwkv_pairs", "rows"
        else:
            ws = _gla_weights(w_in[i], gla_a1[j], gla_a2[j], gla_ab[j])
            q_p, k_p, v_p, gate_p, la_p, qm_p = _gla_proj(h_p, ws, PROJ_BLOCK)
            q_s, k_s, v_s, gate_s, la_s, qm_s = _gla_proj(h_s, ws, SAMPLE_BLOCK)
            tok_p, sp_t = _gla_chunk_scan(q_p, k_p, v_p, la_p, gla_norm_g[j], bp, tp)
            col = lambda t_: t_.transpose(1, 0, 2).reshape(bs * nh, GLA_DK, 1)
            os_, ss = _gla_step(col(q_s), col(k_s), v_s.transpose(1, 0, 2).reshape(bs * nh, 1, GLA_DV), col(la_s),
                                gla_norm_g[j], state_gla_S[j].reshape(bs * nh, GLA_DK, GLA_DV))
            tok_s = os_.reshape(bs, nh, GLA_DV).transpose(1, 0, 2)
            p_gla_S.append(sp_t.transpose(0, 1, 3, 2))
            s_gla_S.append(ss.reshape(bs, nh, GLA_DK, GLA_DV))
            layout_p = layout_s = "gla_heads"

        att_p = _mem_attn(qm_p.reshape(bp, tp, MEM_WIDTH), pk, pv, i, 512).reshape(np_, MEM_WIDTH)
        att_s = _mem_attn(qm_s.reshape(bs, ts, MEM_WIDTH), sk, sv, i, 1).reshape(ns, MEM_WIDTH)
        ws = [bf(w_out[i]), norm_ffn_g[i].reshape(1, D_MODEL), w_router[i], b_router[i]]
        x1_p, h2, logits_p = _out_proj(tok_p, gate_p, att_p, x_p, ws, layout_p, PROJ_BLOCK, m, 0)
        x1_s, h2, logits_s = _out_proj(tok_s, gate_s, att_s, x_s, ws, layout_s, SAMPLE_BLOCK, m, np_, h2_buffer=h2)
        tok_sorted, gate_sorted, inv, item_tile, item_expert, n_items, lo, hi = _route(
            jnp.concatenate([logits_p, logits_s], axis=0), np_, MOE_BLOCK)
        rows = _moe_ffn(h2[tok_sorted], gate_sorted, item_tile, item_expert, n_items, lo, hi,
                        exp_w_up, exp_w_down, i)
        gathered = rows[inv]
        g_next = norm_mix_g[i + 1] if i + 1 < depth else norm_final_g
        x_p, h_p = _combine(x1_p, gathered, 0, g_next, LIGHT_BLOCK)
        x_s, h_s = _combine(x1_s, gathered, 2 * np_, g_next, SAMPLE_BLOCK)

    y_prompt = h_p.reshape(bp, tp, D_MODEL)
    y_sample = h_s.reshape(bs, ts, D_MODEL)
    return (y_prompt, y_sample, jnp.stack(p_rw_S), jnp.stack(p_rw_shift), jnp.stack(p_gla_S),
            prompt_mem_k, prompt_mem_v, jnp.stack(s_rw_S), jnp.stack(s_rw_shift), jnp.stack(s_gla_S))
```

```python
import functools

import numpy as np
import jax
import jax.numpy as jnp
from jax import lax
from jax.experimental import pallas as pl
from jax.experimental.pallas import tpu as pltpu

F32 = jnp.float32
BF16 = jnp.bfloat16
HIGHEST = lax.Precision.HIGHEST

D_MODEL = 1024
TOK_WIDTH = 768
MEM_WIDTH = 256
MEM_HEADS = 4
MEM_HEAD_DIM = 64
N_MEM = 256
RWKV_HEADS = 12
RWKV_N = 64
RWKV_PAIRS = RWKV_HEADS // 2
RWKV_GN_EPS = 64e-5
GLA_HEADS = 4
GLA_KW = 384
GLA_DK = 96
GLA_DV = 192
GLA_TAU = 16.0
GLA_CHUNK = 64
GLA_TILE = 8
N_GROUPS = 4
EXPERTS_PER_GROUP = 8
N_EXPERTS = 32
EXPERT_FF = 512
NORM_EPS = 1e-6
ROUTER_LANES = 128
LANES = 128
SUBLANES = 8

PROJ_BLOCK = 512
LIGHT_BLOCK = 512
SAMPLE_BLOCK = 128
MOE_BLOCK = 256
SCAN_TIME_BLOCK = 32
VMEM_LIMIT = 56 * 1024 * 1024


def _cparams(*sem):
    return pltpu.CompilerParams(dimension_semantics=sem, vmem_limit_bytes=VMEM_LIMIT)


def _dot(a, b):
    return jnp.dot(a, b, preferred_element_type=F32)


def _dot_nt(a, b):
    return lax.dot_general(a, b, (((1,), (1,)), ((), ())), preferred_element_type=F32)


def _dot_tn(a, b):
    return lax.dot_general(a, b, (((0,), (0,)), ((), ())), preferred_element_type=F32)


def _rms(x, g):
    return x * lax.rsqrt(jnp.mean(x * x, axis=-1, keepdims=True) + NORM_EPS) * g


def _sigmoid(x):
    return 1.0 / (1.0 + jnp.exp(-x))


def _row_spec(tm, n, offset=0):
    return pl.BlockSpec((tm, n), lambda i: (i + offset, 0))


def _head_row_spec(tm, n):
    return pl.BlockSpec((GLA_HEADS, tm, n), lambda i: (0, i, 0))


def _full_spec(shape):
    nd = len(shape)
    return pl.BlockSpec(shape, lambda *_: (0,) * nd)


def _norm_kernel(x_ref, g_ref, o_ref):
    o_ref[...] = _rms(x_ref[...], g_ref[...])


def _norm(x, g, tm):
    m, d = x.shape
    return pl.pallas_call(
        _norm_kernel, grid=(m // tm,),
        in_specs=[_row_spec(tm, d), _full_spec((1, d))],
        out_specs=_row_spec(tm, d),
        out_shape=jax.ShapeDtypeStruct((m, d), F32),
        compiler_params=_cparams("parallel"), name="rms_norm")(x, g.reshape(1, d))


def _norm_matmul_kernel(x_ref, g_ref, w_ref, o_ref):
    o_ref[...] = _dot(_rms(x_ref[...], g_ref[...]).astype(BF16), w_ref[...])


def _norm_matmul(x, g, w, tm):
    m, d = x.shape
    n = w.shape[1]
    return pl.pallas_call(
        _norm_matmul_kernel, grid=(m // tm,),
        in_specs=[_row_spec(tm, d), _full_spec((1, d)), _full_spec((d, n))],
        out_specs=_row_spec(tm, n),
        out_shape=jax.ShapeDtypeStruct((m, n), F32),
        compiler_params=_cparams("parallel"), name="norm_matmul")(x, g.reshape(1, d), w.astype(BF16))


def _rwkv_proj_kernel(h_ref, hp_ref, mu_ref, wr_ref, wk_ref, wv_ref, wq_ref, w1_ref, w2_ref, w0_ref,
                      a1_ref, a2_ref, a0_ref, g1_ref, g2_ref,
                      r_out, k_out, v_out, d_out, a_out, g_out, q_out, *, blocks_per_seq):
    def put(out, val):
        if blocks_per_seq:
            for p in range(RWKV_PAIRS):
                out[p] = val[:, p * LANES:(p + 1) * LANES]
        else:
            out[...] = val

    h = h_ref[...]
    if blocks_per_seq:
        seq_start = (pl.program_id(0) % blocks_per_seq) == 0
        before = jnp.where(seq_start, 0.0, hp_ref[SUBLANES - 1:SUBLANES, :])
        row = lax.broadcasted_iota(jnp.int32, (h.shape[0], 1), 0)
        hp = jnp.where(row == 0, before, pltpu.roll(h, 1, axis=0))
    else:
        hp = hp_ref[...]
    xx = hp - h

    def mix(j):
        return (h + xx * mu_ref[j:j + 1, :]).astype(BF16)

    put(r_out, _dot(mix(0), wr_ref[...]))
    wl = w0_ref[...] + _dot(jnp.tanh(_dot(mix(1), w1_ref[...])).astype(BF16), w2_ref[...])
    put(k_out, _dot(mix(2), wk_ref[...]))
    put(v_out, _dot(mix(3), wv_ref[...]))
    al = a0_ref[...] + _dot(_dot(mix(4), a1_ref[...]).astype(BF16), a2_ref[...])
    g_out[...] = _dot(_sigmoid(_dot(mix(5), g1_ref[...])).astype(BF16), g2_ref[...])
    q_out[...] = _dot(h.astype(BF16), wq_ref[...])
    z = -wl
    softplus = jnp.maximum(z, 0.0) + jnp.log(1.0 + jnp.exp(-jnp.abs(z)))
    put(d_out, jnp.exp(-jnp.exp(-softplus - 0.5)))
    put(a_out, _sigmoid(al))


def _rwkv_proj(h, h_prev, ws, tm, seq_len):
    m = h.shape[0]
    tw = TOK_WIDTH
    if h_prev is None:
        assert seq_len % tm == 0 and tm % SUBLANES == 0
        per8 = tm // SUBLANES
        hp_spec = pl.BlockSpec((SUBLANES, D_MODEL), lambda i: (jnp.maximum(i * per8 - 1, 0), 0))
        h_prev, blocks_per_seq = h, seq_len // tm
    else:
        hp_spec, blocks_per_seq = _row_spec(tm, D_MODEL), 0
    if blocks_per_seq:
        scan_spec = pl.BlockSpec((RWKV_PAIRS, tm, LANES), lambda i: (0, i, 0))
        scan_shape = jax.ShapeDtypeStruct((RWKV_PAIRS, m, LANES), F32)
    else:
        scan_spec, scan_shape = _row_spec(tm, tw), jax.ShapeDtypeStruct((m, tw), F32)
    return pl.pallas_call(
        functools.partial(_rwkv_proj_kernel, blocks_per_seq=blocks_per_seq), grid=(m // tm,),
        in_specs=[_row_spec(tm, D_MODEL), hp_spec] + [_full_spec(w.shape) for w in ws],
        out_specs=[scan_spec] * 5 + [_row_spec(tm, tw), _row_spec(tm, MEM_WIDTH)],
        out_shape=[scan_shape] * 5 + [jax.ShapeDtypeStruct((m, tw), F32), jax.ShapeDtypeStruct((m, MEM_WIDTH), F32)],
        compiler_params=_cparams("parallel"), name="rwkv_proj")(h, h_prev, *ws)


def _pad_lanes(x):
    short = LANES - x.shape[-1]
    if short == 0:
        return x
    return jnp.concatenate([x, jnp.zeros(x.shape[:-1] + (short,), x.dtype)], axis=-1)


def _rwkv_scan_kernel(r_ref, k_ref, v_ref, d_ref, a_ref, kkp_ref, kap_ref, rkp_ref, lng_ref, lnb_ref, s0_ref,
                      y_ref, sfin_ref, s_scr, v_scr, yrow_scr, *, tc):
    n = RWKV_N
    nl = r_ref.shape[-1]

    @pl.when(pl.program_id(1) == 0)
    def _():
        s_scr[...] = _pad_lanes(s0_ref[...])

    kkp = _pad_lanes(kkp_ref[...])
    kap = _pad_lanes(kap_ref[...])
    rkp = _pad_lanes(rkp_ref[...])
    lng = _pad_lanes(lng_ref[...])
    lnb = _pad_lanes(lnb_ref[...])

    def step(t, carry):
        r_t = _pad_lanes(r_ref[t])
        k_t = _pad_lanes(k_ref[t])
        v_t = _pad_lanes(v_ref[t])
        d_t = _pad_lanes(d_ref[t])
        a_t = _pad_lanes(a_ref[t])
        v_scr[...] = v_t
        kkr = k_t * kkp
        nrm = jnp.maximum(jnp.sqrt(jnp.sum(kkr * kkr, axis=0, keepdims=True)), 1e-12)
        kk = kkr * (1.0 / nrm)
        k2 = k_t * (1.0 + (a_t - 1.0) * kap)
        nkk = -kk
        b_t = kk * a_t

        def ibody(i, c):
            s_i = s_scr[i]
            sa = jnp.sum(s_i * nkk, axis=0, keepdims=True)
            v_i = v_scr[pl.ds(i, 1), :]
            s_n = s_i * d_t + sa * b_t + v_i * k2
            s_scr[i] = s_n
            yrow_scr[pl.ds(i, 1), :] = jnp.sum(s_n * r_t, axis=0, keepdims=True)
            return c

        lax.fori_loop(0, n, ibody, 0, unroll=8)
        y = yrow_scr[...]
        yc = y - jnp.mean(y, axis=0, keepdims=True)
        var = jnp.mean(yc * yc, axis=0, keepdims=True)
        gn = yc * lax.rsqrt(var + RWKV_GN_EPS) * lng + lnb
        bonus = jnp.sum(r_t * k2 * rkp, axis=0, keepdims=True) * v_t
        y_ref[t] = (gn + bonus)[:, :nl]
        return carry

    lax.fori_loop(0, tc, step, 0)

    @pl.when(pl.program_id(1) == pl.num_programs(1) - 1)
    def _():
        sfin_ref[...] = s_scr[:, :, :nl]


def _rwkv_scan(r, k, v, d, a, params, s0, lane_block, tc):
    t, n, l = r.shape
    seq = pl.BlockSpec((tc, n, lane_block), lambda li, ti: (ti, 0, li))
    par = pl.BlockSpec((n, lane_block), lambda li, ti: (0, li))
    st = pl.BlockSpec((n, n, lane_block), lambda li, ti: (0, 0, li))
    return pl.pallas_call(
        functools.partial(_rwkv_scan_kernel, tc=tc), grid=(l // lane_block, t // tc),
        in_specs=[seq] * 5 + [par] * 5 + [st],
        out_specs=[seq, st],
        out_shape=[jax.ShapeDtypeStruct((t, n, l), F32), jax.ShapeDtypeStruct((n, n, l), F32)],
        scratch_shapes=[pltpu.VMEM((n, n, LANES), F32), pltpu.VMEM((n, LANES), F32), pltpu.VMEM((n, LANES), F32)],
        compiler_params=_cparams("parallel", "arbitrary"), name="rwkv_scan")(r, k, v, d, a, *params, s0)


PREP_TILES = 7


def _rwkv_prep_kernel(r_ref, k_ref, v_ref, d_ref, a_ref, kkp_ref, kap_ref, rkp_ref, o_ref, *, tc, nb):
    n = RWKV_N
    rows_per_pair = nb * tc
    ins = [ref.reshape(RWKV_PAIRS * rows_per_pair, LANES) for ref in (r_ref, k_ref, v_ref, d_ref, a_ref)]
    zero_rows = jnp.zeros((LANES - RWKV_HEADS * nb, LANES), F32)
    first_half = lax.broadcasted_iota(jnp.int32, (nb, LANES), 1) < n
    kkp = kkp_ref[...]
    kap = kap_ref[...]
    rkp = rkp_ref[...]

    def load_transposed_pair(x2, t):
        pieces = []
        for p in range(RWKV_PAIRS):
            now = x2[pl.ds(p * rows_per_pair + t, nb, stride=tc), :]
            nxt = x2[pl.ds(p * rows_per_pair + t + 1, nb, stride=tc), :]
            pieces += [jnp.where(first_half, now, pltpu.roll(nxt, n, axis=1)),
                       jnp.where(first_half, pltpu.roll(now, n, axis=1), nxt)]
        both = jnp.concatenate(pieces + [zero_rows], axis=0).T
        return both[:n, :], both[n:, :]

    def emit(t, r_t, k_t, v_t, d_t, a_t):
        kkr = k_t * kkp
        nrm = jnp.maximum(jnp.sqrt(jnp.sum(kkr * kkr, axis=0, keepdims=True)), 1e-12)
        kk = kkr * (1.0 / nrm)
        k2 = k_t * (1.0 + (a_t - 1.0) * kap)
        o_ref[t, 0] = -kk
        o_ref[t, 1] = d_t
        o_ref[t, 2] = kk * a_t
        o_ref[t, 3] = k2
        o_ref[t, 4] = r_t
        o_ref[t, 5] = v_t
        o_ref[t, 6] = jnp.sum(r_t * k2 * rkp, axis=0, keepdims=True) * v_t

    def two_steps(u, carry):
        t = 2 * u
        tiles = [load_transposed_pair(x2, t) for x2 in ins]
        emit(t, *[tile[0] for tile in tiles])
        emit(t + 1, *[tile[1] for tile in tiles])
        return carry

    lax.fori_loop(0, tc // 2, two_steps, 0, unroll=2)


def _rwkv_state_scan_kernel(x_ref, lng_ref, lnb_ref, y_ref, sfin_ref, s_scr, sa_scr, yrow_scr, *, tc):
    n = RWKV_N
    groups = n // SUBLANES

    @pl.when(pl.program_id(0) == 0)
    def _():
        s_scr[...] = jnp.zeros(s_scr.shape, F32)

    lng = lng_ref[...]
    lnb = lnb_ref[...]
    sub = lax.broadcasted_iota(jnp.int32, (SUBLANES, LANES), 0)
    low4 = sub < 4
    low2 = (sub & 3) < 2
    low1 = (sub & 1) == 0

    def fold(x, y, dist, low):
        if dist == 4:
            return jnp.where(low, x, y) + pltpu.roll(jnp.where(low, y, x), 4, axis=0)
        return (jnp.where(low, x, pltpu.roll(y, dist, axis=0))
                + jnp.where(low, pltpu.roll(x, SUBLANES - dist, axis=0), y))

    def sublane_sums(ps):
        z = [fold(ps[0], ps[4], 4, low4), fold(ps[2], ps[6], 4, low4),
             fold(ps[1], ps[5], 4, low4), fold(ps[3], ps[7], 4, low4)]
        return fold(fold(z[0], z[1], 2, low2), fold(z[2], z[3], 2, low2), 1, low1)

    def tile_sum(x):
        acc = x[0:SUBLANES]
        for u in range(1, groups):
            acc = acc + x[u * SUBLANES:(u + 1) * SUBLANES]
        return acc

    def step(t, carry):
        nkk = x_ref[t, 0]
        d_t = x_ref[t, 1]
        b_t = x_ref[t, 2]
        k2 = x_ref[t, 3]
        r_t = x_ref[t, 4]
        for g in range(groups):
            sa_scr[g * SUBLANES:(g + 1) * SUBLANES, :] = sublane_sums(
                [tile_sum(s_scr[g * SUBLANES + u] * nkk) for u in range(SUBLANES)])
        for g in range(groups):
            ps = []
            for u in range(SUBLANES):
                i = g * SUBLANES + u
                s_n = s_scr[i] * d_t + sa_scr[pl.ds(i, 1), :] * b_t + x_ref[t, 5, pl.ds(i, 1), :] * k2
                s_scr[i] = s_n
                ps.append(tile_sum(s_n * r_t))
            yrow_scr[g * SUBLANES:(g + 1) * SUBLANES, :] = sublane_sums(ps)
        y = yrow_scr[...]
        yc = y - jnp.mean(y, axis=0, keepdims=True)
        var = jnp.mean(yc * yc, axis=0, keepdims=True)
        y_ref[t] = yc * lax.rsqrt(var + RWKV_GN_EPS) * lng + lnb + x_ref[t, 6]
        return carry

    lax.fori_loop(0, tc, step, 0)

    @pl.when(pl.program_id(0) == pl.num_programs(0) - 1)
    def _():
        sfin_ref[...] = s_scr[...]


def _rwkv_unprep_kernel(y_ref, o_ref, *, tc, nb):
    n = RWKV_N
    rows_per_pair = nb * tc
    o2 = o_ref.reshape(RWKV_PAIRS * rows_per_pair, LANES)
    first_half = lax.broadcasted_iota(jnp.int32, (nb, LANES), 1) < n

    def two_steps(u, carry):
        t = 2 * u
        w = jnp.concatenate([y_ref[t], y_ref[t + 1]], axis=0).T
        for p in range(RWKV_PAIRS):
            even = w[(2 * p) * nb:(2 * p + 1) * nb, :]
            odd = w[(2 * p + 1) * nb:(2 * p + 2) * nb, :]
            o2[pl.ds(p * rows_per_pair + t, nb, stride=tc), :] = jnp.where(
                first_half, even, pltpu.roll(odd, n, axis=1))
            o2[pl.ds(p * rows_per_pair + t + 1, nb, stride=tc), :] = jnp.where(
                first_half, pltpu.roll(even, n, axis=1), odd)
        return carry

    lax.fori_loop(0, tc // 2, two_steps, 0, unroll=2)


def _rwkv_prompt_mixer(r, k, v, d, a, kkp, kap, rkp, lng, lnb, nb, t, tc):
    n = RWKV_N
    assert tc % SUBLANES == 0 and tc % 4 == 0 and t % tc == 0 and RWKV_HEADS * nb <= LANES
    grid = (t // tc,)
    tok = pl.BlockSpec((RWKV_PAIRS, nb, tc, LANES), lambda ti: (0, 0, ti, 0))
    par = pl.BlockSpec((n, LANES), lambda ti: (0, 0))
    tiles = pl.BlockSpec((tc, PREP_TILES, n, LANES), lambda ti: (ti, 0, 0, 0))
    ytile = pl.BlockSpec((tc, n, LANES), lambda ti: (ti, 0, 0))
    prepared = pl.pallas_call(
        functools.partial(_rwkv_prep_kernel, tc=tc, nb=nb), grid=grid,
        in_specs=[tok] * 5 + [par] * 3, out_specs=tiles,
        out_shape=jax.ShapeDtypeStruct((t, PREP_TILES, n, LANES), F32),
        compiler_params=_cparams("parallel"), name="rwkv_prep")(
            *[x.reshape(RWKV_PAIRS, nb, t, LANES) for x in (r, k, v, d, a)], kkp, kap, rkp)
    y, s = pl.pallas_call(
        functools.partial(_rwkv_state_scan_kernel, tc=tc), grid=grid,
        in_specs=[tiles, par, par],
        out_specs=[ytile, pl.BlockSpec((n, n, LANES), lambda ti: (0, 0, 0))],
        out_shape=[jax.ShapeDtypeStruct((t, n, LANES), F32), jax.ShapeDtypeStruct((n, n, LANES), F32)],
        scratch_shapes=[pltpu.VMEM((n, n, LANES), F32), pltpu.VMEM((n, LANES), F32), pltpu.VMEM((n, LANES), F32)],
        compiler_params=_cparams("arbitrary"), name="rwkv_state_scan")(prepared, lng, lnb)
    tok_out = pl.pallas_call(
        functools.partial(_rwkv_unprep_kernel, tc=tc, nb=nb), grid=grid,
        in_specs=[ytile], out_specs=tok,
        out_shape=jax.ShapeDtypeStruct((RWKV_PAIRS, nb, t, LANES), F32),
        compiler_params=_cparams("parallel"), name="rwkv_unprep")(y)
    return tok_out.reshape(RWKV_PAIRS, nb * t, LANES), s


def _rwkv_seq_scan_kernel(r_ref, k_ref, v_ref, d_ref, a_ref, kkp_ref, kap_ref, rkp_ref, lng_ref, lnb_ref,
                          y_ref, sfin_ref, s_scr, xin_scr, yout_scr, yrow_scr, sa_scr, *, tc, nb):
    n = RWKV_N
    groups = n // SUBLANES
    rows_per_pair = nb * tc
    ins = [ref.reshape(RWKV_PAIRS * rows_per_pair, LANES) for ref in (r_ref, k_ref, v_ref, d_ref, a_ref)]
    y2 = y_ref.reshape(RWKV_PAIRS * rows_per_pair, LANES)
    zero_rows = jnp.zeros((LANES - RWKV_HEADS * nb, n), F32)
    zero_lanes = jnp.zeros((LANES, LANES - n), F32)

    def load_transposed(x2, t):
        pieces = []
        for p in range(RWKV_PAIRS):
            vp = x2[pl.ds(p * rows_per_pair + t, nb, stride=tc), :]
            pieces += [vp[:, :n], vp[:, n:]]
        m = jnp.concatenate(pieces + [zero_rows], axis=0)
        return jnp.concatenate([m, zero_lanes], axis=1).T[:n, :]

    def store_token_major(y, t):
        w = jnp.concatenate([y, jnp.zeros((LANES - n, LANES), F32)], axis=0).T
        for p in range(RWKV_PAIRS):
            even = w[(2 * p) * nb:(2 * p + 1) * nb, :n]
            odd = w[(2 * p + 1) * nb:(2 * p + 2) * nb, :n]
            y2[pl.ds(p * rows_per_pair + t, nb, stride=tc), :] = jnp.concatenate([even, odd], axis=1)

    @pl.when(pl.program_id(0) == 0)
    def _():
        s_scr[...] = jnp.zeros(s_scr.shape, F32)
        yout_scr[...] = jnp.zeros(yout_scr.shape, F32)

    for q in range(5):
        xin_scr[0, q] = load_transposed(ins[q], 0)

    kkp = kkp_ref[...]
    kap = kap_ref[...]
    rkp = rkp_ref[...]
    lng = lng_ref[...]
    lnb = lnb_ref[...]
    sub = lax.broadcasted_iota(jnp.int32, (SUBLANES, LANES), 0)
    low4 = sub < 4
    low2 = (sub & 3) < 2
    low1 = (sub & 1) == 0

    def fold(x, y, dist, low):
        if dist == 4:
            return jnp.where(low, x, y) + pltpu.roll(jnp.where(low, y, x), 4, axis=0)
        return (jnp.where(low, x, pltpu.roll(y, dist, axis=0))
                + jnp.where(low, pltpu.roll(x, SUBLANES - dist, axis=0), y))

    def sublane_sums(ps):
        z = [fold(ps[0], ps[4], 4, low4), fold(ps[2], ps[6], 4, low4),
             fold(ps[1], ps[5], 4, low4), fold(ps[3], ps[7], 4, low4)]
        return fold(fold(z[0], z[1], 2, low2), fold(z[2], z[3], 2, low2), 1, low1)

    def tile_sum(x):
        acc = x[0:SUBLANES]
        for u in range(1, groups):
            acc = acc + x[u * SUBLANES:(u + 1) * SUBLANES]
        return acc

    def step(t, carry):
        slot = lax.rem(t, 2)
        other = 1 - slot
        r_t = xin_scr[slot, 0]
        k_t = xin_scr[slot, 1]
        v_t = xin_scr[slot, 2]
        d_t = xin_scr[slot, 3]
        a_t = xin_scr[slot, 4]
        kkr = k_t * kkp
        nrm = jnp.maximum(jnp.sqrt(jnp.sum(kkr * kkr, axis=0, keepdims=True)), 1e-12)
        kk = kkr * (1.0 / nrm)
        k2 = k_t * (1.0 + (a_t - 1.0) * kap)
        nkk = -kk
        b_t = kk * a_t
        for g in range(groups):
            sa_scr[g * SUBLANES:(g + 1) * SUBLANES, :] = sublane_sums(
                [tile_sum(s_scr[g * SUBLANES + u] * nkk) for u in range(SUBLANES)])
        for g in range(groups):
            ps = []
            for u in range(SUBLANES):
                i = g * SUBLANES + u
                s_n = s_scr[i] * d_t + sa_scr[pl.ds(i, 1), :] * b_t + xin_scr[slot, 2, pl.ds(i, 1), :] * k2
                s_scr[i] = s_n
                ps.append(tile_sum(s_n * r_t))
            yrow_scr[g * SUBLANES:(g + 1) * SUBLANES, :] = sublane_sums(ps)
        store_token_major(yout_scr[other], jnp.maximum(t - 1, 0))
        t_next = jnp.minimum(t + 1, tc - 1)
        for q in range(5):
            xin_scr[other, q] = load_transposed(ins[q], t_next)
        y = yrow_scr[...]
        yc = y - jnp.mean(y, axis=0, keepdims=True)
        var = jnp.mean(yc * yc, axis=0, keepdims=True)
        gn = yc * lax.rsqrt(var + RWKV_GN_EPS) * lng + lnb
        bonus = jnp.sum(r_t * k2 * rkp, axis=0, keepdims=True) * v_t
        yout_scr[slot] = gn + bonus
        return carry

    lax.fori_loop(0, tc, step, 0)
    store_token_major(yout_scr[(tc - 1) % 2], tc - 1)

    @pl.when(pl.program_id(0) == pl.num_programs(0) - 1)
    def _():
        sfin_ref[...] = s_scr[...]


def _rwkv_seq_scan(r, k, v, d, a, params, nb, t, tc):
    n = RWKV_N
    assert tc % 2 == 0 and tc % SUBLANES == 0 and t % tc == 0 and RWKV_HEADS * nb <= LANES
    seq = pl.BlockSpec((RWKV_PAIRS, nb, tc, LANES), lambda ti: (0, 0, ti, 0))
    par = pl.BlockSpec((n, LANES), lambda ti: (0, 0))
    st = pl.BlockSpec((n, n, LANES), lambda ti: (0, 0, 0))
    args = [x.reshape(RWKV_PAIRS, nb, t, LANES) for x in (r, k, v, d, a)]
    y, s = pl.pallas_call(
        functools.partial(_rwkv_seq_scan_kernel, tc=tc, nb=nb), grid=(t // tc,),
        in_specs=[seq] * 5 + [par] * 5,
        out_specs=[seq, st],
        out_shape=[jax.ShapeDtypeStruct((RWKV_PAIRS, nb, t, LANES), F32), jax.ShapeDtypeStruct((n, n, LANES), F32)],
        scratch_shapes=[pltpu.VMEM((n, n, LANES), F32), pltpu.VMEM((2, 5, n, LANES), F32),
                        pltpu.VMEM((2, n, LANES), F32), pltpu.VMEM((n, LANES), F32), pltpu.VMEM((n, LANES), F32)],
        compiler_params=_cparams("arbitrary"), name="rwkv_seq_scan")(*args, *params)
    return y.reshape(RWKV_PAIRS, nb * t, LANES), s


def _gla_proj_kernel(h_ref, wq_ref, wk_ref, wv_ref, wr_ref, wm_ref, a1_ref, a2_ref, ab_ref,
                     q_out, k_out, v_out, r_out, la_out, qm_out):
    hb = h_ref[...].astype(BF16)
    low = _dot(hb, a1_ref[...]).astype(BF16)
    for hd in range(GLA_HEADS):
        q_out[hd] = _dot(hb, wq_ref[hd])
        k_out[hd] = _dot(hb, wk_ref[hd])
        v_out[hd] = _dot(hb, wv_ref[hd])
        r_out[hd] = _dot(hb, wr_ref[hd])
        x = _dot(low, a2_ref[hd]) + ab_ref[hd]
        log_sigmoid = jnp.minimum(x, 0.0) - jnp.log(1.0 + jnp.exp(-jnp.abs(x)))
        la_out[hd] = log_sigmoid / GLA_TAU
    qm_out[...] = _dot(hb, wm_ref[...])


def _gla_weights(w_in, a1, a2, ab):
    kw, tw, nh = GLA_KW, TOK_WIDTH, GLA_HEADS

    def heads(w, d):
        return w.reshape(w.shape[0], nh, d).transpose(1, 0, 2)

    return [heads(w_in[:, :kw], GLA_DK).astype(BF16), heads(w_in[:, kw:2 * kw], GLA_DK).astype(BF16),
            heads(w_in[:, 2 * kw:2 * kw + tw], GLA_DV).astype(BF16),
            heads(w_in[:, 2 * kw + tw:3 * tw], GLA_DV).astype(BF16), w_in[:, 3 * tw:].astype(BF16),
            a1.astype(BF16), heads(a2, GLA_DK).astype(BF16), ab.reshape(nh, 1, GLA_DK)]


def _gla_proj(h, ws, tm):
    m = h.shape[0]
    nh = GLA_HEADS
    widths = [GLA_DK, GLA_DK, GLA_DV, GLA_DV, GLA_DK]
    return pl.pallas_call(
        _gla_proj_kernel, grid=(m // tm,),
        in_specs=[_row_spec(tm, D_MODEL)] + [_full_spec(w.shape) for w in ws],
        out_specs=[_head_row_spec(tm, w) for w in widths] + [_row_spec(tm, MEM_WIDTH)],
        out_shape=[jax.ShapeDtypeStruct((nh, m, w), F32) for w in widths] + [jax.ShapeDtypeStruct((m, MEM_WIDTH), F32)],
        compiler_params=_cparams("parallel"), name="gla_proj")(h, *ws)


def _gla_out_norm(o, g):
    return o * lax.rsqrt(jnp.mean(o * o, axis=-1, keepdims=True) + NORM_EPS) * g


def _gla_chunk_kernel(q_ref, k_ref, v_ref, la_ref, ng_ref, o_ref, sfin_ref, st_scr):
    c, dk = GLA_CHUNK, GLA_DK

    @pl.when(pl.program_id(1) == 0)
    def _():
        st_scr[...] = jnp.zeros(st_scr.shape, F32)

    row = lax.broadcasted_iota(jnp.int32, (c, c), 0)
    col = lax.broadcasted_iota(jnp.int32, (c, c), 1)
    tril = (row >= col).astype(F32)
    rr = lax.broadcasted_iota(jnp.int32, (c, dk), 0)
    ones_sum = jnp.ones((dk, LANES), BF16)

    heads = range(GLA_HEADS)
    tril_b = tril.astype(BF16)

    def cumsum_rows(la):
        hi = la.astype(BF16)
        r1 = la - hi.astype(F32)
        mid = r1.astype(BF16)
        lo = (r1 - mid.astype(F32)).astype(BF16)
        return _dot(tril_b, hi) + _dot(tril_b, mid) + _dot(tril_b, lo)

    def tile_roll(x, dlt):
        return pltpu.roll(x.reshape(c // GLA_TILE, GLA_TILE, dk), dlt, axis=1).reshape(c, dk)

    k = [k_ref[hd] for hd in heads]
    vb = [v_ref[hd].astype(BF16) for hd in heads]
    q = [q_ref[hd] * (dk ** -0.5) for hd in heads]
    b = [cumsum_rows(la_ref[hd]) for hd in heads]
    st = [st_scr[hd] for hd in heads]
    inter = [_dot_nt((q[hd] * jnp.exp(b[hd])).astype(BF16), st[hd].astype(BF16)) for hd in heads]

    att = [jnp.zeros((c, c), F32) for hd in heads]
    blk = c // 2
    while blk >= GLA_TILE:
        two = 2 * blk
        upper = (rr & (two - 1)) >= blk
        same_block = (row ^ col) < two
        parts = []
        for hd in heads:
            b_ref_rows = jnp.concatenate(
                [jnp.broadcast_to(b[hd][s0 + blk - 1:s0 + blk, :], (two, dk)) for s0 in range(0, c, two)], axis=0)
            q_l = jnp.where(upper, q[hd] * jnp.exp(jnp.minimum(b[hd] - b_ref_rows, 0.0)), 0.0).astype(BF16)
            k_l = jnp.where(upper, 0.0, k[hd] * jnp.exp(jnp.minimum(b_ref_rows - b[hd], 0.0))).astype(BF16)
            parts.append(_dot_nt(q_l, k_l))
        att = [att[hd] + jnp.where(same_block, parts[hd], 0.0) for hd in heads]
        blk //= 2

    sums = []
    for hd in heads:
        prods = [(q[hd] * k[hd]).astype(BF16)]
        for dlt in range(1, GLA_TILE):
            p = q[hd] * tile_roll(k[hd], dlt) * jnp.exp(jnp.minimum(b[hd] - tile_roll(b[hd], dlt), 0.0))
            prods.append(jnp.where((rr & (GLA_TILE - 1)) >= dlt, p, 0.0).astype(BF16))
        sums.append(_dot(jnp.concatenate(prods, axis=0), ones_sum))
    for dlt in range(GLA_TILE):
        on_diag = col == row - dlt
        att = [att[hd] + jnp.where(on_diag, sums[hd][dlt * c:(dlt + 1) * c, :c], 0.0) for hd in heads]

    o = [inter[hd] + _dot(att[hd].astype(BF16), vb[hd]) for hd in heads]
    for hd in heads:
        o_ref[hd] = _gla_out_norm(o[hd], ng_ref[...])
    for hd in heads:
        b_end = b[hd][c - 1:c, :]
        kd = (k[hd] * jnp.exp(b_end - b[hd])).astype(BF16)
        st_scr[hd] = jnp.exp(b_end) * st[hd] + _dot_tn(vb[hd], kd)

    @pl.when(pl.program_id(1) == pl.num_programs(1) - 1)
    def _():
        sfin_ref[0] = st_scr[...]


def _gla_chunk_scan(q, k, v, la, norm_g, batch, t):
    nh, dk, dv, c = GLA_HEADS, GLA_DK, GLA_DV, GLA_CHUNK
    nc = t // c
    kspec = pl.BlockSpec((nh, c, dk), lambda i, j: (0, i * nc + j, 0))
    vspec = pl.BlockSpec((nh, c, dv), lambda i, j: (0, i * nc + j, 0))
    sspec = pl.BlockSpec((1, nh, dv, dk), lambda i, j: (i, 0, 0, 0))
    return pl.pallas_call(
        _gla_chunk_kernel, grid=(batch, nc),
        in_specs=[kspec, kspec, vspec, kspec, pl.BlockSpec((1, dv), lambda i, j: (0, 0))],
        out_specs=[vspec, sspec],
        out_shape=[jax.ShapeDtypeStruct((nh, batch * t, dv), F32), jax.ShapeDtypeStruct((batch, nh, dv, dk), F32)],
        scratch_shapes=[pltpu.VMEM((nh, dv, dk), F32)],
        compiler_params=_cparams("parallel", "arbitrary"), name="gla_chunk")(q, k, v, la, norm_g.reshape(1, dv))


GLA_STEP_GROUP = 8


def _gla_step_kernel(q_ref, k_ref, v_ref, la_ref, ng_ref, s0_ref, o_ref, s_ref):
    for g in range(GLA_STEP_GROUP):
        kv = k_ref[g].astype(BF16).astype(F32) * v_ref[g].astype(BF16).astype(F32)
        s_new = jnp.exp(la_ref[g]) * s0_ref[g] + kv
        s_ref[g] = s_new
        o = jnp.sum((q_ref[g] * (GLA_DK ** -0.5)) * s_new, axis=0, keepdims=True)
        o_ref[g] = _gla_out_norm(o, ng_ref[...])


def _gla_step(q, k, v, la, norm_g, s0):
    bh, dk, _ = q.shape
    dv = v.shape[2]
    g = GLA_STEP_GROUP
    cspec = pl.BlockSpec((g, dk, 1), lambda i: (i, 0, 0))
    vspec = pl.BlockSpec((g, 1, dv), lambda i: (i, 0, 0))
    sspec = pl.BlockSpec((g, dk, dv), lambda i: (i, 0, 0))
    return pl.pallas_call(
        _gla_step_kernel, grid=(bh // g,),
        in_specs=[cspec, cspec, vspec, cspec, _full_spec((1, dv)), sspec],
        out_specs=[vspec, sspec],
        out_shape=[jax.ShapeDtypeStruct((bh, 1, dv), F32), jax.ShapeDtypeStruct((bh, dk, dv), F32)],
        compiler_params=_cparams("parallel"), name="gla_step")(q, k, v, la, norm_g.reshape(1, dv), s0)


def _mem_attn_kernel(q_ref, k_ref, v_ref, o_ref):
    q = q_ref[0]
    k = k_ref[0].astype(BF16)
    v_ones = jnp.concatenate([v_ref[0].astype(BF16), jnp.ones((N_MEM, LANES), BF16)], axis=1)
    head_of_lane = lax.broadcasted_iota(jnp.int32, (1, MEM_WIDTH), 1) // MEM_HEAD_DIM
    heads = range(MEM_HEADS)
    mine = [head_of_lane == h for h in heads]
    qh = [jnp.where(mine[h], q, 0.0).astype(BF16) for h in heads]
    s = [_dot_nt(qh[h], k) * (MEM_HEAD_DIM ** -0.5) for h in heads]
    e = [jnp.exp(s[h] - jnp.max(s[h], axis=-1, keepdims=True)).astype(BF16) for h in heads]
    ev = [_dot(e[h], v_ones) for h in heads]
    out = jnp.zeros(q.shape, F32)
    for h in heads:
        inv = 1.0 / ev[h][:, MEM_WIDTH:]
        out = out + jnp.where(mine[h], ev[h][:, :MEM_WIDTH] * jnp.concatenate([inv, inv], axis=1), 0.0)
    o_ref[0] = out


def _mem_attn(q, mem_k, mem_v, layer, tq):
    b, t, w = q.shape
    qspec = pl.BlockSpec((1, tq, w), lambda i, j: (i, j, 0))
    mspec = pl.BlockSpec((None, 1, N_MEM, w), lambda i, j: (layer, i, 0, 0))
    return pl.pallas_call(
        _mem_attn_kernel, grid=(b, t // tq),
        in_specs=[qspec, mspec, mspec], out_specs=qspec,
        out_shape=jax.ShapeDtypeStruct((b, t, w), F32),
        compiler_params=_cparams("parallel", "parallel"), name="mem_attn")(q, mem_k, mem_v)


def _out_proj_kernel(*refs, layout, aliased):
    tok_ref, gate_ref, att_ref, x_ref, wo_ref, g_ref, wr_ref, br_ref = refs[:8]
    x1_out, h2_out, logit_out = refs[8 + aliased:]
    x1 = x_ref[...] + _dot(att_ref[...].astype(BF16), wo_ref[TOK_WIDTH:, :])
    if layout == "gla_heads":
        for hd in range(GLA_HEADS):
            gate = gate_ref[hd]
            mixed = (tok_ref[hd] * (gate * _sigmoid(gate))).astype(BF16)
            x1 = x1 + _dot(mixed, wo_ref[hd * GLA_DV:(hd + 1) * GLA_DV, :])
    else:
        if layout == "rwkv_pairs":
            tok = jnp.concatenate([tok_ref[p] for p in range(RWKV_PAIRS)], axis=1)
        else:
            tok = tok_ref[...]
        x1 = x1 + _dot((tok * gate_ref[...]).astype(BF16), wo_ref[:TOK_WIDTH, :])
    x1_out[...] = x1
    h2 = _rms(x1, g_ref[...])
    h2_out[...] = h2
    logit_out[...] = jnp.dot(h2, wr_ref[...], precision=HIGHEST, preferred_element_type=F32) + br_ref[...]


def _out_proj(tok, gate, att, x, ws, layout, tm, h2_rows, h2_row_offset, h2_buffer=None):
    m = x.shape[0]
    aliased = h2_buffer is not None
    rows_spec = _row_spec(tm, TOK_WIDTH)
    tok_spec, gate_spec = {
        "rows": (rows_spec, rows_spec),
        "rwkv_pairs": (pl.BlockSpec((RWKV_PAIRS, tm, LANES), lambda i: (0, i, 0)), rows_spec),
        "gla_heads": (_head_row_spec(tm, GLA_DV), _head_row_spec(tm, GLA_DV))}[layout]
    in_specs = ([tok_spec, gate_spec, _row_spec(tm, MEM_WIDTH), _row_spec(tm, D_MODEL)]
                + [_full_spec(w.shape) for w in ws])
    args = [tok, gate, att, x, *ws]
    if aliased:
        in_specs.append(pl.BlockSpec(memory_space=pl.ANY))
        args.append(h2_buffer)
    return pl.pallas_call(
        functools.partial(_out_proj_kernel, layout=layout, aliased=int(aliased)), grid=(m // tm,),
        in_specs=in_specs,
        out_specs=[_row_spec(tm, D_MODEL), _row_spec(tm, D_MODEL, h2_row_offset // tm), _row_spec(tm, ROUTER_LANES)],
        out_shape=[jax.ShapeDtypeStruct((m, D_MODEL), F32), jax.ShapeDtypeStruct((h2_rows, D_MODEL), F32),
                   jax.ShapeDtypeStruct((m, ROUTER_LANES), F32)],
        input_output_aliases={len(args) - 1: 1} if aliased else {},
        compiler_params=_cparams("parallel"), name="out_proj")(*args)


def _moe_kernel(tile_ref, exp_ref, nitem_ref, lo_ref, hi_ref, x_ref, gate_ref, wu_ref, wd_ref, o_ref, wu_scr, wd_scr):
    w = pl.program_id(0)
    prev = jnp.maximum(w - 1, 0)
    e = exp_ref[w]
    valid = w < nitem_ref[0]

    @pl.when(jnp.logical_and(valid, jnp.logical_or(w == 0, e != exp_ref[prev])))
    def _():
        wu_scr[...] = wu_ref[0, 0].astype(BF16)
        wd_scr[...] = wd_ref[0, 0].astype(BF16)

    @pl.when(jnp.logical_or(w == 0, tile_ref[w] != tile_ref[prev]))
    def _():
        o_ref[...] = jnp.zeros(o_ref.shape, F32)

    @pl.when(valid)
    def _():
        gu = _dot(x_ref[...].astype(BF16), wu_scr[...])
        g = gu[:, :EXPERT_FF]
        act = (g * _sigmoid(g) * gu[:, EXPERT_FF:]).astype(BF16)
        out = _dot(act, wd_scr[...]) * gate_ref[...]
        rows = tile_ref[w] * MOE_BLOCK + lax.broadcasted_iota(jnp.int32, (MOE_BLOCK, 1), 0)
        mine = jnp.logical_and(rows >= lo_ref[e], rows < hi_ref[e])
        o_ref[...] = o_ref[...] + jnp.where(mine, out, 0.0)


def _moe_ffn(xs, row_gate, item_tile, item_expert, n_items, lo, hi, w_up, w_down, layer):
    tm = MOE_BLOCK
    n_work = item_tile.shape[0]
    a = xs.shape[0]
    row_map = lambda w, tile, ex, ni, lo_, hi_: (tile[w], 0)
    exp_map = lambda w, tile, ex, ni, lo_, hi_: (layer, ex[w], 0, 0)
    grid_spec = pltpu.PrefetchScalarGridSpec(
        num_scalar_prefetch=5, grid=(n_work,),
        in_specs=[pl.BlockSpec((tm, D_MODEL), row_map),
                  pl.BlockSpec((tm, 1), row_map),
                  pl.BlockSpec((1, 1, D_MODEL, 2 * EXPERT_FF), exp_map),
                  pl.BlockSpec((1, 1, EXPERT_FF, D_MODEL), exp_map)],
        out_specs=pl.BlockSpec((tm, D_MODEL), row_map),
        scratch_shapes=[pltpu.VMEM((D_MODEL, 2 * EXPERT_FF), BF16), pltpu.VMEM((EXPERT_FF, D_MODEL), BF16)])
    return pl.pallas_call(
        _moe_kernel, grid_spec=grid_spec,
        out_shape=jax.ShapeDtypeStruct((a, D_MODEL), F32),
        compiler_params=_cparams("arbitrary"), name="moe_ffn")(
            item_tile, item_expert, n_items, lo, hi, xs, row_gate, w_up, w_down)


def _route(logits, n_prompt, tm):
    m = logits.shape[0]
    n_sample = m - n_prompt
    a = 2 * m
    gl = logits[:, :N_GROUPS]
    el = logits[:, N_GROUPS:N_GROUPS + N_EXPERTS].reshape(m, N_GROUPS, EXPERTS_PER_GROUP)
    group = jnp.argmax(gl, -1).astype(jnp.int32)
    p_group = jnp.max(jax.nn.softmax(gl, -1), -1, keepdims=True)
    in_group = jnp.take_along_axis(el, group[:, None, None], axis=1)[:, 0]
    top_val, top_idx = lax.top_k(in_group, 2)
    gate = p_group * jax.nn.softmax(top_val, -1)
    expert = group[:, None] * EXPERTS_PER_GROUP + top_idx.astype(jnp.int32)

    def by_id(t):
        return jnp.concatenate([t[:n_prompt, 0], t[:n_prompt, 1], t[n_prompt:, 0], t[n_prompt:, 1]])

    tok_of_id = jnp.asarray(np.concatenate([np.arange(n_prompt), np.arange(n_prompt),
                                            n_prompt + np.arange(n_sample), n_prompt + np.arange(n_sample)]), jnp.int32)
    flat_e = by_id(expert)
    ids = jnp.arange(a, dtype=jnp.int32)
    _, order, gate_sorted, tok_sorted = lax.sort((flat_e, ids, by_id(gate), tok_of_id), num_keys=1, is_stable=True)
    _, inv = lax.sort((order, ids), num_keys=1)
    experts = jnp.arange(N_EXPERTS, dtype=jnp.int32)
    counts = jnp.sum((flat_e[:, None] == experts[None, :]).astype(jnp.int32), axis=0)
    hi = jnp.cumsum(counts).astype(jnp.int32)
    lo = hi - counts
    n_tiles = a // tm
    first_tile = lo // tm
    tiles_of = jnp.where(counts > 0, (hi - 1) // tm - first_tile + 1, 0)
    item_end = jnp.cumsum(tiles_of).astype(jnp.int32)
    n_items = item_end[-1:]
    n_work = n_tiles + N_EXPERTS - 1
    w = jnp.minimum(jnp.arange(n_work, dtype=jnp.int32), n_items[0] - 1)
    item_expert = jnp.sum((item_end[None, :] <= w[:, None]).astype(jnp.int32), axis=1)
    onehot = (item_expert[:, None] == experts[None, :]).astype(jnp.int32)
    item_tile = jnp.sum(onehot * (first_tile - (item_end - tiles_of))[None, :], axis=1) + w
    return tok_sorted, gate_sorted.reshape(a, 1), inv, item_tile.astype(jnp.int32), item_expert, n_items, lo, hi


def _combine_kernel(x_ref, r0_ref, r1_ref, g_ref, x_out, h_out):
    x2 = x_ref[...] + (r0_ref[...] + r1_ref[...])
    x_out[...] = x2
    h_out[...] = _rms(x2, g_ref[...])


def _combine(x1, gathered, first_row, g, tm):
    m = x1.shape[0]
    spec = _row_spec(tm, D_MODEL)
    return pl.pallas_call(
        _combine_kernel, grid=(m // tm,),
        in_specs=[spec, _row_spec(tm, D_MODEL, first_row // tm), _row_spec(tm, D_MODEL, (first_row + m) // tm),
                  _full_spec((1, D_MODEL))],
        out_specs=[spec, spec],
        out_shape=[jax.ShapeDtypeStruct((m, D_MODEL), F32)] * 2,
        compiler_params=_cparams("parallel"), name="moe_combine")(x1, gathered, gathered, g.reshape(1, D_MODEL))


def _to_scan_layout(t2d, b, t):
    return t2d.reshape(b, t, RWKV_HEADS, RWKV_N).transpose(1, 3, 0, 2).reshape(t, RWKV_N, b * RWKV_HEADS)


def _from_scan_layout(y, b, t):
    return y.reshape(t, RWKV_N, b, RWKV_HEADS).transpose(2, 0, 3, 1).reshape(b * t, TOK_WIDTH)


def _scan_param(p, b):
    return jnp.tile(p.reshape(RWKV_HEADS, RWKV_N).T, (1, b))


def _seq_scan_param(p, b):
    lanes = jnp.repeat(p.reshape(RWKV_HEADS, RWKV_N).T, b, axis=1)
    return jnp.pad(lanes, ((0, 0), (0, LANES - RWKV_HEADS * b)))


def _state_to_scan(s):
    b = s.shape[0]
    return s.transpose(2, 3, 0, 1).reshape(RWKV_N, RWKV_N, b * RWKV_HEADS)


def _state_from_scan(s, b):
    return s.reshape(RWKV_N, RWKV_N, b, RWKV_HEADS).transpose(2, 3, 0, 1)


def kernel(x_prompt, x_sample, mem_prompt, state_rwkv_S, state_rwkv_shift, state_gla_S, cache_mem_k, cache_mem_v, norm_mix_g, norm_ffn_g, norm_mem_g, norm_final_g, w_in, w_out, w_mem_kv, rw_mu, rw_w0, rw_w1, rw_w2, rw_a0, rw_a1, rw_a2, rw_g1, rw_g2, rw_k_k, rw_k_a, rw_r_k, rw_ln_g, rw_ln_b, gla_a1, gla_a2, gla_ab, gla_norm_g, router_wg, router_bg, router_we, router_be, exp_w_up, exp_w_down):
    bp, tp, _ = x_prompt.shape
    bs, ts, _ = x_sample.shape
    assert ts == 1 and tp % GLA_CHUNK == 0 and tp % SCAN_TIME_BLOCK == 0
    np_ = bp * tp
    ns = bs * ts
    m = np_ + ns
    assert np_ % LIGHT_BLOCK == 0 and ns % SAMPLE_BLOCK == 0 and (2 * m) % MOE_BLOCK == 0
    depth = w_in.shape[0]
    nh = GLA_HEADS
    tw = TOK_WIDTH
    bf = lambda t_: t_.astype(BF16)

    mem2d = mem_prompt.reshape(bp * N_MEM, D_MODEL)
    mem_kv = [_norm_matmul(mem2d, norm_mem_g[i], w_mem_kv[i], 512) for i in range(depth)]
    pk = jnp.stack([kv[:, :MEM_WIDTH].reshape(bp, N_MEM, MEM_WIDTH) for kv in mem_kv])
    pv = jnp.stack([kv[:, MEM_WIDTH:].reshape(bp, N_MEM, MEM_WIDTH) for kv in mem_kv])
    prompt_mem_k = pk.reshape(depth, bp, N_MEM, MEM_HEADS, MEM_HEAD_DIM)
    prompt_mem_v = pv.reshape(depth, bp, N_MEM, MEM_HEADS, MEM_HEAD_DIM)
    sk = cache_mem_k.reshape(depth, bs, N_MEM, MEM_WIDTH)
    sv = cache_mem_v.reshape(depth, bs, N_MEM, MEM_WIDTH)

    x_p = x_prompt.reshape(np_, D_MODEL)
    x_s = x_sample.reshape(ns, D_MODEL)
    h_p = _norm(x_p, norm_mix_g[0], LIGHT_BLOCK)
    h_s = _norm(x_s, norm_mix_g[0], SAMPLE_BLOCK)

    w_router = jnp.zeros((depth, D_MODEL, ROUTER_LANES), F32)
    w_router = w_router.at[:, :, :N_GROUPS].set(router_wg).at[:, :, N_GROUPS:N_GROUPS + N_EXPERTS].set(router_we)
    b_router = jnp.zeros((depth, 1, ROUTER_LANES), F32)
    b_router = b_router.at[:, 0, :N_GROUPS].set(router_bg).at[:, 0, N_GROUPS:N_GROUPS + N_EXPERTS].set(router_be)

    p_rw_S, p_rw_shift, p_gla_S, s_rw_S, s_rw_shift, s_gla_S = [], [], [], [], [], []
    for i in range(depth):
        j = i // 2
        if i % 2 == 0:
            wi = w_in[i]
            ws = [rw_mu[j], bf(wi[:, :tw]), bf(wi[:, tw:2 * tw]), bf(wi[:, 2 * tw:3 * tw]), bf(wi[:, 3 * tw:]),
                  bf(rw_w1[j]), bf(rw_w2[j]), rw_w0[j].reshape(1, tw), bf(rw_a1[j]), bf(rw_a2[j]),
                  rw_a0[j].reshape(1, tw), bf(rw_g1[j]), bf(rw_g2[j])]
            *rkvda_p, gate_p, qm_p = _rwkv_proj(h_p, None, ws, PROJ_BLOCK, tp)
            *rkvda_s, gate_s, qm_s = _rwkv_proj(h_s, state_rwkv_shift[j], ws, SAMPLE_BLOCK, ts)
            pvec = [rw_k_k[j], rw_k_a[j], rw_r_k[j], rw_ln_g[j], rw_ln_b[j]]
            tok_p, sp = _rwkv_prompt_mixer(*rkvda_p, *[_seq_scan_param(p, bp) for p in pvec], bp, tp, SCAN_TIME_BLOCK)
            ys, ss = _rwkv_scan(*[_to_scan_layout(t_, bs, ts) for t_ in rkvda_s], [_scan_param(p, bs) for p in pvec],
                                _state_to_scan(state_rwkv_S[j]), LANES, 1)
            tok_s = _from_scan_layout(ys, bs, ts)
            p_rw_S.append(sp[:, :, :RWKV_HEADS * bp].reshape(RWKV_N, RWKV_N, RWKV_HEADS, bp).transpose(3, 2, 0, 1))
            s_rw_S.append(_state_from_scan(ss, bs))
            p_rw_shift.append(h_p.reshape(bp, tp, D_MODEL)[:, -1])
            s_rw_shift.append(h_s)
            layout_p, layout_s = "rwkv_pairs", "rows"
        else:
            ws = _gla_weights(w_in[i], gla_a1[j], gla_a2[j], gla_ab[j])
            q_p, k_p, v_p, gate_p, la_p, qm_p = _gla_proj(h_p, ws, PROJ_BLOCK)
            q_s, k_s, v_s, gate_s, la_s, qm_s = _gla_proj(h_s, ws, SAMPLE_BLOCK)
            tok_p, sp_t = _gla_chunk_scan(q_p, k_p, v_p, la_p, gla_norm_g[j], bp, tp)
            col = lambda t_: t_.transpose(1, 0, 2).reshape(bs * nh, GLA_DK, 1)
            os_, ss = _gla_step(col(q_s), col(k_s), v_s.transpose(1, 0, 2).reshape(bs * nh, 1, GLA_DV), col(la_s),
                                gla_norm_g[j], state_gla_S[j].reshape(bs * nh, GLA_DK, GLA_DV))
            tok_s = os_.reshape(bs, nh, GLA_DV).transpose(1, 0, 2)
            p_gla_S.append(sp_t.transpose(0, 1, 3, 2))
            s_gla_S.append(ss.reshape(bs, nh, GLA_DK, GLA_DV))
            layout_p = layout_s = "gla_heads"

        att_p = _mem_attn(qm_p.reshape(bp, tp, MEM_WIDTH), pk, pv, i, 512).reshape(np_, MEM_WIDTH)
        att_s = _mem_attn(qm_s.reshape(bs, ts, MEM_WIDTH), sk, sv, i, 1).reshape(ns, MEM_WIDTH)
        ws = [bf(w_out[i]), norm_ffn_g[i].reshape(1, D_MODEL), w_router[i], b_router[i]]
        x1_p, h2, logits_p = _out_proj(tok_p, gate_p, att_p, x_p, ws, layout_p, PROJ_BLOCK, m, 0)
        x1_s, h2, logits_s = _out_proj(tok_s, gate_s, att_s, x_s, ws, layout_s, SAMPLE_BLOCK, m, np_, h2_buffer=h2)
        tok_sorted, gate_sorted, inv, item_tile, item_expert, n_items, lo, hi = _route(
            jnp.concatenate([logits_p, logits_s], axis=0), np_, MOE_BLOCK)
        rows = _moe_ffn(h2[tok_sorted], gate_sorted, item_tile, item_expert, n_items, lo, hi,
                        exp_w_up, exp_w_down, i)
        gathered = rows[inv]
        g_next = norm_mix_g[i + 1] if i + 1 < depth else norm_final_g
        x_p, h_p = _combine(x1_p, gathered, 0, g_next, LIGHT_BLOCK)
        x_s, h_s = _combine(x1_s, gathered, 2 * np_, g_next, SAMPLE_BLOCK)

    y_prompt = h_p.reshape(bp, tp, D_MODEL)
    y_sample = h_s.reshape(bs, ts, D_MODEL)
    return (y_prompt, y_sample, jnp.stack(p_rw_S), jnp.stack(p_rw_shift), jnp.stack(p_gla_S),
            prompt_mem_k, prompt_mem_v, jnp.stack(s_rw_S), jnp.stack(s_rw_shift), jnp.stack(s_gla_S))
```

```python
import functools

import numpy as np
import jax
import jax.numpy as jnp
from jax import lax
from jax.experimental import pallas as pl
from jax.experimental.pallas import tpu as pltpu

F32 = jnp.float32
BF16 = jnp.bfloat16
HIGHEST = lax.Precision.HIGHEST

D_MODEL = 1024
TOK_WIDTH = 768
MEM_WIDTH = 256
MEM_HEADS = 4
MEM_HEAD_DIM = 64
N_MEM = 256
RWKV_HEADS = 12
RWKV_N = 64
RWKV_PAIRS = RWKV_HEADS // 2
RWKV_GN_EPS = 64e-5
GLA_HEADS = 4
GLA_KW = 384
GLA_DK = 96
GLA_DV = 192
GLA_TAU = 16.0
GLA_CHUNK = 64
GLA_TILE = 8
N_GROUPS = 4
EXPERTS_PER_GROUP = 8
N_EXPERTS = 32
EXPERT_FF = 512
NORM_EPS = 1e-6
ROUTER_LANES = 128
LANES = 128
SUBLANES = 8

PROJ_BLOCK = 512
LIGHT_BLOCK = 512
SAMPLE_BLOCK = 128
MOE_BLOCK = 256
SCAN_TIME_BLOCK = 32
VMEM_LIMIT = 56 * 1024 * 1024


def _cparams(*sem):
    return pltpu.CompilerParams(dimension_semantics=sem, vmem_limit_bytes=VMEM_LIMIT)


def _dot(a, b):
    return jnp.dot(a, b, preferred_element_type=F32)


def _dot_nt(a, b):
    return lax.dot_general(a, b, (((1,), (1,)), ((), ())), preferred_element_type=F32)


def _dot_tn(a, b):
    return lax.dot_general(a, b, (((0,), (0,)), ((), ())), preferred_element_type=F32)


def _rms(x, g):
    return x * lax.rsqrt(jnp.mean(x * x, axis=-1, keepdims=True) + NORM_EPS) * g


def _sigmoid(x):
    return 1.0 / (1.0 + jnp.exp(-x))


def _row_spec(tm, n, offset=0):
    return pl.BlockSpec((tm, n), lambda i: (i + offset, 0))


def _head_row_spec(tm, n):
    return pl.BlockSpec((GLA_HEADS, tm, n), lambda i: (0, i, 0))


def _full_spec(shape):
    nd = len(shape)
    return pl.BlockSpec(shape, lambda *_: (0,) * nd)


def _norm_kernel(x_ref, g_ref, o_ref):
    o_ref[...] = _rms(x_ref[...], g_ref[...])


def _norm(x, g, tm):
    m, d = x.shape
    return pl.pallas_call(
        _norm_kernel, grid=(m // tm,),
        in_specs=[_row_spec(tm, d), _full_spec((1, d))],
        out_specs=_row_spec(tm, d),
        out_shape=jax.ShapeDtypeStruct((m, d), F32),
        compiler_params=_cparams("parallel"), name="rms_norm")(x, g.reshape(1, d))


def _norm_matmul_kernel(x_ref, g_ref, w_ref, o_ref):
    o_ref[...] = _dot(_rms(x_ref[...], g_ref[...]).astype(BF16), w_ref[...])


def _norm_matmul(x, g, w, tm):
    m, d = x.shape
    n = w.shape[1]
    return pl.pallas_call(
        _norm_matmul_kernel, grid=(m // tm,),
        in_specs=[_row_spec(tm, d), _full_spec((1, d)), _full_spec((d, n))],
        out_specs=_row_spec(tm, n),
        out_shape=jax.ShapeDtypeStruct((m, n), F32),
        compiler_params=_cparams("parallel"), name="norm_matmul")(x, g.reshape(1, d), w.astype(BF16))


def _rwkv_proj_kernel(h_ref, hp_ref, *refs, blocks_per_seq, norm_input):
    if norm_input:
        ng_ref, refs = refs[0], refs[1:]
        normed = lambda t_: _rms(t_, ng_ref[...])
    else:
        normed = lambda t_: t_
    (mu_ref, wr_ref, wk_ref, wv_ref, wq_ref, w1_ref, w2_ref, w0_ref, a1_ref, a2_ref, a0_ref, g1_ref, g2_ref,
     r_out, k_out, v_out, d_out, a_out, g_out, q_out) = refs

    def put(out, val):
        if blocks_per_seq:
            for p in range(RWKV_PAIRS):
                out[p] = val[:, p * LANES:(p + 1) * LANES]
        else:
            out[...] = val

    h = normed(h_ref[...])
    if blocks_per_seq:
        seq_start = (pl.program_id(0) % blocks_per_seq) == 0
        before = jnp.where(seq_start, 0.0, normed(hp_ref[SUBLANES - 1:SUBLANES, :]))
        row = lax.broadcasted_iota(jnp.int32, (h.shape[0], 1), 0)
        hp = jnp.where(row == 0, before, pltpu.roll(h, 1, axis=0))
    else:
        hp = hp_ref[...]
    xx = hp - h

    def mix(j):
        return (h + xx * mu_ref[j:j + 1, :]).astype(BF16)

    put(r_out, _dot(mix(0), wr_ref[...]))
    wl = w0_ref[...] + _dot(jnp.tanh(_dot(mix(1), w1_ref[...])).astype(BF16), w2_ref[...])
    put(k_out, _dot(mix(2), wk_ref[...]))
    put(v_out, _dot(mix(3), wv_ref[...]))
    al = a0_ref[...] + _dot(_dot(mix(4), a1_ref[...]).astype(BF16), a2_ref[...])
    g_out[...] = _dot(_sigmoid(_dot(mix(5), g1_ref[...])).astype(BF16), g2_ref[...])
    q_out[...] = _dot(h.astype(BF16), wq_ref[...])
    z = -wl
    softplus = jnp.maximum(z, 0.0) + jnp.log(1.0 + jnp.exp(-jnp.abs(z)))
    put(d_out, jnp.exp(-jnp.exp(-softplus - 0.5)))
    put(a_out, _sigmoid(al))


def _rwkv_proj(h, h_prev, ws, tm, seq_len, norm_g=None):
    m = h.shape[0]
    if norm_g is not None:
        assert h_prev is None
        ws = [norm_g.reshape(1, D_MODEL)] + list(ws)
    tw = TOK_WIDTH
    if h_prev is None:
        assert seq_len % tm == 0 and tm % SUBLANES == 0
        per8 = tm // SUBLANES
        hp_spec = pl.BlockSpec((SUBLANES, D_MODEL), lambda i: (jnp.maximum(i * per8 - 1, 0), 0))
        h_prev, blocks_per_seq = h, seq_len // tm
    else:
        hp_spec, blocks_per_seq = _row_spec(tm, D_MODEL), 0
    if blocks_per_seq:
        scan_spec = pl.BlockSpec((RWKV_PAIRS, tm, LANES), lambda i: (0, i, 0))
        scan_shape = jax.ShapeDtypeStruct((RWKV_PAIRS, m, LANES), F32)
    else:
        scan_spec, scan_shape = _row_spec(tm, tw), jax.ShapeDtypeStruct((m, tw), F32)
    return pl.pallas_call(
        functools.partial(_rwkv_proj_kernel, blocks_per_seq=blocks_per_seq, norm_input=norm_g is not None),
        grid=(m // tm,),
        in_specs=[_row_spec(tm, D_MODEL), hp_spec] + [_full_spec(w.shape) for w in ws],
        out_specs=[scan_spec] * 5 + [_row_spec(tm, tw), _row_spec(tm, MEM_WIDTH)],
        out_shape=[scan_shape] * 5 + [jax.ShapeDtypeStruct((m, tw), F32), jax.ShapeDtypeStruct((m, MEM_WIDTH), F32)],
        compiler_params=_cparams("parallel"), name="rwkv_proj")(h, h_prev, *ws)


def _pad_lanes(x):
    short = LANES - x.shape[-1]
    if short == 0:
        return x
    return jnp.concatenate([x, jnp.zeros(x.shape[:-1] + (short,), x.dtype)], axis=-1)


def _rwkv_scan_kernel(r_ref, k_ref, v_ref, d_ref, a_ref, kkp_ref, kap_ref, rkp_ref, lng_ref, lnb_ref, s0_ref,
                      y_ref, sfin_ref, s_scr, v_scr, yrow_scr, *, tc):
    n = RWKV_N
    nl = r_ref.shape[-1]

    @pl.when(pl.program_id(1) == 0)
    def _():
        s_scr[...] = _pad_lanes(s0_ref[...])

    kkp = _pad_lanes(kkp_ref[...])
    kap = _pad_lanes(kap_ref[...])
    rkp = _pad_lanes(rkp_ref[...])
    lng = _pad_lanes(lng_ref[...])
    lnb = _pad_lanes(lnb_ref[...])

    def step(t, carry):
        r_t = _pad_lanes(r_ref[t])
        k_t = _pad_lanes(k_ref[t])
        v_t = _pad_lanes(v_ref[t])
        d_t = _pad_lanes(d_ref[t])
        a_t = _pad_lanes(a_ref[t])
        v_scr[...] = v_t
        kkr = k_t * kkp
        nrm = jnp.maximum(jnp.sqrt(jnp.sum(kkr * kkr, axis=0, keepdims=True)), 1e-12)
        kk = kkr * (1.0 / nrm)
        k2 = k_t * (1.0 + (a_t - 1.0) * kap)
        nkk = -kk
        b_t = kk * a_t

        def ibody(i, c):
            s_i = s_scr[i]
            sa = jnp.sum(s_i * nkk, axis=0, keepdims=True)
            v_i = v_scr[pl.ds(i, 1), :]
            s_n = s_i * d_t + sa * b_t + v_i * k2
            s_scr[i] = s_n
            yrow_scr[pl.ds(i, 1), :] = jnp.sum(s_n * r_t, axis=0, keepdims=True)
            return c

        lax.fori_loop(0, n, ibody, 0, unroll=8)
        y = yrow_scr[...]
        yc = y - jnp.mean(y, axis=0, keepdims=True)
        var = jnp.mean(yc * yc, axis=0, keepdims=True)
        gn = yc * lax.rsqrt(var + RWKV_GN_EPS) * lng + lnb
        bonus = jnp.sum(r_t * k2 * rkp, axis=0, keepdims=True) * v_t
        y_ref[t] = (gn + bonus)[:, :nl]
        return carry

    lax.fori_loop(0, tc, step, 0)

    @pl.when(pl.program_id(1) == pl.num_programs(1) - 1)
    def _():
        sfin_ref[...] = s_scr[:, :, :nl]


def _rwkv_scan(r, k, v, d, a, params, s0, lane_block, tc):
    t, n, l = r.shape
    seq = pl.BlockSpec((tc, n, lane_block), lambda li, ti: (ti, 0, li))
    par = pl.BlockSpec((n, lane_block), lambda li, ti: (0, li))
    st = pl.BlockSpec((n, n, lane_block), lambda li, ti: (0, 0, li))
    return pl.pallas_call(
        functools.partial(_rwkv_scan_kernel, tc=tc), grid=(l // lane_block, t // tc),
        in_specs=[seq] * 5 + [par] * 5 + [st],
        out_specs=[seq, st],
        out_shape=[jax.ShapeDtypeStruct((t, n, l), F32), jax.ShapeDtypeStruct((n, n, l), F32)],
        scratch_shapes=[pltpu.VMEM((n, n, LANES), F32), pltpu.VMEM((n, LANES), F32), pltpu.VMEM((n, LANES), F32)],
        compiler_params=_cparams("parallel", "arbitrary"), name="rwkv_scan")(r, k, v, d, a, *params, s0)


PREP_TILES = 6


def _rwkv_prep_kernel(r_ref, k_ref, v_ref, d_ref, a_ref, kkp_ref, kap_ref, o_ref, *, tc, nb):
    n = RWKV_N
    rows_per_pair = nb * tc
    ins = [ref.reshape(RWKV_PAIRS * rows_per_pair, LANES) for ref in (r_ref, k_ref, v_ref, d_ref, a_ref)]
    zero_rows = jnp.zeros((LANES - RWKV_HEADS * nb, LANES), F32)
    first_half = lax.broadcasted_iota(jnp.int32, (nb, LANES), 1) < n
    kkp = kkp_ref[...]
    kap = kap_ref[...]

    def load_transposed_pair(x2, t):
        pieces = []
        for p in range(RWKV_PAIRS):
            now = x2[pl.ds(p * rows_per_pair + t, nb, stride=tc), :]
            nxt = x2[pl.ds(p * rows_per_pair + t + 1, nb, stride=tc), :]
            pieces += [jnp.where(first_half, now, pltpu.roll(nxt, n, axis=1)),
                       jnp.where(first_half, pltpu.roll(now, n, axis=1), nxt)]
        both = jnp.concatenate(pieces + [zero_rows], axis=0).T
        return both[:n, :], both[n:, :]

    def emit(t, r_t, k_t, v_t, d_t, a_t):
        kkr = k_t * kkp
        nrm = jnp.maximum(jnp.sqrt(jnp.sum(kkr * kkr, axis=0, keepdims=True)), 1e-12)
        kk = kkr * (1.0 / nrm)
        k2 = k_t * (1.0 + (a_t - 1.0) * kap)
        o_ref[t, 0] = -kk
        o_ref[t, 1] = d_t
        o_ref[t, 2] = kk * a_t
        o_ref[t, 3] = k2
        o_ref[t, 4] = r_t
        o_ref[t, 5] = v_t

    def two_steps(u, carry):
        t = 2 * u
        tiles = [load_transposed_pair(x2, t) for x2 in ins]
        emit(t, *[tile[0] for tile in tiles])
        emit(t + 1, *[tile[1] for tile in tiles])
        return carry

    lax.fori_loop(0, tc // 2, two_steps, 0, unroll=2)


def _rwkv_state_scan_kernel(x_ref, rkp_ref, lng_ref, lnb_ref, y_ref, sfin_ref, s_scr, sa_scr, yrow_scr, *, tc):
    n = RWKV_N
    groups = n // SUBLANES

    @pl.when(pl.program_id(0) == 0)
    def _():
        s_scr[...] = jnp.zeros(s_scr.shape, F32)

    rkp = rkp_ref[...]
    lng = lng_ref[...]
    lnb = lnb_ref[...]
    sub = lax.broadcasted_iota(jnp.int32, (SUBLANES, LANES), 0)
    low4 = sub < 4
    low2 = (sub & 3) < 2
    low1 = (sub & 1) == 0

    def fold(x, y, dist, low):
        if dist == 4:
            return jnp.where(low, x, y) + pltpu.roll(jnp.where(low, y, x), 4, axis=0)
        return (jnp.where(low, x, pltpu.roll(y, dist, axis=0))
                + jnp.where(low, pltpu.roll(x, SUBLANES - dist, axis=0), y))

    def sublane_sums(ps):
        z = [fold(ps[0], ps[4], 4, low4), fold(ps[2], ps[6], 4, low4),
             fold(ps[1], ps[5], 4, low4), fold(ps[3], ps[7], 4, low4)]
        return fold(fold(z[0], z[1], 2, low2), fold(z[2], z[3], 2, low2), 1, low1)

    def tile_sum(x):
        acc = x[0:SUBLANES]
        for u in range(1, groups):
            acc = acc + x[u * SUBLANES:(u + 1) * SUBLANES]
        return acc

    def step(t, carry):
        nkk = x_ref[t, 0]
        d_t = x_ref[t, 1]
        b_t = x_ref[t, 2]
        k2 = x_ref[t, 3]
        r_t = x_ref[t, 4]
        for g in range(groups):
            sa_scr[g * SUBLANES:(g + 1) * SUBLANES, :] = sublane_sums(
                [tile_sum(s_scr[g * SUBLANES + u] * nkk) for u in range(SUBLANES)])
        for g in range(groups):
            ps = []
            for u in range(SUBLANES):
                i = g * SUBLANES + u
                s_n = s_scr[i] * d_t + sa_scr[pl.ds(i, 1), :] * b_t + x_ref[t, 5, pl.ds(i, 1), :] * k2
                s_scr[i] = s_n
                ps.append(tile_sum(s_n * r_t))
            yrow_scr[g * SUBLANES:(g + 1) * SUBLANES, :] = sublane_sums(ps)
        y = yrow_scr[...]
        yc = y - jnp.mean(y, axis=0, keepdims=True)
        var = jnp.mean(yc * yc, axis=0, keepdims=True)
        bonus = jnp.sum(r_t * k2 * rkp, axis=0, keepdims=True) * x_ref[t, 5]
        y_ref[t] = yc * lax.rsqrt(var + RWKV_GN_EPS) * lng + lnb + bonus
        return carry

    lax.fori_loop(0, tc, step, 0)

    @pl.when(pl.program_id(0) == pl.num_programs(0) - 1)
    def _():
        sfin_ref[...] = s_scr[...]


def _rwkv_unprep_kernel(y_ref, o_ref, *, tc, nb):
    n = RWKV_N
    rows_per_pair = nb * tc
    o2 = o_ref.reshape(RWKV_PAIRS * rows_per_pair, LANES)
    first_half = lax.broadcasted_iota(jnp.int32, (nb, LANES), 1) < n

    def two_steps(u, carry):
        t = 2 * u
        w = jnp.concatenate([y_ref[t], y_ref[t + 1]], axis=0).T
        for p in range(RWKV_PAIRS):
            even = w[(2 * p) * nb:(2 * p + 1) * nb, :]
            odd = w[(2 * p + 1) * nb:(2 * p + 2) * nb, :]
            o2[pl.ds(p * rows_per_pair + t, nb, stride=tc), :] = jnp.where(
                first_half, even, pltpu.roll(odd, n, axis=1))
            o2[pl.ds(p * rows_per_pair + t + 1, nb, stride=tc), :] = jnp.where(
                first_half, pltpu.roll(even, n, axis=1), odd)
        return carry

    lax.fori_loop(0, tc // 2, two_steps, 0, unroll=2)


def _rwkv_prompt_mixer(r, k, v, d, a, kkp, kap, rkp, lng, lnb, nb, t, tc):
    n = RWKV_N
    assert tc % SUBLANES == 0 and tc % 4 == 0 and t % tc == 0 and RWKV_HEADS * nb <= LANES
    grid = (t // tc,)
    tok = pl.BlockSpec((RWKV_PAIRS, nb, tc, LANES), lambda ti: (0, 0, ti, 0))
    par = pl.BlockSpec((n, LANES), lambda ti: (0, 0))
    tiles = pl.BlockSpec((tc, PREP_TILES, n, LANES), lambda ti: (ti, 0, 0, 0))
    ytile = pl.BlockSpec((tc, n, LANES), lambda ti: (ti, 0, 0))
    prepared = pl.pallas_call(
        functools.partial(_rwkv_prep_kernel, tc=tc, nb=nb), grid=grid,
        in_specs=[tok] * 5 + [par] * 2, out_specs=tiles,
        out_shape=jax.ShapeDtypeStruct((t, PREP_TILES, n, LANES), F32),
        compiler_params=_cparams("parallel"), name="rwkv_prep")(
            *[x.reshape(RWKV_PAIRS, nb, t, LANES) for x in (r, k, v, d, a)], kkp, kap)
    y, s = pl.pallas_call(
        functools.partial(_rwkv_state_scan_kernel, tc=tc), grid=grid,
        in_specs=[tiles, par, par, par],
        out_specs=[ytile, pl.BlockSpec((n, n, LANES), lambda ti: (0, 0, 0))],
        out_shape=[jax.ShapeDtypeStruct((t, n, LANES), F32), jax.ShapeDtypeStruct((n, n, LANES), F32)],
        scratch_shapes=[pltpu.VMEM((n, n, LANES), F32), pltpu.VMEM((n, LANES), F32), pltpu.VMEM((n, LANES), F32)],
        compiler_params=_cparams("arbitrary"), name="rwkv_state_scan")(prepared, rkp, lng, lnb)
    tok_out = pl.pallas_call(
        functools.partial(_rwkv_unprep_kernel, tc=tc, nb=nb), grid=grid,
        in_specs=[ytile], out_specs=tok,
        out_shape=jax.ShapeDtypeStruct((RWKV_PAIRS, nb, t, LANES), F32),
        compiler_params=_cparams("parallel"), name="rwkv_unprep")(y)
    return tok_out.reshape(RWKV_PAIRS, nb * t, LANES), s


def _gla_proj_kernel(h_ref, wq_ref, wk_ref, wv_ref, wr_ref, wm_ref, a1_ref, a2_ref, ab_ref,
                     q_out, k_out, v_out, r_out, la_out, qm_out):
    hb = h_ref[...].astype(BF16)
    low = _dot(hb, a1_ref[...]).astype(BF16)
    for hd in range(GLA_HEADS):
        q_out[hd] = _dot(hb, wq_ref[hd])
        k_out[hd] = _dot(hb, wk_ref[hd])
        v_out[hd] = _dot(hb, wv_ref[hd])
        r_out[hd] = _dot(hb, wr_ref[hd])
        x = _dot(low, a2_ref[hd]) + ab_ref[hd]
        log_sigmoid = jnp.minimum(x, 0.0) - jnp.log(1.0 + jnp.exp(-jnp.abs(x)))
        la_out[hd] = log_sigmoid / GLA_TAU
    qm_out[...] = _dot(hb, wm_ref[...])


def _gla_weights(w_in, a1, a2, ab):
    kw, tw, nh = GLA_KW, TOK_WIDTH, GLA_HEADS

    def heads(w, d):
        return w.reshape(w.shape[0], nh, d).transpose(1, 0, 2)

    return [heads(w_in[:, :kw], GLA_DK).astype(BF16), heads(w_in[:, kw:2 * kw], GLA_DK).astype(BF16),
            heads(w_in[:, 2 * kw:2 * kw + tw], GLA_DV).astype(BF16),
            heads(w_in[:, 2 * kw + tw:3 * tw], GLA_DV).astype(BF16), w_in[:, 3 * tw:].astype(BF16),
            a1.astype(BF16), heads(a2, GLA_DK).astype(BF16), ab.reshape(nh, 1, GLA_DK)]


def _gla_proj(h, ws, tm):
    m = h.shape[0]
    nh = GLA_HEADS
    widths = [GLA_DK, GLA_DK, GLA_DV, GLA_DV, GLA_DK]
    return pl.pallas_call(
        _gla_proj_kernel, grid=(m // tm,),
        in_specs=[_row_spec(tm, D_MODEL)] + [_full_spec(w.shape) for w in ws],
        out_specs=[_head_row_spec(tm, w) for w in widths] + [_row_spec(tm, MEM_WIDTH)],
        out_shape=[jax.ShapeDtypeStruct((nh, m, w), F32) for w in widths] + [jax.ShapeDtypeStruct((m, MEM_WIDTH), F32)],
        compiler_params=_cparams("parallel"), name="gla_proj")(h, *ws)


def _gla_out_norm(o, g):
    return o * lax.rsqrt(jnp.mean(o * o, axis=-1, keepdims=True) + NORM_EPS) * g


def _gla_chunk_kernel(q_ref, k_ref, v_ref, la_ref, ng_ref, o_ref, sfin_ref, st_scr):
    c, dk = GLA_CHUNK, GLA_DK

    @pl.when(pl.program_id(1) == 0)
    def _():
        st_scr[...] = jnp.zeros(st_scr.shape, F32)

    row = lax.broadcasted_iota(jnp.int32, (c, c), 0)
    col = lax.broadcasted_iota(jnp.int32, (c, c), 1)
    tril = (row >= col).astype(F32)
    rr = lax.broadcasted_iota(jnp.int32, (c, dk), 0)
    ones_sum = jnp.ones((dk, LANES), BF16)

    heads = range(GLA_HEADS)
    tril_b = tril.astype(BF16)

    def cumsum_rows(la):
        hi = la.astype(BF16)
        r1 = la - hi.astype(F32)
        mid = r1.astype(BF16)
        lo = (r1 - mid.astype(F32)).astype(BF16)
        return _dot(tril_b, hi) + _dot(tril_b, mid) + _dot(tril_b, lo)

    def tile_roll(x, dlt):
        return pltpu.roll(x.reshape(c // GLA_TILE, GLA_TILE, dk), dlt, axis=1).reshape(c, dk)

    k = [k_ref[hd] for hd in heads]
    vb = [v_ref[hd].astype(BF16) for hd in heads]
    q = [q_ref[hd] * (dk ** -0.5) for hd in heads]
    b = [cumsum_rows(la_ref[hd]) for hd in heads]
    st = [st_scr[hd] for hd in heads]
    inter = [_dot_nt((q[hd] * jnp.exp(b[hd])).astype(BF16), st[hd].astype(BF16)) for hd in heads]

    att = [jnp.zeros((c, c), F32) for hd in heads]
    blk = c // 2
    while blk >= GLA_TILE:
        two = 2 * blk
        upper = (rr & (two - 1)) >= blk
        same_block = (row ^ col) < two
        parts = []
        for hd in heads:
            b_ref_rows = jnp.concatenate(
                [jnp.broadcast_to(b[hd][s0 + blk - 1:s0 + blk, :], (two, dk)) for s0 in range(0, c, two)], axis=0)
            q_l = jnp.where(upper, q[hd] * jnp.exp(jnp.minimum(b[hd] - b_ref_rows, 0.0)), 0.0).astype(BF16)
            k_l = jnp.where(upper, 0.0, k[hd] * jnp.exp(jnp.minimum(b_ref_rows - b[hd], 0.0))).astype(BF16)
            parts.append(_dot_nt(q_l, k_l))
        att = [att[hd] + jnp.where(same_block, parts[hd], 0.0) for hd in heads]
        blk //= 2

    sums = []
    for hd in heads:
        prods = [(q[hd] * k[hd]).astype(BF16)]
        for dlt in range(1, GLA_TILE):
            p = q[hd] * tile_roll(k[hd], dlt) * jnp.exp(jnp.minimum(b[hd] - tile_roll(b[hd], dlt), 0.0))
            prods.append(jnp.where((rr & (GLA_TILE - 1)) >= dlt, p, 0.0).astype(BF16))
        sums.append(_dot(jnp.concatenate(prods, axis=0), ones_sum))
    for dlt in range(GLA_TILE):
        on_diag = col == row - dlt
        att = [att[hd] + jnp.where(on_diag, sums[hd][dlt * c:(dlt + 1) * c, :c], 0.0) for hd in heads]

    o = [inter[hd] + _dot(att[hd].astype(BF16), vb[hd]) for hd in heads]
    for hd in heads:
        o_ref[hd] = _gla_out_norm(o[hd], ng_ref[...])
    for hd in heads:
        b_end = b[hd][c - 1:c, :]
        kd = (k[hd] * jnp.exp(b_end - b[hd])).astype(BF16)
        st_scr[hd] = jnp.exp(b_end) * st[hd] + _dot_tn(vb[hd], kd)

    @pl.when(pl.program_id(1) == pl.num_programs(1) - 1)
    def _():
        sfin_ref[0] = st_scr[...]


def _gla_chunk_scan(q, k, v, la, norm_g, batch, t):
    nh, dk, dv, c = GLA_HEADS, GLA_DK, GLA_DV, GLA_CHUNK
    nc = t // c
    kspec = pl.BlockSpec((nh, c, dk), lambda i, j: (0, i * nc + j, 0))
    vspec = pl.BlockSpec((nh, c, dv), lambda i, j: (0, i * nc + j, 0))
    sspec = pl.BlockSpec((1, nh, dv, dk), lambda i, j: (i, 0, 0, 0))
    return pl.pallas_call(
        _gla_chunk_kernel, grid=(batch, nc),
        in_specs=[kspec, kspec, vspec, kspec, pl.BlockSpec((1, dv), lambda i, j: (0, 0))],
        out_specs=[vspec, sspec],
        out_shape=[jax.ShapeDtypeStruct((nh, batch * t, dv), F32), jax.ShapeDtypeStruct((batch, nh, dv, dk), F32)],
        scratch_shapes=[pltpu.VMEM((nh, dv, dk), F32)],
        compiler_params=_cparams("parallel", "arbitrary"), name="gla_chunk")(q, k, v, la, norm_g.reshape(1, dv))


GLA_STEP_GROUP = 8


def _gla_step_kernel(q_ref, k_ref, v_ref, la_ref, ng_ref, s0_ref, o_ref, s_ref):
    for g in range(GLA_STEP_GROUP):
        kv = k_ref[g].astype(BF16).astype(F32) * v_ref[g].astype(BF16).astype(F32)
        s_new = jnp.exp(la_ref[g]) * s0_ref[g] + kv
        s_ref[g] = s_new
        o = jnp.sum((q_ref[g] * (GLA_DK ** -0.5)) * s_new, axis=0, keepdims=True)
        o_ref[g] = _gla_out_norm(o, ng_ref[...])


def _gla_step(q, k, v, la, norm_g, s0):
    bh, dk, _ = q.shape
    dv = v.shape[2]
    g = GLA_STEP_GROUP
    cspec = pl.BlockSpec((g, dk, 1), lambda i: (i, 0, 0))
    vspec = pl.BlockSpec((g, 1, dv), lambda i: (i, 0, 0))
    sspec = pl.BlockSpec((g, dk, dv), lambda i: (i, 0, 0))
    return pl.pallas_call(
        _gla_step_kernel, grid=(bh // g,),
        in_specs=[cspec, cspec, vspec, cspec, _full_spec((1, dv)), sspec],
        out_specs=[vspec, sspec],
        out_shape=[jax.ShapeDtypeStruct((bh, 1, dv), F32), jax.ShapeDtypeStruct((bh, dk, dv), F32)],
        compiler_params=_cparams("parallel"), name="gla_step")(q, k, v, la, norm_g.reshape(1, dv), s0)


def _mem_attn_kernel(q_ref, k_ref, v_ref, o_ref):
    q = q_ref[0]
    k = k_ref[0].astype(BF16)
    v_ones = jnp.concatenate([v_ref[0].astype(BF16), jnp.ones((N_MEM, LANES), BF16)], axis=1)
    head_of_lane = lax.broadcasted_iota(jnp.int32, (1, MEM_WIDTH), 1) // MEM_HEAD_DIM
    heads = range(MEM_HEADS)
    mine = [head_of_lane == h for h in heads]
    qh = [jnp.where(mine[h], q, 0.0).astype(BF16) for h in heads]
    s = [_dot_nt(qh[h], k) * (MEM_HEAD_DIM ** -0.5) for h in heads]
    e = [jnp.exp(s[h] - jnp.max(s[h], axis=-1, keepdims=True)).astype(BF16) for h in heads]
    ev = [_dot(e[h], v_ones) for h in heads]
    out = jnp.zeros(q.shape, F32)
    for h in heads:
        inv = 1.0 / ev[h][:, MEM_WIDTH:]
        out = out + jnp.where(mine[h], ev[h][:, :MEM_WIDTH] * jnp.concatenate([inv, inv], axis=1), 0.0)
    o_ref[0] = out


def _mem_attn(q, mem_k, mem_v, layer, tq):
    b, t, w = q.shape
    qspec = pl.BlockSpec((1, tq, w), lambda i, j: (i, j, 0))
    mspec = pl.BlockSpec((None, 1, N_MEM, w), lambda i, j: (layer, i, 0, 0))
    return pl.pallas_call(
        _mem_attn_kernel, grid=(b, t // tq),
        in_specs=[qspec, mspec, mspec], out_specs=qspec,
        out_shape=jax.ShapeDtypeStruct((b, t, w), F32),
        compiler_params=_cparams("parallel", "parallel"), name="mem_attn")(q, mem_k, mem_v)


def _out_proj_kernel(*refs, layout, aliased):
    tok_ref, gate_ref, att_ref, x_ref, wo_ref, g_ref, wr_hi_ref, wr_lo_ref, br_ref = refs[:9]
    x1_out, h2_out, logit_out = refs[9 + aliased:]
    x1 = x_ref[...] + _dot(att_ref[...].astype(BF16), wo_ref[TOK_WIDTH:, :])
    if layout == "gla_heads":
        for hd in range(GLA_HEADS):
            gate = gate_ref[hd]
            mixed = (tok_ref[hd] * (gate * _sigmoid(gate))).astype(BF16)
            x1 = x1 + _dot(mixed, wo_ref[hd * GLA_DV:(hd + 1) * GLA_DV, :])
    else:
        if layout == "rwkv_pairs":
            tok = jnp.concatenate([tok_ref[p] for p in range(RWKV_PAIRS)], axis=1)
        else:
            tok = tok_ref[...]
        x1 = x1 + _dot((tok * gate_ref[...]).astype(BF16), wo_ref[:TOK_WIDTH, :])
    x1_out[...] = x1
    h2 = _rms(x1, g_ref[...])
    h2_out[...] = h2
    h_hi = h2.astype(BF16)
    h_lo = (h2 - h_hi.astype(F32)).astype(BF16)
    logit_out[...] = (_dot(h_hi, wr_hi_ref[...]) + _dot(h_lo, wr_hi_ref[...]) + _dot(h_hi, wr_lo_ref[...])
                      + br_ref[...])


def _out_proj(tok, gate, att, x, ws, layout, tm, h2_rows, h2_row_offset, h2_buffer=None):
    m = x.shape[0]
    aliased = h2_buffer is not None
    rows_spec = _row_spec(tm, TOK_WIDTH)
    tok_spec, gate_spec = {
        "rows": (rows_spec, rows_spec),
        "rwkv_pairs": (pl.BlockSpec((RWKV_PAIRS, tm, LANES), lambda i: (0, i, 0)), rows_spec),
        "gla_heads": (_head_row_spec(tm, GLA_DV), _head_row_spec(tm, GLA_DV))}[layout]
    in_specs = ([tok_spec, gate_spec, _row_spec(tm, MEM_WIDTH), _row_spec(tm, D_MODEL)]
                + [_full_spec(w.shape) for w in ws])
    args = [tok, gate, att, x, *ws]
    if aliased:
        in_specs.append(pl.BlockSpec(memory_space=pl.ANY))
        args.append(h2_buffer)
    return pl.pallas_call(
        functools.partial(_out_proj_kernel, layout=layout, aliased=int(aliased)), grid=(m // tm,),
        in_specs=in_specs,
        out_specs=[_row_spec(tm, D_MODEL), _row_spec(tm, D_MODEL, h2_row_offset // tm), _row_spec(tm, ROUTER_LANES)],
        out_shape=[jax.ShapeDtypeStruct((m, D_MODEL), F32), jax.ShapeDtypeStruct((h2_rows, D_MODEL), F32),
                   jax.ShapeDtypeStruct((m, ROUTER_LANES), F32)],
        input_output_aliases={len(args) - 1: 1} if aliased else {},
        compiler_params=_cparams("parallel"), name="out_proj")(*args)


def _moe_kernel(tile_ref, exp_ref, nitem_ref, lo_ref, hi_ref, x_ref, gate_ref, wu_ref, wd_ref, o_ref, wu_scr, wd_scr):
    w = pl.program_id(0)
    prev = jnp.maximum(w - 1, 0)
    e = exp_ref[w]
    valid = w < nitem_ref[0]

    @pl.when(jnp.logical_and(valid, jnp.logical_or(w == 0, e != exp_ref[prev])))
    def _():
        wu_scr[...] = wu_ref[0, 0].astype(BF16)
        wd_scr[...] = wd_ref[0, 0].astype(BF16)

    @pl.when(jnp.logical_or(w == 0, tile_ref[w] != tile_ref[prev]))
    def _():
        o_ref[...] = jnp.zeros(o_ref.shape, F32)

    @pl.when(valid)
    def _():
        gu = _dot(x_ref[...].astype(BF16), wu_scr[...])
        g = gu[:, :EXPERT_FF]
        act = (g * _sigmoid(g) * gu[:, EXPERT_FF:]).astype(BF16)
        out = _dot(act, wd_scr[...]) * gate_ref[...]
        rows = tile_ref[w] * MOE_BLOCK + lax.broadcasted_iota(jnp.int32, (MOE_BLOCK, 1), 0)
        mine = jnp.logical_and(rows >= lo_ref[e], rows < hi_ref[e])
        o_ref[...] = o_ref[...] + jnp.where(mine, out, 0.0)


def _moe_ffn(xs, row_gate, item_tile, item_expert, n_items, lo, hi, w_up, w_down, layer):
    tm = MOE_BLOCK
    n_work = item_tile.shape[0]
    a = xs.shape[0]
    row_map = lambda w, tile, ex, ni, lo_, hi_: (tile[w], 0)
    exp_map = lambda w, tile, ex, ni, lo_, hi_: (layer, ex[w], 0, 0)
    grid_spec = pltpu.PrefetchScalarGridSpec(
        num_scalar_prefetch=5, grid=(n_work,),
        in_specs=[pl.BlockSpec((tm, D_MODEL), row_map),
                  pl.BlockSpec((tm, 1), row_map),
                  pl.BlockSpec((1, 1, D_MODEL, 2 * EXPERT_FF), exp_map),
                  pl.BlockSpec((1, 1, EXPERT_FF, D_MODEL), exp_map)],
        out_specs=pl.BlockSpec((tm, D_MODEL), row_map),
        scratch_shapes=[pltpu.VMEM((D_MODEL, 2 * EXPERT_FF), BF16), pltpu.VMEM((EXPERT_FF, D_MODEL), BF16)])
    return pl.pallas_call(
        _moe_kernel, grid_spec=grid_spec,
        out_shape=jax.ShapeDtypeStruct((a, D_MODEL), F32),
        compiler_params=_cparams("arbitrary"), name="moe_ffn")(
            item_tile, item_expert, n_items, lo, hi, xs, row_gate, w_up, w_down)


def _route(logits, n_prompt, tm):
    m = logits.shape[0]
    n_sample = m - n_prompt
    a = 2 * m
    gl = logits[:, :N_GROUPS]
    el = logits[:, N_GROUPS:N_GROUPS + N_EXPERTS].reshape(m, N_GROUPS, EXPERTS_PER_GROUP)
    group = jnp.argmax(gl, -1).astype(jnp.int32)
    p_group = jnp.max(jax.nn.softmax(gl, -1), -1, keepdims=True)
    in_group = jnp.take_along_axis(el, group[:, None, None], axis=1)[:, 0]
    top_val, top_idx = lax.top_k(in_group, 2)
    gate = p_group * jax.nn.softmax(top_val, -1)
    expert = group[:, None] * EXPERTS_PER_GROUP + top_idx.astype(jnp.int32)

    def by_id(t):
        return jnp.concatenate([t[:n_prompt, 0], t[:n_prompt, 1], t[n_prompt:, 0], t[n_prompt:, 1]])

    tok_of_id = jnp.asarray(np.concatenate([np.arange(n_prompt), np.arange(n_prompt),
                                            n_prompt + np.arange(n_sample), n_prompt + np.arange(n_sample)]), jnp.int32)
    flat_e = by_id(expert)
    ids = jnp.arange(a, dtype=jnp.int32)
    _, order, gate_sorted, tok_sorted = lax.sort((flat_e, ids, by_id(gate), tok_of_id), num_keys=1, is_stable=True)
    _, inv = lax.sort((order, ids), num_keys=1)
    experts = jnp.arange(N_EXPERTS, dtype=jnp.int32)
    counts = jnp.sum((flat_e[:, None] == experts[None, :]).astype(jnp.int32), axis=0)
    hi = jnp.cumsum(counts).astype(jnp.int32)
    lo = hi - counts
    n_tiles = a // tm
    first_tile = lo // tm
    tiles_of = jnp.where(counts > 0, (hi - 1) // tm - first_tile + 1, 0)
    item_end = jnp.cumsum(tiles_of).astype(jnp.int32)
    n_items = item_end[-1:]
    n_work = n_tiles + N_EXPERTS - 1
    w = jnp.minimum(jnp.arange(n_work, dtype=jnp.int32), n_items[0] - 1)
    item_expert = jnp.sum((item_end[None, :] <= w[:, None]).astype(jnp.int32), axis=1)
    onehot = (item_expert[:, None] == experts[None, :]).astype(jnp.int32)
    item_tile = jnp.sum(onehot * (first_tile - (item_end - tiles_of))[None, :], axis=1) + w
    return tok_sorted, gate_sorted.reshape(a, 1), inv, item_tile.astype(jnp.int32), item_expert, n_items, lo, hi


def _combine_kernel(x_ref, r0_ref, r1_ref, g_ref, *outs):
    x2 = x_ref[...] + (r0_ref[...] + r1_ref[...])
    outs[-1][...] = _rms(x2, g_ref[...])
    if len(outs) == 2:
        outs[0][...] = x2


def _combine(x1, gathered, first_row, g, tm, last_layer):
    m = x1.shape[0]
    spec = _row_spec(tm, D_MODEL)
    n_out = 1 if last_layer else 2
    outs = pl.pallas_call(
        _combine_kernel, grid=(m // tm,),
        in_specs=[spec, _row_spec(tm, D_MODEL, first_row // tm), _row_spec(tm, D_MODEL, (first_row + m) // tm),
                  _full_spec((1, D_MODEL))],
        out_specs=[spec] * n_out,
        out_shape=[jax.ShapeDtypeStruct((m, D_MODEL), F32)] * n_out,
        compiler_params=_cparams("parallel"), name="moe_combine")(x1, gathered, gathered, g.reshape(1, D_MODEL))
    return (None, outs[0]) if last_layer else tuple(outs)


def _to_scan_layout(t2d, b, t):
    return t2d.reshape(b, t, RWKV_HEADS, RWKV_N).transpose(1, 3, 0, 2).reshape(t, RWKV_N, b * RWKV_HEADS)


def _from_scan_layout(y, b, t):
    return y.reshape(t, RWKV_N, b, RWKV_HEADS).transpose(2, 0, 3, 1).reshape(b * t, TOK_WIDTH)


def _scan_param(p, b):
    return jnp.tile(p.reshape(RWKV_HEADS, RWKV_N).T, (1, b))


def _seq_scan_param(p, b):
    lanes = jnp.repeat(p.reshape(RWKV_HEADS, RWKV_N).T, b, axis=1)
    return jnp.pad(lanes, ((0, 0), (0, LANES - RWKV_HEADS * b)))


def _state_to_scan(s):
    b = s.shape[0]
    return s.transpose(2, 3, 0, 1).reshape(RWKV_N, RWKV_N, b * RWKV_HEADS)


def _state_from_scan(s, b):
    return s.reshape(RWKV_N, RWKV_N, b, RWKV_HEADS).transpose(2, 3, 0, 1)


def kernel(x_prompt, x_sample, mem_prompt, state_rwkv_S, state_rwkv_shift, state_gla_S, cache_mem_k, cache_mem_v, norm_mix_g, norm_ffn_g, norm_mem_g, norm_final_g, w_in, w_out, w_mem_kv, rw_mu, rw_w0, rw_w1, rw_w2, rw_a0, rw_a1, rw_a2, rw_g1, rw_g2, rw_k_k, rw_k_a, rw_r_k, rw_ln_g, rw_ln_b, gla_a1, gla_a2, gla_ab, gla_norm_g, router_wg, router_bg, router_we, router_be, exp_w_up, exp_w_down):
    bp, tp, _ = x_prompt.shape
    bs, ts, _ = x_sample.shape
    assert ts == 1 and tp % GLA_CHUNK == 0 and tp % SCAN_TIME_BLOCK == 0
    np_ = bp * tp
    ns = bs * ts
    m = np_ + ns
    assert np_ % LIGHT_BLOCK == 0 and ns % SAMPLE_BLOCK == 0 and (2 * m) % MOE_BLOCK == 0
    depth = w_in.shape[0]
    nh = GLA_HEADS
    tw = TOK_WIDTH
    bf = lambda t_: t_.astype(BF16)

    mem2d = mem_prompt.reshape(bp * N_MEM, D_MODEL)
    mem_kv = [_norm_matmul(mem2d, norm_mem_g[i], w_mem_kv[i], 512) for i in range(depth)]
    pk = jnp.stack([kv[:, :MEM_WIDTH].reshape(bp, N_MEM, MEM_WIDTH) for kv in mem_kv])
    pv = jnp.stack([kv[:, MEM_WIDTH:].reshape(bp, N_MEM, MEM_WIDTH) for kv in mem_kv])
    prompt_mem_k = pk.reshape(depth, bp, N_MEM, MEM_HEADS, MEM_HEAD_DIM)
    prompt_mem_v = pv.reshape(depth, bp, N_MEM, MEM_HEADS, MEM_HEAD_DIM)
    sk = cache_mem_k.reshape(depth, bs, N_MEM, MEM_WIDTH)
    sv = cache_mem_v.reshape(depth, bs, N_MEM, MEM_WIDTH)

    x_p = x_prompt.reshape(np_, D_MODEL)
    x_s = x_sample.reshape(ns, D_MODEL)
    h_p = None
    h_s = _norm(x_s, norm_mix_g[0], SAMPLE_BLOCK)

    w_router = jnp.zeros((depth, D_MODEL, ROUTER_LANES), F32)
    w_router = w_router.at[:, :, :N_GROUPS].set(router_wg).at[:, :, N_GROUPS:N_GROUPS + N_EXPERTS].set(router_we)
    b_router = jnp.zeros((depth, 1, ROUTER_LANES), F32)
    b_router = b_router.at[:, 0, :N_GROUPS].set(router_bg).at[:, 0, N_GROUPS:N_GROUPS + N_EXPERTS].set(router_be)

    p_rw_S, p_rw_shift, p_gla_S, s_rw_S, s_rw_shift, s_gla_S = [], [], [], [], [], []
    for i in range(depth):
        j = i // 2
        if i % 2 == 0:
            wi = w_in[i]
            ws = [rw_mu[j], bf(wi[:, :tw]), bf(wi[:, tw:2 * tw]), bf(wi[:, 2 * tw:3 * tw]), bf(wi[:, 3 * tw:]),
                  bf(rw_w1[j]), bf(rw_w2[j]), rw_w0[j].reshape(1, tw), bf(rw_a1[j]), bf(rw_a2[j]),
                  rw_a0[j].reshape(1, tw), bf(rw_g1[j]), bf(rw_g2[j])]
            if h_p is None:
                *rkvda_p, gate_p, qm_p = _rwkv_proj(x_p, None, ws, PROJ_BLOCK, tp, norm_g=norm_mix_g[i])
                shift_p = _norm(x_prompt[:, -1, :], norm_mix_g[i], bp)
            else:
                *rkvda_p, gate_p, qm_p = _rwkv_proj(h_p, None, ws, PROJ_BLOCK, tp)
                shift_p = h_p.reshape(bp, tp, D_MODEL)[:, -1]
            *rkvda_s, gate_s, qm_s = _rwkv_proj(h_s, state_rwkv_shift[j], ws, SAMPLE_BLOCK, ts)
            pvec = [rw_k_k[j], rw_k_a[j], rw_r_k[j], rw_ln_g[j], rw_ln_b[j]]
            tok_p, sp = _rwkv_prompt_mixer(*rkvda_p, *[_seq_scan_param(p, bp) for p in pvec], bp, tp, SCAN_TIME_BLOCK)
            ys, ss = _rwkv_scan(*[_to_scan_layout(t_, bs, ts) for t_ in rkvda_s], [_scan_param(p, bs) for p in pvec],
                                _state_to_scan(state_rwkv_S[j]), LANES, 1)
            tok_s = _from_scan_layout(ys, bs, ts)
            p_rw_S.append(sp[:, :, :RWKV_HEADS * bp].reshape(RWKV_N, RWKV_N, RWKV_HEADS, bp).transpose(3, 2, 0, 1))
            s_rw_S.append(_state_from_scan(ss, bs))
            p_rw_shift.append(shift_p)
            s_rw_shift.append(h_s)
            layout_p, layout_s = "rwkv_pairs", "rows"
        else:
            ws = _gla_weights(w_in[i], gla_a1[j], gla_a2[j], gla_ab[j])
            q_p, k_p, v_p, gate_p, la_p, qm_p = _gla_proj(h_p, ws, PROJ_BLOCK)
            q_s, k_s, v_s, gate_s, la_s, qm_s = _gla_proj(h_s, ws, SAMPLE_BLOCK)
            tok_p, sp_t = _gla_chunk_scan(q_p, k_p, v_p, la_p, gla_norm_g[j], bp, tp)
            col = lambda t_: t_.transpose(1, 0, 2).reshape(bs * nh, GLA_DK, 1)
            os_, ss = _gla_step(col(q_s), col(k_s), v_s.transpose(1, 0, 2).reshape(bs * nh, 1, GLA_DV), col(la_s),
                                gla_norm_g[j], state_gla_S[j].reshape(bs * nh, GLA_DK, GLA_DV))
            tok_s = os_.reshape(bs, nh, GLA_DV).transpose(1, 0, 2)
            p_gla_S.append(sp_t.transpose(0, 1, 3, 2))
            s_gla_S.append(ss.reshape(bs, nh, GLA_DK, GLA_DV))
            layout_p = layout_s = "gla_heads"

        att_p = _mem_attn(qm_p.reshape(bp, tp, MEM_WIDTH), pk, pv, i, 512).reshape(np_, MEM_WIDTH)
        att_s = _mem_attn(qm_s.reshape(bs, ts, MEM_WIDTH), sk, sv, i, 1).reshape(ns, MEM_WIDTH)
        wr_hi = bf(w_router[i])
        wr_lo = bf(w_router[i] - wr_hi.astype(F32))
        ws = [bf(w_out[i]), norm_ffn_g[i].reshape(1, D_MODEL), wr_hi, wr_lo, b_router[i]]
        x1_p, h2, logits_p = _out_proj(tok_p, gate_p, att_p, x_p, ws, layout_p, PROJ_BLOCK, m, 0)
        x1_s, h2, logits_s = _out_proj(tok_s, gate_s, att_s, x_s, ws, layout_s, SAMPLE_BLOCK, m, np_, h2_buffer=h2)
        tok_sorted, gate_sorted, inv, item_tile, item_expert, n_items, lo, hi = _route(
            jnp.concatenate([logits_p, logits_s], axis=0), np_, MOE_BLOCK)
        rows = _moe_ffn(h2[tok_sorted], gate_sorted, item_tile, item_expert, n_items, lo, hi,
                        exp_w_up, exp_w_down, i)
        gathered = rows[inv]
        g_next = norm_mix_g[i + 1] if i + 1 < depth else norm_final_g
        x_p, h_p = _combine(x1_p, gathered, 0, g_next, LIGHT_BLOCK, i + 1 == depth)
        x_s, h_s = _combine(x1_s, gathered, 2 * np_, g_next, SAMPLE_BLOCK, i + 1 == depth)

    y_prompt = h_p.reshape(bp, tp, D_MODEL)
    y_sample = h_s.reshape(bs, ts, D_MODEL)
    return (y_prompt, y_sample, jnp.stack(p_rw_S), jnp.stack(p_rw_shift), jnp.stack(p_gla_S),
            prompt_mem_k, prompt_mem_v, jnp.stack(s_rw_S), jnp.stack(s_rw_shift), jnp.stack(s_gla_S))
```

```python
import functools

import numpy as np
import jax
import jax.numpy as jnp
from jax import lax
from jax.experimental import pallas as pl
from jax.experimental.pallas import tpu as pltpu

F32 = jnp.float32
BF16 = jnp.bfloat16

D_MODEL = 1024
TOK_WIDTH = 768
MEM_WIDTH = 256
MEM_HEADS = 4
MEM_HEAD_DIM = 64
N_MEM = 256
RWKV_HEADS = 12
RWKV_N = 64
RWKV_PAIRS = RWKV_HEADS // 2
RWKV_GN_EPS = 64e-5
GLA_HEADS = 4
GLA_KW = 384
GLA_DK = 96
GLA_DV = 192
GLA_TAU = 16.0
GLA_CHUNK = 64
GLA_TILE = 8
N_GROUPS = 4
EXPERTS_PER_GROUP = 8
N_EXPERTS = 32
EXPERT_FF = 512
NORM_EPS = 1e-6
ROUTER_LANES = 128
LANES = 128
SUBLANES = 8

PROJ_BLOCK = 512
LIGHT_BLOCK = 512
SAMPLE_BLOCK = 128
MOE_BLOCK = 256
SCAN_TIME_BLOCK = 32
VMEM_LIMIT = 56 * 1024 * 1024


def _cparams(*sem):
    return pltpu.CompilerParams(dimension_semantics=sem, vmem_limit_bytes=VMEM_LIMIT)


def _dot(a, b):
    return jnp.dot(a, b, preferred_element_type=F32)


def _dot_nt(a, b):
    return lax.dot_general(a, b, (((1,), (1,)), ((), ())), preferred_element_type=F32)


def _dot_tn(a, b):
    return lax.dot_general(a, b, (((0,), (0,)), ((), ())), preferred_element_type=F32)


def _rms(x, g):
    return x * lax.rsqrt(jnp.mean(x * x, axis=-1, keepdims=True) + NORM_EPS) * g


def _sigmoid(x):
    return 1.0 / (1.0 + jnp.exp(-x))


def _row_spec(tm, n, offset=0):
    return pl.BlockSpec((tm, n), lambda i: (i + offset, 0))


def _head_row_spec(tm, n):
    return pl.BlockSpec((GLA_HEADS, tm, n), lambda i: (0, i, 0))


def _full_spec(shape):
    nd = len(shape)
    return pl.BlockSpec(shape, lambda *_: (0,) * nd)


def _norm_kernel(x_ref, g_ref, o_ref):
    o_ref[...] = _rms(x_ref[...], g_ref[...])


def _norm(x, g, tm):
    m, d = x.shape
    return pl.pallas_call(
        _norm_kernel, grid=(m // tm,),
        in_specs=[_row_spec(tm, d), _full_spec((1, d))],
        out_specs=_row_spec(tm, d),
        out_shape=jax.ShapeDtypeStruct((m, d), F32),
        compiler_params=_cparams("parallel"), name="rms_norm")(x, g.reshape(1, d))


def _norm_matmul_kernel(x_ref, g_ref, w_ref, o_ref):
    o_ref[...] = _dot(_rms(x_ref[...], g_ref[...]).astype(BF16), w_ref[...])


def _norm_matmul(x, g, w, tm):
    m, d = x.shape
    n = w.shape[1]
    return pl.pallas_call(
        _norm_matmul_kernel, grid=(m // tm,),
        in_specs=[_row_spec(tm, d), _full_spec((1, d)), _full_spec((d, n))],
        out_specs=_row_spec(tm, n),
        out_shape=jax.ShapeDtypeStruct((m, n), F32),
        compiler_params=_cparams("parallel"), name="norm_matmul")(x, g.reshape(1, d), w.astype(BF16))


def _rwkv_proj_kernel(h_ref, hp_ref, *refs, blocks_per_seq, norm_input):
    if norm_input:
        ng_ref, refs = refs[0], refs[1:]
        normed = lambda t_: _rms(t_, ng_ref[...])
    else:
        normed = lambda t_: t_
    (mu_ref, wr_ref, wk_ref, wv_ref, wq_ref, w1_ref, w2_ref, w0_ref, a1_ref, a2_ref, a0_ref, g1_ref, g2_ref,
     r_out, k_out, v_out, d_out, a_out, g_out, q_out) = refs

    def put(out, val):
        if blocks_per_seq:
            for p in range(RWKV_PAIRS):
                out[p] = val[:, p * LANES:(p + 1) * LANES]
        else:
            out[...] = val

    h = normed(h_ref[...])
    if blocks_per_seq:
        seq_start = (pl.program_id(0) % blocks_per_seq) == 0
        before = jnp.where(seq_start, 0.0, normed(hp_ref[SUBLANES - 1:SUBLANES, :]))
        row = lax.broadcasted_iota(jnp.int32, (h.shape[0], 1), 0)
        hp = jnp.where(row == 0, before, pltpu.roll(h, 1, axis=0))
    else:
        hp = hp_ref[...]
    xx = hp - h

    def mix(j):
        return (h + xx * mu_ref[j:j + 1, :]).astype(BF16)

    put(r_out, _dot(mix(0), wr_ref[...]))
    wl = w0_ref[...] + _dot(jnp.tanh(_dot(mix(1), w1_ref[...])).astype(BF16), w2_ref[...])
    put(k_out, _dot(mix(2), wk_ref[...]))
    put(v_out, _dot(mix(3), wv_ref[...]))
    al = a0_ref[...] + _dot(_dot(mix(4), a1_ref[...]).astype(BF16), a2_ref[...])
    g_out[...] = _dot(_sigmoid(_dot(mix(5), g1_ref[...])).astype(BF16), g2_ref[...])
    q_out[...] = _dot(h.astype(BF16), wq_ref[...])
    z = -wl
    softplus = jnp.maximum(z, 0.0) + jnp.log(1.0 + jnp.exp(-jnp.abs(z)))
    put(d_out, jnp.exp(-jnp.exp(-softplus - 0.5)))
    put(a_out, _sigmoid(al))


def _rwkv_proj(h, h_prev, ws, tm, seq_len, norm_g=None):
    m = h.shape[0]
    if norm_g is not None:
        assert h_prev is None
        ws = [norm_g.reshape(1, D_MODEL)] + list(ws)
    tw = TOK_WIDTH
    if h_prev is None:
        assert seq_len % tm == 0 and tm % SUBLANES == 0
        per8 = tm // SUBLANES
        hp_spec = pl.BlockSpec((SUBLANES, D_MODEL), lambda i: (jnp.maximum(i * per8 - 1, 0), 0))
        h_prev, blocks_per_seq = h, seq_len // tm
    else:
        hp_spec, blocks_per_seq = _row_spec(tm, D_MODEL), 0
    if blocks_per_seq:
        scan_spec = pl.BlockSpec((RWKV_PAIRS, tm, LANES), lambda i: (0, i, 0))
        scan_shape = jax.ShapeDtypeStruct((RWKV_PAIRS, m, LANES), F32)
    else:
        scan_spec, scan_shape = _row_spec(tm, tw), jax.ShapeDtypeStruct((m, tw), F32)
    return pl.pallas_call(
        functools.partial(_rwkv_proj_kernel, blocks_per_seq=blocks_per_seq, norm_input=norm_g is not None),
        grid=(m // tm,),
        in_specs=[_row_spec(tm, D_MODEL), hp_spec] + [_full_spec(w.shape) for w in ws],
        out_specs=[scan_spec] * 5 + [_row_spec(tm, tw), _row_spec(tm, MEM_WIDTH)],
        out_shape=[scan_shape] * 5 + [jax.ShapeDtypeStruct((m, tw), F32), jax.ShapeDtypeStruct((m, MEM_WIDTH), F32)],
        compiler_params=_cparams("parallel"), name="rwkv_proj")(h, h_prev, *ws)


RWKV_STEP_STATES = 48


def _rwkv_step_kernel(r_ref, k_ref, d_ref, a_ref, v_ref, kkp_ref, kap_ref, rkp_ref, lng_ref, lnb_ref, s0_ref,
                      y_ref, s_ref):
    r = r_ref[...]
    k = k_ref[...]
    a = a_ref[...]
    v = v_ref[...]
    kkr = k * kkp_ref[...]
    nrm = jnp.maximum(jnp.sqrt(jnp.sum(kkr * kkr, axis=2, keepdims=True)), 1e-12)
    kk = kkr * (1.0 / nrm)
    k2 = k * (1.0 + (a - 1.0) * kap_ref[...])
    s0 = s0_ref[...]
    sa = jnp.sum(s0 * (-kk), axis=2, keepdims=True)
    s_new = s0 * d_ref[...] + sa * (kk * a) + v * k2
    s_ref[...] = s_new
    y = jnp.sum(s_new * r, axis=2, keepdims=True)
    yc = y - jnp.mean(y, axis=1, keepdims=True)
    var = jnp.mean(yc * yc, axis=1, keepdims=True)
    bonus = jnp.sum(r * k2 * rkp_ref[...], axis=2, keepdims=True) * v
    y_ref[...] = yc * lax.rsqrt(var + RWKV_GN_EPS) * lng_ref[...] + lnb_ref[...] + bonus


def _rwkv_step(r, k, v, d, a, params, s0):
    b = r.shape[0]
    nh, n, g = RWKV_HEADS, RWKV_N, RWKV_STEP_STATES
    bh = b * nh
    assert bh % g == 0 and g % nh == 0
    row = lambda t_: t_.reshape(bh, 1, n)
    col = lambda t_: t_.reshape(bh, n, 1)
    prow = lambda p: jnp.tile(p.reshape(nh, 1, n), (g // nh, 1, 1))
    pcol = lambda p: jnp.tile(p.reshape(nh, n, 1), (g // nh, 1, 1))
    kkp, kap, rkp, lng, lnb = params
    rspec = pl.BlockSpec((g, 1, n), lambda i: (i, 0, 0))
    cspec = pl.BlockSpec((g, n, 1), lambda i: (i, 0, 0))
    sspec = pl.BlockSpec((g, n, n), lambda i: (i, 0, 0))
    y, s = pl.pallas_call(
        _rwkv_step_kernel, grid=(bh // g,),
        in_specs=[rspec] * 4 + [cspec] + [_full_spec((g, 1, n))] * 3 + [_full_spec((g, n, 1))] * 2 + [sspec],
        out_specs=[cspec, sspec],
        out_shape=[jax.ShapeDtypeStruct((bh, n, 1), F32), jax.ShapeDtypeStruct((bh, n, n), F32)],
        compiler_params=_cparams("parallel"), name="rwkv_step")(
            row(r), row(k), row(d), row(a), col(v), prow(kkp), prow(kap), prow(rkp), pcol(lng), pcol(lnb),
            s0.reshape(bh, n, n))
    return y.reshape(b, nh * n), s.reshape(b, nh, n, n)


PREP_TILES = 6


def _rwkv_prep_kernel(r_ref, k_ref, v_ref, d_ref, a_ref, kkp_ref, kap_ref, o_ref, *, tc, nb):
    n = RWKV_N
    rows_per_pair = nb * tc
    ins = [ref.reshape(RWKV_PAIRS * rows_per_pair, LANES) for ref in (r_ref, k_ref, v_ref, d_ref, a_ref)]
    zero_rows = jnp.zeros((LANES - RWKV_HEADS * nb, LANES), F32)
    first_half = lax.broadcasted_iota(jnp.int32, (nb, LANES), 1) < n
    kkp = kkp_ref[...]
    kap = kap_ref[...]

    def load_transposed_pair(x2, t):
        pieces = []
        for p in range(RWKV_PAIRS):
            now = x2[pl.ds(p * rows_per_pair + t, nb, stride=tc), :]
            nxt = x2[pl.ds(p * rows_per_pair + t + 1, nb, stride=tc), :]
            pieces += [jnp.where(first_half, now, pltpu.roll(nxt, n, axis=1)),
                       jnp.where(first_half, pltpu.roll(now, n, axis=1), nxt)]
        both = jnp.concatenate(pieces + [zero_rows], axis=0).T
        return both[:n, :], both[n:, :]

    def emit(t, r_t, k_t, v_t, d_t, a_t):
        kkr = k_t * kkp
        nrm = jnp.maximum(jnp.sqrt(jnp.sum(kkr * kkr, axis=0, keepdims=True)), 1e-12)
        kk = kkr * (1.0 / nrm)
        k2 = k_t * (1.0 + (a_t - 1.0) * kap)
        o_ref[t, 0] = -kk
        o_ref[t, 1] = d_t
        o_ref[t, 2] = kk * a_t
        o_ref[t, 3] = k2
        o_ref[t, 4] = r_t
        o_ref[t, 5] = v_t

    def two_steps(u, carry):
        t = 2 * u
        tiles = [load_transposed_pair(x2, t) for x2 in ins]
        emit(t, *[tile[0] for tile in tiles])
        emit(t + 1, *[tile[1] for tile in tiles])
        return carry

    lax.fori_loop(0, tc // 2, two_steps, 0, unroll=2)


def _rwkv_state_scan_kernel(x_ref, rkp_ref, lng_ref, lnb_ref, y_ref, sfin_ref, s_scr, sa_scr, yrow_scr, *, tc):
    n = RWKV_N
    groups = n // SUBLANES

    @pl.when(pl.program_id(0) == 0)
    def _():
        s_scr[...] = jnp.zeros(s_scr.shape, F32)

    rkp = rkp_ref[...]
    lng = lng_ref[...]
    lnb = lnb_ref[...]
    sub = lax.broadcasted_iota(jnp.int32, (SUBLANES, LANES), 0)
    low4 = sub < 4
    low2 = (sub & 3) < 2
    low1 = (sub & 1) == 0

    def fold(x, y, dist, low):
        if dist == 4:
            return jnp.where(low, x, y) + pltpu.roll(jnp.where(low, y, x), 4, axis=0)
        return (jnp.where(low, x, pltpu.roll(y, dist, axis=0))
                + jnp.where(low, pltpu.roll(x, SUBLANES - dist, axis=0), y))

    def sublane_sums(ps):
        z = [fold(ps[0], ps[4], 4, low4), fold(ps[2], ps[6], 4, low4),
             fold(ps[1], ps[5], 4, low4), fold(ps[3], ps[7], 4, low4)]
        return fold(fold(z[0], z[1], 2, low2), fold(z[2], z[3], 2, low2), 1, low1)

    def tile_sum(x):
        acc = x[0:SUBLANES]
        for u in range(1, groups):
            acc = acc + x[u * SUBLANES:(u + 1) * SUBLANES]
        return acc

    def step(t, carry):
        nkk = x_ref[t, 0]
        d_t = x_ref[t, 1]
        b_t = x_ref[t, 2]
        k2 = x_ref[t, 3]
        r_t = x_ref[t, 4]
        for g in range(groups):
            sa_scr[g * SUBLANES:(g + 1) * SUBLANES, :] = sublane_sums(
                [tile_sum(s_scr[g * SUBLANES + u] * nkk) for u in range(SUBLANES)])
        for g in range(groups):
            ps = []
            for u in range(SUBLANES):
                i = g * SUBLANES + u
                s_n = s_scr[i] * d_t + sa_scr[pl.ds(i, 1), :] * b_t + x_ref[t, 5, pl.ds(i, 1), :] * k2
                s_scr[i] = s_n
                ps.append(tile_sum(s_n * r_t))
            yrow_scr[g * SUBLANES:(g + 1) * SUBLANES, :] = sublane_sums(ps)
        y = yrow_scr[...]
        yc = y - jnp.mean(y, axis=0, keepdims=True)
        var = jnp.mean(yc * yc, axis=0, keepdims=True)
        bonus = jnp.sum(r_t * k2 * rkp, axis=0, keepdims=True) * x_ref[t, 5]
        y_ref[t] = yc * lax.rsqrt(var + RWKV_GN_EPS) * lng + lnb + bonus
        return carry

    lax.fori_loop(0, tc, step, 0)

    @pl.when(pl.program_id(0) == pl.num_programs(0) - 1)
    def _():
        sfin_ref[...] = s_scr[...]


def _rwkv_unprep_kernel(y_ref, o_ref, *, tc, nb):
    n = RWKV_N
    rows_per_pair = nb * tc
    o2 = o_ref.reshape(RWKV_PAIRS * rows_per_pair, LANES)
    first_half = lax.broadcasted_iota(jnp.int32, (nb, LANES), 1) < n

    def two_steps(u, carry):
        t = 2 * u
        w = jnp.concatenate([y_ref[t], y_ref[t + 1]], axis=0).T
        for p in range(RWKV_PAIRS):
            even = w[(2 * p) * nb:(2 * p + 1) * nb, :]
            odd = w[(2 * p + 1) * nb:(2 * p + 2) * nb, :]
            o2[pl.ds(p * rows_per_pair + t, nb, stride=tc), :] = jnp.where(
                first_half, even, pltpu.roll(odd, n, axis=1))
            o2[pl.ds(p * rows_per_pair + t + 1, nb, stride=tc), :] = jnp.where(
                first_half, pltpu.roll(even, n, axis=1), odd)
        return carry

    lax.fori_loop(0, tc // 2, two_steps, 0, unroll=2)


def _rwkv_prompt_mixer(r, k, v, d, a, kkp, kap, rkp, lng, lnb, nb, t, tc):
    n = RWKV_N
    assert tc % SUBLANES == 0 and tc % 4 == 0 and t % tc == 0 and RWKV_HEADS * nb <= LANES
    grid = (t // tc,)
    tok = pl.BlockSpec((RWKV_PAIRS, nb, tc, LANES), lambda ti: (0, 0, ti, 0))
    par = pl.BlockSpec((n, LANES), lambda ti: (0, 0))
    tiles = pl.BlockSpec((tc, PREP_TILES, n, LANES), lambda ti: (ti, 0, 0, 0))
    ytile = pl.BlockSpec((tc, n, LANES), lambda ti: (ti, 0, 0))
    prepared = pl.pallas_call(
        functools.partial(_rwkv_prep_kernel, tc=tc, nb=nb), grid=grid,
        in_specs=[tok] * 5 + [par] * 2, out_specs=tiles,
        out_shape=jax.ShapeDtypeStruct((t, PREP_TILES, n, LANES), F32),
        compiler_params=_cparams("parallel"), name="rwkv_prep")(
            *[x.reshape(RWKV_PAIRS, nb, t, LANES) for x in (r, k, v, d, a)], kkp, kap)
    y, s = pl.pallas_call(
        functools.partial(_rwkv_state_scan_kernel, tc=tc), grid=grid,
        in_specs=[tiles, par, par, par],
        out_specs=[ytile, pl.BlockSpec((n, n, LANES), lambda ti: (0, 0, 0))],
        out_shape=[jax.ShapeDtypeStruct((t, n, LANES), F32), jax.ShapeDtypeStruct((n, n, LANES), F32)],
        scratch_shapes=[pltpu.VMEM((n, n, LANES), F32), pltpu.VMEM((n, LANES), F32), pltpu.VMEM((n, LANES), F32)],
        compiler_params=_cparams("arbitrary"), name="rwkv_state_scan")(prepared, rkp, lng, lnb)
    tok_out = pl.pallas_call(
        functools.partial(_rwkv_unprep_kernel, tc=tc, nb=nb), grid=grid,
        in_specs=[ytile], out_specs=tok,
        out_shape=jax.ShapeDtypeStruct((RWKV_PAIRS, nb, t, LANES), F32),
        compiler_params=_cparams("parallel"), name="rwkv_unprep")(y)
    return tok_out.reshape(RWKV_PAIRS, nb * t, LANES), s


def _gla_proj_kernel(h_ref, wq_ref, wk_ref, wv_ref, wr_ref, wm_ref, a1_ref, a2_ref, ab_ref,
                     q_out, k_out, v_out, r_out, la_out, qm_out):
    hb = h_ref[...].astype(BF16)
    low = _dot(hb, a1_ref[...]).astype(BF16)
    for hd in range(GLA_HEADS):
        q_out[hd] = _dot(hb, wq_ref[hd])
        k_out[hd] = _dot(hb, wk_ref[hd])
        v_out[hd] = _dot(hb, wv_ref[hd])
        r_out[hd] = _dot(hb, wr_ref[hd])
        x = _dot(low, a2_ref[hd]) + ab_ref[hd]
        log_sigmoid = jnp.minimum(x, 0.0) - jnp.log(1.0 + jnp.exp(-jnp.abs(x)))
        la_out[hd] = log_sigmoid / GLA_TAU
    qm_out[...] = _dot(hb, wm_ref[...])


def _gla_weights(w_in, a1, a2, ab):
    kw, tw, nh = GLA_KW, TOK_WIDTH, GLA_HEADS

    def heads(w, d):
        return w.reshape(w.shape[0], nh, d).transpose(1, 0, 2)

    return [heads(w_in[:, :kw], GLA_DK).astype(BF16), heads(w_in[:, kw:2 * kw], GLA_DK).astype(BF16),
            heads(w_in[:, 2 * kw:2 * kw + tw], GLA_DV).astype(BF16),
            heads(w_in[:, 2 * kw + tw:3 * tw], GLA_DV).astype(BF16), w_in[:, 3 * tw:].astype(BF16),
            a1.astype(BF16), heads(a2, GLA_DK).astype(BF16), ab.reshape(nh, 1, GLA_DK)]


def _gla_proj(h, ws, tm):
    m = h.shape[0]
    nh = GLA_HEADS
    widths = [GLA_DK, GLA_DK, GLA_DV, GLA_DV, GLA_DK]
    return pl.pallas_call(
        _gla_proj_kernel, grid=(m // tm,),
        in_specs=[_row_spec(tm, D_MODEL)] + [_full_spec(w.shape) for w in ws],
        out_specs=[_head_row_spec(tm, w) for w in widths] + [_row_spec(tm, MEM_WIDTH)],
        out_shape=[jax.ShapeDtypeStruct((nh, m, w), F32) for w in widths] + [jax.ShapeDtypeStruct((m, MEM_WIDTH), F32)],
        compiler_params=_cparams("parallel"), name="gla_proj")(h, *ws)


def _gla_out_norm(o, g):
    return o * lax.rsqrt(jnp.mean(o * o, axis=-1, keepdims=True) + NORM_EPS) * g


def _gla_chunk_kernel(q_ref, k_ref, v_ref, la_ref, ng_ref, o_ref, sfin_ref, st_scr):
    c, dk = GLA_CHUNK, GLA_DK

    @pl.when(pl.program_id(1) == 0)
    def _():
        st_scr[...] = jnp.zeros(st_scr.shape, F32)

    row = lax.broadcasted_iota(jnp.int32, (c, c), 0)
    col = lax.broadcasted_iota(jnp.int32, (c, c), 1)
    tril = (row >= col).astype(F32)
    rr = lax.broadcasted_iota(jnp.int32, (c, dk), 0)
    ones_sum = jnp.ones((dk, LANES), BF16)

    heads = range(GLA_HEADS)
    tril_b = tril.astype(BF16)

    def cumsum_rows(la):
        hi = la.astype(BF16)
        r1 = la - hi.astype(F32)
        mid = r1.astype(BF16)
        lo = (r1 - mid.astype(F32)).astype(BF16)
        return _dot(tril_b, hi) + _dot(tril_b, mid) + _dot(tril_b, lo)

    def tile_roll(x, dlt):
        return pltpu.roll(x.reshape(c // GLA_TILE, GLA_TILE, dk), dlt, axis=1).reshape(c, dk)

    k = [k_ref[hd] for hd in heads]
    vb = [v_ref[hd].astype(BF16) for hd in heads]
    q = [q_ref[hd] * (dk ** -0.5) for hd in heads]
    b = [cumsum_rows(la_ref[hd]) for hd in heads]
    st = [st_scr[hd] for hd in heads]
    inter = [_dot_nt((q[hd] * jnp.exp(b[hd])).astype(BF16), st[hd].astype(BF16)) for hd in heads]

    att = [jnp.zeros((c, c), F32) for hd in heads]
    blk = c // 2
    while blk >= GLA_TILE:
        two = 2 * blk
        upper = (rr & (two - 1)) >= blk
        same_block = (row ^ col) < two
        parts = []
        for hd in heads:
            b_ref_rows = jnp.concatenate(
                [jnp.broadcast_to(b[hd][s0 + blk - 1:s0 + blk, :], (two, dk)) for s0 in range(0, c, two)], axis=0)
            q_l = jnp.where(upper, q[hd] * jnp.exp(jnp.minimum(b[hd] - b_ref_rows, 0.0)), 0.0).astype(BF16)
            k_l = jnp.where(upper, 0.0, k[hd] * jnp.exp(jnp.minimum(b_ref_rows - b[hd], 0.0))).astype(BF16)
            parts.append(_dot_nt(q_l, k_l))
        att = [att[hd] + jnp.where(same_block, parts[hd], 0.0) for hd in heads]
        blk //= 2

    sums = []
    for hd in heads:
        prods = [(q[hd] * k[hd]).astype(BF16)]
        for dlt in range(1, GLA_TILE):
            p = q[hd] * tile_roll(k[hd], dlt) * jnp.exp(jnp.minimum(b[hd] - tile_roll(b[hd], dlt), 0.0))
            prods.append(jnp.where((rr & (GLA_TILE - 1)) >= dlt, p, 0.0).astype(BF16))
        sums.append(_dot(jnp.concatenate(prods, axis=0), ones_sum))
    for dlt in range(GLA_TILE):
        on_diag = col == row - dlt
        att = [att[hd] + jnp.where(on_diag, sums[hd][dlt * c:(dlt + 1) * c, :c], 0.0) for hd in heads]

    o = [inter[hd] + _dot(att[hd].astype(BF16), vb[hd]) for hd in heads]
    for hd in heads:
        o_ref[hd] = _gla_out_norm(o[hd], ng_ref[...])
    for hd in heads:
        b_end = b[hd][c - 1:c, :]
        kd = (k[hd] * jnp.exp(b_end - b[hd])).astype(BF16)
        st_scr[hd] = jnp.exp(b_end) * st[hd] + _dot_tn(vb[hd], kd)

    @pl.when(pl.program_id(1) == pl.num_programs(1) - 1)
    def _():
        sfin_ref[0] = st_scr[...]


def _gla_chunk_scan(q, k, v, la, norm_g, batch, t):
    nh, dk, dv, c = GLA_HEADS, GLA_DK, GLA_DV, GLA_CHUNK
    nc = t // c
    kspec = pl.BlockSpec((nh, c, dk), lambda i, j: (0, i * nc + j, 0))
    vspec = pl.BlockSpec((nh, c, dv), lambda i, j: (0, i * nc + j, 0))
    sspec = pl.BlockSpec((1, nh, dv, dk), lambda i, j: (i, 0, 0, 0))
    return pl.pallas_call(
        _gla_chunk_kernel, grid=(batch, nc),
        in_specs=[kspec, kspec, vspec, kspec, pl.BlockSpec((1, dv), lambda i, j: (0, 0))],
        out_specs=[vspec, sspec],
        out_shape=[jax.ShapeDtypeStruct((nh, batch * t, dv), F32), jax.ShapeDtypeStruct((batch, nh, dv, dk), F32)],
        scratch_shapes=[pltpu.VMEM((nh, dv, dk), F32)],
        compiler_params=_cparams("parallel", "arbitrary"), name="gla_chunk")(q, k, v, la, norm_g.reshape(1, dv))


GLA_STEP_GROUP = 8


def _gla_step_kernel(q_ref, k_ref, v_ref, la_ref, ng_ref, s0_ref, o_ref, s_ref):
    for g in range(GLA_STEP_GROUP):
        kv = k_ref[g].astype(BF16).astype(F32) * v_ref[g].astype(BF16).astype(F32)
        s_new = jnp.exp(la_ref[g]) * s0_ref[g] + kv
        s_ref[g] = s_new
        o = jnp.sum((q_ref[g] * (GLA_DK ** -0.5)) * s_new, axis=0, keepdims=True)
        o_ref[g] = _gla_out_norm(o, ng_ref[...])


def _gla_step(q, k, v, la, norm_g, s0):
    bh, dk, _ = q.shape
    dv = v.shape[2]
    g = GLA_STEP_GROUP
    cspec = pl.BlockSpec((g, dk, 1), lambda i: (i, 0, 0))
    vspec = pl.BlockSpec((g, 1, dv), lambda i: (i, 0, 0))
    sspec = pl.BlockSpec((g, dk, dv), lambda i: (i, 0, 0))
    return pl.pallas_call(
        _gla_step_kernel, grid=(bh // g,),
        in_specs=[cspec, cspec, vspec, cspec, _full_spec((1, dv)), sspec],
        out_specs=[vspec, sspec],
        out_shape=[jax.ShapeDtypeStruct((bh, 1, dv), F32), jax.ShapeDtypeStruct((bh, dk, dv), F32)],
        compiler_params=_cparams("parallel"), name="gla_step")(q, k, v, la, norm_g.reshape(1, dv), s0)


def _mem_attn_kernel(q_ref, k_ref, v_ref, o_ref):
    q = q_ref[0]
    k = k_ref[0].astype(BF16)
    v_ones = jnp.concatenate([v_ref[0].astype(BF16), jnp.ones((N_MEM, LANES), BF16)], axis=1)
    head_of_lane = lax.broadcasted_iota(jnp.int32, (1, MEM_WIDTH), 1) // MEM_HEAD_DIM
    heads = range(MEM_HEADS)
    mine = [head_of_lane == h for h in heads]
    qh = [jnp.where(mine[h], q, 0.0).astype(BF16) for h in heads]
    s = [_dot_nt(qh[h], k) * (MEM_HEAD_DIM ** -0.5) for h in heads]
    e = [jnp.exp(s[h] - jnp.max(s[h], axis=-1, keepdims=True)).astype(BF16) for h in heads]
    ev = [_dot(e[h], v_ones) for h in heads]
    out = jnp.zeros(q.shape, F32)
    for h in heads:
        inv = 1.0 / ev[h][:, MEM_WIDTH:]
        out = out + jnp.where(mine[h], ev[h][:, :MEM_WIDTH] * jnp.concatenate([inv, inv], axis=1), 0.0)
    o_ref[0] = out


def _mem_attn(q, mem_k, mem_v, layer, tq):
    b, t, w = q.shape
    qspec = pl.BlockSpec((1, tq, w), lambda i, j: (i, j, 0))
    mspec = pl.BlockSpec((None, 1, N_MEM, w), lambda i, j: (layer, i, 0, 0))
    return pl.pallas_call(
        _mem_attn_kernel, grid=(b, t // tq),
        in_specs=[qspec, mspec, mspec], out_specs=qspec,
        out_shape=jax.ShapeDtypeStruct((b, t, w), F32),
        compiler_params=_cparams("parallel", "parallel"), name="mem_attn")(q, mem_k, mem_v)


def _out_proj_kernel(*refs, layout, aliased):
    tok_ref, gate_ref, att_ref, x_ref, wo_ref, g_ref, wr_hi_ref, wr_lo_ref, br_ref = refs[:9]
    x1_out, h2_out, logit_out = refs[9 + aliased:]
    x1 = x_ref[...] + _dot(att_ref[...].astype(BF16), wo_ref[TOK_WIDTH:, :])
    if layout == "gla_heads":
        for hd in range(GLA_HEADS):
            gate = gate_ref[hd]
            mixed = (tok_ref[hd] * (gate * _sigmoid(gate))).astype(BF16)
            x1 = x1 + _dot(mixed, wo_ref[hd * GLA_DV:(hd + 1) * GLA_DV, :])
    else:
        if layout == "rwkv_pairs":
            tok = jnp.concatenate([tok_ref[p] for p in range(RWKV_PAIRS)], axis=1)
        else:
            tok = tok_ref[...]
        x1 = x1 + _dot((tok * gate_ref[...]).astype(BF16), wo_ref[:TOK_WIDTH, :])
    x1_out[...] = x1
    h2 = _rms(x1, g_ref[...])
    h2_out[...] = h2
    h_hi = h2.astype(BF16)
    h_lo = (h2 - h_hi.astype(F32)).astype(BF16)
    logit_out[...] = (_dot(h_hi, wr_hi_ref[...]) + _dot(h_lo, wr_hi_ref[...]) + _dot(h_hi, wr_lo_ref[...])
                      + br_ref[...])


def _out_proj(tok, gate, att, x, ws, layout, tm, h2_rows, h2_row_offset, h2_buffer=None):
    m = x.shape[0]
    aliased = h2_buffer is not None
    rows_spec = _row_spec(tm, TOK_WIDTH)
    tok_spec, gate_spec = {
        "rows": (rows_spec, rows_spec),
        "rwkv_pairs": (pl.BlockSpec((RWKV_PAIRS, tm, LANES), lambda i: (0, i, 0)), rows_spec),
        "gla_heads": (_head_row_spec(tm, GLA_DV), _head_row_spec(tm, GLA_DV))}[layout]
    in_specs = ([tok_spec, gate_spec, _row_spec(tm, MEM_WIDTH), _row_spec(tm, D_MODEL)]
                + [_full_spec(w.shape) for w in ws])
    args = [tok, gate, att, x, *ws]
    if aliased:
        in_specs.append(pl.BlockSpec(memory_space=pl.ANY))
        args.append(h2_buffer)
    return pl.pallas_call(
        functools.partial(_out_proj_kernel, layout=layout, aliased=int(aliased)), grid=(m // tm,),
        in_specs=in_specs,
        out_specs=[_row_spec(tm, D_MODEL), _row_spec(tm, D_MODEL, h2_row_offset // tm), _row_spec(tm, ROUTER_LANES)],
        out_shape=[jax.ShapeDtypeStruct((m, D_MODEL), F32), jax.ShapeDtypeStruct((h2_rows, D_MODEL), F32),
                   jax.ShapeDtypeStruct((m, ROUTER_LANES), F32)],
        input_output_aliases={len(args) - 1: 1} if aliased else {},
        compiler_params=_cparams("parallel"), name="out_proj")(*args)


def _moe_kernel(tile_ref, exp_ref, nitem_ref, lo_ref, hi_ref, x_ref, gate_ref, wu_ref, wd_ref, o_ref, wu_scr, wd_scr):
    w = pl.program_id(0)
    prev = jnp.maximum(w - 1, 0)
    e = exp_ref[w]
    valid = w < nitem_ref[0]

    @pl.when(jnp.logical_and(valid, jnp.logical_or(w == 0, e != exp_ref[prev])))
    def _():
        wu_scr[...] = wu_ref[0, 0].astype(BF16)
        wd_scr[...] = wd_ref[0, 0].astype(BF16)

    @pl.when(jnp.logical_or(w == 0, tile_ref[w] != tile_ref[prev]))
    def _():
        o_ref[...] = jnp.zeros(o_ref.shape, F32)

    @pl.when(valid)
    def _():
        gu = _dot(x_ref[...].astype(BF16), wu_scr[...])
        g = gu[:, :EXPERT_FF]
        act = (g * _sigmoid(g) * gu[:, EXPERT_FF:]).astype(BF16)
        out = _dot(act, wd_scr[...]) * gate_ref[...]
        rows = tile_ref[w] * MOE_BLOCK + lax.broadcasted_iota(jnp.int32, (MOE_BLOCK, 1), 0)
        mine = jnp.logical_and(rows >= lo_ref[e], rows < hi_ref[e])
        o_ref[...] = o_ref[...] + jnp.where(mine, out, 0.0)


def _moe_ffn(xs, row_gate, item_tile, item_expert, n_items, lo, hi, w_up, w_down, layer):
    tm = MOE_BLOCK
    n_work = item_tile.shape[0]
    a = xs.shape[0]
    row_map = lambda w, tile, ex, ni, lo_, hi_: (tile[w], 0)
    exp_map = lambda w, tile, ex, ni, lo_, hi_: (layer, ex[w], 0, 0)
    grid_spec = pltpu.PrefetchScalarGridSpec(
        num_scalar_prefetch=5, grid=(n_work,),
        in_specs=[pl.BlockSpec((tm, D_MODEL), row_map),
                  pl.BlockSpec((tm, 1), row_map),
                  pl.BlockSpec((1, 1, D_MODEL, 2 * EXPERT_FF), exp_map),
                  pl.BlockSpec((1, 1, EXPERT_FF, D_MODEL), exp_map)],
        out_specs=pl.BlockSpec((tm, D_MODEL), row_map),
        scratch_shapes=[pltpu.VMEM((D_MODEL, 2 * EXPERT_FF), BF16), pltpu.VMEM((EXPERT_FF, D_MODEL), BF16)])
    return pl.pallas_call(
        _moe_kernel, grid_spec=grid_spec,
        out_shape=jax.ShapeDtypeStruct((a, D_MODEL), F32),
        compiler_params=_cparams("arbitrary"), name="moe_ffn")(
            item_tile, item_expert, n_items, lo, hi, xs, row_gate, w_up, w_down)


def _route(logits, n_prompt, tm):
    m = logits.shape[0]
    n_sample = m - n_prompt
    a = 2 * m
    gl = logits[:, :N_GROUPS]
    el = logits[:, N_GROUPS:N_GROUPS + N_EXPERTS].reshape(m, N_GROUPS, EXPERTS_PER_GROUP)
    group = jnp.argmax(gl, -1).astype(jnp.int32)
    p_group = jnp.max(jax.nn.softmax(gl, -1), -1, keepdims=True)
    in_group = jnp.take_along_axis(el, group[:, None, None], axis=1)[:, 0]
    top_val, top_idx = lax.top_k(in_group, 2)
    gate = p_group * jax.nn.softmax(top_val, -1)
    expert = group[:, None] * EXPERTS_PER_GROUP + top_idx.astype(jnp.int32)

    def by_id(t):
        return jnp.concatenate([t[:n_prompt, 0], t[:n_prompt, 1], t[n_prompt:, 0], t[n_prompt:, 1]])

    tok_of_id = jnp.asarray(np.concatenate([np.arange(n_prompt), np.arange(n_prompt),
                                            n_prompt + np.arange(n_sample), n_prompt + np.arange(n_sample)]), jnp.int32)
    flat_e = by_id(expert)
    ids = jnp.arange(a, dtype=jnp.int32)
    _, order, gate_sorted, tok_sorted = lax.sort((flat_e, ids, by_id(gate), tok_of_id), num_keys=1, is_stable=True)
    _, inv = lax.sort((order, ids), num_keys=1)
    experts = jnp.arange(N_EXPERTS, dtype=jnp.int32)
    counts = jnp.sum((flat_e[:, None] == experts[None, :]).astype(jnp.int32), axis=0)
    hi = jnp.cumsum(counts).astype(jnp.int32)
    lo = hi - counts
    n_tiles = a // tm
    first_tile = lo // tm
    tiles_of = jnp.where(counts > 0, (hi - 1) // tm - first_tile + 1, 0)
    item_end = jnp.cumsum(tiles_of).astype(jnp.int32)
    n_items = item_end[-1:]
    n_work = n_tiles + N_EXPERTS - 1
    w = jnp.minimum(jnp.arange(n_work, dtype=jnp.int32), n_items[0] - 1)
    item_expert = jnp.sum((item_end[None, :] <= w[:, None]).astype(jnp.int32), axis=1)
    onehot = (item_expert[:, None] == experts[None, :]).astype(jnp.int32)
    item_tile = jnp.sum(onehot * (first_tile - (item_end - tiles_of))[None, :], axis=1) + w
    return tok_sorted, gate_sorted.reshape(a, 1), inv, item_tile.astype(jnp.int32), item_expert, n_items, lo, hi


def _combine_kernel(x_ref, r0_ref, r1_ref, g_ref, *outs):
    x2 = x_ref[...] + (r0_ref[...] + r1_ref[...])
    outs[-1][...] = _rms(x2, g_ref[...])
    if len(outs) == 2:
        outs[0][...] = x2


def _combine(x1, gathered, first_row, g, tm, last_layer):
    m = x1.shape[0]
    spec = _row_spec(tm, D_MODEL)
    n_out = 1 if last_layer else 2
    outs = pl.pallas_call(
        _combine_kernel, grid=(m // tm,),
        in_specs=[spec, _row_spec(tm, D_MODEL, first_row // tm), _row_spec(tm, D_MODEL, (first_row + m) // tm),
                  _full_spec((1, D_MODEL))],
        out_specs=[spec] * n_out,
        out_shape=[jax.ShapeDtypeStruct((m, D_MODEL), F32)] * n_out,
        compiler_params=_cparams("parallel"), name="moe_combine")(x1, gathered, gathered, g.reshape(1, D_MODEL))
    return (None, outs[0]) if last_layer else tuple(outs)


def _seq_scan_param(p, b):
    lanes = jnp.repeat(p.reshape(RWKV_HEADS, RWKV_N).T, b, axis=1)
    return jnp.pad(lanes, ((0, 0), (0, LANES - RWKV_HEADS * b)))


def kernel(x_prompt, x_sample, mem_prompt, state_rwkv_S, state_rwkv_shift, state_gla_S, cache_mem_k, cache_mem_v, norm_mix_g, norm_ffn_g, norm_mem_g, norm_final_g, w_in, w_out, w_mem_kv, rw_mu, rw_w0, rw_w1, rw_w2, rw_a0, rw_a1, rw_a2, rw_g1, rw_g2, rw_k_k, rw_k_a, rw_r_k, rw_ln_g, rw_ln_b, gla_a1, gla_a2, gla_ab, gla_norm_g, router_wg, router_bg, router_we, router_be, exp_w_up, exp_w_down):
    bp, tp, _ = x_prompt.shape
    bs, ts, _ = x_sample.shape
    assert ts == 1 and tp % GLA_CHUNK == 0 and tp % SCAN_TIME_BLOCK == 0
    np_ = bp * tp
    ns = bs * ts
    m = np_ + ns
    assert np_ % LIGHT_BLOCK == 0 and ns % SAMPLE_BLOCK == 0 and (2 * m) % MOE_BLOCK == 0
    depth = w_in.shape[0]
    nh = GLA_HEADS
    tw = TOK_WIDTH
    bf = lambda t_: t_.astype(BF16)

    mem2d = mem_prompt.reshape(bp * N_MEM, D_MODEL)
    mem_kv = [_norm_matmul(mem2d, norm_mem_g[i], w_mem_kv[i], 512) for i in range(depth)]
    pk = jnp.stack([kv[:, :MEM_WIDTH].reshape(bp, N_MEM, MEM_WIDTH) for kv in mem_kv])
    pv = jnp.stack([kv[:, MEM_WIDTH:].reshape(bp, N_MEM, MEM_WIDTH) for kv in mem_kv])
    prompt_mem_k = pk.reshape(depth, bp, N_MEM, MEM_HEADS, MEM_HEAD_DIM)
    prompt_mem_v = pv.reshape(depth, bp, N_MEM, MEM_HEADS, MEM_HEAD_DIM)
    sk = cache_mem_k.reshape(depth, bs, N_MEM, MEM_WIDTH)
    sv = cache_mem_v.reshape(depth, bs, N_MEM, MEM_WIDTH)

    x_p = x_prompt.reshape(np_, D_MODEL)
    x_s = x_sample.reshape(ns, D_MODEL)
    h_p = None
    h_s = _norm(x_s, norm_mix_g[0], SAMPLE_BLOCK)

    w_router = jnp.zeros((depth, D_MODEL, ROUTER_LANES), F32)
    w_router = w_router.at[:, :, :N_GROUPS].set(router_wg).at[:, :, N_GROUPS:N_GROUPS + N_EXPERTS].set(router_we)
    b_router = jnp.zeros((depth, 1, ROUTER_LANES), F32)
    b_router = b_router.at[:, 0, :N_GROUPS].set(router_bg).at[:, 0, N_GROUPS:N_GROUPS + N_EXPERTS].set(router_be)

    p_rw_S, p_rw_shift, p_gla_S, s_rw_S, s_rw_shift, s_gla_S = [], [], [], [], [], []
    for i in range(depth):
        j = i // 2
        if i % 2 == 0:
            wi = w_in[i]
            ws = [rw_mu[j], bf(wi[:, :tw]), bf(wi[:, tw:2 * tw]), bf(wi[:, 2 * tw:3 * tw]), bf(wi[:, 3 * tw:]),
                  bf(rw_w1[j]), bf(rw_w2[j]), rw_w0[j].reshape(1, tw), bf(rw_a1[j]), bf(rw_a2[j]),
                  rw_a0[j].reshape(1, tw), bf(rw_g1[j]), bf(rw_g2[j])]
            if h_p is None:
                *rkvda_p, gate_p, qm_p = _rwkv_proj(x_p, None, ws, PROJ_BLOCK, tp, norm_g=norm_mix_g[i])
                shift_p = _norm(x_prompt[:, -1, :], norm_mix_g[i], bp)
            else:
                *rkvda_p, gate_p, qm_p = _rwkv_proj(h_p, None, ws, PROJ_BLOCK, tp)
                shift_p = h_p.reshape(bp, tp, D_MODEL)[:, -1]
            *rkvda_s, gate_s, qm_s = _rwkv_proj(h_s, state_rwkv_shift[j], ws, SAMPLE_BLOCK, ts)
            pvec = [rw_k_k[j], rw_k_a[j], rw_r_k[j], rw_ln_g[j], rw_ln_b[j]]
            tok_p, sp = _rwkv_prompt_mixer(*rkvda_p, *[_seq_scan_param(p, bp) for p in pvec], bp, tp, SCAN_TIME_BLOCK)
            tok_s, ss = _rwkv_step(*rkvda_s, pvec, state_rwkv_S[j])
            p_rw_S.append(sp[:, :, :RWKV_HEADS * bp].reshape(RWKV_N, RWKV_N, RWKV_HEADS, bp).transpose(3, 2, 0, 1))
            s_rw_S.append(ss)
            p_rw_shift.append(shift_p)
            s_rw_shift.append(h_s)
            layout_p, layout_s = "rwkv_pairs", "rows"
        else:
            ws = _gla_weights(w_in[i], gla_a1[j], gla_a2[j], gla_ab[j])
            q_p, k_p, v_p, gate_p, la_p, qm_p = _gla_proj(h_p, ws, PROJ_BLOCK)
            q_s, k_s, v_s, gate_s, la_s, qm_s = _gla_proj(h_s, ws, SAMPLE_BLOCK)
            tok_p, sp_t = _gla_chunk_scan(q_p, k_p, v_p, la_p, gla_norm_g[j], bp, tp)
            col = lambda t_: t_.transpose(1, 0, 2).reshape(bs * nh, GLA_DK, 1)
            os_, ss = _gla_step(col(q_s), col(k_s), v_s.transpose(1, 0, 2).reshape(bs * nh, 1, GLA_DV), col(la_s),
                                gla_norm_g[j], state_gla_S[j].reshape(bs * nh, GLA_DK, GLA_DV))
            tok_s = os_.reshape(bs, nh, GLA_DV).transpose(1, 0, 2)
            p_gla_S.append(sp_t.transpose(0, 1, 3, 2))
            s_gla_S.append(ss.reshape(bs, nh, GLA_DK, GLA_DV))
            layout_p = layout_s = "gla_heads"

        att_p = _mem_attn(qm_p.reshape(bp, tp, MEM_WIDTH), pk, pv, i, 512).reshape(np_, MEM_WIDTH)
        att_s = _mem_attn(qm_s.reshape(bs, ts, MEM_WIDTH), sk, sv, i, 1).reshape(ns, MEM_WIDTH)
        wr_hi = bf(w_router[i])
        wr_lo = bf(w_router[i] - wr_hi.astype(F32))
        ws = [bf(w_out[i]), norm_ffn_g[i].reshape(1, D_MODEL), wr_hi, wr_lo, b_router[i]]
        x1_p, h2, logits_p = _out_proj(tok_p, gate_p, att_p, x_p, ws, layout_p, PROJ_BLOCK, m, 0)
        x1_s, h2, logits_s = _out_proj(tok_s, gate_s, att_s, x_s, ws, layout_s, SAMPLE_BLOCK, m, np_, h2_buffer=h2)
        tok_sorted, gate_sorted, inv, item_tile, item_expert, n_items, lo, hi = _route(
            jnp.concatenate([logits_p, logits_s], axis=0), np_, MOE_BLOCK)
        rows = _moe_ffn(h2[tok_sorted], gate_sorted, item_tile, item_expert, n_items, lo, hi,
                        exp_w_up, exp_w_down, i)
        gathered = rows[inv]
        g_next = norm_mix_g[i + 1] if i + 1 < depth else norm_final_g
        x_p, h_p = _combine(x1_p, gathered, 0, g_next, LIGHT_BLOCK, i + 1 == depth)
        x_s, h_s = _combine(x1_s, gathered, 2 * np_, g_next, SAMPLE_BLOCK, i + 1 == depth)

    y_prompt = h_p.reshape(bp, tp, D_MODEL)
    y_sample = h_s.reshape(bs, ts, D_MODEL)
    return (y_prompt, y_sample, jnp.stack(p_rw_S), jnp.stack(p_rw_shift), jnp.stack(p_gla_S),
            prompt_mem_k, prompt_mem_v, jnp.stack(s_rw_S), jnp.stack(s_rw_shift), jnp.stack(s_gla_S))
```

```python
import functools

import numpy as np
import jax
import jax.numpy as jnp
from jax import lax
from jax.experimental import pallas as pl
from jax.experimental.pallas import tpu as pltpu

F32 = jnp.float32
BF16 = jnp.bfloat16

D_MODEL = 1024
TOK_WIDTH = 768
MEM_WIDTH = 256
MEM_HEADS = 4
MEM_HEAD_DIM = 64
N_MEM = 256
RWKV_HEADS = 12
RWKV_N = 64
RWKV_PAIRS = RWKV_HEADS // 2
RWKV_GN_EPS = 64e-5
GLA_HEADS = 4
GLA_KW = 384
GLA_DK = 96
GLA_DV = 192
GLA_TAU = 16.0
GLA_CHUNK = 64
GLA_TILE = 8
N_GROUPS = 4
EXPERTS_PER_GROUP = 8
N_EXPERTS = 32
EXPERT_FF = 512
NORM_EPS = 1e-6
ROUTER_LANES = 128
LANES = 128
SUBLANES = 8

PROJ_BLOCK = 512
LIGHT_BLOCK = 512
SAMPLE_BLOCK = 128
MOE_BLOCK = 256
SCAN_TIME_BLOCK = 32
VMEM_LIMIT = 56 * 1024 * 1024


def _cparams(*sem):
    return pltpu.CompilerParams(dimension_semantics=sem, vmem_limit_bytes=VMEM_LIMIT)


def _dot(a, b):
    return jnp.dot(a, b, preferred_element_type=F32)


def _dot_nt(a, b):
    return lax.dot_general(a, b, (((1,), (1,)), ((), ())), preferred_element_type=F32)


def _dot_tn(a, b):
    return lax.dot_general(a, b, (((0,), (0,)), ((), ())), preferred_element_type=F32)


def _rms(x, g):
    return x * lax.rsqrt(jnp.mean(x * x, axis=-1, keepdims=True) + NORM_EPS) * g


def _sigmoid(x):
    return 1.0 / (1.0 + jnp.exp(-x))


def _row_spec(tm, n, offset=0):
    return pl.BlockSpec((tm, n), lambda i: (i + offset, 0))


def _head_row_spec(tm, n):
    return pl.BlockSpec((GLA_HEADS, tm, n), lambda i: (0, i, 0))


def _full_spec(shape):
    nd = len(shape)
    return pl.BlockSpec(shape, lambda *_: (0,) * nd)


def _norm_kernel(x_ref, g_ref, o_ref):
    o_ref[...] = _rms(x_ref[...], g_ref[...])


def _norm(x, g, tm):
    m, d = x.shape
    return pl.pallas_call(
        _norm_kernel, grid=(m // tm,),
        in_specs=[_row_spec(tm, d), _full_spec((1, d))],
        out_specs=_row_spec(tm, d),
        out_shape=jax.ShapeDtypeStruct((m, d), F32),
        compiler_params=_cparams("parallel"), name="rms_norm")(x, g.reshape(1, d))


def _norm_matmul_kernel(x_ref, g_ref, w_ref, o_ref):
    o_ref[...] = _dot(_rms(x_ref[...], g_ref[...]).astype(BF16), w_ref[...])


def _norm_matmul(x, g, w, tm):
    m, d = x.shape
    n = w.shape[1]
    return pl.pallas_call(
        _norm_matmul_kernel, grid=(m // tm,),
        in_specs=[_row_spec(tm, d), _full_spec((1, d)), _full_spec((d, n))],
        out_specs=_row_spec(tm, n),
        out_shape=jax.ShapeDtypeStruct((m, n), F32),
        compiler_params=_cparams("parallel"), name="norm_matmul")(x, g.reshape(1, d), w.astype(BF16))


def _rwkv_proj_kernel(h_ref, hp_ref, *refs, blocks_per_seq, norm_input):
    if norm_input:
        ng_ref, refs = refs[0], refs[1:]
        normed = lambda t_: _rms(t_, ng_ref[...])
    else:
        normed = lambda t_: t_
    (mu_ref, wr_ref, wk_ref, wv_ref, wq_ref, w1_ref, w2_ref, w0_ref, a1_ref, a2_ref, a0_ref, g1_ref, g2_ref,
     r_out, k_out, v_out, d_out, a_out, g_out, q_out) = refs

    def put(out, val):
        if blocks_per_seq:
            for p in range(RWKV_PAIRS):
                out[p] = val[:, p * LANES:(p + 1) * LANES]
        else:
            out[...] = val

    h = normed(h_ref[...])
    if blocks_per_seq:
        seq_start = (pl.program_id(0) % blocks_per_seq) == 0
        before = jnp.where(seq_start, 0.0, normed(hp_ref[SUBLANES - 1:SUBLANES, :]))
        row = lax.broadcasted_iota(jnp.int32, (h.shape[0], 1), 0)
        hp = jnp.where(row == 0, before, pltpu.roll(h, 1, axis=0))
    else:
        hp = hp_ref[...]
    xx = hp - h

    def mix(j):
        return (h + xx * mu_ref[j:j + 1, :]).astype(BF16)

    put(r_out, _dot(mix(0), wr_ref[...]))
    wl = w0_ref[...] + _dot(jnp.tanh(_dot(mix(1), w1_ref[...])).astype(BF16), w2_ref[...])
    put(k_out, _dot(mix(2), wk_ref[...]))
    put(v_out, _dot(mix(3), wv_ref[...]))
    al = a0_ref[...] + _dot(_dot(mix(4), a1_ref[...]).astype(BF16), a2_ref[...])
    g_out[...] = _dot(_sigmoid(_dot(mix(5), g1_ref[...])).astype(BF16), g2_ref[...])
    q_out[...] = _dot(h.astype(BF16), wq_ref[...])
    z = -wl
    softplus = jnp.maximum(z, 0.0) + jnp.log(1.0 + jnp.exp(-jnp.abs(z)))
    put(d_out, jnp.exp(-jnp.exp(-softplus - 0.5)))
    put(a_out, _sigmoid(al))


def _rwkv_proj(h, h_prev, ws, tm, seq_len, norm_g=None):
    m = h.shape[0]
    if norm_g is not None:
        assert h_prev is None
        ws = [norm_g.reshape(1, D_MODEL)] + list(ws)
    tw = TOK_WIDTH
    if h_prev is None:
        assert seq_len % tm == 0 and tm % SUBLANES == 0
        per8 = tm // SUBLANES
        hp_spec = pl.BlockSpec((SUBLANES, D_MODEL), lambda i: (jnp.maximum(i * per8 - 1, 0), 0))
        h_prev, blocks_per_seq = h, seq_len // tm
    else:
        hp_spec, blocks_per_seq = _row_spec(tm, D_MODEL), 0
    if blocks_per_seq:
        scan_spec = pl.BlockSpec((RWKV_PAIRS, tm, LANES), lambda i: (0, i, 0))
        scan_shape = jax.ShapeDtypeStruct((RWKV_PAIRS, m, LANES), F32)
    else:
        scan_spec, scan_shape = _row_spec(tm, tw), jax.ShapeDtypeStruct((m, tw), F32)
    return pl.pallas_call(
        functools.partial(_rwkv_proj_kernel, blocks_per_seq=blocks_per_seq, norm_input=norm_g is not None),
        grid=(m // tm,),
        in_specs=[_row_spec(tm, D_MODEL), hp_spec] + [_full_spec(w.shape) for w in ws],
        out_specs=[scan_spec] * 5 + [_row_spec(tm, tw), _row_spec(tm, MEM_WIDTH)],
        out_shape=[scan_shape] * 5 + [jax.ShapeDtypeStruct((m, tw), F32), jax.ShapeDtypeStruct((m, MEM_WIDTH), F32)],
        compiler_params=_cparams("parallel"), name="rwkv_proj")(h, h_prev, *ws)


def _pad_lanes(x):
    short = LANES - x.shape[-1]
    if short == 0:
        return x
    return jnp.concatenate([x, jnp.zeros(x.shape[:-1] + (short,), x.dtype)], axis=-1)


def _rwkv_scan_kernel(r_ref, k_ref, v_ref, d_ref, a_ref, kkp_ref, kap_ref, rkp_ref, lng_ref, lnb_ref, s0_ref,
                      y_ref, sfin_ref, s_scr, v_scr, yrow_scr, *, tc):
    n = RWKV_N
    nl = r_ref.shape[-1]

    @pl.when(pl.program_id(1) == 0)
    def _():
        s_scr[...] = _pad_lanes(s0_ref[...])

    kkp = _pad_lanes(kkp_ref[...])
    kap = _pad_lanes(kap_ref[...])
    rkp = _pad_lanes(rkp_ref[...])
    lng = _pad_lanes(lng_ref[...])
    lnb = _pad_lanes(lnb_ref[...])

    def step(t, carry):
        r_t = _pad_lanes(r_ref[t])
        k_t = _pad_lanes(k_ref[t])
        v_t = _pad_lanes(v_ref[t])
        d_t = _pad_lanes(d_ref[t])
        a_t = _pad_lanes(a_ref[t])
        v_scr[...] = v_t
        kkr = k_t * kkp
        nrm = jnp.maximum(jnp.sqrt(jnp.sum(kkr * kkr, axis=0, keepdims=True)), 1e-12)
        kk = kkr * (1.0 / nrm)
        k2 = k_t * (1.0 + (a_t - 1.0) * kap)
        nkk = -kk
        b_t = kk * a_t

        def ibody(i, c):
            s_i = s_scr[i]
            sa = jnp.sum(s_i * nkk, axis=0, keepdims=True)
            v_i = v_scr[pl.ds(i, 1), :]
            s_n = s_i * d_t + sa * b_t + v_i * k2
            s_scr[i] = s_n
            yrow_scr[pl.ds(i, 1), :] = jnp.sum(s_n * r_t, axis=0, keepdims=True)
            return c

        lax.fori_loop(0, n, ibody, 0, unroll=8)
        y = yrow_scr[...]
        yc = y - jnp.mean(y, axis=0, keepdims=True)
        var = jnp.mean(yc * yc, axis=0, keepdims=True)
        gn = yc * lax.rsqrt(var + RWKV_GN_EPS) * lng + lnb
        bonus = jnp.sum(r_t * k2 * rkp, axis=0, keepdims=True) * v_t
        y_ref[t] = (gn + bonus)[:, :nl]
        return carry

    lax.fori_loop(0, tc, step, 0)

    @pl.when(pl.program_id(1) == pl.num_programs(1) - 1)
    def _():
        sfin_ref[...] = s_scr[:, :, :nl]


def _rwkv_scan(r, k, v, d, a, params, s0, lane_block, tc):
    t, n, l = r.shape
    seq = pl.BlockSpec((tc, n, lane_block), lambda li, ti: (ti, 0, li))
    par = pl.BlockSpec((n, lane_block), lambda li, ti: (0, li))
    st = pl.BlockSpec((n, n, lane_block), lambda li, ti: (0, 0, li))
    return pl.pallas_call(
        functools.partial(_rwkv_scan_kernel, tc=tc), grid=(l // lane_block, t // tc),
        in_specs=[seq] * 5 + [par] * 5 + [st],
        out_specs=[seq, st],
        out_shape=[jax.ShapeDtypeStruct((t, n, l), F32), jax.ShapeDtypeStruct((n, n, l), F32)],
        scratch_shapes=[pltpu.VMEM((n, n, LANES), F32), pltpu.VMEM((n, LANES), F32), pltpu.VMEM((n, LANES), F32)],
        compiler_params=_cparams("parallel", "arbitrary"), name="rwkv_scan")(r, k, v, d, a, *params, s0)


PREP_TILES = 6


def _rwkv_prep_kernel(r_ref, k_ref, v_ref, d_ref, a_ref, kkp_ref, kap_ref, o_ref, *, tc, nb):
    n = RWKV_N
    rows_per_pair = nb * tc
    ins = [ref.reshape(RWKV_PAIRS * rows_per_pair, LANES) for ref in (r_ref, k_ref, v_ref, d_ref, a_ref)]
    zero_rows = jnp.zeros((LANES - RWKV_HEADS * nb, LANES), F32)
    first_half = lax.broadcasted_iota(jnp.int32, (nb, LANES), 1) < n
    kkp = kkp_ref[...]
    kap = kap_ref[...]

    def load_transposed_pair(x2, t):
        pieces = []
        for p in range(RWKV_PAIRS):
            now = x2[pl.ds(p * rows_per_pair + t, nb, stride=tc), :]
            nxt = x2[pl.ds(p * rows_per_pair + t + 1, nb, stride=tc), :]
            pieces += [jnp.where(first_half, now, pltpu.roll(nxt, n, axis=1)),
                       jnp.where(first_half, pltpu.roll(now, n, axis=1), nxt)]
        both = jnp.concatenate(pieces + [zero_rows], axis=0).T
        return both[:n, :], both[n:, :]

    def emit(t, r_t, k_t, v_t, d_t, a_t):
        kkr = k_t * kkp
        nrm = jnp.maximum(jnp.sqrt(jnp.sum(kkr * kkr, axis=0, keepdims=True)), 1e-12)
        kk = kkr * (1.0 / nrm)
        k2 = k_t * (1.0 + (a_t - 1.0) * kap)
        o_ref[t, 0] = -kk
        o_ref[t, 1] = d_t
        o_ref[t, 2] = kk * a_t
        o_ref[t, 3] = k2
        o_ref[t, 4] = r_t
        o_ref[t, 5] = v_t

    def two_steps(u, carry):
        t = 2 * u
        tiles = [load_transposed_pair(x2, t) for x2 in ins]
        emit(t, *[tile[0] for tile in tiles])
        emit(t + 1, *[tile[1] for tile in tiles])
        return carry

    lax.fori_loop(0, tc // 2, two_steps, 0, unroll=2)


def _rwkv_state_scan_kernel(x_ref, rkp_ref, lng_ref, lnb_ref, y_ref, sfin_ref, s_scr, sa_scr, yrow_scr, *, tc):
    n = RWKV_N
    groups = n // SUBLANES

    @pl.when(pl.program_id(0) == 0)
    def _():
        s_scr[...] = jnp.zeros(s_scr.shape, F32)

    rkp = rkp_ref[...]
    lng = lng_ref[...]
    lnb = lnb_ref[...]
    sub = lax.broadcasted_iota(jnp.int32, (SUBLANES, LANES), 0)
    low4 = sub < 4
    low2 = (sub & 3) < 2
    low1 = (sub & 1) == 0

    def fold(x, y, dist, low):
        if dist == 4:
            return jnp.where(low, x, y) + pltpu.roll(jnp.where(low, y, x), 4, axis=0)
        return (jnp.where(low, x, pltpu.roll(y, dist, axis=0))
                + jnp.where(low, pltpu.roll(x, SUBLANES - dist, axis=0), y))

    def sublane_sums(ps):
        z = [fold(ps[0], ps[4], 4, low4), fold(ps[2], ps[6], 4, low4),
             fold(ps[1], ps[5], 4, low4), fold(ps[3], ps[7], 4, low4)]
        return fold(fold(z[0], z[1], 2, low2), fold(z[2], z[3], 2, low2), 1, low1)

    def tile_sum(x):
        acc = x[0:SUBLANES]
        for u in range(1, groups):
            acc = acc + x[u * SUBLANES:(u + 1) * SUBLANES]
        return acc

    def step(t, carry):
        nkk = x_ref[t, 0]
        d_t = x_ref[t, 1]
        b_t = x_ref[t, 2]
        k2 = x_ref[t, 3]
        r_t = x_ref[t, 4]
        for g in range(groups):
            sa_scr[g * SUBLANES:(g + 1) * SUBLANES, :] = sublane_sums(
                [tile_sum(s_scr[g * SUBLANES + u] * nkk) for u in range(SUBLANES)])
        for g in range(groups):
            ps = []
            for u in range(SUBLANES):
                i = g * SUBLANES + u
                s_n = s_scr[i] * d_t + sa_scr[pl.ds(i, 1), :] * b_t + x_ref[t, 5, pl.ds(i, 1), :] * k2
                s_scr[i] = s_n
                ps.append(tile_sum(s_n * r_t))
            yrow_scr[g * SUBLANES:(g + 1) * SUBLANES, :] = sublane_sums(ps)
        y = yrow_scr[...]
        yc = y - jnp.mean(y, axis=0, keepdims=True)
        var = jnp.mean(yc * yc, axis=0, keepdims=True)
        bonus = jnp.sum(r_t * k2 * rkp, axis=0, keepdims=True) * x_ref[t, 5]
        y_ref[t] = yc * lax.rsqrt(var + RWKV_GN_EPS) * lng + lnb + bonus
        return carry

    lax.fori_loop(0, tc, step, 0)

    @pl.when(pl.program_id(0) == pl.num_programs(0) - 1)
    def _():
        sfin_ref[...] = s_scr[...]


def _rwkv_unprep_kernel(y_ref, o_ref, *, tc, nb):
    n = RWKV_N
    rows_per_pair = nb * tc
    o2 = o_ref.reshape(RWKV_PAIRS * rows_per_pair, LANES)
    first_half = lax.broadcasted_iota(jnp.int32, (nb, LANES), 1) < n

    def two_steps(u, carry):
        t = 2 * u
        w = jnp.concatenate([y_ref[t], y_ref[t + 1]], axis=0).T
        for p in range(RWKV_PAIRS):
            even = w[(2 * p) * nb:(2 * p + 1) * nb, :]
            odd = w[(2 * p + 1) * nb:(2 * p + 2) * nb, :]
            o2[pl.ds(p * rows_per_pair + t, nb, stride=tc), :] = jnp.where(
                first_half, even, pltpu.roll(odd, n, axis=1))
            o2[pl.ds(p * rows_per_pair + t + 1, nb, stride=tc), :] = jnp.where(
                first_half, pltpu.roll(even, n, axis=1), odd)
        return carry

    lax.fori_loop(0, tc // 2, two_steps, 0, unroll=2)


def _rwkv_prompt_mixer(r, k, v, d, a, kkp, kap, rkp, lng, lnb, nb, t, tc):
    n = RWKV_N
    assert tc % SUBLANES == 0 and tc % 4 == 0 and t % tc == 0 and RWKV_HEADS * nb <= LANES
    grid = (t // tc,)
    tok = pl.BlockSpec((RWKV_PAIRS, nb, tc, LANES), lambda ti: (0, 0, ti, 0))
    par = pl.BlockSpec((n, LANES), lambda ti: (0, 0))
    tiles = pl.BlockSpec((tc, PREP_TILES, n, LANES), lambda ti: (ti, 0, 0, 0))
    ytile = pl.BlockSpec((tc, n, LANES), lambda ti: (ti, 0, 0))
    prepared = pl.pallas_call(
        functools.partial(_rwkv_prep_kernel, tc=tc, nb=nb), grid=grid,
        in_specs=[tok] * 5 + [par] * 2, out_specs=tiles,
        out_shape=jax.ShapeDtypeStruct((t, PREP_TILES, n, LANES), F32),
        compiler_params=_cparams("parallel"), name="rwkv_prep")(
            *[x.reshape(RWKV_PAIRS, nb, t, LANES) for x in (r, k, v, d, a)], kkp, kap)
    y, s = pl.pallas_call(
        functools.partial(_rwkv_state_scan_kernel, tc=tc), grid=grid,
        in_specs=[tiles, par, par, par],
        out_specs=[ytile, pl.BlockSpec((n, n, LANES), lambda ti: (0, 0, 0))],
        out_shape=[jax.ShapeDtypeStruct((t, n, LANES), F32), jax.ShapeDtypeStruct((n, n, LANES), F32)],
        scratch_shapes=[pltpu.VMEM((n, n, LANES), F32), pltpu.VMEM((n, LANES), F32), pltpu.VMEM((n, LANES), F32)],
        compiler_params=_cparams("arbitrary"), name="rwkv_state_scan")(prepared, rkp, lng, lnb)
    tok_out = pl.pallas_call(
        functools.partial(_rwkv_unprep_kernel, tc=tc, nb=nb), grid=grid,
        in_specs=[ytile], out_specs=tok,
        out_shape=jax.ShapeDtypeStruct((RWKV_PAIRS, nb, t, LANES), F32),
        compiler_params=_cparams("parallel"), name="rwkv_unprep")(y)
    return tok_out.reshape(RWKV_PAIRS, nb * t, LANES), s


def _gla_proj_kernel(h_ref, wq_ref, wk_ref, wv_ref, wr_ref, wm_ref, a1_ref, a2_ref, ab_ref,
                     q_out, k_out, v_out, r_out, la_out, qm_out):
    hb = h_ref[...].astype(BF16)
    low = _dot(hb, a1_ref[...]).astype(BF16)
    for hd in range(GLA_HEADS):
        q_out[hd] = _dot(hb, wq_ref[hd])
        k_out[hd] = _dot(hb, wk_ref[hd])
        v_out[hd] = _dot(hb, wv_ref[hd])
        r_out[hd] = _dot(hb, wr_ref[hd])
        x = _dot(low, a2_ref[hd]) + ab_ref[hd]
        log_sigmoid = jnp.minimum(x, 0.0) - jnp.log(1.0 + jnp.exp(-jnp.abs(x)))
        la_out[hd] = log_sigmoid / GLA_TAU
    qm_out[...] = _dot(hb, wm_ref[...])


def _gla_weights(w_in, a1, a2, ab):
    kw, tw, nh = GLA_KW, TOK_WIDTH, GLA_HEADS

    def heads(w, d):
        return w.reshape(w.shape[0], nh, d).transpose(1, 0, 2)

    return [heads(w_in[:, :kw], GLA_DK).astype(BF16), heads(w_in[:, kw:2 * kw], GLA_DK).astype(BF16),
            heads(w_in[:, 2 * kw:2 * kw + tw], GLA_DV).astype(BF16),
            heads(w_in[:, 2 * kw + tw:3 * tw], GLA_DV).astype(BF16), w_in[:, 3 * tw:].astype(BF16),
            a1.astype(BF16), heads(a2, GLA_DK).astype(BF16), ab.reshape(nh, 1, GLA_DK)]


def _gla_proj(h, ws, tm):
    m = h.shape[0]
    nh = GLA_HEADS
    widths = [GLA_DK, GLA_DK, GLA_DV, GLA_DV, GLA_DK]
    return pl.pallas_call(
        _gla_proj_kernel, grid=(m // tm,),
        in_specs=[_row_spec(tm, D_MODEL)] + [_full_spec(w.shape) for w in ws],
        out_specs=[_head_row_spec(tm, w) for w in widths] + [_row_spec(tm, MEM_WIDTH)],
        out_shape=[jax.ShapeDtypeStruct((nh, m, w), F32) for w in widths] + [jax.ShapeDtypeStruct((m, MEM_WIDTH), F32)],
        compiler_params=_cparams("parallel"), name="gla_proj")(h, *ws)


def _gla_out_norm(o, g):
    return o * lax.rsqrt(jnp.mean(o * o, axis=-1, keepdims=True) + NORM_EPS) * g


def _gla_chunk_kernel(q_ref, k_ref, v_ref, la_ref, ng_ref, o_ref, sfin_ref, st_scr):
    c, dk = GLA_CHUNK, GLA_DK

    @pl.when(pl.program_id(1) == 0)
    def _():
        st_scr[...] = jnp.zeros(st_scr.shape, F32)

    row = lax.broadcasted_iota(jnp.int32, (c, c), 0)
    col = lax.broadcasted_iota(jnp.int32, (c, c), 1)
    tril = (row >= col).astype(F32)
    rr = lax.broadcasted_iota(jnp.int32, (c, dk), 0)
    ones_sum = jnp.ones((dk, LANES), BF16)

    heads = range(GLA_HEADS)
    tril_b = tril.astype(BF16)

    def cumsum_rows(la):
        hi = la.astype(BF16)
        r1 = la - hi.astype(F32)
        mid = r1.astype(BF16)
        lo = (r1 - mid.astype(F32)).astype(BF16)
        return _dot(tril_b, hi) + _dot(tril_b, mid) + _dot(tril_b, lo)

    def tile_roll(x, dlt):
        return pltpu.roll(x.reshape(c // GLA_TILE, GLA_TILE, dk), dlt, axis=1).reshape(c, dk)

    k = [k_ref[hd] for hd in heads]
    vb = [v_ref[hd].astype(BF16) for hd in heads]
    q = [q_ref[hd] * (dk ** -0.5) for hd in heads]
    b = [cumsum_rows(la_ref[hd]) for hd in heads]
    st = [st_scr[hd] for hd in heads]
    inter = [_dot_nt((q[hd] * jnp.exp(b[hd])).astype(BF16), st[hd].astype(BF16)) for hd in heads]

    att = [jnp.zeros((c, c), F32) for hd in heads]
    blk = c // 2
    while blk >= GLA_TILE:
        two = 2 * blk
        upper = (rr & (two - 1)) >= blk
        same_block = (row ^ col) < two
        parts = []
        for hd in heads:
            b_ref_rows = jnp.concatenate(
                [jnp.broadcast_to(b[hd][s0 + blk - 1:s0 + blk, :], (two, dk)) for s0 in range(0, c, two)], axis=0)
            q_l = jnp.where(upper, q[hd] * jnp.exp(jnp.minimum(b[hd] - b_ref_rows, 0.0)), 0.0).astype(BF16)
            k_l = jnp.where(upper, 0.0, k[hd] * jnp.exp(jnp.minimum(b_ref_rows - b[hd], 0.0))).astype(BF16)
            parts.append(_dot_nt(q_l, k_l))
        att = [att[hd] + jnp.where(same_block, parts[hd], 0.0) for hd in heads]
        blk //= 2

    sums = []
    for hd in heads:
        prods = [(q[hd] * k[hd]).astype(BF16)]
        for dlt in range(1, GLA_TILE):
            p = q[hd] * tile_roll(k[hd], dlt) * jnp.exp(jnp.minimum(b[hd] - tile_roll(b[hd], dlt), 0.0))
            prods.append(jnp.where((rr & (GLA_TILE - 1)) >= dlt, p, 0.0).astype(BF16))
        sums.append(_dot(jnp.concatenate(prods, axis=0), ones_sum))
    for dlt in range(GLA_TILE):
        on_diag = col == row - dlt
        att = [att[hd] + jnp.where(on_diag, sums[hd][dlt * c:(dlt + 1) * c, :c], 0.0) for hd in heads]

    o = [inter[hd] + _dot(att[hd].astype(BF16), vb[hd]) for hd in heads]
    for hd in heads:
        o_ref[hd] = _gla_out_norm(o[hd], ng_ref[...])
    for hd in heads:
        b_end = b[hd][c - 1:c, :]
        kd = (k[hd] * jnp.exp(b_end - b[hd])).astype(BF16)
        st_scr[hd] = jnp.exp(b_end) * st[hd] + _dot_tn(vb[hd], kd)

    @pl.when(pl.program_id(1) == pl.num_programs(1) - 1)
    def _():
        sfin_ref[0] = st_scr[...]


def _gla_chunk_scan(q, k, v, la, norm_g, batch, t):
    nh, dk, dv, c = GLA_HEADS, GLA_DK, GLA_DV, GLA_CHUNK
    nc = t // c
    kspec = pl.BlockSpec((nh, c, dk), lambda i, j: (0, i * nc + j, 0))
    vspec = pl.BlockSpec((nh, c, dv), lambda i, j: (0, i * nc + j, 0))
    sspec = pl.BlockSpec((1, nh, dv, dk), lambda i, j: (i, 0, 0, 0))
    return pl.pallas_call(
        _gla_chunk_kernel, grid=(batch, nc),
        in_specs=[kspec, kspec, vspec, kspec, pl.BlockSpec((1, dv), lambda i, j: (0, 0))],
        out_specs=[vspec, sspec],
        out_shape=[jax.ShapeDtypeStruct((nh, batch * t, dv), F32), jax.ShapeDtypeStruct((batch, nh, dv, dk), F32)],
        scratch_shapes=[pltpu.VMEM((nh, dv, dk), F32)],
        compiler_params=_cparams("parallel", "arbitrary"), name="gla_chunk")(q, k, v, la, norm_g.reshape(1, dv))


GLA_STEP_GROUP = 8


def _gla_step_kernel(q_ref, k_ref, v_ref, la_ref, ng_ref, s0_ref, o_ref, s_ref):
    g_n, dk = GLA_STEP_GROUP, GLA_DK

    def columns(ref):
        x = jnp.concatenate([ref[...], jnp.zeros((g_n, LANES - dk), F32)], axis=1)
        return jnp.concatenate([x, jnp.zeros((LANES - g_n, LANES), F32)], axis=0).T

    q_cols, k_cols, la_cols = columns(q_ref), columns(k_ref), columns(la_ref)
    v = v_ref[...]
    rows = []
    for g in range(g_n):
        kv = k_cols[:dk, g:g + 1].astype(BF16).astype(F32) * v[g:g + 1, :].astype(BF16).astype(F32)
        s_new = jnp.exp(la_cols[:dk, g:g + 1]) * s0_ref[g] + kv
        s_ref[g] = s_new
        o = jnp.sum((q_cols[:dk, g:g + 1] * (dk ** -0.5)) * s_new, axis=0, keepdims=True)
        rows.append(_gla_out_norm(o, ng_ref[...]))
    o_ref[...] = jnp.concatenate(rows, axis=0)


def _gla_step(q, k, v, la, norm_g, s0):
    bh, dk = q.shape
    dv = v.shape[1]
    g = GLA_STEP_GROUP
    kspec = _row_spec(g, dk)
    vspec = _row_spec(g, dv)
    sspec = pl.BlockSpec((g, dk, dv), lambda i: (i, 0, 0))
    return pl.pallas_call(
        _gla_step_kernel, grid=(bh // g,),
        in_specs=[kspec, kspec, vspec, kspec, _full_spec((1, dv)), sspec],
        out_specs=[vspec, sspec],
        out_shape=[jax.ShapeDtypeStruct((bh, dv), F32), jax.ShapeDtypeStruct((bh, dk, dv), F32)],
        compiler_params=_cparams("parallel"), name="gla_step")(q, k, v, la, norm_g.reshape(1, dv), s0)


def _mem_attn_kernel(q_ref, k_ref, v_ref, o_ref):
    q = q_ref[0]
    k = k_ref[0].astype(BF16)
    v_ones = jnp.concatenate([v_ref[0].astype(BF16), jnp.ones((N_MEM, LANES), BF16)], axis=1)
    head_of_lane = lax.broadcasted_iota(jnp.int32, (1, MEM_WIDTH), 1) // MEM_HEAD_DIM
    heads = range(MEM_HEADS)
    mine = [head_of_lane == h for h in heads]
    qh = [jnp.where(mine[h], q, 0.0).astype(BF16) for h in heads]
    s = [_dot_nt(qh[h], k) * (MEM_HEAD_DIM ** -0.5) for h in heads]
    e = [jnp.exp(s[h] - jnp.max(s[h], axis=-1, keepdims=True)).astype(BF16) for h in heads]
    ev = [_dot(e[h], v_ones) for h in heads]
    out = jnp.zeros(q.shape, F32)
    for h in heads:
        inv = 1.0 / ev[h][:, MEM_WIDTH:]
        out = out + jnp.where(mine[h], ev[h][:, :MEM_WIDTH] * jnp.concatenate([inv, inv], axis=1), 0.0)
    o_ref[0] = out


def _mem_attn(q, mem_k, mem_v, layer, tq):
    b, t, w = q.shape
    qspec = pl.BlockSpec((1, tq, w), lambda i, j: (i, j, 0))
    mspec = pl.BlockSpec((None, 1, N_MEM, w), lambda i, j: (layer, i, 0, 0))
    return pl.pallas_call(
        _mem_attn_kernel, grid=(b, t // tq),
        in_specs=[qspec, mspec, mspec], out_specs=qspec,
        out_shape=jax.ShapeDtypeStruct((b, t, w), F32),
        compiler_params=_cparams("parallel", "parallel"), name="mem_attn")(q, mem_k, mem_v)


def _out_proj_kernel(*refs, layout, aliased):
    tok_ref, gate_ref, att_ref, x_ref, wo_ref, g_ref, wr_hi_ref, wr_lo_ref, br_ref = refs[:9]
    x1_out, h2_out, logit_out = refs[9 + aliased:]
    x1 = x_ref[...] + _dot(att_ref[...].astype(BF16), wo_ref[TOK_WIDTH:, :])
    if layout == "gla_heads":
        for hd in range(GLA_HEADS):
            gate = gate_ref[hd]
            mixed = (tok_ref[hd] * (gate * _sigmoid(gate))).astype(BF16)
            x1 = x1 + _dot(mixed, wo_ref[hd * GLA_DV:(hd + 1) * GLA_DV, :])
    else:
        if layout == "rwkv_pairs":
            tok = jnp.concatenate([tok_ref[p] for p in range(RWKV_PAIRS)], axis=1)
        else:
            tok = tok_ref[...]
        x1 = x1 + _dot((tok * gate_ref[...]).astype(BF16), wo_ref[:TOK_WIDTH, :])
    x1_out[...] = x1
    h2 = _rms(x1, g_ref[...])
    h2_out[...] = h2
    h_hi = h2.astype(BF16)
    h_lo = (h2 - h_hi.astype(F32)).astype(BF16)
    logit_out[...] = (_dot(h_hi, wr_hi_ref[...]) + _dot(h_lo, wr_hi_ref[...]) + _dot(h_hi, wr_lo_ref[...])
                      + br_ref[...])


def _out_proj(tok, gate, att, x, ws, layout, tm, h2_rows, h2_row_offset, h2_buffer=None):
    m = x.shape[0]
    aliased = h2_buffer is not None
    rows_spec = _row_spec(tm, TOK_WIDTH)
    tok_spec, gate_spec = {
        "rows": (rows_spec, rows_spec),
        "rwkv_pairs": (pl.BlockSpec((RWKV_PAIRS, tm, LANES), lambda i: (0, i, 0)), rows_spec),
        "gla_heads": (_head_row_spec(tm, GLA_DV), _head_row_spec(tm, GLA_DV))}[layout]
    in_specs = ([tok_spec, gate_spec, _row_spec(tm, MEM_WIDTH), _row_spec(tm, D_MODEL)]
                + [_full_spec(w.shape) for w in ws])
    args = [tok, gate, att, x, *ws]
    if aliased:
        in_specs.append(pl.BlockSpec(memory_space=pl.ANY))
        args.append(h2_buffer)
    return pl.pallas_call(
        functools.partial(_out_proj_kernel, layout=layout, aliased=int(aliased)), grid=(m // tm,),
        in_specs=in_specs,
        out_specs=[_row_spec(tm, D_MODEL), _row_spec(tm, D_MODEL, h2_row_offset // tm), _row_spec(tm, ROUTER_LANES)],
        out_shape=[jax.ShapeDtypeStruct((m, D_MODEL), F32), jax.ShapeDtypeStruct((h2_rows, D_MODEL), F32),
                   jax.ShapeDtypeStruct((m, ROUTER_LANES), F32)],
        input_output_aliases={len(args) - 1: 1} if aliased else {},
        compiler_params=_cparams("parallel"), name="out_proj")(*args)


def _moe_kernel(tile_ref, exp_ref, nitem_ref, lo_ref, hi_ref, x_ref, gate_ref, wu_ref, wd_ref, o_ref, wu_scr, wd_scr):
    w = pl.program_id(0)
    prev = jnp.maximum(w - 1, 0)
    e = exp_ref[w]
    valid = w < nitem_ref[0]

    @pl.when(jnp.logical_and(valid, jnp.logical_or(w == 0, e != exp_ref[prev])))
    def _():
        wu_scr[...] = wu_ref[0, 0].astype(BF16)
        wd_scr[...] = wd_ref[0, 0].astype(BF16)

    @pl.when(jnp.logical_or(w == 0, tile_ref[w] != tile_ref[prev]))
    def _():
        o_ref[...] = jnp.zeros(o_ref.shape, F32)

    @pl.when(valid)
    def _():
        gu = _dot(x_ref[...].astype(BF16), wu_scr[...])
        g = gu[:, :EXPERT_FF]
        act = (g * _sigmoid(g) * gu[:, EXPERT_FF:]).astype(BF16)
        out = _dot(act, wd_scr[...]) * gate_ref[...]
        rows = tile_ref[w] * MOE_BLOCK + lax.broadcasted_iota(jnp.int32, (MOE_BLOCK, 1), 0)
        mine = jnp.logical_and(rows >= lo_ref[e], rows < hi_ref[e])
        o_ref[...] = o_ref[...] + jnp.where(mine, out, 0.0)


def _moe_ffn(xs, row_gate, item_tile, item_expert, n_items, lo, hi, w_up, w_down, layer):
    tm = MOE_BLOCK
    n_work = item_tile.shape[0]
    a = xs.shape[0]
    row_map = lambda w, tile, ex, ni, lo_, hi_: (tile[w], 0)
    exp_map = lambda w, tile, ex, ni, lo_, hi_: (layer, ex[w], 0, 0)
    grid_spec = pltpu.PrefetchScalarGridSpec(
        num_scalar_prefetch=5, grid=(n_work,),
        in_specs=[pl.BlockSpec((tm, D_MODEL), row_map),
                  pl.BlockSpec((tm, 1), row_map),
                  pl.BlockSpec((1, 1, D_MODEL, 2 * EXPERT_FF), exp_map),
                  pl.BlockSpec((1, 1, EXPERT_FF, D_MODEL), exp_map)],
        out_specs=pl.BlockSpec((tm, D_MODEL), row_map),
        scratch_shapes=[pltpu.VMEM((D_MODEL, 2 * EXPERT_FF), BF16), pltpu.VMEM((EXPERT_FF, D_MODEL), BF16)])
    return pl.pallas_call(
        _moe_kernel, grid_spec=grid_spec,
        out_shape=jax.ShapeDtypeStruct((a, D_MODEL), F32),
        compiler_params=_cparams("arbitrary"), name="moe_ffn")(
            item_tile, item_expert, n_items, lo, hi, xs, row_gate, w_up, w_down)


def _route(logits, n_prompt, tm):
    m = logits.shape[0]
    n_sample = m - n_prompt
    a = 2 * m
    gl = logits[:, :N_GROUPS]
    el = logits[:, N_GROUPS:N_GROUPS + N_EXPERTS]
    group = jnp.argmax(gl, -1).astype(jnp.int32)
    p_group = jnp.max(jax.nn.softmax(gl, -1), -1, keepdims=True)
    group_of_expert = jnp.arange(N_EXPERTS, dtype=jnp.int32) // EXPERTS_PER_GROUP
    top_val, expert = lax.top_k(jnp.where(group_of_expert[None, :] == group[:, None], el, -jnp.inf), 2)
    gate = p_group * jax.nn.softmax(top_val, -1)
    expert = expert.astype(jnp.int32)

    def by_id(t):
        return jnp.concatenate([t[:n_prompt, 0], t[:n_prompt, 1], t[n_prompt:, 0], t[n_prompt:, 1]])

    tok_of_id = jnp.asarray(np.concatenate([np.arange(n_prompt), np.arange(n_prompt),
                                            n_prompt + np.arange(n_sample), n_prompt + np.arange(n_sample)]), jnp.int32)
    flat_e = by_id(expert)
    ids = jnp.arange(a, dtype=jnp.int32)
    _, order, gate_sorted, tok_sorted = lax.sort((flat_e, ids, by_id(gate), tok_of_id), num_keys=1, is_stable=True)
    _, inv = lax.sort((order, ids), num_keys=1)
    experts = jnp.arange(N_EXPERTS, dtype=jnp.int32)
    counts = jnp.sum((flat_e[:, None] == experts[None, :]).astype(jnp.int32), axis=0)
    hi = jnp.cumsum(counts).astype(jnp.int32)
    lo = hi - counts
    n_tiles = a // tm
    first_tile = lo // tm
    tiles_of = jnp.where(counts > 0, (hi - 1) // tm - first_tile + 1, 0)
    item_end = jnp.cumsum(tiles_of).astype(jnp.int32)
    n_items = item_end[-1:]
    n_work = n_tiles + N_EXPERTS - 1
    w = jnp.minimum(jnp.arange(n_work, dtype=jnp.int32), n_items[0] - 1)
    item_expert = jnp.sum((item_end[None, :] <= w[:, None]).astype(jnp.int32), axis=1)
    onehot = (item_expert[:, None] == experts[None, :]).astype(jnp.int32)
    item_tile = jnp.sum(onehot * (first_tile - (item_end - tiles_of))[None, :], axis=1) + w
    return tok_sorted, gate_sorted.reshape(a, 1), inv, item_tile.astype(jnp.int32), item_expert, n_items, lo, hi


def _combine_kernel(x_ref, r0_ref, r1_ref, g_ref, *outs):
    x2 = x_ref[...] + (r0_ref[...] + r1_ref[...])
    outs[-1][...] = _rms(x2, g_ref[...])
    if len(outs) == 2:
        outs[0][...] = x2


def _combine(x1, gathered, first_row, g, tm, last_layer):
    m = x1.shape[0]
    spec = _row_spec(tm, D_MODEL)
    n_out = 1 if last_layer else 2
    outs = pl.pallas_call(
        _combine_kernel, grid=(m // tm,),
        in_specs=[spec, _row_spec(tm, D_MODEL, first_row // tm), _row_spec(tm, D_MODEL, (first_row + m) // tm),
                  _full_spec((1, D_MODEL))],
        out_specs=[spec] * n_out,
        out_shape=[jax.ShapeDtypeStruct((m, D_MODEL), F32)] * n_out,
        compiler_params=_cparams("parallel"), name="moe_combine")(x1, gathered, gathered, g.reshape(1, D_MODEL))
    return (None, outs[0]) if last_layer else tuple(outs)


def _to_scan_layout(t2d, b, t):
    return t2d.reshape(b, t, RWKV_HEADS, RWKV_N).transpose(1, 3, 0, 2).reshape(t, RWKV_N, b * RWKV_HEADS)


def _from_scan_layout(y, b, t):
    return y.reshape(t, RWKV_N, b, RWKV_HEADS).transpose(2, 0, 3, 1).reshape(b * t, TOK_WIDTH)


def _scan_param(p, b):
    return jnp.tile(p.reshape(RWKV_HEADS, RWKV_N).T, (1, b))


def _seq_scan_param(p, b):
    lanes = jnp.repeat(p.reshape(RWKV_HEADS, RWKV_N).T, b, axis=1)
    return jnp.pad(lanes, ((0, 0), (0, LANES - RWKV_HEADS * b)))


def _state_to_scan(s):
    b = s.shape[0]
    return s.transpose(2, 3, 0, 1).reshape(RWKV_N, RWKV_N, b * RWKV_HEADS)


def _state_from_scan(s, b):
    return s.reshape(RWKV_N, RWKV_N, b, RWKV_HEADS).transpose(2, 3, 0, 1)


def kernel(x_prompt, x_sample, mem_prompt, state_rwkv_S, state_rwkv_shift, state_gla_S, cache_mem_k, cache_mem_v, norm_mix_g, norm_ffn_g, norm_mem_g, norm_final_g, w_in, w_out, w_mem_kv, rw_mu, rw_w0, rw_w1, rw_w2, rw_a0, rw_a1, rw_a2, rw_g1, rw_g2, rw_k_k, rw_k_a, rw_r_k, rw_ln_g, rw_ln_b, gla_a1, gla_a2, gla_ab, gla_norm_g, router_wg, router_bg, router_we, router_be, exp_w_up, exp_w_down):
    bp, tp, _ = x_prompt.shape
    bs, ts, _ = x_sample.shape
    assert ts == 1 and tp % GLA_CHUNK == 0 and tp % SCAN_TIME_BLOCK == 0
    np_ = bp * tp
    ns = bs * ts
    m = np_ + ns
    assert np_ % LIGHT_BLOCK == 0 and ns % SAMPLE_BLOCK == 0 and (2 * m) % MOE_BLOCK == 0
    depth = w_in.shape[0]
    nh = GLA_HEADS
    tw = TOK_WIDTH
    bf = lambda t_: t_.astype(BF16)

    mem2d = mem_prompt.reshape(bp * N_MEM, D_MODEL)
    mem_kv = [_norm_matmul(mem2d, norm_mem_g[i], w_mem_kv[i], 512) for i in range(depth)]
    pk = jnp.stack([kv[:, :MEM_WIDTH].reshape(bp, N_MEM, MEM_WIDTH) for kv in mem_kv])
    pv = jnp.stack([kv[:, MEM_WIDTH:].reshape(bp, N_MEM, MEM_WIDTH) for kv in mem_kv])
    prompt_mem_k = pk.reshape(depth, bp, N_MEM, MEM_HEADS, MEM_HEAD_DIM)
    prompt_mem_v = pv.reshape(depth, bp, N_MEM, MEM_HEADS, MEM_HEAD_DIM)
    sk = cache_mem_k.reshape(depth, bs, N_MEM, MEM_WIDTH)
    sv = cache_mem_v.reshape(depth, bs, N_MEM, MEM_WIDTH)

    x_p = x_prompt.reshape(np_, D_MODEL)
    x_s = x_sample.reshape(ns, D_MODEL)
    h_p = None
    h_s = _norm(x_s, norm_mix_g[0], SAMPLE_BLOCK)

    w_router = jnp.zeros((depth, D_MODEL, ROUTER_LANES), F32)
    w_router = w_router.at[:, :, :N_GROUPS].set(router_wg).at[:, :, N_GROUPS:N_GROUPS + N_EXPERTS].set(router_we)
    b_router = jnp.zeros((depth, 1, ROUTER_LANES), F32)
    b_router = b_router.at[:, 0, :N_GROUPS].set(router_bg).at[:, 0, N_GROUPS:N_GROUPS + N_EXPERTS].set(router_be)

    p_rw_S, p_rw_shift, p_gla_S, s_rw_S, s_rw_shift, s_gla_S = [], [], [], [], [], []
    for i in range(depth):
        j = i // 2
        if i % 2 == 0:
            wi = w_in[i]
            ws = [rw_mu[j], bf(wi[:, :tw]), bf(wi[:, tw:2 * tw]), bf(wi[:, 2 * tw:3 * tw]), bf(wi[:, 3 * tw:]),
                  bf(rw_w1[j]), bf(rw_w2[j]), rw_w0[j].reshape(1, tw), bf(rw_a1[j]), bf(rw_a2[j]),
                  rw_a0[j].reshape(1, tw), bf(rw_g1[j]), bf(rw_g2[j])]
            if h_p is None:
                *rkvda_p, gate_p, qm_p = _rwkv_proj(x_p, None, ws, PROJ_BLOCK, tp, norm_g=norm_mix_g[i])
                shift_p = _norm(x_prompt[:, -1, :], norm_mix_g[i], bp)
            else:
                *rkvda_p, gate_p, qm_p = _rwkv_proj(h_p, None, ws, PROJ_BLOCK, tp)
                shift_p = h_p.reshape(bp, tp, D_MODEL)[:, -1]
            *rkvda_s, gate_s, qm_s = _rwkv_proj(h_s, state_rwkv_shift[j], ws, SAMPLE_BLOCK, ts)
            pvec = [rw_k_k[j], rw_k_a[j], rw_r_k[j], rw_ln_g[j], rw_ln_b[j]]
            tok_p, sp = _rwkv_prompt_mixer(*rkvda_p, *[_seq_scan_param(p, bp) for p in pvec], bp, tp, SCAN_TIME_BLOCK)
            ys, ss = _rwkv_scan(*[_to_scan_layout(t_, bs, ts) for t_ in rkvda_s], [_scan_param(p, bs) for p in pvec],
                                _state_to_scan(state_rwkv_S[j]), LANES, 1)
            tok_s = _from_scan_layout(ys, bs, ts)
            p_rw_S.append(sp[:, :, :RWKV_HEADS * bp].reshape(RWKV_N, RWKV_N, RWKV_HEADS, bp).transpose(3, 2, 0, 1))
            s_rw_S.append(_state_from_scan(ss, bs))
            p_rw_shift.append(shift_p)
            s_rw_shift.append(h_s)
            layout_p, layout_s = "rwkv_pairs", "rows"
        else:
            ws = _gla_weights(w_in[i], gla_a1[j], gla_a2[j], gla_ab[j])
            q_p, k_p, v_p, gate_p, la_p, qm_p = _gla_proj(h_p, ws, PROJ_BLOCK)
            q_s, k_s, v_s, gate_s, la_s, qm_s = _gla_proj(h_s, ws, SAMPLE_BLOCK)
            tok_p, sp_t = _gla_chunk_scan(q_p, k_p, v_p, la_p, gla_norm_g[j], bp, tp)
            rows_of = lambda t_: t_.transpose(1, 0, 2).reshape(bs * nh, t_.shape[2])
            os_, ss = _gla_step(rows_of(q_s), rows_of(k_s), rows_of(v_s), rows_of(la_s),
                                gla_norm_g[j], state_gla_S[j].reshape(bs * nh, GLA_DK, GLA_DV))
            tok_s = os_.reshape(bs, nh, GLA_DV).transpose(1, 0, 2)
            p_gla_S.append(sp_t.transpose(0, 1, 3, 2))
            s_gla_S.append(ss.reshape(bs, nh, GLA_DK, GLA_DV))
            layout_p = layout_s = "gla_heads"

        att_p = _mem_attn(qm_p.reshape(bp, tp, MEM_WIDTH), pk, pv, i, 512).reshape(np_, MEM_WIDTH)
        att_s = _mem_attn(qm_s.reshape(bs, ts, MEM_WIDTH), sk, sv, i, 1).reshape(ns, MEM_WIDTH)
        wr_hi = bf(w_router[i])
        wr_lo = bf(w_router[i] - wr_hi.astype(F32))
        ws = [bf(w_out[i]), norm_ffn_g[i].reshape(1, D_MODEL), wr_hi, wr_lo, b_router[i]]
        x1_p, h2, logits_p = _out_proj(tok_p, gate_p, att_p, x_p, ws, layout_p, PROJ_BLOCK, m, 0)
        x1_s, h2, logits_s = _out_proj(tok_s, gate_s, att_s, x_s, ws, layout_s, SAMPLE_BLOCK, m, np_, h2_buffer=h2)
        tok_sorted, gate_sorted, inv, item_tile, item_expert, n_items, lo, hi = _route(
            jnp.concatenate([logits_p, logits_s], axis=0), np_, MOE_BLOCK)
        rows = _moe_ffn(h2[tok_sorted], gate_sorted, item_tile, item_expert, n_items, lo, hi,
                        exp_w_up, exp_w_down, i)
        gathered = rows[inv]
        g_next = norm_mix_g[i + 1] if i + 1 < depth else norm_final_g
        x_p, h_p = _combine(x1_p, gathered, 0, g_next, LIGHT_BLOCK, i + 1 == depth)
        x_s, h_s = _combine(x1_s, gathered, 2 * np_, g_next, SAMPLE_BLOCK, i + 1 == depth)

    y_prompt = h_p.reshape(bp, tp, D_MODEL)
    y_sample = h_s.reshape(bs, ts, D_MODEL)
    return (y_prompt, y_sample, jnp.stack(p_rw_S), jnp.stack(p_rw_shift), jnp.stack(p_gla_S),
            prompt_mem_k, prompt_mem_v, jnp.stack(s_rw_S), jnp.stack(s_rw_shift), jnp.stack(s_gla_S))
```

```python
import functools

import numpy as np
import jax
import jax.numpy as jnp
from jax import lax
from jax.experimental import pallas as pl
from jax.experimental.pallas import tpu as pltpu

F32 = jnp.float32
BF16 = jnp.bfloat16

D_MODEL = 1024
TOK_WIDTH = 768
MEM_WIDTH = 256
MEM_HEADS = 4
MEM_HEAD_DIM = 64
N_MEM = 256
RWKV_HEADS = 12
RWKV_N = 64
RWKV_PAIRS = RWKV_HEADS // 2
RWKV_GN_EPS = 64e-5
GLA_HEADS = 4
GLA_KW = 384
GLA_DK = 96
GLA_DV = 192
GLA_TAU = 16.0
GLA_CHUNK = 64
GLA_TILE = 8
N_GROUPS = 4
EXPERTS_PER_GROUP = 8
N_EXPERTS = 32
EXPERT_FF = 512
NORM_EPS = 1e-6
ROUTER_LANES = 128
LANES = 128
SUBLANES = 8

PROJ_BLOCK = 512
LIGHT_BLOCK = 512
SAMPLE_BLOCK = 128
MOE_BLOCK = 256
SCAN_TIME_BLOCK = 32
VMEM_LIMIT = 56 * 1024 * 1024


def _cparams(*sem):
    return pltpu.CompilerParams(dimension_semantics=sem, vmem_limit_bytes=VMEM_LIMIT)


def _dot(a, b):
    return jnp.dot(a, b, preferred_element_type=F32)


def _dot_nt(a, b):
    return lax.dot_general(a, b, (((1,), (1,)), ((), ())), preferred_element_type=F32)


def _dot_tn(a, b):
    return lax.dot_general(a, b, (((0,), (0,)), ((), ())), preferred_element_type=F32)


def _rms(x, g):
    return x * lax.rsqrt(jnp.mean(x * x, axis=-1, keepdims=True) + NORM_EPS) * g


def _sigmoid(x):
    return 1.0 / (1.0 + jnp.exp(-x))


def _row_spec(tm, n, offset=0):
    return pl.BlockSpec((tm, n), lambda i: (i + offset, 0))


def _head_row_spec(tm, n):
    return pl.BlockSpec((GLA_HEADS, tm, n), lambda i: (0, i, 0))


def _full_spec(shape):
    nd = len(shape)
    return pl.BlockSpec(shape, lambda *_: (0,) * nd)


def _norm_kernel(x_ref, g_ref, o_ref):
    o_ref[...] = _rms(x_ref[...], g_ref[...])


def _norm(x, g, tm):
    m, d = x.shape
    return pl.pallas_call(
        _norm_kernel, grid=(m // tm,),
        in_specs=[_row_spec(tm, d), _full_spec((1, d))],
        out_specs=_row_spec(tm, d),
        out_shape=jax.ShapeDtypeStruct((m, d), F32),
        compiler_params=_cparams("parallel"), name="rms_norm")(x, g.reshape(1, d))


def _norm_matmul_kernel(x_ref, g_ref, w_ref, o_ref):
    o_ref[...] = _dot(_rms(x_ref[...], g_ref[...]).astype(BF16), w_ref[...])


def _norm_matmul(x, g, w, tm):
    m, d = x.shape
    n = w.shape[1]
    return pl.pallas_call(
        _norm_matmul_kernel, grid=(m // tm,),
        in_specs=[_row_spec(tm, d), _full_spec((1, d)), _full_spec((d, n))],
        out_specs=_row_spec(tm, n),
        out_shape=jax.ShapeDtypeStruct((m, n), F32),
        compiler_params=_cparams("parallel"), name="norm_matmul")(x, g.reshape(1, d), w.astype(BF16))


def _rwkv_proj_kernel(h_ref, hp_ref, *refs, blocks_per_seq, norm_input):
    if norm_input:
        ng_ref, refs = refs[0], refs[1:]
        normed = lambda t_: _rms(t_, ng_ref[...])
    else:
        normed = lambda t_: t_
    (mu_ref, wr_ref, wk_ref, wv_ref, wq_ref, w1_ref, w2_ref, w0_ref, a1_ref, a2_ref, a0_ref, g1_ref, g2_ref,
     r_out, k_out, v_out, d_out, a_out, g_out, q_out) = refs

    def put(out, val):
        if blocks_per_seq:
            for p in range(RWKV_PAIRS):
                out[p] = val[:, p * LANES:(p + 1) * LANES]
        else:
            out[...] = val

    h = normed(h_ref[...])
    if blocks_per_seq:
        seq_start = (pl.program_id(0) % blocks_per_seq) == 0
        before = jnp.where(seq_start, 0.0, normed(hp_ref[SUBLANES - 1:SUBLANES, :]))
        row = lax.broadcasted_iota(jnp.int32, (h.shape[0], 1), 0)
        hp = jnp.where(row == 0, before, pltpu.roll(h, 1, axis=0))
    else:
        hp = hp_ref[...]
    xx = hp - h

    def mix(j):
        return (h + xx * mu_ref[j:j + 1, :]).astype(BF16)

    put(r_out, _dot(mix(0), wr_ref[...]))
    wl = w0_ref[...] + _dot(jnp.tanh(_dot(mix(1), w1_ref[...])).astype(BF16), w2_ref[...])
    put(k_out, _dot(mix(2), wk_ref[...]))
    put(v_out, _dot(mix(3), wv_ref[...]))
    al = a0_ref[...] + _dot(_dot(mix(4), a1_ref[...]).astype(BF16), a2_ref[...])
    g_out[...] = _dot(_sigmoid(_dot(mix(5), g1_ref[...])).astype(BF16), g2_ref[...])
    q_out[...] = _dot(h.astype(BF16), wq_ref[...])
    z = -wl
    softplus = jnp.maximum(z, 0.0) + jnp.log(1.0 + jnp.exp(-jnp.abs(z)))
    put(d_out, jnp.exp(-jnp.exp(-softplus - 0.5)))
    put(a_out, _sigmoid(al))


def _rwkv_proj(h, h_prev, ws, tm, seq_len, norm_g=None):
    m = h.shape[0]
    if norm_g is not None:
        assert h_prev is None
        ws = [norm_g.reshape(1, D_MODEL)] + list(ws)
    tw = TOK_WIDTH
    if h_prev is None:
        assert seq_len % tm == 0 and tm % SUBLANES == 0
        per8 = tm // SUBLANES
        hp_spec = pl.BlockSpec((SUBLANES, D_MODEL), lambda i: (jnp.maximum(i * per8 - 1, 0), 0))
        h_prev, blocks_per_seq = h, seq_len // tm
    else:
        hp_spec, blocks_per_seq = _row_spec(tm, D_MODEL), 0
    if blocks_per_seq:
        scan_spec = pl.BlockSpec((RWKV_PAIRS, tm, LANES), lambda i: (0, i, 0))
        scan_shape = jax.ShapeDtypeStruct((RWKV_PAIRS, m, LANES), F32)
    else:
        scan_spec, scan_shape = _row_spec(tm, tw), jax.ShapeDtypeStruct((m, tw), F32)
    return pl.pallas_call(
        functools.partial(_rwkv_proj_kernel, blocks_per_seq=blocks_per_seq, norm_input=norm_g is not None),
        grid=(m // tm,),
        in_specs=[_row_spec(tm, D_MODEL), hp_spec] + [_full_spec(w.shape) for w in ws],
        out_specs=[scan_spec] * 5 + [_row_spec(tm, tw), _row_spec(tm, MEM_WIDTH)],
        out_shape=[scan_shape] * 5 + [jax.ShapeDtypeStruct((m, tw), F32), jax.ShapeDtypeStruct((m, MEM_WIDTH), F32)],
        compiler_params=_cparams("parallel"), name="rwkv_proj")(h, h_prev, *ws)


def _pad_lanes(x):
    short = LANES - x.shape[-1]
    if short == 0:
        return x
    return jnp.concatenate([x, jnp.zeros(x.shape[:-1] + (short,), x.dtype)], axis=-1)


def _rwkv_scan_kernel(r_ref, k_ref, v_ref, d_ref, a_ref, kkp_ref, kap_ref, rkp_ref, lng_ref, lnb_ref, s0_ref,
                      y_ref, sfin_ref, s_scr, v_scr, yrow_scr, *, tc):
    n = RWKV_N
    nl = r_ref.shape[-1]

    @pl.when(pl.program_id(1) == 0)
    def _():
        s_scr[...] = _pad_lanes(s0_ref[...])

    kkp = _pad_lanes(kkp_ref[...])
    kap = _pad_lanes(kap_ref[...])
    rkp = _pad_lanes(rkp_ref[...])
    lng = _pad_lanes(lng_ref[...])
    lnb = _pad_lanes(lnb_ref[...])

    def step(t, carry):
        r_t = _pad_lanes(r_ref[t])
        k_t = _pad_lanes(k_ref[t])
        v_t = _pad_lanes(v_ref[t])
        d_t = _pad_lanes(d_ref[t])
        a_t = _pad_lanes(a_ref[t])
        v_scr[...] = v_t
        kkr = k_t * kkp
        nrm = jnp.maximum(jnp.sqrt(jnp.sum(kkr * kkr, axis=0, keepdims=True)), 1e-12)
        kk = kkr * (1.0 / nrm)
        k2 = k_t * (1.0 + (a_t - 1.0) * kap)
        nkk = -kk
        b_t = kk * a_t

        def ibody(i, c):
            s_i = s_scr[i]
            sa = jnp.sum(s_i * nkk, axis=0, keepdims=True)
            v_i = v_scr[pl.ds(i, 1), :]
            s_n = s_i * d_t + sa * b_t + v_i * k2
            s_scr[i] = s_n
            yrow_scr[pl.ds(i, 1), :] = jnp.sum(s_n * r_t, axis=0, keepdims=True)
            return c

        lax.fori_loop(0, n, ibody, 0, unroll=8)
        y = yrow_scr[...]
        yc = y - jnp.mean(y, axis=0, keepdims=True)
        var = jnp.mean(yc * yc, axis=0, keepdims=True)
        gn = yc * lax.rsqrt(var + RWKV_GN_EPS) * lng + lnb
        bonus = jnp.sum(r_t * k2 * rkp, axis=0, keepdims=True) * v_t
        y_ref[t] = (gn + bonus)[:, :nl]
        return carry

    lax.fori_loop(0, tc, step, 0)

    @pl.when(pl.program_id(1) == pl.num_programs(1) - 1)
    def _():
        sfin_ref[...] = s_scr[:, :, :nl]


def _rwkv_scan(r, k, v, d, a, params, s0, lane_block, tc):
    t, n, l = r.shape
    seq = pl.BlockSpec((tc, n, lane_block), lambda li, ti: (ti, 0, li))
    par = pl.BlockSpec((n, lane_block), lambda li, ti: (0, li))
    st = pl.BlockSpec((n, n, lane_block), lambda li, ti: (0, 0, li))
    return pl.pallas_call(
        functools.partial(_rwkv_scan_kernel, tc=tc), grid=(l // lane_block, t // tc),
        in_specs=[seq] * 5 + [par] * 5 + [st],
        out_specs=[seq, st],
        out_shape=[jax.ShapeDtypeStruct((t, n, l), F32), jax.ShapeDtypeStruct((n, n, l), F32)],
        scratch_shapes=[pltpu.VMEM((n, n, LANES), F32), pltpu.VMEM((n, LANES), F32), pltpu.VMEM((n, LANES), F32)],
        compiler_params=_cparams("parallel", "arbitrary"), name="rwkv_scan")(r, k, v, d, a, *params, s0)


PREP_TILES = 6
DECAY_WINDOW = 16


def _rwkv_prep_kernel(r_ref, k_ref, v_ref, d_ref, a_ref, kkp_ref, kap_ref, o_ref, *, tc, nb):
    n = RWKV_N
    rows_per_pair = nb * tc
    ins = [ref.reshape(RWKV_PAIRS * rows_per_pair, LANES) for ref in (r_ref, k_ref, v_ref, d_ref, a_ref)]
    zero_rows = jnp.zeros((LANES - RWKV_HEADS * nb, LANES), F32)
    first_half = lax.broadcasted_iota(jnp.int32, (nb, LANES), 1) < n
    kkp = kkp_ref[...]
    kap = kap_ref[...]

    def load_transposed_pair(x2, t):
        pieces = []
        for p in range(RWKV_PAIRS):
            now = x2[pl.ds(p * rows_per_pair + t, nb, stride=tc), :]
            nxt = x2[pl.ds(p * rows_per_pair + t + 1, nb, stride=tc), :]
            pieces += [jnp.where(first_half, now, pltpu.roll(nxt, n, axis=1)),
                       jnp.where(first_half, pltpu.roll(now, n, axis=1), nxt)]
        both = jnp.concatenate(pieces + [zero_rows], axis=0).T
        return both[:n, :], both[n:, :]

    def emit(t, c_prev, r_t, k_t, v_t, d_t, a_t):
        kkr = k_t * kkp
        nrm = jnp.maximum(jnp.sqrt(jnp.sum(kkr * kkr, axis=0, keepdims=True)), 1e-12)
        kk = kkr * (1.0 / nrm)
        k2 = k_t * (1.0 + (a_t - 1.0) * kap)
        c_prev = jnp.where(t % DECAY_WINDOW == 0, 1.0, c_prev)
        c_t = c_prev * d_t
        inv = 1.0 / jnp.maximum(c_t, 1e-30)
        o_ref[t, 0] = -kk * c_prev
        o_ref[t, 1] = c_t
        o_ref[t, 2] = kk * a_t * inv
        o_ref[t, 3] = k2 * inv
        o_ref[t, 4] = r_t * c_t
        o_ref[t, 5] = v_t
        return c_t

    def two_steps(u, c):
        t = 2 * u
        tiles = [load_transposed_pair(x2, t) for x2 in ins]
        c = emit(t, c, *[tile[0] for tile in tiles])
        return emit(t + 1, c, *[tile[1] for tile in tiles])

    lax.fori_loop(0, tc // 2, two_steps, jnp.ones((n, LANES), F32), unroll=2)


def _rwkv_state_scan_kernel(x_ref, rkp_ref, lng_ref, lnb_ref, y_ref, sfin_ref, s_scr, sa_scr, yrow_scr, *, tc):
    n = RWKV_N
    groups = n // SUBLANES

    @pl.when(pl.program_id(0) == 0)
    def _():
        s_scr[...] = jnp.zeros(s_scr.shape, F32)

    rkp = rkp_ref[...]
    lng = lng_ref[...]
    lnb = lnb_ref[...]
    sub = lax.broadcasted_iota(jnp.int32, (SUBLANES, LANES), 0)
    low4 = sub < 4
    low2 = (sub & 3) < 2
    low1 = (sub & 1) == 0

    def fold(x, y, dist, low):
        if dist == 4:
            return jnp.where(low, x, y) + pltpu.roll(jnp.where(low, y, x), 4, axis=0)
        return (jnp.where(low, x, pltpu.roll(y, dist, axis=0))
                + jnp.where(low, pltpu.roll(x, SUBLANES - dist, axis=0), y))

    def sublane_sums(ps):
        z = [fold(ps[0], ps[4], 4, low4), fold(ps[2], ps[6], 4, low4),
             fold(ps[1], ps[5], 4, low4), fold(ps[3], ps[7], 4, low4)]
        return fold(fold(z[0], z[1], 2, low2), fold(z[2], z[3], 2, low2), 1, low1)

    def tile_sum(x):
        acc = x[0:SUBLANES]
        for u in range(1, groups):
            acc = acc + x[u * SUBLANES:(u + 1) * SUBLANES]
        return acc

    def step(t, carry):
        nkk = x_ref[t, 0]
        b_t = x_ref[t, 2]
        k2 = x_ref[t, 3]
        r_t = x_ref[t, 4]
        for g in range(groups):
            sa_scr[g * SUBLANES:(g + 1) * SUBLANES, :] = sublane_sums(
                [tile_sum(s_scr[g * SUBLANES + u] * nkk) for u in range(SUBLANES)])
        for g in range(groups):
            ps = []
            for u in range(SUBLANES):
                i = g * SUBLANES + u
                s_n = s_scr[i] + sa_scr[pl.ds(i, 1), :] * b_t + x_ref[t, 5, pl.ds(i, 1), :] * k2
                s_scr[i] = s_n
                ps.append(tile_sum(s_n * r_t))
            yrow_scr[g * SUBLANES:(g + 1) * SUBLANES, :] = sublane_sums(ps)
        y = yrow_scr[...]
        yc = y - jnp.mean(y, axis=0, keepdims=True)
        var = jnp.mean(yc * yc, axis=0, keepdims=True)
        bonus = jnp.sum(r_t * k2 * rkp, axis=0, keepdims=True) * x_ref[t, 5]
        y_ref[t] = yc * lax.rsqrt(var + RWKV_GN_EPS) * lng + lnb + bonus

        @pl.when(t % DECAY_WINDOW == DECAY_WINDOW - 1)
        def _():
            c_t = x_ref[t, 1]
            for i in range(n):
                s_scr[i] = s_scr[i] * c_t

        return carry

    lax.fori_loop(0, tc, step, 0)

    @pl.when(pl.program_id(0) == pl.num_programs(0) - 1)
    def _():
        sfin_ref[...] = s_scr[...]


def _rwkv_unprep_kernel(y_ref, o_ref, *, tc, nb):
    n = RWKV_N
    rows_per_pair = nb * tc
    o2 = o_ref.reshape(RWKV_PAIRS * rows_per_pair, LANES)
    first_half = lax.broadcasted_iota(jnp.int32, (nb, LANES), 1) < n

    def two_steps(u, carry):
        t = 2 * u
        w = jnp.concatenate([y_ref[t], y_ref[t + 1]], axis=0).T
        for p in range(RWKV_PAIRS):
            even = w[(2 * p) * nb:(2 * p + 1) * nb, :]
            odd = w[(2 * p + 1) * nb:(2 * p + 2) * nb, :]
            o2[pl.ds(p * rows_per_pair + t, nb, stride=tc), :] = jnp.where(
                first_half, even, pltpu.roll(odd, n, axis=1))
            o2[pl.ds(p * rows_per_pair + t + 1, nb, stride=tc), :] = jnp.where(
                first_half, pltpu.roll(even, n, axis=1), odd)
        return carry

    lax.fori_loop(0, tc // 2, two_steps, 0, unroll=2)


def _rwkv_prompt_mixer(r, k, v, d, a, kkp, kap, rkp, lng, lnb, nb, t, tc):
    n = RWKV_N
    assert tc % SUBLANES == 0 and tc % 4 == 0 and t % tc == 0 and RWKV_HEADS * nb <= LANES
    assert tc % DECAY_WINDOW == 0
    grid = (t // tc,)
    tok = pl.BlockSpec((RWKV_PAIRS, nb, tc, LANES), lambda ti: (0, 0, ti, 0))
    par = pl.BlockSpec((n, LANES), lambda ti: (0, 0))
    tiles = pl.BlockSpec((tc, PREP_TILES, n, LANES), lambda ti: (ti, 0, 0, 0))
    ytile = pl.BlockSpec((tc, n, LANES), lambda ti: (ti, 0, 0))
    prepared = pl.pallas_call(
        functools.partial(_rwkv_prep_kernel, tc=tc, nb=nb), grid=grid,
        in_specs=[tok] * 5 + [par] * 2, out_specs=tiles,
        out_shape=jax.ShapeDtypeStruct((t, PREP_TILES, n, LANES), F32),
        compiler_params=_cparams("parallel"), name="rwkv_prep")(
            *[x.reshape(RWKV_PAIRS, nb, t, LANES) for x in (r, k, v, d, a)], kkp, kap)
    y, s = pl.pallas_call(
        functools.partial(_rwkv_state_scan_kernel, tc=tc), grid=grid,
        in_specs=[tiles, par, par, par],
        out_specs=[ytile, pl.BlockSpec((n, n, LANES), lambda ti: (0, 0, 0))],
        out_shape=[jax.ShapeDtypeStruct((t, n, LANES), F32), jax.ShapeDtypeStruct((n, n, LANES), F32)],
        scratch_shapes=[pltpu.VMEM((n, n, LANES), F32), pltpu.VMEM((n, LANES), F32), pltpu.VMEM((n, LANES), F32)],
        compiler_params=_cparams("arbitrary"), name="rwkv_state_scan")(prepared, rkp, lng, lnb)
    tok_out = pl.pallas_call(
        functools.partial(_rwkv_unprep_kernel, tc=tc, nb=nb), grid=grid,
        in_specs=[ytile], out_specs=tok,
        out_shape=jax.ShapeDtypeStruct((RWKV_PAIRS, nb, t, LANES), F32),
        compiler_params=_cparams("parallel"), name="rwkv_unprep")(y)
    return tok_out.reshape(RWKV_PAIRS, nb * t, LANES), s


def _gla_proj_kernel(h_ref, wq_ref, wk_ref, wv_ref, wr_ref, wm_ref, a1_ref, a2_ref, ab_ref,
                     q_out, k_out, v_out, r_out, la_out, qm_out):
    hb = h_ref[...].astype(BF16)
    low = _dot(hb, a1_ref[...]).astype(BF16)
    for hd in range(GLA_HEADS):
        q_out[hd] = _dot(hb, wq_ref[hd])
        k_out[hd] = _dot(hb, wk_ref[hd])
        v_out[hd] = _dot(hb, wv_ref[hd])
        r_out[hd] = _dot(hb, wr_ref[hd])
        x = _dot(low, a2_ref[hd]) + ab_ref[hd]
        log_sigmoid = jnp.minimum(x, 0.0) - jnp.log(1.0 + jnp.exp(-jnp.abs(x)))
        la_out[hd] = log_sigmoid / GLA_TAU
    qm_out[...] = _dot(hb, wm_ref[...])


def _gla_weights(w_in, a1, a2, ab):
    kw, tw, nh = GLA_KW, TOK_WIDTH, GLA_HEADS

    def heads(w, d):
        return w.reshape(w.shape[0], nh, d).transpose(1, 0, 2)

    return [heads(w_in[:, :kw], GLA_DK).astype(BF16), heads(w_in[:, kw:2 * kw], GLA_DK).astype(BF16),
            heads(w_in[:, 2 * kw:2 * kw + tw], GLA_DV).astype(BF16),
            heads(w_in[:, 2 * kw + tw:3 * tw], GLA_DV).astype(BF16), w_in[:, 3 * tw:].astype(BF16),
            a1.astype(BF16), heads(a2, GLA_DK).astype(BF16), ab.reshape(nh, 1, GLA_DK)]


def _gla_proj(h, ws, tm):
    m = h.shape[0]
    nh = GLA_HEADS
    widths = [GLA_DK, GLA_DK, GLA_DV, GLA_DV, GLA_DK]
    return pl.pallas_call(
        _gla_proj_kernel, grid=(m // tm,),
        in_specs=[_row_spec(tm, D_MODEL)] + [_full_spec(w.shape) for w in ws],
        out_specs=[_head_row_spec(tm, w) for w in widths] + [_row_spec(tm, MEM_WIDTH)],
        out_shape=[jax.ShapeDtypeStruct((nh, m, w), F32) for w in widths] + [jax.ShapeDtypeStruct((m, MEM_WIDTH), F32)],
        compiler_params=_cparams("parallel"), name="gla_proj")(h, *ws)


def _gla_out_norm(o, g):
    return o * lax.rsqrt(jnp.mean(o * o, axis=-1, keepdims=True) + NORM_EPS) * g


def _gla_chunk_kernel(q_ref, k_ref, v_ref, la_ref, ng_ref, o_ref, sfin_ref, st_scr):
    c, dk = GLA_CHUNK, GLA_DK

    @pl.when(pl.program_id(1) == 0)
    def _():
        st_scr[...] = jnp.zeros(st_scr.shape, F32)

    row = lax.broadcasted_iota(jnp.int32, (c, c), 0)
    col = lax.broadcasted_iota(jnp.int32, (c, c), 1)
    tril = (row >= col).astype(F32)
    rr = lax.broadcasted_iota(jnp.int32, (c, dk), 0)
    ones_sum = jnp.ones((dk, LANES), BF16)

    heads = range(GLA_HEADS)
    tril_b = tril.astype(BF16)

    def cumsum_rows(la):
        hi = la.astype(BF16)
        r1 = la - hi.astype(F32)
        mid = r1.astype(BF16)
        lo = (r1 - mid.astype(F32)).astype(BF16)
        return _dot(tril_b, hi) + _dot(tril_b, mid) + _dot(tril_b, lo)

    def tile_roll(x, dlt):
        return pltpu.roll(x.reshape(c // GLA_TILE, GLA_TILE, dk), dlt, axis=1).reshape(c, dk)

    k = [k_ref[hd] for hd in heads]
    vb = [v_ref[hd].astype(BF16) for hd in heads]
    q = [q_ref[hd] * (dk ** -0.5) for hd in heads]
    b = [cumsum_rows(la_ref[hd]) for hd in heads]
    st = [st_scr[hd] for hd in heads]
    inter = [_dot_nt((q[hd] * jnp.exp(b[hd])).astype(BF16), st[hd].astype(BF16)) for hd in heads]

    att = [jnp.zeros((c, c), F32) for hd in heads]
    blk = c // 2
    while blk >= GLA_TILE:
        two = 2 * blk
        upper = (rr & (two - 1)) >= blk
        same_block = (row ^ col) < two
        parts = []
        for hd in heads:
            b_ref_rows = jnp.concatenate(
                [jnp.broadcast_to(b[hd][s0 + blk - 1:s0 + blk, :], (two, dk)) for s0 in range(0, c, two)], axis=0)
            q_l = jnp.where(upper, q[hd] * jnp.exp(jnp.minimum(b[hd] - b_ref_rows, 0.0)), 0.0).astype(BF16)
            k_l = jnp.where(upper, 0.0, k[hd] * jnp.exp(jnp.minimum(b_ref_rows - b[hd], 0.0))).astype(BF16)
            parts.append(_dot_nt(q_l, k_l))
        att = [att[hd] + jnp.where(same_block, parts[hd], 0.0) for hd in heads]
        blk //= 2

    sums = []
    for hd in heads:
        prods = [(q[hd] * k[hd]).astype(BF16)]
        for dlt in range(1, GLA_TILE):
            p = q[hd] * tile_roll(k[hd], dlt) * jnp.exp(jnp.minimum(b[hd] - tile_roll(b[hd], dlt), 0.0))
            prods.append(jnp.where((rr & (GLA_TILE - 1)) >= dlt, p, 0.0).astype(BF16))
        sums.append(_dot(jnp.concatenate(prods, axis=0), ones_sum))
    for dlt in range(GLA_TILE):
        on_diag = col == row - dlt
        att = [att[hd] + jnp.where(on_diag, sums[hd][dlt * c:(dlt + 1) * c, :c], 0.0) for hd in heads]

    o = [inter[hd] + _dot(att[hd].astype(BF16), vb[hd]) for hd in heads]
    for hd in heads:
        o_ref[hd] = _gla_out_norm(o[hd], ng_ref[...])
    for hd in heads:
        b_end = b[hd][c - 1:c, :]
        kd = (k[hd] * jnp.exp(b_end - b[hd])).astype(BF16)
        st_scr[hd] = jnp.exp(b_end) * st[hd] + _dot_tn(vb[hd], kd)

    @pl.when(pl.program_id(1) == pl.num_programs(1) - 1)
    def _():
        sfin_ref[0] = st_scr[...]


def _gla_chunk_scan(q, k, v, la, norm_g, batch, t):
    nh, dk, dv, c = GLA_HEADS, GLA_DK, GLA_DV, GLA_CHUNK
    nc = t // c
    kspec = pl.BlockSpec((nh, c, dk), lambda i, j: (0, i * nc + j, 0))
    vspec = pl.BlockSpec((nh, c, dv), lambda i, j: (0, i * nc + j, 0))
    sspec = pl.BlockSpec((1, nh, dv, dk), lambda i, j: (i, 0, 0, 0))
    return pl.pallas_call(
        _gla_chunk_kernel, grid=(batch, nc),
        in_specs=[kspec, kspec, vspec, kspec, pl.BlockSpec((1, dv), lambda i, j: (0, 0))],
        out_specs=[vspec, sspec],
        out_shape=[jax.ShapeDtypeStruct((nh, batch * t, dv), F32), jax.ShapeDtypeStruct((batch, nh, dv, dk), F32)],
        scratch_shapes=[pltpu.VMEM((nh, dv, dk), F32)],
        compiler_params=_cparams("parallel", "arbitrary"), name="gla_chunk")(q, k, v, la, norm_g.reshape(1, dv))


GLA_STEP_GROUP = 8


def _gla_step_kernel(q_ref, k_ref, v_ref, la_ref, ng_ref, s0_ref, o_ref, s_ref):
    g_n, dk = GLA_STEP_GROUP, GLA_DK

    def columns(ref):
        x = jnp.concatenate([ref[...], jnp.zeros((g_n, LANES - dk), F32)], axis=1)
        return jnp.concatenate([x, jnp.zeros((LANES - g_n, LANES), F32)], axis=0).T

    q_cols, k_cols, la_cols = columns(q_ref), columns(k_ref), columns(la_ref)
    v = v_ref[...]
    rows = []
    for g in range(g_n):
        kv = k_cols[:dk, g:g + 1].astype(BF16).astype(F32) * v[g:g + 1, :].astype(BF16).astype(F32)
        s_new = jnp.exp(la_cols[:dk, g:g + 1]) * s0_ref[g] + kv
        s_ref[g] = s_new
        o = jnp.sum((q_cols[:dk, g:g + 1] * (dk ** -0.5)) * s_new, axis=0, keepdims=True)
        rows.append(_gla_out_norm(o, ng_ref[...]))
    o_ref[...] = jnp.concatenate(rows, axis=0)


def _gla_step(q, k, v, la, norm_g, s0):
    bh, dk = q.shape
    dv = v.shape[1]
    g = GLA_STEP_GROUP
    kspec = _row_spec(g, dk)
    vspec = _row_spec(g, dv)
    sspec = pl.BlockSpec((g, dk, dv), lambda i: (i, 0, 0))
    return pl.pallas_call(
        _gla_step_kernel, grid=(bh // g,),
        in_specs=[kspec, kspec, vspec, kspec, _full_spec((1, dv)), sspec],
        out_specs=[vspec, sspec],
        out_shape=[jax.ShapeDtypeStruct((bh, dv), F32), jax.ShapeDtypeStruct((bh, dk, dv), F32)],
        compiler_params=_cparams("parallel"), name="gla_step")(q, k, v, la, norm_g.reshape(1, dv), s0)


def _mem_attn_kernel(q_ref, k_ref, v_ref, o_ref):
    q = q_ref[0]
    k = k_ref[0].astype(BF16)
    v_ones = jnp.concatenate([v_ref[0].astype(BF16), jnp.ones((N_MEM, LANES), BF16)], axis=1)
    head_of_lane = lax.broadcasted_iota(jnp.int32, (1, MEM_WIDTH), 1) // MEM_HEAD_DIM
    heads = range(MEM_HEADS)
    mine = [head_of_lane == h for h in heads]
    qh = [jnp.where(mine[h], q, 0.0).astype(BF16) for h in heads]
    s = [_dot_nt(qh[h], k) * (MEM_HEAD_DIM ** -0.5) for h in heads]
    e = [jnp.exp(s[h] - jnp.max(s[h], axis=-1, keepdims=True)).astype(BF16) for h in heads]
    ev = [_dot(e[h], v_ones) for h in heads]
    out = jnp.zeros(q.shape, F32)
    for h in heads:
        inv = 1.0 / ev[h][:, MEM_WIDTH:]
        out = out + jnp.where(mine[h], ev[h][:, :MEM_WIDTH] * jnp.concatenate([inv, inv], axis=1), 0.0)
    o_ref[0] = out


def _mem_attn(q, mem_k, mem_v, layer, tq):
    b, t, w = q.shape
    qspec = pl.BlockSpec((1, tq, w), lambda i, j: (i, j, 0))
    mspec = pl.BlockSpec((None, 1, N_MEM, w), lambda i, j: (layer, i, 0, 0))
    return pl.pallas_call(
        _mem_attn_kernel, grid=(b, t // tq),
        in_specs=[qspec, mspec, mspec], out_specs=qspec,
        out_shape=jax.ShapeDtypeStruct((b, t, w), F32),
        compiler_params=_cparams("parallel", "parallel"), name="mem_attn")(q, mem_k, mem_v)


def _out_proj_kernel(*refs, layout, aliased):
    tok_ref, gate_ref, att_ref, x_ref, wo_ref, g_ref, wr_hi_ref, wr_lo_ref, br_ref = refs[:9]
    x1_out, h2_out, logit_out = refs[9 + aliased:]
    x1 = x_ref[...] + _dot(att_ref[...].astype(BF16), wo_ref[TOK_WIDTH:, :])
    if layout == "gla_heads":
        for hd in range(GLA_HEADS):
            gate = gate_ref[hd]
            mixed = (tok_ref[hd] * (gate * _sigmoid(gate))).astype(BF16)
            x1 = x1 + _dot(mixed, wo_ref[hd * GLA_DV:(hd + 1) * GLA_DV, :])
    else:
        if layout == "rwkv_pairs":
            tok = jnp.concatenate([tok_ref[p] for p in range(RWKV_PAIRS)], axis=1)
        else:
            tok = tok_ref[...]
        x1 = x1 + _dot((tok * gate_ref[...]).astype(BF16), wo_ref[:TOK_WIDTH, :])
    x1_out[...] = x1
    h2 = _rms(x1, g_ref[...])
    h2_out[...] = h2
    h_hi = h2.astype(BF16)
    h_lo = (h2 - h_hi.astype(F32)).astype(BF16)
    logit_out[...] = (_dot(h_hi, wr_hi_ref[...]) + _dot(h_lo, wr_hi_ref[...]) + _dot(h_hi, wr_lo_ref[...])
                      + br_ref[...])


def _out_proj(tok, gate, att, x, ws, layout, tm, h2_rows, h2_row_offset, h2_buffer=None):
    m = x.shape[0]
    aliased = h2_buffer is not None
    rows_spec = _row_spec(tm, TOK_WIDTH)
    tok_spec, gate_spec = {
        "rows": (rows_spec, rows_spec),
        "rwkv_pairs": (pl.BlockSpec((RWKV_PAIRS, tm, LANES), lambda i: (0, i, 0)), rows_spec),
        "gla_heads": (_head_row_spec(tm, GLA_DV), _head_row_spec(tm, GLA_DV))}[layout]
    in_specs = ([tok_spec, gate_spec, _row_spec(tm, MEM_WIDTH), _row_spec(tm, D_MODEL)]
                + [_full_spec(w.shape) for w in ws])
    args = [tok, gate, att, x, *ws]
    if aliased:
        in_specs.append(pl.BlockSpec(memory_space=pl.ANY))
        args.append(h2_buffer)
    return pl.pallas_call(
        functools.partial(_out_proj_kernel, layout=layout, aliased=int(aliased)), grid=(m // tm,),
        in_specs=in_specs,
        out_specs=[_row_spec(tm, D_MODEL), _row_spec(tm, D_MODEL, h2_row_offset // tm), _row_spec(tm, ROUTER_LANES)],
        out_shape=[jax.ShapeDtypeStruct((m, D_MODEL), F32), jax.ShapeDtypeStruct((h2_rows, D_MODEL), F32),
                   jax.ShapeDtypeStruct((m, ROUTER_LANES), F32)],
        input_output_aliases={len(args) - 1: 1} if aliased else {},
        compiler_params=_cparams("parallel"), name="out_proj")(*args)


def _moe_kernel(tile_ref, exp_ref, nitem_ref, lo_ref, hi_ref, x_ref, gate_ref, wu_ref, wd_ref, o_ref, wu_scr, wd_scr):
    w = pl.program_id(0)
    prev = jnp.maximum(w - 1, 0)
    e = exp_ref[w]
    valid = w < nitem_ref[0]

    @pl.when(jnp.logical_and(valid, jnp.logical_or(w == 0, e != exp_ref[prev])))
    def _():
        wu_scr[...] = wu_ref[0, 0].astype(BF16)
        wd_scr[...] = wd_ref[0, 0].astype(BF16)

    @pl.when(jnp.logical_or(w == 0, tile_ref[w] != tile_ref[prev]))
    def _():
        o_ref[...] = jnp.zeros(o_ref.shape, F32)

    @pl.when(valid)
    def _():
        gu = _dot(x_ref[...].astype(BF16), wu_scr[...])
        g = gu[:, :EXPERT_FF]
        act = (g * _sigmoid(g) * gu[:, EXPERT_FF:]).astype(BF16)
        out = _dot(act, wd_scr[...]) * gate_ref[...]
        rows = tile_ref[w] * MOE_BLOCK + lax.broadcasted_iota(jnp.int32, (MOE_BLOCK, 1), 0)
        mine = jnp.logical_and(rows >= lo_ref[e], rows < hi_ref[e])
        o_ref[...] = o_ref[...] + jnp.where(mine, out, 0.0)


def _moe_ffn(xs, row_gate, item_tile, item_expert, n_items, lo, hi, w_up, w_down, layer):
    tm = MOE_BLOCK
    n_work = item_tile.shape[0]
    a = xs.shape[0]
    row_map = lambda w, tile, ex, ni, lo_, hi_: (tile[w], 0)
    exp_map = lambda w, tile, ex, ni, lo_, hi_: (layer, ex[w], 0, 0)
    grid_spec = pltpu.PrefetchScalarGridSpec(
        num_scalar_prefetch=5, grid=(n_work,),
        in_specs=[pl.BlockSpec((tm, D_MODEL), row_map),
                  pl.BlockSpec((tm, 1), row_map),
                  pl.BlockSpec((1, 1, D_MODEL, 2 * EXPERT_FF), exp_map),
                  pl.BlockSpec((1, 1, EXPERT_FF, D_MODEL), exp_map)],
        out_specs=pl.BlockSpec((tm, D_MODEL), row_map),
        scratch_shapes=[pltpu.VMEM((D_MODEL, 2 * EXPERT_FF), BF16), pltpu.VMEM((EXPERT_FF, D_MODEL), BF16)])
    return pl.pallas_call(
        _moe_kernel, grid_spec=grid_spec,
        out_shape=jax.ShapeDtypeStruct((a, D_MODEL), F32),
        compiler_params=_cparams("arbitrary"), name="moe_ffn")(
            item_tile, item_expert, n_items, lo, hi, xs, row_gate, w_up, w_down)


def _route(logits, n_prompt, tm):
    m = logits.shape[0]
    n_sample = m - n_prompt
    a = 2 * m
    gl = logits[:, :N_GROUPS]
    el = logits[:, N_GROUPS:N_GROUPS + N_EXPERTS]
    group = jnp.argmax(gl, -1).astype(jnp.int32)
    p_group = jnp.max(jax.nn.softmax(gl, -1), -1, keepdims=True)
    group_of_expert = jnp.arange(N_EXPERTS, dtype=jnp.int32) // EXPERTS_PER_GROUP
    top_val, expert = lax.top_k(jnp.where(group_of_expert[None, :] == group[:, None], el, -jnp.inf), 2)
    gate = p_group * jax.nn.softmax(top_val, -1)
    expert = expert.astype(jnp.int32)

    def by_id(t):
        return jnp.concatenate([t[:n_prompt, 0], t[:n_prompt, 1], t[n_prompt:, 0], t[n_prompt:, 1]])

    tok_of_id = jnp.asarray(np.concatenate([np.arange(n_prompt), np.arange(n_prompt),
                                            n_prompt + np.arange(n_sample), n_prompt + np.arange(n_sample)]), jnp.int32)
    flat_e = by_id(expert)
    ids = jnp.arange(a, dtype=jnp.int32)
    _, order, gate_sorted, tok_sorted = lax.sort((flat_e, ids, by_id(gate), tok_of_id), num_keys=1, is_stable=True)
    _, inv = lax.sort((order, ids), num_keys=1)
    experts = jnp.arange(N_EXPERTS, dtype=jnp.int32)
    counts = jnp.sum((flat_e[:, None] == experts[None, :]).astype(jnp.int32), axis=0)
    hi = jnp.cumsum(counts).astype(jnp.int32)
    lo = hi - counts
    n_tiles = a // tm
    first_tile = lo // tm
    tiles_of = jnp.where(counts > 0, (hi - 1) // tm - first_tile + 1, 0)
    item_end = jnp.cumsum(tiles_of).astype(jnp.int32)
    n_items = item_end[-1:]
    n_work = n_tiles + N_EXPERTS - 1
    w = jnp.minimum(jnp.arange(n_work, dtype=jnp.int32), n_items[0] - 1)
    item_expert = jnp.sum((item_end[None, :] <= w[:, None]).astype(jnp.int32), axis=1)
    onehot = (item_expert[:, None] == experts[None, :]).astype(jnp.int32)
    item_tile = jnp.sum(onehot * (first_tile - (item_end - tiles_of))[None, :], axis=1) + w
    return tok_sorted, gate_sorted.reshape(a, 1), inv, item_tile.astype(jnp.int32), item_expert, n_items, lo, hi


def _combine_kernel(x_ref, r0_ref, r1_ref, g_ref, *outs):
    x2 = x_ref[...] + (r0_ref[...] + r1_ref[...])
    outs[-1][...] = _rms(x2, g_ref[...])
    if len(outs) == 2:
        outs[0][...] = x2


def _combine(x1, gathered, first_row, g, tm, last_layer):
    m = x1.shape[0]
    spec = _row_spec(tm, D_MODEL)
    n_out = 1 if last_layer else 2
    outs = pl.pallas_call(
        _combine_kernel, grid=(m // tm,),
        in_specs=[spec, _row_spec(tm, D_MODEL, first_row // tm), _row_spec(tm, D_MODEL, (first_row + m) // tm),
                  _full_spec((1, D_MODEL))],
        out_specs=[spec] * n_out,
        out_shape=[jax.ShapeDtypeStruct((m, D_MODEL), F32)] * n_out,
        compiler_params=_cparams("parallel"), name="moe_combine")(x1, gathered, gathered, g.reshape(1, D_MODEL))
    return (None, outs[0]) if last_layer else tuple(outs)


def _to_scan_layout(t2d, b, t):
    return t2d.reshape(b, t, RWKV_HEADS, RWKV_N).transpose(1, 3, 0, 2).reshape(t, RWKV_N, b * RWKV_HEADS)


def _from_scan_layout(y, b, t):
    return y.reshape(t, RWKV_N, b, RWKV_HEADS).transpose(2, 0, 3, 1).reshape(b * t, TOK_WIDTH)


def _scan_param(p, b):
    return jnp.tile(p.reshape(RWKV_HEADS, RWKV_N).T, (1, b))


def _seq_scan_param(p, b):
    lanes = jnp.repeat(p.reshape(RWKV_HEADS, RWKV_N).T, b, axis=1)
    return jnp.pad(lanes, ((0, 0), (0, LANES - RWKV_HEADS * b)))


def _state_to_scan(s):
    b = s.shape[0]
    return s.transpose(2, 3, 0, 1).reshape(RWKV_N, RWKV_N, b * RWKV_HEADS)


def _state_from_scan(s, b):
    return s.reshape(RWKV_N, RWKV_N, b, RWKV_HEADS).transpose(2, 3, 0, 1)


def kernel(x_prompt, x_sample, mem_prompt, state_rwkv_S, state_rwkv_shift, state_gla_S, cache_mem_k, cache_mem_v, norm_mix_g, norm_ffn_g, norm_mem_g, norm_final_g, w_in, w_out, w_mem_kv, rw_mu, rw_w0, rw_w1, rw_w2, rw_a0, rw_a1, rw_a2, rw_g1, rw_g2, rw_k_k, rw_k_a, rw_r_k, rw_ln_g, rw_ln_b, gla_a1, gla_a2, gla_ab, gla_norm_g, router_wg, router_bg, router_we, router_be, exp_w_up, exp_w_down):
    bp, tp, _ = x_prompt.shape
    bs, ts, _ = x_sample.shape
    assert ts == 1 and tp % GLA_CHUNK == 0 and tp % SCAN_TIME_BLOCK == 0
    np_ = bp * tp
    ns = bs * ts
    m = np_ + ns
    assert np_ % LIGHT_BLOCK == 0 and ns % SAMPLE_BLOCK == 0 and (2 * m) % MOE_BLOCK == 0
    depth = w_in.shape[0]
    nh = GLA_HEADS
    tw = TOK_WIDTH
    bf = lambda t_: t_.astype(BF16)

    mem2d = mem_prompt.reshape(bp * N_MEM, D_MODEL)
    mem_kv = [_norm_matmul(mem2d, norm_mem_g[i], w_mem_kv[i], 512) for i in range(depth)]
    pk = jnp.stack([kv[:, :MEM_WIDTH].reshape(bp, N_MEM, MEM_WIDTH) for kv in mem_kv])
    pv = jnp.stack([kv[:, MEM_WIDTH:].reshape(bp, N_MEM, MEM_WIDTH) for kv in mem_kv])
    prompt_mem_k = pk.reshape(depth, bp, N_MEM, MEM_HEADS, MEM_HEAD_DIM)
    prompt_mem_v = pv.reshape(depth, bp, N_MEM, MEM_HEADS, MEM_HEAD_DIM)
    sk = cache_mem_k.reshape(depth, bs, N_MEM, MEM_WIDTH)
    sv = cache_mem_v.reshape(depth, bs, N_MEM, MEM_WIDTH)

    x_p = x_prompt.reshape(np_, D_MODEL)
    x_s = x_sample.reshape(ns, D_MODEL)
    h_p = None
    h_s = _norm(x_s, norm_mix_g[0], SAMPLE_BLOCK)

    w_router = jnp.zeros((depth, D_MODEL, ROUTER_LANES), F32)
    w_router = w_router.at[:, :, :N_GROUPS].set(router_wg).at[:, :, N_GROUPS:N_GROUPS + N_EXPERTS].set(router_we)
    b_router = jnp.zeros((depth, 1, ROUTER_LANES), F32)
    b_router = b_router.at[:, 0, :N_GROUPS].set(router_bg).at[:, 0, N_GROUPS:N_GROUPS + N_EXPERTS].set(router_be)

    p_rw_S, p_rw_shift, p_gla_S, s_rw_S, s_rw_shift, s_gla_S = [], [], [], [], [], []
    for i in range(depth):
        j = i // 2
        if i % 2 == 0:
            wi = w_in[i]
            ws = [rw_mu[j], bf(wi[:, :tw]), bf(wi[:, tw:2 * tw]), bf(wi[:, 2 * tw:3 * tw]), bf(wi[:, 3 * tw:]),
                  bf(rw_w1[j]), bf(rw_w2[j]), rw_w0[j].reshape(1, tw), bf(rw_a1[j]), bf(rw_a2[j]),
                  rw_a0[j].reshape(1, tw), bf(rw_g1[j]), bf(rw_g2[j])]
            if h_p is None:
                *rkvda_p, gate_p, qm_p = _rwkv_proj(x_p, None, ws, PROJ_BLOCK, tp, norm_g=norm_mix_g[i])
                shift_p = _norm(x_prompt[:, -1, :], norm_mix_g[i], bp)
            else:
                *rkvda_p, gate_p, qm_p = _rwkv_proj(h_p, None, ws, PROJ_BLOCK, tp)
                shift_p = h_p.reshape(bp, tp, D_MODEL)[:, -1]
            *rkvda_s, gate_s, qm_s = _rwkv_proj(h_s, state_rwkv_shift[j], ws, SAMPLE_BLOCK, ts)
            pvec = [rw_k_k[j], rw_k_a[j], rw_r_k[j], rw_ln_g[j], rw_ln_b[j]]
            tok_p, sp = _rwkv_prompt_mixer(*rkvda_p, *[_seq_scan_param(p, bp) for p in pvec], bp, tp, SCAN_TIME_BLOCK)
            ys, ss = _rwkv_scan(*[_to_scan_layout(t_, bs, ts) for t_ in rkvda_s], [_scan_param(p, bs) for p in pvec],
                                _state_to_scan(state_rwkv_S[j]), LANES, 1)
            tok_s = _from_scan_layout(ys, bs, ts)
            p_rw_S.append(sp[:, :, :RWKV_HEADS * bp].reshape(RWKV_N, RWKV_N, RWKV_HEADS, bp).transpose(3, 2, 0, 1))
            s_rw_S.append(_state_from_scan(ss, bs))
            p_rw_shift.append(shift_p)
            s_rw_shift.append(h_s)
            layout_p, layout_s = "rwkv_pairs", "rows"
        else:
            ws = _gla_weights(w_in[i], gla_a1[j], gla_a2[j], gla_ab[j])
            q_p, k_p, v_p, gate_p, la_p, qm_p = _gla_proj(h_p, ws, PROJ_BLOCK)
            q_s, k_s, v_s, gate_s, la_s, qm_s = _gla_proj(h_s, ws, SAMPLE_BLOCK)
            tok_p, sp_t = _gla_chunk_scan(q_p, k_p, v_p, la_p, gla_norm_g[j], bp, tp)
            rows_of = lambda t_: t_.transpose(1, 0, 2).reshape(bs * nh, t_.shape[2])
            os_, ss = _gla_step(rows_of(q_s), rows_of(k_s), rows_of(v_s), rows_of(la_s),
                                gla_norm_g[j], state_gla_S[j].reshape(bs * nh, GLA_DK, GLA_DV))
            tok_s = os_.reshape(bs, nh, GLA_DV).transpose(1, 0, 2)
            p_gla_S.append(sp_t.transpose(0, 1, 3, 2))
            s_gla_S.append(ss.reshape(bs, nh, GLA_DK, GLA_DV))
            layout_p = layout_s = "gla_heads"

        att_p = _mem_attn(qm_p.reshape(bp, tp, MEM_WIDTH), pk, pv, i, 512).reshape(np_, MEM_WIDTH)
        att_s = _mem_attn(qm_s.reshape(bs, ts, MEM_WIDTH), sk, sv, i, 1).reshape(ns, MEM_WIDTH)
        wr_hi = bf(w_router[i])
        wr_lo = bf(w_router[i] - wr_hi.astype(F32))
        ws = [bf(w_out[i]), norm_ffn_g[i].reshape(1, D_MODEL), wr_hi, wr_lo, b_router[i]]
        x1_p, h2, logits_p = _out_proj(tok_p, gate_p, att_p, x_p, ws, layout_p, PROJ_BLOCK, m, 0)
        x1_s, h2, logits_s = _out_proj(tok_s, gate_s, att_s, x_s, ws, layout_s, SAMPLE_BLOCK, m, np_, h2_buffer=h2)
        tok_sorted, gate_sorted, inv, item_tile, item_expert, n_items, lo, hi = _route(
            jnp.concatenate([logits_p, logits_s], axis=0), np_, MOE_BLOCK)
        rows = _moe_ffn(h2[tok_sorted], gate_sorted, item_tile, item_expert, n_items, lo, hi,
                        exp_w_up, exp_w_down, i)
        gathered = rows[inv]
        g_next = norm_mix_g[i + 1] if i + 1 < depth else norm_final_g
        x_p, h_p = _combine(x1_p, gathered, 0, g_next, LIGHT_BLOCK, i + 1 == depth)
        x_s, h_s = _combine(x1_s, gathered, 2 * np_, g_next, SAMPLE_BLOCK, i + 1 == depth)

    y_prompt = h_p.reshape(bp, tp, D_MODEL)
    y_sample = h_s.reshape(bs, ts, D_MODEL)
    return (y_prompt, y_sample, jnp.stack(p_rw_S), jnp.stack(p_rw_shift), jnp.stack(p_gla_S),
            prompt_mem_k, prompt_mem_v, jnp.stack(s_rw_S), jnp.stack(s_rw_shift), jnp.stack(s_gla_S))
```

```python
import functools

import numpy as np
import jax
import jax.numpy as jnp
from jax import lax
from jax.experimental import pallas as pl
from jax.experimental.pallas import tpu as pltpu

F32 = jnp.float32
BF16 = jnp.bfloat16

D_MODEL = 1024
TOK_WIDTH = 768
MEM_WIDTH = 256
MEM_HEADS = 4
MEM_HEAD_DIM = 64
N_MEM = 256
RWKV_HEADS = 12
RWKV_N = 64
RWKV_PAIRS = RWKV_HEADS // 2
RWKV_GN_EPS = 64e-5
GLA_HEADS = 4
GLA_KW = 384
GLA_DK = 96
GLA_DV = 192
GLA_TAU = 16.0
GLA_CHUNK = 64
GLA_TILE = 8
N_GROUPS = 4
EXPERTS_PER_GROUP = 8
N_EXPERTS = 32
EXPERT_FF = 512
NORM_EPS = 1e-6
ROUTER_LANES = 128
LANES = 128
SUBLANES = 8

PROJ_BLOCK = 512
LIGHT_BLOCK = 512
SAMPLE_BLOCK = 128
MOE_BLOCK = 256
SCAN_TIME_BLOCK = 32
VMEM_LIMIT = 56 * 1024 * 1024


def _cparams(*sem):
    return pltpu.CompilerParams(dimension_semantics=sem, vmem_limit_bytes=VMEM_LIMIT)


def _dot(a, b):
    return jnp.dot(a, b, preferred_element_type=F32)


def _dot_nt(a, b):
    return lax.dot_general(a, b, (((1,), (1,)), ((), ())), preferred_element_type=F32)


def _dot_tn(a, b):
    return lax.dot_general(a, b, (((0,), (0,)), ((), ())), preferred_element_type=F32)


def _rms(x, g):
    return x * lax.rsqrt(jnp.mean(x * x, axis=-1, keepdims=True) + NORM_EPS) * g


def _sigmoid(x):
    return 1.0 / (1.0 + jnp.exp(-x))


def _row_spec(tm, n, offset=0):
    return pl.BlockSpec((tm, n), lambda i: (i + offset, 0))


def _head_row_spec(tm, n):
    return pl.BlockSpec((GLA_HEADS, tm, n), lambda i: (0, i, 0))


def _full_spec(shape):
    nd = len(shape)
    return pl.BlockSpec(shape, lambda *_: (0,) * nd)


def _norm_kernel(x_ref, g_ref, o_ref):
    o_ref[...] = _rms(x_ref[...], g_ref[...])


def _norm(x, g, tm):
    m, d = x.shape
    return pl.pallas_call(
        _norm_kernel, grid=(m // tm,),
        in_specs=[_row_spec(tm, d), _full_spec((1, d))],
        out_specs=_row_spec(tm, d),
        out_shape=jax.ShapeDtypeStruct((m, d), F32),
        compiler_params=_cparams("parallel"), name="rms_norm")(x, g.reshape(1, d))


def _norm_matmul_kernel(x_ref, g_ref, w_ref, o_ref):
    o_ref[...] = _dot(_rms(x_ref[...], g_ref[...]).astype(BF16), w_ref[...])


def _norm_matmul(x, g, w, tm):
    m, d = x.shape
    n = w.shape[1]
    return pl.pallas_call(
        _norm_matmul_kernel, grid=(m // tm,),
        in_specs=[_row_spec(tm, d), _full_spec((1, d)), _full_spec((d, n))],
        out_specs=_row_spec(tm, n),
        out_shape=jax.ShapeDtypeStruct((m, n), F32),
        compiler_params=_cparams("parallel"), name="norm_matmul")(x, g.reshape(1, d), w.astype(BF16))


def _rwkv_proj_kernel(h_ref, hp_ref, *refs, blocks_per_seq, norm_input):
    if norm_input:
        ng_ref, refs = refs[0], refs[1:]
        normed = lambda t_: _rms(t_, ng_ref[...])
    else:
        normed = lambda t_: t_
    (mu_ref, wr_ref, wk_ref, wv_ref, wq_ref, w1_ref, w2_ref, w0_ref, a1_ref, a2_ref, a0_ref, g1_ref, g2_ref,
     r_out, k_out, v_out, d_out, a_out, g_out, q_out) = refs

    def put(out, val):
        if blocks_per_seq:
            for p in range(RWKV_PAIRS):
                out[p] = val[:, p * LANES:(p + 1) * LANES]
        else:
            out[...] = val

    h = normed(h_ref[...])
    if blocks_per_seq:
        seq_start = (pl.program_id(0) % blocks_per_seq) == 0
        before = jnp.where(seq_start, 0.0, normed(hp_ref[SUBLANES - 1:SUBLANES, :]))
        row = lax.broadcasted_iota(jnp.int32, (h.shape[0], 1), 0)
        hp = jnp.where(row == 0, before, pltpu.roll(h, 1, axis=0))
    else:
        hp = hp_ref[...]
    xx = hp - h

    def mix(j):
        return (h + xx * mu_ref[j:j + 1, :]).astype(BF16)

    put(r_out, _dot(mix(0), wr_ref[...]))
    wl = w0_ref[...] + _dot(jnp.tanh(_dot(mix(1), w1_ref[...])).astype(BF16), w2_ref[...])
    put(k_out, _dot(mix(2), wk_ref[...]))
    put(v_out, _dot(mix(3), wv_ref[...]))
    al = a0_ref[...] + _dot(_dot(mix(4), a1_ref[...]).astype(BF16), a2_ref[...])
    g_out[...] = _dot(_sigmoid(_dot(mix(5), g1_ref[...])).astype(BF16), g2_ref[...])
    q_out[...] = _dot(h.astype(BF16), wq_ref[...])
    z = -wl
    softplus = jnp.maximum(z, 0.0) + jnp.log(1.0 + jnp.exp(-jnp.abs(z)))
    put(d_out, jnp.exp(-jnp.exp(-softplus - 0.5)))
    put(a_out, _sigmoid(al))


def _rwkv_proj(h, h_prev, ws, tm, seq_len, norm_g=None):
    m = h.shape[0]
    if norm_g is not None:
        assert h_prev is None
        ws = [norm_g.reshape(1, D_MODEL)] + list(ws)
    tw = TOK_WIDTH
    if h_prev is None:
        assert seq_len % tm == 0 and tm % SUBLANES == 0
        per8 = tm // SUBLANES
        hp_spec = pl.BlockSpec((SUBLANES, D_MODEL), lambda i: (jnp.maximum(i * per8 - 1, 0), 0))
        h_prev, blocks_per_seq = h, seq_len // tm
    else:
        hp_spec, blocks_per_seq = _row_spec(tm, D_MODEL), 0
    if blocks_per_seq:
        scan_spec = pl.BlockSpec((RWKV_PAIRS, tm, LANES), lambda i: (0, i, 0))
        scan_shape = jax.ShapeDtypeStruct((RWKV_PAIRS, m, LANES), F32)
    else:
        scan_spec, scan_shape = _row_spec(tm, tw), jax.ShapeDtypeStruct((m, tw), F32)
    return pl.pallas_call(
        functools.partial(_rwkv_proj_kernel, blocks_per_seq=blocks_per_seq, norm_input=norm_g is not None),
        grid=(m // tm,),
        in_specs=[_row_spec(tm, D_MODEL), hp_spec] + [_full_spec(w.shape) for w in ws],
        out_specs=[scan_spec] * 5 + [_row_spec(tm, tw), _row_spec(tm, MEM_WIDTH)],
        out_shape=[scan_shape] * 5 + [jax.ShapeDtypeStruct((m, tw), F32), jax.ShapeDtypeStruct((m, MEM_WIDTH), F32)],
        compiler_params=_cparams("parallel"), name="rwkv_proj")(h, h_prev, *ws)


def _pad_lanes(x):
    short = LANES - x.shape[-1]
    if short == 0:
        return x
    return jnp.concatenate([x, jnp.zeros(x.shape[:-1] + (short,), x.dtype)], axis=-1)


def _rwkv_scan_kernel(r_ref, k_ref, v_ref, d_ref, a_ref, kkp_ref, kap_ref, rkp_ref, lng_ref, lnb_ref, s0_ref,
                      y_ref, sfin_ref, s_scr, v_scr, yrow_scr, *, tc):
    n = RWKV_N
    nl = r_ref.shape[-1]

    @pl.when(pl.program_id(1) == 0)
    def _():
        s_scr[...] = _pad_lanes(s0_ref[...])

    kkp = _pad_lanes(kkp_ref[...])
    kap = _pad_lanes(kap_ref[...])
    rkp = _pad_lanes(rkp_ref[...])
    lng = _pad_lanes(lng_ref[...])
    lnb = _pad_lanes(lnb_ref[...])

    def step(t, carry):
        r_t = _pad_lanes(r_ref[t])
        k_t = _pad_lanes(k_ref[t])
        v_t = _pad_lanes(v_ref[t])
        d_t = _pad_lanes(d_ref[t])
        a_t = _pad_lanes(a_ref[t])
        v_scr[...] = v_t
        kkr = k_t * kkp
        nrm = jnp.maximum(jnp.sqrt(jnp.sum(kkr * kkr, axis=0, keepdims=True)), 1e-12)
        kk = kkr * (1.0 / nrm)
        k2 = k_t * (1.0 + (a_t - 1.0) * kap)
        nkk = -kk
        b_t = kk * a_t

        def ibody(i, c):
            s_i = s_scr[i]
            sa = jnp.sum(s_i * nkk, axis=0, keepdims=True)
            v_i = v_scr[pl.ds(i, 1), :]
            s_n = s_i * d_t + sa * b_t + v_i * k2
            s_scr[i] = s_n
            yrow_scr[pl.ds(i, 1), :] = jnp.sum(s_n * r_t, axis=0, keepdims=True)
            return c

        lax.fori_loop(0, n, ibody, 0, unroll=8)
        y = yrow_scr[...]
        yc = y - jnp.mean(y, axis=0, keepdims=True)
        var = jnp.mean(yc * yc, axis=0, keepdims=True)
        gn = yc * lax.rsqrt(var + RWKV_GN_EPS) * lng + lnb
        bonus = jnp.sum(r_t * k2 * rkp, axis=0, keepdims=True) * v_t
        y_ref[t] = (gn + bonus)[:, :nl]
        return carry

    lax.fori_loop(0, tc, step, 0)

    @pl.when(pl.program_id(1) == pl.num_programs(1) - 1)
    def _():
        sfin_ref[...] = s_scr[:, :, :nl]


def _rwkv_scan(r, k, v, d, a, params, s0, lane_block, tc):
    t, n, l = r.shape
    seq = pl.BlockSpec((tc, n, lane_block), lambda li, ti: (ti, 0, li))
    par = pl.BlockSpec((n, lane_block), lambda li, ti: (0, li))
    st = pl.BlockSpec((n, n, lane_block), lambda li, ti: (0, 0, li))
    return pl.pallas_call(
        functools.partial(_rwkv_scan_kernel, tc=tc), grid=(l // lane_block, t // tc),
        in_specs=[seq] * 5 + [par] * 5 + [st],
        out_specs=[seq, st],
        out_shape=[jax.ShapeDtypeStruct((t, n, l), F32), jax.ShapeDtypeStruct((n, n, l), F32)],
        scratch_shapes=[pltpu.VMEM((n, n, LANES), F32), pltpu.VMEM((n, LANES), F32), pltpu.VMEM((n, LANES), F32)],
        compiler_params=_cparams("parallel", "arbitrary"), name="rwkv_scan")(r, k, v, d, a, *params, s0)


PREP_TILES = 6
DECAY_WINDOW = 16


def _rwkv_prep_kernel(r_ref, k_ref, v_ref, d_ref, a_ref, kkp_ref, kap_ref, o_ref, *, tc, nb):
    n = RWKV_N
    rows_per_pair = nb * tc
    ins = [ref.reshape(RWKV_PAIRS * rows_per_pair, LANES) for ref in (r_ref, k_ref, v_ref, d_ref, a_ref)]
    zero_rows = jnp.zeros((LANES - RWKV_HEADS * nb, LANES), F32)
    first_half = lax.broadcasted_iota(jnp.int32, (nb, LANES), 1) < n
    kkp = kkp_ref[...]
    kap = kap_ref[...]

    def load_transposed_pair(x2, t):
        pieces = []
        for p in range(RWKV_PAIRS):
            now = x2[pl.ds(p * rows_per_pair + t, nb, stride=tc), :]
            nxt = x2[pl.ds(p * rows_per_pair + t + 1, nb, stride=tc), :]
            pieces += [jnp.where(first_half, now, pltpu.roll(nxt, n, axis=1)),
                       jnp.where(first_half, pltpu.roll(now, n, axis=1), nxt)]
        both = jnp.concatenate(pieces + [zero_rows], axis=0).T
        return both[:n, :], both[n:, :]

    def emit(t, c_prev, r_t, k_t, v_t, d_t, a_t):
        kkr = k_t * kkp
        nrm = jnp.maximum(jnp.sqrt(jnp.sum(kkr * kkr, axis=0, keepdims=True)), 1e-12)
        kk = kkr * (1.0 / nrm)
        k2 = k_t * (1.0 + (a_t - 1.0) * kap)
        c_prev = jnp.where(t % DECAY_WINDOW == 0, 1.0, c_prev)
        c_t = c_prev * d_t
        inv = 1.0 / jnp.maximum(c_t, 1e-30)
        o_ref[t, 0] = -kk * c_prev
        o_ref[t, 1] = c_t
        o_ref[t, 2] = kk * a_t * inv
        o_ref[t, 3] = k2 * inv
        o_ref[t, 4] = r_t * c_t
        o_ref[t, 5] = v_t
        return c_t

    def two_steps(u, c):
        t = 2 * u
        tiles = [load_transposed_pair(x2, t) for x2 in ins]
        c = emit(t, c, *[tile[0] for tile in tiles])
        return emit(t + 1, c, *[tile[1] for tile in tiles])

    lax.fori_loop(0, tc // 2, two_steps, jnp.ones((n, LANES), F32), unroll=2)


def _rwkv_state_scan_kernel(x_ref, rkp_ref, lng_ref, lnb_ref, y_ref, sfin_ref, s_scr, sa_scr, yrow_scr, *, tc):
    n = RWKV_N
    groups = n // SUBLANES

    @pl.when(pl.program_id(0) == 0)
    def _():
        s_scr[...] = jnp.zeros(s_scr.shape, F32)

    rkp = rkp_ref[...]
    lng = lng_ref[...]
    lnb = lnb_ref[...]
    sub = lax.broadcasted_iota(jnp.int32, (SUBLANES, LANES), 0)
    low4 = sub < 4
    low2 = (sub & 3) < 2
    low1 = (sub & 1) == 0

    def fold(x, y, dist, low):
        if dist == 4:
            return jnp.where(low, x, y) + pltpu.roll(jnp.where(low, y, x), 4, axis=0)
        return (jnp.where(low, x, pltpu.roll(y, dist, axis=0))
                + jnp.where(low, pltpu.roll(x, SUBLANES - dist, axis=0), y))

    def sublane_sums(ps):
        z = [fold(ps[0], ps[4], 4, low4), fold(ps[2], ps[6], 4, low4),
             fold(ps[1], ps[5], 4, low4), fold(ps[3], ps[7], 4, low4)]
        return fold(fold(z[0], z[1], 2, low2), fold(z[2], z[3], 2, low2), 1, low1)

    def tile_sum(x):
        acc = x[0:SUBLANES]
        for u in range(1, groups):
            acc = acc + x[u * SUBLANES:(u + 1) * SUBLANES]
        return acc

    def step(t, carry):
        nkk = x_ref[t, 0]
        b_t = x_ref[t, 2]
        k2 = x_ref[t, 3]
        r_t = x_ref[t, 4]
        for g in range(groups):
            sa_scr[g * SUBLANES:(g + 1) * SUBLANES, :] = sublane_sums(
                [tile_sum(s_scr[g * SUBLANES + u] * nkk) for u in range(SUBLANES)])
        for g in range(groups):
            ps = []
            for u in range(SUBLANES):
                i = g * SUBLANES + u
                s_n = s_scr[i] + sa_scr[pl.ds(i, 1), :] * b_t + x_ref[t, 5, pl.ds(i, 1), :] * k2
                s_scr[i] = s_n
                ps.append(tile_sum(s_n * r_t))
            yrow_scr[g * SUBLANES:(g + 1) * SUBLANES, :] = sublane_sums(ps)
        y = yrow_scr[...]
        yc = y - jnp.mean(y, axis=0, keepdims=True)
        var = jnp.mean(yc * yc, axis=0, keepdims=True)
        bonus = jnp.sum(r_t * k2 * rkp, axis=0, keepdims=True) * x_ref[t, 5]
        y_ref[t] = yc * lax.rsqrt(var + RWKV_GN_EPS) * lng + lnb + bonus

        @pl.when(t % DECAY_WINDOW == DECAY_WINDOW - 1)
        def _():
            c_t = x_ref[t, 1]
            for i in range(n):
                s_scr[i] = s_scr[i] * c_t

        return carry

    lax.fori_loop(0, tc, step, 0)

    @pl.when(pl.program_id(0) == pl.num_programs(0) - 1)
    def _():
        sfin_ref[...] = s_scr[...]


def _rwkv_unprep_kernel(y_ref, o_ref, *, tc, nb):
    n = RWKV_N
    rows_per_pair = nb * tc
    o2 = o_ref.reshape(RWKV_PAIRS * rows_per_pair, LANES)
    first_half = lax.broadcasted_iota(jnp.int32, (nb, LANES), 1) < n

    def two_steps(u, carry):
        t = 2 * u
        w = jnp.concatenate([y_ref[t], y_ref[t + 1]], axis=0).T
        for p in range(RWKV_PAIRS):
            even = w[(2 * p) * nb:(2 * p + 1) * nb, :]
            odd = w[(2 * p + 1) * nb:(2 * p + 2) * nb, :]
            o2[pl.ds(p * rows_per_pair + t, nb, stride=tc), :] = jnp.where(
                first_half, even, pltpu.roll(odd, n, axis=1))
            o2[pl.ds(p * rows_per_pair + t + 1, nb, stride=tc), :] = jnp.where(
                first_half, pltpu.roll(even, n, axis=1), odd)
        return carry

    lax.fori_loop(0, tc // 2, two_steps, 0, unroll=2)


def _rwkv_prompt_mixer(r, k, v, d, a, kkp, kap, rkp, lng, lnb, nb, t, tc):
    n = RWKV_N
    assert tc % SUBLANES == 0 and tc % 4 == 0 and t % tc == 0 and RWKV_HEADS * nb <= LANES
    assert tc % DECAY_WINDOW == 0
    grid = (t // tc,)
    tok = pl.BlockSpec((RWKV_PAIRS, nb, tc, LANES), lambda ti: (0, 0, ti, 0))
    par = pl.BlockSpec((n, LANES), lambda ti: (0, 0))
    tiles = pl.BlockSpec((tc, PREP_TILES, n, LANES), lambda ti: (ti, 0, 0, 0))
    ytile = pl.BlockSpec((tc, n, LANES), lambda ti: (ti, 0, 0))
    prepared = pl.pallas_call(
        functools.partial(_rwkv_prep_kernel, tc=tc, nb=nb), grid=grid,
        in_specs=[tok] * 5 + [par] * 2, out_specs=tiles,
        out_shape=jax.ShapeDtypeStruct((t, PREP_TILES, n, LANES), F32),
        compiler_params=_cparams("parallel"), name="rwkv_prep")(
            *[x.reshape(RWKV_PAIRS, nb, t, LANES) for x in (r, k, v, d, a)], kkp, kap)
    y, s = pl.pallas_call(
        functools.partial(_rwkv_state_scan_kernel, tc=tc), grid=grid,
        in_specs=[tiles, par, par, par],
        out_specs=[ytile, pl.BlockSpec((n, n, LANES), lambda ti: (0, 0, 0))],
        out_shape=[jax.ShapeDtypeStruct((t, n, LANES), F32), jax.ShapeDtypeStruct((n, n, LANES), F32)],
        scratch_shapes=[pltpu.VMEM((n, n, LANES), F32), pltpu.VMEM((n, LANES), F32), pltpu.VMEM((n, LANES), F32)],
        compiler_params=_cparams("arbitrary"), name="rwkv_state_scan")(prepared, rkp, lng, lnb)
    tok_out = pl.pallas_call(
        functools.partial(_rwkv_unprep_kernel, tc=tc, nb=nb), grid=grid,
        in_specs=[ytile], out_specs=tok,
        out_shape=jax.ShapeDtypeStruct((RWKV_PAIRS, nb, t, LANES), F32),
        compiler_params=_cparams("parallel"), name="rwkv_unprep")(y)
    return tok_out.reshape(RWKV_PAIRS, nb * t, LANES), s


def _gla_proj_kernel(*refs, fused_combine):
    if fused_combine:
        x_ref, r0_ref, r1_ref, g_ref = refs[:4]
        refs, x_out = refs[4:-1], refs[-1]
        x2 = x_ref[...] + (r0_ref[...] + r1_ref[...])
        x_out[...] = x2
        h = _rms(x2, g_ref[...])
    else:
        h, refs = refs[0][...], refs[1:]
    (wq_ref, wk_ref, wv_ref, wr_ref, wm_ref, a1_ref, a2_ref, ab_ref,
     q_out, k_out, v_out, r_out, la_out, qm_out) = refs
    hb = h.astype(BF16)
    low = _dot(hb, a1_ref[...]).astype(BF16)
    for hd in range(GLA_HEADS):
        q_out[hd] = _dot(hb, wq_ref[hd])
        k_out[hd] = _dot(hb, wk_ref[hd])
        v_out[hd] = _dot(hb, wv_ref[hd])
        r_out[hd] = _dot(hb, wr_ref[hd])
        x = _dot(low, a2_ref[hd]) + ab_ref[hd]
        log_sigmoid = jnp.minimum(x, 0.0) - jnp.log(1.0 + jnp.exp(-jnp.abs(x)))
        la_out[hd] = log_sigmoid / GLA_TAU
    qm_out[...] = _dot(hb, wm_ref[...])


def _gla_weights(w_in, a1, a2, ab):
    kw, tw, nh = GLA_KW, TOK_WIDTH, GLA_HEADS

    def heads(w, d):
        return w.reshape(w.shape[0], nh, d).transpose(1, 0, 2)

    return [heads(w_in[:, :kw], GLA_DK).astype(BF16), heads(w_in[:, kw:2 * kw], GLA_DK).astype(BF16),
            heads(w_in[:, 2 * kw:2 * kw + tw], GLA_DV).astype(BF16),
            heads(w_in[:, 2 * kw + tw:3 * tw], GLA_DV).astype(BF16), w_in[:, 3 * tw:].astype(BF16),
            a1.astype(BF16), heads(a2, GLA_DK).astype(BF16), ab.reshape(nh, 1, GLA_DK)]


def _gla_proj(h, ws, tm, pending=None):
    nh = GLA_HEADS
    widths = [GLA_DK, GLA_DK, GLA_DV, GLA_DV, GLA_DK]
    if pending is None:
        m = h.shape[0]
        lead_specs, lead_args = [_row_spec(tm, D_MODEL)], [h]
    else:
        x1, gathered, first_row, norm_g = pending
        m = x1.shape[0]
        lead_specs = [_row_spec(tm, D_MODEL), _row_spec(tm, D_MODEL, first_row // tm),
                      _row_spec(tm, D_MODEL, (first_row + m) // tm), _full_spec((1, D_MODEL))]
        lead_args = [x1, gathered, gathered, norm_g.reshape(1, D_MODEL)]
    out_specs = [_head_row_spec(tm, w) for w in widths] + [_row_spec(tm, MEM_WIDTH)]
    out_shape = [jax.ShapeDtypeStruct((nh, m, w), F32) for w in widths] + [jax.ShapeDtypeStruct((m, MEM_WIDTH), F32)]
    if pending is not None:
        out_specs.append(_row_spec(tm, D_MODEL))
        out_shape.append(jax.ShapeDtypeStruct((m, D_MODEL), F32))
    return pl.pallas_call(
        functools.partial(_gla_proj_kernel, fused_combine=pending is not None), grid=(m // tm,),
        in_specs=lead_specs + [_full_spec(w.shape) for w in ws],
        out_specs=out_specs, out_shape=out_shape,
        compiler_params=_cparams("parallel"), name="gla_proj")(*lead_args, *ws)


def _gla_out_norm(o, g):
    return o * lax.rsqrt(jnp.mean(o * o, axis=-1, keepdims=True) + NORM_EPS) * g


def _gla_chunk_kernel(q_ref, k_ref, v_ref, la_ref, ng_ref, o_ref, sfin_ref, st_scr):
    c, dk = GLA_CHUNK, GLA_DK

    @pl.when(pl.program_id(1) == 0)
    def _():
        st_scr[...] = jnp.zeros(st_scr.shape, F32)

    row = lax.broadcasted_iota(jnp.int32, (c, c), 0)
    col = lax.broadcasted_iota(jnp.int32, (c, c), 1)
    tril = (row >= col).astype(F32)
    rr = lax.broadcasted_iota(jnp.int32, (c, dk), 0)
    ones_sum = jnp.ones((dk, LANES), BF16)

    heads = range(GLA_HEADS)
    tril_b = tril.astype(BF16)

    def cumsum_rows(la):
        hi = la.astype(BF16)
        r1 = la - hi.astype(F32)
        mid = r1.astype(BF16)
        lo = (r1 - mid.astype(F32)).astype(BF16)
        return _dot(tril_b, hi) + _dot(tril_b, mid) + _dot(tril_b, lo)

    def tile_roll(x, dlt):
        return pltpu.roll(x.reshape(c // GLA_TILE, GLA_TILE, dk), dlt, axis=1).reshape(c, dk)

    k = [k_ref[hd] for hd in heads]
    vb = [v_ref[hd].astype(BF16) for hd in heads]
    q = [q_ref[hd] * (dk ** -0.5) for hd in heads]
    b = [cumsum_rows(la_ref[hd]) for hd in heads]
    st = [st_scr[hd] for hd in heads]
    inter = [_dot_nt((q[hd] * jnp.exp(b[hd])).astype(BF16), st[hd].astype(BF16)) for hd in heads]

    att = [jnp.zeros((c, c), F32) for hd in heads]
    blk = c // 2
    while blk >= GLA_TILE:
        two = 2 * blk
        upper = (rr & (two - 1)) >= blk
        same_block = (row ^ col) < two
        parts = []
        for hd in heads:
            b_ref_rows = jnp.concatenate(
                [jnp.broadcast_to(b[hd][s0 + blk - 1:s0 + blk, :], (two, dk)) for s0 in range(0, c, two)], axis=0)
            q_l = jnp.where(upper, q[hd] * jnp.exp(jnp.minimum(b[hd] - b_ref_rows, 0.0)), 0.0).astype(BF16)
            k_l = jnp.where(upper, 0.0, k[hd] * jnp.exp(jnp.minimum(b_ref_rows - b[hd], 0.0))).astype(BF16)
            parts.append(_dot_nt(q_l, k_l))
        att = [att[hd] + jnp.where(same_block, parts[hd], 0.0) for hd in heads]
        blk //= 2

    sums = []
    for hd in heads:
        prods = [(q[hd] * k[hd]).astype(BF16)]
        for dlt in range(1, GLA_TILE):
            p = q[hd] * tile_roll(k[hd], dlt) * jnp.exp(jnp.minimum(b[hd] - tile_roll(b[hd], dlt), 0.0))
            prods.append(jnp.where((rr & (GLA_TILE - 1)) >= dlt, p, 0.0).astype(BF16))
        sums.append(_dot(jnp.concatenate(prods, axis=0), ones_sum))
    for dlt in range(GLA_TILE):
        on_diag = col == row - dlt
        att = [att[hd] + jnp.where(on_diag, sums[hd][dlt * c:(dlt + 1) * c, :c], 0.0) for hd in heads]

    o = [inter[hd] + _dot(att[hd].astype(BF16), vb[hd]) for hd in heads]
    for hd in heads:
        o_ref[hd] = _gla_out_norm(o[hd], ng_ref[...])
    for hd in heads:
        b_end = b[hd][c - 1:c, :]
        kd = (k[hd] * jnp.exp(b_end - b[hd])).astype(BF16)
        st_scr[hd] = jnp.exp(b_end) * st[hd] + _dot_tn(vb[hd], kd)

    @pl.when(pl.program_id(1) == pl.num_programs(1) - 1)
    def _():
        sfin_ref[0] = st_scr[...]


def _gla_chunk_scan(q, k, v, la, norm_g, batch, t):
    nh, dk, dv, c = GLA_HEADS, GLA_DK, GLA_DV, GLA_CHUNK
    nc = t // c
    kspec = pl.BlockSpec((nh, c, dk), lambda i, j: (0, i * nc + j, 0))
    vspec = pl.BlockSpec((nh, c, dv), lambda i, j: (0, i * nc + j, 0))
    sspec = pl.BlockSpec((1, nh, dv, dk), lambda i, j: (i, 0, 0, 0))
    return pl.pallas_call(
        _gla_chunk_kernel, grid=(batch, nc),
        in_specs=[kspec, kspec, vspec, kspec, pl.BlockSpec((1, dv), lambda i, j: (0, 0))],
        out_specs=[vspec, sspec],
        out_shape=[jax.ShapeDtypeStruct((nh, batch * t, dv), F32), jax.ShapeDtypeStruct((batch, nh, dv, dk), F32)],
        scratch_shapes=[pltpu.VMEM((nh, dv, dk), F32)],
        compiler_params=_cparams("parallel", "arbitrary"), name="gla_chunk")(q, k, v, la, norm_g.reshape(1, dv))


GLA_STEP_GROUP = 8


def _gla_step_kernel(q_ref, k_ref, v_ref, la_ref, ng_ref, s0_ref, o_ref, s_ref):
    g_n, dk = GLA_STEP_GROUP, GLA_DK

    def columns(ref):
        x = jnp.concatenate([ref[...], jnp.zeros((g_n, LANES - dk), F32)], axis=1)
        return jnp.concatenate([x, jnp.zeros((LANES - g_n, LANES), F32)], axis=0).T

    q_cols, k_cols, la_cols = columns(q_ref), columns(k_ref), columns(la_ref)
    v = v_ref[...]
    rows = []
    for g in range(g_n):
        kv = k_cols[:dk, g:g + 1].astype(BF16).astype(F32) * v[g:g + 1, :].astype(BF16).astype(F32)
        s_new = jnp.exp(la_cols[:dk, g:g + 1]) * s0_ref[g] + kv
        s_ref[g] = s_new
        o = jnp.sum((q_cols[:dk, g:g + 1] * (dk ** -0.5)) * s_new, axis=0, keepdims=True)
        rows.append(_gla_out_norm(o, ng_ref[...]))
    o_ref[...] = jnp.concatenate(rows, axis=0)


def _gla_step(q, k, v, la, norm_g, s0):
    bh, dk = q.shape
    dv = v.shape[1]
    g = GLA_STEP_GROUP
    kspec = _row_spec(g, dk)
    vspec = _row_spec(g, dv)
    sspec = pl.BlockSpec((g, dk, dv), lambda i: (i, 0, 0))
    return pl.pallas_call(
        _gla_step_kernel, grid=(bh // g,),
        in_specs=[kspec, kspec, vspec, kspec, _full_spec((1, dv)), sspec],
        out_specs=[vspec, sspec],
        out_shape=[jax.ShapeDtypeStruct((bh, dv), F32), jax.ShapeDtypeStruct((bh, dk, dv), F32)],
        compiler_params=_cparams("parallel"), name="gla_step")(q, k, v, la, norm_g.reshape(1, dv), s0)


def _mem_attn_kernel(q_ref, k_ref, v_ref, o_ref):
    q = q_ref[0]
    k = k_ref[0].astype(BF16)
    v_ones = jnp.concatenate([v_ref[0].astype(BF16), jnp.ones((N_MEM, LANES), BF16)], axis=1)
    head_of_lane = lax.broadcasted_iota(jnp.int32, (1, MEM_WIDTH), 1) // MEM_HEAD_DIM
    heads = range(MEM_HEADS)
    mine = [head_of_lane == h for h in heads]
    qh = [jnp.where(mine[h], q, 0.0).astype(BF16) for h in heads]
    s = [_dot_nt(qh[h], k) * (MEM_HEAD_DIM ** -0.5) for h in heads]
    e = [jnp.exp(s[h] - jnp.max(s[h], axis=-1, keepdims=True)).astype(BF16) for h in heads]
    ev = [_dot(e[h], v_ones) for h in heads]
    out = jnp.zeros(q.shape, F32)
    for h in heads:
        inv = 1.0 / ev[h][:, MEM_WIDTH:]
        out = out + jnp.where(mine[h], ev[h][:, :MEM_WIDTH] * jnp.concatenate([inv, inv], axis=1), 0.0)
    o_ref[0] = out


def _mem_attn(q, mem_k, mem_v, layer, tq):
    b, t, w = q.shape
    qspec = pl.BlockSpec((1, tq, w), lambda i, j: (i, j, 0))
    mspec = pl.BlockSpec((None, 1, N_MEM, w), lambda i, j: (layer, i, 0, 0))
    return pl.pallas_call(
        _mem_attn_kernel, grid=(b, t // tq),
        in_specs=[qspec, mspec, mspec], out_specs=qspec,
        out_shape=jax.ShapeDtypeStruct((b, t, w), F32),
        compiler_params=_cparams("parallel", "parallel"), name="mem_attn")(q, mem_k, mem_v)


def _out_proj_kernel(*refs, layout, aliased):
    tok_ref, gate_ref, att_ref, x_ref, wo_ref, g_ref, wr_hi_ref, wr_lo_ref, br_ref = refs[:9]
    x1_out, h2_out, logit_out = refs[9 + aliased:]
    x1 = x_ref[...] + _dot(att_ref[...].astype(BF16), wo_ref[TOK_WIDTH:, :])
    if layout == "gla_heads":
        for hd in range(GLA_HEADS):
            gate = gate_ref[hd]
            mixed = (tok_ref[hd] * (gate * _sigmoid(gate))).astype(BF16)
            x1 = x1 + _dot(mixed, wo_ref[hd * GLA_DV:(hd + 1) * GLA_DV, :])
    else:
        if layout == "rwkv_pairs":
            tok = jnp.concatenate([tok_ref[p] for p in range(RWKV_PAIRS)], axis=1)
        else:
            tok = tok_ref[...]
        x1 = x1 + _dot((tok * gate_ref[...]).astype(BF16), wo_ref[:TOK_WIDTH, :])
    x1_out[...] = x1
    h2 = _rms(x1, g_ref[...])
    h2_out[...] = h2
    h_hi = h2.astype(BF16)
    h_lo = (h2 - h_hi.astype(F32)).astype(BF16)
    logit_out[...] = (_dot(h_hi, wr_hi_ref[...]) + _dot(h_lo, wr_hi_ref[...]) + _dot(h_hi, wr_lo_ref[...])
                      + br_ref[...])


def _out_proj(tok, gate, att, x, ws, layout, tm, h2_rows, h2_row_offset, h2_buffer=None):
    m = x.shape[0]
    aliased = h2_buffer is not None
    rows_spec = _row_spec(tm, TOK_WIDTH)
    tok_spec, gate_spec = {
        "rows": (rows_spec, rows_spec),
        "rwkv_pairs": (pl.BlockSpec((RWKV_PAIRS, tm, LANES), lambda i: (0, i, 0)), rows_spec),
        "gla_heads": (_head_row_spec(tm, GLA_DV), _head_row_spec(tm, GLA_DV))}[layout]
    in_specs = ([tok_spec, gate_spec, _row_spec(tm, MEM_WIDTH), _row_spec(tm, D_MODEL)]
                + [_full_spec(w.shape) for w in ws])
    args = [tok, gate, att, x, *ws]
    if aliased:
        in_specs.append(pl.BlockSpec(memory_space=pl.ANY))
        args.append(h2_buffer)
    return pl.pallas_call(
        functools.partial(_out_proj_kernel, layout=layout, aliased=int(aliased)), grid=(m // tm,),
        in_specs=in_specs,
        out_specs=[_row_spec(tm, D_MODEL), _row_spec(tm, D_MODEL, h2_row_offset // tm), _row_spec(tm, ROUTER_LANES)],
        out_shape=[jax.ShapeDtypeStruct((m, D_MODEL), F32), jax.ShapeDtypeStruct((h2_rows, D_MODEL), F32),
                   jax.ShapeDtypeStruct((m, ROUTER_LANES), F32)],
        input_output_aliases={len(args) - 1: 1} if aliased else {},
        compiler_params=_cparams("parallel"), name="out_proj")(*args)


def _moe_kernel(tile_ref, exp_ref, nitem_ref, lo_ref, hi_ref, x_ref, gate_ref, wu_ref, wd_ref, o_ref, wu_scr, wd_scr):
    w = pl.program_id(0)
    prev = jnp.maximum(w - 1, 0)
    e = exp_ref[w]
    valid = w < nitem_ref[0]

    @pl.when(jnp.logical_and(valid, jnp.logical_or(w == 0, e != exp_ref[prev])))
    def _():
        wu_scr[...] = wu_ref[0, 0].astype(BF16)
        wd_scr[...] = wd_ref[0, 0].astype(BF16)

    @pl.when(jnp.logical_or(w == 0, tile_ref[w] != tile_ref[prev]))
    def _():
        o_ref[...] = jnp.zeros(o_ref.shape, F32)

    @pl.when(valid)
    def _():
        gu = _dot(x_ref[...].astype(BF16), wu_scr[...])
        g = gu[:, :EXPERT_FF]
        act = (g * _sigmoid(g) * gu[:, EXPERT_FF:]).astype(BF16)
        out = _dot(act, wd_scr[...]) * gate_ref[...]
        rows = tile_ref[w] * MOE_BLOCK + lax.broadcasted_iota(jnp.int32, (MOE_BLOCK, 1), 0)
        mine = jnp.logical_and(rows >= lo_ref[e], rows < hi_ref[e])
        o_ref[...] = o_ref[...] + jnp.where(mine, out, 0.0)


def _moe_ffn(xs, row_gate, item_tile, item_expert, n_items, lo, hi, w_up, w_down, layer):
    tm = MOE_BLOCK
    n_work = item_tile.shape[0]
    a = xs.shape[0]
    row_map = lambda w, tile, ex, ni, lo_, hi_: (tile[w], 0)
    exp_map = lambda w, tile, ex, ni, lo_, hi_: (layer, ex[w], 0, 0)
    grid_spec = pltpu.PrefetchScalarGridSpec(
        num_scalar_prefetch=5, grid=(n_work,),
        in_specs=[pl.BlockSpec((tm, D_MODEL), row_map),
                  pl.BlockSpec((tm, 1), row_map),
                  pl.BlockSpec((1, 1, D_MODEL, 2 * EXPERT_FF), exp_map),
                  pl.BlockSpec((1, 1, EXPERT_FF, D_MODEL), exp_map)],
        out_specs=pl.BlockSpec((tm, D_MODEL), row_map),
        scratch_shapes=[pltpu.VMEM((D_MODEL, 2 * EXPERT_FF), BF16), pltpu.VMEM((EXPERT_FF, D_MODEL), BF16)])
    return pl.pallas_call(
        _moe_kernel, grid_spec=grid_spec,
        out_shape=jax.ShapeDtypeStruct((a, D_MODEL), F32),
        compiler_params=_cparams("arbitrary"), name="moe_ffn")(
            item_tile, item_expert, n_items, lo, hi, xs, row_gate, w_up, w_down)


def _route(logits, n_prompt, tm):
    m = logits.shape[0]
    n_sample = m - n_prompt
    a = 2 * m
    gl = logits[:, :N_GROUPS]
    el = logits[:, N_GROUPS:N_GROUPS + N_EXPERTS]
    group = jnp.argmax(gl, -1).astype(jnp.int32)
    p_group = jnp.max(jax.nn.softmax(gl, -1), -1, keepdims=True)
    experts_row = jnp.arange(N_EXPERTS, dtype=jnp.int32)[None, :]
    masked = jnp.where(experts_row // EXPERTS_PER_GROUP == group[:, None], el, -jnp.inf)
    e1 = jnp.argmax(masked, -1).astype(jnp.int32)
    masked2 = jnp.where(experts_row == e1[:, None], -jnp.inf, masked)
    e2 = jnp.argmax(masked2, -1).astype(jnp.int32)
    top_val = jnp.stack([jnp.max(masked, -1), jnp.max(masked2, -1)], axis=-1)
    gate = p_group * jax.nn.softmax(top_val, -1)
    expert = jnp.stack([e1, e2], axis=-1)

    def by_id(t):
        return jnp.concatenate([t[:n_prompt, 0], t[:n_prompt, 1], t[n_prompt:, 0], t[n_prompt:, 1]])

    tok_of_id = jnp.asarray(np.concatenate([np.arange(n_prompt), np.arange(n_prompt),
                                            n_prompt + np.arange(n_sample), n_prompt + np.arange(n_sample)]), jnp.int32)
    flat_e = by_id(expert)
    ids = jnp.arange(a, dtype=jnp.int32)
    _, order, gate_sorted, tok_sorted = lax.sort((flat_e, ids, by_id(gate), tok_of_id), num_keys=1, is_stable=True)
    _, inv = lax.sort((order, ids), num_keys=1)
    experts = jnp.arange(N_EXPERTS, dtype=jnp.int32)
    counts = jnp.sum((flat_e[:, None] == experts[None, :]).astype(jnp.int32), axis=0)
    hi = jnp.cumsum(counts).astype(jnp.int32)
    lo = hi - counts
    n_tiles = a // tm
    first_tile = lo // tm
    tiles_of = jnp.where(counts > 0, (hi - 1) // tm - first_tile + 1, 0)
    item_end = jnp.cumsum(tiles_of).astype(jnp.int32)
    n_items = item_end[-1:]
    n_work = n_tiles + N_EXPERTS - 1
    w = jnp.minimum(jnp.arange(n_work, dtype=jnp.int32), n_items[0] - 1)
    item_expert = jnp.sum((item_end[None, :] <= w[:, None]).astype(jnp.int32), axis=1)
    onehot = (item_expert[:, None] == experts[None, :]).astype(jnp.int32)
    item_tile = jnp.sum(onehot * (first_tile - (item_end - tiles_of))[None, :], axis=1) + w
    return tok_sorted, gate_sorted.reshape(a, 1), inv, item_tile.astype(jnp.int32), item_expert, n_items, lo, hi


def _combine_kernel(x_ref, r0_ref, r1_ref, g_ref, *outs):
    x2 = x_ref[...] + (r0_ref[...] + r1_ref[...])
    outs[-1][...] = _rms(x2, g_ref[...])
    if len(outs) == 2:
        outs[0][...] = x2


def _combine(x1, gathered, first_row, g, tm, last_layer):
    m = x1.shape[0]
    spec = _row_spec(tm, D_MODEL)
    n_out = 1 if last_layer else 2
    outs = pl.pallas_call(
        _combine_kernel, grid=(m // tm,),
        in_specs=[spec, _row_spec(tm, D_MODEL, first_row // tm), _row_spec(tm, D_MODEL, (first_row + m) // tm),
                  _full_spec((1, D_MODEL))],
        out_specs=[spec] * n_out,
        out_shape=[jax.ShapeDtypeStruct((m, D_MODEL), F32)] * n_out,
        compiler_params=_cparams("parallel"), name="moe_combine")(x1, gathered, gathered, g.reshape(1, D_MODEL))
    return (None, outs[0]) if last_layer else tuple(outs)


def _to_scan_layout(t2d, b, t):
    return t2d.reshape(b, t, RWKV_HEADS, RWKV_N).transpose(1, 3, 0, 2).reshape(t, RWKV_N, b * RWKV_HEADS)


def _from_scan_layout(y, b, t):
    return y.reshape(t, RWKV_N, b, RWKV_HEADS).transpose(2, 0, 3, 1).reshape(b * t, TOK_WIDTH)


def _scan_param(p, b):
    return jnp.tile(p.reshape(RWKV_HEADS, RWKV_N).T, (1, b))


def _seq_scan_param(p, b):
    lanes = jnp.repeat(p.reshape(RWKV_HEADS, RWKV_N).T, b, axis=1)
    return jnp.pad(lanes, ((0, 0), (0, LANES - RWKV_HEADS * b)))


def _state_to_scan(s):
    b = s.shape[0]
    return s.transpose(2, 3, 0, 1).reshape(RWKV_N, RWKV_N, b * RWKV_HEADS)


def _state_from_scan(s, b):
    return s.reshape(RWKV_N, RWKV_N, b, RWKV_HEADS).transpose(2, 3, 0, 1)


def kernel(x_prompt, x_sample, mem_prompt, state_rwkv_S, state_rwkv_shift, state_gla_S, cache_mem_k, cache_mem_v, norm_mix_g, norm_ffn_g, norm_mem_g, norm_final_g, w_in, w_out, w_mem_kv, rw_mu, rw_w0, rw_w1, rw_w2, rw_a0, rw_a1, rw_a2, rw_g1, rw_g2, rw_k_k, rw_k_a, rw_r_k, rw_ln_g, rw_ln_b, gla_a1, gla_a2, gla_ab, gla_norm_g, router_wg, router_bg, router_we, router_be, exp_w_up, exp_w_down):
    bp, tp, _ = x_prompt.shape
    bs, ts, _ = x_sample.shape
    assert ts == 1 and tp % GLA_CHUNK == 0 and tp % SCAN_TIME_BLOCK == 0
    np_ = bp * tp
    ns = bs * ts
    m = np_ + ns
    assert np_ % LIGHT_BLOCK == 0 and ns % SAMPLE_BLOCK == 0 and (2 * m) % MOE_BLOCK == 0
    depth = w_in.shape[0]
    nh = GLA_HEADS
    tw = TOK_WIDTH
    bf = lambda t_: t_.astype(BF16)

    mem2d = mem_prompt.reshape(bp * N_MEM, D_MODEL)
    mem_kv = [_norm_matmul(mem2d, norm_mem_g[i], w_mem_kv[i], 512) for i in range(depth)]
    pk = jnp.stack([kv[:, :MEM_WIDTH].reshape(bp, N_MEM, MEM_WIDTH) for kv in mem_kv])
    pv = jnp.stack([kv[:, MEM_WIDTH:].reshape(bp, N_MEM, MEM_WIDTH) for kv in mem_kv])
    prompt_mem_k = pk.reshape(depth, bp, N_MEM, MEM_HEADS, MEM_HEAD_DIM)
    prompt_mem_v = pv.reshape(depth, bp, N_MEM, MEM_HEADS, MEM_HEAD_DIM)
    sk = cache_mem_k.reshape(depth, bs, N_MEM, MEM_WIDTH)
    sv = cache_mem_v.reshape(depth, bs, N_MEM, MEM_WIDTH)

    x_p = x_prompt.reshape(np_, D_MODEL)
    x_s = x_sample.reshape(ns, D_MODEL)
    h_p = None
    h_s = _norm(x_s, norm_mix_g[0], SAMPLE_BLOCK)

    w_router = jnp.zeros((depth, D_MODEL, ROUTER_LANES), F32)
    w_router = w_router.at[:, :, :N_GROUPS].set(router_wg).at[:, :, N_GROUPS:N_GROUPS + N_EXPERTS].set(router_we)
    b_router = jnp.zeros((depth, 1, ROUTER_LANES), F32)
    b_router = b_router.at[:, 0, :N_GROUPS].set(router_bg).at[:, 0, N_GROUPS:N_GROUPS + N_EXPERTS].set(router_be)

    p_rw_S, p_rw_shift, p_gla_S, s_rw_S, s_rw_shift, s_gla_S = [], [], [], [], [], []
    pending_p = pending_s = None
    for i in range(depth):
        j = i // 2
        if i % 2 == 0:
            wi = w_in[i]
            ws = [rw_mu[j], bf(wi[:, :tw]), bf(wi[:, tw:2 * tw]), bf(wi[:, 2 * tw:3 * tw]), bf(wi[:, 3 * tw:]),
                  bf(rw_w1[j]), bf(rw_w2[j]), rw_w0[j].reshape(1, tw), bf(rw_a1[j]), bf(rw_a2[j]),
                  rw_a0[j].reshape(1, tw), bf(rw_g1[j]), bf(rw_g2[j])]
            if h_p is None:
                *rkvda_p, gate_p, qm_p = _rwkv_proj(x_p, None, ws, PROJ_BLOCK, tp, norm_g=norm_mix_g[i])
                shift_p = _norm(x_prompt[:, -1, :], norm_mix_g[i], bp)
            else:
                *rkvda_p, gate_p, qm_p = _rwkv_proj(h_p, None, ws, PROJ_BLOCK, tp)
                shift_p = h_p.reshape(bp, tp, D_MODEL)[:, -1]
            *rkvda_s, gate_s, qm_s = _rwkv_proj(h_s, state_rwkv_shift[j], ws, SAMPLE_BLOCK, ts)
            pvec = [rw_k_k[j], rw_k_a[j], rw_r_k[j], rw_ln_g[j], rw_ln_b[j]]
            tok_p, sp = _rwkv_prompt_mixer(*rkvda_p, *[_seq_scan_param(p, bp) for p in pvec], bp, tp, SCAN_TIME_BLOCK)
            ys, ss = _rwkv_scan(*[_to_scan_layout(t_, bs, ts) for t_ in rkvda_s], [_scan_param(p, bs) for p in pvec],
                                _state_to_scan(state_rwkv_S[j]), LANES, 1)
            tok_s = _from_scan_layout(ys, bs, ts)
            p_rw_S.append(sp[:, :, :RWKV_HEADS * bp].reshape(RWKV_N, RWKV_N, RWKV_HEADS, bp).transpose(3, 2, 0, 1))
            s_rw_S.append(_state_from_scan(ss, bs))
            p_rw_shift.append(shift_p)
            s_rw_shift.append(h_s)
            layout_p, layout_s = "rwkv_pairs", "rows"
        else:
            ws = _gla_weights(w_in[i], gla_a1[j], gla_a2[j], gla_ab[j])
            if pending_p is None:
                q_p, k_p, v_p, gate_p, la_p, qm_p = _gla_proj(h_p, ws, PROJ_BLOCK)
                q_s, k_s, v_s, gate_s, la_s, qm_s = _gla_proj(h_s, ws, SAMPLE_BLOCK)
            else:
                q_p, k_p, v_p, gate_p, la_p, qm_p, x_p = _gla_proj(None, ws, PROJ_BLOCK, pending_p)
                q_s, k_s, v_s, gate_s, la_s, qm_s, x_s = _gla_proj(None, ws, SAMPLE_BLOCK, pending_s)
            tok_p, sp_t = _gla_chunk_scan(q_p, k_p, v_p, la_p, gla_norm_g[j], bp, tp)
            rows_of = lambda t_: t_.transpose(1, 0, 2).reshape(bs * nh, t_.shape[2])
            os_, ss = _gla_step(rows_of(q_s), rows_of(k_s), rows_of(v_s), rows_of(la_s),
                                gla_norm_g[j], state_gla_S[j].reshape(bs * nh, GLA_DK, GLA_DV))
            tok_s = os_.reshape(bs, nh, GLA_DV).transpose(1, 0, 2)
            p_gla_S.append(sp_t.transpose(0, 1, 3, 2))
            s_gla_S.append(ss.reshape(bs, nh, GLA_DK, GLA_DV))
            layout_p = layout_s = "gla_heads"

        att_p = _mem_attn(qm_p.reshape(bp, tp, MEM_WIDTH), pk, pv, i, 512).reshape(np_, MEM_WIDTH)
        att_s = _mem_attn(qm_s.reshape(bs, ts, MEM_WIDTH), sk, sv, i, 1).reshape(ns, MEM_WIDTH)
        wr_hi = bf(w_router[i])
        wr_lo = bf(w_router[i] - wr_hi.astype(F32))
        ws = [bf(w_out[i]), norm_ffn_g[i].reshape(1, D_MODEL), wr_hi, wr_lo, b_router[i]]
        x1_p, h2, logits_p = _out_proj(tok_p, gate_p, att_p, x_p, ws, layout_p, PROJ_BLOCK, m, 0)
        x1_s, h2, logits_s = _out_proj(tok_s, gate_s, att_s, x_s, ws, layout_s, SAMPLE_BLOCK, m, np_, h2_buffer=h2)
        tok_sorted, gate_sorted, inv, item_tile, item_expert, n_items, lo, hi = _route(
            jnp.concatenate([logits_p, logits_s], axis=0), np_, MOE_BLOCK)
        rows = _moe_ffn(h2[tok_sorted], gate_sorted, item_tile, item_expert, n_items, lo, hi,
                        exp_w_up, exp_w_down, i)
        gathered = rows[inv]
        g_next = norm_mix_g[i + 1] if i + 1 < depth else norm_final_g
        if i + 1 < depth and (i + 1) % 2 == 1:
            pending_p = (x1_p, gathered, 0, g_next)
            pending_s = (x1_s, gathered, 2 * np_, g_next)
        else:
            pending_p = pending_s = None
            x_p, h_p = _combine(x1_p, gathered, 0, g_next, LIGHT_BLOCK, i + 1 == depth)
            x_s, h_s = _combine(x1_s, gathered, 2 * np_, g_next, SAMPLE_BLOCK, i + 1 == depth)

    y_prompt = h_p.reshape(bp, tp, D_MODEL)
    y_sample = h_s.reshape(bs, ts, D_MODEL)
    return (y_prompt, y_sample, jnp.stack(p_rw_S), jnp.stack(p_rw_shift), jnp.stack(p_gla_S),
            prompt_mem_k, prompt_mem_v, jnp.stack(s_rw_S), jnp.stack(s_rw_shift), jnp.stack(s_gla_S))
```

```python
import functools

import numpy as np
import jax
import jax.numpy as jnp
from jax import lax
from jax.experimental import pallas as pl
from jax.experimental.pallas import tpu as pltpu

F32 = jnp.float32
BF16 = jnp.bfloat16

D_MODEL = 1024
TOK_WIDTH = 768
MEM_WIDTH = 256
MEM_HEADS = 4
MEM_HEAD_DIM = 64
N_MEM = 256
RWKV_HEADS = 12
RWKV_N = 64
RWKV_PAIRS = RWKV_HEADS // 2
RWKV_GN_EPS = 64e-5
GLA_HEADS = 4
GLA_KW = 384
GLA_DK = 96
GLA_DV = 192
GLA_TAU = 16.0
GLA_CHUNK = 64
GLA_TILE = 8
N_GROUPS = 4
EXPERTS_PER_GROUP = 8
N_EXPERTS = 32
EXPERT_FF = 512
NORM_EPS = 1e-6
ROUTER_LANES = 128
LANES = 128
SUBLANES = 8

PROJ_BLOCK = 512
LIGHT_BLOCK = 512
SAMPLE_BLOCK = 128
MOE_BLOCK = 256
SCAN_TIME_BLOCK = 32
VMEM_LIMIT = 56 * 1024 * 1024


def _cparams(*sem):
    return pltpu.CompilerParams(dimension_semantics=sem, vmem_limit_bytes=VMEM_LIMIT)


def _dot(a, b):
    return jnp.dot(a, b, preferred_element_type=F32)


def _dot_nt(a, b):
    return lax.dot_general(a, b, (((1,), (1,)), ((), ())), preferred_element_type=F32)


def _dot_tn(a, b):
    return lax.dot_general(a, b, (((0,), (0,)), ((), ())), preferred_element_type=F32)


def _rms(x, g):
    return x * lax.rsqrt(jnp.mean(x * x, axis=-1, keepdims=True) + NORM_EPS) * g


def _sigmoid(x):
    return 1.0 / (1.0 + jnp.exp(-x))


def _row_spec(tm, n, offset=0):
    return pl.BlockSpec((tm, n), lambda i: (i + offset, 0))


def _head_row_spec(tm, n):
    return pl.BlockSpec((GLA_HEADS, tm, n), lambda i: (0, i, 0))


def _full_spec(shape):
    nd = len(shape)
    return pl.BlockSpec(shape, lambda *_: (0,) * nd)


def _norm_kernel(x_ref, g_ref, o_ref):
    o_ref[...] = _rms(x_ref[...], g_ref[...])


def _norm(x, g, tm):
    m, d = x.shape
    return pl.pallas_call(
        _norm_kernel, grid=(m // tm,),
        in_specs=[_row_spec(tm, d), _full_spec((1, d))],
        out_specs=_row_spec(tm, d),
        out_shape=jax.ShapeDtypeStruct((m, d), F32),
        compiler_params=_cparams("parallel"), name="rms_norm")(x, g.reshape(1, d))


def _norm_matmul_kernel(x_ref, g_ref, w_ref, o_ref):
    o_ref[...] = _dot(_rms(x_ref[...], g_ref[...]).astype(BF16), w_ref[...])


def _norm_matmul(x, g, w, tm):
    m, d = x.shape
    n = w.shape[1]
    return pl.pallas_call(
        _norm_matmul_kernel, grid=(m // tm,),
        in_specs=[_row_spec(tm, d), _full_spec((1, d)), _full_spec((d, n))],
        out_specs=_row_spec(tm, n),
        out_shape=jax.ShapeDtypeStruct((m, n), F32),
        compiler_params=_cparams("parallel"), name="norm_matmul")(x, g.reshape(1, d), w.astype(BF16))


def _rwkv_proj_kernel(h_ref, hp_ref, *refs, blocks_per_seq, norm_input):
    if norm_input:
        ng_ref, refs = refs[0], refs[1:]
        normed = lambda t_: _rms(t_, ng_ref[...])
    else:
        normed = lambda t_: t_
    (mu_ref, wr_ref, wk_ref, wv_ref, wq_ref, w1_ref, w2_ref, w0_ref, a1_ref, a2_ref, a0_ref, g1_ref, g2_ref,
     r_out, k_out, v_out, d_out, a_out, g_out, q_out) = refs

    def put(out, val):
        if blocks_per_seq:
            for p in range(RWKV_PAIRS):
                out[p] = val[:, p * LANES:(p + 1) * LANES]
        else:
            out[...] = val

    h = normed(h_ref[...])
    if blocks_per_seq:
        seq_start = (pl.program_id(0) % blocks_per_seq) == 0
        before = jnp.where(seq_start, 0.0, normed(hp_ref[SUBLANES - 1:SUBLANES, :]))
        row = lax.broadcasted_iota(jnp.int32, (h.shape[0], 1), 0)
        hp = jnp.where(row == 0, before, pltpu.roll(h, 1, axis=0))
    else:
        hp = hp_ref[...]
    xx = hp - h

    def mix(j):
        return (h + xx * mu_ref[j:j + 1, :]).astype(BF16)

    put(r_out, _dot(mix(0), wr_ref[...]))
    wl = w0_ref[...] + _dot(jnp.tanh(_dot(mix(1), w1_ref[...])).astype(BF16), w2_ref[...])
    put(k_out, _dot(mix(2), wk_ref[...]))
    put(v_out, _dot(mix(3), wv_ref[...]))
    al = a0_ref[...] + _dot(_dot(mix(4), a1_ref[...]).astype(BF16), a2_ref[...])
    g_out[...] = _dot(_sigmoid(_dot(mix(5), g1_ref[...])).astype(BF16), g2_ref[...])
    q_out[...] = _dot(h.astype(BF16), wq_ref[...])
    z = -wl
    softplus = jnp.maximum(z, 0.0) + jnp.log(1.0 + jnp.exp(-jnp.abs(z)))
    put(d_out, jnp.exp(-jnp.exp(-softplus - 0.5)))
    put(a_out, _sigmoid(al))


def _rwkv_proj(h, h_prev, ws, tm, seq_len, norm_g=None):
    m = h.shape[0]
    if norm_g is not None:
        assert h_prev is None
        ws = [norm_g.reshape(1, D_MODEL)] + list(ws)
    tw = TOK_WIDTH
    if h_prev is None:
        assert seq_len % tm == 0 and tm % SUBLANES == 0
        per8 = tm // SUBLANES
        hp_spec = pl.BlockSpec((SUBLANES, D_MODEL), lambda i: (jnp.maximum(i * per8 - 1, 0), 0))
        h_prev, blocks_per_seq = h, seq_len // tm
    else:
        hp_spec, blocks_per_seq = _row_spec(tm, D_MODEL), 0
    if blocks_per_seq:
        scan_spec = pl.BlockSpec((RWKV_PAIRS, tm, LANES), lambda i: (0, i, 0))
        scan_shape = jax.ShapeDtypeStruct((RWKV_PAIRS, m, LANES), F32)
    else:
        scan_spec, scan_shape = _row_spec(tm, tw), jax.ShapeDtypeStruct((m, tw), F32)
    return pl.pallas_call(
        functools.partial(_rwkv_proj_kernel, blocks_per_seq=blocks_per_seq, norm_input=norm_g is not None),
        grid=(m // tm,),
        in_specs=[_row_spec(tm, D_MODEL), hp_spec] + [_full_spec(w.shape) for w in ws],
        out_specs=[scan_spec] * 5 + [_row_spec(tm, tw), _row_spec(tm, MEM_WIDTH)],
        out_shape=[scan_shape] * 5 + [jax.ShapeDtypeStruct((m, tw), F32), jax.ShapeDtypeStruct((m, MEM_WIDTH), F32)],
        compiler_params=_cparams("parallel"), name="rwkv_proj")(h, h_prev, *ws)


def _pad_lanes(x):
    short = LANES - x.shape[-1]
    if short == 0:
        return x
    return jnp.concatenate([x, jnp.zeros(x.shape[:-1] + (short,), x.dtype)], axis=-1)


def _rwkv_scan_kernel(r_ref, k_ref, v_ref, d_ref, a_ref, kkp_ref, kap_ref, rkp_ref, lng_ref, lnb_ref, s0_ref,
                      y_ref, sfin_ref, s_scr, v_scr, yrow_scr, *, tc):
    n = RWKV_N
    nl = r_ref.shape[-1]

    @pl.when(pl.program_id(1) == 0)
    def _():
        s_scr[...] = _pad_lanes(s0_ref[...])

    kkp = _pad_lanes(kkp_ref[...])
    kap = _pad_lanes(kap_ref[...])
    rkp = _pad_lanes(rkp_ref[...])
    lng = _pad_lanes(lng_ref[...])
    lnb = _pad_lanes(lnb_ref[...])

    def step(t, carry):
        r_t = _pad_lanes(r_ref[t])
        k_t = _pad_lanes(k_ref[t])
        v_t = _pad_lanes(v_ref[t])
        d_t = _pad_lanes(d_ref[t])
        a_t = _pad_lanes(a_ref[t])
        v_scr[...] = v_t
        kkr = k_t * kkp
        nrm = jnp.maximum(jnp.sqrt(jnp.sum(kkr * kkr, axis=0, keepdims=True)), 1e-12)
        kk = kkr * (1.0 / nrm)
        k2 = k_t * (1.0 + (a_t - 1.0) * kap)
        nkk = -kk
        b_t = kk * a_t

        def ibody(i, c):
            s_i = s_scr[i]
            sa = jnp.sum(s_i * nkk, axis=0, keepdims=True)
            v_i = v_scr[pl.ds(i, 1), :]
            s_n = s_i * d_t + sa * b_t + v_i * k2
            s_scr[i] = s_n
            yrow_scr[pl.ds(i, 1), :] = jnp.sum(s_n * r_t, axis=0, keepdims=True)
            return c

        lax.fori_loop(0, n, ibody, 0, unroll=8)
        y = yrow_scr[...]
        yc = y - jnp.mean(y, axis=0, keepdims=True)
        var = jnp.mean(yc * yc, axis=0, keepdims=True)
        gn = yc * lax.rsqrt(var + RWKV_GN_EPS) * lng + lnb
        bonus = jnp.sum(r_t * k2 * rkp, axis=0, keepdims=True) * v_t
        y_ref[t] = (gn + bonus)[:, :nl]
        return carry

    lax.fori_loop(0, tc, step, 0)

    @pl.when(pl.program_id(1) == pl.num_programs(1) - 1)
    def _():
        sfin_ref[...] = s_scr[:, :, :nl]


def _rwkv_scan(r, k, v, d, a, params, s0, lane_block, tc):
    t, n, l = r.shape
    seq = pl.BlockSpec((tc, n, lane_block), lambda li, ti: (ti, 0, li))
    par = pl.BlockSpec((n, lane_block), lambda li, ti: (0, li))
    st = pl.BlockSpec((n, n, lane_block), lambda li, ti: (0, 0, li))
    return pl.pallas_call(
        functools.partial(_rwkv_scan_kernel, tc=tc), grid=(l // lane_block, t // tc),
        in_specs=[seq] * 5 + [par] * 5 + [st],
        out_specs=[seq, st],
        out_shape=[jax.ShapeDtypeStruct((t, n, l), F32), jax.ShapeDtypeStruct((n, n, l), F32)],
        scratch_shapes=[pltpu.VMEM((n, n, LANES), F32), pltpu.VMEM((n, LANES), F32), pltpu.VMEM((n, LANES), F32)],
        compiler_params=_cparams("parallel", "arbitrary"), name="rwkv_scan")(r, k, v, d, a, *params, s0)


PREP_TILES = 6
DECAY_WINDOW = 16


def _rwkv_prep_kernel(r_ref, k_ref, v_ref, d_ref, a_ref, kkp_ref, kap_ref, o_ref, *, tc, nb):
    n = RWKV_N
    rows_per_pair = nb * tc
    ins = [ref.reshape(RWKV_PAIRS * rows_per_pair, LANES) for ref in (r_ref, k_ref, v_ref, d_ref, a_ref)]
    zero_rows = jnp.zeros((LANES - RWKV_HEADS * nb, LANES), F32)
    first_half = lax.broadcasted_iota(jnp.int32, (nb, LANES), 1) < n
    kkp = kkp_ref[...]
    kap = kap_ref[...]

    def load_transposed_pair(x2, t):
        pieces = []
        for p in range(RWKV_PAIRS):
            now = x2[pl.ds(p * rows_per_pair + t, nb, stride=tc), :]
            nxt = x2[pl.ds(p * rows_per_pair + t + 1, nb, stride=tc), :]
            pieces += [jnp.where(first_half, now, pltpu.roll(nxt, n, axis=1)),
                       jnp.where(first_half, pltpu.roll(now, n, axis=1), nxt)]
        both = jnp.concatenate(pieces + [zero_rows], axis=0).T
        return both[:n, :], both[n:, :]

    def emit(t, c_prev, r_t, k_t, v_t, d_t, a_t):
        kkr = k_t * kkp
        nrm = jnp.maximum(jnp.sqrt(jnp.sum(kkr * kkr, axis=0, keepdims=True)), 1e-12)
        kk = kkr * (1.0 / nrm)
        k2 = k_t * (1.0 + (a_t - 1.0) * kap)
        c_prev = jnp.where(t % DECAY_WINDOW == 0, 1.0, c_prev)
        c_t = c_prev * d_t
        inv = 1.0 / jnp.maximum(c_t, 1e-30)
        o_ref[t, 0] = -kk * c_prev
        o_ref[t, 1] = c_t
        o_ref[t, 2] = kk * a_t * inv
        o_ref[t, 3] = k2 * inv
        o_ref[t, 4] = r_t * c_t
        o_ref[t, 5] = v_t
        return c_t

    def two_steps(u, c):
        t = 2 * u
        tiles = [load_transposed_pair(x2, t) for x2 in ins]
        c = emit(t, c, *[tile[0] for tile in tiles])
        return emit(t + 1, c, *[tile[1] for tile in tiles])

    lax.fori_loop(0, tc // 2, two_steps, jnp.ones((n, LANES), F32), unroll=2)


def _rwkv_state_scan_kernel(x_ref, rkp_ref, lng_ref, lnb_ref, y_ref, sfin_ref, s_scr, sa_scr, yrow_scr, *, tc):
    n = RWKV_N
    groups = n // SUBLANES

    @pl.when(pl.program_id(0) == 0)
    def _():
        s_scr[...] = jnp.zeros(s_scr.shape, F32)

    rkp = rkp_ref[...]
    lng = lng_ref[...]
    lnb = lnb_ref[...]
    sub = lax.broadcasted_iota(jnp.int32, (SUBLANES, LANES), 0)
    low4 = sub < 4
    low2 = (sub & 3) < 2
    low1 = (sub & 1) == 0

    def fold(x, y, dist, low):
        if dist == 4:
            return jnp.where(low, x, y) + pltpu.roll(jnp.where(low, y, x), 4, axis=0)
        return (jnp.where(low, x, pltpu.roll(y, dist, axis=0))
                + jnp.where(low, pltpu.roll(x, SUBLANES - dist, axis=0), y))

    def sublane_sums(ps):
        z = [fold(ps[0], ps[4], 4, low4), fold(ps[2], ps[6], 4, low4),
             fold(ps[1], ps[5], 4, low4), fold(ps[3], ps[7], 4, low4)]
        return fold(fold(z[0], z[1], 2, low2), fold(z[2], z[3], 2, low2), 1, low1)

    def tile_sum(x):
        acc = x[0:SUBLANES]
        for u in range(1, groups):
            acc = acc + x[u * SUBLANES:(u + 1) * SUBLANES]
        return acc

    def step(t, carry):
        nkk = x_ref[t, 0]
        b_t = x_ref[t, 2]
        k2 = x_ref[t, 3]
        r_t = x_ref[t, 4]
        for g in range(groups):
            sa_scr[g * SUBLANES:(g + 1) * SUBLANES, :] = sublane_sums(
                [tile_sum(s_scr[g * SUBLANES + u] * nkk) for u in range(SUBLANES)])
        for g in range(groups):
            ps = []
            for u in range(SUBLANES):
                i = g * SUBLANES + u
                s_n = s_scr[i] + sa_scr[pl.ds(i, 1), :] * b_t + x_ref[t, 5, pl.ds(i, 1), :] * k2
                s_scr[i] = s_n
                ps.append(tile_sum(s_n * r_t))
            yrow_scr[g * SUBLANES:(g + 1) * SUBLANES, :] = sublane_sums(ps)
        y = yrow_scr[...]
        yc = y - jnp.mean(y, axis=0, keepdims=True)
        var = jnp.mean(yc * yc, axis=0, keepdims=True)
        bonus = jnp.sum(r_t * k2 * rkp, axis=0, keepdims=True) * x_ref[t, 5]
        y_ref[t] = yc * lax.rsqrt(var + RWKV_GN_EPS) * lng + lnb + bonus

        @pl.when(t % DECAY_WINDOW == DECAY_WINDOW - 1)
        def _():
            c_t = x_ref[t, 1]
            for i in range(n):
                s_scr[i] = s_scr[i] * c_t

        return carry

    lax.fori_loop(0, tc, step, 0)

    @pl.when(pl.program_id(0) == pl.num_programs(0) - 1)
    def _():
        sfin_ref[...] = s_scr[...]


def _rwkv_unprep_kernel(y_ref, o_ref, *, tc, nb):
    n = RWKV_N
    rows_per_pair = nb * tc
    o2 = o_ref.reshape(RWKV_PAIRS * rows_per_pair, LANES)
    first_half = lax.broadcasted_iota(jnp.int32, (nb, LANES), 1) < n

    def two_steps(u, carry):
        t = 2 * u
        w = jnp.concatenate([y_ref[t], y_ref[t + 1]], axis=0).T
        for p in range(RWKV_PAIRS):
            even = w[(2 * p) * nb:(2 * p + 1) * nb, :]
            odd = w[(2 * p + 1) * nb:(2 * p + 2) * nb, :]
            o2[pl.ds(p * rows_per_pair + t, nb, stride=tc), :] = jnp.where(
                first_half, even, pltpu.roll(odd, n, axis=1))
            o2[pl.ds(p * rows_per_pair + t + 1, nb, stride=tc), :] = jnp.where(
                first_half, pltpu.roll(even, n, axis=1), odd)
        return carry

    lax.fori_loop(0, tc // 2, two_steps, 0, unroll=2)


def _rwkv_prompt_mixer(r, k, v, d, a, kkp, kap, rkp, lng, lnb, nb, t, tc):
    n = RWKV_N
    assert tc % SUBLANES == 0 and tc % 4 == 0 and t % tc == 0 and RWKV_HEADS * nb <= LANES
    assert tc % DECAY_WINDOW == 0
    grid = (t // tc,)
    tok = pl.BlockSpec((RWKV_PAIRS, nb, tc, LANES), lambda ti: (0, 0, ti, 0))
    par = pl.BlockSpec((n, LANES), lambda ti: (0, 0))
    tiles = pl.BlockSpec((tc, PREP_TILES, n, LANES), lambda ti: (ti, 0, 0, 0))
    ytile = pl.BlockSpec((tc, n, LANES), lambda ti: (ti, 0, 0))
    prepared = pl.pallas_call(
        functools.partial(_rwkv_prep_kernel, tc=tc, nb=nb), grid=grid,
        in_specs=[tok] * 5 + [par] * 2, out_specs=tiles,
        out_shape=jax.ShapeDtypeStruct((t, PREP_TILES, n, LANES), F32),
        compiler_params=_cparams("parallel"), name="rwkv_prep")(
            *[x.reshape(RWKV_PAIRS, nb, t, LANES) for x in (r, k, v, d, a)], kkp, kap)
    y, s = pl.pallas_call(
        functools.partial(_rwkv_state_scan_kernel, tc=tc), grid=grid,
        in_specs=[tiles, par, par, par],
        out_specs=[ytile, pl.BlockSpec((n, n, LANES), lambda ti: (0, 0, 0))],
        out_shape=[jax.ShapeDtypeStruct((t, n, LANES), F32), jax.ShapeDtypeStruct((n, n, LANES), F32)],
        scratch_shapes=[pltpu.VMEM((n, n, LANES), F32), pltpu.VMEM((n, LANES), F32), pltpu.VMEM((n, LANES), F32)],
        compiler_params=_cparams("arbitrary"), name="rwkv_state_scan")(prepared, rkp, lng, lnb)
    tok_out = pl.pallas_call(
        functools.partial(_rwkv_unprep_kernel, tc=tc, nb=nb), grid=grid,
        in_specs=[ytile], out_specs=tok,
        out_shape=jax.ShapeDtypeStruct((RWKV_PAIRS, nb, t, LANES), F32),
        compiler_params=_cparams("parallel"), name="rwkv_unprep")(y)
    return tok_out.reshape(RWKV_PAIRS, nb * t, LANES), s


def _gla_proj_kernel(*refs, fused_combine):
    if fused_combine:
        x_ref, r0_ref, r1_ref, g_ref = refs[:4]
        refs, x_out = refs[4:-1], refs[-1]
        x2 = x_ref[...] + (r0_ref[...] + r1_ref[...])
        x_out[...] = x2
        h = _rms(x2, g_ref[...])
    else:
        h, refs = refs[0][...], refs[1:]
    (wq_ref, wk_ref, wv_ref, wr_ref, wm_ref, a1_ref, a2_ref, ab_ref,
     q_out, k_out, v_out, r_out, la_out, qm_out) = refs
    hb = h.astype(BF16)
    low = _dot(hb, a1_ref[...]).astype(BF16)
    for hd in range(GLA_HEADS):
        q_out[hd] = _dot(hb, wq_ref[hd])
        k_out[hd] = _dot(hb, wk_ref[hd])
        v_out[hd] = _dot(hb, wv_ref[hd])
        r_out[hd] = _dot(hb, wr_ref[hd])
        x = _dot(low, a2_ref[hd]) + ab_ref[hd]
        log_sigmoid = jnp.minimum(x, 0.0) - jnp.log(1.0 + jnp.exp(-jnp.abs(x)))
        la_out[hd] = log_sigmoid / GLA_TAU
    qm_out[...] = _dot(hb, wm_ref[...])


def _gla_weights(w_in, a1, a2, ab):
    kw, tw, nh = GLA_KW, TOK_WIDTH, GLA_HEADS

    def heads(w, d):
        return w.reshape(w.shape[0], nh, d).transpose(1, 0, 2)

    return [heads(w_in[:, :kw], GLA_DK).astype(BF16), heads(w_in[:, kw:2 * kw], GLA_DK).astype(BF16),
            heads(w_in[:, 2 * kw:2 * kw + tw], GLA_DV).astype(BF16),
            heads(w_in[:, 2 * kw + tw:3 * tw], GLA_DV).astype(BF16), w_in[:, 3 * tw:].astype(BF16),
            a1.astype(BF16), heads(a2, GLA_DK).astype(BF16), ab.reshape(nh, 1, GLA_DK)]


def _gla_proj(h, ws, tm, pending=None):
    nh = GLA_HEADS
    widths = [GLA_DK, GLA_DK, GLA_DV, GLA_DV, GLA_DK]
    if pending is None:
        m = h.shape[0]
        lead_specs, lead_args = [_row_spec(tm, D_MODEL)], [h]
    else:
        x1, gathered, first_row, norm_g = pending
        m = x1.shape[0]
        lead_specs = [_row_spec(tm, D_MODEL), _row_spec(tm, D_MODEL, first_row // tm),
                      _row_spec(tm, D_MODEL, (first_row + m) // tm), _full_spec((1, D_MODEL))]
        lead_args = [x1, gathered, gathered, norm_g.reshape(1, D_MODEL)]
    out_specs = [_head_row_spec(tm, w) for w in widths] + [_row_spec(tm, MEM_WIDTH)]
    out_shape = [jax.ShapeDtypeStruct((nh, m, w), F32) for w in widths] + [jax.ShapeDtypeStruct((m, MEM_WIDTH), F32)]
    if pending is not None:
        out_specs.append(_row_spec(tm, D_MODEL))
        out_shape.append(jax.ShapeDtypeStruct((m, D_MODEL), F32))
    return pl.pallas_call(
        functools.partial(_gla_proj_kernel, fused_combine=pending is not None), grid=(m // tm,),
        in_specs=lead_specs + [_full_spec(w.shape) for w in ws],
        out_specs=out_specs, out_shape=out_shape,
        compiler_params=_cparams("parallel"), name="gla_proj")(*lead_args, *ws)


def _gla_out_norm(o, g):
    return o * lax.rsqrt(jnp.mean(o * o, axis=-1, keepdims=True) + NORM_EPS) * g


def _gla_chunk_kernel(q_ref, k_ref, v_ref, la_ref, ng_ref, o_ref, sfin_ref, st_scr):
    c, dk = GLA_CHUNK, GLA_DK

    @pl.when(pl.program_id(1) == 0)
    def _():
        st_scr[...] = jnp.zeros(st_scr.shape, F32)

    row = lax.broadcasted_iota(jnp.int32, (c, c), 0)
    col = lax.broadcasted_iota(jnp.int32, (c, c), 1)
    tril = (row >= col).astype(F32)
    rr = lax.broadcasted_iota(jnp.int32, (c, dk), 0)
    ones_sum = jnp.ones((dk, LANES), BF16)

    heads = range(GLA_HEADS)
    tril_b = tril.astype(BF16)

    def cumsum_rows(la):
        hi = la.astype(BF16)
        r1 = la - hi.astype(F32)
        mid = r1.astype(BF16)
        lo = (r1 - mid.astype(F32)).astype(BF16)
        return _dot(tril_b, hi) + _dot(tril_b, mid) + _dot(tril_b, lo)

    def tile_roll(x, dlt):
        return pltpu.roll(x.reshape(c // GLA_TILE, GLA_TILE, dk), dlt, axis=1).reshape(c, dk)

    k = [k_ref[hd] for hd in heads]
    vb = [v_ref[hd].astype(BF16) for hd in heads]
    q = [q_ref[hd] * (dk ** -0.5) for hd in heads]
    b = [cumsum_rows(la_ref[hd]) for hd in heads]
    st = [st_scr[hd] for hd in heads]
    inter = [_dot_nt((q[hd] * jnp.exp(b[hd])).astype(BF16), st[hd].astype(BF16)) for hd in heads]

    att = [jnp.zeros((c, c), F32) for hd in heads]
    blk = c // 2
    while blk >= GLA_TILE:
        two = 2 * blk
        upper = (rr & (two - 1)) >= blk
        same_block = (row ^ col) < two
        parts = []
        for hd in heads:
            b_ref_rows = jnp.concatenate(
                [jnp.broadcast_to(b[hd][s0 + blk - 1:s0 + blk, :], (two, dk)) for s0 in range(0, c, two)], axis=0)
            q_l = jnp.where(upper, q[hd] * jnp.exp(jnp.minimum(b[hd] - b_ref_rows, 0.0)), 0.0).astype(BF16)
            k_l = jnp.where(upper, 0.0, k[hd] * jnp.exp(jnp.minimum(b_ref_rows - b[hd], 0.0))).astype(BF16)
            parts.append(_dot_nt(q_l, k_l))
        att = [att[hd] + jnp.where(same_block, parts[hd], 0.0) for hd in heads]
        blk //= 2

    sums = []
    for hd in heads:
        prods = [(q[hd] * k[hd]).astype(BF16)]
        for dlt in range(1, GLA_TILE):
            p = q[hd] * tile_roll(k[hd], dlt) * jnp.exp(jnp.minimum(b[hd] - tile_roll(b[hd], dlt), 0.0))
            prods.append(jnp.where((rr & (GLA_TILE - 1)) >= dlt, p, 0.0).astype(BF16))
        sums.append(_dot(jnp.concatenate(prods, axis=0), ones_sum))
    for dlt in range(GLA_TILE):
        on_diag = col == row - dlt
        att = [att[hd] + jnp.where(on_diag, sums[hd][dlt * c:(dlt + 1) * c, :c], 0.0) for hd in heads]

    o = [inter[hd] + _dot(att[hd].astype(BF16), vb[hd]) for hd in heads]
    for hd in heads:
        o_ref[hd] = _gla_out_norm(o[hd], ng_ref[...])
    for hd in heads:
        b_end = b[hd][c - 1:c, :]
        kd = (k[hd] * jnp.exp(b_end - b[hd])).astype(BF16)
        st_scr[hd] = jnp.exp(b_end) * st[hd] + _dot_tn(vb[hd], kd)

    @pl.when(pl.program_id(1) == pl.num_programs(1) - 1)
    def _():
        sfin_ref[0] = st_scr[...]


def _gla_chunk_scan(q, k, v, la, norm_g, batch, t):
    nh, dk, dv, c = GLA_HEADS, GLA_DK, GLA_DV, GLA_CHUNK
    nc = t // c
    kspec = pl.BlockSpec((nh, c, dk), lambda i, j: (0, i * nc + j, 0))
    vspec = pl.BlockSpec((nh, c, dv), lambda i, j: (0, i * nc + j, 0))
    sspec = pl.BlockSpec((1, nh, dv, dk), lambda i, j: (i, 0, 0, 0))
    return pl.pallas_call(
        _gla_chunk_kernel, grid=(batch, nc),
        in_specs=[kspec, kspec, vspec, kspec, pl.BlockSpec((1, dv), lambda i, j: (0, 0))],
        out_specs=[vspec, sspec],
        out_shape=[jax.ShapeDtypeStruct((nh, batch * t, dv), F32), jax.ShapeDtypeStruct((batch, nh, dv, dk), F32)],
        scratch_shapes=[pltpu.VMEM((nh, dv, dk), F32)],
        compiler_params=_cparams("parallel", "arbitrary"), name="gla_chunk")(q, k, v, la, norm_g.reshape(1, dv))


GLA_STEP_GROUP = 8


def _gla_step_kernel(q_ref, k_ref, v_ref, la_ref, ng_ref, s0_ref, o_ref, s_ref):
    g_n, dk = GLA_STEP_GROUP, GLA_DK

    def columns(ref):
        x = jnp.concatenate([ref[...], jnp.zeros((g_n, LANES - dk), F32)], axis=1)
        return jnp.concatenate([x, jnp.zeros((LANES - g_n, LANES), F32)], axis=0).T

    q_cols, k_cols, la_cols = columns(q_ref), columns(k_ref), columns(la_ref)
    v = v_ref[...]
    rows = []
    for g in range(g_n):
        kv = k_cols[:dk, g:g + 1].astype(BF16).astype(F32) * v[g:g + 1, :].astype(BF16).astype(F32)
        bi, hi = divmod(g, GLA_HEADS)
        s_new = jnp.exp(la_cols[:dk, g:g + 1]) * s0_ref[bi, hi] + kv
        s_ref[bi, hi] = s_new
        o = jnp.sum((q_cols[:dk, g:g + 1] * (dk ** -0.5)) * s_new, axis=0, keepdims=True)
        rows.append(_gla_out_norm(o, ng_ref[...]))
    o_ref[...] = jnp.concatenate(rows, axis=0)


def _gla_step(q, k, v, la, norm_g, states, layer):
    bh, dk = q.shape
    dv = v.shape[1]
    g, nh = GLA_STEP_GROUP, GLA_HEADS
    b = bh // nh
    kspec = _row_spec(g, dk)
    vspec = _row_spec(g, dv)
    return pl.pallas_call(
        _gla_step_kernel, grid=(bh // g,),
        in_specs=[kspec, kspec, vspec, kspec, _full_spec((1, dv)),
                  pl.BlockSpec((None, g // nh, nh, dk, dv), lambda i: (layer, i, 0, 0, 0))],
        out_specs=[vspec, pl.BlockSpec((g // nh, nh, dk, dv), lambda i: (i, 0, 0, 0))],
        out_shape=[jax.ShapeDtypeStruct((bh, dv), F32), jax.ShapeDtypeStruct((b, nh, dk, dv), F32)],
        compiler_params=_cparams("parallel"), name="gla_step")(q, k, v, la, norm_g.reshape(1, dv), states)


def _mem_attn_kernel(q_ref, k_ref, v_ref, o_ref):
    q = q_ref[0]
    k = k_ref[0].astype(BF16)
    v_ones = jnp.concatenate([v_ref[0].astype(BF16), jnp.ones((N_MEM, LANES), BF16)], axis=1)
    head_of_lane = lax.broadcasted_iota(jnp.int32, (1, MEM_WIDTH), 1) // MEM_HEAD_DIM
    heads = range(MEM_HEADS)
    mine = [head_of_lane == h for h in heads]
    qh = [jnp.where(mine[h], q, 0.0).astype(BF16) for h in heads]
    s = [_dot_nt(qh[h], k) * (MEM_HEAD_DIM ** -0.5) for h in heads]
    e = [jnp.exp(s[h] - jnp.max(s[h], axis=-1, keepdims=True)).astype(BF16) for h in heads]
    ev = [_dot(e[h], v_ones) for h in heads]
    out = jnp.zeros(q.shape, F32)
    for h in heads:
        inv = 1.0 / ev[h][:, MEM_WIDTH:]
        out = out + jnp.where(mine[h], ev[h][:, :MEM_WIDTH] * jnp.concatenate([inv, inv], axis=1), 0.0)
    o_ref[0] = out


def _mem_attn(q, mem_k, mem_v, layer, tq):
    b, t, w = q.shape
    qspec = pl.BlockSpec((1, tq, w), lambda i, j: (i, j, 0))
    mspec = pl.BlockSpec((None, 1, N_MEM, w), lambda i, j: (layer, i, 0, 0))
    return pl.pallas_call(
        _mem_attn_kernel, grid=(b, t // tq),
        in_specs=[qspec, mspec, mspec], out_specs=qspec,
        out_shape=jax.ShapeDtypeStruct((b, t, w), F32),
        compiler_params=_cparams("parallel", "parallel"), name="mem_attn")(q, mem_k, mem_v)


def _out_proj_kernel(*refs, layout, aliased):
    tok_ref, gate_ref, att_ref, x_ref, wo_ref, g_ref, wr_hi_ref, wr_lo_ref, br_ref = refs[:9]
    x1_out, h2_out, logit_out = refs[9 + aliased:]
    x1 = x_ref[...] + _dot(att_ref[...].astype(BF16), wo_ref[TOK_WIDTH:, :])
    if layout == "gla_heads":
        for hd in range(GLA_HEADS):
            gate = gate_ref[hd]
            mixed = (tok_ref[hd] * (gate * _sigmoid(gate))).astype(BF16)
            x1 = x1 + _dot(mixed, wo_ref[hd * GLA_DV:(hd + 1) * GLA_DV, :])
    else:
        if layout == "rwkv_pairs":
            tok = jnp.concatenate([tok_ref[p] for p in range(RWKV_PAIRS)], axis=1)
        else:
            tok = tok_ref[...]
        x1 = x1 + _dot((tok * gate_ref[...]).astype(BF16), wo_ref[:TOK_WIDTH, :])
    x1_out[...] = x1
    h2 = _rms(x1, g_ref[...])
    h2_out[...] = h2
    h_hi = h2.astype(BF16)
    h_lo = (h2 - h_hi.astype(F32)).astype(BF16)
    logit_out[...] = (_dot(h_hi, wr_hi_ref[...]) + _dot(h_lo, wr_hi_ref[...]) + _dot(h_hi, wr_lo_ref[...])
                      + br_ref[...])


def _out_proj(tok, gate, att, x, ws, layout, tm, h2_rows, h2_row_offset, h2_buffer=None):
    m = x.shape[0]
    aliased = h2_buffer is not None
    rows_spec = _row_spec(tm, TOK_WIDTH)
    tok_spec, gate_spec = {
        "rows": (rows_spec, rows_spec),
        "rwkv_pairs": (pl.BlockSpec((RWKV_PAIRS, tm, LANES), lambda i: (0, i, 0)), rows_spec),
        "gla_heads": (_head_row_spec(tm, GLA_DV), _head_row_spec(tm, GLA_DV))}[layout]
    in_specs = ([tok_spec, gate_spec, _row_spec(tm, MEM_WIDTH), _row_spec(tm, D_MODEL)]
                + [_full_spec(w.shape) for w in ws])
    args = [tok, gate, att, x, *ws]
    if aliased:
        in_specs.append(pl.BlockSpec(memory_space=pl.ANY))
        args.append(h2_buffer)
    return pl.pallas_call(
        functools.partial(_out_proj_kernel, layout=layout, aliased=int(aliased)), grid=(m // tm,),
        in_specs=in_specs,
        out_specs=[_row_spec(tm, D_MODEL), _row_spec(tm, D_MODEL, h2_row_offset // tm), _row_spec(tm, ROUTER_LANES)],
        out_shape=[jax.ShapeDtypeStruct((m, D_MODEL), F32), jax.ShapeDtypeStruct((h2_rows, D_MODEL), F32),
                   jax.ShapeDtypeStruct((m, ROUTER_LANES), F32)],
        input_output_aliases={len(args) - 1: 1} if aliased else {},
        compiler_params=_cparams("parallel"), name="out_proj")(*args)


def _moe_kernel(tile_ref, exp_ref, nitem_ref, lo_ref, hi_ref, x_ref, gate_ref, wu_ref, wd_ref, o_ref, wu_scr, wd_scr):
    w = pl.program_id(0)
    prev = jnp.maximum(w - 1, 0)
    e = exp_ref[w]
    valid = w < nitem_ref[0]

    @pl.when(jnp.logical_and(valid, jnp.logical_or(w == 0, e != exp_ref[prev])))
    def _():
        wu_scr[...] = wu_ref[0, 0].astype(BF16)
        wd_scr[...] = wd_ref[0, 0].astype(BF16)

    @pl.when(jnp.logical_or(w == 0, tile_ref[w] != tile_ref[prev]))
    def _():
        o_ref[...] = jnp.zeros(o_ref.shape, F32)

    @pl.when(valid)
    def _():
        gu = _dot(x_ref[...].astype(BF16), wu_scr[...])
        g = gu[:, :EXPERT_FF]
        act = (g * _sigmoid(g) * gu[:, EXPERT_FF:]).astype(BF16)
        out = _dot(act, wd_scr[...]) * gate_ref[...]
        rows = tile_ref[w] * MOE_BLOCK + lax.broadcasted_iota(jnp.int32, (MOE_BLOCK, 1), 0)
        mine = jnp.logical_and(rows >= lo_ref[e], rows < hi_ref[e])
        o_ref[...] = o_ref[...] + jnp.where(mine, out, 0.0)


def _moe_ffn(xs, row_gate, item_tile, item_expert, n_items, lo, hi, w_up, w_down, layer):
    tm = MOE_BLOCK
    n_work = item_tile.shape[0]
    a = xs.shape[0]
    row_map = lambda w, tile, ex, ni, lo_, hi_: (tile[w], 0)
    exp_map = lambda w, tile, ex, ni, lo_, hi_: (layer, ex[w], 0, 0)
    grid_spec = pltpu.PrefetchScalarGridSpec(
        num_scalar_prefetch=5, grid=(n_work,),
        in_specs=[pl.BlockSpec((tm, D_MODEL), row_map),
                  pl.BlockSpec((tm, 1), row_map),
                  pl.BlockSpec((1, 1, D_MODEL, 2 * EXPERT_FF), exp_map),
                  pl.BlockSpec((1, 1, EXPERT_FF, D_MODEL), exp_map)],
        out_specs=pl.BlockSpec((tm, D_MODEL), row_map),
        scratch_shapes=[pltpu.VMEM((D_MODEL, 2 * EXPERT_FF), BF16), pltpu.VMEM((EXPERT_FF, D_MODEL), BF16)])
    return pl.pallas_call(
        _moe_kernel, grid_spec=grid_spec,
        out_shape=jax.ShapeDtypeStruct((a, D_MODEL), F32),
        compiler_params=_cparams("arbitrary"), name="moe_ffn")(
            item_tile, item_expert, n_items, lo, hi, xs, row_gate, w_up, w_down)


def _route(logits, n_prompt, tm):
    m = logits.shape[0]
    n_sample = m - n_prompt
    a = 2 * m
    gl = logits[:, :N_GROUPS]
    el = logits[:, N_GROUPS:N_GROUPS + N_EXPERTS]
    group = jnp.argmax(gl, -1).astype(jnp.int32)
    p_group = jnp.max(jax.nn.softmax(gl, -1), -1, keepdims=True)
    experts_row = jnp.arange(N_EXPERTS, dtype=jnp.int32)[None, :]
    masked = jnp.where(experts_row // EXPERTS_PER_GROUP == group[:, None], el, -jnp.inf)
    e1 = jnp.argmax(masked, -1).astype(jnp.int32)
    masked2 = jnp.where(experts_row == e1[:, None], -jnp.inf, masked)
    e2 = jnp.argmax(masked2, -1).astype(jnp.int32)
    top_val = jnp.stack([jnp.max(masked, -1), jnp.max(masked2, -1)], axis=-1)
    gate = p_group * jax.nn.softmax(top_val, -1)
    expert = jnp.stack([e1, e2], axis=-1)

    def by_id(t):
        return jnp.concatenate([t[:n_prompt, 0], t[:n_prompt, 1], t[n_prompt:, 0], t[n_prompt:, 1]])

    tok_of_id = jnp.asarray(np.concatenate([np.arange(n_prompt), np.arange(n_prompt),
                                            n_prompt + np.arange(n_sample), n_prompt + np.arange(n_sample)]), jnp.int32)
    flat_e = by_id(expert)
    ids = jnp.arange(a, dtype=jnp.int32)
    _, order, gate_sorted, tok_sorted = lax.sort((flat_e, ids, by_id(gate), tok_of_id), num_keys=1, is_stable=True)
    _, inv = lax.sort((order, ids), num_keys=1)
    experts = jnp.arange(N_EXPERTS, dtype=jnp.int32)
    counts = jnp.sum((flat_e[:, None] == experts[None, :]).astype(jnp.int32), axis=0)
    hi = jnp.cumsum(counts).astype(jnp.int32)
    lo = hi - counts
    n_tiles = a // tm
    first_tile = lo // tm
    tiles_of = jnp.where(counts > 0, (hi - 1) // tm - first_tile + 1, 0)
    item_end = jnp.cumsum(tiles_of).astype(jnp.int32)
    n_items = item_end[-1:]
    n_work = n_tiles + N_EXPERTS - 1
    w = jnp.minimum(jnp.arange(n_work, dtype=jnp.int32), n_items[0] - 1)
    item_expert = jnp.sum((item_end[None, :] <= w[:, None]).astype(jnp.int32), axis=1)
    onehot = (item_expert[:, None] == experts[None, :]).astype(jnp.int32)
    item_tile = jnp.sum(onehot * (first_tile - (item_end - tiles_of))[None, :], axis=1) + w
    return tok_sorted, gate_sorted.reshape(a, 1), inv, item_tile.astype(jnp.int32), item_expert, n_items, lo, hi


def _combine_kernel(x_ref, r0_ref, r1_ref, g_ref, *outs):
    x2 = x_ref[...] + (r0_ref[...] + r1_ref[...])
    outs[-1][...] = _rms(x2, g_ref[...])
    if len(outs) == 2:
        outs[0][...] = x2


def _combine(x1, gathered, first_row, g, tm, last_layer):
    m = x1.shape[0]
    spec = _row_spec(tm, D_MODEL)
    n_out = 1 if last_layer else 2
    outs = pl.pallas_call(
        _combine_kernel, grid=(m // tm,),
        in_specs=[spec, _row_spec(tm, D_MODEL, first_row // tm), _row_spec(tm, D_MODEL, (first_row + m) // tm),
                  _full_spec((1, D_MODEL))],
        out_specs=[spec] * n_out,
        out_shape=[jax.ShapeDtypeStruct((m, D_MODEL), F32)] * n_out,
        compiler_params=_cparams("parallel"), name="moe_combine")(x1, gathered, gathered, g.reshape(1, D_MODEL))
    return (None, outs[0]) if last_layer else tuple(outs)


def _to_scan_layout(t2d, b, t):
    return t2d.reshape(b, t, RWKV_HEADS, RWKV_N).transpose(1, 3, 0, 2).reshape(t, RWKV_N, b * RWKV_HEADS)


def _from_scan_layout(y, b, t):
    return y.reshape(t, RWKV_N, b, RWKV_HEADS).transpose(2, 0, 3, 1).reshape(b * t, TOK_WIDTH)


def _scan_param(p, b):
    return jnp.tile(p.reshape(RWKV_HEADS, RWKV_N).T, (1, b))


def _seq_scan_param(p, b):
    lanes = jnp.repeat(p.reshape(RWKV_HEADS, RWKV_N).T, b, axis=1)
    return jnp.pad(lanes, ((0, 0), (0, LANES - RWKV_HEADS * b)))


def _state_to_scan(s):
    b = s.shape[0]
    return s.transpose(2, 3, 0, 1).reshape(RWKV_N, RWKV_N, b * RWKV_HEADS)


def _state_from_scan(s, b):
    return s.reshape(RWKV_N, RWKV_N, b, RWKV_HEADS).transpose(2, 3, 0, 1)


def kernel(x_prompt, x_sample, mem_prompt, state_rwkv_S, state_rwkv_shift, state_gla_S, cache_mem_k, cache_mem_v, norm_mix_g, norm_ffn_g, norm_mem_g, norm_final_g, w_in, w_out, w_mem_kv, rw_mu, rw_w0, rw_w1, rw_w2, rw_a0, rw_a1, rw_a2, rw_g1, rw_g2, rw_k_k, rw_k_a, rw_r_k, rw_ln_g, rw_ln_b, gla_a1, gla_a2, gla_ab, gla_norm_g, router_wg, router_bg, router_we, router_be, exp_w_up, exp_w_down):
    bp, tp, _ = x_prompt.shape
    bs, ts, _ = x_sample.shape
    assert ts == 1 and tp % GLA_CHUNK == 0 and tp % SCAN_TIME_BLOCK == 0
    np_ = bp * tp
    ns = bs * ts
    m = np_ + ns
    assert np_ % LIGHT_BLOCK == 0 and ns % SAMPLE_BLOCK == 0 and (2 * m) % MOE_BLOCK == 0
    depth = w_in.shape[0]
    nh = GLA_HEADS
    tw = TOK_WIDTH
    bf = lambda t_: t_.astype(BF16)

    mem2d = mem_prompt.reshape(bp * N_MEM, D_MODEL)
    mem_kv = [_norm_matmul(mem2d, norm_mem_g[i], w_mem_kv[i], 512) for i in range(depth)]
    pk = jnp.stack([kv[:, :MEM_WIDTH].reshape(bp, N_MEM, MEM_WIDTH) for kv in mem_kv])
    pv = jnp.stack([kv[:, MEM_WIDTH:].reshape(bp, N_MEM, MEM_WIDTH) for kv in mem_kv])
    prompt_mem_k = pk.reshape(depth, bp, N_MEM, MEM_HEADS, MEM_HEAD_DIM)
    prompt_mem_v = pv.reshape(depth, bp, N_MEM, MEM_HEADS, MEM_HEAD_DIM)
    sk = cache_mem_k.reshape(depth, bs, N_MEM, MEM_WIDTH)
    sv = cache_mem_v.reshape(depth, bs, N_MEM, MEM_WIDTH)

    x_p = x_prompt.reshape(np_, D_MODEL)
    x_s = x_sample.reshape(ns, D_MODEL)
    h_p = None
    h_s = _norm(x_s, norm_mix_g[0], SAMPLE_BLOCK)

    w_router = jnp.zeros((depth, D_MODEL, ROUTER_LANES), F32)
    w_router = w_router.at[:, :, :N_GROUPS].set(router_wg).at[:, :, N_GROUPS:N_GROUPS + N_EXPERTS].set(router_we)
    b_router = jnp.zeros((depth, 1, ROUTER_LANES), F32)
    b_router = b_router.at[:, 0, :N_GROUPS].set(router_bg).at[:, 0, N_GROUPS:N_GROUPS + N_EXPERTS].set(router_be)

    p_rw_S, p_rw_shift, p_gla_S, s_rw_S, s_rw_shift, s_gla_S = [], [], [], [], [], []
    pending_p = pending_s = None
    for i in range(depth):
        j = i // 2
        if i % 2 == 0:
            wi = w_in[i]
            ws = [rw_mu[j], bf(wi[:, :tw]), bf(wi[:, tw:2 * tw]), bf(wi[:, 2 * tw:3 * tw]), bf(wi[:, 3 * tw:]),
                  bf(rw_w1[j]), bf(rw_w2[j]), rw_w0[j].reshape(1, tw), bf(rw_a1[j]), bf(rw_a2[j]),
                  rw_a0[j].reshape(1, tw), bf(rw_g1[j]), bf(rw_g2[j])]
            if h_p is None:
                *rkvda_p, gate_p, qm_p = _rwkv_proj(x_p, None, ws, PROJ_BLOCK, tp, norm_g=norm_mix_g[i])
                shift_p = _norm(x_prompt[:, -1, :], norm_mix_g[i], bp)
            else:
                *rkvda_p, gate_p, qm_p = _rwkv_proj(h_p, None, ws, PROJ_BLOCK, tp)
                shift_p = h_p.reshape(bp, tp, D_MODEL)[:, -1]
            *rkvda_s, gate_s, qm_s = _rwkv_proj(h_s, state_rwkv_shift[j], ws, SAMPLE_BLOCK, ts)
            pvec = [rw_k_k[j], rw_k_a[j], rw_r_k[j], rw_ln_g[j], rw_ln_b[j]]
            tok_p, sp = _rwkv_prompt_mixer(*rkvda_p, *[_seq_scan_param(p, bp) for p in pvec], bp, tp, SCAN_TIME_BLOCK)
            ys, ss = _rwkv_scan(*[_to_scan_layout(t_, bs, ts) for t_ in rkvda_s], [_scan_param(p, bs) for p in pvec],
                                _state_to_scan(state_rwkv_S[j]), LANES, 1)
            tok_s = _from_scan_layout(ys, bs, ts)
            p_rw_S.append(sp[:, :, :RWKV_HEADS * bp].reshape(RWKV_N, RWKV_N, RWKV_HEADS, bp).transpose(3, 2, 0, 1))
            s_rw_S.append(_state_from_scan(ss, bs))
            p_rw_shift.append(shift_p)
            s_rw_shift.append(h_s)
            layout_p, layout_s = "rwkv_pairs", "rows"
        else:
            ws = _gla_weights(w_in[i], gla_a1[j], gla_a2[j], gla_ab[j])
            if pending_p is None:
                q_p, k_p, v_p, gate_p, la_p, qm_p = _gla_proj(h_p, ws, PROJ_BLOCK)
                q_s, k_s, v_s, gate_s, la_s, qm_s = _gla_proj(h_s, ws, SAMPLE_BLOCK)
            else:
                q_p, k_p, v_p, gate_p, la_p, qm_p, x_p = _gla_proj(None, ws, PROJ_BLOCK, pending_p)
                q_s, k_s, v_s, gate_s, la_s, qm_s, x_s = _gla_proj(None, ws, SAMPLE_BLOCK, pending_s)
            tok_p, sp_t = _gla_chunk_scan(q_p, k_p, v_p, la_p, gla_norm_g[j], bp, tp)
            rows_of = lambda t_: t_.transpose(1, 0, 2).reshape(bs * nh, t_.shape[2])
            os_, ss = _gla_step(rows_of(q_s), rows_of(k_s), rows_of(v_s), rows_of(la_s), gla_norm_g[j], state_gla_S, j)
            tok_s = os_.reshape(bs, nh, GLA_DV).transpose(1, 0, 2)
            p_gla_S.append(sp_t.transpose(0, 1, 3, 2))
            s_gla_S.append(ss)
            layout_p = layout_s = "gla_heads"

        att_p = _mem_attn(qm_p.reshape(bp, tp, MEM_WIDTH), pk, pv, i, 512).reshape(np_, MEM_WIDTH)
        att_s = _mem_attn(qm_s.reshape(bs, ts, MEM_WIDTH), sk, sv, i, 1).reshape(ns, MEM_WIDTH)
        wr_hi = bf(w_router[i])
        wr_lo = bf(w_router[i] - wr_hi.astype(F32))
        ws = [bf(w_out[i]), norm_ffn_g[i].reshape(1, D_MODEL), wr_hi, wr_lo, b_router[i]]
        x1_p, h2, logits_p = _out_proj(tok_p, gate_p, att_p, x_p, ws, layout_p, PROJ_BLOCK, m, 0)
        x1_s, h2, logits_s = _out_proj(tok_s, gate_s, att_s, x_s, ws, layout_s, SAMPLE_BLOCK, m, np_, h2_buffer=h2)
        tok_sorted, gate_sorted, inv, item_tile, item_expert, n_items, lo, hi = _route(
            jnp.concatenate([logits_p, logits_s], axis=0), np_, MOE_BLOCK)
        rows = _moe_ffn(h2[tok_sorted], gate_sorted, item_tile, item_expert, n_items, lo, hi,
                        exp_w_up, exp_w_down, i)
        gathered = rows[inv]
        g_next = norm_mix_g[i + 1] if i + 1 < depth else norm_final_g
        if i + 1 < depth and (i + 1) % 2 == 1:
            pending_p = (x1_p, gathered, 0, g_next)
            pending_s = (x1_s, gathered, 2 * np_, g_next)
        else:
            pending_p = pending_s = None
            x_p, h_p = _combine(x1_p, gathered, 0, g_next, LIGHT_BLOCK, i + 1 == depth)
            x_s, h_s = _combine(x1_s, gathered, 2 * np_, g_next, SAMPLE_BLOCK, i + 1 == depth)

    y_prompt = h_p.reshape(bp, tp, D_MODEL)
    y_sample = h_s.reshape(bs, ts, D_MODEL)
    return (y_prompt, y_sample, jnp.stack(p_rw_S), jnp.stack(p_rw_shift), jnp.stack(p_gla_S),
            prompt_mem_k, prompt_mem_v, jnp.stack(s_rw_S), jnp.stack(s_rw_shift), jnp.stack(s_gla_S))
```

```python
import functools

import numpy as np
import jax
import jax.numpy as jnp
from jax import lax
from jax.experimental import pallas as pl
from jax.experimental.pallas import tpu as pltpu

F32 = jnp.float32
BF16 = jnp.bfloat16

D_MODEL = 1024
TOK_WIDTH = 768
MEM_WIDTH = 256
MEM_HEADS = 4
MEM_HEAD_DIM = 64
N_MEM = 256
RWKV_HEADS = 12
RWKV_N = 64
RWKV_PAIRS = RWKV_HEADS // 2
RWKV_GN_EPS = 64e-5
GLA_HEADS = 4
GLA_KW = 384
GLA_DK = 96
GLA_DV = 192
GLA_TAU = 16.0
GLA_CHUNK = 64
GLA_TILE = 8
N_GROUPS = 4
EXPERTS_PER_GROUP = 8
N_EXPERTS = 32
EXPERT_FF = 512
NORM_EPS = 1e-6
ROUTER_LANES = 128
LANES = 128
SUBLANES = 8

PROJ_BLOCK = 512
LIGHT_BLOCK = 512
SAMPLE_BLOCK = 128
MOE_BLOCK = 256
SCAN_TIME_BLOCK = 32
VMEM_LIMIT = 56 * 1024 * 1024


def _cparams(*sem):
    return pltpu.CompilerParams(dimension_semantics=sem, vmem_limit_bytes=VMEM_LIMIT)


def _dot(a, b):
    return jnp.dot(a, b, preferred_element_type=F32)


def _dot_nt(a, b):
    return lax.dot_general(a, b, (((1,), (1,)), ((), ())), preferred_element_type=F32)


def _dot_tn(a, b):
    return lax.dot_general(a, b, (((0,), (0,)), ((), ())), preferred_element_type=F32)


def _rms(x, g):
    return x * lax.rsqrt(jnp.mean(x * x, axis=-1, keepdims=True) + NORM_EPS) * g


def _sigmoid(x):
    return 1.0 / (1.0 + jnp.exp(-x))


def _row_spec(tm, n, offset=0):
    return pl.BlockSpec((tm, n), lambda i: (i + offset, 0))


def _head_row_spec(tm, n):
    return pl.BlockSpec((GLA_HEADS, tm, n), lambda i: (0, i, 0))


def _full_spec(shape):
    nd = len(shape)
    return pl.BlockSpec(shape, lambda *_: (0,) * nd)


def _norm_kernel(x_ref, g_ref, o_ref):
    o_ref[...] = _rms(x_ref[...], g_ref[...])


def _norm(x, g, tm):
    m, d = x.shape
    return pl.pallas_call(
        _norm_kernel, grid=(m // tm,),
        in_specs=[_row_spec(tm, d), _full_spec((1, d))],
        out_specs=_row_spec(tm, d),
        out_shape=jax.ShapeDtypeStruct((m, d), F32),
        compiler_params=_cparams("parallel"), name="rms_norm")(x, g.reshape(1, d))


def _norm_matmul_kernel(x_ref, g_ref, w_ref, o_ref):
    o_ref[...] = _dot(_rms(x_ref[...], g_ref[...]).astype(BF16), w_ref[...])


def _norm_matmul(x, g, w, tm):
    m, d = x.shape
    n = w.shape[1]
    return pl.pallas_call(
        _norm_matmul_kernel, grid=(m // tm,),
        in_specs=[_row_spec(tm, d), _full_spec((1, d)), _full_spec((d, n))],
        out_specs=_row_spec(tm, n),
        out_shape=jax.ShapeDtypeStruct((m, n), F32),
        compiler_params=_cparams("parallel"), name="norm_matmul")(x, g.reshape(1, d), w.astype(BF16))


def _rwkv_proj_kernel(h_ref, hp_ref, *refs, blocks_per_seq, norm_input):
    if norm_input:
        ng_ref, refs = refs[0], refs[1:]
        normed = lambda t_: _rms(t_, ng_ref[...])
    else:
        normed = lambda t_: t_
    (mu_ref, wr_ref, wk_ref, wv_ref, wq_ref, w1_ref, w2_ref, w0_ref, a1_ref, a2_ref, a0_ref, g1_ref, g2_ref,
     r_out, k_out, v_out, d_out, a_out, g_out, q_out) = refs

    def put(out, val):
        if blocks_per_seq:
            for p in range(RWKV_PAIRS):
                out[p] = val[:, p * LANES:(p + 1) * LANES]
        else:
            out[...] = val

    h = normed(h_ref[...])
    if blocks_per_seq:
        seq_start = (pl.program_id(0) % blocks_per_seq) == 0
        before = jnp.where(seq_start, 0.0, normed(hp_ref[SUBLANES - 1:SUBLANES, :]))
        row = lax.broadcasted_iota(jnp.int32, (h.shape[0], 1), 0)
        hp = jnp.where(row == 0, before, pltpu.roll(h, 1, axis=0))
    else:
        hp = hp_ref[...]
    xx = hp - h

    def mix(j):
        return (h + xx * mu_ref[j:j + 1, :]).astype(BF16)

    put(r_out, _dot(mix(0), wr_ref[...]))
    wl = w0_ref[...] + _dot(jnp.tanh(_dot(mix(1), w1_ref[...])).astype(BF16), w2_ref[...])
    put(k_out, _dot(mix(2), wk_ref[...]))
    put(v_out, _dot(mix(3), wv_ref[...]))
    al = a0_ref[...] + _dot(_dot(mix(4), a1_ref[...]).astype(BF16), a2_ref[...])
    g_out[...] = _dot(_sigmoid(_dot(mix(5), g1_ref[...])).astype(BF16), g2_ref[...])
    q_out[...] = _dot(h.astype(BF16), wq_ref[...])
    z = -wl
    softplus = jnp.maximum(z, 0.0) + jnp.log(1.0 + jnp.exp(-jnp.abs(z)))
    put(d_out, jnp.exp(-jnp.exp(-softplus - 0.5)))
    put(a_out, _sigmoid(al))


def _rwkv_proj(h, h_prev, ws, tm, seq_len, norm_g=None):
    m = h.shape[0]
    if norm_g is not None:
        assert h_prev is None
        ws = [norm_g.reshape(1, D_MODEL)] + list(ws)
    tw = TOK_WIDTH
    if h_prev is None:
        assert seq_len % tm == 0 and tm % SUBLANES == 0
        per8 = tm // SUBLANES
        hp_spec = pl.BlockSpec((SUBLANES, D_MODEL), lambda i: (jnp.maximum(i * per8 - 1, 0), 0))
        h_prev, blocks_per_seq = h, seq_len // tm
    else:
        hp_spec, blocks_per_seq = _row_spec(tm, D_MODEL), 0
    if blocks_per_seq:
        scan_spec = pl.BlockSpec((RWKV_PAIRS, tm, LANES), lambda i: (0, i, 0))
        scan_shape = jax.ShapeDtypeStruct((RWKV_PAIRS, m, LANES), F32)
    else:
        scan_spec, scan_shape = _row_spec(tm, tw), jax.ShapeDtypeStruct((m, tw), F32)
    return pl.pallas_call(
        functools.partial(_rwkv_proj_kernel, blocks_per_seq=blocks_per_seq, norm_input=norm_g is not None),
        grid=(m // tm,),
        in_specs=[_row_spec(tm, D_MODEL), hp_spec] + [_full_spec(w.shape) for w in ws],
        out_specs=[scan_spec] * 5 + [_row_spec(tm, tw), _row_spec(tm, MEM_WIDTH)],
        out_shape=[scan_shape] * 5 + [jax.ShapeDtypeStruct((m, tw), F32), jax.ShapeDtypeStruct((m, MEM_WIDTH), F32)],
        compiler_params=_cparams("parallel"), name="rwkv_proj")(h, h_prev, *ws)


def _pad_lanes(x):
    short = LANES - x.shape[-1]
    if short == 0:
        return x
    return jnp.concatenate([x, jnp.zeros(x.shape[:-1] + (short,), x.dtype)], axis=-1)


def _rwkv_scan_kernel(r_ref, k_ref, v_ref, d_ref, a_ref, kkp_ref, kap_ref, rkp_ref, lng_ref, lnb_ref, s0_ref,
                      y_ref, sfin_ref, s_scr, v_scr, yrow_scr, *, tc):
    n = RWKV_N
    nl = r_ref.shape[-1]

    @pl.when(pl.program_id(1) == 0)
    def _():
        s_scr[...] = _pad_lanes(s0_ref[...])

    kkp = _pad_lanes(kkp_ref[...])
    kap = _pad_lanes(kap_ref[...])
    rkp = _pad_lanes(rkp_ref[...])
    lng = _pad_lanes(lng_ref[...])
    lnb = _pad_lanes(lnb_ref[...])

    def step(t, carry):
        r_t = _pad_lanes(r_ref[t])
        k_t = _pad_lanes(k_ref[t])
        v_t = _pad_lanes(v_ref[t])
        d_t = _pad_lanes(d_ref[t])
        a_t = _pad_lanes(a_ref[t])
        v_scr[...] = v_t
        kkr = k_t * kkp
        nrm = jnp.maximum(jnp.sqrt(jnp.sum(kkr * kkr, axis=0, keepdims=True)), 1e-12)
        kk = kkr * (1.0 / nrm)
        k2 = k_t * (1.0 + (a_t - 1.0) * kap)
        nkk = -kk
        b_t = kk * a_t

        def ibody(i, c):
            s_i = s_scr[i]
            sa = jnp.sum(s_i * nkk, axis=0, keepdims=True)
            v_i = v_scr[pl.ds(i, 1), :]
            s_n = s_i * d_t + sa * b_t + v_i * k2
            s_scr[i] = s_n
            yrow_scr[pl.ds(i, 1), :] = jnp.sum(s_n * r_t, axis=0, keepdims=True)
            return c

        lax.fori_loop(0, n, ibody, 0, unroll=8)
        y = yrow_scr[...]
        yc = y - jnp.mean(y, axis=0, keepdims=True)
        var = jnp.mean(yc * yc, axis=0, keepdims=True)
        gn = yc * lax.rsqrt(var + RWKV_GN_EPS) * lng + lnb
        bonus = jnp.sum(r_t * k2 * rkp, axis=0, keepdims=True) * v_t
        y_ref[t] = (gn + bonus)[:, :nl]
        return carry

    lax.fori_loop(0, tc, step, 0)

    @pl.when(pl.program_id(1) == pl.num_programs(1) - 1)
    def _():
        sfin_ref[...] = s_scr[:, :, :nl]


def _rwkv_scan(r, k, v, d, a, params, s0, lane_block, tc):
    t, n, l = r.shape
    seq = pl.BlockSpec((tc, n, lane_block), lambda li, ti: (ti, 0, li))
    par = pl.BlockSpec((n, lane_block), lambda li, ti: (0, li))
    st = pl.BlockSpec((n, n, lane_block), lambda li, ti: (0, 0, li))
    return pl.pallas_call(
        functools.partial(_rwkv_scan_kernel, tc=tc), grid=(l // lane_block, t // tc),
        in_specs=[seq] * 5 + [par] * 5 + [st],
        out_specs=[seq, st],
        out_shape=[jax.ShapeDtypeStruct((t, n, l), F32), jax.ShapeDtypeStruct((n, n, l), F32)],
        scratch_shapes=[pltpu.VMEM((n, n, LANES), F32), pltpu.VMEM((n, LANES), F32), pltpu.VMEM((n, LANES), F32)],
        compiler_params=_cparams("parallel", "arbitrary"), name="rwkv_scan")(r, k, v, d, a, *params, s0)


PREP_TILES = 6
DECAY_WINDOW = 16


def _rwkv_prep_kernel(r_ref, k_ref, v_ref, d_ref, a_ref, kkp_ref, kap_ref, o_ref, *, tc, nb):
    n = RWKV_N
    rows_per_pair = nb * tc
    ins = [ref.reshape(RWKV_PAIRS * rows_per_pair, LANES) for ref in (r_ref, k_ref, v_ref, d_ref, a_ref)]
    zero_rows = jnp.zeros((LANES - RWKV_HEADS * nb, LANES), F32)
    first_half = lax.broadcasted_iota(jnp.int32, (nb, LANES), 1) < n
    kkp = kkp_ref[...]
    kap = kap_ref[...]

    def load_transposed_pair(x2, t):
        pieces = []
        for p in range(RWKV_PAIRS):
            now = x2[pl.ds(p * rows_per_pair + t, nb, stride=tc), :]
            nxt = x2[pl.ds(p * rows_per_pair + t + 1, nb, stride=tc), :]
            pieces += [jnp.where(first_half, now, pltpu.roll(nxt, n, axis=1)),
                       jnp.where(first_half, pltpu.roll(now, n, axis=1), nxt)]
        both = jnp.concatenate(pieces + [zero_rows], axis=0).T
        return both[:n, :], both[n:, :]

    def emit(t, c_prev, r_t, k_t, v_t, d_t, a_t):
        kkr = k_t * kkp
        nrm = jnp.maximum(jnp.sqrt(jnp.sum(kkr * kkr, axis=0, keepdims=True)), 1e-12)
        kk = kkr * (1.0 / nrm)
        k2 = k_t * (1.0 + (a_t - 1.0) * kap)
        c_prev = jnp.where(t % DECAY_WINDOW == 0, 1.0, c_prev)
        c_t = c_prev * d_t
        inv = 1.0 / jnp.maximum(c_t, 1e-30)
        o_ref[t, 0] = -kk * c_prev
        o_ref[t, 1] = c_t
        o_ref[t, 2] = kk * a_t * inv
        o_ref[t, 3] = k2 * inv
        o_ref[t, 4] = r_t * c_t
        o_ref[t, 5] = v_t
        return c_t

    def two_steps(u, c):
        t = 2 * u
        tiles = [load_transposed_pair(x2, t) for x2 in ins]
        c = emit(t, c, *[tile[0] for tile in tiles])
        return emit(t + 1, c, *[tile[1] for tile in tiles])

    lax.fori_loop(0, tc // 2, two_steps, jnp.ones((n, LANES), F32), unroll=2)


def _rwkv_state_scan_kernel(x_ref, rkp_ref, lng_ref, lnb_ref, y_ref, sfin_ref, s_scr, sa_scr, yrow_scr, *, tc):
    n = RWKV_N
    groups = n // SUBLANES

    @pl.when(pl.program_id(0) == 0)
    def _():
        s_scr[...] = jnp.zeros(s_scr.shape, F32)

    rkp = rkp_ref[...]
    lng = lng_ref[...]
    lnb = lnb_ref[...]
    sub = lax.broadcasted_iota(jnp.int32, (SUBLANES, LANES), 0)
    low4 = sub < 4
    low2 = (sub & 3) < 2
    low1 = (sub & 1) == 0

    def fold(x, y, dist, low):
        if dist == 4:
            return jnp.where(low, x, y) + pltpu.roll(jnp.where(low, y, x), 4, axis=0)
        return (jnp.where(low, x, pltpu.roll(y, dist, axis=0))
                + jnp.where(low, pltpu.roll(x, SUBLANES - dist, axis=0), y))

    def sublane_sums(ps):
        z = [fold(ps[0], ps[4], 4, low4), fold(ps[2], ps[6], 4, low4),
             fold(ps[1], ps[5], 4, low4), fold(ps[3], ps[7], 4, low4)]
        return fold(fold(z[0], z[1], 2, low2), fold(z[2], z[3], 2, low2), 1, low1)

    def tile_sum(x):
        acc = x[0:SUBLANES]
        for u in range(1, groups):
            acc = acc + x[u * SUBLANES:(u + 1) * SUBLANES]
        return acc

    def step(t, carry):
        nkk = x_ref[t, 0]
        b_t = x_ref[t, 2]
        k2 = x_ref[t, 3]
        r_t = x_ref[t, 4]
        for g in range(groups):
            sa_scr[g * SUBLANES:(g + 1) * SUBLANES, :] = sublane_sums(
                [tile_sum(s_scr[g * SUBLANES + u] * nkk) for u in range(SUBLANES)])
        for g in range(groups):
            ps = []
            for u in range(SUBLANES):
                i = g * SUBLANES + u
                s_n = s_scr[i] + sa_scr[pl.ds(i, 1), :] * b_t + x_ref[t, 5, pl.ds(i, 1), :] * k2
                s_scr[i] = s_n
                ps.append(tile_sum(s_n * r_t))
            yrow_scr[g * SUBLANES:(g + 1) * SUBLANES, :] = sublane_sums(ps)
        y = yrow_scr[...]
        yc = y - jnp.mean(y, axis=0, keepdims=True)
        var = jnp.mean(yc * yc, axis=0, keepdims=True)
        bonus = jnp.sum(r_t * k2 * rkp, axis=0, keepdims=True) * x_ref[t, 5]
        y_ref[t] = yc * lax.rsqrt(var + RWKV_GN_EPS) * lng + lnb + bonus

        @pl.when(t % DECAY_WINDOW == DECAY_WINDOW - 1)
        def _():
            c_t = x_ref[t, 1]
            for i in range(n):
                s_scr[i] = s_scr[i] * c_t

        return carry

    lax.fori_loop(0, tc, step, 0)

    @pl.when(pl.program_id(0) == pl.num_programs(0) - 1)
    def _():
        sfin_ref[...] = s_scr[...]


def _rwkv_unprep_kernel(y_ref, o_ref, *, tc, nb):
    n = RWKV_N
    rows_per_pair = nb * tc
    o2 = o_ref.reshape(RWKV_PAIRS * rows_per_pair, LANES)
    first_half = lax.broadcasted_iota(jnp.int32, (nb, LANES), 1) < n

    def two_steps(u, carry):
        t = 2 * u
        w = jnp.concatenate([y_ref[t], y_ref[t + 1]], axis=0).T
        for p in range(RWKV_PAIRS):
            even = w[(2 * p) * nb:(2 * p + 1) * nb, :]
            odd = w[(2 * p + 1) * nb:(2 * p + 2) * nb, :]
            o2[pl.ds(p * rows_per_pair + t, nb, stride=tc), :] = jnp.where(
                first_half, even, pltpu.roll(odd, n, axis=1))
            o2[pl.ds(p * rows_per_pair + t + 1, nb, stride=tc), :] = jnp.where(
                first_half, pltpu.roll(even, n, axis=1), odd)
        return carry

    lax.fori_loop(0, tc // 2, two_steps, 0, unroll=2)


def _rwkv_prompt_mixer(r, k, v, d, a, kkp, kap, rkp, lng, lnb, nb, t, tc):
    n = RWKV_N
    assert tc % SUBLANES == 0 and tc % 4 == 0 and t % tc == 0 and RWKV_HEADS * nb <= LANES
    assert tc % DECAY_WINDOW == 0
    grid = (t // tc,)
    tok = pl.BlockSpec((RWKV_PAIRS, nb, tc, LANES), lambda ti: (0, 0, ti, 0))
    par = pl.BlockSpec((n, LANES), lambda ti: (0, 0))
    tiles = pl.BlockSpec((tc, PREP_TILES, n, LANES), lambda ti: (ti, 0, 0, 0))
    ytile = pl.BlockSpec((tc, n, LANES), lambda ti: (ti, 0, 0))
    prepared = pl.pallas_call(
        functools.partial(_rwkv_prep_kernel, tc=tc, nb=nb), grid=grid,
        in_specs=[tok] * 5 + [par] * 2, out_specs=tiles,
        out_shape=jax.ShapeDtypeStruct((t, PREP_TILES, n, LANES), F32),
        compiler_params=_cparams("parallel"), name="rwkv_prep")(
            *[x.reshape(RWKV_PAIRS, nb, t, LANES) for x in (r, k, v, d, a)], kkp, kap)
    y, s = pl.pallas_call(
        functools.partial(_rwkv_state_scan_kernel, tc=tc), grid=grid,
        in_specs=[tiles, par, par, par],
        out_specs=[ytile, pl.BlockSpec((n, n, LANES), lambda ti: (0, 0, 0))],
        out_shape=[jax.ShapeDtypeStruct((t, n, LANES), F32), jax.ShapeDtypeStruct((n, n, LANES), F32)],
        scratch_shapes=[pltpu.VMEM((n, n, LANES), F32), pltpu.VMEM((n, LANES), F32), pltpu.VMEM((n, LANES), F32)],
        compiler_params=_cparams("arbitrary"), name="rwkv_state_scan")(prepared, rkp, lng, lnb)
    tok_out = pl.pallas_call(
        functools.partial(_rwkv_unprep_kernel, tc=tc, nb=nb), grid=grid,
        in_specs=[ytile], out_specs=tok,
        out_shape=jax.ShapeDtypeStruct((RWKV_PAIRS, nb, t, LANES), F32),
        compiler_params=_cparams("parallel"), name="rwkv_unprep")(y)
    return tok_out.reshape(RWKV_PAIRS, nb * t, LANES), s


def _gla_proj_kernel(*refs, fused_combine):
    if fused_combine:
        x_ref, r0_ref, r1_ref, g_ref = refs[:4]
        refs, x_out = refs[4:-1], refs[-1]
        x2 = x_ref[...] + (r0_ref[...] + r1_ref[...])
        x_out[...] = x2
        h = _rms(x2, g_ref[...])
    else:
        h, refs = refs[0][...], refs[1:]
    (wq_ref, wk_ref, wv_ref, wr_ref, wm_ref, a1_ref, a2_ref, ab_ref,
     q_out, k_out, v_out, r_out, la_out, qm_out) = refs
    hb = h.astype(BF16)
    low = _dot(hb, a1_ref[...]).astype(BF16)
    for hd in range(GLA_HEADS):
        q_out[hd] = _dot(hb, wq_ref[hd])
        k_out[hd] = _dot(hb, wk_ref[hd])
        v_out[hd] = _dot(hb, wv_ref[hd])
        r_out[hd] = _dot(hb, wr_ref[hd])
        x = _dot(low, a2_ref[hd]) + ab_ref[hd]
        log_sigmoid = jnp.minimum(x, 0.0) - jnp.log(1.0 + jnp.exp(-jnp.abs(x)))
        la_out[hd] = log_sigmoid / GLA_TAU
    qm_out[...] = _dot(hb, wm_ref[...])


def _gla_weights(w_in, a1, a2, ab):
    kw, tw, nh = GLA_KW, TOK_WIDTH, GLA_HEADS

    def heads(w, d):
        return w.reshape(w.shape[0], nh, d).transpose(1, 0, 2)

    return [heads(w_in[:, :kw], GLA_DK).astype(BF16), heads(w_in[:, kw:2 * kw], GLA_DK).astype(BF16),
            heads(w_in[:, 2 * kw:2 * kw + tw], GLA_DV).astype(BF16),
            heads(w_in[:, 2 * kw + tw:3 * tw], GLA_DV).astype(BF16), w_in[:, 3 * tw:].astype(BF16),
            a1.astype(BF16), heads(a2, GLA_DK).astype(BF16), ab.reshape(nh, 1, GLA_DK)]


def _gla_proj(h, ws, tm, pending=None):
    nh = GLA_HEADS
    widths = [GLA_DK, GLA_DK, GLA_DV, GLA_DV, GLA_DK]
    if pending is None:
        m = h.shape[0]
        lead_specs, lead_args = [_row_spec(tm, D_MODEL)], [h]
    else:
        x1, gathered, first_row, norm_g = pending
        m = x1.shape[0]
        lead_specs = [_row_spec(tm, D_MODEL), _row_spec(tm, D_MODEL, first_row // tm),
                      _row_spec(tm, D_MODEL, (first_row + m) // tm), _full_spec((1, D_MODEL))]
        lead_args = [x1, gathered, gathered, norm_g.reshape(1, D_MODEL)]
    out_specs = [_head_row_spec(tm, w) for w in widths] + [_row_spec(tm, MEM_WIDTH)]
    out_shape = [jax.ShapeDtypeStruct((nh, m, w), F32) for w in widths] + [jax.ShapeDtypeStruct((m, MEM_WIDTH), F32)]
    if pending is not None:
        out_specs.append(_row_spec(tm, D_MODEL))
        out_shape.append(jax.ShapeDtypeStruct((m, D_MODEL), F32))
    return pl.pallas_call(
        functools.partial(_gla_proj_kernel, fused_combine=pending is not None), grid=(m // tm,),
        in_specs=lead_specs + [_full_spec(w.shape) for w in ws],
        out_specs=out_specs, out_shape=out_shape,
        compiler_params=_cparams("parallel"), name="gla_proj")(*lead_args, *ws)


def _gla_out_norm(o, g):
    return o * lax.rsqrt(jnp.mean(o * o, axis=-1, keepdims=True) + NORM_EPS) * g


def _gla_chunk_kernel(q_ref, k_ref, v_ref, la_ref, ng_ref, o_ref, sfin_ref, st_scr):
    c, dk = GLA_CHUNK, GLA_DK

    @pl.when(pl.program_id(1) == 0)
    def _():
        st_scr[...] = jnp.zeros(st_scr.shape, F32)

    row = lax.broadcasted_iota(jnp.int32, (c, c), 0)
    col = lax.broadcasted_iota(jnp.int32, (c, c), 1)
    tril = (row >= col).astype(F32)
    rr = lax.broadcasted_iota(jnp.int32, (c, dk), 0)
    ones_sum = jnp.ones((dk, LANES), BF16)

    heads = range(GLA_HEADS)
    tril_b = tril.astype(BF16)

    def cumsum_rows(la):
        hi = la.astype(BF16)
        r1 = la - hi.astype(F32)
        mid = r1.astype(BF16)
        lo = (r1 - mid.astype(F32)).astype(BF16)
        return _dot(tril_b, hi) + _dot(tril_b, mid) + _dot(tril_b, lo)

    def tile_roll(x, dlt):
        return pltpu.roll(x.reshape(c // GLA_TILE, GLA_TILE, dk), dlt, axis=1).reshape(c, dk)

    k = [k_ref[hd] for hd in heads]
    vb = [v_ref[hd].astype(BF16) for hd in heads]
    q = [q_ref[hd] * (dk ** -0.5) for hd in heads]
    b = [cumsum_rows(la_ref[hd]) for hd in heads]
    st = [st_scr[hd] for hd in heads]
    inter = [_dot_nt((q[hd] * jnp.exp(b[hd])).astype(BF16), st[hd].astype(BF16)) for hd in heads]

    att = [jnp.zeros((c, c), F32) for hd in heads]
    blk = c // 2
    while blk >= GLA_TILE:
        two = 2 * blk
        upper = (rr & (two - 1)) >= blk
        same_block = (row ^ col) < two
        parts = []
        for hd in heads:
            b_ref_rows = jnp.concatenate(
                [jnp.broadcast_to(b[hd][s0 + blk - 1:s0 + blk, :], (two, dk)) for s0 in range(0, c, two)], axis=0)
            q_l = jnp.where(upper, q[hd] * jnp.exp(jnp.minimum(b[hd] - b_ref_rows, 0.0)), 0.0).astype(BF16)
            k_l = jnp.where(upper, 0.0, k[hd] * jnp.exp(jnp.minimum(b_ref_rows - b[hd], 0.0))).astype(BF16)
            parts.append(_dot_nt(q_l, k_l))
        att = [att[hd] + jnp.where(same_block, parts[hd], 0.0) for hd in heads]
        blk //= 2

    sums = []
    for hd in heads:
        prods = [(q[hd] * k[hd]).astype(BF16)]
        for dlt in range(1, GLA_TILE):
            p = q[hd] * tile_roll(k[hd], dlt) * jnp.exp(jnp.minimum(b[hd] - tile_roll(b[hd], dlt), 0.0))
            prods.append(jnp.where((rr & (GLA_TILE - 1)) >= dlt, p, 0.0).astype(BF16))
        sums.append(_dot(jnp.concatenate(prods, axis=0), ones_sum))
    for dlt in range(GLA_TILE):
        on_diag = col == row - dlt
        att = [att[hd] + jnp.where(on_diag, sums[hd][dlt * c:(dlt + 1) * c, :c], 0.0) for hd in heads]

    o = [inter[hd] + _dot(att[hd].astype(BF16), vb[hd]) for hd in heads]
    for hd in heads:
        o_ref[hd] = _gla_out_norm(o[hd], ng_ref[...])
    for hd in heads:
        b_end = b[hd][c - 1:c, :]
        kd = (k[hd] * jnp.exp(b_end - b[hd])).astype(BF16)
        st_scr[hd] = jnp.exp(b_end) * st[hd] + _dot_tn(vb[hd], kd)

    @pl.when(pl.program_id(1) == pl.num_programs(1) - 1)
    def _():
        sfin_ref[0] = st_scr[...]


def _gla_chunk_scan(q, k, v, la, norm_g, batch, t):
    nh, dk, dv, c = GLA_HEADS, GLA_DK, GLA_DV, GLA_CHUNK
    nc = t // c
    kspec = pl.BlockSpec((nh, c, dk), lambda i, j: (0, i * nc + j, 0))
    vspec = pl.BlockSpec((nh, c, dv), lambda i, j: (0, i * nc + j, 0))
    sspec = pl.BlockSpec((1, nh, dv, dk), lambda i, j: (i, 0, 0, 0))
    return pl.pallas_call(
        _gla_chunk_kernel, grid=(batch, nc),
        in_specs=[kspec, kspec, vspec, kspec, pl.BlockSpec((1, dv), lambda i, j: (0, 0))],
        out_specs=[vspec, sspec],
        out_shape=[jax.ShapeDtypeStruct((nh, batch * t, dv), F32), jax.ShapeDtypeStruct((batch, nh, dv, dk), F32)],
        scratch_shapes=[pltpu.VMEM((nh, dv, dk), F32)],
        compiler_params=_cparams("parallel", "arbitrary"), name="gla_chunk")(q, k, v, la, norm_g.reshape(1, dv))


GLA_STEP_GROUP = 8


def _gla_step_kernel(q_ref, k_ref, v_ref, la_ref, ng_ref, s0_ref, o_ref, s_ref):
    g_n, dk = GLA_STEP_GROUP, GLA_DK

    def columns(ref):
        x = jnp.concatenate([ref[...], jnp.zeros((g_n, LANES - dk), F32)], axis=1)
        return jnp.concatenate([x, jnp.zeros((LANES - g_n, LANES), F32)], axis=0).T

    q_cols, k_cols, la_cols = columns(q_ref), columns(k_ref), columns(la_ref)
    v = v_ref[...]
    rows = []
    for g in range(g_n):
        kv = k_cols[:dk, g:g + 1].astype(BF16).astype(F32) * v[g:g + 1, :].astype(BF16).astype(F32)
        bi, hi = divmod(g, GLA_HEADS)
        s_new = jnp.exp(la_cols[:dk, g:g + 1]) * s0_ref[bi, hi] + kv
        s_ref[bi, hi] = s_new
        o = jnp.sum((q_cols[:dk, g:g + 1] * (dk ** -0.5)) * s_new, axis=0, keepdims=True)
        rows.append(_gla_out_norm(o, ng_ref[...]))
    o_ref[...] = jnp.concatenate(rows, axis=0)


def _gla_step(q, k, v, la, norm_g, states, layer):
    bh, dk = q.shape
    dv = v.shape[1]
    g, nh = GLA_STEP_GROUP, GLA_HEADS
    b = bh // nh
    kspec = _row_spec(g, dk)
    vspec = _row_spec(g, dv)
    return pl.pallas_call(
        _gla_step_kernel, grid=(bh // g,),
        in_specs=[kspec, kspec, vspec, kspec, _full_spec((1, dv)),
                  pl.BlockSpec((None, g // nh, nh, dk, dv), lambda i: (layer, i, 0, 0, 0))],
        out_specs=[vspec, pl.BlockSpec((g // nh, nh, dk, dv), lambda i: (i, 0, 0, 0))],
        out_shape=[jax.ShapeDtypeStruct((bh, dv), F32), jax.ShapeDtypeStruct((b, nh, dk, dv), F32)],
        compiler_params=_cparams("parallel"), name="gla_step")(q, k, v, la, norm_g.reshape(1, dv), states)


def _mem_attn_kernel(q_ref, k_ref, v_ref, o_ref, *, memory_transposed):
    q = q_ref[0]
    k = k_ref[0].astype(BF16)
    ones = jnp.ones((LANES, N_MEM) if memory_transposed else (N_MEM, LANES), BF16)
    v_ones = jnp.concatenate([v_ref[0].astype(BF16), ones], axis=0 if memory_transposed else 1)
    scores, values = (_dot, _dot_nt) if memory_transposed else (_dot_nt, _dot)
    head_of_lane = lax.broadcasted_iota(jnp.int32, (1, MEM_WIDTH), 1) // MEM_HEAD_DIM
    heads = range(MEM_HEADS)
    mine = [head_of_lane == h for h in heads]
    qh = [jnp.where(mine[h], q, 0.0).astype(BF16) for h in heads]
    s = [scores(qh[h], k) * (MEM_HEAD_DIM ** -0.5) for h in heads]
    e = [jnp.exp(s[h] - jnp.max(s[h], axis=-1, keepdims=True)).astype(BF16) for h in heads]
    ev = [values(e[h], v_ones) for h in heads]
    out = jnp.zeros(q.shape, F32)
    for h in heads:
        inv = 1.0 / ev[h][:, MEM_WIDTH:]
        out = out + jnp.where(mine[h], ev[h][:, :MEM_WIDTH] * jnp.concatenate([inv, inv], axis=1), 0.0)
    o_ref[0] = out


def _mem_attn(q, mem_k, mem_v, layer, tq, memory_transposed=False):
    b, t, w = q.shape
    qspec = pl.BlockSpec((1, tq, w), lambda i, j: (i, j, 0))
    mspec = pl.BlockSpec((None, 1, N_MEM, w), lambda i, j: (layer, i, 0, 0))
    return pl.pallas_call(
        functools.partial(_mem_attn_kernel, memory_transposed=memory_transposed), grid=(b, t // tq),
        in_specs=[qspec, mspec, mspec], out_specs=qspec,
        out_shape=jax.ShapeDtypeStruct((b, t, w), F32),
        compiler_params=_cparams("parallel", "parallel"), name="mem_attn")(q, mem_k, mem_v)


def _out_proj_kernel(*refs, layout, aliased):
    tok_ref, gate_ref, att_ref, x_ref, wo_ref, g_ref, wr_hi_ref, wr_lo_ref, br_ref = refs[:9]
    x1_out, h2_out, logit_out = refs[9 + aliased:]
    x1 = x_ref[...] + _dot(att_ref[...].astype(BF16), wo_ref[TOK_WIDTH:, :])
    if layout == "gla_heads":
        for hd in range(GLA_HEADS):
            gate = gate_ref[hd]
            mixed = (tok_ref[hd] * (gate * _sigmoid(gate))).astype(BF16)
            x1 = x1 + _dot(mixed, wo_ref[hd * GLA_DV:(hd + 1) * GLA_DV, :])
    else:
        if layout == "rwkv_pairs":
            tok = jnp.concatenate([tok_ref[p] for p in range(RWKV_PAIRS)], axis=1)
        else:
            tok = tok_ref[...]
        x1 = x1 + _dot((tok * gate_ref[...]).astype(BF16), wo_ref[:TOK_WIDTH, :])
    x1_out[...] = x1
    h2 = _rms(x1, g_ref[...])
    h2_out[...] = h2
    h_hi = h2.astype(BF16)
    h_lo = (h2 - h_hi.astype(F32)).astype(BF16)
    logit_out[...] = (_dot(h_hi, wr_hi_ref[...]) + _dot(h_lo, wr_hi_ref[...]) + _dot(h_hi, wr_lo_ref[...])
                      + br_ref[...])


def _out_proj(tok, gate, att, x, ws, layout, tm, h2_rows, h2_row_offset, h2_buffer=None):
    m = x.shape[0]
    aliased = h2_buffer is not None
    rows_spec = _row_spec(tm, TOK_WIDTH)
    tok_spec, gate_spec = {
        "rows": (rows_spec, rows_spec),
        "rwkv_pairs": (pl.BlockSpec((RWKV_PAIRS, tm, LANES), lambda i: (0, i, 0)), rows_spec),
        "gla_heads": (_head_row_spec(tm, GLA_DV), _head_row_spec(tm, GLA_DV))}[layout]
    in_specs = ([tok_spec, gate_spec, _row_spec(tm, MEM_WIDTH), _row_spec(tm, D_MODEL)]
                + [_full_spec(w.shape) for w in ws])
    args = [tok, gate, att, x, *ws]
    if aliased:
        in_specs.append(pl.BlockSpec(memory_space=pl.ANY))
        args.append(h2_buffer)
    return pl.pallas_call(
        functools.partial(_out_proj_kernel, layout=layout, aliased=int(aliased)), grid=(m // tm,),
        in_specs=in_specs,
        out_specs=[_row_spec(tm, D_MODEL), _row_spec(tm, D_MODEL, h2_row_offset // tm), _row_spec(tm, ROUTER_LANES)],
        out_shape=[jax.ShapeDtypeStruct((m, D_MODEL), F32), jax.ShapeDtypeStruct((h2_rows, D_MODEL), F32),
                   jax.ShapeDtypeStruct((m, ROUTER_LANES), F32)],
        input_output_aliases={len(args) - 1: 1} if aliased else {},
        compiler_params=_cparams("parallel"), name="out_proj")(*args)


def _moe_kernel(tile_ref, exp_ref, nitem_ref, lo_ref, hi_ref, x_ref, gate_ref, wu_ref, wd_ref, o_ref, wu_scr, wd_scr):
    w = pl.program_id(0)
    prev = jnp.maximum(w - 1, 0)
    e = exp_ref[w]
    valid = w < nitem_ref[0]

    @pl.when(jnp.logical_and(valid, jnp.logical_or(w == 0, e != exp_ref[prev])))
    def _():
        wu_scr[...] = wu_ref[0, 0].astype(BF16)
        wd_scr[...] = wd_ref[0, 0].astype(BF16)

    @pl.when(jnp.logical_or(w == 0, tile_ref[w] != tile_ref[prev]))
    def _():
        o_ref[...] = jnp.zeros(o_ref.shape, F32)

    @pl.when(valid)
    def _():
        gu = _dot(x_ref[...].astype(BF16), wu_scr[...])
        g = gu[:, :EXPERT_FF]
        act = (g * _sigmoid(g) * gu[:, EXPERT_FF:]).astype(BF16)
        gate_col = jnp.concatenate([gate_ref[...], jnp.zeros((LANES - 1, MOE_BLOCK), F32)], axis=0).T[:, :1]
        out = _dot(act, wd_scr[...]) * gate_col
        rows = tile_ref[w] * MOE_BLOCK + lax.broadcasted_iota(jnp.int32, (MOE_BLOCK, 1), 0)
        mine = jnp.logical_and(rows >= lo_ref[e], rows < hi_ref[e])
        o_ref[...] = o_ref[...] + jnp.where(mine, out, 0.0)


def _moe_ffn(xs, row_gate, item_tile, item_expert, n_items, lo, hi, w_up, w_down, layer):
    tm = MOE_BLOCK
    n_work = item_tile.shape[0]
    a = xs.shape[0]
    row_map = lambda w, tile, ex, ni, lo_, hi_: (tile[w], 0)
    exp_map = lambda w, tile, ex, ni, lo_, hi_: (layer, ex[w], 0, 0)
    grid_spec = pltpu.PrefetchScalarGridSpec(
        num_scalar_prefetch=5, grid=(n_work,),
        in_specs=[pl.BlockSpec((tm, D_MODEL), row_map),
                  pl.BlockSpec((None, 1, tm), lambda w, tile, ex, ni, lo_, hi_: (tile[w], 0, 0)),
                  pl.BlockSpec((1, 1, D_MODEL, 2 * EXPERT_FF), exp_map),
                  pl.BlockSpec((1, 1, EXPERT_FF, D_MODEL), exp_map)],
        out_specs=pl.BlockSpec((tm, D_MODEL), row_map),
        scratch_shapes=[pltpu.VMEM((D_MODEL, 2 * EXPERT_FF), BF16), pltpu.VMEM((EXPERT_FF, D_MODEL), BF16)])
    return pl.pallas_call(
        _moe_kernel, grid_spec=grid_spec,
        out_shape=jax.ShapeDtypeStruct((a, D_MODEL), F32),
        compiler_params=_cparams("arbitrary"), name="moe_ffn")(
            item_tile, item_expert, n_items, lo, hi, xs, row_gate.reshape(a // tm, 1, tm), w_up, w_down)


def _route(logits, n_prompt, tm):
    m = logits.shape[0]
    n_sample = m - n_prompt
    a = 2 * m
    gl = logits[:, :N_GROUPS]
    el = logits[:, N_GROUPS:N_GROUPS + N_EXPERTS]
    group = jnp.argmax(gl, -1).astype(jnp.int32)
    p_group = jnp.max(jax.nn.softmax(gl, -1), -1, keepdims=True)
    experts_row = jnp.arange(N_EXPERTS, dtype=jnp.int32)[None, :]
    masked = jnp.where(experts_row // EXPERTS_PER_GROUP == group[:, None], el, -jnp.inf)
    e1 = jnp.argmax(masked, -1).astype(jnp.int32)
    masked2 = jnp.where(experts_row == e1[:, None], -jnp.inf, masked)
    e2 = jnp.argmax(masked2, -1).astype(jnp.int32)
    top_val = jnp.stack([jnp.max(masked, -1), jnp.max(masked2, -1)], axis=-1)
    gate = p_group * jax.nn.softmax(top_val, -1)
    expert = jnp.stack([e1, e2], axis=-1)

    def by_id(t):
        return jnp.concatenate([t[:n_prompt, 0], t[:n_prompt, 1], t[n_prompt:, 0], t[n_prompt:, 1]])

    tok_of_id = jnp.asarray(np.concatenate([np.arange(n_prompt), np.arange(n_prompt),
                                            n_prompt + np.arange(n_sample), n_prompt + np.arange(n_sample)]), jnp.int32)
    flat_e = by_id(expert)
    ids = jnp.arange(a, dtype=jnp.int32)
    _, order, gate_sorted, tok_sorted = lax.sort((flat_e, ids, by_id(gate), tok_of_id), num_keys=1, is_stable=True)
    _, inv = lax.sort((order, ids), num_keys=1)
    experts = jnp.arange(N_EXPERTS, dtype=jnp.int32)
    counts = jnp.sum((flat_e[:, None] == experts[None, :]).astype(jnp.int32), axis=0)
    hi = jnp.cumsum(counts).astype(jnp.int32)
    lo = hi - counts
    n_tiles = a // tm
    first_tile = lo // tm
    tiles_of = jnp.where(counts > 0, (hi - 1) // tm - first_tile + 1, 0)
    item_end = jnp.cumsum(tiles_of).astype(jnp.int32)
    n_items = item_end[-1:]
    n_work = n_tiles + N_EXPERTS - 1
    w = jnp.minimum(jnp.arange(n_work, dtype=jnp.int32), n_items[0] - 1)
    item_expert = jnp.sum((item_end[None, :] <= w[:, None]).astype(jnp.int32), axis=1)
    onehot = (item_expert[:, None] == experts[None, :]).astype(jnp.int32)
    item_tile = jnp.sum(onehot * (first_tile - (item_end - tiles_of))[None, :], axis=1) + w
    return tok_sorted, gate_sorted, inv, item_tile.astype(jnp.int32), item_expert, n_items, lo, hi


def _combine_kernel(x_ref, r0_ref, r1_ref, g_ref, *outs):
    x2 = x_ref[...] + (r0_ref[...] + r1_ref[...])
    outs[-1][...] = _rms(x2, g_ref[...])
    if len(outs) == 2:
        outs[0][...] = x2


def _combine(x1, gathered, first_row, g, tm, last_layer):
    m = x1.shape[0]
    spec = _row_spec(tm, D_MODEL)
    n_out = 1 if last_layer else 2
    outs = pl.pallas_call(
        _combine_kernel, grid=(m // tm,),
        in_specs=[spec, _row_spec(tm, D_MODEL, first_row // tm), _row_spec(tm, D_MODEL, (first_row + m) // tm),
                  _full_spec((1, D_MODEL))],
        out_specs=[spec] * n_out,
        out_shape=[jax.ShapeDtypeStruct((m, D_MODEL), F32)] * n_out,
        compiler_params=_cparams("parallel"), name="moe_combine")(x1, gathered, gathered, g.reshape(1, D_MODEL))
    return (None, outs[0]) if last_layer else tuple(outs)


def _to_scan_layout(t2d, b, t):
    return t2d.reshape(b, t, RWKV_HEADS, RWKV_N).transpose(1, 3, 0, 2).reshape(t, RWKV_N, b * RWKV_HEADS)


def _from_scan_layout(y, b, t):
    return y.reshape(t, RWKV_N, b, RWKV_HEADS).transpose(2, 0, 3, 1).reshape(b * t, TOK_WIDTH)


def _scan_param(p, b):
    return jnp.tile(p.reshape(RWKV_HEADS, RWKV_N).T, (1, b))


def _seq_scan_param(p, b):
    lanes = jnp.repeat(p.reshape(RWKV_HEADS, RWKV_N).T, b, axis=1)
    return jnp.pad(lanes, ((0, 0), (0, LANES - RWKV_HEADS * b)))


def _state_to_scan(s):
    b = s.shape[0]
    return s.transpose(2, 3, 0, 1).reshape(RWKV_N, RWKV_N, b * RWKV_HEADS)


def _state_from_scan(s, b):
    return s.reshape(RWKV_N, RWKV_N, b, RWKV_HEADS).transpose(2, 3, 0, 1)


def kernel(x_prompt, x_sample, mem_prompt, state_rwkv_S, state_rwkv_shift, state_gla_S, cache_mem_k, cache_mem_v, norm_mix_g, norm_ffn_g, norm_mem_g, norm_final_g, w_in, w_out, w_mem_kv, rw_mu, rw_w0, rw_w1, rw_w2, rw_a0, rw_a1, rw_a2, rw_g1, rw_g2, rw_k_k, rw_k_a, rw_r_k, rw_ln_g, rw_ln_b, gla_a1, gla_a2, gla_ab, gla_norm_g, router_wg, router_bg, router_we, router_be, exp_w_up, exp_w_down):
    bp, tp, _ = x_prompt.shape
    bs, ts, _ = x_sample.shape
    assert ts == 1 and tp % GLA_CHUNK == 0 and tp % SCAN_TIME_BLOCK == 0
    np_ = bp * tp
    ns = bs * ts
    m = np_ + ns
    assert np_ % LIGHT_BLOCK == 0 and ns % SAMPLE_BLOCK == 0 and (2 * m) % MOE_BLOCK == 0
    depth = w_in.shape[0]
    nh = GLA_HEADS
    tw = TOK_WIDTH
    bf = lambda t_: t_.astype(BF16)

    mem2d = mem_prompt.reshape(bp * N_MEM, D_MODEL)
    mem_kv = [_norm_matmul(mem2d, norm_mem_g[i], w_mem_kv[i], 512) for i in range(depth)]
    pk = jnp.stack([kv[:, :MEM_WIDTH].reshape(bp, N_MEM, MEM_WIDTH) for kv in mem_kv])
    pv = jnp.stack([kv[:, MEM_WIDTH:].reshape(bp, N_MEM, MEM_WIDTH) for kv in mem_kv])
    prompt_mem_k = pk.reshape(depth, bp, N_MEM, MEM_HEADS, MEM_HEAD_DIM)
    prompt_mem_v = pv.reshape(depth, bp, N_MEM, MEM_HEADS, MEM_HEAD_DIM)
    sk = cache_mem_k.reshape(depth, bs, N_MEM, MEM_WIDTH).transpose(0, 1, 3, 2)
    sv = cache_mem_v.reshape(depth, bs, N_MEM, MEM_WIDTH).transpose(0, 1, 3, 2)

    x_p = x_prompt.reshape(np_, D_MODEL)
    x_s = x_sample.reshape(ns, D_MODEL)
    h_p = None
    h_s = _norm(x_s, norm_mix_g[0], SAMPLE_BLOCK)

    w_router = jnp.zeros((depth, D_MODEL, ROUTER_LANES), F32)
    w_router = w_router.at[:, :, :N_GROUPS].set(router_wg).at[:, :, N_GROUPS:N_GROUPS + N_EXPERTS].set(router_we)
    b_router = jnp.zeros((depth, 1, ROUTER_LANES), F32)
    b_router = b_router.at[:, 0, :N_GROUPS].set(router_bg).at[:, 0, N_GROUPS:N_GROUPS + N_EXPERTS].set(router_be)

    p_rw_S, p_rw_shift, p_gla_S, s_rw_S, s_rw_shift, s_gla_S = [], [], [], [], [], []
    pending_p = pending_s = None
    for i in range(depth):
        j = i // 2
        if i % 2 == 0:
            wi = w_in[i]
            ws = [rw_mu[j], bf(wi[:, :tw]), bf(wi[:, tw:2 * tw]), bf(wi[:, 2 * tw:3 * tw]), bf(wi[:, 3 * tw:]),
                  bf(rw_w1[j]), bf(rw_w2[j]), rw_w0[j].reshape(1, tw), bf(rw_a1[j]), bf(rw_a2[j]),
                  rw_a0[j].reshape(1, tw), bf(rw_g1[j]), bf(rw_g2[j])]
            if h_p is None:
                *rkvda_p, gate_p, qm_p = _rwkv_proj(x_p, None, ws, PROJ_BLOCK, tp, norm_g=norm_mix_g[i])
                shift_p = _norm(x_prompt[:, -1, :], norm_mix_g[i], bp)
            else:
                *rkvda_p, gate_p, qm_p = _rwkv_proj(h_p, None, ws, PROJ_BLOCK, tp)
                shift_p = h_p.reshape(bp, tp, D_MODEL)[:, -1]
            *rkvda_s, gate_s, qm_s = _rwkv_proj(h_s, state_rwkv_shift[j], ws, SAMPLE_BLOCK, ts)
            pvec = [rw_k_k[j], rw_k_a[j], rw_r_k[j], rw_ln_g[j], rw_ln_b[j]]
            tok_p, sp = _rwkv_prompt_mixer(*rkvda_p, *[_seq_scan_param(p, bp) for p in pvec], bp, tp, SCAN_TIME_BLOCK)
            ys, ss = _rwkv_scan(*[_to_scan_layout(t_, bs, ts) for t_ in rkvda_s], [_scan_param(p, bs) for p in pvec],
                                _state_to_scan(state_rwkv_S[j]), LANES, 1)
            tok_s = _from_scan_layout(ys, bs, ts)
            p_rw_S.append(sp[:, :, :RWKV_HEADS * bp].reshape(RWKV_N, RWKV_N, RWKV_HEADS, bp).transpose(3, 2, 0, 1))
            s_rw_S.append(_state_from_scan(ss, bs))
            p_rw_shift.append(shift_p)
            s_rw_shift.append(h_s)
            layout_p, layout_s = "rwkv_pairs", "rows"
        else:
            ws = _gla_weights(w_in[i], gla_a1[j], gla_a2[j], gla_ab[j])
            if pending_p is None:
                q_p, k_p, v_p, gate_p, la_p, qm_p = _gla_proj(h_p, ws, PROJ_BLOCK)
                q_s, k_s, v_s, gate_s, la_s, qm_s = _gla_proj(h_s, ws, SAMPLE_BLOCK)
            else:
                q_p, k_p, v_p, gate_p, la_p, qm_p, x_p = _gla_proj(None, ws, PROJ_BLOCK, pending_p)
                q_s, k_s, v_s, gate_s, la_s, qm_s, x_s = _gla_proj(None, ws, SAMPLE_BLOCK, pending_s)
            tok_p, sp_t = _gla_chunk_scan(q_p, k_p, v_p, la_p, gla_norm_g[j], bp, tp)
            rows_of = lambda t_: t_.transpose(1, 0, 2).reshape(bs * nh, t_.shape[2])
            os_, ss = _gla_step(rows_of(q_s), rows_of(k_s), rows_of(v_s), rows_of(la_s), gla_norm_g[j], state_gla_S, j)
            tok_s = os_.reshape(bs, nh, GLA_DV).transpose(1, 0, 2)
            p_gla_S.append(sp_t.transpose(0, 1, 3, 2))
            s_gla_S.append(ss)
            layout_p = layout_s = "gla_heads"

        att_p = _mem_attn(qm_p.reshape(bp, tp, MEM_WIDTH), pk, pv, i, 512).reshape(np_, MEM_WIDTH)
        att_s = _mem_attn(qm_s.reshape(bs, ts, MEM_WIDTH), sk, sv, i, 1, memory_transposed=True).reshape(ns, MEM_WIDTH)
        wr_hi = bf(w_router[i])
        wr_lo = bf(w_router[i] - wr_hi.astype(F32))
        ws = [bf(w_out[i]), norm_ffn_g[i].reshape(1, D_MODEL), wr_hi, wr_lo, b_router[i]]
        x1_p, h2, logits_p = _out_proj(tok_p, gate_p, att_p, x_p, ws, layout_p, PROJ_BLOCK, m, 0)
        x1_s, h2, logits_s = _out_proj(tok_s, gate_s, att_s, x_s, ws, layout_s, SAMPLE_BLOCK, m, np_, h2_buffer=h2)
        tok_sorted, gate_sorted, inv, item_tile, item_expert, n_items, lo, hi = _route(
            jnp.concatenate([logits_p, logits_s], axis=0), np_, MOE_BLOCK)
        rows = _moe_ffn(h2[tok_sorted], gate_sorted, item_tile, item_expert, n_items, lo, hi,
                        exp_w_up, exp_w_down, i)
        gathered = rows[inv]
        g_next = norm_mix_g[i + 1] if i + 1 < depth else norm_final_g
        if i + 1 < depth and (i + 1) % 2 == 1:
            pending_p = (x1_p, gathered, 0, g_next)
            pending_s = (x1_s, gathered, 2 * np_, g_next)
        else:
            pending_p = pending_s = None
            x_p, h_p = _combine(x1_p, gathered, 0, g_next, LIGHT_BLOCK, i + 1 == depth)
            x_s, h_s = _combine(x1_s, gathered, 2 * np_, g_next, SAMPLE_BLOCK, i + 1 == depth)

    y_prompt = h_p.reshape(bp, tp, D_MODEL)
    y_sample = h_s.reshape(bs, ts, D_MODEL)
    return (y_prompt, y_sample, jnp.stack(p_rw_S), jnp.stack(p_rw_shift), jnp.stack(p_gla_S),
            prompt_mem_k, prompt_mem_v, jnp.stack(s_rw_S), jnp.stack(s_rw_shift), jnp.stack(s_gla_S))
```

```python
import functools

import numpy as np
import jax
import jax.numpy as jnp
from jax import lax
from jax.experimental import pallas as pl
from jax.experimental.pallas import tpu as pltpu

F32 = jnp.float32
BF16 = jnp.bfloat16

D_MODEL = 1024
TOK_WIDTH = 768
MEM_WIDTH = 256
MEM_HEADS = 4
MEM_HEAD_DIM = 64
N_MEM = 256
RWKV_HEADS = 12
RWKV_N = 64
RWKV_PAIRS = RWKV_HEADS // 2
RWKV_GN_EPS = 64e-5
GLA_HEADS = 4
GLA_KW = 384
GLA_DK = 96
GLA_DV = 192
GLA_TAU = 16.0
GLA_CHUNK = 64
GLA_TILE = 8
N_GROUPS = 4
EXPERTS_PER_GROUP = 8
N_EXPERTS = 32
EXPERT_FF = 512
NORM_EPS = 1e-6
ROUTER_LANES = 128
LANES = 128
SUBLANES = 8

PROJ_BLOCK = 512
LIGHT_BLOCK = 512
SAMPLE_BLOCK = 128
MOE_BLOCK = 256
SCAN_TIME_BLOCK = 32
VMEM_LIMIT = 56 * 1024 * 1024


def _cparams(*sem):
    return pltpu.CompilerParams(dimension_semantics=sem, vmem_limit_bytes=VMEM_LIMIT)


def _dot(a, b):
    return jnp.dot(a, b, preferred_element_type=F32)


def _dot_nt(a, b):
    return lax.dot_general(a, b, (((1,), (1,)), ((), ())), preferred_element_type=F32)


def _dot_tn(a, b):
    return lax.dot_general(a, b, (((0,), (0,)), ((), ())), preferred_element_type=F32)


def _rms(x, g):
    return x * lax.rsqrt(jnp.mean(x * x, axis=-1, keepdims=True) + NORM_EPS) * g


def _sigmoid(x):
    return 1.0 / (1.0 + jnp.exp(-x))


def _row_spec(tm, n, offset=0):
    return pl.BlockSpec((tm, n), lambda i: (i + offset, 0))


def _head_row_spec(tm, n):
    return pl.BlockSpec((GLA_HEADS, tm, n), lambda i: (0, i, 0))


def _full_spec(shape):
    nd = len(shape)
    return pl.BlockSpec(shape, lambda *_: (0,) * nd)


def _norm_kernel(x_ref, g_ref, o_ref):
    o_ref[...] = _rms(x_ref[...], g_ref[...])


def _norm(x, g, tm):
    m, d = x.shape
    return pl.pallas_call(
        _norm_kernel, grid=(m // tm,),
        in_specs=[_row_spec(tm, d), _full_spec((1, d))],
        out_specs=_row_spec(tm, d),
        out_shape=jax.ShapeDtypeStruct((m, d), F32),
        compiler_params=_cparams("parallel"), name="rms_norm")(x, g.reshape(1, d))


def _norm_matmul_kernel(x_ref, g_ref, w_ref, o_ref):
    o_ref[...] = _dot(_rms(x_ref[...], g_ref[...]).astype(BF16), w_ref[...])


def _norm_matmul(x, g, w, tm):
    m, d = x.shape
    n = w.shape[1]
    return pl.pallas_call(
        _norm_matmul_kernel, grid=(m // tm,),
        in_specs=[_row_spec(tm, d), _full_spec((1, d)), _full_spec((d, n))],
        out_specs=_row_spec(tm, n),
        out_shape=jax.ShapeDtypeStruct((m, n), F32),
        compiler_params=_cparams("parallel"), name="norm_matmul")(x, g.reshape(1, d), w.astype(BF16))


def _rwkv_proj_kernel(h_ref, hp_ref, *refs, blocks_per_seq, norm_input):
    if norm_input:
        ng_ref, refs = refs[0], refs[1:]
        normed = lambda t_: _rms(t_, ng_ref[...])
    else:
        normed = lambda t_: t_
    (mu_ref, wr_ref, wk_ref, wv_ref, wq_ref, w1_ref, w2_ref, w0_ref, a1_ref, a2_ref, a0_ref, g1_ref, g2_ref,
     r_out, k_out, v_out, d_out, a_out, g_out, q_out) = refs

    def put(out, val):
        if blocks_per_seq:
            for p in range(RWKV_PAIRS):
                out[p] = val[:, p * LANES:(p + 1) * LANES]
        else:
            out[...] = val

    h = normed(h_ref[...])
    if blocks_per_seq:
        seq_start = (pl.program_id(0) % blocks_per_seq) == 0
        before = jnp.where(seq_start, 0.0, normed(hp_ref[SUBLANES - 1:SUBLANES, :]))
        row = lax.broadcasted_iota(jnp.int32, (h.shape[0], 1), 0)
        hp = jnp.where(row == 0, before, pltpu.roll(h, 1, axis=0))
    else:
        hp = hp_ref[...]
    xx = hp - h

    def mix(j):
        return (h + xx * mu_ref[j:j + 1, :]).astype(BF16)

    put(r_out, _dot(mix(0), wr_ref[...]))
    wl = w0_ref[...] + _dot(jnp.tanh(_dot(mix(1), w1_ref[...])).astype(BF16), w2_ref[...])
    put(k_out, _dot(mix(2), wk_ref[...]))
    put(v_out, _dot(mix(3), wv_ref[...]))
    al = a0_ref[...] + _dot(_dot(mix(4), a1_ref[...]).astype(BF16), a2_ref[...])
    g_out[...] = _dot(_sigmoid(_dot(mix(5), g1_ref[...])).astype(BF16), g2_ref[...])
    q_out[...] = _dot(h.astype(BF16), wq_ref[...])
    z = -wl
    softplus = jnp.maximum(z, 0.0) + jnp.log(1.0 + jnp.exp(-jnp.abs(z)))
    put(d_out, jnp.exp(-jnp.exp(-softplus - 0.5)))
    put(a_out, _sigmoid(al))


def _rwkv_proj(h, h_prev, ws, tm, seq_len, norm_g=None):
    m = h.shape[0]
    if norm_g is not None:
        assert h_prev is None
        ws = [norm_g.reshape(1, D_MODEL)] + list(ws)
    tw = TOK_WIDTH
    if h_prev is None:
        assert seq_len % tm == 0 and tm % SUBLANES == 0
        per8 = tm // SUBLANES
        hp_spec = pl.BlockSpec((SUBLANES, D_MODEL), lambda i: (jnp.maximum(i * per8 - 1, 0), 0))
        h_prev, blocks_per_seq = h, seq_len // tm
    else:
        hp_spec, blocks_per_seq = _row_spec(tm, D_MODEL), 0
    if blocks_per_seq:
        scan_spec = pl.BlockSpec((RWKV_PAIRS, tm, LANES), lambda i: (0, i, 0))
        scan_shape = jax.ShapeDtypeStruct((RWKV_PAIRS, m, LANES), F32)
    else:
        scan_spec, scan_shape = _row_spec(tm, tw), jax.ShapeDtypeStruct((m, tw), F32)
    return pl.pallas_call(
        functools.partial(_rwkv_proj_kernel, blocks_per_seq=blocks_per_seq, norm_input=norm_g is not None),
        grid=(m // tm,),
        in_specs=[_row_spec(tm, D_MODEL), hp_spec] + [_full_spec(w.shape) for w in ws],
        out_specs=[scan_spec] * 5 + [_row_spec(tm, tw), _row_spec(tm, MEM_WIDTH)],
        out_shape=[scan_shape] * 5 + [jax.ShapeDtypeStruct((m, tw), F32), jax.ShapeDtypeStruct((m, MEM_WIDTH), F32)],
        compiler_params=_cparams("parallel"), name="rwkv_proj")(h, h_prev, *ws)


def _pad_lanes(x):
    short = LANES - x.shape[-1]
    if short == 0:
        return x
    return jnp.concatenate([x, jnp.zeros(x.shape[:-1] + (short,), x.dtype)], axis=-1)


def _rwkv_scan_kernel(r_ref, k_ref, v_ref, d_ref, a_ref, kkp_ref, kap_ref, rkp_ref, lng_ref, lnb_ref, s0_ref,
                      y_ref, sfin_ref, s_scr, v_scr, yrow_scr, *, tc):
    n = RWKV_N
    nl = r_ref.shape[-1]

    @pl.when(pl.program_id(1) == 0)
    def _():
        s_scr[...] = _pad_lanes(s0_ref[...])

    kkp = _pad_lanes(kkp_ref[...])
    kap = _pad_lanes(kap_ref[...])
    rkp = _pad_lanes(rkp_ref[...])
    lng = _pad_lanes(lng_ref[...])
    lnb = _pad_lanes(lnb_ref[...])

    def step(t, carry):
        r_t = _pad_lanes(r_ref[t])
        k_t = _pad_lanes(k_ref[t])
        v_t = _pad_lanes(v_ref[t])
        d_t = _pad_lanes(d_ref[t])
        a_t = _pad_lanes(a_ref[t])
        v_scr[...] = v_t
        kkr = k_t * kkp
        nrm = jnp.maximum(jnp.sqrt(jnp.sum(kkr * kkr, axis=0, keepdims=True)), 1e-12)
        kk = kkr * (1.0 / nrm)
        k2 = k_t * (1.0 + (a_t - 1.0) * kap)
        nkk = -kk
        b_t = kk * a_t

        def ibody(i, c):
            s_i = s_scr[i]
            sa = jnp.sum(s_i * nkk, axis=0, keepdims=True)
            v_i = v_scr[pl.ds(i, 1), :]
            s_n = s_i * d_t + sa * b_t + v_i * k2
            s_scr[i] = s_n
            yrow_scr[pl.ds(i, 1), :] = jnp.sum(s_n * r_t, axis=0, keepdims=True)
            return c

        lax.fori_loop(0, n, ibody, 0, unroll=8)
        y = yrow_scr[...]
        yc = y - jnp.mean(y, axis=0, keepdims=True)
        var = jnp.mean(yc * yc, axis=0, keepdims=True)
        gn = yc * lax.rsqrt(var + RWKV_GN_EPS) * lng + lnb
        bonus = jnp.sum(r_t * k2 * rkp, axis=0, keepdims=True) * v_t
        y_ref[t] = (gn + bonus)[:, :nl]
        return carry

    lax.fori_loop(0, tc, step, 0)

    @pl.when(pl.program_id(1) == pl.num_programs(1) - 1)
    def _():
        sfin_ref[...] = s_scr[:, :, :nl]


def _rwkv_scan(r, k, v, d, a, params, s0, lane_block, tc):
    t, n, l = r.shape
    seq = pl.BlockSpec((tc, n, lane_block), lambda li, ti: (ti, 0, li))
    par = pl.BlockSpec((n, lane_block), lambda li, ti: (0, li))
    st = pl.BlockSpec((n, n, lane_block), lambda li, ti: (0, 0, li))
    return pl.pallas_call(
        functools.partial(_rwkv_scan_kernel, tc=tc), grid=(l // lane_block, t // tc),
        in_specs=[seq] * 5 + [par] * 5 + [st],
        out_specs=[seq, st],
        out_shape=[jax.ShapeDtypeStruct((t, n, l), F32), jax.ShapeDtypeStruct((n, n, l), F32)],
        scratch_shapes=[pltpu.VMEM((n, n, LANES), F32), pltpu.VMEM((n, LANES), F32), pltpu.VMEM((n, LANES), F32)],
        compiler_params=_cparams("parallel", "arbitrary"), name="rwkv_scan")(r, k, v, d, a, *params, s0)


PREP_TILES = 6
DECAY_WINDOW = 16


def _rwkv_prep_kernel(r_ref, k_ref, v_ref, d_ref, a_ref, kkp_ref, kap_ref, o_ref, *, tc, nb):
    n = RWKV_N
    rows_per_pair = nb * tc
    ins = [ref.reshape(RWKV_PAIRS * rows_per_pair, LANES) for ref in (r_ref, k_ref, v_ref, d_ref, a_ref)]
    zero_rows = jnp.zeros((LANES - RWKV_HEADS * nb, LANES), F32)
    first_half = lax.broadcasted_iota(jnp.int32, (nb, LANES), 1) < n
    kkp = kkp_ref[...]
    kap = kap_ref[...]

    def load_transposed_pair(x2, t):
        pieces = []
        for p in range(RWKV_PAIRS):
            now = x2[pl.ds(p * rows_per_pair + t, nb, stride=tc), :]
            nxt = x2[pl.ds(p * rows_per_pair + t + 1, nb, stride=tc), :]
            pieces += [jnp.where(first_half, now, pltpu.roll(nxt, n, axis=1)),
                       jnp.where(first_half, pltpu.roll(now, n, axis=1), nxt)]
        both = jnp.concatenate(pieces + [zero_rows], axis=0).T
        return both[:n, :], both[n:, :]

    def emit(t, c_prev, r_t, k_t, v_t, d_t, a_t):
        kkr = k_t * kkp
        nrm = jnp.maximum(jnp.sqrt(jnp.sum(kkr * kkr, axis=0, keepdims=True)), 1e-12)
        kk = kkr * (1.0 / nrm)
        k2 = k_t * (1.0 + (a_t - 1.0) * kap)
        c_prev = jnp.where(t % DECAY_WINDOW == 0, 1.0, c_prev)
        c_t = c_prev * d_t
        inv = 1.0 / jnp.maximum(c_t, 1e-30)
        o_ref[t, 0] = -kk * c_prev
        o_ref[t, 1] = c_t
        o_ref[t, 2] = kk * a_t * inv
        o_ref[t, 3] = k2 * inv
        o_ref[t, 4] = r_t * c_t
        o_ref[t, 5] = v_t
        return c_t

    def two_steps(u, c):
        t = 2 * u
        tiles = [load_transposed_pair(x2, t) for x2 in ins]
        c = emit(t, c, *[tile[0] for tile in tiles])
        return emit(t + 1, c, *[tile[1] for tile in tiles])

    lax.fori_loop(0, tc // 2, two_steps, jnp.ones((n, LANES), F32), unroll=2)


def _rwkv_state_scan_kernel(x_ref, rkp_ref, lng_ref, lnb_ref, y_ref, sfin_ref, s_scr, sa_scr, yrow_scr, *, tc):
    n = RWKV_N
    groups = n // SUBLANES

    @pl.when(pl.program_id(0) == 0)
    def _():
        s_scr[...] = jnp.zeros(s_scr.shape, F32)

    rkp = rkp_ref[...]
    lng = lng_ref[...]
    lnb = lnb_ref[...]
    sub = lax.broadcasted_iota(jnp.int32, (SUBLANES, LANES), 0)
    low4 = sub < 4
    low2 = (sub & 3) < 2
    low1 = (sub & 1) == 0

    def fold(x, y, dist, low):
        if dist == 4:
            return jnp.where(low, x, y) + pltpu.roll(jnp.where(low, y, x), 4, axis=0)
        return (jnp.where(low, x, pltpu.roll(y, dist, axis=0))
                + jnp.where(low, pltpu.roll(x, SUBLANES - dist, axis=0), y))

    def sublane_sums(ps):
        z = [fold(ps[0], ps[4], 4, low4), fold(ps[2], ps[6], 4, low4),
             fold(ps[1], ps[5], 4, low4), fold(ps[3], ps[7], 4, low4)]
        return fold(fold(z[0], z[1], 2, low2), fold(z[2], z[3], 2, low2), 1, low1)

    def tile_sum(x):
        acc = x[0:SUBLANES]
        for u in range(1, groups):
            acc = acc + x[u * SUBLANES:(u + 1) * SUBLANES]
        return acc

    def step(t, carry):
        nkk = x_ref[t, 0]
        b_t = x_ref[t, 2]
        k2 = x_ref[t, 3]
        r_t = x_ref[t, 4]
        for g in range(groups):
            sa_scr[g * SUBLANES:(g + 1) * SUBLANES, :] = sublane_sums(
                [tile_sum(s_scr[g * SUBLANES + u] * nkk) for u in range(SUBLANES)])
        for g in range(groups):
            ps = []
            for u in range(SUBLANES):
                i = g * SUBLANES + u
                s_n = s_scr[i] + sa_scr[pl.ds(i, 1), :] * b_t + x_ref[t, 5, pl.ds(i, 1), :] * k2
                s_scr[i] = s_n
                ps.append(tile_sum(s_n * r_t))
            yrow_scr[g * SUBLANES:(g + 1) * SUBLANES, :] = sublane_sums(ps)
        y = yrow_scr[...]
        yc = y - jnp.mean(y, axis=0, keepdims=True)
        var = jnp.mean(yc * yc, axis=0, keepdims=True)
        bonus = jnp.sum(r_t * k2 * rkp, axis=0, keepdims=True) * x_ref[t, 5]
        y_ref[t] = yc * lax.rsqrt(var + RWKV_GN_EPS) * lng + lnb + bonus

        @pl.when(t % DECAY_WINDOW == DECAY_WINDOW - 1)
        def _():
            c_t = x_ref[t, 1]
            for i in range(n):
                s_scr[i] = s_scr[i] * c_t

        return carry

    lax.fori_loop(0, tc, step, 0)

    @pl.when(pl.program_id(0) == pl.num_programs(0) - 1)
    def _():
        sfin_ref[...] = s_scr[...]


def _rwkv_unprep_kernel(y_ref, o_ref, *, tc, nb):
    n = RWKV_N
    rows_per_pair = nb * tc
    o2 = o_ref.reshape(RWKV_PAIRS * rows_per_pair, LANES)
    first_half = lax.broadcasted_iota(jnp.int32, (nb, LANES), 1) < n

    def two_steps(u, carry):
        t = 2 * u
        w = jnp.concatenate([y_ref[t], y_ref[t + 1]], axis=0).T
        for p in range(RWKV_PAIRS):
            even = w[(2 * p) * nb:(2 * p + 1) * nb, :]
            odd = w[(2 * p + 1) * nb:(2 * p + 2) * nb, :]
            o2[pl.ds(p * rows_per_pair + t, nb, stride=tc), :] = jnp.where(
                first_half, even, pltpu.roll(odd, n, axis=1))
            o2[pl.ds(p * rows_per_pair + t + 1, nb, stride=tc), :] = jnp.where(
                first_half, pltpu.roll(even, n, axis=1), odd)
        return carry

    lax.fori_loop(0, tc // 2, two_steps, 0, unroll=2)


def _rwkv_prompt_mixer(r, k, v, d, a, kkp, kap, rkp, lng, lnb, nb, t, tc):
    n = RWKV_N
    assert tc % SUBLANES == 0 and tc % 4 == 0 and t % tc == 0 and RWKV_HEADS * nb <= LANES
    assert tc % DECAY_WINDOW == 0
    grid = (t // tc,)
    tok = pl.BlockSpec((RWKV_PAIRS, nb, tc, LANES), lambda ti: (0, 0, ti, 0))
    par = pl.BlockSpec((n, LANES), lambda ti: (0, 0))
    tiles = pl.BlockSpec((tc, PREP_TILES, n, LANES), lambda ti: (ti, 0, 0, 0))
    ytile = pl.BlockSpec((tc, n, LANES), lambda ti: (ti, 0, 0))
    prepared = pl.pallas_call(
        functools.partial(_rwkv_prep_kernel, tc=tc, nb=nb), grid=grid,
        in_specs=[tok] * 5 + [par] * 2, out_specs=tiles,
        out_shape=jax.ShapeDtypeStruct((t, PREP_TILES, n, LANES), F32),
        compiler_params=_cparams("parallel"), name="rwkv_prep")(
            *[x.reshape(RWKV_PAIRS, nb, t, LANES) for x in (r, k, v, d, a)], kkp, kap)
    y, s = pl.pallas_call(
        functools.partial(_rwkv_state_scan_kernel, tc=tc), grid=grid,
        in_specs=[tiles, par, par, par],
        out_specs=[ytile, pl.BlockSpec((n, n, LANES), lambda ti: (0, 0, 0))],
        out_shape=[jax.ShapeDtypeStruct((t, n, LANES), F32), jax.ShapeDtypeStruct((n, n, LANES), F32)],
        scratch_shapes=[pltpu.VMEM((n, n, LANES), F32), pltpu.VMEM((n, LANES), F32), pltpu.VMEM((n, LANES), F32)],
        compiler_params=_cparams("arbitrary"), name="rwkv_state_scan")(prepared, rkp, lng, lnb)
    tok_out = pl.pallas_call(
        functools.partial(_rwkv_unprep_kernel, tc=tc, nb=nb), grid=grid,
        in_specs=[ytile], out_specs=tok,
        out_shape=jax.ShapeDtypeStruct((RWKV_PAIRS, nb, t, LANES), F32),
        compiler_params=_cparams("parallel"), name="rwkv_unprep")(y)
    return tok_out.reshape(RWKV_PAIRS, nb * t, LANES), s


def _gla_proj_kernel(*refs, fused_combine):
    if fused_combine:
        x_ref, r0_ref, r1_ref, g_ref = refs[:4]
        refs, x_out = refs[4:-1], refs[-1]
        x2 = x_ref[...] + (r0_ref[...] + r1_ref[...])
        x_out[...] = x2
        h = _rms(x2, g_ref[...])
    else:
        h, refs = refs[0][...], refs[1:]
    (wq_ref, wk_ref, wv_ref, wr_ref, wm_ref, a1_ref, a2_ref, ab_ref,
     q_out, k_out, v_out, r_out, la_out, qm_out) = refs
    hb = h.astype(BF16)
    low = _dot(hb, a1_ref[...]).astype(BF16)
    for hd in range(GLA_HEADS):
        q_out[hd] = _dot(hb, wq_ref[hd])
        k_out[hd] = _dot(hb, wk_ref[hd])
        v_out[hd] = _dot(hb, wv_ref[hd])
        r_out[hd] = _dot(hb, wr_ref[hd])
        x = _dot(low, a2_ref[hd]) + ab_ref[hd]
        log_sigmoid = jnp.minimum(x, 0.0) - jnp.log(1.0 + jnp.exp(-jnp.abs(x)))
        la_out[hd] = log_sigmoid / GLA_TAU
    qm_out[...] = _dot(hb, wm_ref[...])


def _gla_weights(w_in, a1, a2, ab):
    kw, tw, nh = GLA_KW, TOK_WIDTH, GLA_HEADS

    def heads(w, d):
        return w.reshape(w.shape[0], nh, d).transpose(1, 0, 2)

    return [heads(w_in[:, :kw], GLA_DK).astype(BF16), heads(w_in[:, kw:2 * kw], GLA_DK).astype(BF16),
            heads(w_in[:, 2 * kw:2 * kw + tw], GLA_DV).astype(BF16),
            heads(w_in[:, 2 * kw + tw:3 * tw], GLA_DV).astype(BF16), w_in[:, 3 * tw:].astype(BF16),
            a1.astype(BF16), heads(a2, GLA_DK).astype(BF16), ab.reshape(nh, 1, GLA_DK)]


def _gla_proj(h, ws, tm, pending=None):
    nh = GLA_HEADS
    widths = [GLA_DK, GLA_DK, GLA_DV, GLA_DV, GLA_DK]
    if pending is None:
        m = h.shape[0]
        lead_specs, lead_args = [_row_spec(tm, D_MODEL)], [h]
    else:
        x1, gathered, first_row, norm_g = pending
        m = x1.shape[0]
        lead_specs = [_row_spec(tm, D_MODEL), _row_spec(tm, D_MODEL, first_row // tm),
                      _row_spec(tm, D_MODEL, (first_row + m) // tm), _full_spec((1, D_MODEL))]
        lead_args = [x1, gathered, gathered, norm_g.reshape(1, D_MODEL)]
    out_specs = [_head_row_spec(tm, w) for w in widths] + [_row_spec(tm, MEM_WIDTH)]
    out_shape = [jax.ShapeDtypeStruct((nh, m, w), F32) for w in widths] + [jax.ShapeDtypeStruct((m, MEM_WIDTH), F32)]
    if pending is not None:
        out_specs.append(_row_spec(tm, D_MODEL))
        out_shape.append(jax.ShapeDtypeStruct((m, D_MODEL), F32))
    return pl.pallas_call(
        functools.partial(_gla_proj_kernel, fused_combine=pending is not None), grid=(m // tm,),
        in_specs=lead_specs + [_full_spec(w.shape) for w in ws],
        out_specs=out_specs, out_shape=out_shape,
        compiler_params=_cparams("parallel"), name="gla_proj")(*lead_args, *ws)


def _gla_out_norm(o, g):
    return o * lax.rsqrt(jnp.mean(o * o, axis=-1, keepdims=True) + NORM_EPS) * g


def _gla_chunk_kernel(q_ref, k_ref, v_ref, la_ref, ng_ref, o_ref, sfin_ref, st_scr):
    c, dk = GLA_CHUNK, GLA_DK

    @pl.when(pl.program_id(1) == 0)
    def _():
        st_scr[...] = jnp.zeros(st_scr.shape, F32)

    row = lax.broadcasted_iota(jnp.int32, (c, c), 0)
    col = lax.broadcasted_iota(jnp.int32, (c, c), 1)
    tril = (row >= col).astype(F32)
    rr = lax.broadcasted_iota(jnp.int32, (c, dk), 0)
    ones_sum = jnp.ones((dk, LANES), BF16)

    heads = range(GLA_HEADS)
    tril_b = tril.astype(BF16)

    def cumsum_rows(la):
        hi = la.astype(BF16)
        r1 = la - hi.astype(F32)
        mid = r1.astype(BF16)
        lo = (r1 - mid.astype(F32)).astype(BF16)
        return _dot(tril_b, hi) + _dot(tril_b, mid) + _dot(tril_b, lo)

    def tile_roll(x, dlt):
        return pltpu.roll(x.reshape(c // GLA_TILE, GLA_TILE, dk), dlt, axis=1).reshape(c, dk)

    k = [k_ref[hd] for hd in heads]
    vb = [v_ref[hd].astype(BF16) for hd in heads]
    q = [q_ref[hd] * (dk ** -0.5) for hd in heads]
    b = [cumsum_rows(la_ref[hd]) for hd in heads]
    st = [st_scr[hd] for hd in heads]
    inter = [_dot_nt((q[hd] * jnp.exp(b[hd])).astype(BF16), st[hd].astype(BF16)) for hd in heads]

    att = [jnp.zeros((c, c), F32) for hd in heads]
    blk = c // 2
    while blk >= GLA_TILE:
        two = 2 * blk
        upper = (rr & (two - 1)) >= blk
        same_block = (row ^ col) < two
        parts = []
        for hd in heads:
            b_ref_rows = jnp.concatenate(
                [jnp.broadcast_to(b[hd][s0 + blk - 1:s0 + blk, :], (two, dk)) for s0 in range(0, c, two)], axis=0)
            q_l = jnp.where(upper, q[hd] * jnp.exp(jnp.minimum(b[hd] - b_ref_rows, 0.0)), 0.0).astype(BF16)
            k_l = jnp.where(upper, 0.0, k[hd] * jnp.exp(jnp.minimum(b_ref_rows - b[hd], 0.0))).astype(BF16)
            parts.append(_dot_nt(q_l, k_l))
        att = [att[hd] + jnp.where(same_block, parts[hd], 0.0) for hd in heads]
        blk //= 2

    sums = []
    for hd in heads:
        prods = [(q[hd] * k[hd]).astype(BF16)]
        for dlt in range(1, GLA_TILE):
            p = q[hd] * tile_roll(k[hd], dlt) * jnp.exp(jnp.minimum(b[hd] - tile_roll(b[hd], dlt), 0.0))
            prods.append(jnp.where((rr & (GLA_TILE - 1)) >= dlt, p, 0.0).astype(BF16))
        sums.append(_dot(jnp.concatenate(prods, axis=0), ones_sum))
    for dlt in range(GLA_TILE):
        on_diag = col == row - dlt
        att = [att[hd] + jnp.where(on_diag, sums[hd][dlt * c:(dlt + 1) * c, :c], 0.0) for hd in heads]

    o = [inter[hd] + _dot(att[hd].astype(BF16), vb[hd]) for hd in heads]
    for hd in heads:
        o_ref[hd] = _gla_out_norm(o[hd], ng_ref[...])
    for hd in heads:
        b_end = b[hd][c - 1:c, :]
        kd = (k[hd] * jnp.exp(b_end - b[hd])).astype(BF16)
        st_scr[hd] = jnp.exp(b_end) * st[hd] + _dot_tn(vb[hd], kd)

    @pl.when(pl.program_id(1) == pl.num_programs(1) - 1)
    def _():
        sfin_ref[0] = st_scr[...]


def _gla_chunk_scan(q, k, v, la, norm_g, batch, t):
    nh, dk, dv, c = GLA_HEADS, GLA_DK, GLA_DV, GLA_CHUNK
    nc = t // c
    kspec = pl.BlockSpec((nh, c, dk), lambda i, j: (0, i * nc + j, 0))
    vspec = pl.BlockSpec((nh, c, dv), lambda i, j: (0, i * nc + j, 0))
    sspec = pl.BlockSpec((1, nh, dv, dk), lambda i, j: (i, 0, 0, 0))
    return pl.pallas_call(
        _gla_chunk_kernel, grid=(batch, nc),
        in_specs=[kspec, kspec, vspec, kspec, pl.BlockSpec((1, dv), lambda i, j: (0, 0))],
        out_specs=[vspec, sspec],
        out_shape=[jax.ShapeDtypeStruct((nh, batch * t, dv), F32), jax.ShapeDtypeStruct((batch, nh, dv, dk), F32)],
        scratch_shapes=[pltpu.VMEM((nh, dv, dk), F32)],
        compiler_params=_cparams("parallel", "arbitrary"), name="gla_chunk")(q, k, v, la, norm_g.reshape(1, dv))


GLA_STEP_KEY_SPLIT = 2


def _gla_step_kernel(q_ref, k_ref, v_ref, la_ref, ng_ref, s0_ref, o_ref, s_ref):
    rows = s0_ref.shape[1]

    @pl.when(pl.program_id(1) == 0)
    def _():
        o_ref[...] = jnp.zeros(o_ref.shape, F32)

    q = q_ref[0] * (GLA_DK ** -0.5)
    decay = jnp.exp(la_ref[0])
    kb = k_ref[0].astype(BF16).astype(F32)
    vb = v_ref[0].astype(BF16).astype(F32)
    acc = o_ref[0]
    for r in range(rows):
        s_new = decay[r:r + 1, :] * s0_ref[0, r] + kb[r:r + 1, :] * vb
        s_ref[0, r] = s_new
        acc = acc + q[r:r + 1, :] * s_new
    o_ref[0] = acc

    @pl.when(pl.program_id(1) == pl.num_programs(1) - 1)
    def _():
        o_ref[0] = acc * lax.rsqrt(jnp.mean(acc * acc, axis=0, keepdims=True) + NORM_EPS) * ng_ref[...]


def _gla_step(q, k, v, la, norm_g, states, layer):
    nh, dk, b = q.shape
    dv = v.shape[1]
    rows = dk // GLA_STEP_KEY_SPLIT
    assert rows * GLA_STEP_KEY_SPLIT == dk and rows % SUBLANES == 0
    kspec = pl.BlockSpec((1, rows, b), lambda h, r: (h, r, 0))
    vspec = pl.BlockSpec((1, dv, b), lambda h, r: (h, 0, 0))
    return pl.pallas_call(
        _gla_step_kernel, grid=(nh, GLA_STEP_KEY_SPLIT),
        in_specs=[kspec, kspec, vspec, kspec, _full_spec((dv, b)),
                  pl.BlockSpec((None, 1, rows, dv, b), lambda h, r: (layer, h, r, 0, 0))],
        out_specs=[vspec, pl.BlockSpec((1, rows, dv, b), lambda h, r: (h, r, 0, 0))],
        out_shape=[jax.ShapeDtypeStruct((nh, dv, b), F32), jax.ShapeDtypeStruct((nh, dk, dv, b), F32)],
        compiler_params=_cparams("parallel", "arbitrary"), name="gla_step")(
            q, k, v, la, jnp.broadcast_to(norm_g.reshape(dv, 1), (dv, b)), states)


def _mem_attn_kernel(q_ref, k_ref, v_ref, o_ref, *, memory_transposed):
    q = q_ref[0]
    k = k_ref[0].astype(BF16)
    ones = jnp.ones((LANES, N_MEM) if memory_transposed else (N_MEM, LANES), BF16)
    v_ones = jnp.concatenate([v_ref[0].astype(BF16), ones], axis=0 if memory_transposed else 1)
    scores, values = (_dot, _dot_nt) if memory_transposed else (_dot_nt, _dot)
    head_of_lane = lax.broadcasted_iota(jnp.int32, (1, MEM_WIDTH), 1) // MEM_HEAD_DIM
    heads = range(MEM_HEADS)
    mine = [head_of_lane == h for h in heads]
    qh = [jnp.where(mine[h], q, 0.0).astype(BF16) for h in heads]
    s = [scores(qh[h], k) * (MEM_HEAD_DIM ** -0.5) for h in heads]
    e = [jnp.exp(s[h] - jnp.max(s[h], axis=-1, keepdims=True)).astype(BF16) for h in heads]
    ev = [values(e[h], v_ones) for h in heads]
    out = jnp.zeros(q.shape, F32)
    for h in heads:
        inv = 1.0 / ev[h][:, MEM_WIDTH:]
        out = out + jnp.where(mine[h], ev[h][:, :MEM_WIDTH] * jnp.concatenate([inv, inv], axis=1), 0.0)
    o_ref[0] = out


def _mem_attn(q, mem_k, mem_v, layer, tq, memory_transposed=False):
    b, t, w = q.shape
    qspec = pl.BlockSpec((1, tq, w), lambda i, j: (i, j, 0))
    mspec = pl.BlockSpec((None, 1, N_MEM, w), lambda i, j: (layer, i, 0, 0))
    return pl.pallas_call(
        functools.partial(_mem_attn_kernel, memory_transposed=memory_transposed), grid=(b, t // tq),
        in_specs=[qspec, mspec, mspec], out_specs=qspec,
        out_shape=jax.ShapeDtypeStruct((b, t, w), F32),
        compiler_params=_cparams("parallel", "parallel"), name="mem_attn")(q, mem_k, mem_v)


def _out_proj_kernel(*refs, layout, aliased):
    tok_ref, gate_ref, att_ref, x_ref, wo_ref, g_ref, wr_hi_ref, wr_lo_ref, br_ref = refs[:9]
    x1_out, h2_out, logit_out = refs[9 + aliased:]
    x1 = x_ref[...] + _dot(att_ref[...].astype(BF16), wo_ref[TOK_WIDTH:, :])
    if layout == "gla_heads":
        for hd in range(GLA_HEADS):
            gate = gate_ref[hd]
            mixed = (tok_ref[hd] * (gate * _sigmoid(gate))).astype(BF16)
            x1 = x1 + _dot(mixed, wo_ref[hd * GLA_DV:(hd + 1) * GLA_DV, :])
    else:
        if layout == "rwkv_pairs":
            tok = jnp.concatenate([tok_ref[p] for p in range(RWKV_PAIRS)], axis=1)
        else:
            tok = tok_ref[...]
        x1 = x1 + _dot((tok * gate_ref[...]).astype(BF16), wo_ref[:TOK_WIDTH, :])
    x1_out[...] = x1
    h2 = _rms(x1, g_ref[...])
    h2_out[...] = h2
    h_hi = h2.astype(BF16)
    h_lo = (h2 - h_hi.astype(F32)).astype(BF16)
    logit_out[...] = (_dot(h_hi, wr_hi_ref[...]) + _dot(h_lo, wr_hi_ref[...]) + _dot(h_hi, wr_lo_ref[...])
                      + br_ref[...])


def _out_proj(tok, gate, att, x, ws, layout, tm, h2_rows, h2_row_offset, h2_buffer=None):
    m = x.shape[0]
    aliased = h2_buffer is not None
    rows_spec = _row_spec(tm, TOK_WIDTH)
    tok_spec, gate_spec = {
        "rows": (rows_spec, rows_spec),
        "rwkv_pairs": (pl.BlockSpec((RWKV_PAIRS, tm, LANES), lambda i: (0, i, 0)), rows_spec),
        "gla_heads": (_head_row_spec(tm, GLA_DV), _head_row_spec(tm, GLA_DV))}[layout]
    in_specs = ([tok_spec, gate_spec, _row_spec(tm, MEM_WIDTH), _row_spec(tm, D_MODEL)]
                + [_full_spec(w.shape) for w in ws])
    args = [tok, gate, att, x, *ws]
    if aliased:
        in_specs.append(pl.BlockSpec(memory_space=pl.ANY))
        args.append(h2_buffer)
    return pl.pallas_call(
        functools.partial(_out_proj_kernel, layout=layout, aliased=int(aliased)), grid=(m // tm,),
        in_specs=in_specs,
        out_specs=[_row_spec(tm, D_MODEL), _row_spec(tm, D_MODEL, h2_row_offset // tm), _row_spec(tm, ROUTER_LANES)],
        out_shape=[jax.ShapeDtypeStruct((m, D_MODEL), F32), jax.ShapeDtypeStruct((h2_rows, D_MODEL), F32),
                   jax.ShapeDtypeStruct((m, ROUTER_LANES), F32)],
        input_output_aliases={len(args) - 1: 1} if aliased else {},
        compiler_params=_cparams("parallel"), name="out_proj")(*args)


def _moe_kernel(tile_ref, exp_ref, nitem_ref, lo_ref, hi_ref, x_ref, gate_ref, wu_ref, wd_ref, o_ref, wu_scr, wd_scr):
    w = pl.program_id(0)
    prev = jnp.maximum(w - 1, 0)
    e = exp_ref[w]
    valid = w < nitem_ref[0]

    @pl.when(jnp.logical_and(valid, jnp.logical_or(w == 0, e != exp_ref[prev])))
    def _():
        wu_scr[...] = wu_ref[0, 0].astype(BF16)
        wd_scr[...] = wd_ref[0, 0].astype(BF16)

    @pl.when(jnp.logical_or(w == 0, tile_ref[w] != tile_ref[prev]))
    def _():
        o_ref[...] = jnp.zeros(o_ref.shape, F32)

    @pl.when(valid)
    def _():
        gu = _dot(x_ref[...].astype(BF16), wu_scr[...])
        g = gu[:, :EXPERT_FF]
        act = (g * _sigmoid(g) * gu[:, EXPERT_FF:]).astype(BF16)
        gate_col = jnp.concatenate([gate_ref[...], jnp.zeros((LANES - 1, MOE_BLOCK), F32)], axis=0).T[:, :1]
        out = _dot(act, wd_scr[...]) * gate_col
        rows = tile_ref[w] * MOE_BLOCK + lax.broadcasted_iota(jnp.int32, (MOE_BLOCK, 1), 0)
        mine = jnp.logical_and(rows >= lo_ref[e], rows < hi_ref[e])
        o_ref[...] = o_ref[...] + jnp.where(mine, out, 0.0)


def _moe_ffn(xs, row_gate, item_tile, item_expert, n_items, lo, hi, w_up, w_down, layer):
    tm = MOE_BLOCK
    n_work = item_tile.shape[0]
    a = xs.shape[0]
    row_map = lambda w, tile, ex, ni, lo_, hi_: (tile[w], 0)
    exp_map = lambda w, tile, ex, ni, lo_, hi_: (layer, ex[w], 0, 0)
    grid_spec = pltpu.PrefetchScalarGridSpec(
        num_scalar_prefetch=5, grid=(n_work,),
        in_specs=[pl.BlockSpec((tm, D_MODEL), row_map),
                  pl.BlockSpec((None, 1, tm), lambda w, tile, ex, ni, lo_, hi_: (tile[w], 0, 0)),
                  pl.BlockSpec((1, 1, D_MODEL, 2 * EXPERT_FF), exp_map),
                  pl.BlockSpec((1, 1, EXPERT_FF, D_MODEL), exp_map)],
        out_specs=pl.BlockSpec((tm, D_MODEL), row_map),
        scratch_shapes=[pltpu.VMEM((D_MODEL, 2 * EXPERT_FF), BF16), pltpu.VMEM((EXPERT_FF, D_MODEL), BF16)])
    return pl.pallas_call(
        _moe_kernel, grid_spec=grid_spec,
        out_shape=jax.ShapeDtypeStruct((a, D_MODEL), F32),
        compiler_params=_cparams("arbitrary"), name="moe_ffn")(
            item_tile, item_expert, n_items, lo, hi, xs, row_gate.reshape(a // tm, 1, tm), w_up, w_down)


def _route(logits, n_prompt, tm):
    m = logits.shape[0]
    n_sample = m - n_prompt
    a = 2 * m
    gl = logits[:, :N_GROUPS]
    el = logits[:, N_GROUPS:N_GROUPS + N_EXPERTS]
    group = jnp.argmax(gl, -1).astype(jnp.int32)
    p_group = jnp.max(jax.nn.softmax(gl, -1), -1, keepdims=True)
    experts_row = jnp.arange(N_EXPERTS, dtype=jnp.int32)[None, :]
    masked = jnp.where(experts_row // EXPERTS_PER_GROUP == group[:, None], el, -jnp.inf)
    e1 = jnp.argmax(masked, -1).astype(jnp.int32)
    masked2 = jnp.where(experts_row == e1[:, None], -jnp.inf, masked)
    e2 = jnp.argmax(masked2, -1).astype(jnp.int32)
    top_val = jnp.stack([jnp.max(masked, -1), jnp.max(masked2, -1)], axis=-1)
    gate = p_group * jax.nn.softmax(top_val, -1)
    expert = jnp.stack([e1, e2], axis=-1)

    def by_id(t):
        return jnp.concatenate([t[:n_prompt, 0], t[:n_prompt, 1], t[n_prompt:, 0], t[n_prompt:, 1]])

    tok_of_id = jnp.asarray(np.concatenate([np.arange(n_prompt), np.arange(n_prompt),
                                            n_prompt + np.arange(n_sample), n_prompt + np.arange(n_sample)]), jnp.int32)
    flat_e = by_id(expert)
    ids = jnp.arange(a, dtype=jnp.int32)
    _, order, gate_sorted, tok_sorted = lax.sort((flat_e, ids, by_id(gate), tok_of_id), num_keys=1, is_stable=True)
    _, inv = lax.sort((order, ids), num_keys=1)
    experts = jnp.arange(N_EXPERTS, dtype=jnp.int32)
    counts = jnp.sum((flat_e[:, None] == experts[None, :]).astype(jnp.int32), axis=0)
    hi = jnp.cumsum(counts).astype(jnp.int32)
    lo = hi - counts
    n_tiles = a // tm
    first_tile = lo // tm
    tiles_of = jnp.where(counts > 0, (hi - 1) // tm - first_tile + 1, 0)
    item_end = jnp.cumsum(tiles_of).astype(jnp.int32)
    n_items = item_end[-1:]
    n_work = n_tiles + N_EXPERTS - 1
    w = jnp.minimum(jnp.arange(n_work, dtype=jnp.int32), n_items[0] - 1)
    item_expert = jnp.sum((item_end[None, :] <= w[:, None]).astype(jnp.int32), axis=1)
    onehot = (item_expert[:, None] == experts[None, :]).astype(jnp.int32)
    item_tile = jnp.sum(onehot * (first_tile - (item_end - tiles_of))[None, :], axis=1) + w
    return tok_sorted, gate_sorted, inv, item_tile.astype(jnp.int32), item_expert, n_items, lo, hi


def _combine_kernel(x_ref, r0_ref, r1_ref, g_ref, *outs):
    x2 = x_ref[...] + (r0_ref[...] + r1_ref[...])
    outs[-1][...] = _rms(x2, g_ref[...])
    if len(outs) == 2:
        outs[0][...] = x2


def _combine(x1, gathered, first_row, g, tm, last_layer):
    m = x1.shape[0]
    spec = _row_spec(tm, D_MODEL)
    n_out = 1 if last_layer else 2
    outs = pl.pallas_call(
        _combine_kernel, grid=(m // tm,),
        in_specs=[spec, _row_spec(tm, D_MODEL, first_row // tm), _row_spec(tm, D_MODEL, (first_row + m) // tm),
                  _full_spec((1, D_MODEL))],
        out_specs=[spec] * n_out,
        out_shape=[jax.ShapeDtypeStruct((m, D_MODEL), F32)] * n_out,
        compiler_params=_cparams("parallel"), name="moe_combine")(x1, gathered, gathered, g.reshape(1, D_MODEL))
    return (None, outs[0]) if last_layer else tuple(outs)


def _to_scan_layout(t2d, b, t):
    return t2d.reshape(b, t, RWKV_HEADS, RWKV_N).transpose(1, 3, 0, 2).reshape(t, RWKV_N, b * RWKV_HEADS)


def _from_scan_layout(y, b, t):
    return y.reshape(t, RWKV_N, b, RWKV_HEADS).transpose(2, 0, 3, 1).reshape(b * t, TOK_WIDTH)


def _scan_param(p, b):
    return jnp.tile(p.reshape(RWKV_HEADS, RWKV_N).T, (1, b))


def _seq_scan_param(p, b):
    lanes = jnp.repeat(p.reshape(RWKV_HEADS, RWKV_N).T, b, axis=1)
    return jnp.pad(lanes, ((0, 0), (0, LANES - RWKV_HEADS * b)))


def _state_to_scan(s):
    b = s.shape[0]
    return s.transpose(2, 3, 0, 1).reshape(RWKV_N, RWKV_N, b * RWKV_HEADS)


def _state_from_scan(s, b):
    return s.reshape(RWKV_N, RWKV_N, b, RWKV_HEADS).transpose(2, 3, 0, 1)


def kernel(x_prompt, x_sample, mem_prompt, state_rwkv_S, state_rwkv_shift, state_gla_S, cache_mem_k, cache_mem_v, norm_mix_g, norm_ffn_g, norm_mem_g, norm_final_g, w_in, w_out, w_mem_kv, rw_mu, rw_w0, rw_w1, rw_w2, rw_a0, rw_a1, rw_a2, rw_g1, rw_g2, rw_k_k, rw_k_a, rw_r_k, rw_ln_g, rw_ln_b, gla_a1, gla_a2, gla_ab, gla_norm_g, router_wg, router_bg, router_we, router_be, exp_w_up, exp_w_down):
    bp, tp, _ = x_prompt.shape
    bs, ts, _ = x_sample.shape
    assert ts == 1 and tp % GLA_CHUNK == 0 and tp % SCAN_TIME_BLOCK == 0
    np_ = bp * tp
    ns = bs * ts
    m = np_ + ns
    assert np_ % LIGHT_BLOCK == 0 and ns % SAMPLE_BLOCK == 0 and (2 * m) % MOE_BLOCK == 0
    depth = w_in.shape[0]
    nh = GLA_HEADS
    tw = TOK_WIDTH
    bf = lambda t_: t_.astype(BF16)

    mem2d = mem_prompt.reshape(bp * N_MEM, D_MODEL)
    mem_kv = [_norm_matmul(mem2d, norm_mem_g[i], w_mem_kv[i], 512) for i in range(depth)]
    pk = jnp.stack([kv[:, :MEM_WIDTH].reshape(bp, N_MEM, MEM_WIDTH) for kv in mem_kv])
    pv = jnp.stack([kv[:, MEM_WIDTH:].reshape(bp, N_MEM, MEM_WIDTH) for kv in mem_kv])
    prompt_mem_k = pk.reshape(depth, bp, N_MEM, MEM_HEADS, MEM_HEAD_DIM)
    prompt_mem_v = pv.reshape(depth, bp, N_MEM, MEM_HEADS, MEM_HEAD_DIM)
    sk = cache_mem_k.reshape(depth, bs, N_MEM, MEM_WIDTH).transpose(0, 1, 3, 2)
    sv = cache_mem_v.reshape(depth, bs, N_MEM, MEM_WIDTH).transpose(0, 1, 3, 2)

    x_p = x_prompt.reshape(np_, D_MODEL)
    x_s = x_sample.reshape(ns, D_MODEL)
    h_p = None
    h_s = _norm(x_s, norm_mix_g[0], SAMPLE_BLOCK)

    w_router = jnp.zeros((depth, D_MODEL, ROUTER_LANES), F32)
    w_router = w_router.at[:, :, :N_GROUPS].set(router_wg).at[:, :, N_GROUPS:N_GROUPS + N_EXPERTS].set(router_we)
    b_router = jnp.zeros((depth, 1, ROUTER_LANES), F32)
    b_router = b_router.at[:, 0, :N_GROUPS].set(router_bg).at[:, 0, N_GROUPS:N_GROUPS + N_EXPERTS].set(router_be)

    p_rw_S, p_rw_shift, p_gla_S, s_rw_S, s_rw_shift, s_gla_S = [], [], [], [], [], []
    pending_p = pending_s = None
    for i in range(depth):
        j = i // 2
        if i % 2 == 0:
            wi = w_in[i]
            ws = [rw_mu[j], bf(wi[:, :tw]), bf(wi[:, tw:2 * tw]), bf(wi[:, 2 * tw:3 * tw]), bf(wi[:, 3 * tw:]),
                  bf(rw_w1[j]), bf(rw_w2[j]), rw_w0[j].reshape(1, tw), bf(rw_a1[j]), bf(rw_a2[j]),
                  rw_a0[j].reshape(1, tw), bf(rw_g1[j]), bf(rw_g2[j])]
            if h_p is None:
                *rkvda_p, gate_p, qm_p = _rwkv_proj(x_p, None, ws, PROJ_BLOCK, tp, norm_g=norm_mix_g[i])
                shift_p = _norm(x_prompt[:, -1, :], norm_mix_g[i], bp)
            else:
                *rkvda_p, gate_p, qm_p = _rwkv_proj(h_p, None, ws, PROJ_BLOCK, tp)
                shift_p = h_p.reshape(bp, tp, D_MODEL)[:, -1]
            *rkvda_s, gate_s, qm_s = _rwkv_proj(h_s, state_rwkv_shift[j], ws, SAMPLE_BLOCK, ts)
            pvec = [rw_k_k[j], rw_k_a[j], rw_r_k[j], rw_ln_g[j], rw_ln_b[j]]
            tok_p, sp = _rwkv_prompt_mixer(*rkvda_p, *[_seq_scan_param(p, bp) for p in pvec], bp, tp, SCAN_TIME_BLOCK)
            ys, ss = _rwkv_scan(*[_to_scan_layout(t_, bs, ts) for t_ in rkvda_s], [_scan_param(p, bs) for p in pvec],
                                _state_to_scan(state_rwkv_S[j]), LANES, 1)
            tok_s = _from_scan_layout(ys, bs, ts)
            p_rw_S.append(sp[:, :, :RWKV_HEADS * bp].reshape(RWKV_N, RWKV_N, RWKV_HEADS, bp).transpose(3, 2, 0, 1))
            s_rw_S.append(_state_from_scan(ss, bs))
            p_rw_shift.append(shift_p)
            s_rw_shift.append(h_s)
            layout_p, layout_s = "rwkv_pairs", "rows"
        else:
            ws = _gla_weights(w_in[i], gla_a1[j], gla_a2[j], gla_ab[j])
            if pending_p is None:
                q_p, k_p, v_p, gate_p, la_p, qm_p = _gla_proj(h_p, ws, PROJ_BLOCK)
                q_s, k_s, v_s, gate_s, la_s, qm_s = _gla_proj(h_s, ws, SAMPLE_BLOCK)
            else:
                q_p, k_p, v_p, gate_p, la_p, qm_p, x_p = _gla_proj(None, ws, PROJ_BLOCK, pending_p)
                q_s, k_s, v_s, gate_s, la_s, qm_s, x_s = _gla_proj(None, ws, SAMPLE_BLOCK, pending_s)
            tok_p, sp_t = _gla_chunk_scan(q_p, k_p, v_p, la_p, gla_norm_g[j], bp, tp)
            lanes_of = lambda t_: t_.transpose(0, 2, 1)
            os_, ss = _gla_step(lanes_of(q_s), lanes_of(k_s), lanes_of(v_s), lanes_of(la_s), gla_norm_g[j],
                                state_gla_S.transpose(0, 2, 3, 4, 1), j)
            tok_s = os_.transpose(0, 2, 1)
            p_gla_S.append(sp_t.transpose(0, 1, 3, 2))
            s_gla_S.append(ss.transpose(3, 0, 1, 2))
            layout_p = layout_s = "gla_heads"

        att_p = _mem_attn(qm_p.reshape(bp, tp, MEM_WIDTH), pk, pv, i, 512).reshape(np_, MEM_WIDTH)
        att_s = _mem_attn(qm_s.reshape(bs, ts, MEM_WIDTH), sk, sv, i, 1, memory_transposed=True).reshape(ns, MEM_WIDTH)
        wr_hi = bf(w_router[i])
        wr_lo = bf(w_router[i] - wr_hi.astype(F32))
        ws = [bf(w_out[i]), norm_ffn_g[i].reshape(1, D_MODEL), wr_hi, wr_lo, b_router[i]]
        x1_p, h2, logits_p = _out_proj(tok_p, gate_p, att_p, x_p, ws, layout_p, PROJ_BLOCK, m, 0)
        x1_s, h2, logits_s = _out_proj(tok_s, gate_s, att_s, x_s, ws, layout_s, SAMPLE_BLOCK, m, np_, h2_buffer=h2)
        tok_sorted, gate_sorted, inv, item_tile, item_expert, n_items, lo, hi = _route(
            jnp.concatenate([logits_p, logits_s], axis=0), np_, MOE_BLOCK)
        rows = _moe_ffn(h2[tok_sorted], gate_sorted, item_tile, item_expert, n_items, lo, hi,
                        exp_w_up, exp_w_down, i)
        gathered = rows[inv]
        g_next = norm_mix_g[i + 1] if i + 1 < depth else norm_final_g
        if i + 1 < depth and (i + 1) % 2 == 1:
            pending_p = (x1_p, gathered, 0, g_next)
            pending_s = (x1_s, gathered, 2 * np_, g_next)
        else:
            pending_p = pending_s = None
            x_p, h_p = _combine(x1_p, gathered, 0, g_next, LIGHT_BLOCK, i + 1 == depth)
            x_s, h_s = _combine(x1_s, gathered, 2 * np_, g_next, SAMPLE_BLOCK, i + 1 == depth)

    y_prompt = h_p.reshape(bp, tp, D_MODEL)
    y_sample = h_s.reshape(bs, ts, D_MODEL)
    return (y_prompt, y_sample, jnp.stack(p_rw_S), jnp.stack(p_rw_shift), jnp.stack(p_gla_S),
            prompt_mem_k, prompt_mem_v, jnp.stack(s_rw_S), jnp.stack(s_rw_shift), jnp.stack(s_gla_S))
```

```python
import functools

import numpy as np
import jax
import jax.numpy as jnp
from jax import lax
from jax.experimental import pallas as pl
from jax.experimental.pallas import tpu as pltpu

F32 = jnp.float32
BF16 = jnp.bfloat16

D_MODEL = 1024
TOK_WIDTH = 768
MEM_WIDTH = 256
MEM_HEADS = 4
MEM_HEAD_DIM = 64
N_MEM = 256
RWKV_HEADS = 12
RWKV_N = 64
RWKV_PAIRS = RWKV_HEADS // 2
RWKV_GN_EPS = 64e-5
GLA_HEADS = 4
GLA_KW = 384
GLA_DK = 96
GLA_DV = 192
GLA_TAU = 16.0
GLA_CHUNK = 64
GLA_TILE = 8
N_GROUPS = 4
EXPERTS_PER_GROUP = 8
N_EXPERTS = 32
EXPERT_FF = 512
NORM_EPS = 1e-6
ROUTER_LANES = 128
LANES = 128
SUBLANES = 8

PROJ_BLOCK = 512
LIGHT_BLOCK = 512
SAMPLE_BLOCK = 128
MOE_BLOCK = 256
SCAN_TIME_BLOCK = 32
VMEM_LIMIT = 56 * 1024 * 1024


def _cparams(*sem):
    return pltpu.CompilerParams(dimension_semantics=sem, vmem_limit_bytes=VMEM_LIMIT)


def _dot(a, b):
    return jnp.dot(a, b, preferred_element_type=F32)


def _dot_nt(a, b):
    return lax.dot_general(a, b, (((1,), (1,)), ((), ())), preferred_element_type=F32)


def _dot_tn(a, b):
    return lax.dot_general(a, b, (((0,), (0,)), ((), ())), preferred_element_type=F32)


def _rms(x, g):
    return x * lax.rsqrt(jnp.mean(x * x, axis=-1, keepdims=True) + NORM_EPS) * g


def _sigmoid(x):
    return 1.0 / (1.0 + jnp.exp(-x))


def _row_spec(tm, n, offset=0):
    return pl.BlockSpec((tm, n), lambda i: (i + offset, 0))


def _head_row_spec(tm, n):
    return pl.BlockSpec((GLA_HEADS, tm, n), lambda i: (0, i, 0))


def _full_spec(shape):
    nd = len(shape)
    return pl.BlockSpec(shape, lambda *_: (0,) * nd)


def _norm_kernel(x_ref, g_ref, o_ref):
    o_ref[...] = _rms(x_ref[...], g_ref[...])


def _norm(x, g, tm):
    m, d = x.shape
    return pl.pallas_call(
        _norm_kernel, grid=(m // tm,),
        in_specs=[_row_spec(tm, d), _full_spec((1, d))],
        out_specs=_row_spec(tm, d),
        out_shape=jax.ShapeDtypeStruct((m, d), F32),
        compiler_params=_cparams("parallel"), name="rms_norm")(x, g.reshape(1, d))


def _norm_matmul_kernel(x_ref, g_ref, w_ref, o_ref):
    o_ref[...] = _dot(_rms(x_ref[...], g_ref[...]).astype(BF16), w_ref[...])


def _norm_matmul(x, g, w, tm):
    m, d = x.shape
    n = w.shape[1]
    return pl.pallas_call(
        _norm_matmul_kernel, grid=(m // tm,),
        in_specs=[_row_spec(tm, d), _full_spec((1, d)), _full_spec((d, n))],
        out_specs=_row_spec(tm, n),
        out_shape=jax.ShapeDtypeStruct((m, n), F32),
        compiler_params=_cparams("parallel"), name="norm_matmul")(x, g.reshape(1, d), w.astype(BF16))


def _rwkv_proj_kernel(h_ref, hp_ref, *refs, blocks_per_seq, norm_input):
    if norm_input:
        ng_ref, refs = refs[0], refs[1:]
        normed = lambda t_: _rms(t_, ng_ref[...])
    else:
        normed = lambda t_: t_
    (mu_ref, wr_ref, wk_ref, wv_ref, wq_ref, w1_ref, w2_ref, w0_ref, a1_ref, a2_ref, a0_ref, g1_ref, g2_ref,
     r_out, k_out, v_out, d_out, a_out, g_out, q_out) = refs

    def put(out, val):
        if blocks_per_seq:
            for p in range(RWKV_PAIRS):
                out[p] = val[:, p * LANES:(p + 1) * LANES]
        else:
            out[...] = val

    h = normed(h_ref[...])
    if blocks_per_seq:
        seq_start = (pl.program_id(0) % blocks_per_seq) == 0
        before = jnp.where(seq_start, 0.0, normed(hp_ref[SUBLANES - 1:SUBLANES, :]))
        row = lax.broadcasted_iota(jnp.int32, (h.shape[0], 1), 0)
        hp = jnp.where(row == 0, before, pltpu.roll(h, 1, axis=0))
    else:
        hp = hp_ref[...]
    xx = hp - h

    def mix(j):
        return (h + xx * mu_ref[j:j + 1, :]).astype(BF16)

    put(r_out, _dot(mix(0), wr_ref[...]))
    wl = w0_ref[...] + _dot(jnp.tanh(_dot(mix(1), w1_ref[...])).astype(BF16), w2_ref[...])
    put(k_out, _dot(mix(2), wk_ref[...]))
    put(v_out, _dot(mix(3), wv_ref[...]))
    al = a0_ref[...] + _dot(_dot(mix(4), a1_ref[...]).astype(BF16), a2_ref[...])
    g_out[...] = _dot(_sigmoid(_dot(mix(5), g1_ref[...])).astype(BF16), g2_ref[...])
    q_out[...] = _dot(h.astype(BF16), wq_ref[...])
    z = -wl
    softplus = jnp.maximum(z, 0.0) + jnp.log(1.0 + jnp.exp(-jnp.abs(z)))
    put(d_out, jnp.exp(-jnp.exp(-softplus - 0.5)))
    put(a_out, _sigmoid(al))


def _rwkv_proj(h, h_prev, ws, tm, seq_len, norm_g=None):
    m = h.shape[0]
    if norm_g is not None:
        assert h_prev is None
        ws = [norm_g.reshape(1, D_MODEL)] + list(ws)
    tw = TOK_WIDTH
    if h_prev is None:
        assert seq_len % tm == 0 and tm % SUBLANES == 0
        per8 = tm // SUBLANES
        hp_spec = pl.BlockSpec((SUBLANES, D_MODEL), lambda i: (jnp.maximum(i * per8 - 1, 0), 0))
        h_prev, blocks_per_seq = h, seq_len // tm
    else:
        hp_spec, blocks_per_seq = _row_spec(tm, D_MODEL), 0
    if blocks_per_seq:
        scan_spec = pl.BlockSpec((RWKV_PAIRS, tm, LANES), lambda i: (0, i, 0))
        scan_shape = jax.ShapeDtypeStruct((RWKV_PAIRS, m, LANES), F32)
    else:
        scan_spec, scan_shape = _row_spec(tm, tw), jax.ShapeDtypeStruct((m, tw), F32)
    return pl.pallas_call(
        functools.partial(_rwkv_proj_kernel, blocks_per_seq=blocks_per_seq, norm_input=norm_g is not None),
        grid=(m // tm,),
        in_specs=[_row_spec(tm, D_MODEL), hp_spec] + [_full_spec(w.shape) for w in ws],
        out_specs=[scan_spec] * 5 + [_row_spec(tm, tw), _row_spec(tm, MEM_WIDTH)],
        out_shape=[scan_shape] * 5 + [jax.ShapeDtypeStruct((m, tw), F32), jax.ShapeDtypeStruct((m, MEM_WIDTH), F32)],
        compiler_params=_cparams("parallel"), name="rwkv_proj")(h, h_prev, *ws)


def _pad_lanes(x):
    short = LANES - x.shape[-1]
    if short == 0:
        return x
    return jnp.concatenate([x, jnp.zeros(x.shape[:-1] + (short,), x.dtype)], axis=-1)


def _rwkv_scan_kernel(r_ref, k_ref, v_ref, d_ref, a_ref, kkp_ref, kap_ref, rkp_ref, lng_ref, lnb_ref, s0_ref,
                      y_ref, sfin_ref, s_scr, v_scr, yrow_scr, *, tc):
    n = RWKV_N
    nl = r_ref.shape[-1]

    @pl.when(pl.program_id(1) == 0)
    def _():
        s_scr[...] = _pad_lanes(s0_ref[...])

    kkp = _pad_lanes(kkp_ref[...])
    kap = _pad_lanes(kap_ref[...])
    rkp = _pad_lanes(rkp_ref[...])
    lng = _pad_lanes(lng_ref[...])
    lnb = _pad_lanes(lnb_ref[...])

    def step(t, carry):
        r_t = _pad_lanes(r_ref[t])
        k_t = _pad_lanes(k_ref[t])
        v_t = _pad_lanes(v_ref[t])
        d_t = _pad_lanes(d_ref[t])
        a_t = _pad_lanes(a_ref[t])
        v_scr[...] = v_t
        kkr = k_t * kkp
        nrm = jnp.maximum(jnp.sqrt(jnp.sum(kkr * kkr, axis=0, keepdims=True)), 1e-12)
        kk = kkr * (1.0 / nrm)
        k2 = k_t * (1.0 + (a_t - 1.0) * kap)
        nkk = -kk
        b_t = kk * a_t

        def ibody(i, c):
            s_i = s_scr[i]
            sa = jnp.sum(s_i * nkk, axis=0, keepdims=True)
            v_i = v_scr[pl.ds(i, 1), :]
            s_n = s_i * d_t + sa * b_t + v_i * k2
            s_scr[i] = s_n
            yrow_scr[pl.ds(i, 1), :] = jnp.sum(s_n * r_t, axis=0, keepdims=True)
            return c

        lax.fori_loop(0, n, ibody, 0, unroll=8)
        y = yrow_scr[...]
        yc = y - jnp.mean(y, axis=0, keepdims=True)
        var = jnp.mean(yc * yc, axis=0, keepdims=True)
        gn = yc * lax.rsqrt(var + RWKV_GN_EPS) * lng + lnb
        bonus = jnp.sum(r_t * k2 * rkp, axis=0, keepdims=True) * v_t
        y_ref[t] = (gn + bonus)[:, :nl]
        return carry

    lax.fori_loop(0, tc, step, 0)

    @pl.when(pl.program_id(1) == pl.num_programs(1) - 1)
    def _():
        sfin_ref[...] = s_scr[:, :, :nl]


def _rwkv_scan(r, k, v, d, a, params, s0, tc):
    t, n, l = r.shape
    groups, lane_block = s0.shape[0], s0.shape[3]
    assert groups * lane_block == l
    seq = pl.BlockSpec((tc, n, lane_block), lambda li, ti: (ti, 0, li))
    par = pl.BlockSpec((n, lane_block), lambda li, ti: (0, li))
    st = pl.BlockSpec((None, n, n, lane_block), lambda li, ti: (li, 0, 0, 0))
    return pl.pallas_call(
        functools.partial(_rwkv_scan_kernel, tc=tc), grid=(groups, t // tc),
        in_specs=[seq] * 5 + [par] * 5 + [st],
        out_specs=[seq, st],
        out_shape=[jax.ShapeDtypeStruct((t, n, l), F32), jax.ShapeDtypeStruct((groups, n, n, lane_block), F32)],
        scratch_shapes=[pltpu.VMEM((n, n, LANES), F32), pltpu.VMEM((n, LANES), F32), pltpu.VMEM((n, LANES), F32)],
        compiler_params=_cparams("parallel", "arbitrary"), name="rwkv_scan")(r, k, v, d, a, *params, s0)


PREP_TILES = 6
DECAY_WINDOW = 16


def _rwkv_prep_kernel(r_ref, k_ref, v_ref, d_ref, a_ref, kkp_ref, kap_ref, o_ref, *, tc, nb):
    n = RWKV_N
    rows_per_pair = nb * tc
    ins = [ref.reshape(RWKV_PAIRS * rows_per_pair, LANES) for ref in (r_ref, k_ref, v_ref, d_ref, a_ref)]
    zero_rows = jnp.zeros((LANES - RWKV_HEADS * nb, LANES), F32)
    first_half = lax.broadcasted_iota(jnp.int32, (nb, LANES), 1) < n
    kkp = kkp_ref[...]
    kap = kap_ref[...]

    def load_transposed_pair(x2, t):
        pieces = []
        for p in range(RWKV_PAIRS):
            now = x2[pl.ds(p * rows_per_pair + t, nb, stride=tc), :]
            nxt = x2[pl.ds(p * rows_per_pair + t + 1, nb, stride=tc), :]
            pieces += [jnp.where(first_half, now, pltpu.roll(nxt, n, axis=1)),
                       jnp.where(first_half, pltpu.roll(now, n, axis=1), nxt)]
        both = jnp.concatenate(pieces + [zero_rows], axis=0).T
        return both[:n, :], both[n:, :]

    def emit(t, c_prev, r_t, k_t, v_t, d_t, a_t):
        kkr = k_t * kkp
        nrm = jnp.maximum(jnp.sqrt(jnp.sum(kkr * kkr, axis=0, keepdims=True)), 1e-12)
        kk = kkr * (1.0 / nrm)
        k2 = k_t * (1.0 + (a_t - 1.0) * kap)
        c_prev = jnp.where(t % DECAY_WINDOW == 0, 1.0, c_prev)
        c_t = c_prev * d_t
        inv = 1.0 / jnp.maximum(c_t, 1e-30)
        o_ref[t, 0] = -kk * c_prev
        o_ref[t, 1] = c_t
        o_ref[t, 2] = kk * a_t * inv
        o_ref[t, 3] = k2 * inv
        o_ref[t, 4] = r_t * c_t
        o_ref[t, 5] = v_t
        return c_t

    def two_steps(u, c):
        t = 2 * u
        tiles = [load_transposed_pair(x2, t) for x2 in ins]
        c = emit(t, c, *[tile[0] for tile in tiles])
        return emit(t + 1, c, *[tile[1] for tile in tiles])

    lax.fori_loop(0, tc // 2, two_steps, jnp.ones((n, LANES), F32), unroll=2)


def _rwkv_state_scan_kernel(x_ref, rkp_ref, lng_ref, lnb_ref, y_ref, sfin_ref, s_scr, sa_scr, yrow_scr, *, tc):
    n = RWKV_N
    groups = n // SUBLANES

    @pl.when(pl.program_id(0) == 0)
    def _():
        s_scr[...] = jnp.zeros(s_scr.shape, F32)

    rkp = rkp_ref[...]
    lng = lng_ref[...]
    lnb = lnb_ref[...]
    sub = lax.broadcasted_iota(jnp.int32, (SUBLANES, LANES), 0)
    low4 = sub < 4
    low2 = (sub & 3) < 2
    low1 = (sub & 1) == 0

    def fold(x, y, dist, low):
        if dist == 4:
            return jnp.where(low, x, y) + pltpu.roll(jnp.where(low, y, x), 4, axis=0)
        return (jnp.where(low, x, pltpu.roll(y, dist, axis=0))
                + jnp.where(low, pltpu.roll(x, SUBLANES - dist, axis=0), y))

    def sublane_sums(ps):
        z = [fold(ps[0], ps[4], 4, low4), fold(ps[2], ps[6], 4, low4),
             fold(ps[1], ps[5], 4, low4), fold(ps[3], ps[7], 4, low4)]
        return fold(fold(z[0], z[1], 2, low2), fold(z[2], z[3], 2, low2), 1, low1)

    def tile_sum(x):
        acc = x[0:SUBLANES]
        for u in range(1, groups):
            acc = acc + x[u * SUBLANES:(u + 1) * SUBLANES]
        return acc

    def step(t, carry):
        nkk = x_ref[t, 0]
        b_t = x_ref[t, 2]
        k2 = x_ref[t, 3]
        r_t = x_ref[t, 4]
        for g in range(groups):
            sa_scr[g * SUBLANES:(g + 1) * SUBLANES, :] = sublane_sums(
                [tile_sum(s_scr[g * SUBLANES + u] * nkk) for u in range(SUBLANES)])
        for g in range(groups):
            ps = []
            for u in range(SUBLANES):
                i = g * SUBLANES + u
                s_n = s_scr[i] + sa_scr[pl.ds(i, 1), :] * b_t + x_ref[t, 5, pl.ds(i, 1), :] * k2
                s_scr[i] = s_n
                ps.append(tile_sum(s_n * r_t))
            yrow_scr[g * SUBLANES:(g + 1) * SUBLANES, :] = sublane_sums(ps)
        y = yrow_scr[...]
        yc = y - jnp.mean(y, axis=0, keepdims=True)
        var = jnp.mean(yc * yc, axis=0, keepdims=True)
        bonus = jnp.sum(r_t * k2 * rkp, axis=0, keepdims=True) * x_ref[t, 5]
        y_ref[t] = yc * lax.rsqrt(var + RWKV_GN_EPS) * lng + lnb + bonus

        @pl.when(t % DECAY_WINDOW == DECAY_WINDOW - 1)
        def _():
            c_t = x_ref[t, 1]
            for i in range(n):
                s_scr[i] = s_scr[i] * c_t

        return carry

    lax.fori_loop(0, tc, step, 0)

    @pl.when(pl.program_id(0) == pl.num_programs(0) - 1)
    def _():
        sfin_ref[...] = s_scr[...]


def _rwkv_unprep_kernel(y_ref, o_ref, *, tc, nb):
    n = RWKV_N
    rows_per_pair = nb * tc
    o2 = o_ref.reshape(RWKV_PAIRS * rows_per_pair, LANES)
    first_half = lax.broadcasted_iota(jnp.int32, (nb, LANES), 1) < n

    def two_steps(u, carry):
        t = 2 * u
        w = jnp.concatenate([y_ref[t], y_ref[t + 1]], axis=0).T
        for p in range(RWKV_PAIRS):
            even = w[(2 * p) * nb:(2 * p + 1) * nb, :]
            odd = w[(2 * p + 1) * nb:(2 * p + 2) * nb, :]
            o2[pl.ds(p * rows_per_pair + t, nb, stride=tc), :] = jnp.where(
                first_half, even, pltpu.roll(odd, n, axis=1))
            o2[pl.ds(p * rows_per_pair + t + 1, nb, stride=tc), :] = jnp.where(
                first_half, pltpu.roll(even, n, axis=1), odd)
        return carry

    lax.fori_loop(0, tc // 2, two_steps, 0, unroll=2)


def _rwkv_prompt_mixer(r, k, v, d, a, kkp, kap, rkp, lng, lnb, nb, t, tc):
    n = RWKV_N
    assert tc % SUBLANES == 0 and tc % 4 == 0 and t % tc == 0 and RWKV_HEADS * nb <= LANES
    assert tc % DECAY_WINDOW == 0
    grid = (t // tc,)
    tok = pl.BlockSpec((RWKV_PAIRS, nb, tc, LANES), lambda ti: (0, 0, ti, 0))
    par = pl.BlockSpec((n, LANES), lambda ti: (0, 0))
    tiles = pl.BlockSpec((tc, PREP_TILES, n, LANES), lambda ti: (ti, 0, 0, 0))
    ytile = pl.BlockSpec((tc, n, LANES), lambda ti: (ti, 0, 0))
    prepared = pl.pallas_call(
        functools.partial(_rwkv_prep_kernel, tc=tc, nb=nb), grid=grid,
        in_specs=[tok] * 5 + [par] * 2, out_specs=tiles,
        out_shape=jax.ShapeDtypeStruct((t, PREP_TILES, n, LANES), F32),
        compiler_params=_cparams("parallel"), name="rwkv_prep")(
            *[x.reshape(RWKV_PAIRS, nb, t, LANES) for x in (r, k, v, d, a)], kkp, kap)
    y, s = pl.pallas_call(
        functools.partial(_rwkv_state_scan_kernel, tc=tc), grid=grid,
        in_specs=[tiles, par, par, par],
        out_specs=[ytile, pl.BlockSpec((n, n, LANES), lambda ti: (0, 0, 0))],
        out_shape=[jax.ShapeDtypeStruct((t, n, LANES), F32), jax.ShapeDtypeStruct((n, n, LANES), F32)],
        scratch_shapes=[pltpu.VMEM((n, n, LANES), F32), pltpu.VMEM((n, LANES), F32), pltpu.VMEM((n, LANES), F32)],
        compiler_params=_cparams("arbitrary"), name="rwkv_state_scan")(prepared, rkp, lng, lnb)
    tok_out = pl.pallas_call(
        functools.partial(_rwkv_unprep_kernel, tc=tc, nb=nb), grid=grid,
        in_specs=[ytile], out_specs=tok,
        out_shape=jax.ShapeDtypeStruct((RWKV_PAIRS, nb, t, LANES), F32),
        compiler_params=_cparams("parallel"), name="rwkv_unprep")(y)
    return tok_out.reshape(RWKV_PAIRS, nb * t, LANES), s


def _gla_proj_kernel(*refs, fused_combine):
    if fused_combine:
        x_ref, r0_ref, r1_ref, g_ref = refs[:4]
        refs, x_out = refs[4:-1], refs[-1]
        x2 = x_ref[...] + (r0_ref[...] + r1_ref[...])
        x_out[...] = x2
        h = _rms(x2, g_ref[...])
    else:
        h, refs = refs[0][...], refs[1:]
    (wq_ref, wk_ref, wv_ref, wr_ref, wm_ref, a1_ref, a2_ref, ab_ref,
     q_out, k_out, v_out, r_out, la_out, qm_out) = refs
    hb = h.astype(BF16)
    low = _dot(hb, a1_ref[...]).astype(BF16)
    for hd in range(GLA_HEADS):
        q_out[hd] = _dot(hb, wq_ref[hd])
        k_out[hd] = _dot(hb, wk_ref[hd])
        v_out[hd] = _dot(hb, wv_ref[hd])
        r_out[hd] = _dot(hb, wr_ref[hd])
        x = _dot(low, a2_ref[hd]) + ab_ref[hd]
        log_sigmoid = jnp.minimum(x, 0.0) - jnp.log(1.0 + jnp.exp(-jnp.abs(x)))
        la_out[hd] = log_sigmoid / GLA_TAU
    qm_out[...] = _dot(hb, wm_ref[...])


def _gla_weights(w_in, a1, a2, ab):
    kw, tw, nh = GLA_KW, TOK_WIDTH, GLA_HEADS

    def heads(w, d):
        return w.reshape(w.shape[0], nh, d).transpose(1, 0, 2)

    return [heads(w_in[:, :kw], GLA_DK).astype(BF16), heads(w_in[:, kw:2 * kw], GLA_DK).astype(BF16),
            heads(w_in[:, 2 * kw:2 * kw + tw], GLA_DV).astype(BF16),
            heads(w_in[:, 2 * kw + tw:3 * tw], GLA_DV).astype(BF16), w_in[:, 3 * tw:].astype(BF16),
            a1.astype(BF16), heads(a2, GLA_DK).astype(BF16), ab.reshape(nh, 1, GLA_DK)]


def _gla_proj(h, ws, tm, pending=None):
    nh = GLA_HEADS
    widths = [GLA_DK, GLA_DK, GLA_DV, GLA_DV, GLA_DK]
    if pending is None:
        m = h.shape[0]
        lead_specs, lead_args = [_row_spec(tm, D_MODEL)], [h]
    else:
        x1, gathered, first_row, norm_g = pending
        m = x1.shape[0]
        lead_specs = [_row_spec(tm, D_MODEL), _row_spec(tm, D_MODEL, first_row // tm),
                      _row_spec(tm, D_MODEL, (first_row + m) // tm), _full_spec((1, D_MODEL))]
        lead_args = [x1, gathered, gathered, norm_g.reshape(1, D_MODEL)]
    out_specs = [_head_row_spec(tm, w) for w in widths] + [_row_spec(tm, MEM_WIDTH)]
    out_shape = [jax.ShapeDtypeStruct((nh, m, w), F32) for w in widths] + [jax.ShapeDtypeStruct((m, MEM_WIDTH), F32)]
    if pending is not None:
        out_specs.append(_row_spec(tm, D_MODEL))
        out_shape.append(jax.ShapeDtypeStruct((m, D_MODEL), F32))
    return pl.pallas_call(
        functools.partial(_gla_proj_kernel, fused_combine=pending is not None), grid=(m // tm,),
        in_specs=lead_specs + [_full_spec(w.shape) for w in ws],
        out_specs=out_specs, out_shape=out_shape,
        compiler_params=_cparams("parallel"), name="gla_proj")(*lead_args, *ws)


def _gla_out_norm(o, g):
    return o * lax.rsqrt(jnp.mean(o * o, axis=-1, keepdims=True) + NORM_EPS) * g


def _gla_chunk_kernel(q_ref, k_ref, v_ref, la_ref, ng_ref, o_ref, sfin_ref, st_scr):
    c, dk = GLA_CHUNK, GLA_DK

    @pl.when(pl.program_id(1) == 0)
    def _():
        st_scr[...] = jnp.zeros(st_scr.shape, F32)

    row = lax.broadcasted_iota(jnp.int32, (c, c), 0)
    col = lax.broadcasted_iota(jnp.int32, (c, c), 1)
    tril = (row >= col).astype(F32)
    rr = lax.broadcasted_iota(jnp.int32, (c, dk), 0)
    ones_sum = jnp.ones((dk, LANES), BF16)

    heads = range(GLA_HEADS)
    tril_b = tril.astype(BF16)

    def cumsum_rows(la):
        hi = la.astype(BF16)
        r1 = la - hi.astype(F32)
        mid = r1.astype(BF16)
        lo = (r1 - mid.astype(F32)).astype(BF16)
        return _dot(tril_b, hi) + _dot(tril_b, mid) + _dot(tril_b, lo)

    def tile_roll(x, dlt):
        return pltpu.roll(x.reshape(c // GLA_TILE, GLA_TILE, dk), dlt, axis=1).reshape(c, dk)

    k = [k_ref[hd] for hd in heads]
    vb = [v_ref[hd].astype(BF16) for hd in heads]
    q = [q_ref[hd] * (dk ** -0.5) for hd in heads]
    b = [cumsum_rows(la_ref[hd]) for hd in heads]
    st = [st_scr[hd] for hd in heads]
    inter = [_dot_nt((q[hd] * jnp.exp(b[hd])).astype(BF16), st[hd].astype(BF16)) for hd in heads]

    att = [jnp.zeros((c, c), F32) for hd in heads]
    blk = c // 2
    while blk >= GLA_TILE:
        two = 2 * blk
        upper = (rr & (two - 1)) >= blk
        same_block = (row ^ col) < two
        parts = []
        for hd in heads:
            b_ref_rows = jnp.concatenate(
                [jnp.broadcast_to(b[hd][s0 + blk - 1:s0 + blk, :], (two, dk)) for s0 in range(0, c, two)], axis=0)
            q_l = jnp.where(upper, q[hd] * jnp.exp(jnp.minimum(b[hd] - b_ref_rows, 0.0)), 0.0).astype(BF16)
            k_l = jnp.where(upper, 0.0, k[hd] * jnp.exp(jnp.minimum(b_ref_rows - b[hd], 0.0))).astype(BF16)
            parts.append(_dot_nt(q_l, k_l))
        att = [att[hd] + jnp.where(same_block, parts[hd], 0.0) for hd in heads]
        blk //= 2

    sums = []
    for hd in heads:
        prods = [(q[hd] * k[hd]).astype(BF16)]
        for dlt in range(1, GLA_TILE):
            p = q[hd] * tile_roll(k[hd], dlt) * jnp.exp(jnp.minimum(b[hd] - tile_roll(b[hd], dlt), 0.0))
            prods.append(jnp.where((rr & (GLA_TILE - 1)) >= dlt, p, 0.0).astype(BF16))
        sums.append(_dot(jnp.concatenate(prods, axis=0), ones_sum))
    for dlt in range(GLA_TILE):
        on_diag = col == row - dlt
        att = [att[hd] + jnp.where(on_diag, sums[hd][dlt * c:(dlt + 1) * c, :c], 0.0) for hd in heads]

    o = [inter[hd] + _dot(att[hd].astype(BF16), vb[hd]) for hd in heads]
    for hd in heads:
        o_ref[hd] = _gla_out_norm(o[hd], ng_ref[...])
    for hd in heads:
        b_end = b[hd][c - 1:c, :]
        kd = (k[hd] * jnp.exp(b_end - b[hd])).astype(BF16)
        st_scr[hd] = jnp.exp(b_end) * st[hd] + _dot_tn(vb[hd], kd)

    @pl.when(pl.program_id(1) == pl.num_programs(1) - 1)
    def _():
        sfin_ref[0] = st_scr[...]


def _gla_chunk_scan(q, k, v, la, norm_g, batch, t):
    nh, dk, dv, c = GLA_HEADS, GLA_DK, GLA_DV, GLA_CHUNK
    nc = t // c
    kspec = pl.BlockSpec((nh, c, dk), lambda i, j: (0, i * nc + j, 0))
    vspec = pl.BlockSpec((nh, c, dv), lambda i, j: (0, i * nc + j, 0))
    sspec = pl.BlockSpec((1, nh, dv, dk), lambda i, j: (i, 0, 0, 0))
    return pl.pallas_call(
        _gla_chunk_kernel, grid=(batch, nc),
        in_specs=[kspec, kspec, vspec, kspec, pl.BlockSpec((1, dv), lambda i, j: (0, 0))],
        out_specs=[vspec, sspec],
        out_shape=[jax.ShapeDtypeStruct((nh, batch * t, dv), F32), jax.ShapeDtypeStruct((batch, nh, dv, dk), F32)],
        scratch_shapes=[pltpu.VMEM((nh, dv, dk), F32)],
        compiler_params=_cparams("parallel", "arbitrary"), name="gla_chunk")(q, k, v, la, norm_g.reshape(1, dv))


GLA_STEP_KEY_SPLIT = 2


def _gla_step_kernel(q_ref, k_ref, v_ref, la_ref, ng_ref, s0_ref, o_ref, s_ref):
    rows = s0_ref.shape[1]

    @pl.when(pl.program_id(1) == 0)
    def _():
        o_ref[...] = jnp.zeros(o_ref.shape, F32)

    q = q_ref[0] * (GLA_DK ** -0.5)
    decay = jnp.exp(la_ref[0])
    kb = k_ref[0].astype(BF16).astype(F32)
    vb = v_ref[0].astype(BF16).astype(F32)
    acc = o_ref[0]
    for r in range(rows):
        s_new = decay[r:r + 1, :] * s0_ref[0, r] + kb[r:r + 1, :] * vb
        s_ref[0, r] = s_new
        acc = acc + q[r:r + 1, :] * s_new
    o_ref[0] = acc

    @pl.when(pl.program_id(1) == pl.num_programs(1) - 1)
    def _():
        o_ref[0] = acc * lax.rsqrt(jnp.mean(acc * acc, axis=0, keepdims=True) + NORM_EPS) * ng_ref[...]


def _gla_step(q, k, v, la, norm_g, states, layer):
    nh, dk, b = q.shape
    dv = v.shape[1]
    rows = dk // GLA_STEP_KEY_SPLIT
    assert rows * GLA_STEP_KEY_SPLIT == dk and rows % SUBLANES == 0
    kspec = pl.BlockSpec((1, rows, b), lambda h, r: (h, r, 0))
    vspec = pl.BlockSpec((1, dv, b), lambda h, r: (h, 0, 0))
    return pl.pallas_call(
        _gla_step_kernel, grid=(nh, GLA_STEP_KEY_SPLIT),
        in_specs=[kspec, kspec, vspec, kspec, _full_spec((dv, b)),
                  pl.BlockSpec((None, 1, rows, dv, b), lambda h, r: (layer, h, r, 0, 0))],
        out_specs=[vspec, pl.BlockSpec((1, rows, dv, b), lambda h, r: (h, r, 0, 0))],
        out_shape=[jax.ShapeDtypeStruct((nh, dv, b), F32), jax.ShapeDtypeStruct((nh, dk, dv, b), F32)],
        compiler_params=_cparams("parallel", "arbitrary"), name="gla_step")(
            q, k, v, la, jnp.broadcast_to(norm_g.reshape(dv, 1), (dv, b)), states)


def _mem_attn_kernel(q_ref, k_ref, v_ref, o_ref, *, memory_transposed):
    q = q_ref[0]
    k = k_ref[0].astype(BF16)
    ones = jnp.ones((LANES, N_MEM) if memory_transposed else (N_MEM, LANES), BF16)
    v_ones = jnp.concatenate([v_ref[0].astype(BF16), ones], axis=0 if memory_transposed else 1)
    scores, values = (_dot, _dot_nt) if memory_transposed else (_dot_nt, _dot)
    head_of_lane = lax.broadcasted_iota(jnp.int32, (1, MEM_WIDTH), 1) // MEM_HEAD_DIM
    heads = range(MEM_HEADS)
    mine = [head_of_lane == h for h in heads]
    qh = [jnp.where(mine[h], q, 0.0).astype(BF16) for h in heads]
    s = [scores(qh[h], k) * (MEM_HEAD_DIM ** -0.5) for h in heads]
    e = [jnp.exp(s[h] - jnp.max(s[h], axis=-1, keepdims=True)).astype(BF16) for h in heads]
    ev = [values(e[h], v_ones) for h in heads]
    out = jnp.zeros(q.shape, F32)
    for h in heads:
        inv = 1.0 / ev[h][:, MEM_WIDTH:]
        out = out + jnp.where(mine[h], ev[h][:, :MEM_WIDTH] * jnp.concatenate([inv, inv], axis=1), 0.0)
    o_ref[0] = out


def _mem_attn(q, mem_k, mem_v, layer, tq, memory_transposed=False):
    b, t, w = q.shape
    qspec = pl.BlockSpec((1, tq, w), lambda i, j: (i, j, 0))
    mspec = pl.BlockSpec((None, 1, N_MEM, w), lambda i, j: (layer, i, 0, 0))
    return pl.pallas_call(
        functools.partial(_mem_attn_kernel, memory_transposed=memory_transposed), grid=(b, t // tq),
        in_specs=[qspec, mspec, mspec], out_specs=qspec,
        out_shape=jax.ShapeDtypeStruct((b, t, w), F32),
        compiler_params=_cparams("parallel", "parallel"), name="mem_attn")(q, mem_k, mem_v)


def _out_proj_kernel(*refs, layout, aliased):
    tok_ref, gate_ref, att_ref, x_ref, wo_ref, g_ref, wr_hi_ref, wr_lo_ref, br_ref = refs[:9]
    x1_out, h2_out, logit_out = refs[9 + aliased:]
    x1 = x_ref[...] + _dot(att_ref[...].astype(BF16), wo_ref[TOK_WIDTH:, :])
    if layout == "gla_heads":
        for hd in range(GLA_HEADS):
            gate = gate_ref[hd]
            mixed = (tok_ref[hd] * (gate * _sigmoid(gate))).astype(BF16)
            x1 = x1 + _dot(mixed, wo_ref[hd * GLA_DV:(hd + 1) * GLA_DV, :])
    else:
        if layout == "rwkv_pairs":
            tok = jnp.concatenate([tok_ref[p] for p in range(RWKV_PAIRS)], axis=1)
        else:
            tok = tok_ref[...]
        x1 = x1 + _dot((tok * gate_ref[...]).astype(BF16), wo_ref[:TOK_WIDTH, :])
    x1_out[...] = x1
    h2 = _rms(x1, g_ref[...])
    h2_out[...] = h2
    h_hi = h2.astype(BF16)
    h_lo = (h2 - h_hi.astype(F32)).astype(BF16)
    logit_out[...] = (_dot(h_hi, wr_hi_ref[...]) + _dot(h_lo, wr_hi_ref[...]) + _dot(h_hi, wr_lo_ref[...])
                      + br_ref[...])


def _out_proj(tok, gate, att, x, ws, layout, tm, h2_rows, h2_row_offset, h2_buffer=None):
    m = x.shape[0]
    aliased = h2_buffer is not None
    rows_spec = _row_spec(tm, TOK_WIDTH)
    tok_spec, gate_spec = {
        "rows": (rows_spec, rows_spec),
        "rwkv_pairs": (pl.BlockSpec((RWKV_PAIRS, tm, LANES), lambda i: (0, i, 0)), rows_spec),
        "gla_heads": (_head_row_spec(tm, GLA_DV), _head_row_spec(tm, GLA_DV))}[layout]
    in_specs = ([tok_spec, gate_spec, _row_spec(tm, MEM_WIDTH), _row_spec(tm, D_MODEL)]
                + [_full_spec(w.shape) for w in ws])
    args = [tok, gate, att, x, *ws]
    if aliased:
        in_specs.append(pl.BlockSpec(memory_space=pl.ANY))
        args.append(h2_buffer)
    return pl.pallas_call(
        functools.partial(_out_proj_kernel, layout=layout, aliased=int(aliased)), grid=(m // tm,),
        in_specs=in_specs,
        out_specs=[_row_spec(tm, D_MODEL), _row_spec(tm, D_MODEL, h2_row_offset // tm), _row_spec(tm, ROUTER_LANES)],
        out_shape=[jax.ShapeDtypeStruct((m, D_MODEL), F32), jax.ShapeDtypeStruct((h2_rows, D_MODEL), F32),
                   jax.ShapeDtypeStruct((m, ROUTER_LANES), F32)],
        input_output_aliases={len(args) - 1: 1} if aliased else {},
        compiler_params=_cparams("parallel"), name="out_proj")(*args)


def _moe_kernel(tile_ref, exp_ref, nitem_ref, lo_ref, hi_ref, x_ref, gate_ref, wu_ref, wd_ref, o_ref, wu_scr, wd_scr):
    w = pl.program_id(0)
    prev = jnp.maximum(w - 1, 0)
    e = exp_ref[w]
    valid = w < nitem_ref[0]

    @pl.when(jnp.logical_and(valid, jnp.logical_or(w == 0, e != exp_ref[prev])))
    def _():
        wu_scr[...] = wu_ref[0, 0].astype(BF16)
        wd_scr[...] = wd_ref[0, 0].astype(BF16)

    @pl.when(jnp.logical_or(w == 0, tile_ref[w] != tile_ref[prev]))
    def _():
        o_ref[...] = jnp.zeros(o_ref.shape, F32)

    @pl.when(valid)
    def _():
        gu = _dot(x_ref[...].astype(BF16), wu_scr[...])
        g = gu[:, :EXPERT_FF]
        act = (g * _sigmoid(g) * gu[:, EXPERT_FF:]).astype(BF16)
        gate_col = jnp.concatenate([gate_ref[...], jnp.zeros((LANES - 1, MOE_BLOCK), F32)], axis=0).T[:, :1]
        out = _dot(act, wd_scr[...]) * gate_col
        rows = tile_ref[w] * MOE_BLOCK + lax.broadcasted_iota(jnp.int32, (MOE_BLOCK, 1), 0)
        mine = jnp.logical_and(rows >= lo_ref[e], rows < hi_ref[e])
        o_ref[...] = o_ref[...] + jnp.where(mine, out, 0.0)


def _moe_ffn(xs, row_gate, item_tile, item_expert, n_items, lo, hi, w_up, w_down, layer):
    tm = MOE_BLOCK
    n_work = item_tile.shape[0]
    a = xs.shape[0]
    row_map = lambda w, tile, ex, ni, lo_, hi_: (tile[w], 0)
    exp_map = lambda w, tile, ex, ni, lo_, hi_: (layer, ex[w], 0, 0)
    grid_spec = pltpu.PrefetchScalarGridSpec(
        num_scalar_prefetch=5, grid=(n_work,),
        in_specs=[pl.BlockSpec((tm, D_MODEL), row_map),
                  pl.BlockSpec((None, 1, tm), lambda w, tile, ex, ni, lo_, hi_: (tile[w], 0, 0)),
                  pl.BlockSpec((1, 1, D_MODEL, 2 * EXPERT_FF), exp_map),
                  pl.BlockSpec((1, 1, EXPERT_FF, D_MODEL), exp_map)],
        out_specs=pl.BlockSpec((tm, D_MODEL), row_map),
        scratch_shapes=[pltpu.VMEM((D_MODEL, 2 * EXPERT_FF), BF16), pltpu.VMEM((EXPERT_FF, D_MODEL), BF16)])
    return pl.pallas_call(
        _moe_kernel, grid_spec=grid_spec,
        out_shape=jax.ShapeDtypeStruct((a, D_MODEL), F32),
        compiler_params=_cparams("arbitrary"), name="moe_ffn")(
            item_tile, item_expert, n_items, lo, hi, xs, row_gate.reshape(a // tm, 1, tm), w_up, w_down)


def _route(logits, n_prompt, tm):
    m = logits.shape[0]
    n_sample = m - n_prompt
    a = 2 * m
    gl = logits[:, :N_GROUPS]
    el = logits[:, N_GROUPS:N_GROUPS + N_EXPERTS]
    group = jnp.argmax(gl, -1).astype(jnp.int32)
    p_group = jnp.max(jax.nn.softmax(gl, -1), -1, keepdims=True)
    experts_row = jnp.arange(N_EXPERTS, dtype=jnp.int32)[None, :]
    masked = jnp.where(experts_row // EXPERTS_PER_GROUP == group[:, None], el, -jnp.inf)
    e1 = jnp.argmax(masked, -1).astype(jnp.int32)
    masked2 = jnp.where(experts_row == e1[:, None], -jnp.inf, masked)
    e2 = jnp.argmax(masked2, -1).astype(jnp.int32)
    top_val = jnp.stack([jnp.max(masked, -1), jnp.max(masked2, -1)], axis=-1)
    gate = p_group * jax.nn.softmax(top_val, -1)
    expert = jnp.stack([e1, e2], axis=-1)

    def by_id(t):
        return jnp.concatenate([t[:n_prompt, 0], t[:n_prompt, 1], t[n_prompt:, 0], t[n_prompt:, 1]])

    tok_of_id = jnp.asarray(np.concatenate([np.arange(n_prompt), np.arange(n_prompt),
                                            n_prompt + np.arange(n_sample), n_prompt + np.arange(n_sample)]), jnp.int32)
    flat_e = by_id(expert)
    ids = jnp.arange(a, dtype=jnp.int32)
    _, order, gate_sorted, tok_sorted = lax.sort((flat_e, ids, by_id(gate), tok_of_id), num_keys=1, is_stable=True)
    _, inv = lax.sort((order, ids), num_keys=1)
    experts = jnp.arange(N_EXPERTS, dtype=jnp.int32)
    counts = jnp.sum((flat_e[:, None] == experts[None, :]).astype(jnp.int32), axis=0)
    hi = jnp.cumsum(counts).astype(jnp.int32)
    lo = hi - counts
    n_tiles = a // tm
    first_tile = lo // tm
    tiles_of = jnp.where(counts > 0, (hi - 1) // tm - first_tile + 1, 0)
    item_end = jnp.cumsum(tiles_of).astype(jnp.int32)
    n_items = item_end[-1:]
    n_work = n_tiles + N_EXPERTS - 1
    w = jnp.minimum(jnp.arange(n_work, dtype=jnp.int32), n_items[0] - 1)
    item_expert = jnp.sum((item_end[None, :] <= w[:, None]).astype(jnp.int32), axis=1)
    onehot = (item_expert[:, None] == experts[None, :]).astype(jnp.int32)
    item_tile = jnp.sum(onehot * (first_tile - (item_end - tiles_of))[None, :], axis=1) + w
    return tok_sorted, gate_sorted, inv, item_tile.astype(jnp.int32), item_expert, n_items, lo, hi


def _combine_kernel(x_ref, r0_ref, r1_ref, g_ref, *outs):
    x2 = x_ref[...] + (r0_ref[...] + r1_ref[...])
    outs[-1][...] = _rms(x2, g_ref[...])
    if len(outs) == 2:
        outs[0][...] = x2


def _combine(x1, gathered, first_row, g, tm, last_layer):
    m = x1.shape[0]
    spec = _row_spec(tm, D_MODEL)
    n_out = 1 if last_layer else 2
    outs = pl.pallas_call(
        _combine_kernel, grid=(m // tm,),
        in_specs=[spec, _row_spec(tm, D_MODEL, first_row // tm), _row_spec(tm, D_MODEL, (first_row + m) // tm),
                  _full_spec((1, D_MODEL))],
        out_specs=[spec] * n_out,
        out_shape=[jax.ShapeDtypeStruct((m, D_MODEL), F32)] * n_out,
        compiler_params=_cparams("parallel"), name="moe_combine")(x1, gathered, gathered, g.reshape(1, D_MODEL))
    return (None, outs[0]) if last_layer else tuple(outs)


def _to_scan_layout(t2d, b, t):
    return t2d.reshape(b, t, RWKV_HEADS, RWKV_N).transpose(1, 3, 2, 0).reshape(t, RWKV_N, RWKV_HEADS * b)


def _from_scan_layout(y, b, t):
    return y.reshape(t, RWKV_N, RWKV_HEADS, b).transpose(3, 0, 2, 1).reshape(b * t, TOK_WIDTH)


def _scan_param(p, b):
    return jnp.repeat(p.reshape(RWKV_HEADS, RWKV_N).T, b, axis=1)


def _seq_scan_param(p, b):
    return jnp.pad(_scan_param(p, b), ((0, 0), (0, LANES - RWKV_HEADS * b)))


def kernel(x_prompt, x_sample, mem_prompt, state_rwkv_S, state_rwkv_shift, state_gla_S, cache_mem_k, cache_mem_v, norm_mix_g, norm_ffn_g, norm_mem_g, norm_final_g, w_in, w_out, w_mem_kv, rw_mu, rw_w0, rw_w1, rw_w2, rw_a0, rw_a1, rw_a2, rw_g1, rw_g2, rw_k_k, rw_k_a, rw_r_k, rw_ln_g, rw_ln_b, gla_a1, gla_a2, gla_ab, gla_norm_g, router_wg, router_bg, router_we, router_be, exp_w_up, exp_w_down):
    bp, tp, _ = x_prompt.shape
    bs, ts, _ = x_sample.shape
    assert ts == 1 and tp % GLA_CHUNK == 0 and tp % SCAN_TIME_BLOCK == 0
    np_ = bp * tp
    ns = bs * ts
    m = np_ + ns
    assert np_ % LIGHT_BLOCK == 0 and ns % SAMPLE_BLOCK == 0 and (2 * m) % MOE_BLOCK == 0
    depth = w_in.shape[0]
    nh = GLA_HEADS
    tw = TOK_WIDTH
    bf = lambda t_: t_.astype(BF16)

    mem2d = mem_prompt.reshape(bp * N_MEM, D_MODEL)
    mem_kv = [_norm_matmul(mem2d, norm_mem_g[i], w_mem_kv[i], 512) for i in range(depth)]
    pk = jnp.stack([kv[:, :MEM_WIDTH].reshape(bp, N_MEM, MEM_WIDTH) for kv in mem_kv])
    pv = jnp.stack([kv[:, MEM_WIDTH:].reshape(bp, N_MEM, MEM_WIDTH) for kv in mem_kv])
    prompt_mem_k = pk.reshape(depth, bp, N_MEM, MEM_HEADS, MEM_HEAD_DIM)
    prompt_mem_v = pv.reshape(depth, bp, N_MEM, MEM_HEADS, MEM_HEAD_DIM)
    sk = cache_mem_k.reshape(depth, bs, N_MEM, MEM_WIDTH).transpose(0, 1, 3, 2)
    sv = cache_mem_v.reshape(depth, bs, N_MEM, MEM_WIDTH).transpose(0, 1, 3, 2)

    x_p = x_prompt.reshape(np_, D_MODEL)
    x_s = x_sample.reshape(ns, D_MODEL)
    h_p = None
    h_s = _norm(x_s, norm_mix_g[0], SAMPLE_BLOCK)

    w_router = jnp.zeros((depth, D_MODEL, ROUTER_LANES), F32)
    w_router = w_router.at[:, :, :N_GROUPS].set(router_wg).at[:, :, N_GROUPS:N_GROUPS + N_EXPERTS].set(router_we)
    b_router = jnp.zeros((depth, 1, ROUTER_LANES), F32)
    b_router = b_router.at[:, 0, :N_GROUPS].set(router_bg).at[:, 0, N_GROUPS:N_GROUPS + N_EXPERTS].set(router_be)

    p_rw_S, p_rw_shift, p_gla_S, s_rw_S, s_rw_shift, s_gla_S = [], [], [], [], [], []
    pending_p = pending_s = None
    for i in range(depth):
        j = i // 2
        if i % 2 == 0:
            wi = w_in[i]
            ws = [rw_mu[j], bf(wi[:, :tw]), bf(wi[:, tw:2 * tw]), bf(wi[:, 2 * tw:3 * tw]), bf(wi[:, 3 * tw:]),
                  bf(rw_w1[j]), bf(rw_w2[j]), rw_w0[j].reshape(1, tw), bf(rw_a1[j]), bf(rw_a2[j]),
                  rw_a0[j].reshape(1, tw), bf(rw_g1[j]), bf(rw_g2[j])]
            if h_p is None:
                *rkvda_p, gate_p, qm_p = _rwkv_proj(x_p, None, ws, PROJ_BLOCK, tp, norm_g=norm_mix_g[i])
                shift_p = _norm(x_prompt[:, -1, :], norm_mix_g[i], bp)
            else:
                *rkvda_p, gate_p, qm_p = _rwkv_proj(h_p, None, ws, PROJ_BLOCK, tp)
                shift_p = h_p.reshape(bp, tp, D_MODEL)[:, -1]
            *rkvda_s, gate_s, qm_s = _rwkv_proj(h_s, state_rwkv_shift[j], ws, SAMPLE_BLOCK, ts)
            pvec = [rw_k_k[j], rw_k_a[j], rw_r_k[j], rw_ln_g[j], rw_ln_b[j]]
            tok_p, sp = _rwkv_prompt_mixer(*rkvda_p, *[_seq_scan_param(p, bp) for p in pvec], bp, tp, SCAN_TIME_BLOCK)
            assert bs == LANES
            ys, ss = _rwkv_scan(*[_to_scan_layout(t_, bs, ts) for t_ in rkvda_s], [_scan_param(p, bs) for p in pvec],
                                state_rwkv_S[j].transpose(1, 2, 3, 0), 1)
            tok_s = _from_scan_layout(ys, bs, ts)
            p_rw_S.append(sp[:, :, :RWKV_HEADS * bp].reshape(RWKV_N, RWKV_N, RWKV_HEADS, bp).transpose(3, 2, 0, 1))
            s_rw_S.append(ss.transpose(3, 0, 1, 2))
            p_rw_shift.append(shift_p)
            s_rw_shift.append(h_s)
            layout_p, layout_s = "rwkv_pairs", "rows"
        else:
            ws = _gla_weights(w_in[i], gla_a1[j], gla_a2[j], gla_ab[j])
            if pending_p is None:
                q_p, k_p, v_p, gate_p, la_p, qm_p = _gla_proj(h_p, ws, PROJ_BLOCK)
                q_s, k_s, v_s, gate_s, la_s, qm_s = _gla_proj(h_s, ws, SAMPLE_BLOCK)
            else:
                q_p, k_p, v_p, gate_p, la_p, qm_p, x_p = _gla_proj(None, ws, PROJ_BLOCK, pending_p)
                q_s, k_s, v_s, gate_s, la_s, qm_s, x_s = _gla_proj(None, ws, SAMPLE_BLOCK, pending_s)
            tok_p, sp_t = _gla_chunk_scan(q_p, k_p, v_p, la_p, gla_norm_g[j], bp, tp)
            lanes_of = lambda t_: t_.transpose(0, 2, 1)
            os_, ss = _gla_step(lanes_of(q_s), lanes_of(k_s), lanes_of(v_s), lanes_of(la_s), gla_norm_g[j],
                                state_gla_S.transpose(0, 2, 3, 4, 1), j)
            tok_s = os_.transpose(0, 2, 1)
            p_gla_S.append(sp_t.transpose(0, 1, 3, 2))
            s_gla_S.append(ss.transpose(3, 0, 1, 2))
            layout_p = layout_s = "gla_heads"

        att_p = _mem_attn(qm_p.reshape(bp, tp, MEM_WIDTH), pk, pv, i, 512).reshape(np_, MEM_WIDTH)
        att_s = _mem_attn(qm_s.reshape(bs, ts, MEM_WIDTH), sk, sv, i, 1, memory_transposed=True).reshape(ns, MEM_WIDTH)
        wr_hi = bf(w_router[i])
        wr_lo = bf(w_router[i] - wr_hi.astype(F32))
        ws = [bf(w_out[i]), norm_ffn_g[i].reshape(1, D_MODEL), wr_hi, wr_lo, b_router[i]]
        x1_p, h2, logits_p = _out_proj(tok_p, gate_p, att_p, x_p, ws, layout_p, PROJ_BLOCK, m, 0)
        x1_s, h2, logits_s = _out_proj(tok_s, gate_s, att_s, x_s, ws, layout_s, SAMPLE_BLOCK, m, np_, h2_buffer=h2)
        tok_sorted, gate_sorted, inv, item_tile, item_expert, n_items, lo, hi = _route(
            jnp.concatenate([logits_p, logits_s], axis=0), np_, MOE_BLOCK)
        rows = _moe_ffn(h2[tok_sorted], gate_sorted, item_tile, item_expert, n_items, lo, hi,
                        exp_w_up, exp_w_down, i)
        gathered = rows[inv]
        g_next = norm_mix_g[i + 1] if i + 1 < depth else norm_final_g
        if i + 1 < depth and (i + 1) % 2 == 1:
            pending_p = (x1_p, gathered, 0, g_next)
            pending_s = (x1_s, gathered, 2 * np_, g_next)
        else:
            pending_p = pending_s = None
            x_p, h_p = _combine(x1_p, gathered, 0, g_next, LIGHT_BLOCK, i + 1 == depth)
            x_s, h_s = _combine(x1_s, gathered, 2 * np_, g_next, SAMPLE_BLOCK, i + 1 == depth)

    y_prompt = h_p.reshape(bp, tp, D_MODEL)
    y_sample = h_s.reshape(bs, ts, D_MODEL)
    return (y_prompt, y_sample, jnp.stack(p_rw_S), jnp.stack(p_rw_shift), jnp.stack(p_gla_S),
            prompt_mem_k, prompt_mem_v, jnp.stack(s_rw_S), jnp.stack(s_rw_shift), jnp.stack(s_gla_S))
```

```python
import functools

import numpy as np
import jax
import jax.numpy as jnp
from jax import lax
from jax.experimental import pallas as pl
from jax.experimental.pallas import tpu as pltpu

F32 = jnp.float32
BF16 = jnp.bfloat16

D_MODEL = 1024
TOK_WIDTH = 768
MEM_WIDTH = 256
MEM_HEADS = 4
MEM_HEAD_DIM = 64
N_MEM = 256
RWKV_HEADS = 12
RWKV_N = 64
RWKV_PAIRS = RWKV_HEADS // 2
RWKV_GN_EPS = 64e-5
GLA_HEADS = 4
GLA_KW = 384
GLA_DK = 96
GLA_DV = 192
GLA_TAU = 16.0
GLA_CHUNK = 64
GLA_TILE = 8
N_GROUPS = 4
EXPERTS_PER_GROUP = 8
N_EXPERTS = 32
EXPERT_FF = 512
NORM_EPS = 1e-6
ROUTER_LANES = 128
LANES = 128
SUBLANES = 8

PROJ_BLOCK = 512
LIGHT_BLOCK = 512
SAMPLE_BLOCK = 128
MOE_BLOCK = 256
SCAN_TIME_BLOCK = 32
VMEM_LIMIT = 56 * 1024 * 1024


def _cparams(*sem):
    return pltpu.CompilerParams(dimension_semantics=sem, vmem_limit_bytes=VMEM_LIMIT)


def _dot(a, b):
    return jnp.dot(a, b, preferred_element_type=F32)


def _dot_nt(a, b):
    return lax.dot_general(a, b, (((1,), (1,)), ((), ())), preferred_element_type=F32)


def _dot_tn(a, b):
    return lax.dot_general(a, b, (((0,), (0,)), ((), ())), preferred_element_type=F32)


def _rms(x, g):
    return x * lax.rsqrt(jnp.mean(x * x, axis=-1, keepdims=True) + NORM_EPS) * g


def _sigmoid(x):
    return 1.0 / (1.0 + jnp.exp(-x))


def _row_spec(tm, n, offset=0):
    return pl.BlockSpec((tm, n), lambda i: (i + offset, 0))


def _head_row_spec(tm, n):
    return pl.BlockSpec((GLA_HEADS, tm, n), lambda i: (0, i, 0))


def _full_spec(shape):
    nd = len(shape)
    return pl.BlockSpec(shape, lambda *_: (0,) * nd)


def _norm_kernel(x_ref, g_ref, o_ref):
    o_ref[...] = _rms(x_ref[...], g_ref[...])


def _norm(x, g, tm):
    m, d = x.shape
    return pl.pallas_call(
        _norm_kernel, grid=(m // tm,),
        in_specs=[_row_spec(tm, d), _full_spec((1, d))],
        out_specs=_row_spec(tm, d),
        out_shape=jax.ShapeDtypeStruct((m, d), F32),
        compiler_params=_cparams("parallel"), name="rms_norm")(x, g.reshape(1, d))


def _norm_matmul_kernel(x_ref, g_ref, w_ref, o_ref):
    o_ref[...] = _dot(_rms(x_ref[...], g_ref[...]).astype(BF16), w_ref[...])


def _norm_matmul(x, g, w, tm):
    m, d = x.shape
    n = w.shape[1]
    return pl.pallas_call(
        _norm_matmul_kernel, grid=(m // tm,),
        in_specs=[_row_spec(tm, d), _full_spec((1, d)), _full_spec((d, n))],
        out_specs=_row_spec(tm, n),
        out_shape=jax.ShapeDtypeStruct((m, n), F32),
        compiler_params=_cparams("parallel"), name="norm_matmul")(x, g.reshape(1, d), w.astype(BF16))


def _rwkv_proj_kernel(h_ref, hp_ref, *refs, blocks_per_seq, norm_input):
    if norm_input:
        ng_ref, refs = refs[0], refs[1:]
        normed = lambda t_: _rms(t_, ng_ref[...])
    else:
        normed = lambda t_: t_
    (mu_ref, wr_ref, wk_ref, wv_ref, wq_ref, w1_ref, w2_ref, w0_ref, a1_ref, a2_ref, a0_ref, g1_ref, g2_ref,
     r_out, k_out, v_out, d_out, a_out, g_out, q_out) = refs

    def put(out, val):
        if blocks_per_seq:
            for p in range(RWKV_PAIRS):
                out[p] = val[:, p * LANES:(p + 1) * LANES]
        else:
            out[...] = val

    h = normed(h_ref[...])
    if blocks_per_seq:
        seq_start = (pl.program_id(0) % blocks_per_seq) == 0
        before = jnp.where(seq_start, 0.0, normed(hp_ref[SUBLANES - 1:SUBLANES, :]))
        row = lax.broadcasted_iota(jnp.int32, (h.shape[0], 1), 0)
        hp = jnp.where(row == 0, before, pltpu.roll(h, 1, axis=0))
    else:
        hp = hp_ref[...]
    xx = hp - h

    def mix(j):
        return (h + xx * mu_ref[j:j + 1, :]).astype(BF16)

    put(r_out, _dot(mix(0), wr_ref[...]))
    wl = w0_ref[...] + _dot(jnp.tanh(_dot(mix(1), w1_ref[...])).astype(BF16), w2_ref[...])
    put(k_out, _dot(mix(2), wk_ref[...]))
    put(v_out, _dot(mix(3), wv_ref[...]))
    al = a0_ref[...] + _dot(_dot(mix(4), a1_ref[...]).astype(BF16), a2_ref[...])
    g_out[...] = _dot(_sigmoid(_dot(mix(5), g1_ref[...])).astype(BF16), g2_ref[...])
    q_out[...] = _dot(h.astype(BF16), wq_ref[...])
    z = -wl
    softplus = jnp.maximum(z, 0.0) + jnp.log(1.0 + jnp.exp(-jnp.abs(z)))
    put(d_out, jnp.exp(-jnp.exp(-softplus - 0.5)))
    put(a_out, _sigmoid(al))


def _rwkv_proj(h, h_prev, ws, tm, seq_len, norm_g=None):
    m = h.shape[0]
    if norm_g is not None:
        assert h_prev is None
        ws = [norm_g.reshape(1, D_MODEL)] + list(ws)
    tw = TOK_WIDTH
    if h_prev is None:
        assert seq_len % tm == 0 and tm % SUBLANES == 0
        per8 = tm // SUBLANES
        hp_spec = pl.BlockSpec((SUBLANES, D_MODEL), lambda i: (jnp.maximum(i * per8 - 1, 0), 0))
        h_prev, blocks_per_seq = h, seq_len // tm
    else:
        hp_spec, blocks_per_seq = _row_spec(tm, D_MODEL), 0
    if blocks_per_seq:
        scan_spec = pl.BlockSpec((RWKV_PAIRS, tm, LANES), lambda i: (0, i, 0))
        scan_shape = jax.ShapeDtypeStruct((RWKV_PAIRS, m, LANES), F32)
    else:
        scan_spec, scan_shape = _row_spec(tm, tw), jax.ShapeDtypeStruct((m, tw), F32)
    return pl.pallas_call(
        functools.partial(_rwkv_proj_kernel, blocks_per_seq=blocks_per_seq, norm_input=norm_g is not None),
        grid=(m // tm,),
        in_specs=[_row_spec(tm, D_MODEL), hp_spec] + [_full_spec(w.shape) for w in ws],
        out_specs=[scan_spec] * 5 + [_row_spec(tm, tw), _row_spec(tm, MEM_WIDTH)],
        out_shape=[scan_shape] * 5 + [jax.ShapeDtypeStruct((m, tw), F32), jax.ShapeDtypeStruct((m, MEM_WIDTH), F32)],
        compiler_params=_cparams("parallel"), name="rwkv_proj")(h, h_prev, *ws)


def _pad_lanes(x):
    short = LANES - x.shape[-1]
    if short == 0:
        return x
    return jnp.concatenate([x, jnp.zeros(x.shape[:-1] + (short,), x.dtype)], axis=-1)


def _rwkv_scan_kernel(r_ref, k_ref, v_ref, d_ref, a_ref, kkp_ref, kap_ref, rkp_ref, lng_ref, lnb_ref, s0_ref,
                      y_ref, sfin_ref, s_scr, v_scr, yrow_scr, *, tc):
    n = RWKV_N
    nl = r_ref.shape[-1]

    @pl.when(pl.program_id(1) == 0)
    def _():
        s_scr[...] = _pad_lanes(s0_ref[...])

    kkp = _pad_lanes(kkp_ref[...])
    kap = _pad_lanes(kap_ref[...])
    rkp = _pad_lanes(rkp_ref[...])
    lng = _pad_lanes(lng_ref[...])
    lnb = _pad_lanes(lnb_ref[...])

    def step(t, carry):
        r_t = _pad_lanes(r_ref[t])
        k_t = _pad_lanes(k_ref[t])
        v_t = _pad_lanes(v_ref[t])
        d_t = _pad_lanes(d_ref[t])
        a_t = _pad_lanes(a_ref[t])
        v_scr[...] = v_t
        kkr = k_t * kkp
        nrm = jnp.maximum(jnp.sqrt(jnp.sum(kkr * kkr, axis=0, keepdims=True)), 1e-12)
        kk = kkr * (1.0 / nrm)
        k2 = k_t * (1.0 + (a_t - 1.0) * kap)
        nkk = -kk
        b_t = kk * a_t

        def ibody(i, c):
            s_i = s_scr[i]
            sa = jnp.sum(s_i * nkk, axis=0, keepdims=True)
            v_i = v_scr[pl.ds(i, 1), :]
            s_n = s_i * d_t + sa * b_t + v_i * k2
            s_scr[i] = s_n
            yrow_scr[pl.ds(i, 1), :] = jnp.sum(s_n * r_t, axis=0, keepdims=True)
            return c

        lax.fori_loop(0, n, ibody, 0, unroll=8)
        y = yrow_scr[...]
        yc = y - jnp.mean(y, axis=0, keepdims=True)
        var = jnp.mean(yc * yc, axis=0, keepdims=True)
        gn = yc * lax.rsqrt(var + RWKV_GN_EPS) * lng + lnb
        bonus = jnp.sum(r_t * k2 * rkp, axis=0, keepdims=True) * v_t
        y_ref[t] = (gn + bonus)[:, :nl]
        return carry

    lax.fori_loop(0, tc, step, 0)

    @pl.when(pl.program_id(1) == pl.num_programs(1) - 1)
    def _():
        sfin_ref[...] = s_scr[:, :, :nl]


def _rwkv_scan(r, k, v, d, a, params, s0, tc):
    t, n, l = r.shape
    groups, lane_block = s0.shape[0], s0.shape[3]
    assert groups * lane_block == l
    seq = pl.BlockSpec((tc, n, lane_block), lambda li, ti: (ti, 0, li))
    par = pl.BlockSpec((n, lane_block), lambda li, ti: (0, li))
    st = pl.BlockSpec((None, n, n, lane_block), lambda li, ti: (li, 0, 0, 0))
    return pl.pallas_call(
        functools.partial(_rwkv_scan_kernel, tc=tc), grid=(groups, t // tc),
        in_specs=[seq] * 5 + [par] * 5 + [st],
        out_specs=[seq, st],
        out_shape=[jax.ShapeDtypeStruct((t, n, l), F32), jax.ShapeDtypeStruct((groups, n, n, lane_block), F32)],
        scratch_shapes=[pltpu.VMEM((n, n, LANES), F32), pltpu.VMEM((n, LANES), F32), pltpu.VMEM((n, LANES), F32)],
        compiler_params=_cparams("parallel", "arbitrary"), name="rwkv_scan")(r, k, v, d, a, *params, s0)


PREP_TILES = 6
DECAY_WINDOW = 16


def _rwkv_prep_kernel(r_ref, k_ref, v_ref, d_ref, a_ref, kkp_ref, kap_ref, o_ref, *, tc, nb):
    n = RWKV_N
    rows_per_pair = nb * tc
    ins = [ref.reshape(RWKV_PAIRS * rows_per_pair, LANES) for ref in (r_ref, k_ref, v_ref, d_ref, a_ref)]
    zero_rows = jnp.zeros((LANES - RWKV_HEADS * nb, LANES), F32)
    first_half = lax.broadcasted_iota(jnp.int32, (nb, LANES), 1) < n
    kkp = kkp_ref[...]
    kap = kap_ref[...]

    def load_transposed_pair(x2, t):
        pieces = []
        for p in range(RWKV_PAIRS):
            now = x2[pl.ds(p * rows_per_pair + t, nb, stride=tc), :]
            nxt = x2[pl.ds(p * rows_per_pair + t + 1, nb, stride=tc), :]
            pieces += [jnp.where(first_half, now, pltpu.roll(nxt, n, axis=1)),
                       jnp.where(first_half, pltpu.roll(now, n, axis=1), nxt)]
        both = jnp.concatenate(pieces + [zero_rows], axis=0).T
        return both[:n, :], both[n:, :]

    def emit(t, c_prev, r_t, k_t, v_t, d_t, a_t):
        kkr = k_t * kkp
        nrm = jnp.maximum(jnp.sqrt(jnp.sum(kkr * kkr, axis=0, keepdims=True)), 1e-12)
        kk = kkr * (1.0 / nrm)
        k2 = k_t * (1.0 + (a_t - 1.0) * kap)
        c_prev = jnp.where(t % DECAY_WINDOW == 0, 1.0, c_prev)
        c_t = c_prev * d_t
        inv = 1.0 / jnp.maximum(c_t, 1e-30)
        o_ref[t, 0] = -kk * c_prev
        o_ref[t, 1] = c_t
        o_ref[t, 2] = kk * a_t * inv
        o_ref[t, 3] = k2 * inv
        o_ref[t, 4] = r_t * c_t
        o_ref[t, 5] = v_t
        return c_t

    def two_steps(u, c):
        t = 2 * u
        tiles = [load_transposed_pair(x2, t) for x2 in ins]
        c = emit(t, c, *[tile[0] for tile in tiles])
        return emit(t + 1, c, *[tile[1] for tile in tiles])

    lax.fori_loop(0, tc // 2, two_steps, jnp.ones((n, LANES), F32), unroll=4)


def _rwkv_state_scan_kernel(x_ref, rkp_ref, lng_ref, lnb_ref, y_ref, sfin_ref, s_scr, sa_scr, yrow_scr, *, tc):
    n = RWKV_N
    groups = n // SUBLANES

    @pl.when(pl.program_id(0) == 0)
    def _():
        s_scr[...] = jnp.zeros(s_scr.shape, F32)

    rkp = rkp_ref[...]
    lng = lng_ref[...]
    lnb = lnb_ref[...]
    sub = lax.broadcasted_iota(jnp.int32, (SUBLANES, LANES), 0)
    low4 = sub < 4
    low2 = (sub & 3) < 2
    low1 = (sub & 1) == 0

    def fold(x, y, dist, low):
        if dist == 4:
            return jnp.where(low, x, y) + pltpu.roll(jnp.where(low, y, x), 4, axis=0)
        return (jnp.where(low, x, pltpu.roll(y, dist, axis=0))
                + jnp.where(low, pltpu.roll(x, SUBLANES - dist, axis=0), y))

    def sublane_sums(ps):
        z = [fold(ps[0], ps[4], 4, low4), fold(ps[2], ps[6], 4, low4),
             fold(ps[1], ps[5], 4, low4), fold(ps[3], ps[7], 4, low4)]
        return fold(fold(z[0], z[1], 2, low2), fold(z[2], z[3], 2, low2), 1, low1)

    def tile_sum(x):
        acc = x[0:SUBLANES]
        for u in range(1, groups):
            acc = acc + x[u * SUBLANES:(u + 1) * SUBLANES]
        return acc

    def step(t, carry):
        nkk = x_ref[t, 0]
        b_t = x_ref[t, 2]
        k2 = x_ref[t, 3]
        r_t = x_ref[t, 4]
        for g in range(groups):
            sa_scr[g * SUBLANES:(g + 1) * SUBLANES, :] = sublane_sums(
                [tile_sum(s_scr[g * SUBLANES + u] * nkk) for u in range(SUBLANES)])
        for g in range(groups):
            ps = []
            for u in range(SUBLANES):
                i = g * SUBLANES + u
                s_n = s_scr[i] + sa_scr[pl.ds(i, 1), :] * b_t + x_ref[t, 5, pl.ds(i, 1), :] * k2
                s_scr[i] = s_n
                ps.append(tile_sum(s_n * r_t))
            yrow_scr[g * SUBLANES:(g + 1) * SUBLANES, :] = sublane_sums(ps)
        y = yrow_scr[...]
        yc = y - jnp.mean(y, axis=0, keepdims=True)
        var = jnp.mean(yc * yc, axis=0, keepdims=True)
        bonus = jnp.sum(r_t * k2 * rkp, axis=0, keepdims=True) * x_ref[t, 5]
        y_ref[t] = yc * lax.rsqrt(var + RWKV_GN_EPS) * lng + lnb + bonus

        @pl.when(t % DECAY_WINDOW == DECAY_WINDOW - 1)
        def _():
            c_t = x_ref[t, 1]
            for i in range(n):
                s_scr[i] = s_scr[i] * c_t

        return carry

    lax.fori_loop(0, tc, step, 0)

    @pl.when(pl.program_id(0) == pl.num_programs(0) - 1)
    def _():
        sfin_ref[...] = s_scr[...]


def _rwkv_unprep_kernel(y_ref, o_ref, *, tc, nb):
    n = RWKV_N
    rows_per_pair = nb * tc
    o2 = o_ref.reshape(RWKV_PAIRS * rows_per_pair, LANES)
    first_half = lax.broadcasted_iota(jnp.int32, (nb, LANES), 1) < n

    def two_steps(u, carry):
        t = 2 * u
        w = jnp.concatenate([y_ref[t], y_ref[t + 1]], axis=0).T
        for p in range(RWKV_PAIRS):
            even = w[(2 * p) * nb:(2 * p + 1) * nb, :]
            odd = w[(2 * p + 1) * nb:(2 * p + 2) * nb, :]
            o2[pl.ds(p * rows_per_pair + t, nb, stride=tc), :] = jnp.where(
                first_half, even, pltpu.roll(odd, n, axis=1))
            o2[pl.ds(p * rows_per_pair + t + 1, nb, stride=tc), :] = jnp.where(
                first_half, pltpu.roll(even, n, axis=1), odd)
        return carry

    lax.fori_loop(0, tc // 2, two_steps, 0, unroll=2)


def _rwkv_prompt_mixer(r, k, v, d, a, kkp, kap, rkp, lng, lnb, nb, t, tc):
    n = RWKV_N
    assert tc % SUBLANES == 0 and tc % 4 == 0 and t % tc == 0 and RWKV_HEADS * nb <= LANES
    assert tc % DECAY_WINDOW == 0
    grid = (t // tc,)
    tok = pl.BlockSpec((RWKV_PAIRS, nb, tc, LANES), lambda ti: (0, 0, ti, 0))
    par = pl.BlockSpec((n, LANES), lambda ti: (0, 0))
    tiles = pl.BlockSpec((tc, PREP_TILES, n, LANES), lambda ti: (ti, 0, 0, 0))
    ytile = pl.BlockSpec((tc, n, LANES), lambda ti: (ti, 0, 0))
    prepared = pl.pallas_call(
        functools.partial(_rwkv_prep_kernel, tc=tc, nb=nb), grid=grid,
        in_specs=[tok] * 5 + [par] * 2, out_specs=tiles,
        out_shape=jax.ShapeDtypeStruct((t, PREP_TILES, n, LANES), F32),
        compiler_params=_cparams("parallel"), name="rwkv_prep")(
            *[x.reshape(RWKV_PAIRS, nb, t, LANES) for x in (r, k, v, d, a)], kkp, kap)
    y, s = pl.pallas_call(
        functools.partial(_rwkv_state_scan_kernel, tc=tc), grid=grid,
        in_specs=[tiles, par, par, par],
        out_specs=[ytile, pl.BlockSpec((n, n, LANES), lambda ti: (0, 0, 0))],
        out_shape=[jax.ShapeDtypeStruct((t, n, LANES), F32), jax.ShapeDtypeStruct((n, n, LANES), F32)],
        scratch_shapes=[pltpu.VMEM((n, n, LANES), F32), pltpu.VMEM((n, LANES), F32), pltpu.VMEM((n, LANES), F32)],
        compiler_params=_cparams("arbitrary"), name="rwkv_state_scan")(prepared, rkp, lng, lnb)
    tok_out = pl.pallas_call(
        functools.partial(_rwkv_unprep_kernel, tc=tc, nb=nb), grid=grid,
        in_specs=[ytile], out_specs=tok,
        out_shape=jax.ShapeDtypeStruct((RWKV_PAIRS, nb, t, LANES), F32),
        compiler_params=_cparams("parallel"), name="rwkv_unprep")(y)
    return tok_out.reshape(RWKV_PAIRS, nb * t, LANES), s


def _gla_proj_kernel(*refs, fused_combine):
    if fused_combine:
        x_ref, r0_ref, r1_ref, g_ref = refs[:4]
        refs, x_out = refs[4:-1], refs[-1]
        x2 = x_ref[...] + (r0_ref[...] + r1_ref[...])
        x_out[...] = x2
        h = _rms(x2, g_ref[...])
    else:
        h, refs = refs[0][...], refs[1:]
    (wq_ref, wk_ref, wv_ref, wr_ref, wm_ref, a1_ref, a2_ref, ab_ref,
     q_out, k_out, v_out, r_out, la_out, qm_out) = refs
    hb = h.astype(BF16)
    low = _dot(hb, a1_ref[...]).astype(BF16)
    for hd in range(GLA_HEADS):
        q_out[hd] = _dot(hb, wq_ref[hd])
        k_out[hd] = _dot(hb, wk_ref[hd])
        v_out[hd] = _dot(hb, wv_ref[hd])
        r_out[hd] = _dot(hb, wr_ref[hd])
        x = _dot(low, a2_ref[hd]) + ab_ref[hd]
        log_sigmoid = jnp.minimum(x, 0.0) - jnp.log(1.0 + jnp.exp(-jnp.abs(x)))
        la_out[hd] = log_sigmoid / GLA_TAU
    qm_out[...] = _dot(hb, wm_ref[...])


def _gla_weights(w_in, a1, a2, ab):
    kw, tw, nh = GLA_KW, TOK_WIDTH, GLA_HEADS

    def heads(w, d):
        return w.reshape(w.shape[0], nh, d).transpose(1, 0, 2)

    return [heads(w_in[:, :kw], GLA_DK).astype(BF16), heads(w_in[:, kw:2 * kw], GLA_DK).astype(BF16),
            heads(w_in[:, 2 * kw:2 * kw + tw], GLA_DV).astype(BF16),
            heads(w_in[:, 2 * kw + tw:3 * tw], GLA_DV).astype(BF16), w_in[:, 3 * tw:].astype(BF16),
            a1.astype(BF16), heads(a2, GLA_DK).astype(BF16), ab.reshape(nh, 1, GLA_DK)]


def _gla_proj(h, ws, tm, pending=None):
    nh = GLA_HEADS
    widths = [GLA_DK, GLA_DK, GLA_DV, GLA_DV, GLA_DK]
    if pending is None:
        m = h.shape[0]
        lead_specs, lead_args = [_row_spec(tm, D_MODEL)], [h]
    else:
        x1, gathered, first_row, norm_g = pending
        m = x1.shape[0]
        lead_specs = [_row_spec(tm, D_MODEL), _row_spec(tm, D_MODEL, first_row // tm),
                      _row_spec(tm, D_MODEL, (first_row + m) // tm), _full_spec((1, D_MODEL))]
        lead_args = [x1, gathered, gathered, norm_g.reshape(1, D_MODEL)]
    out_specs = [_head_row_spec(tm, w) for w in widths] + [_row_spec(tm, MEM_WIDTH)]
    out_shape = [jax.ShapeDtypeStruct((nh, m, w), F32) for w in widths] + [jax.ShapeDtypeStruct((m, MEM_WIDTH), F32)]
    if pending is not None:
        out_specs.append(_row_spec(tm, D_MODEL))
        out_shape.append(jax.ShapeDtypeStruct((m, D_MODEL), F32))
    return pl.pallas_call(
        functools.partial(_gla_proj_kernel, fused_combine=pending is not None), grid=(m // tm,),
        in_specs=lead_specs + [_full_spec(w.shape) for w in ws],
        out_specs=out_specs, out_shape=out_shape,
        compiler_params=_cparams("parallel"), name="gla_proj")(*lead_args, *ws)


def _gla_out_norm(o, g):
    return o * lax.rsqrt(jnp.mean(o * o, axis=-1, keepdims=True) + NORM_EPS) * g


def _gla_chunk_kernel(q_ref, k_ref, v_ref, la_ref, ng_ref, o_ref, sfin_ref, st_scr):
    c, dk = GLA_CHUNK, GLA_DK

    @pl.when(pl.program_id(1) == 0)
    def _():
        st_scr[...] = jnp.zeros(st_scr.shape, F32)

    row = lax.broadcasted_iota(jnp.int32, (c, c), 0)
    col = lax.broadcasted_iota(jnp.int32, (c, c), 1)
    tril = (row >= col).astype(F32)
    rr = lax.broadcasted_iota(jnp.int32, (c, dk), 0)
    ones_sum = jnp.ones((dk, LANES), BF16)

    heads = range(GLA_HEADS)
    tril_b = tril.astype(BF16)

    def cumsum_rows(la):
        hi = la.astype(BF16)
        r1 = la - hi.astype(F32)
        mid = r1.astype(BF16)
        lo = (r1 - mid.astype(F32)).astype(BF16)
        return _dot(tril_b, hi) + _dot(tril_b, mid) + _dot(tril_b, lo)

    def tile_roll(x, dlt):
        return pltpu.roll(x.reshape(c // GLA_TILE, GLA_TILE, dk), dlt, axis=1).reshape(c, dk)

    k = [k_ref[hd] for hd in heads]
    vb = [v_ref[hd].astype(BF16) for hd in heads]
    q = [q_ref[hd] * (dk ** -0.5) for hd in heads]
    b = [cumsum_rows(la_ref[hd]) for hd in heads]
    st = [st_scr[hd] for hd in heads]
    inter = [_dot_nt((q[hd] * jnp.exp(b[hd])).astype(BF16), st[hd].astype(BF16)) for hd in heads]

    att = [jnp.zeros((c, c), F32) for hd in heads]
    blk = c // 2
    while blk >= GLA_TILE:
        two = 2 * blk
        upper = (rr & (two - 1)) >= blk
        same_block = (row ^ col) < two
        parts = []
        for hd in heads:
            b_ref_rows = jnp.concatenate(
                [jnp.broadcast_to(b[hd][s0 + blk - 1:s0 + blk, :], (two, dk)) for s0 in range(0, c, two)], axis=0)
            q_l = jnp.where(upper, q[hd] * jnp.exp(jnp.minimum(b[hd] - b_ref_rows, 0.0)), 0.0).astype(BF16)
            k_l = jnp.where(upper, 0.0, k[hd] * jnp.exp(jnp.minimum(b_ref_rows - b[hd], 0.0))).astype(BF16)
            parts.append(_dot_nt(q_l, k_l))
        att = [att[hd] + jnp.where(same_block, parts[hd], 0.0) for hd in heads]
        blk //= 2

    sums = []
    for hd in heads:
        prods = [(q[hd] * k[hd]).astype(BF16)]
        for dlt in range(1, GLA_TILE):
            p = q[hd] * tile_roll(k[hd], dlt) * jnp.exp(jnp.minimum(b[hd] - tile_roll(b[hd], dlt), 0.0))
            prods.append(jnp.where((rr & (GLA_TILE - 1)) >= dlt, p, 0.0).astype(BF16))
        sums.append(_dot(jnp.concatenate(prods, axis=0), ones_sum))
    for dlt in range(GLA_TILE):
        on_diag = col == row - dlt
        att = [att[hd] + jnp.where(on_diag, sums[hd][dlt * c:(dlt + 1) * c, :c], 0.0) for hd in heads]

    o = [inter[hd] + _dot(att[hd].astype(BF16), vb[hd]) for hd in heads]
    for hd in heads:
        o_ref[hd] = _gla_out_norm(o[hd], ng_ref[...])
    for hd in heads:
        b_end = b[hd][c - 1:c, :]
        kd = (k[hd] * jnp.exp(b_end - b[hd])).astype(BF16)
        st_scr[hd] = jnp.exp(b_end) * st[hd] + _dot_tn(vb[hd], kd)

    @pl.when(pl.program_id(1) == pl.num_programs(1) - 1)
    def _():
        sfin_ref[0] = st_scr[...]


def _gla_chunk_scan(q, k, v, la, norm_g, batch, t):
    nh, dk, dv, c = GLA_HEADS, GLA_DK, GLA_DV, GLA_CHUNK
    nc = t // c
    kspec = pl.BlockSpec((nh, c, dk), lambda i, j: (0, i * nc + j, 0))
    vspec = pl.BlockSpec((nh, c, dv), lambda i, j: (0, i * nc + j, 0))
    sspec = pl.BlockSpec((1, nh, dv, dk), lambda i, j: (i, 0, 0, 0))
    return pl.pallas_call(
        _gla_chunk_kernel, grid=(batch, nc),
        in_specs=[kspec, kspec, vspec, kspec, pl.BlockSpec((1, dv), lambda i, j: (0, 0))],
        out_specs=[vspec, sspec],
        out_shape=[jax.ShapeDtypeStruct((nh, batch * t, dv), F32), jax.ShapeDtypeStruct((batch, nh, dv, dk), F32)],
        scratch_shapes=[pltpu.VMEM((nh, dv, dk), F32)],
        compiler_params=_cparams("parallel", "arbitrary"), name="gla_chunk")(q, k, v, la, norm_g.reshape(1, dv))


GLA_STEP_KEY_SPLIT = 2


def _gla_step_kernel(q_ref, k_ref, v_ref, la_ref, ng_ref, s0_ref, o_ref, s_ref):
    rows = s0_ref.shape[1]

    @pl.when(pl.program_id(1) == 0)
    def _():
        o_ref[...] = jnp.zeros(o_ref.shape, F32)

    q = q_ref[0] * (GLA_DK ** -0.5)
    decay = jnp.exp(la_ref[0])
    kb = k_ref[0].astype(BF16).astype(F32)
    vb = v_ref[0].astype(BF16).astype(F32)
    acc = o_ref[0]
    for r in range(rows):
        s_new = decay[r:r + 1, :] * s0_ref[0, r] + kb[r:r + 1, :] * vb
        s_ref[0, r] = s_new
        acc = acc + q[r:r + 1, :] * s_new
    o_ref[0] = acc

    @pl.when(pl.program_id(1) == pl.num_programs(1) - 1)
    def _():
        o_ref[0] = acc * lax.rsqrt(jnp.mean(acc * acc, axis=0, keepdims=True) + NORM_EPS) * ng_ref[...]


def _gla_step(q, k, v, la, norm_g, states, layer):
    nh, dk, b = q.shape
    dv = v.shape[1]
    rows = dk // GLA_STEP_KEY_SPLIT
    assert rows * GLA_STEP_KEY_SPLIT == dk and rows % SUBLANES == 0
    kspec = pl.BlockSpec((1, rows, b), lambda h, r: (h, r, 0))
    vspec = pl.BlockSpec((1, dv, b), lambda h, r: (h, 0, 0))
    return pl.pallas_call(
        _gla_step_kernel, grid=(nh, GLA_STEP_KEY_SPLIT),
        in_specs=[kspec, kspec, vspec, kspec, _full_spec((dv, b)),
                  pl.BlockSpec((None, 1, rows, dv, b), lambda h, r: (layer, h, r, 0, 0))],
        out_specs=[vspec, pl.BlockSpec((1, rows, dv, b), lambda h, r: (h, r, 0, 0))],
        out_shape=[jax.ShapeDtypeStruct((nh, dv, b), F32), jax.ShapeDtypeStruct((nh, dk, dv, b), F32)],
        compiler_params=_cparams("parallel", "arbitrary"), name="gla_step")(
            q, k, v, la, jnp.broadcast_to(norm_g.reshape(dv, 1), (dv, b)), states)


def _mem_attn_kernel(q_ref, k_ref, v_ref, o_ref, *, memory_transposed):
    q = q_ref[0]
    k = k_ref[0].astype(BF16)
    ones = jnp.ones((LANES, N_MEM) if memory_transposed else (N_MEM, LANES), BF16)
    v_ones = jnp.concatenate([v_ref[0].astype(BF16), ones], axis=0 if memory_transposed else 1)
    scores, values = (_dot, _dot_nt) if memory_transposed else (_dot_nt, _dot)
    head_of_lane = lax.broadcasted_iota(jnp.int32, (1, MEM_WIDTH), 1) // MEM_HEAD_DIM
    heads = range(MEM_HEADS)
    mine = [head_of_lane == h for h in heads]
    qh = [jnp.where(mine[h], q, 0.0).astype(BF16) for h in heads]
    s = [scores(qh[h], k) * (MEM_HEAD_DIM ** -0.5) for h in heads]
    e = [jnp.exp(s[h] - jnp.max(s[h], axis=-1, keepdims=True)).astype(BF16) for h in heads]
    ev = [values(e[h], v_ones) for h in heads]
    out = jnp.zeros(q.shape, F32)
    for h in heads:
        inv = 1.0 / ev[h][:, MEM_WIDTH:]
        out = out + jnp.where(mine[h], ev[h][:, :MEM_WIDTH] * jnp.concatenate([inv, inv], axis=1), 0.0)
    o_ref[0] = out


def _mem_attn(q, mem_k, mem_v, layer, tq, memory_transposed=False):
    b, t, w = q.shape
    qspec = pl.BlockSpec((1, tq, w), lambda i, j: (i, j, 0))
    mspec = pl.BlockSpec((None, 1, N_MEM, w), lambda i, j: (layer, i, 0, 0))
    return pl.pallas_call(
        functools.partial(_mem_attn_kernel, memory_transposed=memory_transposed), grid=(b, t // tq),
        in_specs=[qspec, mspec, mspec], out_specs=qspec,
        out_shape=jax.ShapeDtypeStruct((b, t, w), F32),
        compiler_params=_cparams("parallel", "parallel"), name="mem_attn")(q, mem_k, mem_v)


def _out_proj_kernel(*refs, layout, aliased):
    tok_ref, gate_ref, att_ref, x_ref, wo_ref, g_ref, wr_hi_ref, wr_lo_ref, br_ref = refs[:9]
    x1_out, h2_out, logit_out = refs[9 + aliased:]
    x1 = x_ref[...] + _dot(att_ref[...].astype(BF16), wo_ref[TOK_WIDTH:, :])
    if layout == "gla_heads":
        for hd in range(GLA_HEADS):
            gate = gate_ref[hd]
            mixed = (tok_ref[hd] * (gate * _sigmoid(gate))).astype(BF16)
            x1 = x1 + _dot(mixed, wo_ref[hd * GLA_DV:(hd + 1) * GLA_DV, :])
    else:
        if layout == "rwkv_pairs":
            tok = jnp.concatenate([tok_ref[p] for p in range(RWKV_PAIRS)], axis=1)
        else:
            tok = tok_ref[...]
        x1 = x1 + _dot((tok * gate_ref[...]).astype(BF16), wo_ref[:TOK_WIDTH, :])
    x1_out[...] = x1
    h2 = _rms(x1, g_ref[...])
    h2_out[...] = h2
    h_hi = h2.astype(BF16)
    h_lo = (h2 - h_hi.astype(F32)).astype(BF16)
    logit_out[...] = (_dot(h_hi, wr_hi_ref[...]) + _dot(h_lo, wr_hi_ref[...]) + _dot(h_hi, wr_lo_ref[...])
                      + br_ref[...])


def _out_proj(tok, gate, att, x, ws, layout, tm, h2_rows, h2_row_offset, h2_buffer=None):
    m = x.shape[0]
    aliased = h2_buffer is not None
    rows_spec = _row_spec(tm, TOK_WIDTH)
    tok_spec, gate_spec = {
        "rows": (rows_spec, rows_spec),
        "rwkv_pairs": (pl.BlockSpec((RWKV_PAIRS, tm, LANES), lambda i: (0, i, 0)), rows_spec),
        "gla_heads": (_head_row_spec(tm, GLA_DV), _head_row_spec(tm, GLA_DV))}[layout]
    in_specs = ([tok_spec, gate_spec, _row_spec(tm, MEM_WIDTH), _row_spec(tm, D_MODEL)]
                + [_full_spec(w.shape) for w in ws])
    args = [tok, gate, att, x, *ws]
    if aliased:
        in_specs.append(pl.BlockSpec(memory_space=pl.ANY))
        args.append(h2_buffer)
    return pl.pallas_call(
        functools.partial(_out_proj_kernel, layout=layout, aliased=int(aliased)), grid=(m // tm,),
        in_specs=in_specs,
        out_specs=[_row_spec(tm, D_MODEL), _row_spec(tm, D_MODEL, h2_row_offset // tm), _row_spec(tm, ROUTER_LANES)],
        out_shape=[jax.ShapeDtypeStruct((m, D_MODEL), F32), jax.ShapeDtypeStruct((h2_rows, D_MODEL), F32),
                   jax.ShapeDtypeStruct((m, ROUTER_LANES), F32)],
        input_output_aliases={len(args) - 1: 1} if aliased else {},
        compiler_params=_cparams("parallel"), name="out_proj")(*args)


def _moe_kernel(tile_ref, exp_ref, nitem_ref, lo_ref, hi_ref, x_ref, gate_ref, wu_ref, wd_ref, o_ref, wu_scr, wd_scr):
    w = pl.program_id(0)
    prev = jnp.maximum(w - 1, 0)
    e = exp_ref[w]
    valid = w < nitem_ref[0]

    @pl.when(jnp.logical_and(valid, jnp.logical_or(w == 0, e != exp_ref[prev])))
    def _():
        wu_scr[...] = wu_ref[0, 0].astype(BF16)
        wd_scr[...] = wd_ref[0, 0].astype(BF16)

    @pl.when(jnp.logical_or(w == 0, tile_ref[w] != tile_ref[prev]))
    def _():
        o_ref[...] = jnp.zeros(o_ref.shape, F32)

    @pl.when(valid)
    def _():
        gu = _dot(x_ref[...].astype(BF16), wu_scr[...])
        g = gu[:, :EXPERT_FF]
        act = (g * _sigmoid(g) * gu[:, EXPERT_FF:]).astype(BF16)
        gate_col = jnp.concatenate([gate_ref[...], jnp.zeros((LANES - 1, MOE_BLOCK), F32)], axis=0).T[:, :1]
        out = _dot(act, wd_scr[...]) * gate_col
        rows = tile_ref[w] * MOE_BLOCK + lax.broadcasted_iota(jnp.int32, (MOE_BLOCK, 1), 0)
        mine = jnp.logical_and(rows >= lo_ref[e], rows < hi_ref[e])
        o_ref[...] = o_ref[...] + jnp.where(mine, out, 0.0)


def _moe_ffn(xs, row_gate, item_tile, item_expert, n_items, lo, hi, w_up, w_down, layer):
    tm = MOE_BLOCK
    n_work = item_tile.shape[0]
    a = xs.shape[0]
    row_map = lambda w, tile, ex, ni, lo_, hi_: (tile[w], 0)
    exp_map = lambda w, tile, ex, ni, lo_, hi_: (layer, ex[w], 0, 0)
    grid_spec = pltpu.PrefetchScalarGridSpec(
        num_scalar_prefetch=5, grid=(n_work,),
        in_specs=[pl.BlockSpec((tm, D_MODEL), row_map),
                  pl.BlockSpec((None, 1, tm), lambda w, tile, ex, ni, lo_, hi_: (tile[w], 0, 0)),
                  pl.BlockSpec((1, 1, D_MODEL, 2 * EXPERT_FF), exp_map),
                  pl.BlockSpec((1, 1, EXPERT_FF, D_MODEL), exp_map)],
        out_specs=pl.BlockSpec((tm, D_MODEL), row_map),
        scratch_shapes=[pltpu.VMEM((D_MODEL, 2 * EXPERT_FF), BF16), pltpu.VMEM((EXPERT_FF, D_MODEL), BF16)])
    return pl.pallas_call(
        _moe_kernel, grid_spec=grid_spec,
        out_shape=jax.ShapeDtypeStruct((a, D_MODEL), F32),
        compiler_params=_cparams("arbitrary"), name="moe_ffn")(
            item_tile, item_expert, n_items, lo, hi, xs, row_gate.reshape(a // tm, 1, tm), w_up, w_down)


def _route(logits, n_prompt, tm):
    m = logits.shape[0]
    n_sample = m - n_prompt
    a = 2 * m
    gl = logits[:, :N_GROUPS]
    el = logits[:, N_GROUPS:N_GROUPS + N_EXPERTS]
    group = jnp.argmax(gl, -1).astype(jnp.int32)
    p_group = jnp.max(jax.nn.softmax(gl, -1), -1, keepdims=True)
    experts_row = jnp.arange(N_EXPERTS, dtype=jnp.int32)[None, :]
    masked = jnp.where(experts_row // EXPERTS_PER_GROUP == group[:, None], el, -jnp.inf)
    e1 = jnp.argmax(masked, -1).astype(jnp.int32)
    masked2 = jnp.where(experts_row == e1[:, None], -jnp.inf, masked)
    e2 = jnp.argmax(masked2, -1).astype(jnp.int32)
    top_val = jnp.stack([jnp.max(masked, -1), jnp.max(masked2, -1)], axis=-1)
    gate = p_group * jax.nn.softmax(top_val, -1)
    expert = jnp.stack([e1, e2], axis=-1)

    def by_id(t):
        return jnp.concatenate([t[:n_prompt, 0], t[:n_prompt, 1], t[n_prompt:, 0], t[n_prompt:, 1]])

    tok_of_id = jnp.asarray(np.concatenate([np.arange(n_prompt), np.arange(n_prompt),
                                            n_prompt + np.arange(n_sample), n_prompt + np.arange(n_sample)]), jnp.int32)
    flat_e = by_id(expert)
    ids = jnp.arange(a, dtype=jnp.int32)
    _, order, gate_sorted, tok_sorted = lax.sort((flat_e, ids, by_id(gate), tok_of_id), num_keys=1, is_stable=True)
    _, inv = lax.sort((order, ids), num_keys=1)
    experts = jnp.arange(N_EXPERTS, dtype=jnp.int32)
    counts = jnp.sum((flat_e[:, None] == experts[None, :]).astype(jnp.int32), axis=0)
    hi = jnp.cumsum(counts).astype(jnp.int32)
    lo = hi - counts
    n_tiles = a // tm
    first_tile = lo // tm
    tiles_of = jnp.where(counts > 0, (hi - 1) // tm - first_tile + 1, 0)
    item_end = jnp.cumsum(tiles_of).astype(jnp.int32)
    n_items = item_end[-1:]
    n_work = n_tiles + N_EXPERTS - 1
    w = jnp.minimum(jnp.arange(n_work, dtype=jnp.int32), n_items[0] - 1)
    item_expert = jnp.sum((item_end[None, :] <= w[:, None]).astype(jnp.int32), axis=1)
    onehot = (item_expert[:, None] == experts[None, :]).astype(jnp.int32)
    item_tile = jnp.sum(onehot * (first_tile - (item_end - tiles_of))[None, :], axis=1) + w
    return tok_sorted, gate_sorted, inv, item_tile.astype(jnp.int32), item_expert, n_items, lo, hi


def _combine_kernel(x_ref, r0_ref, r1_ref, g_ref, *outs):
    x2 = x_ref[...] + (r0_ref[...] + r1_ref[...])
    outs[-1][...] = _rms(x2, g_ref[...])
    if len(outs) == 2:
        outs[0][...] = x2


def _combine(x1, gathered, first_row, g, tm, last_layer):
    m = x1.shape[0]
    spec = _row_spec(tm, D_MODEL)
    n_out = 1 if last_layer else 2
    outs = pl.pallas_call(
        _combine_kernel, grid=(m // tm,),
        in_specs=[spec, _row_spec(tm, D_MODEL, first_row // tm), _row_spec(tm, D_MODEL, (first_row + m) // tm),
                  _full_spec((1, D_MODEL))],
        out_specs=[spec] * n_out,
        out_shape=[jax.ShapeDtypeStruct((m, D_MODEL), F32)] * n_out,
        compiler_params=_cparams("parallel"), name="moe_combine")(x1, gathered, gathered, g.reshape(1, D_MODEL))
    return (None, outs[0]) if last_layer else tuple(outs)


def _to_scan_layout(t2d, b, t):
    return t2d.reshape(b, t, RWKV_HEADS, RWKV_N).transpose(1, 3, 2, 0).reshape(t, RWKV_N, RWKV_HEADS * b)


def _from_scan_layout(y, b, t):
    return y.reshape(t, RWKV_N, RWKV_HEADS, b).transpose(3, 0, 2, 1).reshape(b * t, TOK_WIDTH)


def _scan_param(p, b):
    return jnp.repeat(p.reshape(RWKV_HEADS, RWKV_N).T, b, axis=1)


def _seq_scan_param(p, b):
    return jnp.pad(_scan_param(p, b), ((0, 0), (0, LANES - RWKV_HEADS * b)))


def kernel(x_prompt, x_sample, mem_prompt, state_rwkv_S, state_rwkv_shift, state_gla_S, cache_mem_k, cache_mem_v, norm_mix_g, norm_ffn_g, norm_mem_g, norm_final_g, w_in, w_out, w_mem_kv, rw_mu, rw_w0, rw_w1, rw_w2, rw_a0, rw_a1, rw_a2, rw_g1, rw_g2, rw_k_k, rw_k_a, rw_r_k, rw_ln_g, rw_ln_b, gla_a1, gla_a2, gla_ab, gla_norm_g, router_wg, router_bg, router_we, router_be, exp_w_up, exp_w_down):
    bp, tp, _ = x_prompt.shape
    bs, ts, _ = x_sample.shape
    assert ts == 1 and tp % GLA_CHUNK == 0 and tp % SCAN_TIME_BLOCK == 0
    np_ = bp * tp
    ns = bs * ts
    m = np_ + ns
    assert np_ % LIGHT_BLOCK == 0 and ns % SAMPLE_BLOCK == 0 and (2 * m) % MOE_BLOCK == 0
    depth = w_in.shape[0]
    nh = GLA_HEADS
    tw = TOK_WIDTH
    bf = lambda t_: t_.astype(BF16)

    mem2d = mem_prompt.reshape(bp * N_MEM, D_MODEL)
    mem_kv = [_norm_matmul(mem2d, norm_mem_g[i], w_mem_kv[i], 512) for i in range(depth)]
    pk = jnp.stack([kv[:, :MEM_WIDTH].reshape(bp, N_MEM, MEM_WIDTH) for kv in mem_kv])
    pv = jnp.stack([kv[:, MEM_WIDTH:].reshape(bp, N_MEM, MEM_WIDTH) for kv in mem_kv])
    prompt_mem_k = pk.reshape(depth, bp, N_MEM, MEM_HEADS, MEM_HEAD_DIM)
    prompt_mem_v = pv.reshape(depth, bp, N_MEM, MEM_HEADS, MEM_HEAD_DIM)
    sk = cache_mem_k.reshape(depth, bs, N_MEM, MEM_WIDTH).transpose(0, 1, 3, 2)
    sv = cache_mem_v.reshape(depth, bs, N_MEM, MEM_WIDTH).transpose(0, 1, 3, 2)

    x_p = x_prompt.reshape(np_, D_MODEL)
    x_s = x_sample.reshape(ns, D_MODEL)
    h_p = None
    h_s = _norm(x_s, norm_mix_g[0], SAMPLE_BLOCK)

    w_router = jnp.zeros((depth, D_MODEL, ROUTER_LANES), F32)
    w_router = w_router.at[:, :, :N_GROUPS].set(router_wg).at[:, :, N_GROUPS:N_GROUPS + N_EXPERTS].set(router_we)
    b_router = jnp.zeros((depth, 1, ROUTER_LANES), F32)
    b_router = b_router.at[:, 0, :N_GROUPS].set(router_bg).at[:, 0, N_GROUPS:N_GROUPS + N_EXPERTS].set(router_be)

    p_rw_S, p_rw_shift, p_gla_S, s_rw_S, s_rw_shift, s_gla_S = [], [], [], [], [], []
    pending_p = pending_s = None
    for i in range(depth):
        j = i // 2
        if i % 2 == 0:
            wi = w_in[i]
            ws = [rw_mu[j], bf(wi[:, :tw]), bf(wi[:, tw:2 * tw]), bf(wi[:, 2 * tw:3 * tw]), bf(wi[:, 3 * tw:]),
                  bf(rw_w1[j]), bf(rw_w2[j]), rw_w0[j].reshape(1, tw), bf(rw_a1[j]), bf(rw_a2[j]),
                  rw_a0[j].reshape(1, tw), bf(rw_g1[j]), bf(rw_g2[j])]
            if h_p is None:
                *rkvda_p, gate_p, qm_p = _rwkv_proj(x_p, None, ws, PROJ_BLOCK, tp, norm_g=norm_mix_g[i])
                shift_p = _norm(x_prompt[:, -1, :], norm_mix_g[i], bp)
            else:
                *rkvda_p, gate_p, qm_p = _rwkv_proj(h_p, None, ws, PROJ_BLOCK, tp)
                shift_p = h_p.reshape(bp, tp, D_MODEL)[:, -1]
            *rkvda_s, gate_s, qm_s = _rwkv_proj(h_s, state_rwkv_shift[j], ws, SAMPLE_BLOCK, ts)
            pvec = [rw_k_k[j], rw_k_a[j], rw_r_k[j], rw_ln_g[j], rw_ln_b[j]]
            tok_p, sp = _rwkv_prompt_mixer(*rkvda_p, *[_seq_scan_param(p, bp) for p in pvec], bp, tp, SCAN_TIME_BLOCK)
            assert bs == LANES
            ys, ss = _rwkv_scan(*[_to_scan_layout(t_, bs, ts) for t_ in rkvda_s], [_scan_param(p, bs) for p in pvec],
                                state_rwkv_S[j].transpose(1, 2, 3, 0), 1)
            tok_s = _from_scan_layout(ys, bs, ts)
            p_rw_S.append(sp[:, :, :RWKV_HEADS * bp].reshape(RWKV_N, RWKV_N, RWKV_HEADS, bp).transpose(3, 2, 0, 1))
            s_rw_S.append(ss.transpose(3, 0, 1, 2))
            p_rw_shift.append(shift_p)
            s_rw_shift.append(h_s)
            layout_p, layout_s = "rwkv_pairs", "rows"
        else:
            ws = _gla_weights(w_in[i], gla_a1[j], gla_a2[j], gla_ab[j])
            if pending_p is None:
                q_p, k_p, v_p, gate_p, la_p, qm_p = _gla_proj(h_p, ws, PROJ_BLOCK)
                q_s, k_s, v_s, gate_s, la_s, qm_s = _gla_proj(h_s, ws, SAMPLE_BLOCK)
            else:
                q_p, k_p, v_p, gate_p, la_p, qm_p, x_p = _gla_proj(None, ws, PROJ_BLOCK, pending_p)
                q_s, k_s, v_s, gate_s, la_s, qm_s, x_s = _gla_proj(None, ws, SAMPLE_BLOCK, pending_s)
            tok_p, sp_t = _gla_chunk_scan(q_p, k_p, v_p, la_p, gla_norm_g[j], bp, tp)
            lanes_of = lambda t_: t_.transpose(0, 2, 1)
            os_, ss = _gla_step(lanes_of(q_s), lanes_of(k_s), lanes_of(v_s), lanes_of(la_s), gla_norm_g[j],
                                state_gla_S.transpose(0, 2, 3, 4, 1), j)
            tok_s = os_.transpose(0, 2, 1)
            p_gla_S.append(sp_t.transpose(0, 1, 3, 2))
            s_gla_S.append(ss.transpose(3, 0, 1, 2))
            layout_p = layout_s = "gla_heads"

        att_p = _mem_attn(qm_p.reshape(bp, tp, MEM_WIDTH), pk, pv, i, 512).reshape(np_, MEM_WIDTH)
        att_s = _mem_attn(qm_s.reshape(bs, ts, MEM_WIDTH), sk, sv, i, 1, memory_transposed=True).reshape(ns, MEM_WIDTH)
        wr_hi = bf(w_router[i])
        wr_lo = bf(w_router[i] - wr_hi.astype(F32))
        ws = [bf(w_out[i]), norm_ffn_g[i].reshape(1, D_MODEL), wr_hi, wr_lo, b_router[i]]
        x1_p, h2, logits_p = _out_proj(tok_p, gate_p, att_p, x_p, ws, layout_p, PROJ_BLOCK, m, 0)
        x1_s, h2, logits_s = _out_proj(tok_s, gate_s, att_s, x_s, ws, layout_s, SAMPLE_BLOCK, m, np_, h2_buffer=h2)
        tok_sorted, gate_sorted, inv, item_tile, item_expert, n_items, lo, hi = _route(
            jnp.concatenate([logits_p, logits_s], axis=0), np_, MOE_BLOCK)
        rows = _moe_ffn(h2[tok_sorted], gate_sorted, item_tile, item_expert, n_items, lo, hi,
                        exp_w_up, exp_w_down, i)
        gathered = rows[inv]
        g_next = norm_mix_g[i + 1] if i + 1 < depth else norm_final_g
        if i + 1 < depth and (i + 1) % 2 == 1:
            pending_p = (x1_p, gathered, 0, g_next)
            pending_s = (x1_s, gathered, 2 * np_, g_next)
        else:
            pending_p = pending_s = None
            x_p, h_p = _combine(x1_p, gathered, 0, g_next, LIGHT_BLOCK, i + 1 == depth)
            x_s, h_s = _combine(x1_s, gathered, 2 * np_, g_next, SAMPLE_BLOCK, i + 1 == depth)

    y_prompt = h_p.reshape(bp, tp, D_MODEL)
    y_sample = h_s.reshape(bs, ts, D_MODEL)
    return (y_prompt, y_sample, jnp.stack(p_rw_S), jnp.stack(p_rw_shift), jnp.stack(p_gla_S),
            prompt_mem_k, prompt_mem_v, jnp.stack(s_rw_S), jnp.stack(s_rw_shift), jnp.stack(s_gla_S))
```

```python
import functools

import numpy as np
import jax
import jax.numpy as jnp
from jax import lax
from jax.experimental import pallas as pl
from jax.experimental.pallas import tpu as pltpu

F32 = jnp.float32
BF16 = jnp.bfloat16

D_MODEL = 1024
TOK_WIDTH = 768
MEM_WIDTH = 256
MEM_HEADS = 4
MEM_HEAD_DIM = 64
N_MEM = 256
RWKV_HEADS = 12
RWKV_N = 64
RWKV_PAIRS = RWKV_HEADS // 2
RWKV_GN_EPS = 64e-5
GLA_HEADS = 4
GLA_KW = 384
GLA_DK = 96
GLA_DV = 192
GLA_TAU = 16.0
GLA_CHUNK = 64
GLA_TILE = 8
N_GROUPS = 4
EXPERTS_PER_GROUP = 8
N_EXPERTS = 32
EXPERT_FF = 512
NORM_EPS = 1e-6
ROUTER_LANES = 128
LANES = 128
SUBLANES = 8

PROJ_BLOCK = 512
LIGHT_BLOCK = 512
SAMPLE_BLOCK = 128
MOE_BLOCK = 256
SCAN_TIME_BLOCK = 32
VMEM_LIMIT = 56 * 1024 * 1024


def _cparams(*sem):
    return pltpu.CompilerParams(dimension_semantics=sem, vmem_limit_bytes=VMEM_LIMIT)


def _dot(a, b):
    return jnp.dot(a, b, preferred_element_type=F32)


def _dot_nt(a, b):
    return lax.dot_general(a, b, (((1,), (1,)), ((), ())), preferred_element_type=F32)


def _dot_tn(a, b):
    return lax.dot_general(a, b, (((0,), (0,)), ((), ())), preferred_element_type=F32)


def _rms(x, g):
    return x * lax.rsqrt(jnp.mean(x * x, axis=-1, keepdims=True) + NORM_EPS) * g


def _sigmoid(x):
    return 1.0 / (1.0 + jnp.exp(-x))


def _row_spec(tm, n, offset=0):
    return pl.BlockSpec((tm, n), lambda i: (i + offset, 0))


def _head_row_spec(tm, n):
    return pl.BlockSpec((GLA_HEADS, tm, n), lambda i: (0, i, 0))


def _full_spec(shape):
    nd = len(shape)
    return pl.BlockSpec(shape, lambda *_: (0,) * nd)


def _norm_kernel(x_ref, g_ref, o_ref):
    o_ref[...] = _rms(x_ref[...], g_ref[...])


def _norm(x, g, tm):
    m, d = x.shape
    return pl.pallas_call(
        _norm_kernel, grid=(m // tm,),
        in_specs=[_row_spec(tm, d), _full_spec((1, d))],
        out_specs=_row_spec(tm, d),
        out_shape=jax.ShapeDtypeStruct((m, d), F32),
        compiler_params=_cparams("parallel"), name="rms_norm")(x, g.reshape(1, d))


def _norm_matmul_kernel(x_ref, g_ref, w_ref, o_ref):
    o_ref[...] = _dot(_rms(x_ref[...], g_ref[...]).astype(BF16), w_ref[...])


def _norm_matmul(x, g, w, tm):
    m, d = x.shape
    n = w.shape[1]
    return pl.pallas_call(
        _norm_matmul_kernel, grid=(m // tm,),
        in_specs=[_row_spec(tm, d), _full_spec((1, d)), _full_spec((d, n))],
        out_specs=_row_spec(tm, n),
        out_shape=jax.ShapeDtypeStruct((m, n), F32),
        compiler_params=_cparams("parallel"), name="norm_matmul")(x, g.reshape(1, d), w.astype(BF16))


def _rwkv_proj_kernel(h_ref, hp_ref, *refs, blocks_per_seq, norm_input):
    if norm_input:
        ng_ref, refs = refs[0], refs[1:]
        normed = lambda t_: _rms(t_, ng_ref[...])
    else:
        normed = lambda t_: t_
    (mu_ref, wr_ref, wk_ref, wv_ref, wq_ref, w1_ref, w2_ref, w0_ref, a1_ref, a2_ref, a0_ref, g1_ref, g2_ref,
     r_out, k_out, v_out, d_out, a_out, g_out, q_out) = refs

    def put(out, val):
        if blocks_per_seq:
            for p in range(RWKV_PAIRS):
                out[p] = val[:, p * LANES:(p + 1) * LANES]
        else:
            out[...] = val

    h = normed(h_ref[...])
    if blocks_per_seq:
        seq_start = (pl.program_id(0) % blocks_per_seq) == 0
        before = jnp.where(seq_start, 0.0, normed(hp_ref[SUBLANES - 1:SUBLANES, :]))
        row = lax.broadcasted_iota(jnp.int32, (h.shape[0], 1), 0)
        hp = jnp.where(row == 0, before, pltpu.roll(h, 1, axis=0))
    else:
        hp = hp_ref[...]
    xx = hp - h

    def mix(j):
        return (h + xx * mu_ref[j:j + 1, :]).astype(BF16)

    put(r_out, _dot(mix(0), wr_ref[...]))
    wl = w0_ref[...] + _dot(jnp.tanh(_dot(mix(1), w1_ref[...])).astype(BF16), w2_ref[...])
    put(k_out, _dot(mix(2), wk_ref[...]))
    put(v_out, _dot(mix(3), wv_ref[...]))
    al = a0_ref[...] + _dot(_dot(mix(4), a1_ref[...]).astype(BF16), a2_ref[...])
    g_out[...] = _dot(_sigmoid(_dot(mix(5), g1_ref[...])).astype(BF16), g2_ref[...])
    q_out[...] = _dot(h.astype(BF16), wq_ref[...])
    z = -wl
    softplus = jnp.maximum(z, 0.0) + jnp.log(1.0 + jnp.exp(-jnp.abs(z)))
    put(d_out, jnp.exp(-jnp.exp(-softplus - 0.5)))
    put(a_out, _sigmoid(al))


def _rwkv_proj(h, h_prev, ws, tm, seq_len, norm_g=None):
    m = h.shape[0]
    if norm_g is not None:
        assert h_prev is None
        ws = [norm_g.reshape(1, D_MODEL)] + list(ws)
    tw = TOK_WIDTH
    if h_prev is None:
        assert seq_len % tm == 0 and tm % SUBLANES == 0
        per8 = tm // SUBLANES
        hp_spec = pl.BlockSpec((SUBLANES, D_MODEL), lambda i: (jnp.maximum(i * per8 - 1, 0), 0))
        h_prev, blocks_per_seq = h, seq_len // tm
    else:
        hp_spec, blocks_per_seq = _row_spec(tm, D_MODEL), 0
    if blocks_per_seq:
        scan_spec = pl.BlockSpec((RWKV_PAIRS, tm, LANES), lambda i: (0, i, 0))
        scan_shape = jax.ShapeDtypeStruct((RWKV_PAIRS, m, LANES), F32)
    else:
        scan_spec, scan_shape = _row_spec(tm, tw), jax.ShapeDtypeStruct((m, tw), F32)
    return pl.pallas_call(
        functools.partial(_rwkv_proj_kernel, blocks_per_seq=blocks_per_seq, norm_input=norm_g is not None),
        grid=(m // tm,),
        in_specs=[_row_spec(tm, D_MODEL), hp_spec] + [_full_spec(w.shape) for w in ws],
        out_specs=[scan_spec] * 5 + [_row_spec(tm, tw), _row_spec(tm, MEM_WIDTH)],
        out_shape=[scan_shape] * 5 + [jax.ShapeDtypeStruct((m, tw), F32), jax.ShapeDtypeStruct((m, MEM_WIDTH), F32)],
        compiler_params=_cparams("parallel"), name="rwkv_proj")(h, h_prev, *ws)


def _pad_lanes(x):
    short = LANES - x.shape[-1]
    if short == 0:
        return x
    return jnp.concatenate([x, jnp.zeros(x.shape[:-1] + (short,), x.dtype)], axis=-1)


def _rwkv_scan_kernel(r_ref, k_ref, v_ref, d_ref, a_ref, kkp_ref, kap_ref, rkp_ref, lng_ref, lnb_ref, s0_ref,
                      y_ref, sfin_ref, s_scr, v_scr, yrow_scr, *, tc):
    n = RWKV_N
    nl = r_ref.shape[-1]

    @pl.when(pl.program_id(1) == 0)
    def _():
        s_scr[...] = _pad_lanes(s0_ref[...])

    kkp = _pad_lanes(kkp_ref[...])
    kap = _pad_lanes(kap_ref[...])
    rkp = _pad_lanes(rkp_ref[...])
    lng = _pad_lanes(lng_ref[...])
    lnb = _pad_lanes(lnb_ref[...])

    def step(t, carry):
        r_t = _pad_lanes(r_ref[t])
        k_t = _pad_lanes(k_ref[t])
        v_t = _pad_lanes(v_ref[t])
        d_t = _pad_lanes(d_ref[t])
        a_t = _pad_lanes(a_ref[t])
        v_scr[...] = v_t
        kkr = k_t * kkp
        nrm = jnp.maximum(jnp.sqrt(jnp.sum(kkr * kkr, axis=0, keepdims=True)), 1e-12)
        kk = kkr * (1.0 / nrm)
        k2 = k_t * (1.0 + (a_t - 1.0) * kap)
        nkk = -kk
        b_t = kk * a_t

        def ibody(i, c):
            s_i = s_scr[i]
            sa = jnp.sum(s_i * nkk, axis=0, keepdims=True)
            v_i = v_scr[pl.ds(i, 1), :]
            s_n = s_i * d_t + sa * b_t + v_i * k2
            s_scr[i] = s_n
            yrow_scr[pl.ds(i, 1), :] = jnp.sum(s_n * r_t, axis=0, keepdims=True)
            return c

        lax.fori_loop(0, n, ibody, 0, unroll=8)
        y = yrow_scr[...]
        yc = y - jnp.mean(y, axis=0, keepdims=True)
        var = jnp.mean(yc * yc, axis=0, keepdims=True)
        gn = yc * lax.rsqrt(var + RWKV_GN_EPS) * lng + lnb
        bonus = jnp.sum(r_t * k2 * rkp, axis=0, keepdims=True) * v_t
        y_ref[t] = (gn + bonus)[:, :nl]
        return carry

    lax.fori_loop(0, tc, step, 0)

    @pl.when(pl.program_id(1) == pl.num_programs(1) - 1)
    def _():
        sfin_ref[...] = s_scr[:, :, :nl]


def _rwkv_scan(r, k, v, d, a, params, s0, tc):
    t, n, l = r.shape
    groups, lane_block = s0.shape[0], s0.shape[3]
    assert groups * lane_block == l
    seq = pl.BlockSpec((tc, n, lane_block), lambda li, ti: (ti, 0, li))
    par = pl.BlockSpec((n, lane_block), lambda li, ti: (0, li))
    st = pl.BlockSpec((None, n, n, lane_block), lambda li, ti: (li, 0, 0, 0))
    return pl.pallas_call(
        functools.partial(_rwkv_scan_kernel, tc=tc), grid=(groups, t // tc),
        in_specs=[seq] * 5 + [par] * 5 + [st],
        out_specs=[seq, st],
        out_shape=[jax.ShapeDtypeStruct((t, n, l), F32), jax.ShapeDtypeStruct((groups, n, n, lane_block), F32)],
        scratch_shapes=[pltpu.VMEM((n, n, LANES), F32), pltpu.VMEM((n, LANES), F32), pltpu.VMEM((n, LANES), F32)],
        compiler_params=_cparams("parallel", "arbitrary"), name="rwkv_scan")(r, k, v, d, a, *params, s0)


PREP_TILES = 6
DECAY_WINDOW = 32


def _rwkv_prep_kernel(r_ref, k_ref, v_ref, d_ref, a_ref, kkp_ref, kap_ref, o_ref, *, tc, nb):
    n = RWKV_N
    rows_per_pair = nb * tc
    ins = [ref.reshape(RWKV_PAIRS * rows_per_pair, LANES) for ref in (r_ref, k_ref, v_ref, d_ref, a_ref)]
    zero_rows = jnp.zeros((LANES - RWKV_HEADS * nb, LANES), F32)
    first_half = lax.broadcasted_iota(jnp.int32, (nb, LANES), 1) < n
    kkp = kkp_ref[...]
    kap = kap_ref[...]

    def load_transposed_pair(x2, t):
        pieces = []
        for p in range(RWKV_PAIRS):
            now = x2[pl.ds(p * rows_per_pair + t, nb, stride=tc), :]
            nxt = x2[pl.ds(p * rows_per_pair + t + 1, nb, stride=tc), :]
            pieces += [jnp.where(first_half, now, pltpu.roll(nxt, n, axis=1)),
                       jnp.where(first_half, pltpu.roll(now, n, axis=1), nxt)]
        both = jnp.concatenate(pieces + [zero_rows], axis=0).T
        return both[:n, :], both[n:, :]

    def emit(t, c_prev, r_t, k_t, v_t, d_t, a_t):
        kkr = k_t * kkp
        nrm = jnp.maximum(jnp.sqrt(jnp.sum(kkr * kkr, axis=0, keepdims=True)), 1e-12)
        kk = kkr * (1.0 / nrm)
        k2 = k_t * (1.0 + (a_t - 1.0) * kap)
        c_prev = jnp.where(t % DECAY_WINDOW == 0, 1.0, c_prev)
        c_t = c_prev * d_t
        inv = 1.0 / jnp.maximum(c_t, 1e-30)
        o_ref[t, 0] = -kk * c_prev
        o_ref[t, 1] = c_t
        o_ref[t, 2] = kk * a_t * inv
        o_ref[t, 3] = k2 * inv
        o_ref[t, 4] = r_t * c_t
        o_ref[t, 5] = v_t
        return c_t

    def two_steps(u, c):
        t = 2 * u
        tiles = [load_transposed_pair(x2, t) for x2 in ins]
        c = emit(t, c, *[tile[0] for tile in tiles])
        return emit(t + 1, c, *[tile[1] for tile in tiles])

    lax.fori_loop(0, tc // 2, two_steps, jnp.ones((n, LANES), F32), unroll=4)


def _rwkv_state_scan_kernel(x_ref, rkp_ref, lng_ref, lnb_ref, y_ref, sfin_ref, s_scr, sa_scr, yrow_scr, *, tc):
    n = RWKV_N
    groups = n // SUBLANES

    @pl.when(pl.program_id(0) == 0)
    def _():
        s_scr[...] = jnp.zeros(s_scr.shape, F32)

    rkp = rkp_ref[...]
    lng = lng_ref[...]
    lnb = lnb_ref[...]
    sub = lax.broadcasted_iota(jnp.int32, (SUBLANES, LANES), 0)
    low4 = sub < 4
    low2 = (sub & 3) < 2
    low1 = (sub & 1) == 0

    def fold(x, y, dist, low):
        if dist == 4:
            return jnp.where(low, x, y) + pltpu.roll(jnp.where(low, y, x), 4, axis=0)
        return (jnp.where(low, x, pltpu.roll(y, dist, axis=0))
                + jnp.where(low, pltpu.roll(x, SUBLANES - dist, axis=0), y))

    def sublane_sums(ps):
        z = [fold(ps[0], ps[4], 4, low4), fold(ps[2], ps[6], 4, low4),
             fold(ps[1], ps[5], 4, low4), fold(ps[3], ps[7], 4, low4)]
        return fold(fold(z[0], z[1], 2, low2), fold(z[2], z[3], 2, low2), 1, low1)

    def tile_sum(x):
        acc = x[0:SUBLANES]
        for u in range(1, groups):
            acc = acc + x[u * SUBLANES:(u + 1) * SUBLANES]
        return acc

    def step(t, carry):
        nkk = x_ref[t, 0]
        b_t = x_ref[t, 2]
        k2 = x_ref[t, 3]
        r_t = x_ref[t, 4]
        for g in range(groups):
            sa_scr[g * SUBLANES:(g + 1) * SUBLANES, :] = sublane_sums(
                [tile_sum(s_scr[g * SUBLANES + u] * nkk) for u in range(SUBLANES)])
        for g in range(groups):
            ps = []
            for u in range(SUBLANES):
                i = g * SUBLANES + u
                s_n = s_scr[i] + sa_scr[pl.ds(i, 1), :] * b_t + x_ref[t, 5, pl.ds(i, 1), :] * k2
                s_scr[i] = s_n
                ps.append(tile_sum(s_n * r_t))
            yrow_scr[g * SUBLANES:(g + 1) * SUBLANES, :] = sublane_sums(ps)
        y = yrow_scr[...]
        yc = y - jnp.mean(y, axis=0, keepdims=True)
        var = jnp.mean(yc * yc, axis=0, keepdims=True)
        bonus = jnp.sum(r_t * k2 * rkp, axis=0, keepdims=True) * x_ref[t, 5]
        y_ref[t] = yc * lax.rsqrt(var + RWKV_GN_EPS) * lng + lnb + bonus

        @pl.when(t % DECAY_WINDOW == DECAY_WINDOW - 1)
        def _():
            c_t = x_ref[t, 1]
            for i in range(n):
                s_scr[i] = s_scr[i] * c_t

        return carry

    lax.fori_loop(0, tc, step, 0)

    @pl.when(pl.program_id(0) == pl.num_programs(0) - 1)
    def _():
        sfin_ref[...] = s_scr[...]


def _rwkv_unprep_kernel(y_ref, o_ref, *, tc, nb):
    n = RWKV_N
    rows_per_pair = nb * tc
    o2 = o_ref.reshape(RWKV_PAIRS * rows_per_pair, LANES)
    first_half = lax.broadcasted_iota(jnp.int32, (nb, LANES), 1) < n

    def two_steps(u, carry):
        t = 2 * u
        w = jnp.concatenate([y_ref[t], y_ref[t + 1]], axis=0).T
        for p in range(RWKV_PAIRS):
            even = w[(2 * p) * nb:(2 * p + 1) * nb, :]
            odd = w[(2 * p + 1) * nb:(2 * p + 2) * nb, :]
            o2[pl.ds(p * rows_per_pair + t, nb, stride=tc), :] = jnp.where(
                first_half, even, pltpu.roll(odd, n, axis=1))
            o2[pl.ds(p * rows_per_pair + t + 1, nb, stride=tc), :] = jnp.where(
                first_half, pltpu.roll(even, n, axis=1), odd)
        return carry

    lax.fori_loop(0, tc // 2, two_steps, 0, unroll=4)


def _rwkv_prompt_mixer(r, k, v, d, a, kkp, kap, rkp, lng, lnb, nb, t, tc):
    n = RWKV_N
    assert tc % SUBLANES == 0 and tc % 4 == 0 and t % tc == 0 and RWKV_HEADS * nb <= LANES
    assert tc % DECAY_WINDOW == 0
    grid = (t // tc,)
    tok = pl.BlockSpec((RWKV_PAIRS, nb, tc, LANES), lambda ti: (0, 0, ti, 0))
    par = pl.BlockSpec((n, LANES), lambda ti: (0, 0))
    tiles = pl.BlockSpec((tc, PREP_TILES, n, LANES), lambda ti: (ti, 0, 0, 0))
    ytile = pl.BlockSpec((tc, n, LANES), lambda ti: (ti, 0, 0))
    prepared = pl.pallas_call(
        functools.partial(_rwkv_prep_kernel, tc=tc, nb=nb), grid=grid,
        in_specs=[tok] * 5 + [par] * 2, out_specs=tiles,
        out_shape=jax.ShapeDtypeStruct((t, PREP_TILES, n, LANES), F32),
        compiler_params=_cparams("parallel"), name="rwkv_prep")(
            *[x.reshape(RWKV_PAIRS, nb, t, LANES) for x in (r, k, v, d, a)], kkp, kap)
    y, s = pl.pallas_call(
        functools.partial(_rwkv_state_scan_kernel, tc=tc), grid=grid,
        in_specs=[tiles, par, par, par],
        out_specs=[ytile, pl.BlockSpec((n, n, LANES), lambda ti: (0, 0, 0))],
        out_shape=[jax.ShapeDtypeStruct((t, n, LANES), F32), jax.ShapeDtypeStruct((n, n, LANES), F32)],
        scratch_shapes=[pltpu.VMEM((n, n, LANES), F32), pltpu.VMEM((n, LANES), F32), pltpu.VMEM((n, LANES), F32)],
        compiler_params=_cparams("arbitrary"), name="rwkv_state_scan")(prepared, rkp, lng, lnb)
    tok_out = pl.pallas_call(
        functools.partial(_rwkv_unprep_kernel, tc=tc, nb=nb), grid=grid,
        in_specs=[ytile], out_specs=tok,
        out_shape=jax.ShapeDtypeStruct((RWKV_PAIRS, nb, t, LANES), F32),
        compiler_params=_cparams("parallel"), name="rwkv_unprep")(y)
    return tok_out.reshape(RWKV_PAIRS, nb * t, LANES), s


def _gla_proj_kernel(*refs, fused_combine):
    if fused_combine:
        x_ref, r0_ref, r1_ref, g_ref = refs[:4]
        refs, x_out = refs[4:-1], refs[-1]
        x2 = x_ref[...] + (r0_ref[...] + r1_ref[...])
        x_out[...] = x2
        h = _rms(x2, g_ref[...])
    else:
        h, refs = refs[0][...], refs[1:]
    (wq_ref, wk_ref, wv_ref, wr_ref, wm_ref, a1_ref, a2_ref, ab_ref,
     q_out, k_out, v_out, r_out, la_out, qm_out) = refs
    hb = h.astype(BF16)
    low = _dot(hb, a1_ref[...]).astype(BF16)
    for hd in range(GLA_HEADS):
        q_out[hd] = _dot(hb, wq_ref[hd])
        k_out[hd] = _dot(hb, wk_ref[hd])
        v_out[hd] = _dot(hb, wv_ref[hd])
        r_out[hd] = _dot(hb, wr_ref[hd])
        x = _dot(low, a2_ref[hd]) + ab_ref[hd]
        log_sigmoid = jnp.minimum(x, 0.0) - jnp.log(1.0 + jnp.exp(-jnp.abs(x)))
        la_out[hd] = log_sigmoid / GLA_TAU
    qm_out[...] = _dot(hb, wm_ref[...])


def _gla_weights(w_in, a1, a2, ab):
    kw, tw, nh = GLA_KW, TOK_WIDTH, GLA_HEADS

    def heads(w, d):
        return w.reshape(w.shape[0], nh, d).transpose(1, 0, 2)

    return [heads(w_in[:, :kw], GLA_DK).astype(BF16), heads(w_in[:, kw:2 * kw], GLA_DK).astype(BF16),
            heads(w_in[:, 2 * kw:2 * kw + tw], GLA_DV).astype(BF16),
            heads(w_in[:, 2 * kw + tw:3 * tw], GLA_DV).astype(BF16), w_in[:, 3 * tw:].astype(BF16),
            a1.astype(BF16), heads(a2, GLA_DK).astype(BF16), ab.reshape(nh, 1, GLA_DK)]


def _gla_proj(h, ws, tm, pending=None):
    nh = GLA_HEADS
    widths = [GLA_DK, GLA_DK, GLA_DV, GLA_DV, GLA_DK]
    if pending is None:
        m = h.shape[0]
        lead_specs, lead_args = [_row_spec(tm, D_MODEL)], [h]
    else:
        x1, gathered, first_row, norm_g = pending
        m = x1.shape[0]
        lead_specs = [_row_spec(tm, D_MODEL), _row_spec(tm, D_MODEL, first_row // tm),
                      _row_spec(tm, D_MODEL, (first_row + m) // tm), _full_spec((1, D_MODEL))]
        lead_args = [x1, gathered, gathered, norm_g.reshape(1, D_MODEL)]
    out_specs = [_head_row_spec(tm, w) for w in widths] + [_row_spec(tm, MEM_WIDTH)]
    out_shape = [jax.ShapeDtypeStruct((nh, m, w), F32) for w in widths] + [jax.ShapeDtypeStruct((m, MEM_WIDTH), F32)]
    if pending is not None:
        out_specs.append(_row_spec(tm, D_MODEL))
        out_shape.append(jax.ShapeDtypeStruct((m, D_MODEL), F32))
    return pl.pallas_call(
        functools.partial(_gla_proj_kernel, fused_combine=pending is not None), grid=(m // tm,),
        in_specs=lead_specs + [_full_spec(w.shape) for w in ws],
        out_specs=out_specs, out_shape=out_shape,
        compiler_params=_cparams("parallel"), name="gla_proj")(*lead_args, *ws)


def _gla_out_norm(o, g):
    return o * lax.rsqrt(jnp.mean(o * o, axis=-1, keepdims=True) + NORM_EPS) * g


def _gla_chunk_kernel(q_ref, k_ref, v_ref, la_ref, ng_ref, o_ref, sfin_ref, st_scr):
    c, dk = GLA_CHUNK, GLA_DK

    @pl.when(pl.program_id(1) == 0)
    def _():
        st_scr[...] = jnp.zeros(st_scr.shape, F32)

    row = lax.broadcasted_iota(jnp.int32, (c, c), 0)
    col = lax.broadcasted_iota(jnp.int32, (c, c), 1)
    tril = (row >= col).astype(F32)
    rr = lax.broadcasted_iota(jnp.int32, (c, dk), 0)
    ones_sum = jnp.ones((dk, LANES), BF16)

    heads = range(GLA_HEADS)
    tril_b = tril.astype(BF16)

    def cumsum_rows(la):
        hi = la.astype(BF16)
        r1 = la - hi.astype(F32)
        mid = r1.astype(BF16)
        lo = (r1 - mid.astype(F32)).astype(BF16)
        return _dot(tril_b, hi) + _dot(tril_b, mid) + _dot(tril_b, lo)

    def tile_roll(x, dlt):
        return pltpu.roll(x.reshape(c // GLA_TILE, GLA_TILE, dk), dlt, axis=1).reshape(c, dk)

    k = [k_ref[hd] for hd in heads]
    vb = [v_ref[hd].astype(BF16) for hd in heads]
    q = [q_ref[hd] * (dk ** -0.5) for hd in heads]
    b = [cumsum_rows(la_ref[hd]) for hd in heads]
    st = [st_scr[hd] for hd in heads]
    inter = [_dot_nt((q[hd] * jnp.exp(b[hd])).astype(BF16), st[hd].astype(BF16)) for hd in heads]

    att = [jnp.zeros((c, c), F32) for hd in heads]
    blk = c // 2
    while blk >= GLA_TILE:
        two = 2 * blk
        upper = (rr & (two - 1)) >= blk
        same_block = (row ^ col) < two
        parts = []
        for hd in heads:
            b_ref_rows = jnp.concatenate(
                [jnp.broadcast_to(b[hd][s0 + blk - 1:s0 + blk, :], (two, dk)) for s0 in range(0, c, two)], axis=0)
            q_l = jnp.where(upper, q[hd] * jnp.exp(jnp.minimum(b[hd] - b_ref_rows, 0.0)), 0.0).astype(BF16)
            k_l = jnp.where(upper, 0.0, k[hd] * jnp.exp(jnp.minimum(b_ref_rows - b[hd], 0.0))).astype(BF16)
            parts.append(_dot_nt(q_l, k_l))
        att = [att[hd] + jnp.where(same_block, parts[hd], 0.0) for hd in heads]
        blk //= 2

    sums = []
    for hd in heads:
        prods = [(q[hd] * k[hd]).astype(BF16)]
        for dlt in range(1, GLA_TILE):
            p = q[hd] * tile_roll(k[hd], dlt) * jnp.exp(jnp.minimum(b[hd] - tile_roll(b[hd], dlt), 0.0))
            prods.append(jnp.where((rr & (GLA_TILE - 1)) >= dlt, p, 0.0).astype(BF16))
        sums.append(_dot(jnp.concatenate(prods, axis=0), ones_sum))
    for dlt in range(GLA_TILE):
        on_diag = col == row - dlt
        att = [att[hd] + jnp.where(on_diag, sums[hd][dlt * c:(dlt + 1) * c, :c], 0.0) for hd in heads]

    o = [inter[hd] + _dot(att[hd].astype(BF16), vb[hd]) for hd in heads]
    for hd in heads:
        o_ref[hd] = _gla_out_norm(o[hd], ng_ref[...])
    for hd in heads:
        b_end = b[hd][c - 1:c, :]
        kd = (k[hd] * jnp.exp(b_end - b[hd])).astype(BF16)
        st_scr[hd] = jnp.exp(b_end) * st[hd] + _dot_tn(vb[hd], kd)

    @pl.when(pl.program_id(1) == pl.num_programs(1) - 1)
    def _():
        sfin_ref[0] = st_scr[...]


def _gla_chunk_scan(q, k, v, la, norm_g, batch, t):
    nh, dk, dv, c = GLA_HEADS, GLA_DK, GLA_DV, GLA_CHUNK
    nc = t // c
    kspec = pl.BlockSpec((nh, c, dk), lambda i, j: (0, i * nc + j, 0))
    vspec = pl.BlockSpec((nh, c, dv), lambda i, j: (0, i * nc + j, 0))
    sspec = pl.BlockSpec((1, nh, dv, dk), lambda i, j: (i, 0, 0, 0))
    return pl.pallas_call(
        _gla_chunk_kernel, grid=(batch, nc),
        in_specs=[kspec, kspec, vspec, kspec, pl.BlockSpec((1, dv), lambda i, j: (0, 0))],
        out_specs=[vspec, sspec],
        out_shape=[jax.ShapeDtypeStruct((nh, batch * t, dv), F32), jax.ShapeDtypeStruct((batch, nh, dv, dk), F32)],
        scratch_shapes=[pltpu.VMEM((nh, dv, dk), F32)],
        compiler_params=_cparams("parallel", "arbitrary"), name="gla_chunk")(q, k, v, la, norm_g.reshape(1, dv))


GLA_STEP_KEY_SPLIT = 2


def _gla_step_kernel(q_ref, k_ref, v_ref, la_ref, ng_ref, s0_ref, o_ref, s_ref):
    rows = s0_ref.shape[1]

    @pl.when(pl.program_id(1) == 0)
    def _():
        o_ref[...] = jnp.zeros(o_ref.shape, F32)

    q = q_ref[0] * (GLA_DK ** -0.5)
    decay = jnp.exp(la_ref[0])
    kb = k_ref[0].astype(BF16).astype(F32)
    vb = v_ref[0].astype(BF16).astype(F32)
    acc = o_ref[0]
    for r in range(rows):
        s_new = decay[r:r + 1, :] * s0_ref[0, r] + kb[r:r + 1, :] * vb
        s_ref[0, r] = s_new
        acc = acc + q[r:r + 1, :] * s_new
    o_ref[0] = acc

    @pl.when(pl.program_id(1) == pl.num_programs(1) - 1)
    def _():
        o_ref[0] = acc * lax.rsqrt(jnp.mean(acc * acc, axis=0, keepdims=True) + NORM_EPS) * ng_ref[...]


def _gla_step(q, k, v, la, norm_g, states, layer):
    nh, dk, b = q.shape
    dv = v.shape[1]
    rows = dk // GLA_STEP_KEY_SPLIT
    assert rows * GLA_STEP_KEY_SPLIT == dk and rows % SUBLANES == 0
    kspec = pl.BlockSpec((1, rows, b), lambda h, r: (h, r, 0))
    vspec = pl.BlockSpec((1, dv, b), lambda h, r: (h, 0, 0))
    return pl.pallas_call(
        _gla_step_kernel, grid=(nh, GLA_STEP_KEY_SPLIT),
        in_specs=[kspec, kspec, vspec, kspec, _full_spec((dv, b)),
                  pl.BlockSpec((None, 1, rows, dv, b), lambda h, r: (layer, h, r, 0, 0))],
        out_specs=[vspec, pl.BlockSpec((1, rows, dv, b), lambda h, r: (h, r, 0, 0))],
        out_shape=[jax.ShapeDtypeStruct((nh, dv, b), F32), jax.ShapeDtypeStruct((nh, dk, dv, b), F32)],
        compiler_params=_cparams("parallel", "arbitrary"), name="gla_step")(
            q, k, v, la, jnp.broadcast_to(norm_g.reshape(dv, 1), (dv, b)), states)


def _mem_attn_kernel(q_ref, k_ref, v_ref, o_ref, *, memory_transposed):
    q = q_ref[0]
    k = k_ref[0].astype(BF16)
    ones = jnp.ones((LANES, N_MEM) if memory_transposed else (N_MEM, LANES), BF16)
    v_ones = jnp.concatenate([v_ref[0].astype(BF16), ones], axis=0 if memory_transposed else 1)
    scores, values = (_dot, _dot_nt) if memory_transposed else (_dot_nt, _dot)
    head_of_lane = lax.broadcasted_iota(jnp.int32, (1, MEM_WIDTH), 1) // MEM_HEAD_DIM
    heads = range(MEM_HEADS)
    mine = [head_of_lane == h for h in heads]
    qh = [jnp.where(mine[h], q, 0.0).astype(BF16) for h in heads]
    s = [scores(qh[h], k) * (MEM_HEAD_DIM ** -0.5) for h in heads]
    e = [jnp.exp(s[h] - jnp.max(s[h], axis=-1, keepdims=True)).astype(BF16) for h in heads]
    ev = [values(e[h], v_ones) for h in heads]
    out = jnp.zeros(q.shape, F32)
    for h in heads:
        inv = 1.0 / ev[h][:, MEM_WIDTH:]
        out = out + jnp.where(mine[h], ev[h][:, :MEM_WIDTH] * jnp.concatenate([inv, inv], axis=1), 0.0)
    o_ref[0] = out


def _mem_attn(q, mem_k, mem_v, layer, tq, memory_transposed=False):
    b, t, w = q.shape
    qspec = pl.BlockSpec((1, tq, w), lambda i, j: (i, j, 0))
    mspec = pl.BlockSpec((None, 1, N_MEM, w), lambda i, j: (layer, i, 0, 0))
    return pl.pallas_call(
        functools.partial(_mem_attn_kernel, memory_transposed=memory_transposed), grid=(b, t // tq),
        in_specs=[qspec, mspec, mspec], out_specs=qspec,
        out_shape=jax.ShapeDtypeStruct((b, t, w), F32),
        compiler_params=_cparams("parallel", "parallel"), name="mem_attn")(q, mem_k, mem_v)


def _out_proj_kernel(*refs, layout, aliased):
    tok_ref, gate_ref, att_ref, x_ref, wo_ref, g_ref, wr_hi_ref, wr_lo_ref, br_ref = refs[:9]
    x1_out, h2_out, logit_out = refs[9 + aliased:]
    x1 = x_ref[...] + _dot(att_ref[...].astype(BF16), wo_ref[TOK_WIDTH:, :])
    if layout == "gla_heads":
        for hd in range(GLA_HEADS):
            gate = gate_ref[hd]
            mixed = (tok_ref[hd] * (gate * _sigmoid(gate))).astype(BF16)
            x1 = x1 + _dot(mixed, wo_ref[hd * GLA_DV:(hd + 1) * GLA_DV, :])
    else:
        if layout == "rwkv_pairs":
            tok = jnp.concatenate([tok_ref[p] for p in range(RWKV_PAIRS)], axis=1)
        else:
            tok = tok_ref[...]
        x1 = x1 + _dot((tok * gate_ref[...]).astype(BF16), wo_ref[:TOK_WIDTH, :])
    x1_out[...] = x1
    h2 = _rms(x1, g_ref[...])
    h2_out[...] = h2
    h_hi = h2.astype(BF16)
    h_lo = (h2 - h_hi.astype(F32)).astype(BF16)
    logit_out[...] = (_dot(h_hi, wr_hi_ref[...]) + _dot(h_lo, wr_hi_ref[...]) + _dot(h_hi, wr_lo_ref[...])
                      + br_ref[...])


def _out_proj(tok, gate, att, x, ws, layout, tm, h2_rows, h2_row_offset, h2_buffer=None):
    m = x.shape[0]
    aliased = h2_buffer is not None
    rows_spec = _row_spec(tm, TOK_WIDTH)
    tok_spec, gate_spec = {
        "rows": (rows_spec, rows_spec),
        "rwkv_pairs": (pl.BlockSpec((RWKV_PAIRS, tm, LANES), lambda i: (0, i, 0)), rows_spec),
        "gla_heads": (_head_row_spec(tm, GLA_DV), _head_row_spec(tm, GLA_DV))}[layout]
    in_specs = ([tok_spec, gate_spec, _row_spec(tm, MEM_WIDTH), _row_spec(tm, D_MODEL)]
                + [_full_spec(w.shape) for w in ws])
    args = [tok, gate, att, x, *ws]
    if aliased:
        in_specs.append(pl.BlockSpec(memory_space=pl.ANY))
        args.append(h2_buffer)
    return pl.pallas_call(
        functools.partial(_out_proj_kernel, layout=layout, aliased=int(aliased)), grid=(m // tm,),
        in_specs=in_specs,
        out_specs=[_row_spec(tm, D_MODEL), _row_spec(tm, D_MODEL, h2_row_offset // tm), _row_spec(tm, ROUTER_LANES)],
        out_shape=[jax.ShapeDtypeStruct((m, D_MODEL), F32), jax.ShapeDtypeStruct((h2_rows, D_MODEL), F32),
                   jax.ShapeDtypeStruct((m, ROUTER_LANES), F32)],
        input_output_aliases={len(args) - 1: 1} if aliased else {},
        compiler_params=_cparams("parallel"), name="out_proj")(*args)


def _moe_kernel(tile_ref, exp_ref, nitem_ref, lo_ref, hi_ref, x_ref, gate_ref, wu_ref, wd_ref, o_ref, wu_scr, wd_scr):
    w = pl.program_id(0)
    prev = jnp.maximum(w - 1, 0)
    e = exp_ref[w]
    valid = w < nitem_ref[0]

    @pl.when(jnp.logical_and(valid, jnp.logical_or(w == 0, e != exp_ref[prev])))
    def _():
        wu_scr[...] = wu_ref[0, 0].astype(BF16)
        wd_scr[...] = wd_ref[0, 0].astype(BF16)

    @pl.when(jnp.logical_or(w == 0, tile_ref[w] != tile_ref[prev]))
    def _():
        o_ref[...] = jnp.zeros(o_ref.shape, F32)

    @pl.when(valid)
    def _():
        gu = _dot(x_ref[...].astype(BF16), wu_scr[...])
        g = gu[:, :EXPERT_FF]
        act = (g * _sigmoid(g) * gu[:, EXPERT_FF:]).astype(BF16)
        gate_col = jnp.concatenate([gate_ref[...], jnp.zeros((LANES - 1, MOE_BLOCK), F32)], axis=0).T[:, :1]
        out = _dot(act, wd_scr[...]) * gate_col
        rows = tile_ref[w] * MOE_BLOCK + lax.broadcasted_iota(jnp.int32, (MOE_BLOCK, 1), 0)
        mine = jnp.logical_and(rows >= lo_ref[e], rows < hi_ref[e])
        o_ref[...] = o_ref[...] + jnp.where(mine, out, 0.0)


def _moe_ffn(xs, row_gate, item_tile, item_expert, n_items, lo, hi, w_up, w_down, layer):
    tm = MOE_BLOCK
    n_work = item_tile.shape[0]
    a = xs.shape[0]
    row_map = lambda w, tile, ex, ni, lo_, hi_: (tile[w], 0)
    exp_map = lambda w, tile, ex, ni, lo_, hi_: (layer, ex[w], 0, 0)
    grid_spec = pltpu.PrefetchScalarGridSpec(
        num_scalar_prefetch=5, grid=(n_work,),
        in_specs=[pl.BlockSpec((tm, D_MODEL), row_map),
                  pl.BlockSpec((None, 1, tm), lambda w, tile, ex, ni, lo_, hi_: (tile[w], 0, 0)),
                  pl.BlockSpec((1, 1, D_MODEL, 2 * EXPERT_FF), exp_map),
                  pl.BlockSpec((1, 1, EXPERT_FF, D_MODEL), exp_map)],
        out_specs=pl.BlockSpec((tm, D_MODEL), row_map),
        scratch_shapes=[pltpu.VMEM((D_MODEL, 2 * EXPERT_FF), BF16), pltpu.VMEM((EXPERT_FF, D_MODEL), BF16)])
    return pl.pallas_call(
        _moe_kernel, grid_spec=grid_spec,
        out_shape=jax.ShapeDtypeStruct((a, D_MODEL), F32),
        compiler_params=_cparams("arbitrary"), name="moe_ffn")(
            item_tile, item_expert, n_items, lo, hi, xs, row_gate.reshape(a // tm, 1, tm), w_up, w_down)


def _route(logits, n_prompt, tm):
    m = logits.shape[0]
    n_sample = m - n_prompt
    a = 2 * m
    gl = logits[:, :N_GROUPS]
    el = logits[:, N_GROUPS:N_GROUPS + N_EXPERTS]
    group = jnp.argmax(gl, -1).astype(jnp.int32)
    p_group = jnp.max(jax.nn.softmax(gl, -1), -1, keepdims=True)
    experts_row = jnp.arange(N_EXPERTS, dtype=jnp.int32)[None, :]
    masked = jnp.where(experts_row // EXPERTS_PER_GROUP == group[:, None], el, -jnp.inf)
    e1 = jnp.argmax(masked, -1).astype(jnp.int32)
    masked2 = jnp.where(experts_row == e1[:, None], -jnp.inf, masked)
    e2 = jnp.argmax(masked2, -1).astype(jnp.int32)
    top_val = jnp.stack([jnp.max(masked, -1), jnp.max(masked2, -1)], axis=-1)
    gate = p_group * jax.nn.softmax(top_val, -1)
    expert = jnp.stack([e1, e2], axis=-1)

    def by_id(t):
        return jnp.concatenate([t[:n_prompt, 0], t[:n_prompt, 1], t[n_prompt:, 0], t[n_prompt:, 1]])

    tok_of_id = jnp.asarray(np.concatenate([np.arange(n_prompt), np.arange(n_prompt),
                                            n_prompt + np.arange(n_sample), n_prompt + np.arange(n_sample)]), jnp.int32)
    flat_e = by_id(expert)
    ids = jnp.arange(a, dtype=jnp.int32)
    _, order, gate_sorted, tok_sorted = lax.sort((flat_e, ids, by_id(gate), tok_of_id), num_keys=1, is_stable=True)
    _, inv = lax.sort((order, ids), num_keys=1)
    experts = jnp.arange(N_EXPERTS, dtype=jnp.int32)
    counts = jnp.sum((flat_e[:, None] == experts[None, :]).astype(jnp.int32), axis=0)
    hi = jnp.cumsum(counts).astype(jnp.int32)
    lo = hi - counts
    n_tiles = a // tm
    first_tile = lo // tm
    tiles_of = jnp.where(counts > 0, (hi - 1) // tm - first_tile + 1, 0)
    item_end = jnp.cumsum(tiles_of).astype(jnp.int32)
    n_items = item_end[-1:]
    n_work = n_tiles + N_EXPERTS - 1
    w = jnp.minimum(jnp.arange(n_work, dtype=jnp.int32), n_items[0] - 1)
    item_expert = jnp.sum((item_end[None, :] <= w[:, None]).astype(jnp.int32), axis=1)
    onehot = (item_expert[:, None] == experts[None, :]).astype(jnp.int32)
    item_tile = jnp.sum(onehot * (first_tile - (item_end - tiles_of))[None, :], axis=1) + w
    return tok_sorted, gate_sorted, inv, item_tile.astype(jnp.int32), item_expert, n_items, lo, hi


def _combine_kernel(x_ref, r0_ref, r1_ref, g_ref, *outs):
    x2 = x_ref[...] + (r0_ref[...] + r1_ref[...])
    outs[-1][...] = _rms(x2, g_ref[...])
    if len(outs) == 2:
        outs[0][...] = x2


def _combine(x1, gathered, first_row, g, tm, last_layer):
    m = x1.shape[0]
    spec = _row_spec(tm, D_MODEL)
    n_out = 1 if last_layer else 2
    outs = pl.pallas_call(
        _combine_kernel, grid=(m // tm,),
        in_specs=[spec, _row_spec(tm, D_MODEL, first_row // tm), _row_spec(tm, D_MODEL, (first_row + m) // tm),
                  _full_spec((1, D_MODEL))],
        out_specs=[spec] * n_out,
        out_shape=[jax.ShapeDtypeStruct((m, D_MODEL), F32)] * n_out,
        compiler_params=_cparams("parallel"), name="moe_combine")(x1, gathered, gathered, g.reshape(1, D_MODEL))
    return (None, outs[0]) if last_layer else tuple(outs)


def _to_scan_layout(t2d, b, t):
    return t2d.reshape(b, t, RWKV_HEADS, RWKV_N).transpose(1, 3, 2, 0).reshape(t, RWKV_N, RWKV_HEADS * b)


def _from_scan_layout(y, b, t):
    return y.reshape(t, RWKV_N, RWKV_HEADS, b).transpose(3, 0, 2, 1).reshape(b * t, TOK_WIDTH)


def _scan_param(p, b):
    return jnp.repeat(p.reshape(RWKV_HEADS, RWKV_N).T, b, axis=1)


def _seq_scan_param(p, b):
    return jnp.pad(_scan_param(p, b), ((0, 0), (0, LANES - RWKV_HEADS * b)))


def kernel(x_prompt, x_sample, mem_prompt, state_rwkv_S, state_rwkv_shift, state_gla_S, cache_mem_k, cache_mem_v, norm_mix_g, norm_ffn_g, norm_mem_g, norm_final_g, w_in, w_out, w_mem_kv, rw_mu, rw_w0, rw_w1, rw_w2, rw_a0, rw_a1, rw_a2, rw_g1, rw_g2, rw_k_k, rw_k_a, rw_r_k, rw_ln_g, rw_ln_b, gla_a1, gla_a2, gla_ab, gla_norm_g, router_wg, router_bg, router_we, router_be, exp_w_up, exp_w_down):
    bp, tp, _ = x_prompt.shape
    bs, ts, _ = x_sample.shape
    assert ts == 1 and tp % GLA_CHUNK == 0 and tp % SCAN_TIME_BLOCK == 0
    np_ = bp * tp
    ns = bs * ts
    m = np_ + ns
    assert np_ % LIGHT_BLOCK == 0 and ns % SAMPLE_BLOCK == 0 and (2 * m) % MOE_BLOCK == 0
    depth = w_in.shape[0]
    nh = GLA_HEADS
    tw = TOK_WIDTH
    bf = lambda t_: t_.astype(BF16)

    mem2d = mem_prompt.reshape(bp * N_MEM, D_MODEL)
    mem_kv = [_norm_matmul(mem2d, norm_mem_g[i], w_mem_kv[i], 512) for i in range(depth)]
    pk = jnp.stack([kv[:, :MEM_WIDTH].reshape(bp, N_MEM, MEM_WIDTH) for kv in mem_kv])
    pv = jnp.stack([kv[:, MEM_WIDTH:].reshape(bp, N_MEM, MEM_WIDTH) for kv in mem_kv])
    prompt_mem_k = pk.reshape(depth, bp, N_MEM, MEM_HEADS, MEM_HEAD_DIM)
    prompt_mem_v = pv.reshape(depth, bp, N_MEM, MEM_HEADS, MEM_HEAD_DIM)
    sk = cache_mem_k.reshape(depth, bs, N_MEM, MEM_WIDTH).transpose(0, 1, 3, 2)
    sv = cache_mem_v.reshape(depth, bs, N_MEM, MEM_WIDTH).transpose(0, 1, 3, 2)

    x_p = x_prompt.reshape(np_, D_MODEL)
    x_s = x_sample.reshape(ns, D_MODEL)
    h_p = None
    h_s = _norm(x_s, norm_mix_g[0], SAMPLE_BLOCK)

    w_router = jnp.zeros((depth, D_MODEL, ROUTER_LANES), F32)
    w_router = w_router.at[:, :, :N_GROUPS].set(router_wg).at[:, :, N_GROUPS:N_GROUPS + N_EXPERTS].set(router_we)
    b_router = jnp.zeros((depth, 1, ROUTER_LANES), F32)
    b_router = b_router.at[:, 0, :N_GROUPS].set(router_bg).at[:, 0, N_GROUPS:N_GROUPS + N_EXPERTS].set(router_be)

    p_rw_S, p_rw_shift, p_gla_S, s_rw_S, s_rw_shift, s_gla_S = [], [], [], [], [], []
    pending_p = pending_s = None
    for i in range(depth):
        j = i // 2
        if i % 2 == 0:
            wi = w_in[i]
            ws = [rw_mu[j], bf(wi[:, :tw]), bf(wi[:, tw:2 * tw]), bf(wi[:, 2 * tw:3 * tw]), bf(wi[:, 3 * tw:]),
                  bf(rw_w1[j]), bf(rw_w2[j]), rw_w0[j].reshape(1, tw), bf(rw_a1[j]), bf(rw_a2[j]),
                  rw_a0[j].reshape(1, tw), bf(rw_g1[j]), bf(rw_g2[j])]
            if h_p is None:
                *rkvda_p, gate_p, qm_p = _rwkv_proj(x_p, None, ws, PROJ_BLOCK, tp, norm_g=norm_mix_g[i])
                shift_p = _norm(x_prompt[:, -1, :], norm_mix_g[i], bp)
            else:
                *rkvda_p, gate_p, qm_p = _rwkv_proj(h_p, None, ws, PROJ_BLOCK, tp)
                shift_p = h_p.reshape(bp, tp, D_MODEL)[:, -1]
            *rkvda_s, gate_s, qm_s = _rwkv_proj(h_s, state_rwkv_shift[j], ws, SAMPLE_BLOCK, ts)
            pvec = [rw_k_k[j], rw_k_a[j], rw_r_k[j], rw_ln_g[j], rw_ln_b[j]]
            tok_p, sp = _rwkv_prompt_mixer(*rkvda_p, *[_seq_scan_param(p, bp) for p in pvec], bp, tp, SCAN_TIME_BLOCK)
            assert bs == LANES
            ys, ss = _rwkv_scan(*[_to_scan_layout(t_, bs, ts) for t_ in rkvda_s], [_scan_param(p, bs) for p in pvec],
                                state_rwkv_S[j].transpose(1, 2, 3, 0), 1)
            tok_s = _from_scan_layout(ys, bs, ts)
            p_rw_S.append(sp[:, :, :RWKV_HEADS * bp].reshape(RWKV_N, RWKV_N, RWKV_HEADS, bp).transpose(3, 2, 0, 1))
            s_rw_S.append(ss.transpose(3, 0, 1, 2))
            p_rw_shift.append(shift_p)
            s_rw_shift.append(h_s)
            layout_p, layout_s = "rwkv_pairs", "rows"
        else:
            ws = _gla_weights(w_in[i], gla_a1[j], gla_a2[j], gla_ab[j])
            if pending_p is None:
                q_p, k_p, v_p, gate_p, la_p, qm_p = _gla_proj(h_p, ws, PROJ_BLOCK)
                q_s, k_s, v_s, gate_s, la_s, qm_s = _gla_proj(h_s, ws, SAMPLE_BLOCK)
            else:
                q_p, k_p, v_p, gate_p, la_p, qm_p, x_p = _gla_proj(None, ws, PROJ_BLOCK, pending_p)
                q_s, k_s, v_s, gate_s, la_s, qm_s, x_s = _gla_proj(None, ws, SAMPLE_BLOCK, pending_s)
            tok_p, sp_t = _gla_chunk_scan(q_p, k_p, v_p, la_p, gla_norm_g[j], bp, tp)
            lanes_of = lambda t_: t_.transpose(0, 2, 1)
            os_, ss = _gla_step(lanes_of(q_s), lanes_of(k_s), lanes_of(v_s), lanes_of(la_s), gla_norm_g[j],
                                state_gla_S.transpose(0, 2, 3, 4, 1), j)
            tok_s = os_.transpose(0, 2, 1)
            p_gla_S.append(sp_t.transpose(0, 1, 3, 2))
            s_gla_S.append(ss.transpose(3, 0, 1, 2))
            layout_p = layout_s = "gla_heads"

        att_p = _mem_attn(qm_p.reshape(bp, tp, MEM_WIDTH), pk, pv, i, 512).reshape(np_, MEM_WIDTH)
        att_s = _mem_attn(qm_s.reshape(bs, ts, MEM_WIDTH), sk, sv, i, 1, memory_transposed=True).reshape(ns, MEM_WIDTH)
        wr_hi = bf(w_router[i])
        wr_lo = bf(w_router[i] - wr_hi.astype(F32))
        ws = [bf(w_out[i]), norm_ffn_g[i].reshape(1, D_MODEL), wr_hi, wr_lo, b_router[i]]
        x1_p, h2, logits_p = _out_proj(tok_p, gate_p, att_p, x_p, ws, layout_p, PROJ_BLOCK, m, 0)
        x1_s, h2, logits_s = _out_proj(tok_s, gate_s, att_s, x_s, ws, layout_s, SAMPLE_BLOCK, m, np_, h2_buffer=h2)
        tok_sorted, gate_sorted, inv, item_tile, item_expert, n_items, lo, hi = _route(
            jnp.concatenate([logits_p, logits_s], axis=0), np_, MOE_BLOCK)
        rows = _moe_ffn(h2[tok_sorted], gate_sorted, item_tile, item_expert, n_items, lo, hi,
                        exp_w_up, exp_w_down, i)
        gathered = rows[inv]
        g_next = norm_mix_g[i + 1] if i + 1 < depth else norm_final_g
        if i + 1 < depth and (i + 1) % 2 == 1:
            pending_p = (x1_p, gathered, 0, g_next)
            pending_s = (x1_s, gathered, 2 * np_, g_next)
        else:
            pending_p = pending_s = None
            x_p, h_p = _combine(x1_p, gathered, 0, g_next, LIGHT_BLOCK, i + 1 == depth)
            x_s, h_s = _combine(x1_s, gathered, 2 * np_, g_next, SAMPLE_BLOCK, i + 1 == depth)

    y_prompt = h_p.reshape(bp, tp, D_MODEL)
    y_sample = h_s.reshape(bs, ts, D_MODEL)
    return (y_prompt, y_sample, jnp.stack(p_rw_S), jnp.stack(p_rw_shift), jnp.stack(p_gla_S),
            prompt_mem_k, prompt_mem_v, jnp.stack(s_rw_S), jnp.stack(s_rw_shift), jnp.stack(s_gla_S))
```

```python
import functools

import numpy as np
import jax
import jax.numpy as jnp
from jax import lax
from jax.experimental import pallas as pl
from jax.experimental.pallas import tpu as pltpu

F32 = jnp.float32
BF16 = jnp.bfloat16

D_MODEL = 1024
TOK_WIDTH = 768
MEM_WIDTH = 256
MEM_HEADS = 4
MEM_HEAD_DIM = 64
N_MEM = 256
RWKV_HEADS = 12
RWKV_N = 64
RWKV_PAIRS = RWKV_HEADS // 2
RWKV_GN_EPS = 64e-5
GLA_HEADS = 4
GLA_KW = 384
GLA_DK = 96
GLA_DV = 192
GLA_TAU = 16.0
GLA_CHUNK = 64
GLA_TILE = 8
N_GROUPS = 4
EXPERTS_PER_GROUP = 8
N_EXPERTS = 32
EXPERT_FF = 512
NORM_EPS = 1e-6
ROUTER_LANES = 128
LANES = 128
SUBLANES = 8

PROJ_BLOCK = 512
LIGHT_BLOCK = 512
SAMPLE_BLOCK = 128
MOE_BLOCK = 256
SCAN_TIME_BLOCK = 64
VMEM_LIMIT = 56 * 1024 * 1024


def _cparams(*sem):
    return pltpu.CompilerParams(dimension_semantics=sem, vmem_limit_bytes=VMEM_LIMIT)


def _dot(a, b):
    return jnp.dot(a, b, preferred_element_type=F32)


def _dot_nt(a, b):
    return lax.dot_general(a, b, (((1,), (1,)), ((), ())), preferred_element_type=F32)


def _dot_tn(a, b):
    return lax.dot_general(a, b, (((0,), (0,)), ((), ())), preferred_element_type=F32)


def _rms(x, g):
    return x * lax.rsqrt(jnp.mean(x * x, axis=-1, keepdims=True) + NORM_EPS) * g


def _sigmoid(x):
    return 1.0 / (1.0 + jnp.exp(-x))


def _row_spec(tm, n, offset=0):
    return pl.BlockSpec((tm, n), lambda i: (i + offset, 0))


def _head_row_spec(tm, n):
    return pl.BlockSpec((GLA_HEADS, tm, n), lambda i: (0, i, 0))


def _full_spec(shape):
    nd = len(shape)
    return pl.BlockSpec(shape, lambda *_: (0,) * nd)


def _norm_kernel(x_ref, g_ref, o_ref):
    o_ref[...] = _rms(x_ref[...], g_ref[...])


def _norm(x, g, tm):
    m, d = x.shape
    return pl.pallas_call(
        _norm_kernel, grid=(m // tm,),
        in_specs=[_row_spec(tm, d), _full_spec((1, d))],
        out_specs=_row_spec(tm, d),
        out_shape=jax.ShapeDtypeStruct((m, d), F32),
        compiler_params=_cparams("parallel"), name="rms_norm")(x, g.reshape(1, d))


def _norm_matmul_kernel(x_ref, g_ref, w_ref, o_ref):
    o_ref[...] = _dot(_rms(x_ref[...], g_ref[...]).astype(BF16), w_ref[...])


def _norm_matmul(x, g, w, tm):
    m, d = x.shape
    n = w.shape[1]
    return pl.pallas_call(
        _norm_matmul_kernel, grid=(m // tm,),
        in_specs=[_row_spec(tm, d), _full_spec((1, d)), _full_spec((d, n))],
        out_specs=_row_spec(tm, n),
        out_shape=jax.ShapeDtypeStruct((m, n), F32),
        compiler_params=_cparams("parallel"), name="norm_matmul")(x, g.reshape(1, d), w.astype(BF16))


def _rwkv_proj_kernel(h_ref, hp_ref, *refs, blocks_per_seq, norm_input):
    if norm_input:
        ng_ref, refs = refs[0], refs[1:]
        normed = lambda t_: _rms(t_, ng_ref[...])
    else:
        normed = lambda t_: t_
    (mu_ref, wr_ref, wk_ref, wv_ref, wq_ref, w1_ref, w2_ref, w0_ref, a1_ref, a2_ref, a0_ref, g1_ref, g2_ref,
     r_out, k_out, v_out, d_out, a_out, g_out, q_out) = refs

    def put(out, val):
        if blocks_per_seq:
            for p in range(RWKV_PAIRS):
                out[p] = val[:, p * LANES:(p + 1) * LANES]
        else:
            out[...] = val

    h = normed(h_ref[...])
    if blocks_per_seq:
        seq_start = (pl.program_id(0) % blocks_per_seq) == 0
        before = jnp.where(seq_start, 0.0, normed(hp_ref[SUBLANES - 1:SUBLANES, :]))
        row = lax.broadcasted_iota(jnp.int32, (h.shape[0], 1), 0)
        hp = jnp.where(row == 0, before, pltpu.roll(h, 1, axis=0))
    else:
        hp = hp_ref[...]
    xx = hp - h

    def mix(j):
        return (h + xx * mu_ref[j:j + 1, :]).astype(BF16)

    put(r_out, _dot(mix(0), wr_ref[...]))
    wl = w0_ref[...] + _dot(jnp.tanh(_dot(mix(1), w1_ref[...])).astype(BF16), w2_ref[...])
    put(k_out, _dot(mix(2), wk_ref[...]))
    put(v_out, _dot(mix(3), wv_ref[...]))
    al = a0_ref[...] + _dot(_dot(mix(4), a1_ref[...]).astype(BF16), a2_ref[...])
    g_out[...] = _dot(_sigmoid(_dot(mix(5), g1_ref[...])).astype(BF16), g2_ref[...])
    q_out[...] = _dot(h.astype(BF16), wq_ref[...])
    z = -wl
    softplus = jnp.maximum(z, 0.0) + jnp.log(1.0 + jnp.exp(-jnp.abs(z)))
    put(d_out, jnp.exp(-jnp.exp(-softplus - 0.5)))
    put(a_out, _sigmoid(al))


def _rwkv_proj(h, h_prev, ws, tm, seq_len, norm_g=None):
    m = h.shape[0]
    if norm_g is not None:
        assert h_prev is None
        ws = [norm_g.reshape(1, D_MODEL)] + list(ws)
    tw = TOK_WIDTH
    if h_prev is None:
        assert seq_len % tm == 0 and tm % SUBLANES == 0
        per8 = tm // SUBLANES
        hp_spec = pl.BlockSpec((SUBLANES, D_MODEL), lambda i: (jnp.maximum(i * per8 - 1, 0), 0))
        h_prev, blocks_per_seq = h, seq_len // tm
    else:
        hp_spec, blocks_per_seq = _row_spec(tm, D_MODEL), 0
    if blocks_per_seq:
        scan_spec = pl.BlockSpec((RWKV_PAIRS, tm, LANES), lambda i: (0, i, 0))
        scan_shape = jax.ShapeDtypeStruct((RWKV_PAIRS, m, LANES), F32)
    else:
        scan_spec, scan_shape = _row_spec(tm, tw), jax.ShapeDtypeStruct((m, tw), F32)
    return pl.pallas_call(
        functools.partial(_rwkv_proj_kernel, blocks_per_seq=blocks_per_seq, norm_input=norm_g is not None),
        grid=(m // tm,),
        in_specs=[_row_spec(tm, D_MODEL), hp_spec] + [_full_spec(w.shape) for w in ws],
        out_specs=[scan_spec] * 5 + [_row_spec(tm, tw), _row_spec(tm, MEM_WIDTH)],
        out_shape=[scan_shape] * 5 + [jax.ShapeDtypeStruct((m, tw), F32), jax.ShapeDtypeStruct((m, MEM_WIDTH), F32)],
        compiler_params=_cparams("parallel"), name="rwkv_proj")(h, h_prev, *ws)


def _pad_lanes(x):
    short = LANES - x.shape[-1]
    if short == 0:
        return x
    return jnp.concatenate([x, jnp.zeros(x.shape[:-1] + (short,), x.dtype)], axis=-1)


def _rwkv_scan_kernel(r_ref, k_ref, v_ref, d_ref, a_ref, kkp_ref, kap_ref, rkp_ref, lng_ref, lnb_ref, s0_ref,
                      y_ref, sfin_ref, s_scr, v_scr, yrow_scr, *, tc):
    n = RWKV_N
    nl = r_ref.shape[-1]

    @pl.when(pl.program_id(1) == 0)
    def _():
        s_scr[...] = _pad_lanes(s0_ref[...])

    kkp = _pad_lanes(kkp_ref[...])
    kap = _pad_lanes(kap_ref[...])
    rkp = _pad_lanes(rkp_ref[...])
    lng = _pad_lanes(lng_ref[...])
    lnb = _pad_lanes(lnb_ref[...])

    def step(t, carry):
        r_t = _pad_lanes(r_ref[t])
        k_t = _pad_lanes(k_ref[t])
        v_t = _pad_lanes(v_ref[t])
        d_t = _pad_lanes(d_ref[t])
        a_t = _pad_lanes(a_ref[t])
        v_scr[...] = v_t
        kkr = k_t * kkp
        nrm = jnp.maximum(jnp.sqrt(jnp.sum(kkr * kkr, axis=0, keepdims=True)), 1e-12)
        kk = kkr * (1.0 / nrm)
        k2 = k_t * (1.0 + (a_t - 1.0) * kap)
        nkk = -kk
        b_t = kk * a_t

        def ibody(i, c):
            s_i = s_scr[i]
            sa = jnp.sum(s_i * nkk, axis=0, keepdims=True)
            v_i = v_scr[pl.ds(i, 1), :]
            s_n = s_i * d_t + sa * b_t + v_i * k2
            s_scr[i] = s_n
            yrow_scr[pl.ds(i, 1), :] = jnp.sum(s_n * r_t, axis=0, keepdims=True)
            return c

        lax.fori_loop(0, n, ibody, 0, unroll=8)
        y = yrow_scr[...]
        yc = y - jnp.mean(y, axis=0, keepdims=True)
        var = jnp.mean(yc * yc, axis=0, keepdims=True)
        gn = yc * lax.rsqrt(var + RWKV_GN_EPS) * lng + lnb
        bonus = jnp.sum(r_t * k2 * rkp, axis=0, keepdims=True) * v_t
        y_ref[t] = (gn + bonus)[:, :nl]
        return carry

    lax.fori_loop(0, tc, step, 0)

    @pl.when(pl.program_id(1) == pl.num_programs(1) - 1)
    def _():
        sfin_ref[...] = s_scr[:, :, :nl]


def _rwkv_scan(r, k, v, d, a, params, s0, tc):
    t, n, l = r.shape
    groups, lane_block = s0.shape[0], s0.shape[3]
    assert groups * lane_block == l
    seq = pl.BlockSpec((tc, n, lane_block), lambda li, ti: (ti, 0, li))
    par = pl.BlockSpec((n, lane_block), lambda li, ti: (0, li))
    st = pl.BlockSpec((None, n, n, lane_block), lambda li, ti: (li, 0, 0, 0))
    return pl.pallas_call(
        functools.partial(_rwkv_scan_kernel, tc=tc), grid=(groups, t // tc),
        in_specs=[seq] * 5 + [par] * 5 + [st],
        out_specs=[seq, st],
        out_shape=[jax.ShapeDtypeStruct((t, n, l), F32), jax.ShapeDtypeStruct((groups, n, n, lane_block), F32)],
        scratch_shapes=[pltpu.VMEM((n, n, LANES), F32), pltpu.VMEM((n, LANES), F32), pltpu.VMEM((n, LANES), F32)],
        compiler_params=_cparams("parallel", "arbitrary"), name="rwkv_scan")(r, k, v, d, a, *params, s0)


PREP_TILES = 6
DECAY_WINDOW = 16


def _rwkv_prep_kernel(r_ref, k_ref, v_ref, d_ref, a_ref, kkp_ref, kap_ref, o_ref, *, tc, nb):
    n = RWKV_N
    rows_per_pair = nb * tc
    ins = [ref.reshape(RWKV_PAIRS * rows_per_pair, LANES) for ref in (r_ref, k_ref, v_ref, d_ref, a_ref)]
    zero_rows = jnp.zeros((LANES - RWKV_HEADS * nb, LANES), F32)
    first_half = lax.broadcasted_iota(jnp.int32, (nb, LANES), 1) < n
    kkp = kkp_ref[...]
    kap = kap_ref[...]

    def load_transposed_pair(x2, t):
        pieces = []
        for p in range(RWKV_PAIRS):
            now = x2[pl.ds(p * rows_per_pair + t, nb, stride=tc), :]
            nxt = x2[pl.ds(p * rows_per_pair + t + 1, nb, stride=tc), :]
            pieces += [jnp.where(first_half, now, pltpu.roll(nxt, n, axis=1)),
                       jnp.where(first_half, pltpu.roll(now, n, axis=1), nxt)]
        both = jnp.concatenate(pieces + [zero_rows], axis=0).T
        return both[:n, :], both[n:, :]

    def emit(t, c_prev, r_t, k_t, v_t, d_t, a_t):
        kkr = k_t * kkp
        nrm = jnp.maximum(jnp.sqrt(jnp.sum(kkr * kkr, axis=0, keepdims=True)), 1e-12)
        kk = kkr * (1.0 / nrm)
        k2 = k_t * (1.0 + (a_t - 1.0) * kap)
        c_prev = jnp.where(t % DECAY_WINDOW == 0, 1.0, c_prev)
        c_t = c_prev * d_t
        inv = 1.0 / jnp.maximum(c_t, 1e-30)
        o_ref[t, 0] = -kk * c_prev
        o_ref[t, 1] = c_t
        o_ref[t, 2] = kk * a_t * inv
        o_ref[t, 3] = k2 * inv
        o_ref[t, 4] = r_t * c_t
        o_ref[t, 5] = v_t
        return c_t

    def two_steps(u, c):
        t = 2 * u
        tiles = [load_transposed_pair(x2, t) for x2 in ins]
        c = emit(t, c, *[tile[0] for tile in tiles])
        return emit(t + 1, c, *[tile[1] for tile in tiles])

    lax.fori_loop(0, tc // 2, two_steps, jnp.ones((n, LANES), F32), unroll=4)


def _rwkv_state_scan_kernel(x_ref, rkp_ref, lng_ref, lnb_ref, y_ref, sfin_ref, s_scr, sa_scr, yrow_scr, *, tc):
    n = RWKV_N
    groups = n // SUBLANES

    @pl.when(pl.program_id(0) == 0)
    def _():
        s_scr[...] = jnp.zeros(s_scr.shape, F32)

    rkp = rkp_ref[...]
    lng = lng_ref[...]
    lnb = lnb_ref[...]
    sub = lax.broadcasted_iota(jnp.int32, (SUBLANES, LANES), 0)
    low4 = sub < 4
    low2 = (sub & 3) < 2
    low1 = (sub & 1) == 0

    def fold(x, y, dist, low):
        if dist == 4:
            return jnp.where(low, x, y) + pltpu.roll(jnp.where(low, y, x), 4, axis=0)
        return (jnp.where(low, x, pltpu.roll(y, dist, axis=0))
                + jnp.where(low, pltpu.roll(x, SUBLANES - dist, axis=0), y))

    def sublane_sums(ps):
        z = [fold(ps[0], ps[4], 4, low4), fold(ps[2], ps[6], 4, low4),
             fold(ps[1], ps[5], 4, low4), fold(ps[3], ps[7], 4, low4)]
        return fold(fold(z[0], z[1], 2, low2), fold(z[2], z[3], 2, low2), 1, low1)

    def tile_sum(x):
        acc = x[0:SUBLANES]
        for u in range(1, groups):
            acc = acc + x[u * SUBLANES:(u + 1) * SUBLANES]
        return acc

    def step(t, carry):
        nkk = x_ref[t, 0]
        b_t = x_ref[t, 2]
        k2 = x_ref[t, 3]
        r_t = x_ref[t, 4]
        for g in range(groups):
            sa_scr[g * SUBLANES:(g + 1) * SUBLANES, :] = sublane_sums(
                [tile_sum(s_scr[g * SUBLANES + u] * nkk) for u in range(SUBLANES)])
        for g in range(groups):
            ps = []
            for u in range(SUBLANES):
                i = g * SUBLANES + u
                s_n = s_scr[i] + sa_scr[pl.ds(i, 1), :] * b_t + x_ref[t, 5, pl.ds(i, 1), :] * k2
                s_scr[i] = s_n
                ps.append(tile_sum(s_n * r_t))
            yrow_scr[g * SUBLANES:(g + 1) * SUBLANES, :] = sublane_sums(ps)
        y = yrow_scr[...]
        yc = y - jnp.mean(y, axis=0, keepdims=True)
        var = jnp.mean(yc * yc, axis=0, keepdims=True)
        bonus = jnp.sum(r_t * k2 * rkp, axis=0, keepdims=True) * x_ref[t, 5]
        y_ref[t] = yc * lax.rsqrt(var + RWKV_GN_EPS) * lng + lnb + bonus

        @pl.when(t % DECAY_WINDOW == DECAY_WINDOW - 1)
        def _():
            c_t = x_ref[t, 1]
            for i in range(n):
                s_scr[i] = s_scr[i] * c_t

        return carry

    lax.fori_loop(0, tc, step, 0)

    @pl.when(pl.program_id(0) == pl.num_programs(0) - 1)
    def _():
        sfin_ref[...] = s_scr[...]


def _rwkv_unprep_kernel(y_ref, o_ref, *, tc, nb):
    n = RWKV_N
    rows_per_pair = nb * tc
    o2 = o_ref.reshape(RWKV_PAIRS * rows_per_pair, LANES)
    first_half = lax.broadcasted_iota(jnp.int32, (nb, LANES), 1) < n

    def two_steps(u, carry):
        t = 2 * u
        w = jnp.concatenate([y_ref[t], y_ref[t + 1]], axis=0).T
        for p in range(RWKV_PAIRS):
            even = w[(2 * p) * nb:(2 * p + 1) * nb, :]
            odd = w[(2 * p + 1) * nb:(2 * p + 2) * nb, :]
            o2[pl.ds(p * rows_per_pair + t, nb, stride=tc), :] = jnp.where(
                first_half, even, pltpu.roll(odd, n, axis=1))
            o2[pl.ds(p * rows_per_pair + t + 1, nb, stride=tc), :] = jnp.where(
                first_half, pltpu.roll(even, n, axis=1), odd)
        return carry

    lax.fori_loop(0, tc // 2, two_steps, 0, unroll=2)


def _rwkv_prompt_mixer(r, k, v, d, a, kkp, kap, rkp, lng, lnb, nb, t, tc):
    n = RWKV_N
    assert tc % SUBLANES == 0 and tc % 4 == 0 and t % tc == 0 and RWKV_HEADS * nb <= LANES
    assert tc % DECAY_WINDOW == 0
    grid = (t // tc,)
    tok = pl.BlockSpec((RWKV_PAIRS, nb, tc, LANES), lambda ti: (0, 0, ti, 0))
    par = pl.BlockSpec((n, LANES), lambda ti: (0, 0))
    tiles = pl.BlockSpec((tc, PREP_TILES, n, LANES), lambda ti: (ti, 0, 0, 0))
    ytile = pl.BlockSpec((tc, n, LANES), lambda ti: (ti, 0, 0))
    prepared = pl.pallas_call(
        functools.partial(_rwkv_prep_kernel, tc=tc, nb=nb), grid=grid,
        in_specs=[tok] * 5 + [par] * 2, out_specs=tiles,
        out_shape=jax.ShapeDtypeStruct((t, PREP_TILES, n, LANES), F32),
        compiler_params=_cparams("parallel"), name="rwkv_prep")(
            *[x.reshape(RWKV_PAIRS, nb, t, LANES) for x in (r, k, v, d, a)], kkp, kap)
    y, s = pl.pallas_call(
        functools.partial(_rwkv_state_scan_kernel, tc=tc), grid=grid,
        in_specs=[tiles, par, par, par],
        out_specs=[ytile, pl.BlockSpec((n, n, LANES), lambda ti: (0, 0, 0))],
        out_shape=[jax.ShapeDtypeStruct((t, n, LANES), F32), jax.ShapeDtypeStruct((n, n, LANES), F32)],
        scratch_shapes=[pltpu.VMEM((n, n, LANES), F32), pltpu.VMEM((n, LANES), F32), pltpu.VMEM((n, LANES), F32)],
        compiler_params=_cparams("arbitrary"), name="rwkv_state_scan")(prepared, rkp, lng, lnb)
    tok_out = pl.pallas_call(
        functools.partial(_rwkv_unprep_kernel, tc=tc, nb=nb), grid=grid,
        in_specs=[ytile], out_specs=tok,
        out_shape=jax.ShapeDtypeStruct((RWKV_PAIRS, nb, t, LANES), F32),
        compiler_params=_cparams("parallel"), name="rwkv_unprep")(y)
    return tok_out.reshape(RWKV_PAIRS, nb * t, LANES), s


def _gla_proj_kernel(*refs, fused_combine):
    if fused_combine:
        x_ref, r0_ref, r1_ref, g_ref = refs[:4]
        refs, x_out = refs[4:-1], refs[-1]
        x2 = x_ref[...] + (r0_ref[...] + r1_ref[...])
        x_out[...] = x2
        h = _rms(x2, g_ref[...])
    else:
        h, refs = refs[0][...], refs[1:]
    (wq_ref, wk_ref, wv_ref, wr_ref, wm_ref, a1_ref, a2_ref, ab_ref,
     q_out, k_out, v_out, r_out, la_out, qm_out) = refs
    hb = h.astype(BF16)
    low = _dot(hb, a1_ref[...]).astype(BF16)
    for hd in range(GLA_HEADS):
        q_out[hd] = _dot(hb, wq_ref[hd])
        k_out[hd] = _dot(hb, wk_ref[hd])
        v_out[hd] = _dot(hb, wv_ref[hd])
        r_out[hd] = _dot(hb, wr_ref[hd])
        x = _dot(low, a2_ref[hd]) + ab_ref[hd]
        log_sigmoid = jnp.minimum(x, 0.0) - jnp.log(1.0 + jnp.exp(-jnp.abs(x)))
        la_out[hd] = log_sigmoid / GLA_TAU
    qm_out[...] = _dot(hb, wm_ref[...])


def _gla_weights(w_in, a1, a2, ab):
    kw, tw, nh = GLA_KW, TOK_WIDTH, GLA_HEADS

    def heads(w, d):
        return w.reshape(w.shape[0], nh, d).transpose(1, 0, 2)

    return [heads(w_in[:, :kw], GLA_DK).astype(BF16), heads(w_in[:, kw:2 * kw], GLA_DK).astype(BF16),
            heads(w_in[:, 2 * kw:2 * kw + tw], GLA_DV).astype(BF16),
            heads(w_in[:, 2 * kw + tw:3 * tw], GLA_DV).astype(BF16), w_in[:, 3 * tw:].astype(BF16),
            a1.astype(BF16), heads(a2, GLA_DK).astype(BF16), ab.reshape(nh, 1, GLA_DK)]


def _gla_proj(h, ws, tm, pending=None):
    nh = GLA_HEADS
    widths = [GLA_DK, GLA_DK, GLA_DV, GLA_DV, GLA_DK]
    if pending is None:
        m = h.shape[0]
        lead_specs, lead_args = [_row_spec(tm, D_MODEL)], [h]
    else:
        x1, gathered, first_row, norm_g = pending
        m = x1.shape[0]
        lead_specs = [_row_spec(tm, D_MODEL), _row_spec(tm, D_MODEL, first_row // tm),
                      _row_spec(tm, D_MODEL, (first_row + m) // tm), _full_spec((1, D_MODEL))]
        lead_args = [x1, gathered, gathered, norm_g.reshape(1, D_MODEL)]
    out_specs = [_head_row_spec(tm, w) for w in widths] + [_row_spec(tm, MEM_WIDTH)]
    out_shape = [jax.ShapeDtypeStruct((nh, m, w), F32) for w in widths] + [jax.ShapeDtypeStruct((m, MEM_WIDTH), F32)]
    if pending is not None:
        out_specs.append(_row_spec(tm, D_MODEL))
        out_shape.append(jax.ShapeDtypeStruct((m, D_MODEL), F32))
    return pl.pallas_call(
        functools.partial(_gla_proj_kernel, fused_combine=pending is not None), grid=(m // tm,),
        in_specs=lead_specs + [_full_spec(w.shape) for w in ws],
        out_specs=out_specs, out_shape=out_shape,
        compiler_params=_cparams("parallel"), name="gla_proj")(*lead_args, *ws)


def _gla_out_norm(o, g):
    return o * lax.rsqrt(jnp.mean(o * o, axis=-1, keepdims=True) + NORM_EPS) * g


def _gla_chunk_kernel(q_ref, k_ref, v_ref, la_ref, ng_ref, o_ref, sfin_ref, st_scr):
    c, dk = GLA_CHUNK, GLA_DK

    @pl.when(pl.program_id(1) == 0)
    def _():
        st_scr[...] = jnp.zeros(st_scr.shape, F32)

    row = lax.broadcasted_iota(jnp.int32, (c, c), 0)
    col = lax.broadcasted_iota(jnp.int32, (c, c), 1)
    tril = (row >= col).astype(F32)
    rr = lax.broadcasted_iota(jnp.int32, (c, dk), 0)
    ones_sum = jnp.ones((dk, LANES), BF16)

    heads = range(GLA_HEADS)
    tril_b = tril.astype(BF16)

    def cumsum_rows(la):
        hi = la.astype(BF16)
        r1 = la - hi.astype(F32)
        mid = r1.astype(BF16)
        lo = (r1 - mid.astype(F32)).astype(BF16)
        return _dot(tril_b, hi) + _dot(tril_b, mid) + _dot(tril_b, lo)

    def tile_roll(x, dlt):
        return pltpu.roll(x.reshape(c // GLA_TILE, GLA_TILE, dk), dlt, axis=1).reshape(c, dk)

    k = [k_ref[hd] for hd in heads]
    vb = [v_ref[hd].astype(BF16) for hd in heads]
    q = [q_ref[hd] * (dk ** -0.5) for hd in heads]
    b = [cumsum_rows(la_ref[hd]) for hd in heads]
    st = [st_scr[hd] for hd in heads]
    inter = [_dot_nt((q[hd] * jnp.exp(b[hd])).astype(BF16), st[hd].astype(BF16)) for hd in heads]

    att = [jnp.zeros((c, c), F32) for hd in heads]
    blk = c // 2
    while blk >= GLA_TILE:
        two = 2 * blk
        upper = (rr & (two - 1)) >= blk
        same_block = (row ^ col) < two
        parts = []
        for hd in heads:
            b_ref_rows = jnp.concatenate(
                [jnp.broadcast_to(b[hd][s0 + blk - 1:s0 + blk, :], (two, dk)) for s0 in range(0, c, two)], axis=0)
            q_l = jnp.where(upper, q[hd] * jnp.exp(jnp.minimum(b[hd] - b_ref_rows, 0.0)), 0.0).astype(BF16)
            k_l = jnp.where(upper, 0.0, k[hd] * jnp.exp(jnp.minimum(b_ref_rows - b[hd], 0.0))).astype(BF16)
            parts.append(_dot_nt(q_l, k_l))
        att = [att[hd] + jnp.where(same_block, parts[hd], 0.0) for hd in heads]
        blk //= 2

    sums = []
    for hd in heads:
        prods = [(q[hd] * k[hd]).astype(BF16)]
        for dlt in range(1, GLA_TILE):
            p = q[hd] * tile_roll(k[hd], dlt) * jnp.exp(jnp.minimum(b[hd] - tile_roll(b[hd], dlt), 0.0))
            prods.append(jnp.where((rr & (GLA_TILE - 1)) >= dlt, p, 0.0).astype(BF16))
        sums.append(_dot(jnp.concatenate(prods, axis=0), ones_sum))
    for dlt in range(GLA_TILE):
        on_diag = col == row - dlt
        att = [att[hd] + jnp.where(on_diag, sums[hd][dlt * c:(dlt + 1) * c, :c], 0.0) for hd in heads]

    o = [inter[hd] + _dot(att[hd].astype(BF16), vb[hd]) for hd in heads]
    for hd in heads:
        o_ref[hd] = _gla_out_norm(o[hd], ng_ref[...])
    for hd in heads:
        b_end = b[hd][c - 1:c, :]
        kd = (k[hd] * jnp.exp(b_end - b[hd])).astype(BF16)
        st_scr[hd] = jnp.exp(b_end) * st[hd] + _dot_tn(vb[hd], kd)

    @pl.when(pl.program_id(1) == pl.num_programs(1) - 1)
    def _():
        sfin_ref[0] = st_scr[...]


def _gla_chunk_scan(q, k, v, la, norm_g, batch, t):
    nh, dk, dv, c = GLA_HEADS, GLA_DK, GLA_DV, GLA_CHUNK
    nc = t // c
    kspec = pl.BlockSpec((nh, c, dk), lambda i, j: (0, i * nc + j, 0))
    vspec = pl.BlockSpec((nh, c, dv), lambda i, j: (0, i * nc + j, 0))
    sspec = pl.BlockSpec((1, nh, dv, dk), lambda i, j: (i, 0, 0, 0))
    return pl.pallas_call(
        _gla_chunk_kernel, grid=(batch, nc),
        in_specs=[kspec, kspec, vspec, kspec, pl.BlockSpec((1, dv), lambda i, j: (0, 0))],
        out_specs=[vspec, sspec],
        out_shape=[jax.ShapeDtypeStruct((nh, batch * t, dv), F32), jax.ShapeDtypeStruct((batch, nh, dv, dk), F32)],
        scratch_shapes=[pltpu.VMEM((nh, dv, dk), F32)],
        compiler_params=_cparams("parallel", "arbitrary"), name="gla_chunk")(q, k, v, la, norm_g.reshape(1, dv))


GLA_STEP_KEY_SPLIT = 2


def _gla_step_kernel(q_ref, k_ref, v_ref, la_ref, ng_ref, s0_ref, o_ref, s_ref):
    rows = s0_ref.shape[1]

    @pl.when(pl.program_id(1) == 0)
    def _():
        o_ref[...] = jnp.zeros(o_ref.shape, F32)

    q = q_ref[0] * (GLA_DK ** -0.5)
    decay = jnp.exp(la_ref[0])
    kb = k_ref[0].astype(BF16).astype(F32)
    vb = v_ref[0].astype(BF16).astype(F32)
    acc = o_ref[0]
    for r in range(rows):
        s_new = decay[r:r + 1, :] * s0_ref[0, r] + kb[r:r + 1, :] * vb
        s_ref[0, r] = s_new
        acc = acc + q[r:r + 1, :] * s_new
    o_ref[0] = acc

    @pl.when(pl.program_id(1) == pl.num_programs(1) - 1)
    def _():
        o_ref[0] = acc * lax.rsqrt(jnp.mean(acc * acc, axis=0, keepdims=True) + NORM_EPS) * ng_ref[...]


def _gla_step(q, k, v, la, norm_g, states, layer):
    nh, dk, b = q.shape
    dv = v.shape[1]
    rows = dk // GLA_STEP_KEY_SPLIT
    assert rows * GLA_STEP_KEY_SPLIT == dk and rows % SUBLANES == 0
    kspec = pl.BlockSpec((1, rows, b), lambda h, r: (h, r, 0))
    vspec = pl.BlockSpec((1, dv, b), lambda h, r: (h, 0, 0))
    return pl.pallas_call(
        _gla_step_kernel, grid=(nh, GLA_STEP_KEY_SPLIT),
        in_specs=[kspec, kspec, vspec, kspec, _full_spec((dv, b)),
                  pl.BlockSpec((None, 1, rows, dv, b), lambda h, r: (layer, h, r, 0, 0))],
        out_specs=[vspec, pl.BlockSpec((1, rows, dv, b), lambda h, r: (h, r, 0, 0))],
        out_shape=[jax.ShapeDtypeStruct((nh, dv, b), F32), jax.ShapeDtypeStruct((nh, dk, dv, b), F32)],
        compiler_params=_cparams("parallel", "arbitrary"), name="gla_step")(
            q, k, v, la, jnp.broadcast_to(norm_g.reshape(dv, 1), (dv, b)), states)


def _mem_attn_kernel(q_ref, k_ref, v_ref, o_ref, *, memory_transposed):
    q = q_ref[0]
    k = k_ref[0].astype(BF16)
    ones = jnp.ones((LANES, N_MEM) if memory_transposed else (N_MEM, LANES), BF16)
    v_ones = jnp.concatenate([v_ref[0].astype(BF16), ones], axis=0 if memory_transposed else 1)
    scores, values = (_dot, _dot_nt) if memory_transposed else (_dot_nt, _dot)
    head_of_lane = lax.broadcasted_iota(jnp.int32, (1, MEM_WIDTH), 1) // MEM_HEAD_DIM
    heads = range(MEM_HEADS)
    mine = [head_of_lane == h for h in heads]
    qh = [jnp.where(mine[h], q, 0.0).astype(BF16) for h in heads]
    s = [scores(qh[h], k) * (MEM_HEAD_DIM ** -0.5) for h in heads]
    e = [jnp.exp(s[h] - jnp.max(s[h], axis=-1, keepdims=True)).astype(BF16) for h in heads]
    ev = [values(e[h], v_ones) for h in heads]
    out = jnp.zeros(q.shape, F32)
    for h in heads:
        inv = 1.0 / ev[h][:, MEM_WIDTH:]
        out = out + jnp.where(mine[h], ev[h][:, :MEM_WIDTH] * jnp.concatenate([inv, inv], axis=1), 0.0)
    o_ref[0] = out


def _mem_attn(q, mem_k, mem_v, layer, tq, memory_transposed=False):
    b, t, w = q.shape
    qspec = pl.BlockSpec((1, tq, w), lambda i, j: (i, j, 0))
    mspec = pl.BlockSpec((None, 1, N_MEM, w), lambda i, j: (layer, i, 0, 0))
    return pl.pallas_call(
        functools.partial(_mem_attn_kernel, memory_transposed=memory_transposed), grid=(b, t // tq),
        in_specs=[qspec, mspec, mspec], out_specs=qspec,
        out_shape=jax.ShapeDtypeStruct((b, t, w), F32),
        compiler_params=_cparams("parallel", "parallel"), name="mem_attn")(q, mem_k, mem_v)


def _out_proj_kernel(*refs, layout, aliased):
    tok_ref, gate_ref, att_ref, x_ref, wo_ref, g_ref, wr_hi_ref, wr_lo_ref, br_ref = refs[:9]
    x1_out, h2_out, logit_out = refs[9 + aliased:]
    x1 = x_ref[...] + _dot(att_ref[...].astype(BF16), wo_ref[TOK_WIDTH:, :])
    if layout == "gla_heads":
        for hd in range(GLA_HEADS):
            gate = gate_ref[hd]
            mixed = (tok_ref[hd] * (gate * _sigmoid(gate))).astype(BF16)
            x1 = x1 + _dot(mixed, wo_ref[hd * GLA_DV:(hd + 1) * GLA_DV, :])
    else:
        if layout == "rwkv_pairs":
            tok = jnp.concatenate([tok_ref[p] for p in range(RWKV_PAIRS)], axis=1)
        else:
            tok = tok_ref[...]
        x1 = x1 + _dot((tok * gate_ref[...]).astype(BF16), wo_ref[:TOK_WIDTH, :])
    x1_out[...] = x1
    h2 = _rms(x1, g_ref[...])
    h2_out[...] = h2
    h_hi = h2.astype(BF16)
    h_lo = (h2 - h_hi.astype(F32)).astype(BF16)
    logit_out[...] = (_dot(h_hi, wr_hi_ref[...]) + _dot(h_lo, wr_hi_ref[...]) + _dot(h_hi, wr_lo_ref[...])
                      + br_ref[...])


def _out_proj(tok, gate, att, x, ws, layout, tm, h2_rows, h2_row_offset, h2_buffer=None):
    m = x.shape[0]
    aliased = h2_buffer is not None
    rows_spec = _row_spec(tm, TOK_WIDTH)
    tok_spec, gate_spec = {
        "rows": (rows_spec, rows_spec),
        "rwkv_pairs": (pl.BlockSpec((RWKV_PAIRS, tm, LANES), lambda i: (0, i, 0)), rows_spec),
        "gla_heads": (_head_row_spec(tm, GLA_DV), _head_row_spec(tm, GLA_DV))}[layout]
    in_specs = ([tok_spec, gate_spec, _row_spec(tm, MEM_WIDTH), _row_spec(tm, D_MODEL)]
                + [_full_spec(w.shape) for w in ws])
    args = [tok, gate, att, x, *ws]
    if aliased:
        in_specs.append(pl.BlockSpec(memory_space=pl.ANY))
        args.append(h2_buffer)
    return pl.pallas_call(
        functools.partial(_out_proj_kernel, layout=layout, aliased=int(aliased)), grid=(m // tm,),
        in_specs=in_specs,
        out_specs=[_row_spec(tm, D_MODEL), _row_spec(tm, D_MODEL, h2_row_offset // tm), _row_spec(tm, ROUTER_LANES)],
        out_shape=[jax.ShapeDtypeStruct((m, D_MODEL), F32), jax.ShapeDtypeStruct((h2_rows, D_MODEL), F32),
                   jax.ShapeDtypeStruct((m, ROUTER_LANES), F32)],
        input_output_aliases={len(args) - 1: 1} if aliased else {},
        compiler_params=_cparams("parallel"), name="out_proj")(*args)


def _moe_kernel(tile_ref, exp_ref, nitem_ref, lo_ref, hi_ref, x_ref, gate_ref, wu_ref, wd_ref, o_ref, wu_scr, wd_scr):
    w = pl.program_id(0)
    prev = jnp.maximum(w - 1, 0)
    e = exp_ref[w]
    valid = w < nitem_ref[0]

    @pl.when(jnp.logical_and(valid, jnp.logical_or(w == 0, e != exp_ref[prev])))
    def _():
        wu_scr[...] = wu_ref[0, 0].astype(BF16)
        wd_scr[...] = wd_ref[0, 0].astype(BF16)

    @pl.when(jnp.logical_or(w == 0, tile_ref[w] != tile_ref[prev]))
    def _():
        o_ref[...] = jnp.zeros(o_ref.shape, F32)

    @pl.when(valid)
    def _():
        gu = _dot(x_ref[...].astype(BF16), wu_scr[...])
        g = gu[:, :EXPERT_FF]
        act = (g * _sigmoid(g) * gu[:, EXPERT_FF:]).astype(BF16)
        gate_col = jnp.concatenate([gate_ref[...], jnp.zeros((LANES - 1, MOE_BLOCK), F32)], axis=0).T[:, :1]
        out = _dot(act, wd_scr[...]) * gate_col
        rows = tile_ref[w] * MOE_BLOCK + lax.broadcasted_iota(jnp.int32, (MOE_BLOCK, 1), 0)
        mine = jnp.logical_and(rows >= lo_ref[e], rows < hi_ref[e])
        o_ref[...] = o_ref[...] + jnp.where(mine, out, 0.0)


def _moe_ffn(xs, row_gate, item_tile, item_expert, n_items, lo, hi, w_up, w_down, layer):
    tm = MOE_BLOCK
    n_work = item_tile.shape[0]
    a = xs.shape[0]
    row_map = lambda w, tile, ex, ni, lo_, hi_: (tile[w], 0)
    exp_map = lambda w, tile, ex, ni, lo_, hi_: (layer, ex[w], 0, 0)
    grid_spec = pltpu.PrefetchScalarGridSpec(
        num_scalar_prefetch=5, grid=(n_work,),
        in_specs=[pl.BlockSpec((tm, D_MODEL), row_map),
                  pl.BlockSpec((None, 1, tm), lambda w, tile, ex, ni, lo_, hi_: (tile[w], 0, 0)),
                  pl.BlockSpec((1, 1, D_MODEL, 2 * EXPERT_FF), exp_map),
                  pl.BlockSpec((1, 1, EXPERT_FF, D_MODEL), exp_map)],
        out_specs=pl.BlockSpec((tm, D_MODEL), row_map),
        scratch_shapes=[pltpu.VMEM((D_MODEL, 2 * EXPERT_FF), BF16), pltpu.VMEM((EXPERT_FF, D_MODEL), BF16)])
    return pl.pallas_call(
        _moe_kernel, grid_spec=grid_spec,
        out_shape=jax.ShapeDtypeStruct((a, D_MODEL), F32),
        compiler_params=_cparams("arbitrary"), name="moe_ffn")(
            item_tile, item_expert, n_items, lo, hi, xs, row_gate.reshape(a // tm, 1, tm), w_up, w_down)


def _route(logits, n_prompt, tm):
    m = logits.shape[0]
    n_sample = m - n_prompt
    a = 2 * m
    gl = logits[:, :N_GROUPS]
    el = logits[:, N_GROUPS:N_GROUPS + N_EXPERTS]
    group = jnp.argmax(gl, -1).astype(jnp.int32)
    p_group = jnp.max(jax.nn.softmax(gl, -1), -1, keepdims=True)
    experts_row = jnp.arange(N_EXPERTS, dtype=jnp.int32)[None, :]
    masked = jnp.where(experts_row // EXPERTS_PER_GROUP == group[:, None], el, -jnp.inf)
    e1 = jnp.argmax(masked, -1).astype(jnp.int32)
    masked2 = jnp.where(experts_row == e1[:, None], -jnp.inf, masked)
    e2 = jnp.argmax(masked2, -1).astype(jnp.int32)
    top_val = jnp.stack([jnp.max(masked, -1), jnp.max(masked2, -1)], axis=-1)
    gate = p_group * jax.nn.softmax(top_val, -1)
    expert = jnp.stack([e1, e2], axis=-1)

    def by_id(t):
        return jnp.concatenate([t[:n_prompt, 0], t[:n_prompt, 1], t[n_prompt:, 0], t[n_prompt:, 1]])

    tok_of_id = jnp.asarray(np.concatenate([np.arange(n_prompt), np.arange(n_prompt),
                                            n_prompt + np.arange(n_sample), n_prompt + np.arange(n_sample)]), jnp.int32)
    flat_e = by_id(expert)
    ids = jnp.arange(a, dtype=jnp.int32)
    _, order, gate_sorted, tok_sorted = lax.sort((flat_e, ids, by_id(gate), tok_of_id), num_keys=1, is_stable=True)
    _, inv = lax.sort((order, ids), num_keys=1)
    experts = jnp.arange(N_EXPERTS, dtype=jnp.int32)
    counts = jnp.sum((flat_e[:, None] == experts[None, :]).astype(jnp.int32), axis=0)
    hi = jnp.cumsum(counts).astype(jnp.int32)
    lo = hi - counts
    n_tiles = a // tm
    first_tile = lo // tm
    tiles_of = jnp.where(counts > 0, (hi - 1) // tm - first_tile + 1, 0)
    item_end = jnp.cumsum(tiles_of).astype(jnp.int32)
    n_items = item_end[-1:]
    n_work = n_tiles + N_EXPERTS - 1
    w = jnp.minimum(jnp.arange(n_work, dtype=jnp.int32), n_items[0] - 1)
    item_expert = jnp.sum((item_end[None, :] <= w[:, None]).astype(jnp.int32), axis=1)
    onehot = (item_expert[:, None] == experts[None, :]).astype(jnp.int32)
    item_tile = jnp.sum(onehot * (first_tile - (item_end - tiles_of))[None, :], axis=1) + w
    return tok_sorted, gate_sorted, inv, item_tile.astype(jnp.int32), item_expert, n_items, lo, hi


def _combine_kernel(x_ref, r0_ref, r1_ref, g_ref, *outs):
    x2 = x_ref[...] + (r0_ref[...] + r1_ref[...])
    outs[-1][...] = _rms(x2, g_ref[...])
    if len(outs) == 2:
        outs[0][...] = x2


def _combine(x1, gathered, first_row, g, tm, last_layer):
    m = x1.shape[0]
    spec = _row_spec(tm, D_MODEL)
    n_out = 1 if last_layer else 2
    outs = pl.pallas_call(
        _combine_kernel, grid=(m // tm,),
        in_specs=[spec, _row_spec(tm, D_MODEL, first_row // tm), _row_spec(tm, D_MODEL, (first_row + m) // tm),
                  _full_spec((1, D_MODEL))],
        out_specs=[spec] * n_out,
        out_shape=[jax.ShapeDtypeStruct((m, D_MODEL), F32)] * n_out,
        compiler_params=_cparams("parallel"), name="moe_combine")(x1, gathered, gathered, g.reshape(1, D_MODEL))
    return (None, outs[0]) if last_layer else tuple(outs)


def _to_scan_layout(t2d, b, t):
    return t2d.reshape(b, t, RWKV_HEADS, RWKV_N).transpose(1, 3, 2, 0).reshape(t, RWKV_N, RWKV_HEADS * b)


def _from_scan_layout(y, b, t):
    return y.reshape(t, RWKV_N, RWKV_HEADS, b).transpose(3, 0, 2, 1).reshape(b * t, TOK_WIDTH)


def _scan_param(p, b):
    return jnp.repeat(p.reshape(RWKV_HEADS, RWKV_N).T, b, axis=1)


def _seq_scan_param(p, b):
    return jnp.pad(_scan_param(p, b), ((0, 0), (0, LANES - RWKV_HEADS * b)))


def kernel(x_prompt, x_sample, mem_prompt, state_rwkv_S, state_rwkv_shift, state_gla_S, cache_mem_k, cache_mem_v, norm_mix_g, norm_ffn_g, norm_mem_g, norm_final_g, w_in, w_out, w_mem_kv, rw_mu, rw_w0, rw_w1, rw_w2, rw_a0, rw_a1, rw_a2, rw_g1, rw_g2, rw_k_k, rw_k_a, rw_r_k, rw_ln_g, rw_ln_b, gla_a1, gla_a2, gla_ab, gla_norm_g, router_wg, router_bg, router_we, router_be, exp_w_up, exp_w_down):
    bp, tp, _ = x_prompt.shape
    bs, ts, _ = x_sample.shape
    assert ts == 1 and tp % GLA_CHUNK == 0 and tp % SCAN_TIME_BLOCK == 0
    np_ = bp * tp
    ns = bs * ts
    m = np_ + ns
    assert np_ % LIGHT_BLOCK == 0 and ns % SAMPLE_BLOCK == 0 and (2 * m) % MOE_BLOCK == 0
    depth = w_in.shape[0]
    nh = GLA_HEADS
    tw = TOK_WIDTH
    bf = lambda t_: t_.astype(BF16)

    mem2d = mem_prompt.reshape(bp * N_MEM, D_MODEL)
    mem_kv = [_norm_matmul(mem2d, norm_mem_g[i], w_mem_kv[i], 512) for i in range(depth)]
    pk = jnp.stack([kv[:, :MEM_WIDTH].reshape(bp, N_MEM, MEM_WIDTH) for kv in mem_kv])
    pv = jnp.stack([kv[:, MEM_WIDTH:].reshape(bp, N_MEM, MEM_WIDTH) for kv in mem_kv])
    prompt_mem_k = pk.reshape(depth, bp, N_MEM, MEM_HEADS, MEM_HEAD_DIM)
    prompt_mem_v = pv.reshape(depth, bp, N_MEM, MEM_HEADS, MEM_HEAD_DIM)
    sk = cache_mem_k.reshape(depth, bs, N_MEM, MEM_WIDTH).transpose(0, 1, 3, 2)
    sv = cache_mem_v.reshape(depth, bs, N_MEM, MEM_WIDTH).transpose(0, 1, 3, 2)

    x_p = x_prompt.reshape(np_, D_MODEL)
    x_s = x_sample.reshape(ns, D_MODEL)
    h_p = None
    h_s = _norm(x_s, norm_mix_g[0], SAMPLE_BLOCK)

    w_router = jnp.zeros((depth, D_MODEL, ROUTER_LANES), F32)
    w_router = w_router.at[:, :, :N_GROUPS].set(router_wg).at[:, :, N_GROUPS:N_GROUPS + N_EXPERTS].set(router_we)
    b_router = jnp.zeros((depth, 1, ROUTER_LANES), F32)
    b_router = b_router.at[:, 0, :N_GROUPS].set(router_bg).at[:, 0, N_GROUPS:N_GROUPS + N_EXPERTS].set(router_be)

    p_rw_S, p_rw_shift, p_gla_S, s_rw_S, s_rw_shift, s_gla_S = [], [], [], [], [], []
    pending_p = pending_s = None
    for i in range(depth):
        j = i // 2
        if i % 2 == 0:
            wi = w_in[i]
            ws = [rw_mu[j], bf(wi[:, :tw]), bf(wi[:, tw:2 * tw]), bf(wi[:, 2 * tw:3 * tw]), bf(wi[:, 3 * tw:]),
                  bf(rw_w1[j]), bf(rw_w2[j]), rw_w0[j].reshape(1, tw), bf(rw_a1[j]), bf(rw_a2[j]),
                  rw_a0[j].reshape(1, tw), bf(rw_g1[j]), bf(rw_g2[j])]
            if h_p is None:
                *rkvda_p, gate_p, qm_p = _rwkv_proj(x_p, None, ws, PROJ_BLOCK, tp, norm_g=norm_mix_g[i])
                shift_p = _norm(x_prompt[:, -1, :], norm_mix_g[i], bp)
            else:
                *rkvda_p, gate_p, qm_p = _rwkv_proj(h_p, None, ws, PROJ_BLOCK, tp)
                shift_p = h_p.reshape(bp, tp, D_MODEL)[:, -1]
            *rkvda_s, gate_s, qm_s = _rwkv_proj(h_s, state_rwkv_shift[j], ws, SAMPLE_BLOCK, ts)
            pvec = [rw_k_k[j], rw_k_a[j], rw_r_k[j], rw_ln_g[j], rw_ln_b[j]]
            tok_p, sp = _rwkv_prompt_mixer(*rkvda_p, *[_seq_scan_param(p, bp) for p in pvec], bp, tp, SCAN_TIME_BLOCK)
            assert bs == LANES
            ys, ss = _rwkv_scan(*[_to_scan_layout(t_, bs, ts) for t_ in rkvda_s], [_scan_param(p, bs) for p in pvec],
                                state_rwkv_S[j].transpose(1, 2, 3, 0), 1)
            tok_s = _from_scan_layout(ys, bs, ts)
            p_rw_S.append(sp[:, :, :RWKV_HEADS * bp].reshape(RWKV_N, RWKV_N, RWKV_HEADS, bp).transpose(3, 2, 0, 1))
            s_rw_S.append(ss.transpose(3, 0, 1, 2))
            p_rw_shift.append(shift_p)
            s_rw_shift.append(h_s)
            layout_p, layout_s = "rwkv_pairs", "rows"
        else:
            ws = _gla_weights(w_in[i], gla_a1[j], gla_a2[j], gla_ab[j])
            if pending_p is None:
                q_p, k_p, v_p, gate_p, la_p, qm_p = _gla_proj(h_p, ws, PROJ_BLOCK)
                q_s, k_s, v_s, gate_s, la_s, qm_s = _gla_proj(h_s, ws, SAMPLE_BLOCK)
            else:
                q_p, k_p, v_p, gate_p, la_p, qm_p, x_p = _gla_proj(None, ws, PROJ_BLOCK, pending_p)
                q_s, k_s, v_s, gate_s, la_s, qm_s, x_s = _gla_proj(None, ws, SAMPLE_BLOCK, pending_s)
            tok_p, sp_t = _gla_chunk_scan(q_p, k_p, v_p, la_p, gla_norm_g[j], bp, tp)
            lanes_of = lambda t_: t_.transpose(0, 2, 1)
            os_, ss = _gla_step(lanes_of(q_s), lanes_of(k_s), lanes_of(v_s), lanes_of(la_s), gla_norm_g[j],
                                state_gla_S.transpose(0, 2, 3, 4, 1), j)
            tok_s = os_.transpose(0, 2, 1)
            p_gla_S.append(sp_t.transpose(0, 1, 3, 2))
            s_gla_S.append(ss.transpose(3, 0, 1, 2))
            layout_p = layout_s = "gla_heads"

        att_p = _mem_attn(qm_p.reshape(bp, tp, MEM_WIDTH), pk, pv, i, 512).reshape(np_, MEM_WIDTH)
        att_s = _mem_attn(qm_s.reshape(bs, ts, MEM_WIDTH), sk, sv, i, 1, memory_transposed=True).reshape(ns, MEM_WIDTH)
        wr_hi = bf(w_router[i])
        wr_lo = bf(w_router[i] - wr_hi.astype(F32))
        ws = [bf(w_out[i]), norm_ffn_g[i].reshape(1, D_MODEL), wr_hi, wr_lo, b_router[i]]
        x1_p, h2, logits_p = _out_proj(tok_p, gate_p, att_p, x_p, ws, layout_p, PROJ_BLOCK, m, 0)
        x1_s, h2, logits_s = _out_proj(tok_s, gate_s, att_s, x_s, ws, layout_s, SAMPLE_BLOCK, m, np_, h2_buffer=h2)
        tok_sorted, gate_sorted, inv, item_tile, item_expert, n_items, lo, hi = _route(
            jnp.concatenate([logits_p, logits_s], axis=0), np_, MOE_BLOCK)
        rows = _moe_ffn(h2[tok_sorted], gate_sorted, item_tile, item_expert, n_items, lo, hi,
                        exp_w_up, exp_w_down, i)
        gathered = rows[inv]
        g_next = norm_mix_g[i + 1] if i + 1 < depth else norm_final_g
        if i + 1 < depth and (i + 1) % 2 == 1:
            pending_p = (x1_p, gathered, 0, g_next)
            pending_s = (x1_s, gathered, 2 * np_, g_next)
        else:
            pending_p = pending_s = None
            x_p, h_p = _combine(x1_p, gathered, 0, g_next, LIGHT_BLOCK, i + 1 == depth)
            x_s, h_s = _combine(x1_s, gathered, 2 * np_, g_next, SAMPLE_BLOCK, i + 1 == depth)

    y_prompt = h_p.reshape(bp, tp, D_MODEL)
    y_sample = h_s.reshape(bs, ts, D_MODEL)
    return (y_prompt, y_sample, jnp.stack(p_rw_S), jnp.stack(p_rw_shift), jnp.stack(p_gla_S),
            prompt_mem_k, prompt_mem_v, jnp.stack(s_rw_S), jnp.stack(s_rw_shift), jnp.stack(s_gla_S))
```
